```python
import math
import jax, jax.numpy as jnp
from jax import lax
import numpy as np

D_MODEL = 2048
BATCH = 8
SEQ = 8192
DEPTH = 1

D_MIX = D_MODEL
CONV_CH = D_MIX // 2
CONV_GROUPS = 16
CONV_WIDTH = 31
N_HEADS = 16
HEAD_DIM = 64
ATT_WIDTH = N_HEADS * HEAD_DIM
D_IN = 2 * CONV_CH + 3 * ATT_WIDTH
D_FF = 5632
DILATED_PATTERNS = ((128, 1), (512, 4), (2048, 16))
ALIBI_MAX_BIAS = 8.0
EPS = 1e-6

kernel_name = "hymba_conformer_dilated_alibi_layer"


def rms_norm(x, g):
    xf = x.astype(jnp.float32)
    y = xf * lax.rsqrt(jnp.mean(xf * xf, axis=-1, keepdims=True) + EPS)
    return (y * g.astype(jnp.float32)).astype(x.dtype)


def layer_norm(x, g, b):
    xf = x.astype(jnp.float32)
    mu = jnp.mean(xf, axis=-1, keepdims=True)
    var = jnp.mean(jnp.square(xf - mu), axis=-1, keepdims=True)
    y = (xf - mu) * lax.rsqrt(var + EPS)
    return (y * g.astype(jnp.float32) + b.astype(jnp.float32)).astype(x.dtype)


def swiglu_ffn(x, w_gate, w_up, w_down):
    return (jax.nn.silu(x @ w_gate) * (x @ w_up)) @ w_down


def alibi_slopes(n_heads):
    return 2.0 ** (-ALIBI_MAX_BIAS * jnp.arange(1, n_heads + 1, dtype=jnp.float32) / n_heads)


def conformer_conv(u, w_dw, b_dw, ln_g, ln_b):
    a, gate = jnp.split(u, 2, axis=-1)
    v = a * jax.nn.sigmoid(gate)
    v_pad = jnp.pad(v, ((0, 0), (CONV_WIDTH - 1, 0), (0, 0)))
    y = lax.conv_general_dilated(
        v_pad, w_dw[:, None, :].astype(v.dtype), window_strides=(1,), padding="VALID",
        dimension_numbers=("NWC", "WIO", "NWC"), feature_group_count=CONV_CH)
    y = y + b_dw.astype(y.dtype)
    y = layer_norm(y, ln_g, ln_b)
    return jax.nn.silu(y)


def dilated_branch(q, k, v, window, dilation, slopes):
    b, h, s, hd = q.shape
    w = window // dilation
    chunk = dilation * w
    s_pad = -(-s // chunk) * chunk
    n_sub = s_pad // dilation
    nb = n_sub // w

    def to_blocks(t):
        t = jnp.pad(t, ((0, 0), (0, 0), (0, s_pad - s), (0, 0)))
        t = t.reshape(b, h, n_sub, dilation, hd).transpose(0, 1, 3, 2, 4)
        return t.reshape(b, h, dilation, nb, w, hd)

    def with_prev(t):
        prev = jnp.pad(t[:, :, :, :-1], ((0, 0), (0, 0), (0, 0), (1, 0), (0, 0), (0, 0)))
        return jnp.concatenate([prev, t], axis=4)

    qb = to_blocks(q)
    kk = with_prev(to_blocks(k))
    vv = with_prev(to_blocks(v))

    scores = jnp.einsum("bhrnqd,bhrnkd->bhrnqk", qb, kk).astype(jnp.float32)
    scores = scores * (1.0 / math.sqrt(hd))
    qi = jnp.arange(w)[:, None]
    kj = jnp.arange(2 * w)[None, :]
    sub_dist = w + qi - kj
    key_sub_pos = (jnp.arange(nb)[:, None, None] - 1) * w + kj[None]
    valid = (sub_dist >= 0)[None] & (sub_dist <= w)[None] & (key_sub_pos >= 0)
    token_dist = (dilation * sub_dist).astype(jnp.float32)
    bias = -slopes[:, None, None] * token_dist[None]
    scores = scores + bias[None, :, None, None]
    scores = jnp.where(valid[None, None, None], scores, -jnp.inf)

    lse = jax.nn.logsumexp(scores, axis=-1)
    p = jnp.exp(scores - lse[..., None])
    out = jnp.einsum("bhrnqk,bhrnkd->bhrnqd", p, vv.astype(jnp.float32))

    out = out.reshape(b, h, dilation, n_sub, hd).transpose(0, 1, 3, 2, 4).reshape(b, h, s_pad, hd)
    lse = lse.reshape(b, h, dilation, n_sub).transpose(0, 1, 3, 2).reshape(b, h, s_pad)
    return out[:, :, :s], lse[:, :, :s]


def dilated_attention(zq, zk, zv, q_norm_g, k_norm_g):
    b, s, _ = zq.shape
    heads = lambda t: t.reshape(b, s, N_HEADS, HEAD_DIM).transpose(0, 2, 1, 3)
    q = rms_norm(heads(zq), q_norm_g)
    k = rms_norm(heads(zk), k_norm_g)
    v = heads(zv)
    slopes = alibi_slopes(N_HEADS)
    outs, lses = [], []
    for window, dilation in DILATED_PATTERNS:
        o, l = dilated_branch(q, k, v, window, dilation, slopes)
        outs.append(o)
        lses.append(l)
    wts = jax.nn.softmax(jnp.stack(lses, axis=0), axis=0)
    out = jnp.sum(wts[..., None] * jnp.stack(outs, axis=0), axis=0)
    return out.transpose(0, 2, 1, 3).reshape(b, s, ATT_WIDTH).astype(zq.dtype)


def _fwd_setup_inputs(seed: int = 0) -> dict:
    key = jax.random.key(seed)
    ks = jax.random.split(key, 20)
    f32 = jnp.float32
    nrm = lambda k, shape, fan_in: jax.random.normal(k, shape, f32) * (fan_in ** -0.5)
    gain = lambda k, shape: 1.0 + 0.02 * jax.random.normal(k, shape, f32)
    small = lambda k, shape: 0.02 * jax.random.normal(k, shape, f32)
    L = DEPTH
    return {
        "x": jax.random.normal(ks[0], (BATCH, SEQ, D_MODEL), f32),
        "ffn1_norm_g": gain(ks[1], (L, D_MODEL)),
        "ffn1_w_gate": nrm(ks[2], (L, D_MODEL, D_FF), D_MODEL),
        "ffn1_w_up": nrm(ks[3], (L, D_MODEL, D_FF), D_MODEL),
        "ffn1_w_down": nrm(ks[4], (L, D_FF, D_MODEL), D_FF),
        "mix_norm_g": gain(ks[5], (L, D_MODEL)),
        "w_in": nrm(ks[6], (L, D_MODEL, D_IN), D_MODEL),
        "conv_w_dw": nrm(ks[7], (L, CONV_WIDTH, CONV_CH), CONV_WIDTH),
        "conv_b_dw": small(ks[8], (L, CONV_CH)),
        "conv_ln_g": gain(ks[9], (L, CONV_CH)),
        "conv_ln_b": small(ks[10], (L, CONV_CH)),
        "q_norm_g": gain(ks[11], (L, HEAD_DIM)),
        "k_norm_g": gain(ks[12], (L, HEAD_DIM)),
        "w_out": nrm(ks[13], (L, D_MIX, D_MODEL), D_MIX),
        "ffn2_norm_g": gain(ks[14], (L, D_MODEL)),
        "ffn2_w_gate": nrm(ks[15], (L, D_MODEL, D_FF), D_MODEL),
        "ffn2_w_up": nrm(ks[16], (L, D_MODEL, D_FF), D_MODEL),
        "ffn2_w_down": nrm(ks[17], (L, D_FF, D_MODEL), D_FF),
    }


def _fwd_reference(x, ffn1_norm_g, ffn1_w_gate, ffn1_w_up, ffn1_w_down, mix_norm_g, w_in,
              conv_w_dw, conv_b_dw, conv_ln_g, conv_ln_b, q_norm_g, k_norm_g, w_out,
              ffn2_norm_g, ffn2_w_gate, ffn2_w_up, ffn2_w_down):
    for l in range(DEPTH):
        x = x + 0.5 * swiglu_ffn(rms_norm(x, ffn1_norm_g[l]), ffn1_w_gate[l], ffn1_w_up[l], ffn1_w_down[l])
        h = rms_norm(x, mix_norm_g[l])
        z = h @ w_in[l]
        c0 = 2 * CONV_CH
        z_conv = z[..., :c0]
        z_q = z[..., c0:c0 + ATT_WIDTH]
        z_k = z[..., c0 + ATT_WIDTH:c0 + 2 * ATT_WIDTH]
        z_v = z[..., c0 + 2 * ATT_WIDTH:]
        y_conv = conformer_conv(z_conv, conv_w_dw[l], conv_b_dw[l], conv_ln_g[l], conv_ln_b[l])
        y_att = dilated_attention(z_q, z_k, z_v, q_norm_g[l], k_norm_g[l])
        y = jnp.concatenate([y_conv, y_att], axis=-1) @ w_out[l]
        x = x + y
        x = x + 0.5 * swiglu_ffn(rms_norm(x, ffn2_norm_g[l]), ffn2_w_gate[l], ffn2_w_up[l], ffn2_w_down[l])
    return x


import jax as _jax
import jax.numpy as _jnp

TWIN_FORMAT = 'train_step'
FWD_PARAMS = ['x', 'ffn1_norm_g', 'ffn1_w_gate', 'ffn1_w_up', 'ffn1_w_down', 'mix_norm_g', 'w_in', 'conv_w_dw', 'conv_b_dw', 'conv_ln_g', 'conv_ln_b', 'q_norm_g', 'k_norm_g', 'w_out', 'ffn2_norm_g', 'ffn2_w_gate', 'ffn2_w_up', 'ffn2_w_down']
TWIN_WEIGHTS = ['ffn1_norm_g', 'ffn1_w_gate', 'ffn1_w_up', 'ffn1_w_down', 'mix_norm_g', 'w_in', 'conv_w_dw', 'conv_b_dw', 'conv_ln_g', 'conv_ln_b', 'q_norm_g', 'k_norm_g', 'w_out', 'ffn2_norm_g', 'ffn2_w_gate', 'ffn2_w_up', 'ffn2_w_down']
TWIN_DIFF_INPUT = 'x'
TWIN_INPUTS = ['x', 'ffn1_norm_g', 'ffn1_w_gate', 'ffn1_w_up', 'ffn1_w_down', 'mix_norm_g', 'w_in', 'conv_w_dw', 'conv_b_dw', 'conv_ln_g', 'conv_ln_b', 'q_norm_g', 'k_norm_g', 'w_out', 'ffn2_norm_g', 'ffn2_w_gate', 'ffn2_w_up', 'ffn2_w_down', 'loss_target', 'm_ffn1_norm_g', 'm_ffn1_w_gate', 'm_ffn1_w_up', 'm_ffn1_w_down', 'm_mix_norm_g', 'm_w_in', 'm_conv_w_dw', 'm_conv_b_dw', 'm_conv_ln_g', 'm_conv_ln_b', 'm_q_norm_g', 'm_k_norm_g', 'm_w_out', 'm_ffn2_norm_g', 'm_ffn2_w_gate', 'm_ffn2_w_up', 'm_ffn2_w_down', 'v_ffn1_norm_g', 'v_ffn1_w_gate', 'v_ffn1_w_up', 'v_ffn1_w_down', 'v_mix_norm_g', 'v_w_in', 'v_conv_w_dw', 'v_conv_b_dw', 'v_conv_ln_g', 'v_conv_ln_b', 'v_q_norm_g', 'v_k_norm_g', 'v_w_out', 'v_ffn2_norm_g', 'v_ffn2_w_gate', 'v_ffn2_w_up', 'v_ffn2_w_down']
TWIN_OUTPUTS = ['loss', 'grad_x', 'grad_ffn1_norm_g', 'grad_ffn1_w_gate', 'grad_ffn1_w_up', 'grad_ffn1_w_down', 'grad_mix_norm_g', 'grad_w_in', 'grad_conv_w_dw', 'grad_conv_b_dw', 'grad_conv_ln_g', 'grad_conv_ln_b', 'grad_q_norm_g', 'grad_k_norm_g', 'grad_w_out', 'grad_ffn2_norm_g', 'grad_ffn2_w_gate', 'grad_ffn2_w_up', 'grad_ffn2_w_down', 'delta_ffn1_norm_g', 'delta_ffn1_w_gate', 'delta_ffn1_w_up', 'delta_ffn1_w_down', 'delta_mix_norm_g', 'delta_w_in', 'delta_conv_w_dw', 'delta_conv_b_dw', 'delta_conv_ln_g', 'delta_conv_ln_b', 'delta_q_norm_g', 'delta_k_norm_g', 'delta_w_out', 'delta_ffn2_norm_g', 'delta_ffn2_w_gate', 'delta_ffn2_w_up', 'delta_ffn2_w_down', 'new_m_ffn1_norm_g', 'new_m_ffn1_w_gate', 'new_m_ffn1_w_up', 'new_m_ffn1_w_down', 'new_m_mix_norm_g', 'new_m_w_in', 'new_m_conv_w_dw', 'new_m_conv_b_dw', 'new_m_conv_ln_g', 'new_m_conv_ln_b', 'new_m_q_norm_g', 'new_m_k_norm_g', 'new_m_w_out', 'new_m_ffn2_norm_g', 'new_m_ffn2_w_gate', 'new_m_ffn2_w_up', 'new_m_ffn2_w_down', 'new_v_ffn1_norm_g', 'new_v_ffn1_w_gate', 'new_v_ffn1_w_up', 'new_v_ffn1_w_down', 'new_v_mix_norm_g', 'new_v_w_in', 'new_v_conv_w_dw', 'new_v_conv_b_dw', 'new_v_conv_ln_g', 'new_v_conv_ln_b', 'new_v_q_norm_g', 'new_v_k_norm_g', 'new_v_w_out', 'new_v_ffn2_norm_g', 'new_v_ffn2_w_gate', 'new_v_ffn2_w_up', 'new_v_ffn2_w_down']
TWIN_LEAF_KINDS = {'loss': 'loss', 'grad_x': 'grad_x', 'grad_ffn1_norm_g': 'grad_w', 'grad_ffn1_w_gate': 'grad_w', 'grad_ffn1_w_up': 'grad_w', 'grad_ffn1_w_down': 'grad_w', 'grad_mix_norm_g': 'grad_w', 'grad_w_in': 'grad_w', 'grad_conv_w_dw': 'grad_w', 'grad_conv_b_dw': 'grad_w', 'grad_conv_ln_g': 'grad_w', 'grad_conv_ln_b': 'grad_w', 'grad_q_norm_g': 'grad_w', 'grad_k_norm_g': 'grad_w', 'grad_w_out': 'grad_w', 'grad_ffn2_norm_g': 'grad_w', 'grad_ffn2_w_gate': 'grad_w', 'grad_ffn2_w_up': 'grad_w', 'grad_ffn2_w_down': 'grad_w', 'delta_ffn1_norm_g': 'delta_w', 'delta_ffn1_w_gate': 'delta_w', 'delta_ffn1_w_up': 'delta_w', 'delta_ffn1_w_down': 'delta_w', 'delta_mix_norm_g': 'delta_w', 'delta_w_in': 'delta_w', 'delta_conv_w_dw': 'delta_w', 'delta_conv_b_dw': 'delta_w', 'delta_conv_ln_g': 'delta_w', 'delta_conv_ln_b': 'delta_w', 'delta_q_norm_g': 'delta_w', 'delta_k_norm_g': 'delta_w', 'delta_w_out': 'delta_w', 'delta_ffn2_norm_g': 'delta_w', 'delta_ffn2_w_gate': 'delta_w', 'delta_ffn2_w_up': 'delta_w', 'delta_ffn2_w_down': 'delta_w', 'new_m_ffn1_norm_g': 'new_m', 'new_m_ffn1_w_gate': 'new_m', 'new_m_ffn1_w_up': 'new_m', 'new_m_ffn1_w_down': 'new_m', 'new_m_mix_norm_g': 'new_m', 'new_m_w_in': 'new_m', 'new_m_conv_w_dw': 'new_m', 'new_m_conv_b_dw': 'new_m', 'new_m_conv_ln_g': 'new_m', 'new_m_conv_ln_b': 'new_m', 'new_m_q_norm_g': 'new_m', 'new_m_k_norm_g': 'new_m', 'new_m_w_out': 'new_m', 'new_m_ffn2_norm_g': 'new_m', 'new_m_ffn2_w_gate': 'new_m', 'new_m_ffn2_w_up': 'new_m', 'new_m_ffn2_w_down': 'new_m', 'new_v_ffn1_norm_g': 'new_v', 'new_v_ffn1_w_gate': 'new_v', 'new_v_ffn1_w_up': 'new_v', 'new_v_ffn1_w_down': 'new_v', 'new_v_mix_norm_g': 'new_v', 'new_v_w_in': 'new_v', 'new_v_conv_w_dw': 'new_v', 'new_v_conv_b_dw': 'new_v', 'new_v_conv_ln_g': 'new_v', 'new_v_conv_ln_b': 'new_v', 'new_v_q_norm_g': 'new_v', 'new_v_k_norm_g': 'new_v', 'new_v_w_out': 'new_v', 'new_v_ffn2_norm_g': 'new_v', 'new_v_ffn2_w_gate': 'new_v', 'new_v_ffn2_w_up': 'new_v', 'new_v_ffn2_w_down': 'new_v'}


def _forward(args):
    return _fwd_reference(*[args[k] for k in FWD_PARAMS])


def _output_shape():
    def fwd():
        inp = _fwd_setup_inputs(0)
        return _fwd_reference(*[inp[k] for k in FWD_PARAMS])
    out = _jax.eval_shape(fwd)
    return out.shape, out.dtype

N_MICROBATCH = 1
ADAM_LR = 0.001
ADAM_B1 = 0.9
ADAM_B2 = 0.999
ADAM_EPS = 1e-08
ADAM_WD = 0.01
ADAM_STEP = 10
PER_EXAMPLE_BATCH_AXIS = {'x': 0, 'loss_target': 0}
SHARED_INPUTS = []
_WEIGHT_DTYPES = {'ffn1_norm_g': _jnp.float32, 'ffn1_w_gate': _jnp.float32, 'ffn1_w_up': _jnp.float32, 'ffn1_w_down': _jnp.float32, 'mix_norm_g': _jnp.float32, 'w_in': _jnp.float32, 'conv_w_dw': _jnp.float32, 'conv_b_dw': _jnp.float32, 'conv_ln_g': _jnp.float32, 'conv_ln_b': _jnp.float32, 'q_norm_g': _jnp.float32, 'k_norm_g': _jnp.float32, 'w_out': _jnp.float32, 'ffn2_norm_g': _jnp.float32, 'ffn2_w_gate': _jnp.float32, 'ffn2_w_up': _jnp.float32, 'ffn2_w_down': _jnp.float32}
MOMENT_SCALE = {'ffn1_norm_g': 6.188923e+00, 'ffn1_w_gate': 5.092557e-02, 'ffn1_w_up': 5.545086e-02, 'ffn1_w_down': 9.077145e-02, 'mix_norm_g': 2.075388e+00, 'w_in': 1.350969e-01, 'conv_w_dw': 2.632020e-01, 'conv_b_dw': 4.947201e+00, 'conv_ln_g': 1.409889e+01, 'conv_ln_b': 9.317559e+00, 'q_norm_g': 1.261814e+01, 'k_norm_g': 1.256109e+01, 'w_out': 6.418544e-01, 'ffn2_norm_g': 6.183227e+00, 'ffn2_w_gate': 8.145655e-02, 'ffn2_w_up': 5.879245e-02, 'ffn2_w_down': 9.387282e-02}


def _to_microbatches(a, axis):
    t = _jnp.moveaxis(a, axis, 0)
    t = t.reshape((N_MICROBATCH, t.shape[0] // N_MICROBATCH) + t.shape[1:])
    return _jnp.moveaxis(t, 1, axis + 1)


def setup_inputs(seed: int = 0) -> dict:
    inp = _fwd_setup_inputs(seed)
    key = _jax.random.fold_in(_jax.random.key(seed), 7919)
    shape, _ = _output_shape()
    out = dict(inp)
    out["loss_target"] = _jax.random.normal(_jax.random.fold_in(key, 0), shape, _jnp.float32)
    for i, name in enumerate(TWIN_WEIGHTS):
        w = inp[name].astype(_jnp.float32)
        if MOMENT_SCALE is None:
            s = _jnp.sqrt(_jnp.mean(_jnp.square(w)) + 1e-30)
        else:
            s = MOMENT_SCALE[name]
        km, kv = _jax.random.split(_jax.random.fold_in(key, i + 1))
        out[name] = w
        out["m_" + name] = s * _jax.random.normal(km, w.shape, _jnp.float32)
        out["v_" + name] = (s * s) * _jax.random.uniform(kv, w.shape, _jnp.float32, 0.5, 1.5)
    if N_MICROBATCH > 1:
        for name, axis in PER_EXAMPLE_BATCH_AXIS.items():
            out[name] = _to_microbatches(out[name], axis)
    return {'x': out['x'], 'ffn1_norm_g': out['ffn1_norm_g'], 'ffn1_w_gate': out['ffn1_w_gate'], 'ffn1_w_up': out['ffn1_w_up'], 'ffn1_w_down': out['ffn1_w_down'], 'mix_norm_g': out['mix_norm_g'], 'w_in': out['w_in'], 'conv_w_dw': out['conv_w_dw'], 'conv_b_dw': out['conv_b_dw'], 'conv_ln_g': out['conv_ln_g'], 'conv_ln_b': out['conv_ln_b'], 'q_norm_g': out['q_norm_g'], 'k_norm_g': out['k_norm_g'], 'w_out': out['w_out'], 'ffn2_norm_g': out['ffn2_norm_g'], 'ffn2_w_gate': out['ffn2_w_gate'], 'ffn2_w_up': out['ffn2_w_up'], 'ffn2_w_down': out['ffn2_w_down'], 'loss_target': out['loss_target'], 'm_ffn1_norm_g': out['m_ffn1_norm_g'], 'm_ffn1_w_gate': out['m_ffn1_w_gate'], 'm_ffn1_w_up': out['m_ffn1_w_up'], 'm_ffn1_w_down': out['m_ffn1_w_down'], 'm_mix_norm_g': out['m_mix_norm_g'], 'm_w_in': out['m_w_in'], 'm_conv_w_dw': out['m_conv_w_dw'], 'm_conv_b_dw': out['m_conv_b_dw'], 'm_conv_ln_g': out['m_conv_ln_g'], 'm_conv_ln_b': out['m_conv_ln_b'], 'm_q_norm_g': out['m_q_norm_g'], 'm_k_norm_g': out['m_k_norm_g'], 'm_w_out': out['m_w_out'], 'm_ffn2_norm_g': out['m_ffn2_norm_g'], 'm_ffn2_w_gate': out['m_ffn2_w_gate'], 'm_ffn2_w_up': out['m_ffn2_w_up'], 'm_ffn2_w_down': out['m_ffn2_w_down'], 'v_ffn1_norm_g': out['v_ffn1_norm_g'], 'v_ffn1_w_gate': out['v_ffn1_w_gate'], 'v_ffn1_w_up': out['v_ffn1_w_up'], 'v_ffn1_w_down': out['v_ffn1_w_down'], 'v_mix_norm_g': out['v_mix_norm_g'], 'v_w_in': out['v_w_in'], 'v_conv_w_dw': out['v_conv_w_dw'], 'v_conv_b_dw': out['v_conv_b_dw'], 'v_conv_ln_g': out['v_conv_ln_g'], 'v_conv_ln_b': out['v_conv_ln_b'], 'v_q_norm_g': out['v_q_norm_g'], 'v_k_norm_g': out['v_k_norm_g'], 'v_w_out': out['v_w_out'], 'v_ffn2_norm_g': out['v_ffn2_norm_g'], 'v_ffn2_w_gate': out['v_ffn2_w_gate'], 'v_ffn2_w_up': out['v_ffn2_w_up'], 'v_ffn2_w_down': out['v_ffn2_w_down']}


def _loss(weights, diff, rest, loss_target):
    with _jax.named_scope("forward"):
        args = {**rest, TWIN_DIFF_INPUT: diff, **{k: w.astype(_WEIGHT_DTYPES[k]) for k, w in weights.items()}}
        y = _forward(args)
    with _jax.named_scope("loss_head"):
        err = _jnp.square(y.astype(_jnp.float32) - loss_target)
        return 0.5 * _jnp.sum(_jnp.mean(err, axis=-1)) if err.ndim else 0.5 * err


def _adamw(w, g, m, v):
    m = ADAM_B1 * m + (1.0 - ADAM_B1) * g
    v = ADAM_B2 * v + (1.0 - ADAM_B2) * _jnp.square(g)
    m_hat = m / (1.0 - ADAM_B1 ** ADAM_STEP)
    v_hat = v / (1.0 - ADAM_B2 ** ADAM_STEP)
    delta = -ADAM_LR * (m_hat / (_jnp.sqrt(v_hat) + ADAM_EPS) + ADAM_WD * w)
    return delta, m, v


def reference(x, ffn1_norm_g, ffn1_w_gate, ffn1_w_up, ffn1_w_down, mix_norm_g, w_in, conv_w_dw, conv_b_dw, conv_ln_g, conv_ln_b, q_norm_g, k_norm_g, w_out, ffn2_norm_g, ffn2_w_gate, ffn2_w_up, ffn2_w_down, loss_target, m_ffn1_norm_g, m_ffn1_w_gate, m_ffn1_w_up, m_ffn1_w_down, m_mix_norm_g, m_w_in, m_conv_w_dw, m_conv_b_dw, m_conv_ln_g, m_conv_ln_b, m_q_norm_g, m_k_norm_g, m_w_out, m_ffn2_norm_g, m_ffn2_w_gate, m_ffn2_w_up, m_ffn2_w_down, v_ffn1_norm_g, v_ffn1_w_gate, v_ffn1_w_up, v_ffn1_w_down, v_mix_norm_g, v_w_in, v_conv_w_dw, v_conv_b_dw, v_conv_ln_g, v_conv_ln_b, v_q_norm_g, v_k_norm_g, v_w_out, v_ffn2_norm_g, v_ffn2_w_gate, v_ffn2_w_up, v_ffn2_w_down):
    given = dict(x=x, ffn1_norm_g=ffn1_norm_g, ffn1_w_gate=ffn1_w_gate, ffn1_w_up=ffn1_w_up, ffn1_w_down=ffn1_w_down, mix_norm_g=mix_norm_g, w_in=w_in, conv_w_dw=conv_w_dw, conv_b_dw=conv_b_dw, conv_ln_g=conv_ln_g, conv_ln_b=conv_ln_b, q_norm_g=q_norm_g, k_norm_g=k_norm_g, w_out=w_out, ffn2_norm_g=ffn2_norm_g, ffn2_w_gate=ffn2_w_gate, ffn2_w_up=ffn2_w_up, ffn2_w_down=ffn2_w_down, loss_target=loss_target, m_ffn1_norm_g=m_ffn1_norm_g, m_ffn1_w_gate=m_ffn1_w_gate, m_ffn1_w_up=m_ffn1_w_up, m_ffn1_w_down=m_ffn1_w_down, m_mix_norm_g=m_mix_norm_g, m_w_in=m_w_in, m_conv_w_dw=m_conv_w_dw, m_conv_b_dw=m_conv_b_dw, m_conv_ln_g=m_conv_ln_g, m_conv_ln_b=m_conv_ln_b, m_q_norm_g=m_q_norm_g, m_k_norm_g=m_k_norm_g, m_w_out=m_w_out, m_ffn2_norm_g=m_ffn2_norm_g, m_ffn2_w_gate=m_ffn2_w_gate, m_ffn2_w_up=m_ffn2_w_up, m_ffn2_w_down=m_ffn2_w_down, v_ffn1_norm_g=v_ffn1_norm_g, v_ffn1_w_gate=v_ffn1_w_gate, v_ffn1_w_up=v_ffn1_w_up, v_ffn1_w_down=v_ffn1_w_down, v_mix_norm_g=v_mix_norm_g, v_w_in=v_w_in, v_conv_w_dw=v_conv_w_dw, v_conv_b_dw=v_conv_b_dw, v_conv_ln_g=v_conv_ln_g, v_conv_ln_b=v_conv_ln_b, v_q_norm_g=v_q_norm_g, v_k_norm_g=v_k_norm_g, v_w_out=v_w_out, v_ffn2_norm_g=v_ffn2_norm_g, v_ffn2_w_gate=v_ffn2_w_gate, v_ffn2_w_up=v_ffn2_w_up, v_ffn2_w_down=v_ffn2_w_down)
    weights = {n: given[n] for n in TWIN_WEIGHTS}
    shared = {n: given[n] for n in SHARED_INPUTS}
    per_example = {n: given[n] for n in ['x']}
    grad_fn = _jax.value_and_grad(_loss, argnums=(0, 1))

    def one_microbatch(ex, loss_target):
        ex = dict(ex)
        diff = ex.pop(TWIN_DIFF_INPUT)
        return grad_fn(weights, diff, {**shared, **ex}, loss_target)

    if N_MICROBATCH == 1:
        loss, (grad_w, grad_x) = one_microbatch(per_example, given["loss_target"])
    else:
        def body(carry, xs):
            loss_sum, grad_sum = carry
            l_k, (gw_k, gx_k) = one_microbatch(xs[0], xs[1])
            with _jax.named_scope("update"):
                return (loss_sum + l_k, _jax.tree.map(_jnp.add, grad_sum, gw_k)), gx_k

        init = (_jnp.zeros((), _jnp.float32), _jax.tree.map(_jnp.zeros_like, weights))
        (loss, grad_w), grad_x = _jax.lax.scan(body, init, (per_example, given["loss_target"]))
    with _jax.named_scope("update"):
        delta_w, new_m, new_v = {}, {}, {}
        for n in TWIN_WEIGHTS:
            delta_w[n], new_m[n], new_v[n] = _adamw(weights[n], grad_w[n], given["m_" + n], given["v_" + n])
    return (loss, grad_x, *[grad_w[n] for n in TWIN_WEIGHTS], *[delta_w[n] for n in TWIN_WEIGHTS],
            *[new_m[n] for n in TWIN_WEIGHTS], *[new_v[n] for n in TWIN_WEIGHTS])
```

```python
import jax
import jax.numpy as jnp
from jax import lax
from jax.experimental import pallas as pl
from jax.experimental.pallas import tpu as pltpu

F32 = jnp.float32
BF16 = jnp.bfloat16
EPS = 1e-6
WINDOW = 128
DILATIONS = (1, 4, 16)
ALIBI_MAX_BIAS = 8.0
LANES = 128
HALO = 32
ADAM_LR, ADAM_B1, ADAM_B2, ADAM_EPS, ADAM_WD, ADAM_STEP = 0.001, 0.9, 0.999, 1e-08, 0.01, 10
VMEM_LIMIT_MB = 56
MESH = pl.DeviceIdType.MESH
ANY = pl.BlockSpec(memory_space=pl.ANY)
AXES = ("x", "y", "c")
NEG = -1e30


def _pick(n, cands):
    for c in cands:
        if n % c == 0:
            return c
    return n


def _params(nsem):
    return pltpu.CompilerParams(dimension_semantics=("arbitrary",) * nsem,
                                vmem_limit_bytes=VMEM_LIMIT_MB << 20)


def _nn(a, b):
    return jnp.dot(a, b, preferred_element_type=F32)


def _nt(a, b):
    return lax.dot_general(a, b, (((1,), (1,)), ((), ())), preferred_element_type=F32)


def _tn(a, b):
    return lax.dot_general(a, b, (((0,), (0,)), ((), ())), preferred_element_type=F32)


def _sigmoid(v):
    return jax.nn.sigmoid(v)


def _rms_r(xv):
    return lax.rsqrt(jnp.mean(xv * xv, axis=-1, keepdims=True) + EPS)


def _norm_matmul(name, x, g, ws, swiglu):
    T, D = x.shape
    N = ws[0].shape[1]
    tm = _pick(T, (512, 256, 128))
    tn = _pick(N, (512, 256, 128))
    nw = len(ws)

    def body(*refs):
        x_ref, g_ref = refs[:2]
        w_refs = refs[2:2 + nw]
        outs = refs[2 + nw:-1]
        hs = refs[-1]

        @pl.when(pl.program_id(1) == 0)
        def _():
            xv = x_ref[...]
            hv = (xv * _rms_r(xv) * g_ref[...]).astype(BF16)
            hs[...] = hv
            outs[0][...] = hv

        h = hs[...]
        if swiglu:
            gt = _nn(h, w_refs[0][...])
            u = _nn(h, w_refs[1][...])
            outs[1][...] = gt.astype(BF16)
            outs[2][...] = u.astype(BF16)
            outs[3][...] = (gt * _sigmoid(gt) * u).astype(BF16)
        else:
            outs[1][...] = _nn(h, w_refs[0][...])

    row = pl.BlockSpec((tm, D), lambda i, j: (i, 0))
    col = pl.BlockSpec((D, tn), lambda i, j: (0, j))
    tile = pl.BlockSpec((tm, tn), lambda i, j: (i, j))
    if swiglu:
        out_shape = [jax.ShapeDtypeStruct((T, D), BF16)] + [jax.ShapeDtypeStruct((T, N), BF16)] * 3
        out_specs = [row, tile, tile, tile]
    else:
        out_shape = [jax.ShapeDtypeStruct((T, D), BF16), jax.ShapeDtypeStruct((T, N), F32)]
        out_specs = [row, tile]
    return pl.pallas_call(
        body, name=name, grid=(T // tm, N // tn),
        in_specs=[row, pl.BlockSpec((1, D), lambda i, j: (0, 0))] + [col] * nw,
        out_specs=out_specs, out_shape=out_shape,
        scratch_shapes=[pltpu.VMEM((tm, D), BF16)],
        compiler_params=_params(2),
    )(x, g, *ws)


def _matmul_res(name, a, w, res, scale, tgt=None):
    T, K = a.shape
    N = w.shape[1]
    tm = _pick(T, (512, 256, 128))
    tk = _pick(K, (512, 256, 128))
    nk = K // tk
    loss = tgt is not None

    def body(*refs):
        if loss:
            a_ref, w_ref, res_ref, tgt_ref, dx_ref, dxb_ref, lv_ref, acc = refs
        else:
            a_ref, w_ref, res_ref, out_ref, acc = refs
        i, k = pl.program_id(0), pl.program_id(1)

        @pl.when(k == 0)
        def _():
            acc[...] = jnp.zeros_like(acc)

        acc[...] += _nn(a_ref[...], w_ref[...])

        @pl.when(k == nk - 1)
        def _():
            val = res_ref[...] + scale * acc[...]
            if loss:
                dv = val - tgt_ref[...]
                dx = dv * (1.0 / N)
                dx_ref[...] = dx
                dxb_ref[...] = dx.astype(BF16)
                part = jnp.sum(dv * dv, axis=0, keepdims=True)

                @pl.when(i == 0)
                def _():
                    lv_ref[...] = part

                @pl.when(i > 0)
                def _():
                    lv_ref[...] += part
            else:
                out_ref[...] = val

    row = pl.BlockSpec((tm, N), lambda i, k: (i, 0))
    in_specs = [pl.BlockSpec((tm, tk), lambda i, k: (i, k)), pl.BlockSpec((tk, N), lambda i, k: (k, 0)), row]
    args = [a, w, res]
    if loss:
        in_specs.append(row)
        args.append(tgt)
        out_specs = [row, row, pl.BlockSpec((1, N), lambda i, k: (0, 0))]
        out_shape = [jax.ShapeDtypeStruct((T, N), F32), jax.ShapeDtypeStruct((T, N), BF16),
                     jax.ShapeDtypeStruct((1, N), F32)]
    else:
        out_specs = row
        out_shape = jax.ShapeDtypeStruct((T, N), F32)
    return pl.pallas_call(
        body, name=name, grid=(T // tm, nk), in_specs=in_specs, out_specs=out_specs, out_shape=out_shape,
        scratch_shapes=[pltpu.VMEM((tm, N), F32)], compiler_params=_params(2),
    )(*args)


def _nt_matmul(name, dyb, w, scale=1.0, gate=None, up=None):
    T, D = dyb.shape
    N = w.shape[0]
    tm = _pick(T, (512, 256, 128))
    tn = _pick(N, (512, 256, 128))
    swiglu = gate is not None

    def body(*refs):
        if swiglu:
            dy_ref, w_ref, g_ref, u_ref, dg_ref, du_ref = refs
        else:
            dy_ref, w_ref, o_ref, ob_ref = refs
        da = _nt(dy_ref[...], w_ref[...]) * scale
        if swiglu:
            gt = g_ref[...].astype(F32)
            u = u_ref[...].astype(F32)
            sg = _sigmoid(gt)
            dg_ref[...] = (da * u * (sg * (1.0 + gt * (1.0 - sg)))).astype(BF16)
            du_ref[...] = (da * (gt * sg)).astype(BF16)
        else:
            o_ref[...] = da
            ob_ref[...] = da.astype(BF16)

    tile = pl.BlockSpec((tm, tn), lambda i, j: (i, j))
    in_specs = [pl.BlockSpec((tm, D), lambda i, j: (i, 0)), pl.BlockSpec((tn, D), lambda i, j: (j, 0))]
    args = [dyb, w]
    if swiglu:
        in_specs += [tile, tile]
        args += [gate, up]
        out_shape = [jax.ShapeDtypeStruct((T, N), BF16)] * 2
    else:
        out_shape = [jax.ShapeDtypeStruct((T, N), F32), jax.ShapeDtypeStruct((T, N), BF16)]
    return pl.pallas_call(
        body, name=name, grid=(T // tm, N // tn), in_specs=in_specs, out_specs=[tile, tile],
        out_shape=out_shape, compiler_params=_params(2),
    )(*args)


def _nt_rms_bwd(name, As, Ws, x, g, dres):
    T, K = As[0].shape
    D = x.shape[1]
    tm = _pick(T, (512, 256, 128))
    tk = _pick(K, (512, 256, 128))
    nk = K // tk
    na = len(As)

    def body(*refs):
        a_refs = refs[:na]
        w_refs = refs[na:2 * na]
        x_ref, g_ref, dres_ref, dx_ref, dxb_ref, dg_ref, acc = refs[2 * na:]
        i, k = pl.program_id(0), pl.program_id(1)

        @pl.when(k == 0)
        def _():
            acc[...] = jnp.zeros_like(acc)

        for a_ref, w_ref in zip(a_refs, w_refs):
            acc[...] += _nt(a_ref[...], w_ref[...])

        @pl.when(k == nk - 1)
        def _():
            dh = acc[...]
            xv = x_ref[...]
            r = _rms_r(xv)
            gd = dh * g_ref[...]
            dx = dres_ref[...] + r * gd - xv * (r * r * r) * jnp.mean(gd * xv, axis=-1, keepdims=True)
            dx_ref[...] = dx
            dxb_ref[...] = dx.astype(BF16)
            part = jnp.sum(dh * xv * r, axis=0, keepdims=True)

            @pl.when(i == 0)
            def _():
                dg_ref[...] = part

            @pl.when(i > 0)
            def _():
                dg_ref[...] += part

    row = pl.BlockSpec((tm, D), lambda i, k: (i, 0))
    vec = pl.BlockSpec((1, D), lambda i, k: (0, 0))
    return pl.pallas_call(
        body, name=name, grid=(T // tm, nk),
        in_specs=[pl.BlockSpec((tm, tk), lambda i, k: (i, k))] * na
        + [pl.BlockSpec((D, tk), lambda i, k: (0, k))] * na + [row, vec, row],
        out_specs=[row, row, vec],
        out_shape=[jax.ShapeDtypeStruct((T, D), F32), jax.ShapeDtypeStruct((T, D), BF16),
                   jax.ShapeDtypeStruct((1, D), F32)],
        scratch_shapes=[pltpu.VMEM((tm, D), F32)], compiler_params=_params(2),
    )(*As, *Ws, x, g, dres)


def _tn_matmul(name, a, b, scale=1.0):
    T, M = a.shape
    N = b.shape[1]
    tm = _pick(M, (1024, 512, 256, 128))
    tn = _pick(N, (1408, 1280, 1024, 512, 256, 128))
    tk = _pick(T, (512, 256, 128))
    nk = T // tk

    def body(a_ref, b_ref, o_ref, acc):
        k = pl.program_id(2)

        @pl.when(k == 0)
        def _():
            acc[...] = jnp.zeros_like(acc)

        acc[...] += _tn(a_ref[...], b_ref[...])

        @pl.when(k == nk - 1)
        def _():
            o_ref[...] = (acc[...] * scale).astype(BF16)

    return pl.pallas_call(
        body, name=name, grid=(M // tm, N // tn, nk),
        in_specs=[pl.BlockSpec((tk, tm), lambda i, j, k: (k, i)), pl.BlockSpec((tk, tn), lambda i, j, k: (k, j))],
        out_specs=pl.BlockSpec((tm, tn), lambda i, j, k: (i, j)),
        out_shape=jax.ShapeDtypeStruct((M, N), BF16),
        scratch_shapes=[pltpu.VMEM((tm, tn), F32)], compiler_params=_params(3),
    )(a, b)


CONV_ROWS = 128
ROW_CHUNK = 32
LANE_CHUNK = 256


def _conv_fwd(name, z, w32, b, lg, lb, C):
    T = z.shape[0]
    tc = CONV_ROWS
    ntap = 31
    lc = _pick(C, (LANE_CHUNK, LANES))
    rpb = tc // HALO

    def body(zc_ref, zp_ref, w_ref, b_ref, lg_ref, lb_ref, yc_ref, ycv_ref, vbuf, ybuf):
        i = pl.program_id(0)
        zc = zc_ref[...]
        zp = zp_ref[...]
        vbuf[HALO:HALO + tc, :] = zc[:, :C] * _sigmoid(zc[:, C:])
        vbuf[0:HALO, :] = jnp.where(i > 0, zp[:, :C] * _sigmoid(zp[:, C:]), 0.0)
        for r0 in range(0, tc, ROW_CHUNK):
            for c0 in range(0, C, lc):
                acc = jnp.zeros((ROW_CHUNK, lc), F32) + b_ref[:, c0:c0 + lc]
                for k in range(ntap):
                    s = r0 + 2 + k
                    acc = acc + w_ref[k:k + 1, c0:c0 + lc] * vbuf[s:s + ROW_CHUNK, c0:c0 + lc]
                ybuf[r0:r0 + ROW_CHUNK, c0:c0 + lc] = acc
        y = ybuf[...]
        ycv_ref[...] = y
        mu = jnp.mean(y, axis=-1, keepdims=True)
        yc = y - mu
        rstd = lax.rsqrt(jnp.mean(yc * yc, axis=-1, keepdims=True) + EPS)
        ln = yc * rstd * lg_ref[...] + lb_ref[...]
        yc_ref[...] = (ln * _sigmoid(ln)).astype(BF16)

    vec = pl.BlockSpec((1, C), lambda i: (0, 0))
    return pl.pallas_call(
        body, name=name, grid=(T // tc,),
        in_specs=[pl.BlockSpec((tc, 2 * C), lambda i: (i, 0)),
                  pl.BlockSpec((HALO, 2 * C), lambda i: (jnp.maximum(i * rpb - 1, 0), 0)),
                  pl.BlockSpec((HALO, C), lambda i: (0, 0)), vec, vec, vec],
        out_specs=[pl.BlockSpec((tc, C), lambda i: (i, 0))] * 2,
        out_shape=[jax.ShapeDtypeStruct((T, C), BF16), jax.ShapeDtypeStruct((T, C), F32)],
        scratch_shapes=[pltpu.VMEM((tc + HALO, C), F32), pltpu.VMEM((tc, C), F32)],
        compiler_params=_params(1),
    )(z, z, w32, b, lg, lb)


def _conv_bwd(name, z, ycv, dycat, w32, lg, lb, C):
    T = z.shape[0]
    tc = CONV_ROWS
    ntap = 31
    lc = _pick(C, (LANE_CHUNK, LANES))
    rpb = tc // HALO
    nstep = T // tc
    nhb = T // HALO

    def ln_bwd(dyc, y, lgv, lbv):
        mu = jnp.mean(y, axis=-1, keepdims=True)
        yc = y - mu
        rstd = lax.rsqrt(jnp.mean(yc * yc, axis=-1, keepdims=True) + EPS)
        yn = yc * rstd
        ln = yn * lgv + lbv
        sg = _sigmoid(ln)
        dln = dyc * (sg * (1.0 + ln * (1.0 - sg)))
        dyn = dln * lgv
        dy = rstd * (dyn - jnp.mean(dyn, axis=-1, keepdims=True)
                     - yn * jnp.mean(dyn * yn, axis=-1, keepdims=True))
        return dy, dln, yn

    def body(zc_ref, zp_ref, y_ref, yn_ref, d_ref, dn_ref, w_ref, lg_ref, lb_ref,
             dz_ref, dw_ref, db_ref, dlg_ref, dlb_ref, vbuf, dbuf, dvbuf, dwacc):
        i = pl.program_id(0)
        lgv, lbv = lg_ref[...], lb_ref[...]
        zc = zc_ref[...]
        zp = zp_ref[...]
        a = zc[:, :C]
        sgt = _sigmoid(zc[:, C:])
        vbuf[HALO:HALO + tc, :] = a * sgt
        vbuf[0:HALO, :] = jnp.where(i > 0, zp[:, :C] * _sigmoid(zp[:, C:]), 0.0)
        dy, dln, yn = ln_bwd(d_ref[...], y_ref[...], lgv, lbv)
        dbuf[0:tc, :] = dy
        dyn_, _, _ = ln_bwd(dn_ref[...], yn_ref[...], lgv, lbv)
        dbuf[tc:tc + HALO, :] = jnp.where(i < nstep - 1, dyn_, 0.0)

        @pl.when(i == 0)
        def _():
            dwacc[...] = jnp.zeros_like(dwacc)
            db_ref[...] = jnp.zeros_like(db_ref)
            dlg_ref[...] = jnp.zeros_like(dlg_ref)
            dlb_ref[...] = jnp.zeros_like(dlb_ref)

        db_ref[...] += jnp.sum(dy, axis=0, keepdims=True)
        dlg_ref[...] += jnp.sum(dln * yn, axis=0, keepdims=True)
        dlb_ref[...] += jnp.sum(dln, axis=0, keepdims=True)

        for r0 in range(0, tc, ROW_CHUNK):
            for c0 in range(0, C, lc):
                dcur = dbuf[r0:r0 + ROW_CHUNK, c0:c0 + lc]
                acc = jnp.zeros((ROW_CHUNK, lc), F32)
                for k in range(ntap):
                    s = r0 + 30 - k
                    acc = acc + w_ref[k:k + 1, c0:c0 + lc] * dbuf[s:s + ROW_CHUNK, c0:c0 + lc]
                    sv = r0 + 2 + k
                    prod = dcur * vbuf[sv:sv + ROW_CHUNK, c0:c0 + lc]
                    red = prod[0:8]
                    for q in range(8, ROW_CHUNK, 8):
                        red = red + prod[q:q + 8]
                    dwacc[8 * k:8 * k + 8, c0:c0 + lc] += red
                dvbuf[r0:r0 + ROW_CHUNK, c0:c0 + lc] = acc
        dv = dvbuf[...]
        dz_ref[:, :C] = (dv * sgt).astype(BF16)
        dz_ref[:, C:] = (dv * a * sgt * (1.0 - sgt)).astype(BF16)

        @pl.when(i == nstep - 1)
        def _():
            for k in range(ntap):
                dw_ref[k:k + 1, :] = jnp.sum(dwacc[8 * k:8 * k + 8, :], axis=0, keepdims=True)
            dw_ref[ntap:HALO, :] = jnp.zeros((HALO - ntap, C), F32)

    vec = pl.BlockSpec((1, C), lambda i: (0, 0))
    cur = pl.BlockSpec((tc, C), lambda i: (i, 0))
    nxt = pl.BlockSpec((HALO, C), lambda i: (jnp.minimum((i + 1) * rpb, nhb - 1), 0))
    return pl.pallas_call(
        body, name=name, grid=(nstep,),
        in_specs=[pl.BlockSpec((tc, 2 * C), lambda i: (i, 0)),
                  pl.BlockSpec((HALO, 2 * C), lambda i: (jnp.maximum(i * rpb - 1, 0), 0)),
                  cur, nxt, cur, nxt, pl.BlockSpec((HALO, C), lambda i: (0, 0)), vec, vec],
        out_specs=[pl.BlockSpec((tc, 2 * C), lambda i: (i, 0)), pl.BlockSpec((HALO, C), lambda i: (0, 0)),
                   vec, vec, vec],
        out_shape=[jax.ShapeDtypeStruct((T, 2 * C), BF16), jax.ShapeDtypeStruct((HALO, C), F32),
                   jax.ShapeDtypeStruct((1, C), F32), jax.ShapeDtypeStruct((1, C), F32),
                   jax.ShapeDtypeStruct((1, C), F32)],
        scratch_shapes=[pltpu.VMEM((tc + HALO, C), F32), pltpu.VMEM((tc + HALO, C), F32),
                        pltpu.VMEM((tc, C), F32), pltpu.VMEM((8 * HALO, C), F32)],
        compiler_params=_params(1),
    )(z, z, ycv, ycv, dycat, dycat, w32, lg, lb)


def _seg_sum(u, bmat):
    hi = u.astype(BF16)
    lo = (u - hi.astype(F32)).astype(BF16)
    return _nn(hi, bmat) + _nn(lo, bmat)


def _attn_prep(name, z, gq, gk, bmat, A, c0, hd):
    T = z.shape[0]
    tm = _pick(T, (256, 128))

    def body(zq_ref, zk_ref, zv_ref, gq_ref, gk_ref, b_ref, o_ref):
        bm = b_ref[...]
        for idx, (z_ref, g_ref) in enumerate(((zq_ref, gq_ref), (zk_ref, gk_ref))):
            zv = z_ref[...]
            r = lax.rsqrt(_seg_sum(zv * zv, bm) * (1.0 / hd) + EPS)
            o_ref[:, idx * A:(idx + 1) * A] = (zv * r * g_ref[...]).astype(BF16)
        o_ref[:, 2 * A:] = zv_ref[...].astype(BF16)

    vec = pl.BlockSpec((1, A), lambda i: (0, 0))
    return pl.pallas_call(
        body, name=name, grid=(T // tm,),
        in_specs=[pl.BlockSpec((tm, A), lambda i: (i, c0)), pl.BlockSpec((tm, A), lambda i: (i, c0 + 1)),
                  pl.BlockSpec((tm, A), lambda i: (i, c0 + 2)), vec, vec,
                  pl.BlockSpec((A, A), lambda i: (0, 0))],
        out_specs=pl.BlockSpec((tm, 3 * A), lambda i: (i, 0)),
        out_shape=jax.ShapeDtypeStruct((T, 3 * A), BF16), compiler_params=_params(1),
    )(z, z, z, gq, gk, bmat)


def _attn_geometry(T, d, A):
    L = T // d
    Lc = min(L, 1024)
    return L, Lc, Lc // WINDOW, L // Lc, A // LANES


def _scores(qh, k2, slope_d, dist, valid):
    s = _nt(qh, k2) * 0.125 - slope_d * dist
    return jnp.where(valid, s, NEG)


def _band(n):
    qi = lax.broadcasted_iota(jnp.int32, (WINDOW, 2 * WINDOW), 0)
    kj = lax.broadcasted_iota(jnp.int32, (WINDOW, 2 * WINDOW), 1)
    dist_i = WINDOW + qi - kj
    first_key = jnp.where(n > 0, 0, WINDOW)
    valid = (dist_i >= 0) & (dist_i <= WINDOW) & (kj >= first_key)
    return dist_i.astype(F32), valid


def _attn_fwd(name, qkv, slopes, d, A):
    T = qkv.shape[0]
    L, Lc, nb, nch, hpn = _attn_geometry(T, d, A)
    cpr = 3 * hpn
    qv = qkv.reshape(L, d * 3 * A)

    def body(sl_ref, q_ref, k_ref, kh_ref, v_ref, vh_ref, o_ref, l_ref, kbuf, vbuf):
        cb, ch = pl.program_id(0), pl.program_id(1)
        hp = cb % hpn
        kbuf[0:WINDOW, :] = kh_ref[...]
        kbuf[WINDOW:, :] = k_ref[...]
        vbuf[0:WINDOW, :] = vh_ref[...]
        vbuf[WINDOW:, :] = v_ref[...]
        lane = lax.broadcasted_iota(jnp.int32, (1, LANES), 1)
        first = lane < (LANES // 2)

        def step(nl, carry):
            r0 = pl.multiple_of(nl * WINDOW, WINDOW)
            dist, valid = _band(ch * nb + nl)
            qb = q_ref[pl.ds(r0, WINDOW), :]
            k2 = kbuf[pl.ds(r0, 2 * WINDOW), :]
            v2 = vbuf[pl.ds(r0, 2 * WINDOW), :]
            res = []
            for hh in range(2):
                mh = first if hh == 0 else jnp.logical_not(first)
                s = _scores(jnp.where(mh, qb, jnp.zeros_like(qb)), k2, sl_ref[2 * hp + hh] * d, dist, valid)
                mx = jnp.max(s, axis=-1, keepdims=True)
                p = jnp.exp(s - mx)
                den = jnp.sum(p, axis=-1, keepdims=True)
                res.append((_nn(p.astype(BF16), v2) / den, mx + jnp.log(den)))
            o_ref[pl.ds(r0, WINDOW), :] = jnp.where(first, res[0][0], res[1][0])
            l_ref[pl.ds(r0, WINDOW), :] = jnp.where(first, res[0][1], res[1][1])
            return carry

        lax.fori_loop(0, nb, step, 0)

    def cmap(which):
        return lambda cb, ch: (ch, (cb // hpn) * cpr + which * hpn + cb % hpn)

    def hmap(which):
        return lambda cb, ch: (jnp.maximum(ch * nb - 1, 0), (cb // hpn) * cpr + which * hpn + cb % hpn)

    blk = lambda m: pl.BlockSpec((Lc, LANES), m)
    hblk = lambda m: pl.BlockSpec((WINDOW, LANES), m)
    omap = lambda cb, ch: (ch, cb)
    o, l = pl.pallas_call(
        body, name=name, grid=(d * hpn, nch),
        in_specs=[pl.BlockSpec(memory_space=pltpu.SMEM), blk(cmap(0)), blk(cmap(1)), hblk(hmap(1)),
                  blk(cmap(2)), hblk(hmap(2))],
        out_specs=[blk(omap), blk(omap)],
        out_shape=[jax.ShapeDtypeStruct((L, d * A), F32)] * 2,
        scratch_shapes=[pltpu.VMEM((Lc + WINDOW, LANES), BF16)] * 2, compiler_params=_params(2),
    )(slopes, qv, qv, qv, qv, qv)
    return o.reshape(T, A), l.reshape(T, A)


def _attn_combine(name, os_, ls_):
    T, A = os_[0].shape
    tm = _pick(T, (512, 256, 128))
    nbr = len(os_)

    def body(*refs):
        o_refs, l_refs = refs[:nbr], refs[nbr:2 * nbr]
        y_ref, lg_ref = refs[2 * nbr:]
        ls = [r[...] for r in l_refs]
        mx = ls[0]
        for v in ls[1:]:
            mx = jnp.maximum(mx, v)
        es = [jnp.exp(v - mx) for v in ls]
        den = es[0]
        for e in es[1:]:
            den = den + e
        out = es[0] * o_refs[0][...]
        for e, o_ref in zip(es[1:], o_refs[1:]):
            out = out + e * o_ref[...]
        y_ref[...] = (out / den).astype(BF16)
        lg_ref[...] = mx + jnp.log(den)

    blk = pl.BlockSpec((tm, A), lambda i: (i, 0))
    return pl.pallas_call(
        body, name=name, grid=(T // tm,), in_specs=[blk] * (2 * nbr), out_specs=[blk, blk],
        out_shape=[jax.ShapeDtypeStruct((T, A), BF16), jax.ShapeDtypeStruct((T, A), F32)],
        compiler_params=_params(1),
    )(*os_, *ls_)


def _attn_bwd(name, qkv, dycatb, yatt, lg, slopes, d, A, catw, catoff):
    T = qkv.shape[0]
    L, Lc, nb, nch, hpn = _attn_geometry(T, d, A)
    cpr = 3 * hpn
    qv = qkv.reshape(L, d * 3 * A)
    dov = dycatb.reshape(L, d * catw)
    ov = yatt.reshape(L, d * A)
    lv = lg.reshape(L, d * A)
    cw, co = catw // LANES, catoff // LANES

    def body(sl_ref, q_ref, k_ref, kh_ref, v_ref, vh_ref, do_ref, o_ref, l_ref,
             dq_ref, dk_ref, dv_ref, kbuf, vbuf):
        cb, ch = pl.program_id(0), pl.program_id(1)
        hp = cb % hpn
        kbuf[0:WINDOW, :] = kh_ref[...]
        kbuf[WINDOW:, :] = k_ref[...]
        vbuf[0:WINDOW, :] = vh_ref[...]
        vbuf[WINDOW:, :] = v_ref[...]
        lane = lax.broadcasted_iota(jnp.int32, (1, LANES), 1)
        first = lane < (LANES // 2)

        @pl.when(ch == 0)
        def _():
            dk_ref[...] = jnp.zeros_like(dk_ref)
            dv_ref[...] = jnp.zeros_like(dv_ref)

        def step(nl, carry):
            r0 = pl.multiple_of(nl * WINDOW, WINDOW)
            n = ch * nb + nl
            dist, valid = _band(n)
            qb = q_ref[pl.ds(r0, WINDOW), :]
            k2 = kbuf[pl.ds(r0, 2 * WINDOW), :]
            v2 = vbuf[pl.ds(r0, 2 * WINDOW), :]
            dob = do_ref[pl.ds(r0, WINDOW), :]
            dd = dob.astype(F32) * o_ref[pl.ds(r0, WINDOW), :].astype(F32)
            lb = l_ref[pl.ds(r0, WINDOW), :]
            dk2 = jnp.zeros((2 * WINDOW, LANES), F32)
            dv2 = jnp.zeros((2 * WINDOW, LANES), F32)
            dqs = []
            for hh in range(2):
                mh = first if hh == 0 else jnp.logical_not(first)
                qh = jnp.where(mh, qb, jnp.zeros_like(qb))
                doh = jnp.where(mh, dob, jnp.zeros_like(dob))
                s = _scores(qh, k2, sl_ref[2 * hp + hh] * d, dist, valid)
                lcol = lb[:, hh * (LANES // 2):hh * (LANES // 2) + 1]
                p = jnp.exp(s - lcol)
                dcol = jnp.sum(jnp.where(mh, dd, 0.0), axis=-1, keepdims=True)
                ds = (p * (_nt(doh, v2) - dcol) * 0.125).astype(BF16)
                dqs.append(_nn(ds, k2))
                dk2 = dk2 + _tn(ds, qh)
                dv2 = dv2 + _tn(p.astype(BF16), doh)
            dq_ref[pl.ds(r0, WINDOW), :] = jnp.where(first, dqs[0], dqs[1])
            g0 = pl.multiple_of(n * WINDOW, WINDOW)
            dk_ref[pl.ds(g0, WINDOW), :] += dk2[WINDOW:]
            dv_ref[pl.ds(g0, WINDOW), :] += dv2[WINDOW:]

            @pl.when(n > 0)
            def _():
                gp = pl.multiple_of((n - 1) * WINDOW, WINDOW)
                dk_ref[pl.ds(gp, WINDOW), :] += dk2[:WINDOW]
                dv_ref[pl.ds(gp, WINDOW), :] += dv2[:WINDOW]

            return carry

        lax.fori_loop(0, nb, step, 0)

    def cmap(which):
        return lambda cb, ch: (ch, (cb // hpn) * cpr + which * hpn + cb % hpn)

    def hmap(which):
        return lambda cb, ch: (jnp.maximum(ch * nb - 1, 0), (cb // hpn) * cpr + which * hpn + cb % hpn)

    blk = lambda m: pl.BlockSpec((Lc, LANES), m)
    hblk = lambda m: pl.BlockSpec((WINDOW, LANES), m)
    omap = lambda cb, ch: (ch, cb)
    full = pl.BlockSpec((L, LANES), lambda cb, ch: (0, cb))
    dq, dk, dv = pl.pallas_call(
        body, name=name, grid=(d * hpn, nch),
        in_specs=[pl.BlockSpec(memory_space=pltpu.SMEM), blk(cmap(0)), blk(cmap(1)), hblk(hmap(1)),
                  blk(cmap(2)), hblk(hmap(2)),
                  blk(lambda cb, ch: (ch, (cb // hpn) * cw + co + cb % hpn)), blk(omap), blk(omap)],
        out_specs=[blk(omap), full, full],
        out_shape=[jax.ShapeDtypeStruct((L, d * A), F32)] * 3,
        scratch_shapes=[pltpu.VMEM((Lc + WINDOW, LANES), BF16)] * 2, compiler_params=_params(2),
    )(slopes, qv, qv, qv, qv, qv, dov, ov, lv)
    return dq.reshape(T, A), dk.reshape(T, A), dv.reshape(T, A)


def _attn_bwd_combine(name, dqs, dks, dvs, z, gq, gk, bmat, fmat, A, c0, hd):
    T = z.shape[0]
    tm = _pick(T, (256, 128))
    nbr = len(dqs)

    def body(*refs):
        dq_refs, dk_refs, dv_refs = refs[:nbr], refs[nbr:2 * nbr], refs[2 * nbr:3 * nbr]
        zq_ref, zk_ref, gq_ref, gk_ref, b_ref, f_ref, dz_ref, dgq_ref, dgk_ref = refs[3 * nbr:]
        i = pl.program_id(0)
        bm = b_ref[...]

        def tot(rs):
            t = rs[0][...]
            for r in rs[1:]:
                t = t + r[...]
            return t

        for idx, (d_refs, z_ref, g_ref, dg_ref) in enumerate(
                ((dq_refs, zq_ref, gq_ref, dgq_ref), (dk_refs, zk_ref, gk_ref, dgk_ref))):
            dy = tot(d_refs)
            zv = z_ref[...]
            r = lax.rsqrt(_seg_sum(zv * zv, bm) * (1.0 / hd) + EPS)
            gd = dy * g_ref[...]
            mean = _seg_sum(gd * zv, bm) * (1.0 / hd)
            dz_ref[:, idx * A:(idx + 1) * A] = (r * gd - zv * (r * r * r) * mean).astype(BF16)
            part = jnp.sum(dy * zv * r, axis=0, keepdims=True)

            @pl.when(i == 0)
            def _():
                dg_ref[...] = part

            @pl.when(i > 0)
            def _():
                dg_ref[...] += part

        dz_ref[:, 2 * A:] = tot(dv_refs).astype(BF16)

        @pl.when(i == T // tm - 1)
        def _():
            fm = f_ref[...]
            for dg_ref in (dgq_ref, dgk_ref):
                v = jnp.broadcast_to(dg_ref[...], (8, A))
                hi = v.astype(BF16)
                mid = (v - hi.astype(F32)).astype(BF16)
                lo = (v - hi.astype(F32) - mid.astype(F32)).astype(BF16)
                dg_ref[...] = (_nn(hi, fm) + _nn(mid, fm) + _nn(lo, fm))[0:1]

    blk = pl.BlockSpec((tm, A), lambda i: (i, 0))
    vec = pl.BlockSpec((1, A), lambda i: (0, 0))
    return pl.pallas_call(
        body, name=name, grid=(T // tm,),
        in_specs=[blk] * (3 * nbr) + [pl.BlockSpec((tm, A), lambda i: (i, c0)),
                                      pl.BlockSpec((tm, A), lambda i: (i, c0 + 1)), vec, vec,
                                      pl.BlockSpec((A, A), lambda i: (0, 0)),
                                      pl.BlockSpec((A, A), lambda i: (0, 0))],
        out_specs=[pl.BlockSpec((tm, 3 * A), lambda i: (i, 0)), vec, vec],
        out_shape=[jax.ShapeDtypeStruct((T, 3 * A), BF16), jax.ShapeDtypeStruct((1, A), F32),
                   jax.ShapeDtypeStruct((1, A), F32)],
        compiler_params=_params(1),
    )(*dqs, *dks, *dvs, z, z, gq, gk, bmat, fmat)


def _local_step(x, tgt, W, hd):
    T, D = x.shape
    C = W["conv_b_dw"].shape[1]
    Dmix = W["w_out"].shape[0]
    A = Dmix - C
    H = A // hd
    c0 = (2 * C) // A
    slopes = 2.0 ** (-ALIBI_MAX_BIAS * jnp.arange(1, H + 1, dtype=F32) / H)
    seg = jnp.arange(A) // hd
    bmat = (seg[:, None] == seg[None, :]).astype(BF16)
    pos_in_head = jnp.arange(A) % hd
    fmat = (pos_in_head[:, None] == pos_in_head[None, :]).astype(BF16)
    gq = jnp.tile(W["q_norm_g"], (1, H))
    gk = jnp.tile(W["k_norm_g"], (1, H))
    w32 = W["conv_w32"]

    h1, gate1, up1, a1 = _norm_matmul("ffn1_up", x, W["ffn1_norm_g"], [W["ffn1_w_gate"], W["ffn1_w_up"]], True)
    x1 = _matmul_res("ffn1_down", a1, W["ffn1_w_down"], x, 0.5)
    h2, z = _norm_matmul("mix_in", x1, W["mix_norm_g"], [W["w_in"]], False)
    yc, ycv = _conv_fwd("conv_fwd", z, w32, W["conv_b_dw"], W["conv_ln_g"], W["conv_ln_b"], C)
    qkv = _attn_prep("attn_prep", z, gq, gk, bmat, A, c0, hd)
    os_, ls_ = [], []
    for d in DILATIONS:
        o, l = _attn_fwd("attn_fwd_d%d" % d, qkv, slopes, d, A)
        os_.append(o)
        ls_.append(l)
    yatt, lg = _attn_combine("attn_combine", os_, ls_)
    ycat = jnp.concatenate([yc, yatt], axis=1)
    x2 = _matmul_res("mix_out", ycat, W["w_out"], x1, 1.0)
    h3, gate2, up2, a2 = _norm_matmul("ffn2_up", x2, W["ffn2_norm_g"], [W["ffn2_w_gate"], W["ffn2_w_up"]], True)
    dx3, dx3b, lossvec = _matmul_res("ffn2_down_loss", a2, W["ffn2_w_down"], x2, 0.5, tgt=tgt)

    G = {}
    dgate2, dup2 = _nt_matmul("ffn2_dact", dx3b, W["ffn2_w_down"], 0.5, gate2, up2)
    G["ffn2_w_down"] = _tn_matmul("ffn2_dwd", a2, dx3b, 0.5)
    G["ffn2_w_gate"] = _tn_matmul("ffn2_dwg", h3, dgate2)
    G["ffn2_w_up"] = _tn_matmul("ffn2_dwu", h3, dup2)
    dx2, dx2b, G["ffn2_norm_g"] = _nt_rms_bwd("ffn2_dx", [dgate2, dup2], [W["ffn2_w_gate"], W["ffn2_w_up"]],
                                              x2, W["ffn2_norm_g"], dx3)
    G["w_out"] = _tn_matmul("mix_dwout", ycat, dx2b)
    dycat, dycatb = _nt_matmul("mix_dycat", dx2b, W["w_out"])
    dzc, dw32, G["conv_b_dw"], G["conv_ln_g"], G["conv_ln_b"] = _conv_bwd(
        "conv_bwd", z, ycv, dycat, w32, W["conv_ln_g"], W["conv_ln_b"], C)
    dqs, dks, dvs = [], [], []
    for d in DILATIONS:
        dq, dk, dv = _attn_bwd("attn_bwd_d%d" % d, qkv, dycatb, yatt, lg, slopes, d, A, Dmix, C)
        dqs.append(dq)
        dks.append(dk)
        dvs.append(dv)
    dzqkv, dgq, dgk = _attn_bwd_combine("attn_bwd_combine", dqs, dks, dvs, z, gq, gk, bmat, fmat, A, c0, hd)
    G["q_norm_g"], G["k_norm_g"] = dgq, dgk
    G["conv_w32"] = dw32
    dz = jnp.concatenate([dzc, dzqkv], axis=1)
    G["w_in"] = _tn_matmul("mix_dwin", h2, dz)
    dx1, dx1b, G["mix_norm_g"] = _nt_rms_bwd("mix_dx", [dz], [W["w_in"]], x1, W["mix_norm_g"], dx2)
    dgate1, dup1 = _nt_matmul("ffn1_dact", dx1b, W["ffn1_w_down"], 0.5, gate1, up1)
    G["ffn1_w_down"] = _tn_matmul("ffn1_dwd", a1, dx1b, 0.5)
    G["ffn1_w_gate"] = _tn_matmul("ffn1_dwg", h1, dgate1)
    G["ffn1_w_up"] = _tn_matmul("ffn1_dwu", h1, dup1)
    dx0, _, G["ffn1_norm_g"] = _nt_rms_bwd("ffn1_dx", [dgate1, dup1], [W["ffn1_w_gate"], W["ffn1_w_up"]],
                                           x, W["ffn1_norm_g"], dx1)
    return lossvec, dx0, G


BIG = (("ffn1_w_gate", 1), ("ffn1_w_up", 1), ("ffn1_w_down", 0), ("w_in", 1), ("w_out", 0),
       ("ffn2_w_gate", 1), ("ffn2_w_up", 1), ("ffn2_w_down", 0))
FLIPS = ((1, 0), (0, 1), (1, 1))


def _window(ref, shape, axis, slab=None, half=None):
    idx = [pl.ds(0, shape[0]), pl.ds(0, shape[1])]
    if slab is not None:
        n = shape[axis] // 4
        idx[axis] = pl.ds(pl.multiple_of(slab * n, 8), n)
    if half is not None:
        hs = shape[1 - axis] // 2
        idx[1 - axis] = pl.ds(pl.multiple_of(half * hs, 8), hs)
    return ref.at[idx[0], idx[1]]


def _position():
    return lax.axis_index("x"), lax.axis_index("y"), lax.axis_index("c")


def _half_shape(shape, axis):
    return (shape[0] // 2, shape[1]) if axis == 1 else (shape[0], shape[1] // 2)


def _slab_shape(shape, axis):
    return (shape[0], shape[1] // 4) if axis == 1 else (shape[0] // 4, shape[1])


def _piece_shape(shape, axis):
    return _half_shape(_slab_shape(shape, axis), axis)


def _gather_weights(shards, conv_shard):
    nt = len(BIG)
    shapes = []
    for (name, axis), s in zip(BIG, shards):
        shapes.append((s.shape[0], s.shape[1] * 4) if axis == 1 else (s.shape[0] * 4, s.shape[1]))
    cshape = (conv_shard.shape[0], conv_shard.shape[1] * 4)

    def body(*refs):
        ins = refs[:nt]
        cin = refs[nt]
        outs = refs[nt + 1:2 * nt + 1]
        cout = refs[2 * nt + 1]
        send_sems, recv_sems, loc_sems = refs[2 * nt + 2:]
        x, y, c = _position()
        j0 = 2 * x + y
        waits = []
        local = []
        for t, (_, axis) in enumerate(BIG):
            cp = pltpu.make_async_copy(ins[t], _window(outs[t], shapes[t], axis, slab=j0), loc_sems.at[t])
            cp.start()
            local.append(cp)
        cp = pltpu.make_async_copy(cin, _window(cout, cshape, 1, slab=j0), loc_sems.at[nt])
        cp.start()
        local.append(cp)

        def copy(t, k, src, slab, half, to):
            axis = BIG[t][1]
            return pltpu.make_async_remote_copy(
                src_ref=src, dst_ref=_window(outs[t], shapes[t], axis, slab=slab, half=half),
                send_sem=send_sems.at[t, k], recv_sem=recv_sems.at[t, k], device_id=to, device_id_type=MESH)

        def ccopy(k, slab, to):
            return pltpu.make_async_remote_copy(
                src_ref=cin, dst_ref=_window(cout, cshape, 1, slab=slab),
                send_sem=send_sems.at[nt, k], recv_sem=recv_sems.at[nt, k], device_id=to, device_id_type=MESH)

        sends = []
        for k, (fx, fy) in enumerate(FLIPS):
            to = (x ^ fx, y ^ fy, c)
            for t, (_, axis) in enumerate(BIG):
                src = _window(ins[t], _slab_shape(shapes[t], axis), axis, half=c)
                cp = copy(t, k, src, j0, c, to)
                cp.start()
                sends.append(cp)
            cp = ccopy(k, j0, to)
            cp.start()
            sends.append(cp)
        sib = (x, y, 1 - c)
        for k, (fx, fy) in enumerate(FLIPS):
            js = 2 * (x ^ fx) + (y ^ fy)
            for t, (_, axis) in enumerate(BIG):
                landed = _window(outs[t], shapes[t], axis, slab=js, half=c)
                copy(t, k, landed, js, c, sib).wait_recv()
                cp = copy(t, 3 + k, landed, js, c, sib)
                cp.start()
                sends.append(cp)
            ccopy(k, js, sib).wait_recv()
        for k, (fx, fy) in enumerate(FLIPS):
            js = 2 * (x ^ fx) + (y ^ fy)
            for t, (_, axis) in enumerate(BIG):
                other = _window(outs[t], shapes[t], axis, slab=js, half=1 - c)
                copy(t, 3 + k, other, js, 1 - c, sib).wait_recv()
        for cp in sends:
            cp.wait_send()
        for cp in local:
            cp.wait()

    out_shape = [jax.ShapeDtypeStruct(s, BF16) for s in shapes] + [jax.ShapeDtypeStruct(cshape, F32)]
    return pl.pallas_call(
        body, name="gather_weights", in_specs=[ANY] * (nt + 1), out_specs=[ANY] * (nt + 1), out_shape=out_shape,
        scratch_shapes=[pltpu.SemaphoreType.DMA((nt + 1, 6)), pltpu.SemaphoreType.DMA((nt + 1, 6)),
                        pltpu.SemaphoreType.DMA((nt + 1,))],
    )(*shards, conv_shard)


def _pair_exchange(name, srcs, windows, out_shapes, dtype):
    nt = len(srcs)

    def body(*refs):
        ins, outs = refs[:nt], refs[nt:2 * nt]
        send_sems, recv_sems = refs[2 * nt:]
        x, y, c = _position()
        cps = []
        for t in range(nt):
            cp = pltpu.make_async_remote_copy(
                src_ref=windows[t](ins[t], c), dst_ref=outs[t], send_sem=send_sems.at[t], recv_sem=recv_sems.at[t],
                device_id=(x, y, 1 - c), device_id_type=MESH)
            cp.start()
            cps.append(cp)
        for cp in cps:
            cp.wait()

    return pl.pallas_call(
        body, name=name, in_specs=[ANY] * nt, out_specs=[ANY] * nt,
        out_shape=[jax.ShapeDtypeStruct(s, dtype) for s in out_shapes],
        scratch_shapes=[pltpu.SemaphoreType.DMA((nt,)), pltpu.SemaphoreType.DMA((nt,))],
    )(*srcs)


def _chip_scatter(sbs, shapes):
    nt = len(sbs)

    def body(*refs):
        ins, outs = refs[:nt], refs[nt:2 * nt]
        send_sems, recv_sems = refs[2 * nt:]
        x, y, c = _position()
        cps = []
        for k, (fx, fy) in enumerate(FLIPS):
            js = 2 * (x ^ fx) + (y ^ fy)
            for t, (_, axis) in enumerate(BIG):
                hshape = _half_shape(shapes[t], axis)
                cp = pltpu.make_async_remote_copy(
                    src_ref=_window(ins[t], hshape, axis, slab=js), dst_ref=outs[t].at[k],
                    send_sem=send_sems.at[t, k], recv_sem=recv_sems.at[t, k],
                    device_id=(x ^ fx, y ^ fy, c), device_id_type=MESH)
                cp.start()
                cps.append(cp)
        for cp in cps:
            cp.wait()

    return pl.pallas_call(
        body, name="grad_chip_scatter", in_specs=[ANY] * nt, out_specs=[ANY] * nt,
        out_shape=[jax.ShapeDtypeStruct((3,) + _piece_shape(shapes[t], BIG[t][1]), BF16) for t in range(nt)],
        scratch_shapes=[pltpu.SemaphoreType.DMA((nt, 3)), pltpu.SemaphoreType.DMA((nt, 3))],
    )(*sbs)


def _gather_small(packed):
    R, Cc = packed.shape

    def body(p_ref, o_ref, send_sems, recv_sems, loc_sem):
        x, y, c = _position()
        me = 4 * x + 2 * y + c
        mine = pltpu.make_async_copy(p_ref, o_ref.at[me], loc_sem)
        mine.start()
        cps = []
        for k in range(1, 8):
            fx, fy, fc = (k >> 2) & 1, (k >> 1) & 1, k & 1
            cp = pltpu.make_async_remote_copy(
                src_ref=p_ref, dst_ref=o_ref.at[me], send_sem=send_sems.at[k - 1], recv_sem=recv_sems.at[k - 1],
                device_id=(x ^ fx, y ^ fy, c ^ fc), device_id_type=MESH)
            cp.start()
            cps.append(cp)
        for cp in cps:
            cp.wait()
        mine.wait()

    return pl.pallas_call(
        body, name="gather_small_grads", in_specs=[ANY], out_specs=ANY,
        out_shape=jax.ShapeDtypeStruct((8, R, Cc), F32),
        scratch_shapes=[pltpu.SemaphoreType.DMA((7,)), pltpu.SemaphoreType.DMA((7,)), pltpu.SemaphoreType.DMA],
    )(packed)


def _sum_slots(name, slots):
    n, R, Cc = slots.shape

    def body(s_ref, o_ref):
        t = s_ref[0]
        for i in range(1, n):
            t = t + s_ref[i]
        o_ref[...] = t

    return pl.pallas_call(
        body, name=name, grid=(1,), in_specs=[pl.BlockSpec((n, R, Cc), lambda i: (0, 0, 0))],
        out_specs=pl.BlockSpec((R, Cc), lambda i: (0, 0)), out_shape=jax.ShapeDtypeStruct((R, Cc), F32),
        compiler_params=_params(1),
    )(slots)


def _pair_sum(name, pos, g, land, shape, axis):
    hshape = _half_shape(shape, axis)
    R, Cc = hshape
    tr = _pick(R, (256, 128, 64, 32, 16))
    nrb = R // tr

    def body(pos_ref, g_ref, l_ref, o_ref):
        o_ref[...] = (g_ref[...].astype(F32) + l_ref[...].astype(F32)).astype(BF16)

    if axis == 1:
        gmap = lambda i, p: (p[1] * nrb + i, 0)
    else:
        gmap = lambda i, p: (i, p[1])
    blk = pl.BlockSpec((tr, Cc), lambda i, p: (i, 0))
    return pl.pallas_call(
        body, name=name,
        grid_spec=pltpu.PrefetchScalarGridSpec(
            num_scalar_prefetch=1, grid=(nrb,), in_specs=[pl.BlockSpec((tr, Cc), gmap), blk], out_specs=blk),
        out_shape=jax.ShapeDtypeStruct(hshape, BF16), compiler_params=_params(1),
    )(pos, g, land)


def _chip_sum(name, pos, sb, land, shape, axis):
    hshape = _half_shape(shape, axis)
    pshape = _piece_shape(shape, axis)
    R, Cc = pshape
    tr = _pick(R, (256, 128, 64, 32, 16))
    nrb = R // tr

    def body(pos_ref, s_ref, l_ref, o_ref):
        t = s_ref[...].astype(F32)
        for k in range(3):
            t = t + l_ref[k].astype(F32)
        o_ref[...] = t

    if axis == 1:
        smap = lambda i, p: (i, p[0])
    else:
        smap = lambda i, p: (p[0] * nrb + i, 0)
    return pl.pallas_call(
        body, name=name,
        grid_spec=pltpu.PrefetchScalarGridSpec(
            num_scalar_prefetch=1, grid=(nrb,),
            in_specs=[pl.BlockSpec((tr, Cc), smap), pl.BlockSpec((3, tr, Cc), lambda i, p: (0, i, 0))],
            out_specs=pl.BlockSpec((tr, Cc), lambda i, p: (i, 0))),
        out_shape=jax.ShapeDtypeStruct(pshape, F32), compiler_params=_params(1),
    )(pos, sb, land)


def _adam_math(w, g, m, v):
    m = ADAM_B1 * m + (1.0 - ADAM_B1) * g
    v = ADAM_B2 * v + (1.0 - ADAM_B2) * (g * g)
    m_hat = m / (1.0 - ADAM_B1 ** ADAM_STEP)
    v_hat = v / (1.0 - ADAM_B2 ** ADAM_STEP)
    delta = -ADAM_LR * (m_hat / (jnp.sqrt(v_hat) + ADAM_EPS) + ADAM_WD * w)
    return delta, m, v


def _adamw_halves(name, pos, w, m, v, mine, theirs, axis):
    R, Cc = w.shape
    hr, hc = mine.shape
    tr = _pick(hr, (256, 128, 64, 32, 16))
    nrb = hr // tr

    def body(pos_ref, w_ref, m_ref, v_ref, a_ref, b_ref, g_ref, d_ref, nm_ref, nv_ref):
        half = pl.program_id(0)
        g = jnp.where(half == pos_ref[1], a_ref[...], b_ref[...])
        d, nm, nv = _adam_math(w_ref[...], g, m_ref[...], v_ref[...])
        g_ref[...] = g
        d_ref[...] = d
        nm_ref[...] = nm
        nv_ref[...] = nv

    if axis == 1:
        wmap = lambda h, i, p: (h * nrb + i, 0)
    else:
        wmap = lambda h, i, p: (i, h)
    wblk = pl.BlockSpec((tr, hc), wmap)
    hblk = pl.BlockSpec((tr, hc), lambda h, i, p: (i, 0))
    return pl.pallas_call(
        body, name=name,
        grid_spec=pltpu.PrefetchScalarGridSpec(
            num_scalar_prefetch=1, grid=(2, nrb), in_specs=[wblk, wblk, wblk, hblk, hblk], out_specs=[wblk] * 4),
        out_shape=[jax.ShapeDtypeStruct((R, Cc), F32)] * 4, compiler_params=_params(2),
    )(pos, w, m, v, mine, theirs)


def _adamw_small(name, w, g, m, v):
    def body(w_ref, g_ref, m_ref, v_ref, d_ref, nm_ref, nv_ref):
        d, nm, nv = _adam_math(w_ref[...], g_ref[...], m_ref[...], v_ref[...])
        d_ref[...] = d
        nm_ref[...] = nm
        nv_ref[...] = nv

    blk = pl.BlockSpec(w.shape, lambda i: (0, 0))
    return pl.pallas_call(
        body, name=name, grid=(1,), in_specs=[blk] * 4, out_specs=[blk] * 3,
        out_shape=[jax.ShapeDtypeStruct(w.shape, F32)] * 3, compiler_params=_params(1),
    )(w, g, m, v)


SMALL = ("ffn1_norm_g", "mix_norm_g", "conv_b_dw", "conv_ln_g", "conv_ln_b", "q_norm_g", "k_norm_g", "ffn2_norm_g")
ORDER = ("ffn1_norm_g", "ffn1_w_gate", "ffn1_w_up", "ffn1_w_down", "mix_norm_g", "w_in", "conv_w_dw", "conv_b_dw",
         "conv_ln_g", "conv_ln_b", "q_norm_g", "k_norm_g", "w_out", "ffn2_norm_g", "ffn2_w_gate", "ffn2_w_up",
         "ffn2_w_down")


def kernel(x, ffn1_norm_g, ffn1_w_gate, ffn1_w_up, ffn1_w_down, mix_norm_g, w_in, conv_w_dw, conv_b_dw, conv_ln_g, conv_ln_b, q_norm_g, k_norm_g, w_out, ffn2_norm_g, ffn2_w_gate, ffn2_w_up, ffn2_w_down, loss_target, m_ffn1_norm_g, m_ffn1_w_gate, m_ffn1_w_up, m_ffn1_w_down, m_mix_norm_g, m_w_in, m_conv_w_dw, m_conv_b_dw, m_conv_ln_g, m_conv_ln_b, m_q_norm_g, m_k_norm_g, m_w_out, m_ffn2_norm_g, m_ffn2_w_gate, m_ffn2_w_up, m_ffn2_w_down, v_ffn1_norm_g, v_ffn1_w_gate, v_ffn1_w_up, v_ffn1_w_down, v_mix_norm_g, v_w_in, v_conv_w_dw, v_conv_b_dw, v_conv_ln_g, v_conv_ln_b, v_q_norm_g, v_k_norm_g, v_w_out, v_ffn2_norm_g, v_ffn2_w_gate, v_ffn2_w_up, v_ffn2_w_down):
    args = dict(locals())
    P = {n: args[n] for n in ORDER}
    Mo = {n: args["m_" + n] for n in ORDER}
    Vo = {n: args["v_" + n] for n in ORDER}
    xs = x[0]
    tgt = loss_target[0]
    T, D = xs.shape
    hd = q_norm_g.shape[-1]
    C = conv_b_dw.shape[-1]
    cx, cy, cc = _position()
    j0 = 2 * cx + cy
    pos = jnp.stack([j0, cc]).astype(jnp.int32)

    shards = [P[n][0].astype(BF16) for n, _ in BIG]
    conv_shard = jnp.pad(conv_w_dw[0], ((0, HALO - conv_w_dw.shape[1]), (0, 0)))
    gathered = _gather_weights(shards, conv_shard)
    W = {n: g for (n, _), g in zip(BIG, gathered[:-1])}
    W["conv_w32"] = gathered[-1]
    for n in SMALL:
        W[n] = P[n]

    lossvec, dx0, G = _local_step(xs, tgt, W, hd)
    loss = lax.psum(0.5 / D * jnp.sum(lossvec), AXES)

    shapes = [W[n].shape for n, _ in BIG]
    gs = [G[n] for n, _ in BIG]
    to_sibling = [
        (lambda ref, c, s=s, a=a: _window(ref, s, a, half=1 - c)) for s, (_, a) in zip(shapes, BIG)]
    landed = _pair_exchange("grad_pair_exchange", gs, to_sibling,
                            [_half_shape(s, a) for s, (_, a) in zip(shapes, BIG)], BF16)
    sbs = [_pair_sum("pair_sum_" + n, pos, g, l, s, a) for (n, a), g, l, s in zip(BIG, gs, landed, shapes)]
    pieces = _chip_scatter(sbs, shapes)
    mine = [_chip_sum("chip_sum_" + n, pos, sb, l, s, a) for (n, a), sb, l, s in zip(BIG, sbs, pieces, shapes)]
    whole = [(lambda ref, c: ref)] * len(BIG)
    theirs = _pair_exchange("grad_half_exchange", mine, whole, [m.shape for m in mine], F32)

    grads, deltas, new_m, new_v = {}, {}, {}, {}
    for (n, a), mi, th in zip(BIG, mine, theirs):
        g, d, nm, nv = _adamw_halves("adamw_" + n, pos, P[n][0], Mo[n][0], Vo[n][0], mi, th, a)
        grads[n], deltas[n], new_m[n], new_v[n] = g[None], d[None], nm[None], nv[None]

    rows = [G["conv_w32"]]
    for n in ("ffn1_norm_g", "mix_norm_g", "ffn2_norm_g"):
        rows.append(G[n].reshape(D // C, C))
    for n in ("conv_b_dw", "conv_ln_g", "conv_ln_b", "q_norm_g", "k_norm_g"):
        rows.append(G[n])
    packed = jnp.concatenate(rows, axis=0)
    used = packed.shape[0]
    packed = jnp.pad(packed, ((0, -used % 8), (0, 0)))
    total = _sum_slots("sum_small_grads", _gather_small(packed))
    r = HALO
    small_g = {}
    cq = C // 4
    small_g["conv_w_dw"] = lax.dynamic_slice(total[:conv_w_dw.shape[1]], (0, j0 * cq), (conv_w_dw.shape[1], cq))
    for n in ("ffn1_norm_g", "mix_norm_g", "ffn2_norm_g"):
        small_g[n] = total[r:r + D // C].reshape(1, D)
        r += D // C
    for n in ("conv_b_dw", "conv_ln_g", "conv_ln_b"):
        small_g[n] = total[r:r + 1]
        r += 1
    for n in ("q_norm_g", "k_norm_g"):
        small_g[n] = total[r:r + 1, :hd]
        r += 1
    for n in ("conv_w_dw",) + SMALL:
        w2 = P[n][0] if n == "conv_w_dw" else P[n]
        m2 = Mo[n][0] if n == "conv_w_dw" else Mo[n]
        v2 = Vo[n][0] if n == "conv_w_dw" else Vo[n]
        d, nm, nv = _adamw_small("adamw_" + n, w2, small_g[n], m2, v2)
        if n == "conv_w_dw":
            grads[n], deltas[n], new_m[n], new_v[n] = small_g[n][None], d[None], nm[None], nv[None]
        else:
            grads[n], deltas[n], new_m[n], new_v[n] = small_g[n], d, nm, nv

    return (loss, dx0[None], *[grads[n] for n in ORDER], *[deltas[n] for n in ORDER],
            *[new_m[n] for n in ORDER], *[new_v[n] for n in ORDER])
```

```python
import jax
import jax.numpy as jnp
from jax import lax
from jax.experimental import pallas as pl
from jax.experimental.pallas import tpu as pltpu

F32 = jnp.float32
BF16 = jnp.bfloat16
EPS = 1e-6
WINDOW = 128
DILATIONS = (1, 4, 16)
ALIBI_MAX_BIAS = 8.0
LANES = 128
HALO = 32
ADAM_LR, ADAM_B1, ADAM_B2, ADAM_EPS, ADAM_WD, ADAM_STEP = 0.001, 0.9, 0.999, 1e-08, 0.01, 10
VMEM_LIMIT_MB = 56
MESH = pl.DeviceIdType.MESH
ANY = pl.BlockSpec(memory_space=pl.ANY)
AXES = ("x", "y", "c")
NEG = -1e30


def _pick(n, cands):
    for c in cands:
        if n % c == 0:
            return c
    return n


def _params(nsem):
    return pltpu.CompilerParams(dimension_semantics=("arbitrary",) * nsem,
                                vmem_limit_bytes=VMEM_LIMIT_MB << 20)


def _nn(a, b):
    return jnp.dot(a, b, preferred_element_type=F32)


def _nt(a, b):
    return lax.dot_general(a, b, (((1,), (1,)), ((), ())), preferred_element_type=F32)


def _tn(a, b):
    return lax.dot_general(a, b, (((0,), (0,)), ((), ())), preferred_element_type=F32)


def _sigmoid(v):
    return jax.nn.sigmoid(v)


def _rms_r(xv):
    return lax.rsqrt(jnp.mean(xv * xv, axis=-1, keepdims=True) + EPS)


def _norm_matmul(name, x, g, ws, swiglu):
    T, D = x.shape
    N = ws[0].shape[1]
    tm = _pick(T, (512, 256, 128))
    tn = _pick(N, (512, 256, 128))
    nw = len(ws)

    def body(*refs):
        x_ref, g_ref = refs[:2]
        w_refs = refs[2:2 + nw]
        outs = refs[2 + nw:-1]
        hs = refs[-1]

        @pl.when(pl.program_id(1) == 0)
        def _():
            xv = x_ref[...]
            hv = (xv * _rms_r(xv) * g_ref[...]).astype(BF16)
            hs[...] = hv
            outs[0][...] = hv

        h = hs[...]
        if swiglu:
            gt = _nn(h, w_refs[0][...])
            u = _nn(h, w_refs[1][...])
            outs[1][...] = gt.astype(BF16)
            outs[2][...] = u.astype(BF16)
            outs[3][...] = (gt * _sigmoid(gt) * u).astype(BF16)
        else:
            outs[1][...] = _nn(h, w_refs[0][...])

    row = pl.BlockSpec((tm, D), lambda i, j: (i, 0))
    col = pl.BlockSpec((D, tn), lambda i, j: (0, j))
    tile = pl.BlockSpec((tm, tn), lambda i, j: (i, j))
    if swiglu:
        out_shape = [jax.ShapeDtypeStruct((T, D), BF16)] + [jax.ShapeDtypeStruct((T, N), BF16)] * 3
        out_specs = [row, tile, tile, tile]
    else:
        out_shape = [jax.ShapeDtypeStruct((T, D), BF16), jax.ShapeDtypeStruct((T, N), F32)]
        out_specs = [row, tile]
    return pl.pallas_call(
        body, name=name, grid=(T // tm, N // tn),
        in_specs=[row, pl.BlockSpec((1, D), lambda i, j: (0, 0))] + [col] * nw,
        out_specs=out_specs, out_shape=out_shape,
        scratch_shapes=[pltpu.VMEM((tm, D), BF16)],
        compiler_params=_params(2),
    )(x, g, *ws)


def _matmul_res(name, a, w, res, scale, tgt=None):
    T, K = a.shape
    N = w.shape[1]
    tm = _pick(T, (512, 256, 128))
    tk = _pick(K, (1408, 1024, 512, 256, 128))
    nk = K // tk
    loss = tgt is not None

    def body(*refs):
        if loss:
            a_ref, w_ref, res_ref, tgt_ref, dx_ref, dxb_ref, lv_ref, acc = refs
        else:
            a_ref, w_ref, res_ref, out_ref, acc = refs
        i, k = pl.program_id(0), pl.program_id(1)

        @pl.when(k == 0)
        def _():
            acc[...] = jnp.zeros_like(acc)

        acc[...] += _nn(a_ref[...], w_ref[...])

        @pl.when(k == nk - 1)
        def _():
            val = res_ref[...] + scale * acc[...]
            if loss:
                dv = val - tgt_ref[...]
                dx = dv * (1.0 / N)
                dx_ref[...] = dx
                dxb_ref[...] = dx.astype(BF16)
                part = jnp.sum(dv * dv, axis=0, keepdims=True)

                @pl.when(i == 0)
                def _():
                    lv_ref[...] = part

                @pl.when(i > 0)
                def _():
                    lv_ref[...] += part
            else:
                out_ref[...] = val

    row = pl.BlockSpec((tm, N), lambda i, k: (i, 0))
    in_specs = [pl.BlockSpec((tm, tk), lambda i, k: (i, k)), pl.BlockSpec((tk, N), lambda i, k: (k, 0)), row]
    args = [a, w, res]
    if loss:
        in_specs.append(row)
        args.append(tgt)
        out_specs = [row, row, pl.BlockSpec((1, N), lambda i, k: (0, 0))]
        out_shape = [jax.ShapeDtypeStruct((T, N), F32), jax.ShapeDtypeStruct((T, N), BF16),
                     jax.ShapeDtypeStruct((1, N), F32)]
    else:
        out_specs = row
        out_shape = jax.ShapeDtypeStruct((T, N), F32)
    return pl.pallas_call(
        body, name=name, grid=(T // tm, nk), in_specs=in_specs, out_specs=out_specs, out_shape=out_shape,
        scratch_shapes=[pltpu.VMEM((tm, N), F32)], compiler_params=_params(2),
    )(*args)


def _nt_matmul(name, dyb, w, scale=1.0, gate=None, up=None):
    T, D = dyb.shape
    N = w.shape[0]
    tm = _pick(T, (512, 256, 128))
    tn = _pick(N, (512, 256, 128))
    swiglu = gate is not None

    def body(*refs):
        if swiglu:
            dy_ref, w_ref, g_ref, u_ref, dg_ref, du_ref = refs
        else:
            dy_ref, w_ref, o_ref, ob_ref = refs
        da = _nt(dy_ref[...], w_ref[...]) * scale
        if swiglu:
            gt = g_ref[...].astype(F32)
            u = u_ref[...].astype(F32)
            sg = _sigmoid(gt)
            dg_ref[...] = (da * u * (sg * (1.0 + gt * (1.0 - sg)))).astype(BF16)
            du_ref[...] = (da * (gt * sg)).astype(BF16)
        else:
            o_ref[...] = da
            ob_ref[...] = da.astype(BF16)

    tile = pl.BlockSpec((tm, tn), lambda i, j: (i, j))
    in_specs = [pl.BlockSpec((tm, D), lambda i, j: (i, 0)), pl.BlockSpec((tn, D), lambda i, j: (j, 0))]
    args = [dyb, w]
    if swiglu:
        in_specs += [tile, tile]
        args += [gate, up]
        out_shape = [jax.ShapeDtypeStruct((T, N), BF16)] * 2
    else:
        out_shape = [jax.ShapeDtypeStruct((T, N), F32), jax.ShapeDtypeStruct((T, N), BF16)]
    return pl.pallas_call(
        body, name=name, grid=(T // tm, N // tn), in_specs=in_specs, out_specs=[tile, tile],
        out_shape=out_shape, compiler_params=_params(2),
    )(*args)


def _nt_rms_bwd(name, As, Ws, x, g, dres):
    T, K = As[0].shape
    D = x.shape[1]
    tm = _pick(T, (512, 256, 128))
    tk = _pick(K, (512, 256, 128))
    nk = K // tk
    na = len(As)

    def body(*refs):
        a_refs = refs[:na]
        w_refs = refs[na:2 * na]
        x_ref, g_ref, dres_ref, dx_ref, dxb_ref, dg_ref, acc = refs[2 * na:]
        i, k = pl.program_id(0), pl.program_id(1)

        @pl.when(k == 0)
        def _():
            acc[...] = jnp.zeros_like(acc)

        for a_ref, w_ref in zip(a_refs, w_refs):
            acc[...] += _nt(a_ref[...], w_ref[...])

        @pl.when(k == nk - 1)
        def _():
            dh = acc[...]
            xv = x_ref[...]
            r = _rms_r(xv)
            gd = dh * g_ref[...]
            dx = dres_ref[...] + r * gd - xv * (r * r * r) * jnp.mean(gd * xv, axis=-1, keepdims=True)
            dx_ref[...] = dx
            dxb_ref[...] = dx.astype(BF16)
            part = jnp.sum(dh * xv * r, axis=0, keepdims=True)

            @pl.when(i == 0)
            def _():
                dg_ref[...] = part

            @pl.when(i > 0)
            def _():
                dg_ref[...] += part

    row = pl.BlockSpec((tm, D), lambda i, k: (i, 0))
    vec = pl.BlockSpec((1, D), lambda i, k: (0, 0))
    return pl.pallas_call(
        body, name=name, grid=(T // tm, nk),
        in_specs=[pl.BlockSpec((tm, tk), lambda i, k: (i, k))] * na
        + [pl.BlockSpec((D, tk), lambda i, k: (0, k))] * na + [row, vec, row],
        out_specs=[row, row, vec],
        out_shape=[jax.ShapeDtypeStruct((T, D), F32), jax.ShapeDtypeStruct((T, D), BF16),
                   jax.ShapeDtypeStruct((1, D), F32)],
        scratch_shapes=[pltpu.VMEM((tm, D), F32)], compiler_params=_params(2),
    )(*As, *Ws, x, g, dres)


def _tn_matmul(name, a, b, scale=1.0):
    T, M = a.shape
    N = b.shape[1]
    tm = _pick(M, (1024, 512, 256, 128))
    tn = _pick(N, (1408, 1280, 1024, 512, 256, 128))
    tk = _pick(T, (1024, 512, 256, 128))
    nk = T // tk

    def body(a_ref, b_ref, o_ref, acc):
        k = pl.program_id(2)

        @pl.when(k == 0)
        def _():
            acc[...] = jnp.zeros_like(acc)

        acc[...] += _tn(a_ref[...], b_ref[...])

        @pl.when(k == nk - 1)
        def _():
            o_ref[...] = (acc[...] * scale).astype(BF16)

    return pl.pallas_call(
        body, name=name, grid=(M // tm, N // tn, nk),
        in_specs=[pl.BlockSpec((tk, tm), lambda i, j, k: (k, i)), pl.BlockSpec((tk, tn), lambda i, j, k: (k, j))],
        out_specs=pl.BlockSpec((tm, tn), lambda i, j, k: (i, j)),
        out_shape=jax.ShapeDtypeStruct((M, N), BF16),
        scratch_shapes=[pltpu.VMEM((tm, tn), F32)], compiler_params=_params(3),
    )(a, b)


CONV_ROWS = 128
ROW_CHUNK = 32
LANE_CHUNK = 256


def _conv_fwd(name, z, w32, b, lg, lb, C):
    T = z.shape[0]
    tc = CONV_ROWS
    ntap = 31
    lc = _pick(C, (LANE_CHUNK, LANES))
    rpb = tc // HALO

    def body(zc_ref, zp_ref, w_ref, b_ref, lg_ref, lb_ref, yc_ref, ycv_ref, vbuf, ybuf):
        i = pl.program_id(0)
        zc = zc_ref[...]
        zp = zp_ref[...]
        vbuf[HALO:HALO + tc, :] = zc[:, :C] * _sigmoid(zc[:, C:])
        vbuf[0:HALO, :] = jnp.where(i > 0, zp[:, :C] * _sigmoid(zp[:, C:]), 0.0)
        for r0 in range(0, tc, ROW_CHUNK):
            for c0 in range(0, C, lc):
                acc = jnp.zeros((ROW_CHUNK, lc), F32) + b_ref[:, c0:c0 + lc]
                for k in range(ntap):
                    s = r0 + 2 + k
                    acc = acc + w_ref[k:k + 1, c0:c0 + lc] * vbuf[s:s + ROW_CHUNK, c0:c0 + lc]
                ybuf[r0:r0 + ROW_CHUNK, c0:c0 + lc] = acc
        y = ybuf[...]
        ycv_ref[...] = y
        mu = jnp.mean(y, axis=-1, keepdims=True)
        yc = y - mu
        rstd = lax.rsqrt(jnp.mean(yc * yc, axis=-1, keepdims=True) + EPS)
        ln = yc * rstd * lg_ref[...] + lb_ref[...]
        yc_ref[...] = (ln * _sigmoid(ln)).astype(BF16)

    vec = pl.BlockSpec((1, C), lambda i: (0, 0))
    return pl.pallas_call(
        body, name=name, grid=(T // tc,),
        in_specs=[pl.BlockSpec((tc, 2 * C), lambda i: (i, 0)),
                  pl.BlockSpec((HALO, 2 * C), lambda i: (jnp.maximum(i * rpb - 1, 0), 0)),
                  pl.BlockSpec((HALO, C), lambda i: (0, 0)), vec, vec, vec],
        out_specs=[pl.BlockSpec((tc, C), lambda i: (i, 0))] * 2,
        out_shape=[jax.ShapeDtypeStruct((T, C), BF16), jax.ShapeDtypeStruct((T, C), F32)],
        scratch_shapes=[pltpu.VMEM((tc + HALO, C), F32), pltpu.VMEM((tc, C), F32)],
        compiler_params=_params(1),
    )(z, z, w32, b, lg, lb)


def _conv_bwd(name, z, ycv, dycat, w32, lg, lb, C):
    T = z.shape[0]
    tc = CONV_ROWS
    ntap = 31
    lc = _pick(C, (LANE_CHUNK, LANES))
    rpb = tc // HALO
    nstep = T // tc
    nhb = T // HALO

    def ln_bwd(dyc, y, lgv, lbv):
        mu = jnp.mean(y, axis=-1, keepdims=True)
        yc = y - mu
        rstd = lax.rsqrt(jnp.mean(yc * yc, axis=-1, keepdims=True) + EPS)
        yn = yc * rstd
        ln = yn * lgv + lbv
        sg = _sigmoid(ln)
        dln = dyc * (sg * (1.0 + ln * (1.0 - sg)))
        dyn = dln * lgv
        dy = rstd * (dyn - jnp.mean(dyn, axis=-1, keepdims=True)
                     - yn * jnp.mean(dyn * yn, axis=-1, keepdims=True))
        return dy, dln, yn

    def body(zc_ref, zp_ref, y_ref, yn_ref, d_ref, dn_ref, w_ref, lg_ref, lb_ref,
             dz_ref, dw_ref, db_ref, dlg_ref, dlb_ref, vbuf, dbuf, dvbuf, dwacc):
        i = pl.program_id(0)
        lgv, lbv = lg_ref[...], lb_ref[...]
        zc = zc_ref[...]
        zp = zp_ref[...]
        a = zc[:, :C]
        sgt = _sigmoid(zc[:, C:])
        vbuf[HALO:HALO + tc, :] = a * sgt
        vbuf[0:HALO, :] = jnp.where(i > 0, zp[:, :C] * _sigmoid(zp[:, C:]), 0.0)
        dy, dln, yn = ln_bwd(d_ref[...], y_ref[...], lgv, lbv)
        dbuf[0:tc, :] = dy
        dyn_, _, _ = ln_bwd(dn_ref[...], yn_ref[...], lgv, lbv)
        dbuf[tc:tc + HALO, :] = jnp.where(i < nstep - 1, dyn_, 0.0)

        @pl.when(i == 0)
        def _():
            dwacc[...] = jnp.zeros_like(dwacc)
            db_ref[...] = jnp.zeros_like(db_ref)
            dlg_ref[...] = jnp.zeros_like(dlg_ref)
            dlb_ref[...] = jnp.zeros_like(dlb_ref)

        db_ref[...] += jnp.sum(dy, axis=0, keepdims=True)
        dlg_ref[...] += jnp.sum(dln * yn, axis=0, keepdims=True)
        dlb_ref[...] += jnp.sum(dln, axis=0, keepdims=True)

        for r0 in range(0, tc, ROW_CHUNK):
            for c0 in range(0, C, lc):
                dcur = dbuf[r0:r0 + ROW_CHUNK, c0:c0 + lc]
                acc = jnp.zeros((ROW_CHUNK, lc), F32)
                for k in range(ntap):
                    s = r0 + 30 - k
                    acc = acc + w_ref[k:k + 1, c0:c0 + lc] * dbuf[s:s + ROW_CHUNK, c0:c0 + lc]
                    sv = r0 + 2 + k
                    prod = dcur * vbuf[sv:sv + ROW_CHUNK, c0:c0 + lc]
                    red = prod[0:8]
                    for q in range(8, ROW_CHUNK, 8):
                        red = red + prod[q:q + 8]
                    dwacc[8 * k:8 * k + 8, c0:c0 + lc] += red
                dvbuf[r0:r0 + ROW_CHUNK, c0:c0 + lc] = acc
        dv = dvbuf[...]
        dz_ref[:, :C] = (dv * sgt).astype(BF16)
        dz_ref[:, C:] = (dv * a * sgt * (1.0 - sgt)).astype(BF16)

        @pl.when(i == nstep - 1)
        def _():
            for k in range(ntap):
                dw_ref[k:k + 1, :] = jnp.sum(dwacc[8 * k:8 * k + 8, :], axis=0, keepdims=True)
            dw_ref[ntap:HALO, :] = jnp.zeros((HALO - ntap, C), F32)

    vec = pl.BlockSpec((1, C), lambda i: (0, 0))
    cur = pl.BlockSpec((tc, C), lambda i: (i, 0))
    nxt = pl.BlockSpec((HALO, C), lambda i: (jnp.minimum((i + 1) * rpb, nhb - 1), 0))
    return pl.pallas_call(
        body, name=name, grid=(nstep,),
        in_specs=[pl.BlockSpec((tc, 2 * C), lambda i: (i, 0)),
                  pl.BlockSpec((HALO, 2 * C), lambda i: (jnp.maximum(i * rpb - 1, 0), 0)),
                  cur, nxt, cur, nxt, pl.BlockSpec((HALO, C), lambda i: (0, 0)), vec, vec],
        out_specs=[pl.BlockSpec((tc, 2 * C), lambda i: (i, 0)), pl.BlockSpec((HALO, C), lambda i: (0, 0)),
                   vec, vec, vec],
        out_shape=[jax.ShapeDtypeStruct((T, 2 * C), BF16), jax.ShapeDtypeStruct((HALO, C), F32),
                   jax.ShapeDtypeStruct((1, C), F32), jax.ShapeDtypeStruct((1, C), F32),
                   jax.ShapeDtypeStruct((1, C), F32)],
        scratch_shapes=[pltpu.VMEM((tc + HALO, C), F32), pltpu.VMEM((tc + HALO, C), F32),
                        pltpu.VMEM((tc, C), F32), pltpu.VMEM((8 * HALO, C), F32)],
        compiler_params=_params(1),
    )(z, z, ycv, ycv, dycat, dycat, w32, lg, lb)


def _seg_sum(u, bmat):
    hi = u.astype(BF16)
    lo = (u - hi.astype(F32)).astype(BF16)
    return _nn(hi, bmat) + _nn(lo, bmat)


def _attn_prep(name, z, gq, gk, bmat, A, c0, hd):
    T = z.shape[0]
    tm = _pick(T, (256, 128))

    def body(zq_ref, zk_ref, zv_ref, gq_ref, gk_ref, b_ref, o_ref):
        bm = b_ref[...]
        for idx, (z_ref, g_ref) in enumerate(((zq_ref, gq_ref), (zk_ref, gk_ref))):
            zv = z_ref[...]
            r = lax.rsqrt(_seg_sum(zv * zv, bm) * (1.0 / hd) + EPS)
            o_ref[:, idx * A:(idx + 1) * A] = (zv * r * g_ref[...]).astype(BF16)
        o_ref[:, 2 * A:] = zv_ref[...].astype(BF16)

    vec = pl.BlockSpec((1, A), lambda i: (0, 0))
    return pl.pallas_call(
        body, name=name, grid=(T // tm,),
        in_specs=[pl.BlockSpec((tm, A), lambda i: (i, c0)), pl.BlockSpec((tm, A), lambda i: (i, c0 + 1)),
                  pl.BlockSpec((tm, A), lambda i: (i, c0 + 2)), vec, vec,
                  pl.BlockSpec((A, A), lambda i: (0, 0))],
        out_specs=pl.BlockSpec((tm, 3 * A), lambda i: (i, 0)),
        out_shape=jax.ShapeDtypeStruct((T, 3 * A), BF16), compiler_params=_params(1),
    )(z, z, z, gq, gk, bmat)


def _attn_geometry(T, d, A):
    L = T // d
    Lc = min(L, 1024)
    return L, Lc, Lc // WINDOW, L // Lc, A // LANES


QK_SCALE = 0.125
ATTN_UNROLL = 2


def _fill_bias(bias, sl_ref, hp, d):
    qi = lax.broadcasted_iota(jnp.int32, (WINDOW, 2 * WINDOW), 0)
    kj = lax.broadcasted_iota(jnp.int32, (WINDOW, 2 * WINDOW), 1)
    dist = WINDOW + qi - kj
    inband = (dist >= 0) & (dist <= WINDOW)
    distf = dist.astype(F32)
    for hh in range(2):
        b = jnp.where(inband, -(sl_ref[2 * hp + hh] * d) * distf, NEG)
        bias[2 * hh + 1] = b
        bias[2 * hh] = jnp.where(kj >= WINDOW, b, NEG)


def _attn_fwd(name, qkv, slopes, d, A):
    T = qkv.shape[0]
    L, Lc, nb, nch, hpn = _attn_geometry(T, d, A)
    cpr = 3 * hpn
    qv = qkv.reshape(L, d * 3 * A)

    def body(sl_ref, q_ref, k_ref, kh_ref, v_ref, vh_ref, o_ref, l_ref, kbuf, vbuf, bias):
        cb, ch = pl.program_id(0), pl.program_id(1)
        hp = cb % hpn
        kbuf[0:WINDOW, :] = kh_ref[...]
        kbuf[WINDOW:, :] = k_ref[...]
        vbuf[0:WINDOW, :] = vh_ref[...]
        vbuf[WINDOW:, :] = v_ref[...]
        _fill_bias(bias, sl_ref, hp, d)
        lane = lax.broadcasted_iota(jnp.int32, (1, LANES), 1)
        first = lane < (LANES // 2)

        def step(nl, carry):
            r0 = pl.multiple_of(nl * WINDOW, WINDOW)
            later = jnp.where(ch * nb + nl > 0, 1, 0)
            qb = q_ref[pl.ds(r0, WINDOW), :]
            k2 = kbuf[pl.ds(r0, 2 * WINDOW), :]
            v2 = vbuf[pl.ds(r0, 2 * WINDOW), :]
            res = []
            for hh in range(2):
                mh = first if hh == 0 else jnp.logical_not(first)
                s = _nt(jnp.where(mh, qb, jnp.zeros_like(qb)), k2) + bias[2 * hh + later]
                mx = jnp.max(s, axis=-1, keepdims=True)
                p = jnp.exp(s - mx)
                den = jnp.sum(p, axis=-1, keepdims=True)
                res.append((_nn(p.astype(BF16), v2) / den, mx + jnp.log(den)))
            o_ref[pl.ds(r0, WINDOW), :] = jnp.where(first, res[0][0], res[1][0])
            l_ref[pl.ds(r0, WINDOW), :] = jnp.where(first, res[0][1], res[1][1])
            return carry

        lax.fori_loop(0, nb, step, 0, unroll=min(ATTN_UNROLL, nb))

    def cmap(which):
        return lambda cb, ch: (ch, (cb // hpn) * cpr + which * hpn + cb % hpn)

    def hmap(which):
        return lambda cb, ch: (jnp.maximum(ch * nb - 1, 0), (cb // hpn) * cpr + which * hpn + cb % hpn)

    blk = lambda m: pl.BlockSpec((Lc, LANES), m)
    hblk = lambda m: pl.BlockSpec((WINDOW, LANES), m)
    omap = lambda cb, ch: (ch, cb)
    o, l = pl.pallas_call(
        body, name=name, grid=(d * hpn, nch),
        in_specs=[pl.BlockSpec(memory_space=pltpu.SMEM), blk(cmap(0)), blk(cmap(1)), hblk(hmap(1)),
                  blk(cmap(2)), hblk(hmap(2))],
        out_specs=[blk(omap), blk(omap)],
        out_shape=[jax.ShapeDtypeStruct((L, d * A), F32)] * 2,
        scratch_shapes=[pltpu.VMEM((Lc + WINDOW, LANES), BF16)] * 2
        + [pltpu.VMEM((4, WINDOW, 2 * WINDOW), F32)], compiler_params=_params(2),
    )(slopes, qv, qv, qv, qv, qv)
    return o.reshape(T, A), l.reshape(T, A)


def _attn_combine(name, os_, ls_):
    T, A = os_[0].shape
    tm = _pick(T, (512, 256, 128))
    nbr = len(os_)

    def body(*refs):
        o_refs, l_refs = refs[:nbr], refs[nbr:2 * nbr]
        y_ref, lg_ref = refs[2 * nbr:]
        ls = [r[...] for r in l_refs]
        mx = ls[0]
        for v in ls[1:]:
            mx = jnp.maximum(mx, v)
        es = [jnp.exp(v - mx) for v in ls]
        den = es[0]
        for e in es[1:]:
            den = den + e
        out = es[0] * o_refs[0][...]
        for e, o_ref in zip(es[1:], o_refs[1:]):
            out = out + e * o_ref[...]
        y_ref[...] = (out / den).astype(BF16)
        lg_ref[...] = mx + jnp.log(den)

    blk = pl.BlockSpec((tm, A), lambda i: (i, 0))
    return pl.pallas_call(
        body, name=name, grid=(T // tm,), in_specs=[blk] * (2 * nbr), out_specs=[blk, blk],
        out_shape=[jax.ShapeDtypeStruct((T, A), BF16), jax.ShapeDtypeStruct((T, A), F32)],
        compiler_params=_params(1),
    )(*os_, *ls_)


def _attn_bwd(name, qkv, dycatb, yatt, lg, slopes, d, A, catw, catoff):
    T = qkv.shape[0]
    L, Lc, nb, nch, hpn = _attn_geometry(T, d, A)
    cpr = 3 * hpn
    qv = qkv.reshape(L, d * 3 * A)
    dov = dycatb.reshape(L, d * catw)
    ov = yatt.reshape(L, d * A)
    lv = lg.reshape(L, d * A)
    cw, co = catw // LANES, catoff // LANES

    def body(sl_ref, q_ref, k_ref, kh_ref, v_ref, vh_ref, do_ref, o_ref, l_ref,
             dq_ref, dk_ref, dv_ref, kbuf, vbuf, bias):
        cb, ch = pl.program_id(0), pl.program_id(1)
        hp = cb % hpn
        kbuf[0:WINDOW, :] = kh_ref[...]
        kbuf[WINDOW:, :] = k_ref[...]
        vbuf[0:WINDOW, :] = vh_ref[...]
        vbuf[WINDOW:, :] = v_ref[...]
        _fill_bias(bias, sl_ref, hp, d)
        lane = lax.broadcasted_iota(jnp.int32, (1, LANES), 1)
        first = lane < (LANES // 2)

        @pl.when(ch == 0)
        def _():
            dk_ref[...] = jnp.zeros_like(dk_ref)
            dv_ref[...] = jnp.zeros_like(dv_ref)

        def step(nl, carry):
            r0 = pl.multiple_of(nl * WINDOW, WINDOW)
            n = ch * nb + nl
            later = jnp.where(n > 0, 1, 0)
            qb = q_ref[pl.ds(r0, WINDOW), :]
            k2 = kbuf[pl.ds(r0, 2 * WINDOW), :]
            v2 = vbuf[pl.ds(r0, 2 * WINDOW), :]
            dob = do_ref[pl.ds(r0, WINDOW), :]
            dd = dob.astype(F32) * o_ref[pl.ds(r0, WINDOW), :].astype(F32)
            lb = l_ref[pl.ds(r0, WINDOW), :]
            dk2 = jnp.zeros((2 * WINDOW, LANES), F32)
            dv2 = jnp.zeros((2 * WINDOW, LANES), F32)
            dqs = []
            for hh in range(2):
                mh = first if hh == 0 else jnp.logical_not(first)
                qh = jnp.where(mh, qb, jnp.zeros_like(qb))
                doh = jnp.where(mh, dob, jnp.zeros_like(dob))
                lcol = lb[:, hh * (LANES // 2):hh * (LANES // 2) + 1]
                p = jnp.exp(_nt(qh, k2) + bias[2 * hh + later] - lcol)
                dcol = jnp.sum(jnp.where(mh, dd, 0.0), axis=-1, keepdims=True)
                ds = (p * (_nt(doh, v2) - dcol)).astype(BF16)
                dqs.append(_nn(ds, k2))
                dk2 = dk2 + _tn(ds, qh)
                dv2 = dv2 + _tn(p.astype(BF16), doh)
            dq_ref[pl.ds(r0, WINDOW), :] = jnp.where(first, dqs[0], dqs[1])
            g0 = pl.multiple_of(n * WINDOW, WINDOW)
            dk_ref[pl.ds(g0, WINDOW), :] += dk2[WINDOW:]
            dv_ref[pl.ds(g0, WINDOW), :] += dv2[WINDOW:]
            gp = pl.multiple_of(jnp.maximum(n - 1, 0) * WINDOW, WINDOW)
            dk_ref[pl.ds(gp, WINDOW), :] += dk2[:WINDOW]
            dv_ref[pl.ds(gp, WINDOW), :] += dv2[:WINDOW]
            return carry

        lax.fori_loop(0, nb, step, 0, unroll=min(ATTN_UNROLL, nb))

    def cmap(which):
        return lambda cb, ch: (ch, (cb // hpn) * cpr + which * hpn + cb % hpn)

    def hmap(which):
        return lambda cb, ch: (jnp.maximum(ch * nb - 1, 0), (cb // hpn) * cpr + which * hpn + cb % hpn)

    blk = lambda m: pl.BlockSpec((Lc, LANES), m)
    hblk = lambda m: pl.BlockSpec((WINDOW, LANES), m)
    omap = lambda cb, ch: (ch, cb)
    full = pl.BlockSpec((L, LANES), lambda cb, ch: (0, cb))
    dq, dk, dv = pl.pallas_call(
        body, name=name, grid=(d * hpn, nch),
        in_specs=[pl.BlockSpec(memory_space=pltpu.SMEM), blk(cmap(0)), blk(cmap(1)), hblk(hmap(1)),
                  blk(cmap(2)), hblk(hmap(2)),
                  blk(lambda cb, ch: (ch, (cb // hpn) * cw + co + cb % hpn)), blk(omap), blk(omap)],
        out_specs=[blk(omap), full, full],
        out_shape=[jax.ShapeDtypeStruct((L, d * A), F32)] * 3,
        scratch_shapes=[pltpu.VMEM((Lc + WINDOW, LANES), BF16)] * 2
        + [pltpu.VMEM((4, WINDOW, 2 * WINDOW), F32)], compiler_params=_params(2),
    )(slopes, qv, qv, qv, qv, qv, dov, ov, lv)
    return dq.reshape(T, A), dk.reshape(T, A), dv.reshape(T, A)


def _attn_bwd_combine(name, dqs, dks, dvs, z, gq, gk, bmat, fmat, A, c0, hd):
    T = z.shape[0]
    tm = _pick(T, (256, 128))
    nbr = len(dqs)

    def body(*refs):
        dq_refs, dk_refs, dv_refs = refs[:nbr], refs[nbr:2 * nbr], refs[2 * nbr:3 * nbr]
        zq_ref, zk_ref, gq_ref, gk_ref, b_ref, f_ref, dz_ref, dgq_ref, dgk_ref = refs[3 * nbr:]
        i = pl.program_id(0)
        bm = b_ref[...]

        def tot(rs):
            t = rs[0][...]
            for r in rs[1:]:
                t = t + r[...]
            return t

        for idx, (d_refs, z_ref, g_ref, dg_ref, gscale) in enumerate(
                ((dq_refs, zq_ref, gq_ref, dgq_ref, QK_SCALE), (dk_refs, zk_ref, gk_ref, dgk_ref, 1.0))):
            dy = tot(d_refs)
            zv = z_ref[...]
            r = lax.rsqrt(_seg_sum(zv * zv, bm) * (1.0 / hd) + EPS)
            gd = dy * g_ref[...]
            mean = _seg_sum(gd * zv, bm) * (1.0 / hd)
            dz_ref[:, idx * A:(idx + 1) * A] = (r * gd - zv * (r * r * r) * mean).astype(BF16)
            part = jnp.sum(dy * zv * r, axis=0, keepdims=True) * gscale

            @pl.when(i == 0)
            def _():
                dg_ref[...] = part

            @pl.when(i > 0)
            def _():
                dg_ref[...] += part

        dz_ref[:, 2 * A:] = tot(dv_refs).astype(BF16)

        @pl.when(i == T // tm - 1)
        def _():
            fm = f_ref[...]
            for dg_ref in (dgq_ref, dgk_ref):
                v = jnp.broadcast_to(dg_ref[...], (8, A))
                hi = v.astype(BF16)
                mid = (v - hi.astype(F32)).astype(BF16)
                lo = (v - hi.astype(F32) - mid.astype(F32)).astype(BF16)
                dg_ref[...] = (_nn(hi, fm) + _nn(mid, fm) + _nn(lo, fm))[0:1]

    blk = pl.BlockSpec((tm, A), lambda i: (i, 0))
    vec = pl.BlockSpec((1, A), lambda i: (0, 0))
    return pl.pallas_call(
        body, name=name, grid=(T // tm,),
        in_specs=[blk] * (3 * nbr) + [pl.BlockSpec((tm, A), lambda i: (i, c0)),
                                      pl.BlockSpec((tm, A), lambda i: (i, c0 + 1)), vec, vec,
                                      pl.BlockSpec((A, A), lambda i: (0, 0)),
                                      pl.BlockSpec((A, A), lambda i: (0, 0))],
        out_specs=[pl.BlockSpec((tm, 3 * A), lambda i: (i, 0)), vec, vec],
        out_shape=[jax.ShapeDtypeStruct((T, 3 * A), BF16), jax.ShapeDtypeStruct((1, A), F32),
                   jax.ShapeDtypeStruct((1, A), F32)],
        compiler_params=_params(1),
    )(*dqs, *dks, *dvs, z, z, gq, gk, bmat, fmat)


def _local_step(x, tgt, S, comm, hd):
    T, D = x.shape
    C = S["conv_b_dw"].shape[1]
    A = C
    H = A // hd
    Dmix = C + A
    c0 = (2 * C) // A
    slopes = 2.0 ** (-ALIBI_MAX_BIAS * jnp.arange(1, H + 1, dtype=F32) / H)
    seg = jnp.arange(A) // hd
    bmat = (seg[:, None] == seg[None, :]).astype(BF16)
    pos_in_head = jnp.arange(A) % hd
    fmat = (pos_in_head[:, None] == pos_in_head[None, :]).astype(BF16)
    gq = jnp.tile(S["q_norm_g"], (1, H)) * QK_SCALE
    gk = jnp.tile(S["k_norm_g"], (1, H))

    wg1, wu1 = comm.weights(("ffn1_w_gate", "ffn1_w_up"), None)
    h1, gate1, up1, a1 = _norm_matmul("ffn1_up", x, comm.tie(S["ffn1_norm_g"]), [wg1, wu1], True)
    wd1, win, w32 = comm.weights(("ffn1_w_down", "w_in", "conv_w32"), a1)
    x1 = _matmul_res("ffn1_down", a1, wd1, x, 0.5)
    h2, z = _norm_matmul("mix_in", x1, S["mix_norm_g"], [win], False)
    yc, ycv = _conv_fwd("conv_fwd", z, w32, S["conv_b_dw"], S["conv_ln_g"], S["conv_ln_b"], C)
    qkv = _attn_prep("attn_prep", z, gq, gk, bmat, A, c0, hd)
    os_, ls_ = [], []
    for d in DILATIONS:
        o, l = _attn_fwd("attn_fwd_d%d" % d, qkv, slopes, d, A)
        os_.append(o)
        ls_.append(l)
    yatt, lg = _attn_combine("attn_combine", os_, ls_)
    ycat = jnp.concatenate([yc, yatt], axis=1)
    wout, wg2, wu2, wd2 = comm.weights(("w_out", "ffn2_w_gate", "ffn2_w_up", "ffn2_w_down"), yatt)
    x2 = _matmul_res("mix_out", ycat, wout, x1, 1.0)
    h3, gate2, up2, a2 = _norm_matmul("ffn2_up", x2, S["ffn2_norm_g"], [wg2, wu2], True)
    dx3, dx3b, lossvec = _matmul_res("ffn2_down_loss", a2, wd2, x2, 0.5, tgt=tgt)

    G = {}
    dgate2, dup2 = _nt_matmul("ffn2_dact", dx3b, wd2, 0.5, gate2, up2)
    comm.reduce_begin("ffn2", {"ffn2_w_down": _tn_matmul("ffn2_dwd", a2, dx3b, 0.5),
                               "ffn2_w_gate": _tn_matmul("ffn2_dwg", h3, dgate2),
                               "ffn2_w_up": _tn_matmul("ffn2_dwu", h3, dup2)})
    dx2, dx2b, G["ffn2_norm_g"] = _nt_rms_bwd("ffn2_dx", [dgate2, dup2], [wg2, wu2],
                                              x2, comm.tie(S["ffn2_norm_g"]), dx3)
    dwout = _tn_matmul("mix_dwout", ycat, dx2b)
    dycat, dycatb = _nt_matmul("mix_dycat", dx2b, wout)
    dzc, G["conv_w32"], G["conv_b_dw"], G["conv_ln_g"], G["conv_ln_b"] = _conv_bwd(
        "conv_bwd", z, ycv, dycat, w32, S["conv_ln_g"], S["conv_ln_b"], C)
    dqs, dks, dvs = [], [], []
    for d in DILATIONS:
        dq, dk, dv = _attn_bwd("attn_bwd_d%d" % d, qkv, dycatb, yatt, lg, slopes, d, A, Dmix, C)
        dqs.append(dq)
        dks.append(dk)
        dvs.append(dv)
    dzqkv, G["q_norm_g"], G["k_norm_g"] = _attn_bwd_combine(
        "attn_bwd_combine", dqs, dks, dvs, z, gq, gk, bmat, fmat, A, c0, hd)
    comm.reduce_end("ffn2", dzqkv)
    dz = jnp.concatenate([dzc, dzqkv], axis=1)
    comm.reduce_begin("mix", {"w_out": dwout, "w_in": _tn_matmul("mix_dwin", h2, dz)})
    dx1, dx1b, G["mix_norm_g"] = _nt_rms_bwd("mix_dx", [dz], [win], x1, comm.tie(S["mix_norm_g"]), dx2)
    dgate1, dup1 = _nt_matmul("ffn1_dact", dx1b, wd1, 0.5, gate1, up1)
    dwd1 = _tn_matmul("ffn1_dwd", a1, dx1b, 0.5)
    dwg1 = _tn_matmul("ffn1_dwg", h1, dgate1)
    dwu1 = _tn_matmul("ffn1_dwu", h1, dup1)
    comm.reduce_end("mix", dwu1)
    comm.reduce_begin("ffn1", {"ffn1_w_down": dwd1, "ffn1_w_gate": dwg1, "ffn1_w_up": dwu1})
    dx0, _, G["ffn1_norm_g"] = _nt_rms_bwd("ffn1_dx", [dgate1, dup1], [wg1, wu1],
                                           x, comm.tie(S["ffn1_norm_g"]), dx1)
    comm.reduce_end("ffn1", dx0)
    return lossvec, dx0, G


BIG = (("ffn1_w_gate", 1), ("ffn1_w_up", 1), ("ffn1_w_down", 0), ("w_in", 1), ("w_out", 0),
       ("ffn2_w_gate", 1), ("ffn2_w_up", 1), ("ffn2_w_down", 0))
AXIS = dict(BIG)
FLIPS = ((1, 0), (0, 1), (1, 1))
HBM = pl.BlockSpec(memory_space=pltpu.HBM)
SEM = pl.BlockSpec(memory_space=pltpu.SEMAPHORE)
EFFECT = pltpu.SideEffectType.DATAFLOW_SIDE_EFFECTING
TOKEN = jax.ShapeDtypeStruct((8, LANES), F32)


def _window(ref, shape, axis, slab=None, half=None):
    idx = [pl.ds(0, shape[0]), pl.ds(0, shape[1])]
    if slab is not None:
        n = shape[axis] // 4
        idx[axis] = pl.ds(pl.multiple_of(slab * n, 8), n)
    if half is not None:
        hs = shape[1 - axis] // 2
        idx[1 - axis] = pl.ds(pl.multiple_of(half * hs, 8), hs)
    return ref.at[idx[0], idx[1]]


def _position():
    return lax.axis_index("x"), lax.axis_index("y"), lax.axis_index("c")


def _half_shape(shape, axis):
    return (shape[0] // 2, shape[1]) if axis == 1 else (shape[0], shape[1] // 2)


def _slab_shape(shape, axis):
    return (shape[0], shape[1] // 4) if axis == 1 else (shape[0] // 4, shape[1])


def _piece_shape(shape, axis):
    return _half_shape(_slab_shape(shape, axis), axis)


def _full_shape(shard, axis):
    return (shard.shape[0], shard.shape[1] * 4) if axis == 1 else (shard.shape[0] * 4, shard.shape[1])


def _hbm(a):
    return pltpu.with_memory_space_constraint(a, pltpu.HBM)


def _remote(src, dst, send_sem, recv_sem, to):
    return pltpu.make_async_remote_copy(src_ref=src, dst_ref=dst, send_sem=send_sem, recv_sem=recv_sem,
                                        device_id=to, device_id_type=MESH)


def _gather_now(name, axes, shards):
    nt = len(shards)
    shapes = [_full_shape(s, a) for s, a in zip(shards, axes)]

    def body(*refs):
        ins, outs, token = refs[:nt], refs[nt:2 * nt], refs[2 * nt]
        send_sems, recv_sems, loc_sems = refs[2 * nt + 1:]
        x, y, c = _position()
        j0 = 2 * x + y
        sib = (x, y, 1 - c)
        local = []
        for t in range(nt):
            cp = pltpu.make_async_copy(ins[t], _window(outs[t], shapes[t], axes[t], slab=j0), loc_sems.at[t])
            cp.start()
            local.append(cp)

        def copy(t, k, src, slab, half, to):
            return _remote(src, _window(outs[t], shapes[t], axes[t], slab=slab, half=half),
                           send_sems.at[t, k], recv_sems.at[t, k], to)

        sends = []
        for k, (fx, fy) in enumerate(FLIPS):
            for t in range(nt):
                src = _window(ins[t], _slab_shape(shapes[t], axes[t]), axes[t], half=c)
                cp = copy(t, k, src, j0, c, (x ^ fx, y ^ fy, c))
                cp.start()
                sends.append(cp)
        for k, (fx, fy) in enumerate(FLIPS):
            js = 2 * (x ^ fx) + (y ^ fy)
            for t in range(nt):
                landed = _window(outs[t], shapes[t], axes[t], slab=js, half=c)
                copy(t, k, landed, js, c, sib).wait_recv()
                cp = copy(t, 3 + k, landed, js, c, sib)
                cp.start()
                sends.append(cp)
        for k, (fx, fy) in enumerate(FLIPS):
            js = 2 * (x ^ fx) + (y ^ fy)
            for t in range(nt):
                other = _window(outs[t], shapes[t], axes[t], slab=js, half=1 - c)
                copy(t, 3 + k, other, js, 1 - c, sib).wait_recv()
        for cp in sends:
            cp.wait_send()
        for cp in local:
            cp.wait()
        token[...] = jnp.zeros_like(token)

    res = pl.pallas_call(
        body, name=name, in_specs=[ANY] * nt,
        out_specs=[ANY] * nt + [pl.BlockSpec(memory_space=pltpu.VMEM)],
        out_shape=[jax.ShapeDtypeStruct(s, BF16) for s in shapes] + [TOKEN],
        scratch_shapes=[pltpu.SemaphoreType.DMA((nt, 6)), pltpu.SemaphoreType.DMA((nt, 6)),
                        pltpu.SemaphoreType.DMA((nt,))],
    )(*shards)
    return list(res[:nt]), res[nt]


def _split_start(name, arrays, ncopies, plan):
    na = len(arrays)

    def body(*refs):
        ins = refs[:na]
        send_sems, recv_sems = refs[na], refs[na + 1]
        token = refs[-1]
        x, y, c = _position()
        for i, (src, dst, to) in enumerate(plan(ins, x, y, c)):
            _remote(src, dst, send_sems.at[i], recv_sems.at[i], to).start()
        token[...] = jnp.zeros_like(token)

    res = pl.pallas_call(
        body, name=name, in_specs=[HBM] * na,
        out_specs=tuple([SEM, SEM] + [HBM] * na + [pl.BlockSpec(memory_space=pltpu.VMEM)]),
        out_shape=tuple([pltpu.SemaphoreType.DMA((ncopies,)), pltpu.SemaphoreType.DMA((ncopies,))]
                        + [pltpu.HBM(a.shape, a.dtype) for a in arrays] + [TOKEN]),
        input_output_aliases={i: 2 + i for i in range(na)},
        compiler_params=pltpu.CompilerParams(has_side_effects=EFFECT),
    )(*[_hbm(a) for a in arrays])
    return (res[0], res[1]), list(res[2:2 + na]), res[-1]


def _split_wait(name, arrays, sems, after, plan):
    na = len(arrays)

    def body(*refs):
        ins = refs[:na]
        send_sems, recv_sems = refs[na], refs[na + 1]
        x, y, c = _position()
        for i, (src, dst, to) in enumerate(plan(ins, x, y, c)):
            cp = _remote(src, dst, send_sems.at[i], recv_sems.at[i], to)
            cp.wait_send()
            cp.wait_recv()

    res = pl.pallas_call(
        body, name=name, in_specs=[HBM] * na + [SEM, SEM, ANY],
        out_specs=tuple([HBM] * na), out_shape=tuple(pltpu.HBM(a.shape, a.dtype) for a in arrays),
        input_output_aliases={i: i for i in range(na)},
        compiler_params=pltpu.CompilerParams(has_side_effects=EFFECT),
    )(*arrays, *sems, after)
    return list(res)


def _gather_plan(axes, shapes, conv_shape):
    nt = len(axes)

    def plan(refs, x, y, c):
        j0 = 2 * x + y
        out = []
        for fx, fy in FLIPS:
            to = (x ^ fx, y ^ fy, c)
            for t in range(nt):
                src = _window(refs[t], _slab_shape(shapes[t], axes[t]), axes[t], half=c)
                out.append((src, _window(refs[nt + t], shapes[t], axes[t], slab=j0, half=c), to))
            if conv_shape is not None:
                out.append((refs[2 * nt], _window(refs[2 * nt + 1], conv_shape, 1, slab=j0), to))
        return out

    return plan


def _gather_finish(name, axes, shards, fulls, conv=None):
    nt = len(axes)
    shapes = [f.shape for f in fulls]
    nin = 2 * nt + (2 if conv is not None else 0)

    def body(*refs):
        ins = refs[:nt]
        outs = refs[nin:nin + nt]
        send_sems, recv_sems, loc_sems = refs[-3:]
        x, y, c = _position()
        j0 = 2 * x + y
        sib = (x, y, 1 - c)
        local = []
        for t in range(nt):
            cp = pltpu.make_async_copy(ins[t], _window(outs[t], shapes[t], axes[t], slab=j0), loc_sems.at[t])
            cp.start()
            local.append(cp)
        if conv is not None:
            cp = pltpu.make_async_copy(refs[2 * nt], _window(refs[nin + nt], conv[1].shape, 1, slab=j0),
                                       loc_sems.at[nt])
            cp.start()
            local.append(cp)
        cps = []
        for k, (fx, fy) in enumerate(FLIPS):
            js = 2 * (x ^ fx) + (y ^ fy)
            for t in range(nt):
                landed = _window(outs[t], shapes[t], axes[t], slab=js, half=c)
                cp = _remote(landed, landed, send_sems.at[t, k], recv_sems.at[t, k], sib)
                cp.start()
                cps.append(cp)
        for k, (fx, fy) in enumerate(FLIPS):
            js = 2 * (x ^ fx) + (y ^ fy)
            for t in range(nt):
                other = _window(outs[t], shapes[t], axes[t], slab=js, half=1 - c)
                _remote(other, other, send_sems.at[t, k], recv_sems.at[t, k], sib).wait_recv()
        for cp in cps:
            cp.wait_send()
        for cp in local:
            cp.wait()

    args = list(shards) + list(fulls) + (list(conv) if conv is not None else [])
    nout = nt + (1 if conv is not None else 0)
    res = pl.pallas_call(
        body, name=name, in_specs=[ANY] * nin, out_specs=[ANY] * nout,
        out_shape=[jax.ShapeDtypeStruct(f.shape, f.dtype) for f in fulls]
        + ([jax.ShapeDtypeStruct(conv[1].shape, conv[1].dtype)] if conv is not None else []),
        input_output_aliases=dict([(nt + t, t) for t in range(nt)] + ([(2 * nt + 1, nt)] if conv is not None else [])),
        scratch_shapes=[pltpu.SemaphoreType.DMA((nt, 3)), pltpu.SemaphoreType.DMA((nt, 3)),
                        pltpu.SemaphoreType.DMA((nt + 1,))],
    )(*args)
    return list(res)


def _pair_exchange(name, srcs, windows, out_shapes, dtype):
    nt = len(srcs)

    def body(*refs):
        ins, outs = refs[:nt], refs[nt:2 * nt]
        send_sems, recv_sems = refs[2 * nt:]
        x, y, c = _position()
        cps = []
        for t in range(nt):
            cp = _remote(windows[t](ins[t], c), outs[t], send_sems.at[t], recv_sems.at[t], (x, y, 1 - c))
            cp.start()
            cps.append(cp)
        for cp in cps:
            cp.wait()

    return pl.pallas_call(
        body, name=name, in_specs=[ANY] * nt, out_specs=[ANY] * nt,
        out_shape=[jax.ShapeDtypeStruct(s, dtype) for s in out_shapes],
        scratch_shapes=[pltpu.SemaphoreType.DMA((nt,)), pltpu.SemaphoreType.DMA((nt,))],
    )(*srcs)


def _scatter_plan(axes, shapes):
    nt = len(axes)

    def plan(refs, x, y, c):
        out = []
        for k, (fx, fy) in enumerate(FLIPS):
            js = 2 * (x ^ fx) + (y ^ fy)
            for t in range(nt):
                src = _window(refs[t], _half_shape(shapes[t], axes[t]), axes[t], slab=js)
                out.append((src, refs[nt + t].at[k], (x ^ fx, y ^ fy, c)))
        return out

    return plan


def _gather_small(packed):
    R, Cc = packed.shape

    def body(p_ref, o_ref, send_sems, recv_sems, loc_sem):
        x, y, c = _position()
        me = 4 * x + 2 * y + c
        mine = pltpu.make_async_copy(p_ref, o_ref.at[me], loc_sem)
        mine.start()
        cps = []
        for k in range(1, 8):
            fx, fy, fc = (k >> 2) & 1, (k >> 1) & 1, k & 1
            cp = pltpu.make_async_remote_copy(
                src_ref=p_ref, dst_ref=o_ref.at[me], send_sem=send_sems.at[k - 1], recv_sem=recv_sems.at[k - 1],
                device_id=(x ^ fx, y ^ fy, c ^ fc), device_id_type=MESH)
            cp.start()
            cps.append(cp)
        for cp in cps:
            cp.wait()
        mine.wait()

    return pl.pallas_call(
        body, name="gather_small_grads", in_specs=[ANY], out_specs=ANY,
        out_shape=jax.ShapeDtypeStruct((8, R, Cc), F32),
        scratch_shapes=[pltpu.SemaphoreType.DMA((7,)), pltpu.SemaphoreType.DMA((7,)), pltpu.SemaphoreType.DMA],
    )(packed)


def _sum_slots(name, slots):
    n, R, Cc = slots.shape

    def body(s_ref, o_ref):
        t = s_ref[0]
        for i in range(1, n):
            t = t + s_ref[i]
        o_ref[...] = t

    return pl.pallas_call(
        body, name=name, grid=(1,), in_specs=[pl.BlockSpec((n, R, Cc), lambda i: (0, 0, 0))],
        out_specs=pl.BlockSpec((R, Cc), lambda i: (0, 0)), out_shape=jax.ShapeDtypeStruct((R, Cc), F32),
        compiler_params=_params(1),
    )(slots)


def _pair_sum(name, pos, g, land, shape, axis):
    hshape = _half_shape(shape, axis)
    R, Cc = hshape
    tr = _pick(R, (256, 128, 64, 32, 16))
    nrb = R // tr

    def body(pos_ref, g_ref, l_ref, o_ref):
        o_ref[...] = (g_ref[...].astype(F32) + l_ref[...].astype(F32)).astype(BF16)

    if axis == 1:
        gmap = lambda i, p: (p[1] * nrb + i, 0)
    else:
        gmap = lambda i, p: (i, p[1])
    blk = pl.BlockSpec((tr, Cc), lambda i, p: (i, 0))
    return pl.pallas_call(
        body, name=name,
        grid_spec=pltpu.PrefetchScalarGridSpec(
            num_scalar_prefetch=1, grid=(nrb,), in_specs=[pl.BlockSpec((tr, Cc), gmap), blk], out_specs=blk),
        out_shape=jax.ShapeDtypeStruct(hshape, BF16), compiler_params=_params(1),
    )(pos, g, land)


def _chip_sum(name, pos, sb, land, shape, axis):
    hshape = _half_shape(shape, axis)
    pshape = _piece_shape(shape, axis)
    R, Cc = pshape
    tr = _pick(R, (256, 128, 64, 32, 16))
    nrb = R // tr

    def body(pos_ref, s_ref, l_ref, o_ref):
        t = s_ref[...].astype(F32)
        for k in range(3):
            t = t + l_ref[k].astype(F32)
        o_ref[...] = t

    if axis == 1:
        smap = lambda i, p: (i, p[0])
    else:
        smap = lambda i, p: (p[0] * nrb + i, 0)
    return pl.pallas_call(
        body, name=name,
        grid_spec=pltpu.PrefetchScalarGridSpec(
            num_scalar_prefetch=1, grid=(nrb,),
            in_specs=[pl.BlockSpec((tr, Cc), smap), pl.BlockSpec((3, tr, Cc), lambda i, p: (0, i, 0))],
            out_specs=pl.BlockSpec((tr, Cc), lambda i, p: (i, 0))),
        out_shape=jax.ShapeDtypeStruct(pshape, F32), compiler_params=_params(1),
    )(pos, sb, land)


def _adam_math(w, g, m, v):
    m = ADAM_B1 * m + (1.0 - ADAM_B1) * g
    v = ADAM_B2 * v + (1.0 - ADAM_B2) * (g * g)
    m_hat = m / (1.0 - ADAM_B1 ** ADAM_STEP)
    v_hat = v / (1.0 - ADAM_B2 ** ADAM_STEP)
    delta = -ADAM_LR * (m_hat / (jnp.sqrt(v_hat) + ADAM_EPS) + ADAM_WD * w)
    return delta, m, v


def _adamw_halves(name, pos, w, m, v, mine, theirs, axis):
    R, Cc = w.shape
    hr, hc = mine.shape
    tr = _pick(hr, (256, 128, 64, 32, 16))
    nrb = hr // tr

    def body(pos_ref, w_ref, m_ref, v_ref, a_ref, b_ref, g_ref, d_ref, nm_ref, nv_ref):
        half = pl.program_id(0)
        g = jnp.where(half == pos_ref[1], a_ref[...], b_ref[...])
        d, nm, nv = _adam_math(w_ref[...], g, m_ref[...], v_ref[...])
        g_ref[...] = g
        d_ref[...] = d
        nm_ref[...] = nm
        nv_ref[...] = nv

    if axis == 1:
        wmap = lambda h, i, p: (h * nrb + i, 0)
    else:
        wmap = lambda h, i, p: (i, h)
    wblk = pl.BlockSpec((tr, hc), wmap)
    hblk = pl.BlockSpec((tr, hc), lambda h, i, p: (i, 0))
    return pl.pallas_call(
        body, name=name,
        grid_spec=pltpu.PrefetchScalarGridSpec(
            num_scalar_prefetch=1, grid=(2, nrb), in_specs=[wblk, wblk, wblk, hblk, hblk], out_specs=[wblk] * 4),
        out_shape=[jax.ShapeDtypeStruct((R, Cc), F32)] * 4, compiler_params=_params(2),
    )(pos, w, m, v, mine, theirs)


def _adamw_small(name, w, g, m, v):
    def body(w_ref, g_ref, m_ref, v_ref, d_ref, nm_ref, nv_ref):
        d, nm, nv = _adam_math(w_ref[...], g_ref[...], m_ref[...], v_ref[...])
        d_ref[...] = d
        nm_ref[...] = nm
        nv_ref[...] = nv

    blk = pl.BlockSpec(w.shape, lambda i: (0, 0))
    return pl.pallas_call(
        body, name=name, grid=(1,), in_specs=[blk] * 4, out_specs=[blk] * 3,
        out_shape=[jax.ShapeDtypeStruct(w.shape, F32)] * 3, compiler_params=_params(1),
    )(w, g, m, v)


SMALL = ("ffn1_norm_g", "mix_norm_g", "conv_b_dw", "conv_ln_g", "conv_ln_b", "q_norm_g", "k_norm_g", "ffn2_norm_g")
ORDER = ("ffn1_norm_g", "ffn1_w_gate", "ffn1_w_up", "ffn1_w_down", "mix_norm_g", "w_in", "conv_w_dw", "conv_b_dw",
         "conv_ln_g", "conv_ln_b", "q_norm_g", "k_norm_g", "w_out", "ffn2_norm_g", "ffn2_w_gate", "ffn2_w_up",
         "ffn2_w_down")
GATHER_FIRST = ("ffn1_w_gate", "ffn1_w_up")
GATHER_SECOND = ("ffn1_w_down", "w_in")
GATHER_THIRD = ("w_out", "ffn2_w_gate", "ffn2_w_up", "ffn2_w_down")


class _Exchange:
    def __init__(self, P, Mo, Vo, conv_shard, pos):
        self.P, self.Mo, self.Vo, self.pos = P, Mo, Vo, pos
        self.tokens = []
        self.pending = {}
        self.reducing = {}
        self.results = {}
        shards = {n: P[n][0].astype(BF16) for n, _ in BIG}
        self.shapes = {n: _full_shape(shards[n], a) for n, a in BIG}
        first, tok = _gather_now("gather_first", [AXIS[n] for n in GATHER_FIRST], [shards[n] for n in GATHER_FIRST])
        self.ready = dict(zip(GATHER_FIRST, first))
        for gname, names, conv in (("second", GATHER_SECOND, conv_shard), ("third", GATHER_THIRD, None)):
            axes = [AXIS[n] for n in names]
            shapes = [self.shapes[n] for n in names]
            srcs = [shards[n] for n in names]
            cshape = None
            if conv is None:
                srcs[0] = srcs[0] + tok[0, 0].astype(BF16)
            arrays = srcs + [lax.empty(s, BF16) for s in shapes]
            if conv is not None:
                cshape = (conv.shape[0], conv.shape[1] * 4)
                arrays += [conv + tok[0, 0], lax.empty(cshape, F32)]
            plan = _gather_plan(axes, shapes, cshape)
            ncopies = 3 * (len(names) + (conv is not None))
            sems, thru, tok = _split_start("gather_%s_start" % gname, arrays, ncopies, plan)
            self.tokens.append(tok)
            for n in names + (("conv_w32",) if conv is not None else ()):
                self.pending[n] = (gname, names, axes, plan, sems, thru, conv is not None)

    def tie(self, v):
        for tok in self.tokens:
            v = v + tok[0:1, 0:1]
        self.tokens = []
        return v

    def weights(self, names, after):
        if names[0] in self.pending:
            gname, gnames, axes, plan, sems, thru, has_conv = self.pending[names[0]]
            thru = _split_wait("gather_%s_wait" % gname, thru, sems, after, plan)
            nt = len(gnames)
            conv = (thru[2 * nt], thru[2 * nt + 1]) if has_conv else None
            fulls = _gather_finish("gather_%s_finish" % gname, axes, thru[:nt], thru[nt:2 * nt], conv)
            for n, f in zip(gnames, fulls):
                self.ready[n] = f
                del self.pending[n]
            if has_conv:
                self.ready["conv_w32"] = fulls[nt]
                del self.pending["conv_w32"]
        return [self.ready[n] for n in names]

    def reduce_begin(self, gname, grads):
        names = list(grads)
        axes = [AXIS[n] for n in names]
        shapes = [self.shapes[n] for n in names]
        gs = [grads[n] for n in names]
        to_sibling = [(lambda ref, c, s=s, a=a: _window(ref, s, a, half=1 - c)) for s, a in zip(shapes, axes)]
        landed = _pair_exchange("pair_exchange_" + gname, gs, to_sibling,
                                [_half_shape(s, a) for s, a in zip(shapes, axes)], BF16)
        sbs = [_pair_sum("pair_sum_" + n, self.pos, g, l, s, a)
               for n, a, g, l, s in zip(names, axes, gs, landed, shapes)]
        lands = [lax.empty((3,) + _piece_shape(s, a), BF16) for s, a in zip(shapes, axes)]
        plan = _scatter_plan(axes, shapes)
        sems, thru, tok = _split_start("scatter_%s_start" % gname, sbs + lands, 3 * len(names), plan)
        self.tokens.append(tok)
        self.reducing[gname] = (names, axes, shapes, plan, sems, thru)

    def reduce_end(self, gname, after):
        names, axes, shapes, plan, sems, thru = self.reducing.pop(gname)
        nt = len(names)
        thru = _split_wait("scatter_%s_wait" % gname, thru, sems, after, plan)
        mine = [_chip_sum("chip_sum_" + n, self.pos, sb, l, s, a)
                for n, a, sb, l, s in zip(names, axes, thru[:nt], thru[nt:], shapes)]
        theirs = _pair_exchange("half_exchange_" + gname, mine, [(lambda ref, c: ref)] * nt,
                                [m.shape for m in mine], F32)
        for n, a, mi, th in zip(names, axes, mine, theirs):
            g, d, nm, nv = _adamw_halves("adamw_" + n, self.pos, self.P[n][0], self.Mo[n][0], self.Vo[n][0],
                                         mi, th, a)
            self.results[n] = (g[None], d[None], nm[None], nv[None])


def kernel(x, ffn1_norm_g, ffn1_w_gate, ffn1_w_up, ffn1_w_down, mix_norm_g, w_in, conv_w_dw, conv_b_dw, conv_ln_g, conv_ln_b, q_norm_g, k_norm_g, w_out, ffn2_norm_g, ffn2_w_gate, ffn2_w_up, ffn2_w_down, loss_target, m_ffn1_norm_g, m_ffn1_w_gate, m_ffn1_w_up, m_ffn1_w_down, m_mix_norm_g, m_w_in, m_conv_w_dw, m_conv_b_dw, m_conv_ln_g, m_conv_ln_b, m_q_norm_g, m_k_norm_g, m_w_out, m_ffn2_norm_g, m_ffn2_w_gate, m_ffn2_w_up, m_ffn2_w_down, v_ffn1_norm_g, v_ffn1_w_gate, v_ffn1_w_up, v_ffn1_w_down, v_mix_norm_g, v_w_in, v_conv_w_dw, v_conv_b_dw, v_conv_ln_g, v_conv_ln_b, v_q_norm_g, v_k_norm_g, v_w_out, v_ffn2_norm_g, v_ffn2_w_gate, v_ffn2_w_up, v_ffn2_w_down):
    args = dict(locals())
    P = {n: args[n] for n in ORDER}
    Mo = {n: args["m_" + n] for n in ORDER}
    Vo = {n: args["v_" + n] for n in ORDER}
    xs = x[0]
    tgt = loss_target[0]
    T, D = xs.shape
    hd = q_norm_g.shape[-1]
    C = conv_b_dw.shape[-1]
    ntap = conv_w_dw.shape[1]
    cx, cy, cc = _position()
    j0 = 2 * cx + cy
    pos = jnp.stack([j0, cc]).astype(jnp.int32)

    conv_shard = jnp.pad(conv_w_dw[0], ((0, HALO - ntap), (0, 0)))
    comm = _Exchange(P, Mo, Vo, conv_shard, pos)
    lossvec, dx0, G = _local_step(xs, tgt, {n: P[n] for n in SMALL}, comm, hd)
    loss = lax.psum(0.5 / D * jnp.sum(lossvec), AXES)
    grads, deltas, new_m, new_v = {}, {}, {}, {}
    for n, _ in BIG:
        grads[n], deltas[n], new_m[n], new_v[n] = comm.results[n]

    rows = [G["conv_w32"]]
    for n in ("ffn1_norm_g", "mix_norm_g", "ffn2_norm_g"):
        rows.append(G[n].reshape(D // C, C))
    for n in ("conv_b_dw", "conv_ln_g", "conv_ln_b", "q_norm_g", "k_norm_g"):
        rows.append(G[n])
    packed = jnp.concatenate(rows, axis=0)
    packed = jnp.pad(packed, ((0, -packed.shape[0] % 8), (0, 0)))
    total = _sum_slots("sum_small_grads", _gather_small(packed))
    r = HALO
    small_g = {}
    cq = C // 4
    small_g["conv_w_dw"] = lax.dynamic_slice(total[:ntap], (0, j0 * cq), (ntap, cq))
    for n in ("ffn1_norm_g", "mix_norm_g", "ffn2_norm_g"):
        small_g[n] = total[r:r + D // C].reshape(1, D)
        r += D // C
    for n in ("conv_b_dw", "conv_ln_g", "conv_ln_b"):
        small_g[n] = total[r:r + 1]
        r += 1
    for n in ("q_norm_g", "k_norm_g"):
        small_g[n] = total[r:r + 1, :hd]
        r += 1
    for n in ("conv_w_dw",) + SMALL:
        lead = n == "conv_w_dw"
        w2, m2, v2 = (P[n][0], Mo[n][0], Vo[n][0]) if lead else (P[n], Mo[n], Vo[n])
        d, nm, nv = _adamw_small("adamw_" + n, w2, small_g[n], m2, v2)
        if lead:
            grads[n], deltas[n], new_m[n], new_v[n] = small_g[n][None], d[None], nm[None], nv[None]
        else:
            grads[n], deltas[n], new_m[n], new_v[n] = small_g[n], d, nm, nv

    return (loss, dx0[None], *[grads[n] for n in ORDER], *[deltas[n] for n in ORDER],
            *[new_m[n] for n in ORDER], *[new_v[n] for n in ORDER])
```

```python
import jax
import jax.numpy as jnp
from jax import lax
from jax.experimental import pallas as pl
from jax.experimental.pallas import tpu as pltpu

F32 = jnp.float32
BF16 = jnp.bfloat16
EPS = 1e-6
WINDOW = 128
DILATIONS = (1, 4, 16)
ALIBI_MAX_BIAS = 8.0
LANES = 128
HALO = 32
ADAM_LR, ADAM_B1, ADAM_B2, ADAM_EPS, ADAM_WD, ADAM_STEP = 0.001, 0.9, 0.999, 1e-08, 0.01, 10
VMEM_LIMIT_MB = 56
MESH = pl.DeviceIdType.MESH
ANY = pl.BlockSpec(memory_space=pl.ANY)
AXES = ("x", "y", "c")
NEG = -1e30


def _pick(n, cands):
    for c in cands:
        if n % c == 0:
            return c
    return n


def _params(nsem):
    return pltpu.CompilerParams(dimension_semantics=("arbitrary",) * nsem,
                                vmem_limit_bytes=VMEM_LIMIT_MB << 20)


def _nn(a, b):
    return jnp.dot(a, b, preferred_element_type=F32)


def _nt(a, b):
    return lax.dot_general(a, b, (((1,), (1,)), ((), ())), preferred_element_type=F32)


def _tn(a, b):
    return lax.dot_general(a, b, (((0,), (0,)), ((), ())), preferred_element_type=F32)


def _sigmoid(v):
    return jax.nn.sigmoid(v)


def _rms_r(xv):
    return lax.rsqrt(jnp.mean(xv * xv, axis=-1, keepdims=True) + EPS)


def _norm_matmul(name, x, g, ws, swiglu):
    T, D = x.shape
    N = ws[0].shape[1]
    tm = _pick(T, (512, 256, 128))
    tn = _pick(N, (512, 256, 128))
    nw = len(ws)

    def body(*refs):
        x_ref, g_ref = refs[:2]
        w_refs = refs[2:2 + nw]
        outs = refs[2 + nw:-1]
        hs = refs[-1]

        @pl.when(pl.program_id(1) == 0)
        def _():
            xv = x_ref[...]
            hv = (xv * _rms_r(xv) * g_ref[...]).astype(BF16)
            hs[...] = hv
            outs[0][...] = hv

        h = hs[...]
        if swiglu:
            gt = _nn(h, w_refs[0][...])
            u = _nn(h, w_refs[1][...])
            outs[1][...] = gt.astype(BF16)
            outs[2][...] = u.astype(BF16)
            outs[3][...] = (gt * _sigmoid(gt) * u).astype(BF16)
        else:
            outs[1][...] = _nn(h, w_refs[0][...])

    row = pl.BlockSpec((tm, D), lambda i, j: (i, 0))
    col = pl.BlockSpec((D, tn), lambda i, j: (0, j))
    tile = pl.BlockSpec((tm, tn), lambda i, j: (i, j))
    if swiglu:
        out_shape = [jax.ShapeDtypeStruct((T, D), BF16)] + [jax.ShapeDtypeStruct((T, N), BF16)] * 3
        out_specs = [row, tile, tile, tile]
    else:
        out_shape = [jax.ShapeDtypeStruct((T, D), BF16), jax.ShapeDtypeStruct((T, N), F32)]
        out_specs = [row, tile]
    return pl.pallas_call(
        body, name=name, grid=(T // tm, N // tn),
        in_specs=[row, pl.BlockSpec((1, D), lambda i, j: (0, 0))] + [col] * nw,
        out_specs=out_specs, out_shape=out_shape,
        scratch_shapes=[pltpu.VMEM((tm, D), BF16)],
        compiler_params=_params(2),
    )(x, g, *ws)


def _matmul_res(name, a, w, res, scale, tgt=None):
    T, K = a.shape
    N = w.shape[1]
    tm = _pick(T, (512, 256, 128))
    tk = _pick(K, (1408, 1024, 512, 256, 128))
    nk = K // tk
    loss = tgt is not None

    def body(*refs):
        if loss:
            a_ref, w_ref, res_ref, tgt_ref, dx_ref, dxb_ref, lv_ref, acc = refs
        else:
            a_ref, w_ref, res_ref, out_ref, acc = refs
        i, k = pl.program_id(0), pl.program_id(1)

        @pl.when(k == 0)
        def _():
            acc[...] = jnp.zeros_like(acc)

        acc[...] += _nn(a_ref[...], w_ref[...])

        @pl.when(k == nk - 1)
        def _():
            val = res_ref[...] + scale * acc[...]
            if loss:
                dv = val - tgt_ref[...]
                dx = dv * (1.0 / N)
                dx_ref[...] = dx
                dxb_ref[...] = dx.astype(BF16)
                part = jnp.sum(dv * dv, axis=0, keepdims=True)

                @pl.when(i == 0)
                def _():
                    lv_ref[...] = part

                @pl.when(i > 0)
                def _():
                    lv_ref[...] += part
            else:
                out_ref[...] = val

    row = pl.BlockSpec((tm, N), lambda i, k: (i, 0))
    in_specs = [pl.BlockSpec((tm, tk), lambda i, k: (i, k)), pl.BlockSpec((tk, N), lambda i, k: (k, 0)), row]
    args = [a, w, res]
    if loss:
        in_specs.append(row)
        args.append(tgt)
        out_specs = [row, row, pl.BlockSpec((1, N), lambda i, k: (0, 0))]
        out_shape = [jax.ShapeDtypeStruct((T, N), F32), jax.ShapeDtypeStruct((T, N), BF16),
                     jax.ShapeDtypeStruct((1, N), F32)]
    else:
        out_specs = row
        out_shape = jax.ShapeDtypeStruct((T, N), F32)
    return pl.pallas_call(
        body, name=name, grid=(T // tm, nk), in_specs=in_specs, out_specs=out_specs, out_shape=out_shape,
        scratch_shapes=[pltpu.VMEM((tm, N), F32)], compiler_params=_params(2),
    )(*args)


def _nt_matmul(name, dyb, w, scale=1.0, gate=None, up=None):
    T, D = dyb.shape
    N = w.shape[0]
    tm = _pick(T, (512, 256, 128))
    tn = _pick(N, (512, 256, 128))
    swiglu = gate is not None

    def body(*refs):
        if swiglu:
            dy_ref, w_ref, g_ref, u_ref, dg_ref, du_ref = refs
        else:
            dy_ref, w_ref, o_ref, ob_ref = refs
        da = _nt(dy_ref[...], w_ref[...]) * scale
        if swiglu:
            gt = g_ref[...].astype(F32)
            u = u_ref[...].astype(F32)
            sg = _sigmoid(gt)
            dg_ref[...] = (da * u * (sg * (1.0 + gt * (1.0 - sg)))).astype(BF16)
            du_ref[...] = (da * (gt * sg)).astype(BF16)
        else:
            o_ref[...] = da
            ob_ref[...] = da.astype(BF16)

    tile = pl.BlockSpec((tm, tn), lambda i, j: (i, j))
    in_specs = [pl.BlockSpec((tm, D), lambda i, j: (i, 0)), pl.BlockSpec((tn, D), lambda i, j: (j, 0))]
    args = [dyb, w]
    if swiglu:
        in_specs += [tile, tile]
        args += [gate, up]
        out_shape = [jax.ShapeDtypeStruct((T, N), BF16)] * 2
    else:
        out_shape = [jax.ShapeDtypeStruct((T, N), F32), jax.ShapeDtypeStruct((T, N), BF16)]
    return pl.pallas_call(
        body, name=name, grid=(T // tm, N // tn), in_specs=in_specs, out_specs=[tile, tile],
        out_shape=out_shape, compiler_params=_params(2),
    )(*args)


def _nt_rms_bwd(name, As, Ws, x, g, dres):
    T, K = As[0].shape
    D = x.shape[1]
    tm = _pick(T, (512, 256, 128))
    tk = _pick(K, (512, 256, 128))
    nk = K // tk
    na = len(As)

    def body(*refs):
        a_refs = refs[:na]
        w_refs = refs[na:2 * na]
        x_ref, g_ref, dres_ref, dx_ref, dxb_ref, dg_ref, acc = refs[2 * na:]
        i, k = pl.program_id(0), pl.program_id(1)

        @pl.when(k == 0)
        def _():
            acc[...] = jnp.zeros_like(acc)

        for a_ref, w_ref in zip(a_refs, w_refs):
            acc[...] += _nt(a_ref[...], w_ref[...])

        @pl.when(k == nk - 1)
        def _():
            dh = acc[...]
            xv = x_ref[...]
            r = _rms_r(xv)
            gd = dh * g_ref[...]
            dx = dres_ref[...] + r * gd - xv * (r * r * r) * jnp.mean(gd * xv, axis=-1, keepdims=True)
            dx_ref[...] = dx
            dxb_ref[...] = dx.astype(BF16)
            part = jnp.sum(dh * xv * r, axis=0, keepdims=True)

            @pl.when(i == 0)
            def _():
                dg_ref[...] = part

            @pl.when(i > 0)
            def _():
                dg_ref[...] += part

    row = pl.BlockSpec((tm, D), lambda i, k: (i, 0))
    vec = pl.BlockSpec((1, D), lambda i, k: (0, 0))
    return pl.pallas_call(
        body, name=name, grid=(T // tm, nk),
        in_specs=[pl.BlockSpec((tm, tk), lambda i, k: (i, k))] * na
        + [pl.BlockSpec((D, tk), lambda i, k: (0, k))] * na + [row, vec, row],
        out_specs=[row, row, vec],
        out_shape=[jax.ShapeDtypeStruct((T, D), F32), jax.ShapeDtypeStruct((T, D), BF16),
                   jax.ShapeDtypeStruct((1, D), F32)],
        scratch_shapes=[pltpu.VMEM((tm, D), F32)], compiler_params=_params(2),
    )(*As, *Ws, x, g, dres)


def _tn_matmul(name, a, b, scale=1.0):
    T, M = a.shape
    N = b.shape[1]
    tm = _pick(M, (1024, 512, 256, 128))
    tn = _pick(N, (1408, 1280, 1024, 512, 256, 128))
    tk = _pick(T, (1024, 512, 256, 128))
    nk = T // tk

    def body(a_ref, b_ref, o_ref, acc):
        k = pl.program_id(2)

        @pl.when(k == 0)
        def _():
            acc[...] = jnp.zeros_like(acc)

        acc[...] += _tn(a_ref[...], b_ref[...])

        @pl.when(k == nk - 1)
        def _():
            o_ref[...] = (acc[...] * scale).astype(BF16)

    return pl.pallas_call(
        body, name=name, grid=(M // tm, N // tn, nk),
        in_specs=[pl.BlockSpec((tk, tm), lambda i, j, k: (k, i)), pl.BlockSpec((tk, tn), lambda i, j, k: (k, j))],
        out_specs=pl.BlockSpec((tm, tn), lambda i, j, k: (i, j)),
        out_shape=jax.ShapeDtypeStruct((M, N), BF16),
        scratch_shapes=[pltpu.VMEM((tm, tn), F32)], compiler_params=_params(3),
    )(a, b)


CONV_ROWS = 128
ROW_CHUNK = 32
LANE_CHUNK = 256


def _conv_fwd(name, z, w32, b, lg, lb, C):
    T = z.shape[0]
    tc = CONV_ROWS
    ntap = 31
    lc = _pick(C, (LANE_CHUNK, LANES))
    rpb = tc // HALO

    def body(zc_ref, zp_ref, w_ref, b_ref, lg_ref, lb_ref, yc_ref, ycv_ref, vbuf, ybuf):
        i = pl.program_id(0)
        zc = zc_ref[...]
        zp = zp_ref[...]
        vbuf[HALO:HALO + tc, :] = zc[:, :C] * _sigmoid(zc[:, C:])
        vbuf[0:HALO, :] = jnp.where(i > 0, zp[:, :C] * _sigmoid(zp[:, C:]), 0.0)
        for r0 in range(0, tc, ROW_CHUNK):
            for c0 in range(0, C, lc):
                acc = jnp.zeros((ROW_CHUNK, lc), F32) + b_ref[:, c0:c0 + lc]
                for k in range(ntap):
                    s = r0 + 2 + k
                    acc = acc + w_ref[k:k + 1, c0:c0 + lc] * vbuf[s:s + ROW_CHUNK, c0:c0 + lc]
                ybuf[r0:r0 + ROW_CHUNK, c0:c0 + lc] = acc
        y = ybuf[...]
        ycv_ref[...] = y
        mu = jnp.mean(y, axis=-1, keepdims=True)
        yc = y - mu
        rstd = lax.rsqrt(jnp.mean(yc * yc, axis=-1, keepdims=True) + EPS)
        ln = yc * rstd * lg_ref[...] + lb_ref[...]
        yc_ref[...] = (ln * _sigmoid(ln)).astype(BF16)

    vec = pl.BlockSpec((1, C), lambda i: (0, 0))
    return pl.pallas_call(
        body, name=name, grid=(T // tc,),
        in_specs=[pl.BlockSpec((tc, 2 * C), lambda i: (i, 0)),
                  pl.BlockSpec((HALO, 2 * C), lambda i: (jnp.maximum(i * rpb - 1, 0), 0)),
                  pl.BlockSpec((HALO, C), lambda i: (0, 0)), vec, vec, vec],
        out_specs=[pl.BlockSpec((tc, C), lambda i: (i, 0))] * 2,
        out_shape=[jax.ShapeDtypeStruct((T, C), BF16), jax.ShapeDtypeStruct((T, C), F32)],
        scratch_shapes=[pltpu.VMEM((tc + HALO, C), F32), pltpu.VMEM((tc, C), F32)],
        compiler_params=_params(1),
    )(z, z, w32, b, lg, lb)


def _conv_bwd(name, z, ycv, dycat, w32, lg, lb, C):
    T = z.shape[0]
    tc = CONV_ROWS
    ntap = 31
    lc = _pick(C, (LANE_CHUNK, LANES))
    rpb = tc // HALO
    nstep = T // tc
    nhb = T // HALO

    def ln_bwd(dyc, y, lgv, lbv):
        mu = jnp.mean(y, axis=-1, keepdims=True)
        yc = y - mu
        rstd = lax.rsqrt(jnp.mean(yc * yc, axis=-1, keepdims=True) + EPS)
        yn = yc * rstd
        ln = yn * lgv + lbv
        sg = _sigmoid(ln)
        dln = dyc * (sg * (1.0 + ln * (1.0 - sg)))
        dyn = dln * lgv
        dy = rstd * (dyn - jnp.mean(dyn, axis=-1, keepdims=True)
                     - yn * jnp.mean(dyn * yn, axis=-1, keepdims=True))
        return dy, dln, yn

    def body(zc_ref, zp_ref, y_ref, yn_ref, d_ref, dn_ref, w_ref, lg_ref, lb_ref,
             dz_ref, dw_ref, db_ref, dlg_ref, dlb_ref, vbuf, dbuf, dvbuf, dwacc):
        i = pl.program_id(0)
        lgv, lbv = lg_ref[...], lb_ref[...]
        zc = zc_ref[...]
        zp = zp_ref[...]
        a = zc[:, :C]
        sgt = _sigmoid(zc[:, C:])
        vbuf[HALO:HALO + tc, :] = a * sgt
        vbuf[0:HALO, :] = jnp.where(i > 0, zp[:, :C] * _sigmoid(zp[:, C:]), 0.0)
        dy, dln, yn = ln_bwd(d_ref[...], y_ref[...], lgv, lbv)
        dbuf[0:tc, :] = dy
        dyn_, _, _ = ln_bwd(dn_ref[...], yn_ref[...], lgv, lbv)
        dbuf[tc:tc + HALO, :] = jnp.where(i < nstep - 1, dyn_, 0.0)

        @pl.when(i == 0)
        def _():
            dwacc[...] = jnp.zeros_like(dwacc)
            db_ref[...] = jnp.zeros_like(db_ref)
            dlg_ref[...] = jnp.zeros_like(dlg_ref)
            dlb_ref[...] = jnp.zeros_like(dlb_ref)

        db_ref[...] += jnp.sum(dy, axis=0, keepdims=True)
        dlg_ref[...] += jnp.sum(dln * yn, axis=0, keepdims=True)
        dlb_ref[...] += jnp.sum(dln, axis=0, keepdims=True)

        for r0 in range(0, tc, ROW_CHUNK):
            for c0 in range(0, C, lc):
                dcur = dbuf[r0:r0 + ROW_CHUNK, c0:c0 + lc]
                acc = jnp.zeros((ROW_CHUNK, lc), F32)
                for k in range(ntap):
                    s = r0 + 30 - k
                    acc = acc + w_ref[k:k + 1, c0:c0 + lc] * dbuf[s:s + ROW_CHUNK, c0:c0 + lc]
                    sv = r0 + 2 + k
                    prod = dcur * vbuf[sv:sv + ROW_CHUNK, c0:c0 + lc]
                    red = prod[0:8]
                    for q in range(8, ROW_CHUNK, 8):
                        red = red + prod[q:q + 8]
                    dwacc[8 * k:8 * k + 8, c0:c0 + lc] += red
                dvbuf[r0:r0 + ROW_CHUNK, c0:c0 + lc] = acc
        dv = dvbuf[...]
        dz_ref[:, :C] = (dv * sgt).astype(BF16)
        dz_ref[:, C:] = (dv * a * sgt * (1.0 - sgt)).astype(BF16)

        @pl.when(i == nstep - 1)
        def _():
            for k in range(ntap):
                dw_ref[k:k + 1, :] = jnp.sum(dwacc[8 * k:8 * k + 8, :], axis=0, keepdims=True)
            dw_ref[ntap:HALO, :] = jnp.zeros((HALO - ntap, C), F32)

    vec = pl.BlockSpec((1, C), lambda i: (0, 0))
    cur = pl.BlockSpec((tc, C), lambda i: (i, 0))
    nxt = pl.BlockSpec((HALO, C), lambda i: (jnp.minimum((i + 1) * rpb, nhb - 1), 0))
    return pl.pallas_call(
        body, name=name, grid=(nstep,),
        in_specs=[pl.BlockSpec((tc, 2 * C), lambda i: (i, 0)),
                  pl.BlockSpec((HALO, 2 * C), lambda i: (jnp.maximum(i * rpb - 1, 0), 0)),
                  cur, nxt, cur, nxt, pl.BlockSpec((HALO, C), lambda i: (0, 0)), vec, vec],
        out_specs=[pl.BlockSpec((tc, 2 * C), lambda i: (i, 0)), pl.BlockSpec((HALO, C), lambda i: (0, 0)),
                   vec, vec, vec],
        out_shape=[jax.ShapeDtypeStruct((T, 2 * C), BF16), jax.ShapeDtypeStruct((HALO, C), F32),
                   jax.ShapeDtypeStruct((1, C), F32), jax.ShapeDtypeStruct((1, C), F32),
                   jax.ShapeDtypeStruct((1, C), F32)],
        scratch_shapes=[pltpu.VMEM((tc + HALO, C), F32), pltpu.VMEM((tc + HALO, C), F32),
                        pltpu.VMEM((tc, C), F32), pltpu.VMEM((8 * HALO, C), F32)],
        compiler_params=_params(1),
    )(z, z, ycv, ycv, dycat, dycat, w32, lg, lb)


def _seg_sum(u, bmat):
    hi = u.astype(BF16)
    lo = (u - hi.astype(F32)).astype(BF16)
    return _nn(hi, bmat) + _nn(lo, bmat)


def _attn_prep(name, z, gq, gk, bmat, A, c0, hd):
    T = z.shape[0]
    tm = _pick(T, (256, 128))

    def body(zq_ref, zk_ref, zv_ref, gq_ref, gk_ref, b_ref, o_ref):
        bm = b_ref[...]
        for idx, (z_ref, g_ref) in enumerate(((zq_ref, gq_ref), (zk_ref, gk_ref))):
            zv = z_ref[...]
            r = lax.rsqrt(_seg_sum(zv * zv, bm) * (1.0 / hd) + EPS)
            o_ref[:, idx * A:(idx + 1) * A] = (zv * r * g_ref[...]).astype(BF16)
        o_ref[:, 2 * A:] = zv_ref[...].astype(BF16)

    vec = pl.BlockSpec((1, A), lambda i: (0, 0))
    return pl.pallas_call(
        body, name=name, grid=(T // tm,),
        in_specs=[pl.BlockSpec((tm, A), lambda i: (i, c0)), pl.BlockSpec((tm, A), lambda i: (i, c0 + 1)),
                  pl.BlockSpec((tm, A), lambda i: (i, c0 + 2)), vec, vec,
                  pl.BlockSpec((A, A), lambda i: (0, 0))],
        out_specs=pl.BlockSpec((tm, 3 * A), lambda i: (i, 0)),
        out_shape=jax.ShapeDtypeStruct((T, 3 * A), BF16), compiler_params=_params(1),
    )(z, z, z, gq, gk, bmat)


def _attn_geometry(T, d, A):
    L = T // d
    Lc = min(L, 1024)
    return L, Lc, Lc // WINDOW, L // Lc, A // LANES


QK_SCALE = 0.125
ATTN_UNROLL = 4


def _fill_bias(bias, sl_ref, hp, d):
    qi = lax.broadcasted_iota(jnp.int32, (WINDOW, 2 * WINDOW), 0)
    kj = lax.broadcasted_iota(jnp.int32, (WINDOW, 2 * WINDOW), 1)
    dist = WINDOW + qi - kj
    inband = (dist >= 0) & (dist <= WINDOW)
    distf = dist.astype(F32)
    for hh in range(2):
        b = jnp.where(inband, -(sl_ref[2 * hp + hh] * d) * distf, NEG)
        bias[2 * hh + 1] = b
        bias[2 * hh] = jnp.where(kj >= WINDOW, b, NEG)


def _attn_fwd(name, qkv, slopes, d, A):
    T = qkv.shape[0]
    L, Lc, nb, nch, hpn = _attn_geometry(T, d, A)
    cpr = 3 * hpn
    qv = qkv.reshape(L, d * 3 * A)

    def body(sl_ref, q_ref, k_ref, kh_ref, v_ref, vh_ref, o_ref, l_ref, kbuf, vbuf, bias):
        cb, ch = pl.program_id(0), pl.program_id(1)
        hp = cb % hpn
        kbuf[0:WINDOW, :] = kh_ref[...]
        kbuf[WINDOW:, :] = k_ref[...]
        vbuf[0:WINDOW, :] = vh_ref[...]
        vbuf[WINDOW:, :] = v_ref[...]
        _fill_bias(bias, sl_ref, hp, d)
        lane = lax.broadcasted_iota(jnp.int32, (1, LANES), 1)
        first = lane < (LANES // 2)

        def step(nl, carry):
            r0 = pl.multiple_of(nl * WINDOW, WINDOW)
            later = jnp.where(ch * nb + nl > 0, 1, 0)
            qb = q_ref[pl.ds(r0, WINDOW), :]
            k2 = kbuf[pl.ds(r0, 2 * WINDOW), :]
            v2 = vbuf[pl.ds(r0, 2 * WINDOW), :]
            res = []
            for hh in range(2):
                mh = first if hh == 0 else jnp.logical_not(first)
                s = _nt(jnp.where(mh, qb, jnp.zeros_like(qb)), k2) + bias[2 * hh + later]
                mx = jnp.max(s, axis=-1, keepdims=True)
                p = jnp.exp(s - mx)
                den = jnp.sum(p, axis=-1, keepdims=True)
                res.append((_nn(p.astype(BF16), v2) / den, mx + jnp.log(den)))
            o_ref[pl.ds(r0, WINDOW), :] = jnp.where(first, res[0][0], res[1][0])
            l_ref[pl.ds(r0, WINDOW), :] = jnp.where(first, res[0][1], res[1][1])
            return carry

        lax.fori_loop(0, nb, step, 0, unroll=min(ATTN_UNROLL, nb))

    def cmap(which):
        return lambda cb, ch: (ch, (cb // hpn) * cpr + which * hpn + cb % hpn)

    def hmap(which):
        return lambda cb, ch: (jnp.maximum(ch * nb - 1, 0), (cb // hpn) * cpr + which * hpn + cb % hpn)

    blk = lambda m: pl.BlockSpec((Lc, LANES), m)
    hblk = lambda m: pl.BlockSpec((WINDOW, LANES), m)
    omap = lambda cb, ch: (ch, cb)
    o, l = pl.pallas_call(
        body, name=name, grid=(d * hpn, nch),
        in_specs=[pl.BlockSpec(memory_space=pltpu.SMEM), blk(cmap(0)), blk(cmap(1)), hblk(hmap(1)),
                  blk(cmap(2)), hblk(hmap(2))],
        out_specs=[blk(omap), blk(omap)],
        out_shape=[jax.ShapeDtypeStruct((L, d * A), F32)] * 2,
        scratch_shapes=[pltpu.VMEM((Lc + WINDOW, LANES), BF16)] * 2
        + [pltpu.VMEM((4, WINDOW, 2 * WINDOW), F32)], compiler_params=_params(2),
    )(slopes, qv, qv, qv, qv, qv)
    return o.reshape(T, A), l.reshape(T, A)


def _attn_combine(name, os_, ls_):
    T, A = os_[0].shape
    tm = _pick(T, (512, 256, 128))
    nbr = len(os_)

    def body(*refs):
        o_refs, l_refs = refs[:nbr], refs[nbr:2 * nbr]
        y_ref, lg_ref = refs[2 * nbr:]
        ls = [r[...] for r in l_refs]
        mx = ls[0]
        for v in ls[1:]:
            mx = jnp.maximum(mx, v)
        es = [jnp.exp(v - mx) for v in ls]
        den = es[0]
        for e in es[1:]:
            den = den + e
        out = es[0] * o_refs[0][...]
        for e, o_ref in zip(es[1:], o_refs[1:]):
            out = out + e * o_ref[...]
        y_ref[...] = (out / den).astype(BF16)
        lg_ref[...] = mx + jnp.log(den)

    blk = pl.BlockSpec((tm, A), lambda i: (i, 0))
    return pl.pallas_call(
        body, name=name, grid=(T // tm,), in_specs=[blk] * (2 * nbr), out_specs=[blk, blk],
        out_shape=[jax.ShapeDtypeStruct((T, A), BF16), jax.ShapeDtypeStruct((T, A), F32)],
        compiler_params=_params(1),
    )(*os_, *ls_)


def _attn_bwd(name, qkv, dycatb, yatt, lg, slopes, d, A, catw, catoff):
    T = qkv.shape[0]
    L, Lc, nb, nch, hpn = _attn_geometry(T, d, A)
    cpr = 3 * hpn
    qv = qkv.reshape(L, d * 3 * A)
    dov = dycatb.reshape(L, d * catw)
    ov = yatt.reshape(L, d * A)
    lv = lg.reshape(L, d * A)
    cw, co = catw // LANES, catoff // LANES

    def body(sl_ref, q_ref, k_ref, kh_ref, v_ref, vh_ref, do_ref, o_ref, l_ref,
             dq_ref, dk_ref, dv_ref, kbuf, vbuf, bias):
        cb, ch = pl.program_id(0), pl.program_id(1)
        hp = cb % hpn
        kbuf[0:WINDOW, :] = kh_ref[...]
        kbuf[WINDOW:, :] = k_ref[...]
        vbuf[0:WINDOW, :] = vh_ref[...]
        vbuf[WINDOW:, :] = v_ref[...]
        _fill_bias(bias, sl_ref, hp, d)
        lane = lax.broadcasted_iota(jnp.int32, (1, LANES), 1)
        first = lane < (LANES // 2)

        @pl.when(ch == 0)
        def _():
            dk_ref[...] = jnp.zeros_like(dk_ref)
            dv_ref[...] = jnp.zeros_like(dv_ref)

        def step(nl, carry):
            r0 = pl.multiple_of(nl * WINDOW, WINDOW)
            n = ch * nb + nl
            later = jnp.where(n > 0, 1, 0)
            qb = q_ref[pl.ds(r0, WINDOW), :]
            k2 = kbuf[pl.ds(r0, 2 * WINDOW), :]
            v2 = vbuf[pl.ds(r0, 2 * WINDOW), :]
            dob = do_ref[pl.ds(r0, WINDOW), :]
            dd = dob.astype(F32) * o_ref[pl.ds(r0, WINDOW), :].astype(F32)
            lb = l_ref[pl.ds(r0, WINDOW), :]
            dk2 = jnp.zeros((2 * WINDOW, LANES), F32)
            dv2 = jnp.zeros((2 * WINDOW, LANES), F32)
            dqs = []
            for hh in range(2):
                mh = first if hh == 0 else jnp.logical_not(first)
                qh = jnp.where(mh, qb, jnp.zeros_like(qb))
                doh = jnp.where(mh, dob, jnp.zeros_like(dob))
                lcol = lb[:, hh * (LANES // 2):hh * (LANES // 2) + 1]
                p = jnp.exp(_nt(qh, k2) + bias[2 * hh + later] - lcol)
                dcol = jnp.sum(jnp.where(mh, dd, 0.0), axis=-1, keepdims=True)
                ds = (p * (_nt(doh, v2) - dcol)).astype(BF16)
                dqs.append(_nn(ds, k2))
                dk2 = dk2 + _tn(ds, qh)
                dv2 = dv2 + _tn(p.astype(BF16), doh)
            dq_ref[pl.ds(r0, WINDOW), :] = jnp.where(first, dqs[0], dqs[1])
            g0 = pl.multiple_of(n * WINDOW, WINDOW)
            dk_ref[pl.ds(g0, WINDOW), :] += dk2[WINDOW:]
            dv_ref[pl.ds(g0, WINDOW), :] += dv2[WINDOW:]
            gp = pl.multiple_of(jnp.maximum(n - 1, 0) * WINDOW, WINDOW)
            dk_ref[pl.ds(gp, WINDOW), :] += dk2[:WINDOW]
            dv_ref[pl.ds(gp, WINDOW), :] += dv2[:WINDOW]
            return carry

        lax.fori_loop(0, nb, step, 0, unroll=min(ATTN_UNROLL, nb))

    def cmap(which):
        return lambda cb, ch: (ch, (cb // hpn) * cpr + which * hpn + cb % hpn)

    def hmap(which):
        return lambda cb, ch: (jnp.maximum(ch * nb - 1, 0), (cb // hpn) * cpr + which * hpn + cb % hpn)

    blk = lambda m: pl.BlockSpec((Lc, LANES), m)
    hblk = lambda m: pl.BlockSpec((WINDOW, LANES), m)
    omap = lambda cb, ch: (ch, cb)
    full = pl.BlockSpec((L, LANES), lambda cb, ch: (0, cb))
    dq, dk, dv = pl.pallas_call(
        body, name=name, grid=(d * hpn, nch),
        in_specs=[pl.BlockSpec(memory_space=pltpu.SMEM), blk(cmap(0)), blk(cmap(1)), hblk(hmap(1)),
                  blk(cmap(2)), hblk(hmap(2)),
                  blk(lambda cb, ch: (ch, (cb // hpn) * cw + co + cb % hpn)), blk(omap), blk(omap)],
        out_specs=[blk(omap), full, full],
        out_shape=[jax.ShapeDtypeStruct((L, d * A), F32)] * 3,
        scratch_shapes=[pltpu.VMEM((Lc + WINDOW, LANES), BF16)] * 2
        + [pltpu.VMEM((4, WINDOW, 2 * WINDOW), F32)], compiler_params=_params(2),
    )(slopes, qv, qv, qv, qv, qv, dov, ov, lv)
    return dq.reshape(T, A), dk.reshape(T, A), dv.reshape(T, A)


def _attn_bwd_combine(name, dqs, dks, dvs, z, gq, gk, bmat, fmat, A, c0, hd):
    T = z.shape[0]
    tm = _pick(T, (256, 128))
    nbr = len(dqs)

    def body(*refs):
        dq_refs, dk_refs, dv_refs = refs[:nbr], refs[nbr:2 * nbr], refs[2 * nbr:3 * nbr]
        zq_ref, zk_ref, gq_ref, gk_ref, b_ref, f_ref, dz_ref, dgq_ref, dgk_ref = refs[3 * nbr:]
        i = pl.program_id(0)
        bm = b_ref[...]

        def tot(rs):
            t = rs[0][...]
            for r in rs[1:]:
                t = t + r[...]
            return t

        for idx, (d_refs, z_ref, g_ref, dg_ref, gscale) in enumerate(
                ((dq_refs, zq_ref, gq_ref, dgq_ref, QK_SCALE), (dk_refs, zk_ref, gk_ref, dgk_ref, 1.0))):
            dy = tot(d_refs)
            zv = z_ref[...]
            r = lax.rsqrt(_seg_sum(zv * zv, bm) * (1.0 / hd) + EPS)
            gd = dy * g_ref[...]
            mean = _seg_sum(gd * zv, bm) * (1.0 / hd)
            dz_ref[:, idx * A:(idx + 1) * A] = (r * gd - zv * (r * r * r) * mean).astype(BF16)
            part = jnp.sum(dy * zv * r, axis=0, keepdims=True) * gscale

            @pl.when(i == 0)
            def _():
                dg_ref[...] = part

            @pl.when(i > 0)
            def _():
                dg_ref[...] += part

        dz_ref[:, 2 * A:] = tot(dv_refs).astype(BF16)

        @pl.when(i == T // tm - 1)
        def _():
            fm = f_ref[...]
            for dg_ref in (dgq_ref, dgk_ref):
                v = jnp.broadcast_to(dg_ref[...], (8, A))
                hi = v.astype(BF16)
                mid = (v - hi.astype(F32)).astype(BF16)
                lo = (v - hi.astype(F32) - mid.astype(F32)).astype(BF16)
                dg_ref[...] = (_nn(hi, fm) + _nn(mid, fm) + _nn(lo, fm))[0:1]

    blk = pl.BlockSpec((tm, A), lambda i: (i, 0))
    vec = pl.BlockSpec((1, A), lambda i: (0, 0))
    return pl.pallas_call(
        body, name=name, grid=(T // tm,),
        in_specs=[blk] * (3 * nbr) + [pl.BlockSpec((tm, A), lambda i: (i, c0)),
                                      pl.BlockSpec((tm, A), lambda i: (i, c0 + 1)), vec, vec,
                                      pl.BlockSpec((A, A), lambda i: (0, 0)),
                                      pl.BlockSpec((A, A), lambda i: (0, 0))],
        out_specs=[pl.BlockSpec((tm, 3 * A), lambda i: (i, 0)), vec, vec],
        out_shape=[jax.ShapeDtypeStruct((T, 3 * A), BF16), jax.ShapeDtypeStruct((1, A), F32),
                   jax.ShapeDtypeStruct((1, A), F32)],
        compiler_params=_params(1),
    )(*dqs, *dks, *dvs, z, z, gq, gk, bmat, fmat)


def _local_step(x, tgt, S, comm, hd):
    T, D = x.shape
    C = S["conv_b_dw"].shape[1]
    A = C
    H = A // hd
    Dmix = C + A
    c0 = (2 * C) // A
    slopes = 2.0 ** (-ALIBI_MAX_BIAS * jnp.arange(1, H + 1, dtype=F32) / H)
    seg = jnp.arange(A) // hd
    bmat = (seg[:, None] == seg[None, :]).astype(BF16)
    pos_in_head = jnp.arange(A) % hd
    fmat = (pos_in_head[:, None] == pos_in_head[None, :]).astype(BF16)
    gq = jnp.tile(S["q_norm_g"], (1, H)) * QK_SCALE
    gk = jnp.tile(S["k_norm_g"], (1, H))

    wg1, wu1 = comm.weights(("ffn1_w_gate", "ffn1_w_up"), None)
    h1, gate1, up1, a1 = _norm_matmul("ffn1_up", x, comm.tie(S["ffn1_norm_g"]), [wg1, wu1], True)
    wd1, win, w32 = comm.weights(("ffn1_w_down", "w_in", "conv_w32"), a1)
    x1 = _matmul_res("ffn1_down", a1, wd1, x, 0.5)
    h2, z = _norm_matmul("mix_in", x1, S["mix_norm_g"], [win], False)
    yc, ycv = _conv_fwd("conv_fwd", z, w32, S["conv_b_dw"], S["conv_ln_g"], S["conv_ln_b"], C)
    qkv = _attn_prep("attn_prep", z, gq, gk, bmat, A, c0, hd)
    os_, ls_ = [], []
    for d in DILATIONS:
        o, l = _attn_fwd("attn_fwd_d%d" % d, qkv, slopes, d, A)
        os_.append(o)
        ls_.append(l)
    yatt, lg = _attn_combine("attn_combine", os_, ls_)
    ycat = jnp.concatenate([yc, yatt], axis=1)
    wout, wg2, wu2, wd2 = comm.weights(("w_out", "ffn2_w_gate", "ffn2_w_up", "ffn2_w_down"), yatt)
    x2 = _matmul_res("mix_out", ycat, wout, x1, 1.0)
    h3, gate2, up2, a2 = _norm_matmul("ffn2_up", x2, S["ffn2_norm_g"], [wg2, wu2], True)
    dx3, dx3b, lossvec = _matmul_res("ffn2_down_loss", a2, wd2, x2, 0.5, tgt=tgt)

    G = {}
    dgate2, dup2 = _nt_matmul("ffn2_dact", dx3b, wd2, 0.5, gate2, up2)
    comm.reduce_begin("ffn2", {"ffn2_w_down": _tn_matmul("ffn2_dwd", a2, dx3b, 0.5),
                               "ffn2_w_gate": _tn_matmul("ffn2_dwg", h3, dgate2),
                               "ffn2_w_up": _tn_matmul("ffn2_dwu", h3, dup2)})
    dx2, dx2b, G["ffn2_norm_g"] = _nt_rms_bwd("ffn2_dx", [dgate2, dup2], [wg2, wu2],
                                              x2, comm.tie(S["ffn2_norm_g"]), dx3)
    dwout = _tn_matmul("mix_dwout", ycat, dx2b)
    dycat, dycatb = _nt_matmul("mix_dycat", dx2b, wout)
    dzc, G["conv_w32"], G["conv_b_dw"], G["conv_ln_g"], G["conv_ln_b"] = _conv_bwd(
        "conv_bwd", z, ycv, dycat, w32, S["conv_ln_g"], S["conv_ln_b"], C)
    dqs, dks, dvs = [], [], []
    for d in DILATIONS:
        dq, dk, dv = _attn_bwd("attn_bwd_d%d" % d, qkv, dycatb, yatt, lg, slopes, d, A, Dmix, C)
        dqs.append(dq)
        dks.append(dk)
        dvs.append(dv)
    dzqkv, G["q_norm_g"], G["k_norm_g"] = _attn_bwd_combine(
        "attn_bwd_combine", dqs, dks, dvs, z, gq, gk, bmat, fmat, A, c0, hd)
    comm.reduce_end("ffn2", dzqkv)
    dz = jnp.concatenate([dzc, dzqkv], axis=1)
    comm.reduce_begin("mix", {"w_out": dwout, "w_in": _tn_matmul("mix_dwin", h2, dz)})
    dx1, dx1b, G["mix_norm_g"] = _nt_rms_bwd("mix_dx", [dz], [win], x1, comm.tie(S["mix_norm_g"]), dx2)
    dgate1, dup1 = _nt_matmul("ffn1_dact", dx1b, wd1, 0.5, gate1, up1)
    dwd1 = _tn_matmul("ffn1_dwd", a1, dx1b, 0.5)
    dwg1 = _tn_matmul("ffn1_dwg", h1, dgate1)
    dwu1 = _tn_matmul("ffn1_dwu", h1, dup1)
    comm.reduce_end("mix", dwu1)
    comm.reduce_begin("ffn1", {"ffn1_w_down": dwd1, "ffn1_w_gate": dwg1, "ffn1_w_up": dwu1})
    dx0, _, G["ffn1_norm_g"] = _nt_rms_bwd("ffn1_dx", [dgate1, dup1], [wg1, wu1],
                                           x, comm.tie(S["ffn1_norm_g"]), dx1)
    comm.reduce_end("ffn1", dx0)
    return lossvec, dx0, G


BIG = (("ffn1_w_gate", 1), ("ffn1_w_up", 1), ("ffn1_w_down", 0), ("w_in", 1), ("w_out", 0),
       ("ffn2_w_gate", 1), ("ffn2_w_up", 1), ("ffn2_w_down", 0))
AXIS = dict(BIG)
FLIPS = ((1, 0), (0, 1), (1, 1))
HBM = pl.BlockSpec(memory_space=pltpu.HBM)
SEM = pl.BlockSpec(memory_space=pltpu.SEMAPHORE)
EFFECT = pltpu.SideEffectType.DATAFLOW_SIDE_EFFECTING
TOKEN = jax.ShapeDtypeStruct((8, LANES), F32)


def _window(ref, shape, axis, slab=None, half=None):
    idx = [pl.ds(0, shape[0]), pl.ds(0, shape[1])]
    if slab is not None:
        n = shape[axis] // 4
        idx[axis] = pl.ds(pl.multiple_of(slab * n, 8), n)
    if half is not None:
        hs = shape[1 - axis] // 2
        idx[1 - axis] = pl.ds(pl.multiple_of(half * hs, 8), hs)
    return ref.at[idx[0], idx[1]]


def _position():
    return lax.axis_index("x"), lax.axis_index("y"), lax.axis_index("c")


def _half_shape(shape, axis):
    return (shape[0] // 2, shape[1]) if axis == 1 else (shape[0], shape[1] // 2)


def _slab_shape(shape, axis):
    return (shape[0], shape[1] // 4) if axis == 1 else (shape[0] // 4, shape[1])


def _piece_shape(shape, axis):
    return _half_shape(_slab_shape(shape, axis), axis)


def _full_shape(shard, axis):
    return (shard.shape[0], shard.shape[1] * 4) if axis == 1 else (shard.shape[0] * 4, shard.shape[1])


def _hbm(a):
    return pltpu.with_memory_space_constraint(a, pltpu.HBM)


def _remote(src, dst, send_sem, recv_sem, to):
    return pltpu.make_async_remote_copy(src_ref=src, dst_ref=dst, send_sem=send_sem, recv_sem=recv_sem,
                                        device_id=to, device_id_type=MESH)


def _place(name, pos, w, axis):
    R, Cc = w.shape
    tr = _pick(R, (256, 128, 64, 32, 16))
    nrb = R // tr

    def body(pos_ref, w_ref, o_ref):
        o_ref[...] = w_ref[...].astype(BF16)

    omap = (lambda i, p: (i, p[0])) if axis == 1 else (lambda i, p: (p[0] * nrb + i, 0))
    return pl.pallas_call(
        body, name=name,
        grid_spec=pltpu.PrefetchScalarGridSpec(
            num_scalar_prefetch=1, grid=(nrb,), in_specs=[pl.BlockSpec((tr, Cc), lambda i, p: (i, 0))],
            out_specs=pl.BlockSpec((tr, Cc), omap)),
        out_shape=jax.ShapeDtypeStruct(_full_shape(w, axis), BF16), compiler_params=_params(1),
    )(pos, w)


def _gather_now(name, axes, fulls):
    nt = len(fulls)
    shapes = [f.shape for f in fulls]

    def body(*refs):
        outs, token = refs[nt:2 * nt], refs[2 * nt]
        send_sems, recv_sems = refs[2 * nt + 1:]
        x, y, c = _position()
        j0 = 2 * x + y
        sib = (x, y, 1 - c)

        def copy(t, k, slab, half, to):
            win = _window(outs[t], shapes[t], axes[t], slab=slab, half=half)
            return _remote(win, win, send_sems.at[t, k], recv_sems.at[t, k], to)

        sends = []
        for k, (fx, fy) in enumerate(FLIPS):
            for t in range(nt):
                cp = copy(t, k, j0, c, (x ^ fx, y ^ fy, c))
                cp.start()
                sends.append(cp)
        for k, (fx, fy) in enumerate(FLIPS):
            js = 2 * (x ^ fx) + (y ^ fy)
            for t in range(nt):
                copy(t, k, js, c, sib).wait_recv()
                cp = copy(t, 3 + k, js, c, sib)
                cp.start()
                sends.append(cp)
        for k, (fx, fy) in enumerate(FLIPS):
            js = 2 * (x ^ fx) + (y ^ fy)
            for t in range(nt):
                copy(t, 3 + k, js, 1 - c, sib).wait_recv()
        for cp in sends:
            cp.wait_send()
        token[...] = jnp.zeros_like(token)

    res = pl.pallas_call(
        body, name=name, in_specs=[ANY] * nt,
        out_specs=[ANY] * nt + [pl.BlockSpec(memory_space=pltpu.VMEM)],
        out_shape=[jax.ShapeDtypeStruct(s, BF16) for s in shapes] + [TOKEN],
        input_output_aliases={t: t for t in range(nt)},
        scratch_shapes=[pltpu.SemaphoreType.DMA((nt, 6)), pltpu.SemaphoreType.DMA((nt, 6))],
    )(*fulls)
    return list(res[:nt]), res[nt]


def _split_start(name, arrays, ncopies, plan):
    na = len(arrays)

    def body(*refs):
        ins = refs[:na]
        send_sems, recv_sems = refs[na], refs[na + 1]
        token = refs[-1]
        x, y, c = _position()
        for i, (src, dst, to) in enumerate(plan(ins, x, y, c)):
            _remote(src, dst, send_sems.at[i], recv_sems.at[i], to).start()
        token[...] = jnp.zeros_like(token)

    res = pl.pallas_call(
        body, name=name, in_specs=[HBM] * na,
        out_specs=tuple([SEM, SEM] + [HBM] * na + [pl.BlockSpec(memory_space=pltpu.VMEM)]),
        out_shape=tuple([pltpu.SemaphoreType.DMA((ncopies,)), pltpu.SemaphoreType.DMA((ncopies,))]
                        + [pltpu.HBM(a.shape, a.dtype) for a in arrays] + [TOKEN]),
        input_output_aliases={i: 2 + i for i in range(na)},
        compiler_params=pltpu.CompilerParams(has_side_effects=EFFECT),
    )(*[_hbm(a) for a in arrays])
    return (res[0], res[1]), list(res[2:2 + na]), res[-1]


def _split_wait(name, arrays, sems, after, plan):
    na = len(arrays)

    def body(*refs):
        ins = refs[:na]
        send_sems, recv_sems = refs[na], refs[na + 1]
        x, y, c = _position()
        for i, (src, dst, to) in enumerate(plan(ins, x, y, c)):
            cp = _remote(src, dst, send_sems.at[i], recv_sems.at[i], to)
            cp.wait_send()
            cp.wait_recv()

    res = pl.pallas_call(
        body, name=name, in_specs=[HBM] * na + [SEM, SEM, ANY],
        out_specs=tuple([HBM] * na), out_shape=tuple(pltpu.HBM(a.shape, a.dtype) for a in arrays),
        input_output_aliases={i: i for i in range(na)},
        compiler_params=pltpu.CompilerParams(has_side_effects=EFFECT),
    )(*arrays, *sems, after)
    return list(res)


def _gather_plan(axes, shapes, conv_shape):
    nt = len(axes)

    def plan(refs, x, y, c):
        j0 = 2 * x + y
        out = []
        for fx, fy in FLIPS:
            to = (x ^ fx, y ^ fy, c)
            for t in range(nt):
                win = _window(refs[t], shapes[t], axes[t], slab=j0, half=c)
                out.append((win, win, to))
            if conv_shape is not None:
                win = _window(refs[nt], conv_shape, 1, slab=j0)
                out.append((win, win, to))
        return out

    return plan


def _gather_finish(name, axes, fulls):
    nt = len(axes)
    shapes = [f.shape for f in fulls]

    def body(*refs):
        outs = refs[nt:2 * nt]
        send_sems, recv_sems = refs[2 * nt:]
        x, y, c = _position()
        sib = (x, y, 1 - c)
        cps = []
        for k, (fx, fy) in enumerate(FLIPS):
            js = 2 * (x ^ fx) + (y ^ fy)
            for t in range(nt):
                landed = _window(outs[t], shapes[t], axes[t], slab=js, half=c)
                cp = _remote(landed, landed, send_sems.at[t, k], recv_sems.at[t, k], sib)
                cp.start()
                cps.append(cp)
        for k, (fx, fy) in enumerate(FLIPS):
            js = 2 * (x ^ fx) + (y ^ fy)
            for t in range(nt):
                other = _window(outs[t], shapes[t], axes[t], slab=js, half=1 - c)
                _remote(other, other, send_sems.at[t, k], recv_sems.at[t, k], sib).wait_recv()
        for cp in cps:
            cp.wait_send()

    res = pl.pallas_call(
        body, name=name, in_specs=[ANY] * nt, out_specs=[ANY] * nt,
        out_shape=[jax.ShapeDtypeStruct(f.shape, f.dtype) for f in fulls],
        input_output_aliases={t: t for t in range(nt)},
        scratch_shapes=[pltpu.SemaphoreType.DMA((nt, 3)), pltpu.SemaphoreType.DMA((nt, 3))],
    )(*fulls)
    return list(res)


def _pair_exchange(name, srcs, windows, out_shapes, dtype):
    nt = len(srcs)

    def body(*refs):
        ins, outs = refs[:nt], refs[nt:2 * nt]
        send_sems, recv_sems = refs[2 * nt:]
        x, y, c = _position()
        cps = []
        for t in range(nt):
            cp = _remote(windows[t](ins[t], c), outs[t], send_sems.at[t], recv_sems.at[t], (x, y, 1 - c))
            cp.start()
            cps.append(cp)
        for cp in cps:
            cp.wait()

    return pl.pallas_call(
        body, name=name, in_specs=[ANY] * nt, out_specs=[ANY] * nt,
        out_shape=[jax.ShapeDtypeStruct(s, dtype) for s in out_shapes],
        scratch_shapes=[pltpu.SemaphoreType.DMA((nt,)), pltpu.SemaphoreType.DMA((nt,))],
    )(*srcs)


def _scatter_plan(axes, shapes):
    nt = len(axes)

    def plan(refs, x, y, c):
        out = []
        for k, (fx, fy) in enumerate(FLIPS):
            js = 2 * (x ^ fx) + (y ^ fy)
            for t in range(nt):
                src = _window(refs[t], _half_shape(shapes[t], axes[t]), axes[t], slab=js)
                out.append((src, refs[nt + t].at[k], (x ^ fx, y ^ fy, c)))
        return out

    return plan


def _gather_small(packed):
    R, Cc = packed.shape

    def body(p_ref, o_ref, send_sems, recv_sems, loc_sem):
        x, y, c = _position()
        me = 4 * x + 2 * y + c
        mine = pltpu.make_async_copy(p_ref, o_ref.at[me], loc_sem)
        mine.start()
        cps = []
        for k in range(1, 8):
            fx, fy, fc = (k >> 2) & 1, (k >> 1) & 1, k & 1
            cp = pltpu.make_async_remote_copy(
                src_ref=p_ref, dst_ref=o_ref.at[me], send_sem=send_sems.at[k - 1], recv_sem=recv_sems.at[k - 1],
                device_id=(x ^ fx, y ^ fy, c ^ fc), device_id_type=MESH)
            cp.start()
            cps.append(cp)
        for cp in cps:
            cp.wait()
        mine.wait()

    return pl.pallas_call(
        body, name="gather_small_grads", in_specs=[ANY], out_specs=ANY,
        out_shape=jax.ShapeDtypeStruct((8, R, Cc), F32),
        scratch_shapes=[pltpu.SemaphoreType.DMA((7,)), pltpu.SemaphoreType.DMA((7,)), pltpu.SemaphoreType.DMA],
    )(packed)


def _sum_slots(name, slots):
    n, R, Cc = slots.shape

    def body(s_ref, o_ref):
        t = s_ref[0]
        for i in range(1, n):
            t = t + s_ref[i]
        o_ref[...] = t

    return pl.pallas_call(
        body, name=name, grid=(1,), in_specs=[pl.BlockSpec((n, R, Cc), lambda i: (0, 0, 0))],
        out_specs=pl.BlockSpec((R, Cc), lambda i: (0, 0)), out_shape=jax.ShapeDtypeStruct((R, Cc), F32),
        compiler_params=_params(1),
    )(slots)


def _pair_sum(name, pos, g, land, shape, axis):
    hshape = _half_shape(shape, axis)
    R, Cc = hshape
    tr = _pick(R, (256, 128, 64, 32, 16))
    nrb = R // tr

    def body(pos_ref, g_ref, l_ref, o_ref):
        o_ref[...] = (g_ref[...].astype(F32) + l_ref[...].astype(F32)).astype(BF16)

    if axis == 1:
        gmap = lambda i, p: (p[1] * nrb + i, 0)
    else:
        gmap = lambda i, p: (i, p[1])
    blk = pl.BlockSpec((tr, Cc), lambda i, p: (i, 0))
    return pl.pallas_call(
        body, name=name,
        grid_spec=pltpu.PrefetchScalarGridSpec(
            num_scalar_prefetch=1, grid=(nrb,), in_specs=[pl.BlockSpec((tr, Cc), gmap), blk], out_specs=blk),
        out_shape=jax.ShapeDtypeStruct(hshape, BF16), compiler_params=_params(1),
    )(pos, g, land)


def _chip_sum(name, pos, sb, land, shape, axis):
    hshape = _half_shape(shape, axis)
    pshape = _piece_shape(shape, axis)
    R, Cc = pshape
    tr = _pick(R, (256, 128, 64, 32, 16))
    nrb = R // tr

    def body(pos_ref, s_ref, l_ref, o_ref):
        t = s_ref[...].astype(F32)
        for k in range(3):
            t = t + l_ref[k].astype(F32)
        o_ref[...] = t

    if axis == 1:
        smap = lambda i, p: (i, p[0])
    else:
        smap = lambda i, p: (p[0] * nrb + i, 0)
    return pl.pallas_call(
        body, name=name,
        grid_spec=pltpu.PrefetchScalarGridSpec(
            num_scalar_prefetch=1, grid=(nrb,),
            in_specs=[pl.BlockSpec((tr, Cc), smap), pl.BlockSpec((3, tr, Cc), lambda i, p: (0, i, 0))],
            out_specs=pl.BlockSpec((tr, Cc), lambda i, p: (i, 0))),
        out_shape=jax.ShapeDtypeStruct(pshape, F32), compiler_params=_params(1),
    )(pos, sb, land)


def _adam_math(w, g, m, v):
    m = ADAM_B1 * m + (1.0 - ADAM_B1) * g
    v = ADAM_B2 * v + (1.0 - ADAM_B2) * (g * g)
    m_hat = m / (1.0 - ADAM_B1 ** ADAM_STEP)
    v_hat = v / (1.0 - ADAM_B2 ** ADAM_STEP)
    delta = -ADAM_LR * (m_hat / (jnp.sqrt(v_hat) + ADAM_EPS) + ADAM_WD * w)
    return delta, m, v


def _adamw_halves(name, pos, w, m, v, mine, theirs, axis):
    R, Cc = w.shape
    hr, hc = mine.shape
    tr = _pick(hr, (256, 128, 64, 32, 16))
    nrb = hr // tr

    def body(pos_ref, w_ref, m_ref, v_ref, a_ref, b_ref, g_ref, d_ref, nm_ref, nv_ref):
        half = pl.program_id(0)
        g = jnp.where(half == pos_ref[1], a_ref[...], b_ref[...])
        d, nm, nv = _adam_math(w_ref[...], g, m_ref[...], v_ref[...])
        g_ref[...] = g
        d_ref[...] = d
        nm_ref[...] = nm
        nv_ref[...] = nv

    if axis == 1:
        wmap = lambda h, i, p: (h * nrb + i, 0)
    else:
        wmap = lambda h, i, p: (i, h)
    wblk = pl.BlockSpec((tr, hc), wmap)
    hblk = pl.BlockSpec((tr, hc), lambda h, i, p: (i, 0))
    return pl.pallas_call(
        body, name=name,
        grid_spec=pltpu.PrefetchScalarGridSpec(
            num_scalar_prefetch=1, grid=(2, nrb), in_specs=[wblk, wblk, wblk, hblk, hblk], out_specs=[wblk] * 4),
        out_shape=[jax.ShapeDtypeStruct((R, Cc), F32)] * 4, compiler_params=_params(2),
    )(pos, w, m, v, mine, theirs)


def _adamw_small(name, w, g, m, v):
    def body(w_ref, g_ref, m_ref, v_ref, d_ref, nm_ref, nv_ref):
        d, nm, nv = _adam_math(w_ref[...], g_ref[...], m_ref[...], v_ref[...])
        d_ref[...] = d
        nm_ref[...] = nm
        nv_ref[...] = nv

    blk = pl.BlockSpec(w.shape, lambda i: (0, 0))
    return pl.pallas_call(
        body, name=name, grid=(1,), in_specs=[blk] * 4, out_specs=[blk] * 3,
        out_shape=[jax.ShapeDtypeStruct(w.shape, F32)] * 3, compiler_params=_params(1),
    )(w, g, m, v)


SMALL = ("ffn1_norm_g", "mix_norm_g", "conv_b_dw", "conv_ln_g", "conv_ln_b", "q_norm_g", "k_norm_g", "ffn2_norm_g")
ORDER = ("ffn1_norm_g", "ffn1_w_gate", "ffn1_w_up", "ffn1_w_down", "mix_norm_g", "w_in", "conv_w_dw", "conv_b_dw",
         "conv_ln_g", "conv_ln_b", "q_norm_g", "k_norm_g", "w_out", "ffn2_norm_g", "ffn2_w_gate", "ffn2_w_up",
         "ffn2_w_down")
GATHER_FIRST = ("ffn1_w_gate", "ffn1_w_up")
GATHER_SECOND = ("ffn1_w_down", "w_in")
GATHER_THIRD = ("w_out", "ffn2_w_gate", "ffn2_w_up", "ffn2_w_down")


class _Exchange:
    def __init__(self, P, Mo, Vo, conv_shard, pos):
        self.P, self.Mo, self.Vo, self.pos = P, Mo, Vo, pos
        self.tokens = []
        self.pending = {}
        self.reducing = {}
        self.results = {}
        placed = {n: _place("place_" + n, pos, P[n][0], a) for n, a in BIG}
        self.shapes = {n: placed[n].shape for n, _ in BIG}
        first, tok = _gather_now("gather_first", [AXIS[n] for n in GATHER_FIRST], [placed[n] for n in GATHER_FIRST])
        self.ready = dict(zip(GATHER_FIRST, first))
        cq = conv_shard.shape[1]
        conv_full = lax.dynamic_update_slice(jnp.zeros((conv_shard.shape[0], 4 * cq), F32), conv_shard,
                                             (0, pos[0] * cq))
        for gname, names, conv in (("second", GATHER_SECOND, conv_full), ("third", GATHER_THIRD, None)):
            axes = [AXIS[n] for n in names]
            shapes = [self.shapes[n] for n in names]
            arrays = [placed[n] for n in names] + ([conv] if conv is not None else [])
            small = min(range(len(arrays)), key=lambda i: arrays[i].size)
            arrays[small] = arrays[small] + tok[0, 0].astype(arrays[small].dtype)
            plan = _gather_plan(axes, shapes, conv.shape if conv is not None else None)
            sems, thru, tok = _split_start("gather_%s_start" % gname, arrays, 3 * len(arrays), plan)
            self.tokens.append(tok)
            for n in names + (("conv_w32",) if conv is not None else ()):
                self.pending[n] = (gname, names, axes, plan, sems, thru, conv is not None)

    def tie(self, v):
        for tok in self.tokens:
            v = v + tok[0:1, 0:1]
        self.tokens = []
        return v

    def weights(self, names, after):
        if names[0] in self.pending:
            gname, gnames, axes, plan, sems, thru, has_conv = self.pending[names[0]]
            thru = _split_wait("gather_%s_wait" % gname, thru, sems, after, plan)
            nt = len(gnames)
            fulls = _gather_finish("gather_%s_finish" % gname, axes, thru[:nt])
            for n, f in zip(gnames, fulls):
                self.ready[n] = f
                del self.pending[n]
            if has_conv:
                self.ready["conv_w32"] = thru[nt]
                del self.pending["conv_w32"]
        return [self.ready[n] for n in names]

    def reduce_begin(self, gname, grads):
        names = list(grads)
        axes = [AXIS[n] for n in names]
        shapes = [self.shapes[n] for n in names]
        gs = [grads[n] for n in names]
        to_sibling = [(lambda ref, c, s=s, a=a: _window(ref, s, a, half=1 - c)) for s, a in zip(shapes, axes)]
        landed = _pair_exchange("pair_exchange_" + gname, gs, to_sibling,
                                [_half_shape(s, a) for s, a in zip(shapes, axes)], BF16)
        sbs = [_pair_sum("pair_sum_" + n, self.pos, g, l, s, a)
               for n, a, g, l, s in zip(names, axes, gs, landed, shapes)]
        lands = [lax.empty((3,) + _piece_shape(s, a), BF16) for s, a in zip(shapes, axes)]
        plan = _scatter_plan(axes, shapes)
        sems, thru, tok = _split_start("scatter_%s_start" % gname, sbs + lands, 3 * len(names), plan)
        self.tokens.append(tok)
        self.reducing[gname] = (names, axes, shapes, plan, sems, thru)

    def reduce_end(self, gname, after):
        names, axes, shapes, plan, sems, thru = self.reducing.pop(gname)
        nt = len(names)
        thru = _split_wait("scatter_%s_wait" % gname, thru, sems, after, plan)
        mine = [_chip_sum("chip_sum_" + n, self.pos, sb, l, s, a)
                for n, a, sb, l, s in zip(names, axes, thru[:nt], thru[nt:], shapes)]
        theirs = _pair_exchange("half_exchange_" + gname, mine, [(lambda ref, c: ref)] * nt,
                                [m.shape for m in mine], F32)
        for n, a, mi, th in zip(names, axes, mine, theirs):
            g, d, nm, nv = _adamw_halves("adamw_" + n, self.pos, self.P[n][0], self.Mo[n][0], self.Vo[n][0],
                                         mi, th, a)
            self.results[n] = (g[None], d[None], nm[None], nv[None])


def kernel(x, ffn1_norm_g, ffn1_w_gate, ffn1_w_up, ffn1_w_down, mix_norm_g, w_in, conv_w_dw, conv_b_dw, conv_ln_g, conv_ln_b, q_norm_g, k_norm_g, w_out, ffn2_norm_g, ffn2_w_gate, ffn2_w_up, ffn2_w_down, loss_target, m_ffn1_norm_g, m_ffn1_w_gate, m_ffn1_w_up, m_ffn1_w_down, m_mix_norm_g, m_w_in, m_conv_w_dw, m_conv_b_dw, m_conv_ln_g, m_conv_ln_b, m_q_norm_g, m_k_norm_g, m_w_out, m_ffn2_norm_g, m_ffn2_w_gate, m_ffn2_w_up, m_ffn2_w_down, v_ffn1_norm_g, v_ffn1_w_gate, v_ffn1_w_up, v_ffn1_w_down, v_mix_norm_g, v_w_in, v_conv_w_dw, v_conv_b_dw, v_conv_ln_g, v_conv_ln_b, v_q_norm_g, v_k_norm_g, v_w_out, v_ffn2_norm_g, v_ffn2_w_gate, v_ffn2_w_up, v_ffn2_w_down):
    args = dict(locals())
    P = {n: args[n] for n in ORDER}
    Mo = {n: args["m_" + n] for n in ORDER}
    Vo = {n: args["v_" + n] for n in ORDER}
    xs = x[0]
    tgt = loss_target[0]
    T, D = xs.shape
    hd = q_norm_g.shape[-1]
    C = conv_b_dw.shape[-1]
    ntap = conv_w_dw.shape[1]
    cx, cy, cc = _position()
    j0 = 2 * cx + cy
    pos = jnp.stack([j0, cc]).astype(jnp.int32)

    conv_shard = jnp.pad(conv_w_dw[0], ((0, HALO - ntap), (0, 0)))
    comm = _Exchange(P, Mo, Vo, conv_shard, pos)
    lossvec, dx0, G = _local_step(xs, tgt, {n: P[n] for n in SMALL}, comm, hd)
    loss = lax.psum(0.5 / D * jnp.sum(lossvec), AXES)
    grads, deltas, new_m, new_v = {}, {}, {}, {}
    for n, _ in BIG:
        grads[n], deltas[n], new_m[n], new_v[n] = comm.results[n]

    rows = [G["conv_w32"]]
    for n in ("ffn1_norm_g", "mix_norm_g", "ffn2_norm_g"):
        rows.append(G[n].reshape(D // C, C))
    for n in ("conv_b_dw", "conv_ln_g", "conv_ln_b", "q_norm_g", "k_norm_g"):
        rows.append(G[n])
    packed = jnp.concatenate(rows, axis=0)
    packed = jnp.pad(packed, ((0, -packed.shape[0] % 8), (0, 0)))
    total = _sum_slots("sum_small_grads", _gather_small(packed))
    r = HALO
    small_g = {}
    cq = C // 4
    small_g["conv_w_dw"] = lax.dynamic_slice(total[:ntap], (0, j0 * cq), (ntap, cq))
    for n in ("ffn1_norm_g", "mix_norm_g", "ffn2_norm_g"):
        small_g[n] = total[r:r + D // C].reshape(1, D)
        r += D // C
    for n in ("conv_b_dw", "conv_ln_g", "conv_ln_b"):
        small_g[n] = total[r:r + 1]
        r += 1
    for n in ("q_norm_g", "k_norm_g"):
        small_g[n] = total[r:r + 1, :hd]
        r += 1
    for n in ("conv_w_dw",) + SMALL:
        lead = n == "conv_w_dw"
        w2, m2, v2 = (P[n][0], Mo[n][0], Vo[n][0]) if lead else (P[n], Mo[n], Vo[n])
        d, nm, nv = _adamw_small("adamw_" + n, w2, small_g[n], m2, v2)
        if lead:
            grads[n], deltas[n], new_m[n], new_v[n] = small_g[n][None], d[None], nm[None], nv[None]
        else:
            grads[n], deltas[n], new_m[n], new_v[n] = small_g[n], d, nm, nv

    return (loss, dx0[None], *[grads[n] for n in ORDER], *[deltas[n] for n in ORDER],
            *[new_m[n] for n in ORDER], *[new_v[n] for n in ORDER])
```

```python
import jax
import jax.numpy as jnp
from jax import lax
from jax.experimental import pallas as pl
from jax.experimental.pallas import tpu as pltpu

F32 = jnp.float32
BF16 = jnp.bfloat16
EPS = 1e-6
WINDOW = 128
DILATIONS = (1, 4, 16)
ALIBI_MAX_BIAS = 8.0
LANES = 128
HALO = 32
ADAM_LR, ADAM_B1, ADAM_B2, ADAM_EPS, ADAM_WD, ADAM_STEP = 0.001, 0.9, 0.999, 1e-08, 0.01, 10
VMEM_LIMIT_MB = 56
MESH = pl.DeviceIdType.MESH
ANY = pl.BlockSpec(memory_space=pl.ANY)
AXES = ("x", "y", "c")
NEG = -1e30


def _pick(n, cands):
    for c in cands:
        if n % c == 0:
            return c
    return n


def _params(nsem):
    return pltpu.CompilerParams(dimension_semantics=("arbitrary",) * nsem,
                                vmem_limit_bytes=VMEM_LIMIT_MB << 20)


def _nn(a, b):
    return jnp.dot(a, b, preferred_element_type=F32)


def _nt(a, b):
    return lax.dot_general(a, b, (((1,), (1,)), ((), ())), preferred_element_type=F32)


def _tn(a, b):
    return lax.dot_general(a, b, (((0,), (0,)), ((), ())), preferred_element_type=F32)


def _sigmoid(v):
    return jax.nn.sigmoid(v)


def _rms_r(xv):
    return lax.rsqrt(jnp.mean(xv * xv, axis=-1, keepdims=True) + EPS)


def _norm_matmul(name, x, g, ws, swiglu):
    T, D = x.shape
    N = ws[0].shape[1]
    tm = _pick(T, (512, 256, 128))
    tn = _pick(N, (512, 256, 128))
    nw = len(ws)

    def body(*refs):
        x_ref, g_ref = refs[:2]
        w_refs = refs[2:2 + nw]
        outs = refs[2 + nw:-1]
        hs = refs[-1]

        @pl.when(pl.program_id(1) == 0)
        def _():
            xv = x_ref[...]
            hv = (xv * _rms_r(xv) * g_ref[...]).astype(BF16)
            hs[...] = hv
            outs[0][...] = hv

        h = hs[...]
        if swiglu:
            gt = _nn(h, w_refs[0][...])
            u = _nn(h, w_refs[1][...])
            outs[1][...] = gt.astype(BF16)
            outs[2][...] = u.astype(BF16)
            outs[3][...] = (gt * _sigmoid(gt) * u).astype(BF16)
        else:
            outs[1][...] = _nn(h, w_refs[0][...])

    row = pl.BlockSpec((tm, D), lambda i, j: (i, 0))
    col = pl.BlockSpec((D, tn), lambda i, j: (0, j))
    tile = pl.BlockSpec((tm, tn), lambda i, j: (i, j))
    if swiglu:
        out_shape = [jax.ShapeDtypeStruct((T, D), BF16)] + [jax.ShapeDtypeStruct((T, N), BF16)] * 3
        out_specs = [row, tile, tile, tile]
    else:
        out_shape = [jax.ShapeDtypeStruct((T, D), BF16), jax.ShapeDtypeStruct((T, N), F32)]
        out_specs = [row, tile]
    return pl.pallas_call(
        body, name=name, grid=(T // tm, N // tn),
        in_specs=[row, pl.BlockSpec((1, D), lambda i, j: (0, 0))] + [col] * nw,
        out_specs=out_specs, out_shape=out_shape,
        scratch_shapes=[pltpu.VMEM((tm, D), BF16)],
        compiler_params=_params(2),
    )(x, g, *ws)


def _matmul_res(name, a, w, res, scale, tgt=None):
    T, K = a.shape
    N = w.shape[1]
    tm = _pick(T, (512, 256, 128))
    tk = _pick(K, (1408, 1024, 512, 256, 128))
    nk = K // tk
    loss = tgt is not None

    def body(*refs):
        if loss:
            a_ref, w_ref, res_ref, tgt_ref, dx_ref, dxb_ref, lv_ref, acc = refs
        else:
            a_ref, w_ref, res_ref, out_ref, acc = refs
        i, k = pl.program_id(0), pl.program_id(1)

        @pl.when(k == 0)
        def _():
            acc[...] = jnp.zeros_like(acc)

        acc[...] += _nn(a_ref[...], w_ref[...])

        @pl.when(k == nk - 1)
        def _():
            val = res_ref[...] + scale * acc[...]
            if loss:
                dv = val - tgt_ref[...]
                dx = dv * (1.0 / N)
                dx_ref[...] = dx
                dxb_ref[...] = dx.astype(BF16)
                part = jnp.sum(dv * dv, axis=0, keepdims=True)

                @pl.when(i == 0)
                def _():
                    lv_ref[...] = part

                @pl.when(i > 0)
                def _():
                    lv_ref[...] += part
            else:
                out_ref[...] = val

    row = pl.BlockSpec((tm, N), lambda i, k: (i, 0))
    in_specs = [pl.BlockSpec((tm, tk), lambda i, k: (i, k)), pl.BlockSpec((tk, N), lambda i, k: (k, 0)), row]
    args = [a, w, res]
    if loss:
        in_specs.append(row)
        args.append(tgt)
        out_specs = [row, row, pl.BlockSpec((1, N), lambda i, k: (0, 0))]
        out_shape = [jax.ShapeDtypeStruct((T, N), F32), jax.ShapeDtypeStruct((T, N), BF16),
                     jax.ShapeDtypeStruct((1, N), F32)]
    else:
        out_specs = row
        out_shape = jax.ShapeDtypeStruct((T, N), F32)
    return pl.pallas_call(
        body, name=name, grid=(T // tm, nk), in_specs=in_specs, out_specs=out_specs, out_shape=out_shape,
        scratch_shapes=[pltpu.VMEM((tm, N), F32)], compiler_params=_params(2),
    )(*args)


def _nt_matmul(name, dyb, w, scale=1.0, gate=None, up=None):
    T, D = dyb.shape
    N = w.shape[0]
    tm = _pick(T, (512, 256, 128))
    tn = _pick(N, (512, 256, 128))
    swiglu = gate is not None

    def body(*refs):
        if swiglu:
            dy_ref, w_ref, g_ref, u_ref, dg_ref, du_ref = refs
        else:
            dy_ref, w_ref, o_ref, ob_ref = refs
        da = _nt(dy_ref[...], w_ref[...]) * scale
        if swiglu:
            gt = g_ref[...].astype(F32)
            u = u_ref[...].astype(F32)
            sg = _sigmoid(gt)
            dg_ref[...] = (da * u * (sg * (1.0 + gt * (1.0 - sg)))).astype(BF16)
            du_ref[...] = (da * (gt * sg)).astype(BF16)
        else:
            o_ref[...] = da
            ob_ref[...] = da.astype(BF16)

    tile = pl.BlockSpec((tm, tn), lambda i, j: (i, j))
    in_specs = [pl.BlockSpec((tm, D), lambda i, j: (i, 0)), pl.BlockSpec((tn, D), lambda i, j: (j, 0))]
    args = [dyb, w]
    if swiglu:
        in_specs += [tile, tile]
        args += [gate, up]
        out_shape = [jax.ShapeDtypeStruct((T, N), BF16)] * 2
    else:
        out_shape = [jax.ShapeDtypeStruct((T, N), F32), jax.ShapeDtypeStruct((T, N), BF16)]
    return pl.pallas_call(
        body, name=name, grid=(T // tm, N // tn), in_specs=in_specs, out_specs=[tile, tile],
        out_shape=out_shape, compiler_params=_params(2),
    )(*args)


def _nt_rms_bwd(name, As, Ws, x, g, dres):
    T, K = As[0].shape
    D = x.shape[1]
    tm = _pick(T, (512, 256, 128))
    tk = _pick(K, (512, 256, 128))
    nk = K // tk
    na = len(As)

    def body(*refs):
        a_refs = refs[:na]
        w_refs = refs[na:2 * na]
        x_ref, g_ref, dres_ref, dx_ref, dxb_ref, dg_ref, acc = refs[2 * na:]
        i, k = pl.program_id(0), pl.program_id(1)

        @pl.when(k == 0)
        def _():
            acc[...] = jnp.zeros_like(acc)

        for a_ref, w_ref in zip(a_refs, w_refs):
            acc[...] += _nt(a_ref[...], w_ref[...])

        @pl.when(k == nk - 1)
        def _():
            dh = acc[...]
            xv = x_ref[...]
            r = _rms_r(xv)
            gd = dh * g_ref[...]
            dx = dres_ref[...] + r * gd - xv * (r * r * r) * jnp.mean(gd * xv, axis=-1, keepdims=True)
            dx_ref[...] = dx
            dxb_ref[...] = dx.astype(BF16)
            part = jnp.sum(dh * xv * r, axis=0, keepdims=True)

            @pl.when(i == 0)
            def _():
                dg_ref[...] = part

            @pl.when(i > 0)
            def _():
                dg_ref[...] += part

    row = pl.BlockSpec((tm, D), lambda i, k: (i, 0))
    vec = pl.BlockSpec((1, D), lambda i, k: (0, 0))
    return pl.pallas_call(
        body, name=name, grid=(T // tm, nk),
        in_specs=[pl.BlockSpec((tm, tk), lambda i, k: (i, k))] * na
        + [pl.BlockSpec((D, tk), lambda i, k: (0, k))] * na + [row, vec, row],
        out_specs=[row, row, vec],
        out_shape=[jax.ShapeDtypeStruct((T, D), F32), jax.ShapeDtypeStruct((T, D), BF16),
                   jax.ShapeDtypeStruct((1, D), F32)],
        scratch_shapes=[pltpu.VMEM((tm, D), F32)], compiler_params=_params(2),
    )(*As, *Ws, x, g, dres)


def _tn_matmul(name, a, b, scale=1.0):
    T, M = a.shape
    N = b.shape[1]
    tm = _pick(M, (1024, 512, 256, 128))
    tn = _pick(N, (1408, 1280, 1024, 512, 256, 128))
    tk = _pick(T, (1024, 512, 256, 128))
    nk = T // tk

    def body(a_ref, b_ref, o_ref, acc):
        k = pl.program_id(2)

        @pl.when(k == 0)
        def _():
            acc[...] = jnp.zeros_like(acc)

        acc[...] += _tn(a_ref[...], b_ref[...])

        @pl.when(k == nk - 1)
        def _():
            o_ref[...] = (acc[...] * scale).astype(BF16)

    return pl.pallas_call(
        body, name=name, grid=(M // tm, N // tn, nk),
        in_specs=[pl.BlockSpec((tk, tm), lambda i, j, k: (k, i)), pl.BlockSpec((tk, tn), lambda i, j, k: (k, j))],
        out_specs=pl.BlockSpec((tm, tn), lambda i, j, k: (i, j)),
        out_shape=jax.ShapeDtypeStruct((M, N), BF16),
        scratch_shapes=[pltpu.VMEM((tm, tn), F32)], compiler_params=_params(3),
    )(a, b)


CONV_ROWS = 128
ROW_CHUNK = 32
LANE_CHUNK = 256


def _conv_fwd(name, z, w32, b, lg, lb, C):
    T = z.shape[0]
    tc = CONV_ROWS
    ntap = 31
    lc = _pick(C, (LANE_CHUNK, LANES))
    rpb = tc // HALO

    def body(zc_ref, zp_ref, w_ref, b_ref, lg_ref, lb_ref, yc_ref, ycv_ref, vbuf, ybuf):
        i = pl.program_id(0)
        zc = zc_ref[...]
        zp = zp_ref[...]
        vbuf[HALO:HALO + tc, :] = zc[:, :C] * _sigmoid(zc[:, C:])
        vbuf[0:HALO, :] = jnp.where(i > 0, zp[:, :C] * _sigmoid(zp[:, C:]), 0.0)
        for r0 in range(0, tc, ROW_CHUNK):
            for c0 in range(0, C, lc):
                acc = jnp.zeros((ROW_CHUNK, lc), F32) + b_ref[:, c0:c0 + lc]
                for k in range(ntap):
                    s = r0 + 2 + k
                    acc = acc + w_ref[k:k + 1, c0:c0 + lc] * vbuf[s:s + ROW_CHUNK, c0:c0 + lc]
                ybuf[r0:r0 + ROW_CHUNK, c0:c0 + lc] = acc
        y = ybuf[...]
        ycv_ref[...] = y
        mu = jnp.mean(y, axis=-1, keepdims=True)
        yc = y - mu
        rstd = lax.rsqrt(jnp.mean(yc * yc, axis=-1, keepdims=True) + EPS)
        ln = yc * rstd * lg_ref[...] + lb_ref[...]
        yc_ref[...] = (ln * _sigmoid(ln)).astype(BF16)

    vec = pl.BlockSpec((1, C), lambda i: (0, 0))
    return pl.pallas_call(
        body, name=name, grid=(T // tc,),
        in_specs=[pl.BlockSpec((tc, 2 * C), lambda i: (i, 0)),
                  pl.BlockSpec((HALO, 2 * C), lambda i: (jnp.maximum(i * rpb - 1, 0), 0)),
                  pl.BlockSpec((HALO, C), lambda i: (0, 0)), vec, vec, vec],
        out_specs=[pl.BlockSpec((tc, C), lambda i: (i, 0))] * 2,
        out_shape=[jax.ShapeDtypeStruct((T, C), BF16), jax.ShapeDtypeStruct((T, C), F32)],
        scratch_shapes=[pltpu.VMEM((tc + HALO, C), F32), pltpu.VMEM((tc, C), F32)],
        compiler_params=_params(1),
    )(z, z, w32, b, lg, lb)


def _conv_bwd(name, z, ycv, dycat, w32, lg, lb, C):
    T = z.shape[0]
    tc = CONV_ROWS
    ntap = 31
    lc = _pick(C, (LANE_CHUNK, LANES))
    rpb = tc // HALO
    nstep = T // tc
    nhb = T // HALO

    def ln_bwd(dyc, y, lgv, lbv):
        mu = jnp.mean(y, axis=-1, keepdims=True)
        yc = y - mu
        rstd = lax.rsqrt(jnp.mean(yc * yc, axis=-1, keepdims=True) + EPS)
        yn = yc * rstd
        ln = yn * lgv + lbv
        sg = _sigmoid(ln)
        dln = dyc * (sg * (1.0 + ln * (1.0 - sg)))
        dyn = dln * lgv
        dy = rstd * (dyn - jnp.mean(dyn, axis=-1, keepdims=True)
                     - yn * jnp.mean(dyn * yn, axis=-1, keepdims=True))
        return dy, dln, yn

    def body(zc_ref, zp_ref, y_ref, yn_ref, d_ref, dn_ref, w_ref, lg_ref, lb_ref,
             dz_ref, dw_ref, db_ref, dlg_ref, dlb_ref, vbuf, dbuf, dvbuf, dwacc):
        i = pl.program_id(0)
        lgv, lbv = lg_ref[...], lb_ref[...]
        zc = zc_ref[...]
        zp = zp_ref[...]
        a = zc[:, :C]
        sgt = _sigmoid(zc[:, C:])
        vbuf[HALO:HALO + tc, :] = a * sgt
        vbuf[0:HALO, :] = jnp.where(i > 0, zp[:, :C] * _sigmoid(zp[:, C:]), 0.0)
        dy, dln, yn = ln_bwd(d_ref[...], y_ref[...], lgv, lbv)
        dbuf[0:tc, :] = dy
        dyn_, _, _ = ln_bwd(dn_ref[...], yn_ref[...], lgv, lbv)
        dbuf[tc:tc + HALO, :] = jnp.where(i < nstep - 1, dyn_, 0.0)

        @pl.when(i == 0)
        def _():
            dwacc[...] = jnp.zeros_like(dwacc)
            db_ref[...] = jnp.zeros_like(db_ref)
            dlg_ref[...] = jnp.zeros_like(dlg_ref)
            dlb_ref[...] = jnp.zeros_like(dlb_ref)

        db_ref[...] += jnp.sum(dy, axis=0, keepdims=True)
        dlg_ref[...] += jnp.sum(dln * yn, axis=0, keepdims=True)
        dlb_ref[...] += jnp.sum(dln, axis=0, keepdims=True)

        for r0 in range(0, tc, ROW_CHUNK):
            for c0 in range(0, C, lc):
                dcur = dbuf[r0:r0 + ROW_CHUNK, c0:c0 + lc]
                acc = jnp.zeros((ROW_CHUNK, lc), F32)
                for k in range(ntap):
                    s = r0 + 30 - k
                    acc = acc + w_ref[k:k + 1, c0:c0 + lc] * dbuf[s:s + ROW_CHUNK, c0:c0 + lc]
                    sv = r0 + 2 + k
                    prod = dcur * vbuf[sv:sv + ROW_CHUNK, c0:c0 + lc]
                    red = prod[0:8]
                    for q in range(8, ROW_CHUNK, 8):
                        red = red + prod[q:q + 8]
                    dwacc[8 * k:8 * k + 8, c0:c0 + lc] += red
                dvbuf[r0:r0 + ROW_CHUNK, c0:c0 + lc] = acc
        dv = dvbuf[...]
        dz_ref[:, :C] = (dv * sgt).astype(BF16)
        dz_ref[:, C:] = (dv * a * sgt * (1.0 - sgt)).astype(BF16)

        @pl.when(i == nstep - 1)
        def _():
            for k in range(ntap):
                dw_ref[k:k + 1, :] = jnp.sum(dwacc[8 * k:8 * k + 8, :], axis=0, keepdims=True)
            dw_ref[ntap:HALO, :] = jnp.zeros((HALO - ntap, C), F32)

    vec = pl.BlockSpec((1, C), lambda i: (0, 0))
    cur = pl.BlockSpec((tc, C), lambda i: (i, 0))
    nxt = pl.BlockSpec((HALO, C), lambda i: (jnp.minimum((i + 1) * rpb, nhb - 1), 0))
    return pl.pallas_call(
        body, name=name, grid=(nstep,),
        in_specs=[pl.BlockSpec((tc, 2 * C), lambda i: (i, 0)),
                  pl.BlockSpec((HALO, 2 * C), lambda i: (jnp.maximum(i * rpb - 1, 0), 0)),
                  cur, nxt, cur, nxt, pl.BlockSpec((HALO, C), lambda i: (0, 0)), vec, vec],
        out_specs=[pl.BlockSpec((tc, 2 * C), lambda i: (i, 0)), pl.BlockSpec((HALO, C), lambda i: (0, 0)),
                   vec, vec, vec],
        out_shape=[jax.ShapeDtypeStruct((T, 2 * C), BF16), jax.ShapeDtypeStruct((HALO, C), F32),
                   jax.ShapeDtypeStruct((1, C), F32), jax.ShapeDtypeStruct((1, C), F32),
                   jax.ShapeDtypeStruct((1, C), F32)],
        scratch_shapes=[pltpu.VMEM((tc + HALO, C), F32), pltpu.VMEM((tc + HALO, C), F32),
                        pltpu.VMEM((tc, C), F32), pltpu.VMEM((8 * HALO, C), F32)],
        compiler_params=_params(1),
    )(z, z, ycv, ycv, dycat, dycat, w32, lg, lb)


def _seg_sum(u, bmat):
    hi = u.astype(BF16)
    lo = (u - hi.astype(F32)).astype(BF16)
    return _nn(hi, bmat) + _nn(lo, bmat)


def _attn_prep(name, z, gq, gk, bmat, A, c0, hd):
    T = z.shape[0]
    tm = _pick(T, (256, 128))

    def body(zq_ref, zk_ref, zv_ref, gq_ref, gk_ref, b_ref, o_ref):
        bm = b_ref[...]
        for idx, (z_ref, g_ref) in enumerate(((zq_ref, gq_ref), (zk_ref, gk_ref))):
            zv = z_ref[...]
            r = lax.rsqrt(_seg_sum(zv * zv, bm) * (1.0 / hd) + EPS)
            o_ref[:, idx * A:(idx + 1) * A] = zv * r * g_ref[...]
        o_ref[:, 2 * A:] = zv_ref[...]

    vec = pl.BlockSpec((1, A), lambda i: (0, 0))
    return pl.pallas_call(
        body, name=name, grid=(T // tm,),
        in_specs=[pl.BlockSpec((tm, A), lambda i: (i, c0)), pl.BlockSpec((tm, A), lambda i: (i, c0 + 1)),
                  pl.BlockSpec((tm, A), lambda i: (i, c0 + 2)), vec, vec,
                  pl.BlockSpec((A, A), lambda i: (0, 0))],
        out_specs=pl.BlockSpec((tm, 3 * A), lambda i: (i, 0)),
        out_shape=jax.ShapeDtypeStruct((T, 3 * A), F32), compiler_params=_params(1),
    )(z, z, z, gq, gk, bmat)


QK_SCALE = 0.125
ATTN_UNROLL = 4


def _fill_bias(bias, sl_ref, hp, d):
    qi = lax.broadcasted_iota(jnp.int32, (WINDOW, 2 * WINDOW), 0)
    kj = lax.broadcasted_iota(jnp.int32, (WINDOW, 2 * WINDOW), 1)
    dist = WINDOW + qi - kj
    inband = (dist >= 0) & (dist <= WINDOW)
    distf = dist.astype(F32)
    for hh in range(2):
        b = jnp.where(inband, -(sl_ref[2 * hp + hh] * d) * distf, NEG)
        bias[2 * hh + 1] = b
        bias[2 * hh] = jnp.where(kj >= WINDOW, b, NEG)


CHUNK = WINDOW * DILATIONS[-1]


def _deinterleave(dst, src, d, rows, dst_pitch, dst_off, src_off):
    for r in range(d):
        if d == 1:
            val = src[src_off:src_off + rows, :]
        else:
            val = src[pl.ds(src_off + r, rows, stride=d), :]
        lo = r * dst_pitch + dst_off
        dst[lo:lo + rows, :] = val.astype(dst.dtype)


def _interleave_add(dst, start, src, d, rows, src_pitch, src_off):
    for r in range(d):
        lo = r * src_pitch + src_off
        idx = pl.ds(start, rows) if d == 1 else pl.ds(start + r, rows, stride=d)
        dst[idx, :] += src[lo:lo + rows, :]


def _attn_fwd(name, qkv, slopes, A):
    T = qkv.shape[0]
    hpn, nch, nblk = A // LANES, T // CHUNK, CHUNK // WINDOW
    nbranch = len(DILATIONS)

    def body(*refs):
        sl_ref, q_ref, k_ref, kp_ref, v_ref, vp_ref, y_ref, lg_ref, qd, kd, vd, od, ld, bias = refs[:14]
        onat, lnat = refs[14:14 + nbranch], refs[14 + nbranch:]
        hp, ch = pl.program_id(0), pl.program_id(1)
        lane = lax.broadcasted_iota(jnp.int32, (1, LANES), 1)
        first = lane < (LANES // 2)
        for bi, d in enumerate(DILATIONS):
            Ld = CHUNK // d
            seg = Ld + WINDOW
            nbr = Ld // WINDOW
            _deinterleave(qd, q_ref, d, Ld, Ld, 0, 0)
            for dst, cur, prev in ((kd, k_ref, kp_ref), (vd, v_ref, vp_ref)):
                _deinterleave(dst, prev, d, WINDOW, seg, 0, CHUNK - WINDOW * d)
                _deinterleave(dst, cur, d, Ld, seg, WINDOW, 0)
            _fill_bias(bias, sl_ref, hp, d)
            ob, lb = (onat[bi], lnat[bi]) if d == 1 else (od, ld)

            def step(it, carry, Ld=Ld, seg=seg, nbr=nbr, ob=ob, lb=lb):
                r, nl = it // nbr, it % nbr
                q0 = pl.multiple_of(r * Ld + nl * WINDOW, WINDOW)
                k0 = pl.multiple_of(r * seg + nl * WINDOW, WINDOW)
                later = jnp.where(ch * nbr + nl > 0, 1, 0)
                qb = qd[pl.ds(q0, WINDOW), :]
                k2 = kd[pl.ds(k0, 2 * WINDOW), :]
                v2 = vd[pl.ds(k0, 2 * WINDOW), :]
                res = []
                for hh in range(2):
                    mh = first if hh == 0 else jnp.logical_not(first)
                    s = _nt(jnp.where(mh, qb, jnp.zeros_like(qb)), k2) + bias[2 * hh + later]
                    mx = jnp.max(s, axis=-1, keepdims=True)
                    p = jnp.exp(s - mx)
                    den = jnp.sum(p, axis=-1, keepdims=True)
                    res.append((_nn(p.astype(BF16), v2) / den, mx + jnp.log(den)))
                ob[pl.ds(q0, WINDOW), :] = jnp.where(first, res[0][0], res[1][0])
                lb[pl.ds(q0, WINDOW), :] = jnp.where(first, res[0][1], res[1][1])
                return carry

            lax.fori_loop(0, nblk, step, 0, unroll=ATTN_UNROLL)
            if d > 1:
                for r in range(d):
                    onat[bi][pl.ds(r, Ld, stride=d), :] = od[r * Ld:(r + 1) * Ld, :]
                    lnat[bi][pl.ds(r, Ld, stride=d), :] = ld[r * Ld:(r + 1) * Ld, :]
        ls = [l[...] for l in lnat]
        mx = ls[0]
        for v in ls[1:]:
            mx = jnp.maximum(mx, v)
        es = [jnp.exp(v - mx) for v in ls]
        den = es[0]
        for e in es[1:]:
            den = den + e
        out = es[0] * onat[0][...]
        for e, o in zip(es[1:], onat[1:]):
            out = out + e * o[...]
        y_ref[...] = (out / den).astype(BF16)
        lg_ref[...] = mx + jnp.log(den)

    blk = lambda m: pl.BlockSpec((CHUNK, LANES), m)
    cur = lambda which: blk(lambda hp, ch: (ch, which * hpn + hp))
    prev = lambda which: blk(lambda hp, ch: (jnp.maximum(ch - 1, 0), which * hpn + hp))
    omap = blk(lambda hp, ch: (ch, hp))
    f32buf = pltpu.VMEM((CHUNK, LANES), F32)
    return pl.pallas_call(
        body, name=name, grid=(hpn, nch),
        in_specs=[pl.BlockSpec(memory_space=pltpu.SMEM), cur(0), cur(1), prev(1), cur(2), prev(2)],
        out_specs=[omap, omap],
        out_shape=[jax.ShapeDtypeStruct((T, A), BF16), jax.ShapeDtypeStruct((T, A), F32)],
        scratch_shapes=[pltpu.VMEM((CHUNK, LANES), BF16), pltpu.VMEM((2 * CHUNK, LANES), BF16),
                        pltpu.VMEM((2 * CHUNK, LANES), BF16), f32buf, f32buf,
                        pltpu.VMEM((4, WINDOW, 2 * WINDOW), F32)] + [f32buf] * (2 * nbranch),
        compiler_params=_params(2),
    )(slopes, qkv, qkv, qkv, qkv, qkv)


def _attn_bwd(name, qkv, dycat, yatt, lg, slopes, A, catoff):
    T = qkv.shape[0]
    hpn, nch, nblk = A // LANES, T // CHUNK, CHUNK // WINDOW
    co = catoff // LANES

    def body(sl_ref, q_ref, k_ref, kp_ref, v_ref, vp_ref, do_ref, o_ref, l_ref, dq_ref, dk_ref, dv_ref,
             qd, kd, vd, dod, ddn, ddd, ldd, dqd, dkd, dvd, bias):
        hp, ch = pl.program_id(0), pl.program_id(1)
        lane = lax.broadcasted_iota(jnp.int32, (1, LANES), 1)
        first = lane < (LANES // 2)
        ddn[...] = do_ref[...] * o_ref[...].astype(F32)
        dq_ref[...] = jnp.zeros_like(dq_ref)

        @pl.when(ch == 0)
        def _():
            dk_ref[...] = jnp.zeros_like(dk_ref)
            dv_ref[...] = jnp.zeros_like(dv_ref)

        base = ch * CHUNK
        for d in DILATIONS:
            Ld = CHUNK // d
            seg = Ld + WINDOW
            nbr = Ld // WINDOW
            _deinterleave(qd, q_ref, d, Ld, Ld, 0, 0)
            _deinterleave(dod, do_ref, d, Ld, Ld, 0, 0)
            _deinterleave(ddd, ddn, d, Ld, Ld, 0, 0)
            _deinterleave(ldd, l_ref, d, Ld, Ld, 0, 0)
            for dst, cur, prev in ((kd, k_ref, kp_ref), (vd, v_ref, vp_ref)):
                _deinterleave(dst, prev, d, WINDOW, seg, 0, CHUNK - WINDOW * d)
                _deinterleave(dst, cur, d, Ld, seg, WINDOW, 0)
            dkd[0:d * seg, :] = jnp.zeros((d * seg, LANES), F32)
            dvd[0:d * seg, :] = jnp.zeros((d * seg, LANES), F32)
            _fill_bias(bias, sl_ref, hp, d)

            def step(it, carry, Ld=Ld, seg=seg, nbr=nbr):
                r, nl = it // nbr, it % nbr
                q0 = pl.multiple_of(r * Ld + nl * WINDOW, WINDOW)
                k0 = pl.multiple_of(r * seg + nl * WINDOW, WINDOW)
                later = jnp.where(ch * nbr + nl > 0, 1, 0)
                qb = qd[pl.ds(q0, WINDOW), :]
                k2 = kd[pl.ds(k0, 2 * WINDOW), :]
                v2 = vd[pl.ds(k0, 2 * WINDOW), :]
                dob = dod[pl.ds(q0, WINDOW), :]
                dd = ddd[pl.ds(q0, WINDOW), :]
                lb = ldd[pl.ds(q0, WINDOW), :]
                dk2 = jnp.zeros((2 * WINDOW, LANES), F32)
                dv2 = jnp.zeros((2 * WINDOW, LANES), F32)
                dqs = []
                for hh in range(2):
                    mh = first if hh == 0 else jnp.logical_not(first)
                    qh = jnp.where(mh, qb, jnp.zeros_like(qb))
                    doh = jnp.where(mh, dob, jnp.zeros_like(dob))
                    lcol = lb[:, hh * (LANES // 2):hh * (LANES // 2) + 1]
                    p = jnp.exp(_nt(qh, k2) + bias[2 * hh + later] - lcol)
                    dcol = jnp.sum(jnp.where(mh, dd, 0.0), axis=-1, keepdims=True)
                    ds = (p * (_nt(doh, v2) - dcol)).astype(BF16)
                    dqs.append(_nn(ds, k2))
                    dk2 = dk2 + _tn(ds, qh)
                    dv2 = dv2 + _tn(p.astype(BF16), doh)
                dqd[pl.ds(q0, WINDOW), :] = jnp.where(first, dqs[0], dqs[1])
                dkd[pl.ds(k0, 2 * WINDOW), :] += dk2
                dvd[pl.ds(k0, 2 * WINDOW), :] += dv2
                return carry

            lax.fori_loop(0, nblk, step, 0, unroll=ATTN_UNROLL)
            _interleave_add(dq_ref, 0, dqd, d, Ld, Ld, 0)
            for acc, out in ((dkd, dk_ref), (dvd, dv_ref)):
                _interleave_add(out, base, acc, d, Ld, seg, WINDOW)

                @pl.when(ch > 0)
                def _(acc=acc, out=out, d=d, seg=seg):
                    _interleave_add(out, base - WINDOW * d, acc, d, WINDOW, seg, 0)

    blk = lambda m: pl.BlockSpec((CHUNK, LANES), m)
    cur = lambda which: blk(lambda hp, ch: (ch, which * hpn + hp))
    prev = lambda which: blk(lambda hp, ch: (jnp.maximum(ch - 1, 0), which * hpn + hp))
    omap = blk(lambda hp, ch: (ch, hp))
    full = pl.BlockSpec((T, LANES), lambda hp, ch: (0, hp))
    f32buf = pltpu.VMEM((CHUNK, LANES), F32)
    bf16buf = pltpu.VMEM((CHUNK, LANES), BF16)
    return pl.pallas_call(
        body, name=name, grid=(hpn, nch),
        in_specs=[pl.BlockSpec(memory_space=pltpu.SMEM), cur(0), cur(1), prev(1), cur(2), prev(2),
                  blk(lambda hp, ch: (ch, co + hp)), omap, omap],
        out_specs=[omap, full, full],
        out_shape=[jax.ShapeDtypeStruct((T, A), F32)] * 3,
        scratch_shapes=[bf16buf, pltpu.VMEM((2 * CHUNK, LANES), BF16), pltpu.VMEM((2 * CHUNK, LANES), BF16),
                        bf16buf, f32buf, f32buf, f32buf, f32buf,
                        pltpu.VMEM((2 * CHUNK, LANES), F32), pltpu.VMEM((2 * CHUNK, LANES), F32),
                        pltpu.VMEM((4, WINDOW, 2 * WINDOW), F32)],
        compiler_params=_params(2),
    )(slopes, qkv, qkv, qkv, qkv, qkv, dycat, yatt, lg)


def _attn_bwd_combine(name, dqs, dks, dvs, z, gq, gk, bmat, fmat, A, c0, hd):
    T = z.shape[0]
    tm = _pick(T, (256, 128))
    nbr = len(dqs)

    def body(*refs):
        dq_refs, dk_refs, dv_refs = refs[:nbr], refs[nbr:2 * nbr], refs[2 * nbr:3 * nbr]
        zq_ref, zk_ref, gq_ref, gk_ref, b_ref, f_ref, dz_ref, dgq_ref, dgk_ref = refs[3 * nbr:]
        i = pl.program_id(0)
        bm = b_ref[...]

        def tot(rs):
            t = rs[0][...]
            for r in rs[1:]:
                t = t + r[...]
            return t

        for idx, (d_refs, z_ref, g_ref, dg_ref, gscale) in enumerate(
                ((dq_refs, zq_ref, gq_ref, dgq_ref, QK_SCALE), (dk_refs, zk_ref, gk_ref, dgk_ref, 1.0))):
            dy = tot(d_refs)
            zv = z_ref[...]
            r = lax.rsqrt(_seg_sum(zv * zv, bm) * (1.0 / hd) + EPS)
            gd = dy * g_ref[...]
            mean = _seg_sum(gd * zv, bm) * (1.0 / hd)
            dz_ref[:, idx * A:(idx + 1) * A] = (r * gd - zv * (r * r * r) * mean).astype(BF16)
            part = jnp.sum(dy * zv * r, axis=0, keepdims=True) * gscale

            @pl.when(i == 0)
            def _():
                dg_ref[...] = part

            @pl.when(i > 0)
            def _():
                dg_ref[...] += part

        dz_ref[:, 2 * A:] = tot(dv_refs).astype(BF16)

        @pl.when(i == T // tm - 1)
        def _():
            fm = f_ref[...]
            for dg_ref in (dgq_ref, dgk_ref):
                v = jnp.broadcast_to(dg_ref[...], (8, A))
                hi = v.astype(BF16)
                mid = (v - hi.astype(F32)).astype(BF16)
                lo = (v - hi.astype(F32) - mid.astype(F32)).astype(BF16)
                dg_ref[...] = (_nn(hi, fm) + _nn(mid, fm) + _nn(lo, fm))[0:1]

    blk = pl.BlockSpec((tm, A), lambda i: (i, 0))
    vec = pl.BlockSpec((1, A), lambda i: (0, 0))
    return pl.pallas_call(
        body, name=name, grid=(T // tm,),
        in_specs=[blk] * (3 * nbr) + [pl.BlockSpec((tm, A), lambda i: (i, c0)),
                                      pl.BlockSpec((tm, A), lambda i: (i, c0 + 1)), vec, vec,
                                      pl.BlockSpec((A, A), lambda i: (0, 0)),
                                      pl.BlockSpec((A, A), lambda i: (0, 0))],
        out_specs=[pl.BlockSpec((tm, 3 * A), lambda i: (i, 0)), vec, vec],
        out_shape=[jax.ShapeDtypeStruct((T, 3 * A), BF16), jax.ShapeDtypeStruct((1, A), F32),
                   jax.ShapeDtypeStruct((1, A), F32)],
        compiler_params=_params(1),
    )(*dqs, *dks, *dvs, z, z, gq, gk, bmat, fmat)


def _local_step(x, tgt, S, comm, hd):
    T, D = x.shape
    C = S["conv_b_dw"].shape[1]
    A = C
    H = A // hd
    Dmix = C + A
    c0 = (2 * C) // A
    slopes = 2.0 ** (-ALIBI_MAX_BIAS * jnp.arange(1, H + 1, dtype=F32) / H)
    seg = jnp.arange(A) // hd
    bmat = (seg[:, None] == seg[None, :]).astype(BF16)
    pos_in_head = jnp.arange(A) % hd
    fmat = (pos_in_head[:, None] == pos_in_head[None, :]).astype(BF16)
    gq = jnp.tile(S["q_norm_g"], (1, H)) * QK_SCALE
    gk = jnp.tile(S["k_norm_g"], (1, H))

    wg1, wu1 = comm.weights(("ffn1_w_gate", "ffn1_w_up"), None)
    h1, gate1, up1, a1 = _norm_matmul("ffn1_up", x, comm.tie(S["ffn1_norm_g"]), [wg1, wu1], True)
    wd1, win, w32 = comm.weights(("ffn1_w_down", "w_in", "conv_w32"), a1)
    x1 = _matmul_res("ffn1_down", a1, wd1, x, 0.5)
    h2, z = _norm_matmul("mix_in", x1, S["mix_norm_g"], [win], False)
    yc, ycv = _conv_fwd("conv_fwd", z, w32, S["conv_b_dw"], S["conv_ln_g"], S["conv_ln_b"], C)
    qkv = _attn_prep("attn_prep", z, gq, gk, bmat, A, c0, hd)
    yatt, lg = _attn_fwd("attn_fwd", qkv, slopes, A)
    ycat = jnp.concatenate([yc, yatt], axis=1)
    wout, wg2, wu2, wd2 = comm.weights(("w_out", "ffn2_w_gate", "ffn2_w_up", "ffn2_w_down"), yatt)
    x2 = _matmul_res("mix_out", ycat, wout, x1, 1.0)
    h3, gate2, up2, a2 = _norm_matmul("ffn2_up", x2, S["ffn2_norm_g"], [wg2, wu2], True)
    dx3, dx3b, lossvec = _matmul_res("ffn2_down_loss", a2, wd2, x2, 0.5, tgt=tgt)

    G = {}
    dgate2, dup2 = _nt_matmul("ffn2_dact", dx3b, wd2, 0.5, gate2, up2)
    comm.reduce_begin("ffn2", {"ffn2_w_down": _tn_matmul("ffn2_dwd", a2, dx3b, 0.5),
                               "ffn2_w_gate": _tn_matmul("ffn2_dwg", h3, dgate2),
                               "ffn2_w_up": _tn_matmul("ffn2_dwu", h3, dup2)})
    dx2, dx2b, G["ffn2_norm_g"] = _nt_rms_bwd("ffn2_dx", [dgate2, dup2], [wg2, wu2],
                                              x2, comm.tie(S["ffn2_norm_g"]), dx3)
    dwout = _tn_matmul("mix_dwout", ycat, dx2b)
    dycat, _ = _nt_matmul("mix_dycat", dx2b, wout)
    dzc, G["conv_w32"], G["conv_b_dw"], G["conv_ln_g"], G["conv_ln_b"] = _conv_bwd(
        "conv_bwd", z, ycv, dycat, w32, S["conv_ln_g"], S["conv_ln_b"], C)
    dq, dk, dv = _attn_bwd("attn_bwd", qkv, dycat, yatt, lg, slopes, A, C)
    dzqkv, G["q_norm_g"], G["k_norm_g"] = _attn_bwd_combine(
        "attn_bwd_combine", [dq], [dk], [dv], z, gq, gk, bmat, fmat, A, c0, hd)
    comm.reduce_end("ffn2", dzqkv)
    dz = jnp.concatenate([dzc, dzqkv], axis=1)
    comm.reduce_begin("mix", {"w_out": dwout, "w_in": _tn_matmul("mix_dwin", h2, dz)})
    dx1, dx1b, G["mix_norm_g"] = _nt_rms_bwd("mix_dx", [dz], [win], x1, comm.tie(S["mix_norm_g"]), dx2)
    dgate1, dup1 = _nt_matmul("ffn1_dact", dx1b, wd1, 0.5, gate1, up1)
    dwd1 = _tn_matmul("ffn1_dwd", a1, dx1b, 0.5)
    dwg1 = _tn_matmul("ffn1_dwg", h1, dgate1)
    dwu1 = _tn_matmul("ffn1_dwu", h1, dup1)
    comm.reduce_end("mix", dwu1)
    comm.reduce_begin("ffn1", {"ffn1_w_down": dwd1, "ffn1_w_gate": dwg1, "ffn1_w_up": dwu1})
    dx0, _, G["ffn1_norm_g"] = _nt_rms_bwd("ffn1_dx", [dgate1, dup1], [wg1, wu1],
                                           x, comm.tie(S["ffn1_norm_g"]), dx1)
    comm.reduce_end("ffn1", dx0)
    return lossvec, dx0, G


BIG = (("ffn1_w_gate", 1), ("ffn1_w_up", 1), ("ffn1_w_down", 0), ("w_in", 1), ("w_out", 0),
       ("ffn2_w_gate", 1), ("ffn2_w_up", 1), ("ffn2_w_down", 0))
AXIS = dict(BIG)
FLIPS = ((1, 0), (0, 1), (1, 1))
HBM = pl.BlockSpec(memory_space=pltpu.HBM)
SEM = pl.BlockSpec(memory_space=pltpu.SEMAPHORE)
EFFECT = pltpu.SideEffectType.DATAFLOW_SIDE_EFFECTING
TOKEN = jax.ShapeDtypeStruct((8, LANES), F32)


def _window(ref, shape, axis, slab=None, half=None):
    idx = [pl.ds(0, shape[0]), pl.ds(0, shape[1])]
    if slab is not None:
        n = shape[axis] // 4
        idx[axis] = pl.ds(pl.multiple_of(slab * n, 8), n)
    if half is not None:
        hs = shape[1 - axis] // 2
        idx[1 - axis] = pl.ds(pl.multiple_of(half * hs, 8), hs)
    return ref.at[idx[0], idx[1]]


def _position():
    return lax.axis_index("x"), lax.axis_index("y"), lax.axis_index("c")


def _half_shape(shape, axis):
    return (shape[0] // 2, shape[1]) if axis == 1 else (shape[0], shape[1] // 2)


def _slab_shape(shape, axis):
    return (shape[0], shape[1] // 4) if axis == 1 else (shape[0] // 4, shape[1])


def _piece_shape(shape, axis):
    return _half_shape(_slab_shape(shape, axis), axis)


def _full_shape(shard, axis):
    return (shard.shape[0], shard.shape[1] * 4) if axis == 1 else (shard.shape[0] * 4, shard.shape[1])


def _hbm(a):
    return pltpu.with_memory_space_constraint(a, pltpu.HBM)


def _remote(src, dst, send_sem, recv_sem, to):
    return pltpu.make_async_remote_copy(src_ref=src, dst_ref=dst, send_sem=send_sem, recv_sem=recv_sem,
                                        device_id=to, device_id_type=MESH)


def _place(name, pos, w, axis):
    R, Cc = w.shape
    tr = _pick(R, (256, 128, 64, 32, 16))
    nrb = R // tr

    def body(pos_ref, w_ref, o_ref):
        o_ref[...] = w_ref[...].astype(BF16)

    omap = (lambda i, p: (i, p[0])) if axis == 1 else (lambda i, p: (p[0] * nrb + i, 0))
    return pl.pallas_call(
        body, name=name,
        grid_spec=pltpu.PrefetchScalarGridSpec(
            num_scalar_prefetch=1, grid=(nrb,), in_specs=[pl.BlockSpec((tr, Cc), lambda i, p: (i, 0))],
            out_specs=pl.BlockSpec((tr, Cc), omap)),
        out_shape=jax.ShapeDtypeStruct(_full_shape(w, axis), BF16), compiler_params=_params(1),
    )(pos, w)


def _gather_now(name, axes, fulls):
    nt = len(fulls)
    shapes = [f.shape for f in fulls]

    def body(*refs):
        outs, token = refs[nt:2 * nt], refs[2 * nt]
        send_sems, recv_sems = refs[2 * nt + 1:]
        x, y, c = _position()
        j0 = 2 * x + y
        sib = (x, y, 1 - c)

        def copy(t, k, slab, half, to):
            win = _window(outs[t], shapes[t], axes[t], slab=slab, half=half)
            return _remote(win, win, send_sems.at[t, k], recv_sems.at[t, k], to)

        sends = []
        for k, (fx, fy) in enumerate(FLIPS):
            for t in range(nt):
                cp = copy(t, k, j0, c, (x ^ fx, y ^ fy, c))
                cp.start()
                sends.append(cp)
        for k, (fx, fy) in enumerate(FLIPS):
            js = 2 * (x ^ fx) + (y ^ fy)
            for t in range(nt):
                copy(t, k, js, c, sib).wait_recv()
                cp = copy(t, 3 + k, js, c, sib)
                cp.start()
                sends.append(cp)
        for k, (fx, fy) in enumerate(FLIPS):
            js = 2 * (x ^ fx) + (y ^ fy)
            for t in range(nt):
                copy(t, 3 + k, js, 1 - c, sib).wait_recv()
        for cp in sends:
            cp.wait_send()
        token[...] = jnp.zeros_like(token)

    res = pl.pallas_call(
        body, name=name, in_specs=[ANY] * nt,
        out_specs=[ANY] * nt + [pl.BlockSpec(memory_space=pltpu.VMEM)],
        out_shape=[jax.ShapeDtypeStruct(s, BF16) for s in shapes] + [TOKEN],
        input_output_aliases={t: t for t in range(nt)},
        scratch_shapes=[pltpu.SemaphoreType.DMA((nt, 6)), pltpu.SemaphoreType.DMA((nt, 6))],
    )(*fulls)
    return list(res[:nt]), res[nt]


def _split_start(name, arrays, ncopies, plan):
    na = len(arrays)

    def body(*refs):
        ins = refs[:na]
        send_sems, recv_sems = refs[na], refs[na + 1]
        token = refs[-1]
        x, y, c = _position()
        for i, (src, dst, to) in enumerate(plan(ins, x, y, c)):
            _remote(src, dst, send_sems.at[i], recv_sems.at[i], to).start()
        token[...] = jnp.zeros_like(token)

    res = pl.pallas_call(
        body, name=name, in_specs=[HBM] * na,
        out_specs=tuple([SEM, SEM] + [HBM] * na + [pl.BlockSpec(memory_space=pltpu.VMEM)]),
        out_shape=tuple([pltpu.SemaphoreType.DMA((ncopies,)), pltpu.SemaphoreType.DMA((ncopies,))]
                        + [pltpu.HBM(a.shape, a.dtype) for a in arrays] + [TOKEN]),
        input_output_aliases={i: 2 + i for i in range(na)},
        compiler_params=pltpu.CompilerParams(has_side_effects=EFFECT),
    )(*[_hbm(a) for a in arrays])
    return (res[0], res[1]), list(res[2:2 + na]), res[-1]


def _split_wait(name, arrays, sems, after, plan):
    na = len(arrays)

    def body(*refs):
        ins = refs[:na]
        send_sems, recv_sems = refs[na], refs[na + 1]
        x, y, c = _position()
        for i, (src, dst, to) in enumerate(plan(ins, x, y, c)):
            cp = _remote(src, dst, send_sems.at[i], recv_sems.at[i], to)
            cp.wait_send()
            cp.wait_recv()

    res = pl.pallas_call(
        body, name=name, in_specs=[HBM] * na + [SEM, SEM, ANY],
        out_specs=tuple([HBM] * na), out_shape=tuple(pltpu.HBM(a.shape, a.dtype) for a in arrays),
        input_output_aliases={i: i for i in range(na)},
        compiler_params=pltpu.CompilerParams(has_side_effects=EFFECT),
    )(*arrays, *sems, after)
    return list(res)


def _gather_plan(axes, shapes, conv_shape):
    nt = len(axes)

    def plan(refs, x, y, c):
        j0 = 2 * x + y
        out = []
        for fx, fy in FLIPS:
            to = (x ^ fx, y ^ fy, c)
            for t in range(nt):
                win = _window(refs[t], shapes[t], axes[t], slab=j0, half=c)
                out.append((win, win, to))
            if conv_shape is not None:
                win = _window(refs[nt], conv_shape, 1, slab=j0)
                out.append((win, win, to))
        return out

    return plan


def _gather_finish(name, axes, fulls):
    nt = len(axes)
    shapes = [f.shape for f in fulls]

    def body(*refs):
        outs = refs[nt:2 * nt]
        send_sems, recv_sems = refs[2 * nt:]
        x, y, c = _position()
        sib = (x, y, 1 - c)
        cps = []
        for k, (fx, fy) in enumerate(FLIPS):
            js = 2 * (x ^ fx) + (y ^ fy)
            for t in range(nt):
                landed = _window(outs[t], shapes[t], axes[t], slab=js, half=c)
                cp = _remote(landed, landed, send_sems.at[t, k], recv_sems.at[t, k], sib)
                cp.start()
                cps.append(cp)
        for k, (fx, fy) in enumerate(FLIPS):
            js = 2 * (x ^ fx) + (y ^ fy)
            for t in range(nt):
                other = _window(outs[t], shapes[t], axes[t], slab=js, half=1 - c)
                _remote(other, other, send_sems.at[t, k], recv_sems.at[t, k], sib).wait_recv()
        for cp in cps:
            cp.wait_send()

    res = pl.pallas_call(
        body, name=name, in_specs=[ANY] * nt, out_specs=[ANY] * nt,
        out_shape=[jax.ShapeDtypeStruct(f.shape, f.dtype) for f in fulls],
        input_output_aliases={t: t for t in range(nt)},
        scratch_shapes=[pltpu.SemaphoreType.DMA((nt, 3)), pltpu.SemaphoreType.DMA((nt, 3))],
    )(*fulls)
    return list(res)


def _pair_exchange(name, srcs, windows, out_shapes, dtype):
    nt = len(srcs)

    def body(*refs):
        ins, outs = refs[:nt], refs[nt:2 * nt]
        send_sems, recv_sems = refs[2 * nt:]
        x, y, c = _position()
        cps = []
        for t in range(nt):
            cp = _remote(windows[t](ins[t], c), outs[t], send_sems.at[t], recv_sems.at[t], (x, y, 1 - c))
            cp.start()
            cps.append(cp)
        for cp in cps:
            cp.wait()

    return pl.pallas_call(
        body, name=name, in_specs=[ANY] * nt, out_specs=[ANY] * nt,
        out_shape=[jax.ShapeDtypeStruct(s, dtype) for s in out_shapes],
        scratch_shapes=[pltpu.SemaphoreType.DMA((nt,)), pltpu.SemaphoreType.DMA((nt,))],
    )(*srcs)


def _scatter_plan(axes, shapes):
    nt = len(axes)

    def plan(refs, x, y, c):
        out = []
        for k, (fx, fy) in enumerate(FLIPS):
            js = 2 * (x ^ fx) + (y ^ fy)
            for t in range(nt):
                src = _window(refs[t], _half_shape(shapes[t], axes[t]), axes[t], slab=js)
                out.append((src, refs[nt + t].at[k], (x ^ fx, y ^ fy, c)))
        return out

    return plan


def _gather_small(packed):
    R, Cc = packed.shape

    def body(p_ref, o_ref, send_sems, recv_sems, loc_sem):
        x, y, c = _position()
        me = 4 * x + 2 * y + c
        mine = pltpu.make_async_copy(p_ref, o_ref.at[me], loc_sem)
        mine.start()
        cps = []
        for k in range(1, 8):
            fx, fy, fc = (k >> 2) & 1, (k >> 1) & 1, k & 1
            cp = pltpu.make_async_remote_copy(
                src_ref=p_ref, dst_ref=o_ref.at[me], send_sem=send_sems.at[k - 1], recv_sem=recv_sems.at[k - 1],
                device_id=(x ^ fx, y ^ fy, c ^ fc), device_id_type=MESH)
            cp.start()
            cps.append(cp)
        for cp in cps:
            cp.wait()
        mine.wait()

    return pl.pallas_call(
        body, name="gather_small_grads", in_specs=[ANY], out_specs=ANY,
        out_shape=jax.ShapeDtypeStruct((8, R, Cc), F32),
        scratch_shapes=[pltpu.SemaphoreType.DMA((7,)), pltpu.SemaphoreType.DMA((7,)), pltpu.SemaphoreType.DMA],
    )(packed)


def _sum_slots(name, slots):
    n, R, Cc = slots.shape

    def body(s_ref, o_ref):
        t = s_ref[0]
        for i in range(1, n):
            t = t + s_ref[i]
        o_ref[...] = t

    return pl.pallas_call(
        body, name=name, grid=(1,), in_specs=[pl.BlockSpec((n, R, Cc), lambda i: (0, 0, 0))],
        out_specs=pl.BlockSpec((R, Cc), lambda i: (0, 0)), out_shape=jax.ShapeDtypeStruct((R, Cc), F32),
        compiler_params=_params(1),
    )(slots)


def _pair_sum(name, pos, g, land, shape, axis):
    hshape = _half_shape(shape, axis)
    R, Cc = hshape
    tr = _pick(R, (256, 128, 64, 32, 16))
    nrb = R // tr

    def body(pos_ref, g_ref, l_ref, o_ref):
        o_ref[...] = (g_ref[...].astype(F32) + l_ref[...].astype(F32)).astype(BF16)

    if axis == 1:
        gmap = lambda i, p: (p[1] * nrb + i, 0)
    else:
        gmap = lambda i, p: (i, p[1])
    blk = pl.BlockSpec((tr, Cc), lambda i, p: (i, 0))
    return pl.pallas_call(
        body, name=name,
        grid_spec=pltpu.PrefetchScalarGridSpec(
            num_scalar_prefetch=1, grid=(nrb,), in_specs=[pl.BlockSpec((tr, Cc), gmap), blk], out_specs=blk),
        out_shape=jax.ShapeDtypeStruct(hshape, BF16), compiler_params=_params(1),
    )(pos, g, land)


def _chip_sum(name, pos, sb, land, shape, axis):
    hshape = _half_shape(shape, axis)
    pshape = _piece_shape(shape, axis)
    R, Cc = pshape
    tr = _pick(R, (256, 128, 64, 32, 16))
    nrb = R // tr

    def body(pos_ref, s_ref, l_ref, o_ref):
        t = s_ref[...].astype(F32)
        for k in range(3):
            t = t + l_ref[k].astype(F32)
        o_ref[...] = t

    if axis == 1:
        smap = lambda i, p: (i, p[0])
    else:
        smap = lambda i, p: (p[0] * nrb + i, 0)
    return pl.pallas_call(
        body, name=name,
        grid_spec=pltpu.PrefetchScalarGridSpec(
            num_scalar_prefetch=1, grid=(nrb,),
            in_specs=[pl.BlockSpec((tr, Cc), smap), pl.BlockSpec((3, tr, Cc), lambda i, p: (0, i, 0))],
            out_specs=pl.BlockSpec((tr, Cc), lambda i, p: (i, 0))),
        out_shape=jax.ShapeDtypeStruct(pshape, F32), compiler_params=_params(1),
    )(pos, sb, land)


def _adam_math(w, g, m, v):
    m = ADAM_B1 * m + (1.0 - ADAM_B1) * g
    v = ADAM_B2 * v + (1.0 - ADAM_B2) * (g * g)
    m_hat = m / (1.0 - ADAM_B1 ** ADAM_STEP)
    v_hat = v / (1.0 - ADAM_B2 ** ADAM_STEP)
    delta = -ADAM_LR * (m_hat / (jnp.sqrt(v_hat) + ADAM_EPS) + ADAM_WD * w)
    return delta, m, v


def _adamw_halves(name, pos, w, m, v, mine, theirs, axis):
    R, Cc = w.shape
    hr, hc = mine.shape
    tr = _pick(hr, (256, 128, 64, 32, 16))
    nrb = hr // tr

    def body(pos_ref, w_ref, m_ref, v_ref, a_ref, b_ref, g_ref, d_ref, nm_ref, nv_ref):
        half = pl.program_id(0)
        g = jnp.where(half == pos_ref[1], a_ref[...], b_ref[...])
        d, nm, nv = _adam_math(w_ref[...], g, m_ref[...], v_ref[...])
        g_ref[...] = g
        d_ref[...] = d
        nm_ref[...] = nm
        nv_ref[...] = nv

    if axis == 1:
        wmap = lambda h, i, p: (h * nrb + i, 0)
    else:
        wmap = lambda h, i, p: (i, h)
    wblk = pl.BlockSpec((tr, hc), wmap)
    hblk = pl.BlockSpec((tr, hc), lambda h, i, p: (i, 0))
    return pl.pallas_call(
        body, name=name,
        grid_spec=pltpu.PrefetchScalarGridSpec(
            num_scalar_prefetch=1, grid=(2, nrb), in_specs=[wblk, wblk, wblk, hblk, hblk], out_specs=[wblk] * 4),
        out_shape=[jax.ShapeDtypeStruct((R, Cc), F32)] * 4, compiler_params=_params(2),
    )(pos, w, m, v, mine, theirs)


def _adamw_small(name, w, g, m, v):
    def body(w_ref, g_ref, m_ref, v_ref, d_ref, nm_ref, nv_ref):
        d, nm, nv = _adam_math(w_ref[...], g_ref[...], m_ref[...], v_ref[...])
        d_ref[...] = d
        nm_ref[...] = nm
        nv_ref[...] = nv

    blk = pl.BlockSpec(w.shape, lambda i: (0, 0))
    return pl.pallas_call(
        body, name=name, grid=(1,), in_specs=[blk] * 4, out_specs=[blk] * 3,
        out_shape=[jax.ShapeDtypeStruct(w.shape, F32)] * 3, compiler_params=_params(1),
    )(w, g, m, v)


SMALL = ("ffn1_norm_g", "mix_norm_g", "conv_b_dw", "conv_ln_g", "conv_ln_b", "q_norm_g", "k_norm_g", "ffn2_norm_g")
ORDER = ("ffn1_norm_g", "ffn1_w_gate", "ffn1_w_up", "ffn1_w_down", "mix_norm_g", "w_in", "conv_w_dw", "conv_b_dw",
         "conv_ln_g", "conv_ln_b", "q_norm_g", "k_norm_g", "w_out", "ffn2_norm_g", "ffn2_w_gate", "ffn2_w_up",
         "ffn2_w_down")
GATHER_FIRST = ("ffn1_w_gate", "ffn1_w_up")
GATHER_SECOND = ("ffn1_w_down", "w_in")
GATHER_THIRD = ("w_out", "ffn2_w_gate", "ffn2_w_up", "ffn2_w_down")


class _Exchange:
    def __init__(self, P, Mo, Vo, conv_shard, pos):
        self.P, self.Mo, self.Vo, self.pos = P, Mo, Vo, pos
        self.tokens = []
        self.pending = {}
        self.reducing = {}
        self.results = {}
        placed = {n: _place("place_" + n, pos, P[n][0], a) for n, a in BIG}
        self.shapes = {n: placed[n].shape for n, _ in BIG}
        first, tok = _gather_now("gather_first", [AXIS[n] for n in GATHER_FIRST], [placed[n] for n in GATHER_FIRST])
        self.ready = dict(zip(GATHER_FIRST, first))
        cq = conv_shard.shape[1]
        conv_full = lax.dynamic_update_slice(jnp.zeros((conv_shard.shape[0], 4 * cq), F32), conv_shard,
                                             (0, pos[0] * cq))
        for gname, names, conv in (("second", GATHER_SECOND, conv_full), ("third", GATHER_THIRD, None)):
            axes = [AXIS[n] for n in names]
            shapes = [self.shapes[n] for n in names]
            arrays = [placed[n] for n in names] + ([conv] if conv is not None else [])
            small = min(range(len(arrays)), key=lambda i: arrays[i].size)
            arrays[small] = arrays[small] + tok[0, 0].astype(arrays[small].dtype)
            plan = _gather_plan(axes, shapes, conv.shape if conv is not None else None)
            sems, thru, tok = _split_start("gather_%s_start" % gname, arrays, 3 * len(arrays), plan)
            self.tokens.append(tok)
            for n in names + (("conv_w32",) if conv is not None else ()):
                self.pending[n] = (gname, names, axes, plan, sems, thru, conv is not None)

    def tie(self, v):
        for tok in self.tokens:
            v = v + tok[0:1, 0:1]
        self.tokens = []
        return v

    def weights(self, names, after):
        if names[0] in self.pending:
            gname, gnames, axes, plan, sems, thru, has_conv = self.pending[names[0]]
            thru = _split_wait("gather_%s_wait" % gname, thru, sems, after, plan)
            nt = len(gnames)
            fulls = _gather_finish("gather_%s_finish" % gname, axes, thru[:nt])
            for n, f in zip(gnames, fulls):
                self.ready[n] = f
                del self.pending[n]
            if has_conv:
                self.ready["conv_w32"] = thru[nt]
                del self.pending["conv_w32"]
        return [self.ready[n] for n in names]

    def reduce_begin(self, gname, grads):
        names = list(grads)
        axes = [AXIS[n] for n in names]
        shapes = [self.shapes[n] for n in names]
        gs = [grads[n] for n in names]
        to_sibling = [(lambda ref, c, s=s, a=a: _window(ref, s, a, half=1 - c)) for s, a in zip(shapes, axes)]
        landed = _pair_exchange("pair_exchange_" + gname, gs, to_sibling,
                                [_half_shape(s, a) for s, a in zip(shapes, axes)], BF16)
        sbs = [_pair_sum("pair_sum_" + n, self.pos, g, l, s, a)
               for n, a, g, l, s in zip(names, axes, gs, landed, shapes)]
        lands = [lax.empty((3,) + _piece_shape(s, a), BF16) for s, a in zip(shapes, axes)]
        plan = _scatter_plan(axes, shapes)
        sems, thru, tok = _split_start("scatter_%s_start" % gname, sbs + lands, 3 * len(names), plan)
        self.tokens.append(tok)
        self.reducing[gname] = (names, axes, shapes, plan, sems, thru)

    def reduce_end(self, gname, after):
        names, axes, shapes, plan, sems, thru = self.reducing.pop(gname)
        nt = len(names)
        thru = _split_wait("scatter_%s_wait" % gname, thru, sems, after, plan)
        mine = [_chip_sum("chip_sum_" + n, self.pos, sb, l, s, a)
                for n, a, sb, l, s in zip(names, axes, thru[:nt], thru[nt:], shapes)]
        theirs = _pair_exchange("half_exchange_" + gname, mine, [(lambda ref, c: ref)] * nt,
                                [m.shape for m in mine], F32)
        for n, a, mi, th in zip(names, axes, mine, theirs):
            g, d, nm, nv = _adamw_halves("adamw_" + n, self.pos, self.P[n][0], self.Mo[n][0], self.Vo[n][0],
                                         mi, th, a)
            self.results[n] = (g[None], d[None], nm[None], nv[None])


def kernel(x, ffn1_norm_g, ffn1_w_gate, ffn1_w_up, ffn1_w_down, mix_norm_g, w_in, conv_w_dw, conv_b_dw, conv_ln_g, conv_ln_b, q_norm_g, k_norm_g, w_out, ffn2_norm_g, ffn2_w_gate, ffn2_w_up, ffn2_w_down, loss_target, m_ffn1_norm_g, m_ffn1_w_gate, m_ffn1_w_up, m_ffn1_w_down, m_mix_norm_g, m_w_in, m_conv_w_dw, m_conv_b_dw, m_conv_ln_g, m_conv_ln_b, m_q_norm_g, m_k_norm_g, m_w_out, m_ffn2_norm_g, m_ffn2_w_gate, m_ffn2_w_up, m_ffn2_w_down, v_ffn1_norm_g, v_ffn1_w_gate, v_ffn1_w_up, v_ffn1_w_down, v_mix_norm_g, v_w_in, v_conv_w_dw, v_conv_b_dw, v_conv_ln_g, v_conv_ln_b, v_q_norm_g, v_k_norm_g, v_w_out, v_ffn2_norm_g, v_ffn2_w_gate, v_ffn2_w_up, v_ffn2_w_down):
    args = dict(locals())
    P = {n: args[n] for n in ORDER}
    Mo = {n: args["m_" + n] for n in ORDER}
    Vo = {n: args["v_" + n] for n in ORDER}
    xs = x[0]
    tgt = loss_target[0]
    T, D = xs.shape
    hd = q_norm_g.shape[-1]
    C = conv_b_dw.shape[-1]
    ntap = conv_w_dw.shape[1]
    cx, cy, cc = _position()
    j0 = 2 * cx + cy
    pos = jnp.stack([j0, cc]).astype(jnp.int32)

    conv_shard = jnp.pad(conv_w_dw[0], ((0, HALO - ntap), (0, 0)))
    comm = _Exchange(P, Mo, Vo, conv_shard, pos)
    lossvec, dx0, G = _local_step(xs, tgt, {n: P[n] for n in SMALL}, comm, hd)
    loss = lax.psum(0.5 / D * jnp.sum(lossvec), AXES)
    grads, deltas, new_m, new_v = {}, {}, {}, {}
    for n, _ in BIG:
        grads[n], deltas[n], new_m[n], new_v[n] = comm.results[n]

    rows = [G["conv_w32"]]
    for n in ("ffn1_norm_g", "mix_norm_g", "ffn2_norm_g"):
        rows.append(G[n].reshape(D // C, C))
    for n in ("conv_b_dw", "conv_ln_g", "conv_ln_b", "q_norm_g", "k_norm_g"):
        rows.append(G[n])
    packed = jnp.concatenate(rows, axis=0)
    packed = jnp.pad(packed, ((0, -packed.shape[0] % 8), (0, 0)))
    total = _sum_slots("sum_small_grads", _gather_small(packed))
    r = HALO
    small_g = {}
    cq = C // 4
    small_g["conv_w_dw"] = lax.dynamic_slice(total[:ntap], (0, j0 * cq), (ntap, cq))
    for n in ("ffn1_norm_g", "mix_norm_g", "ffn2_norm_g"):
        small_g[n] = total[r:r + D // C].reshape(1, D)
        r += D // C
    for n in ("conv_b_dw", "conv_ln_g", "conv_ln_b"):
        small_g[n] = total[r:r + 1]
        r += 1
    for n in ("q_norm_g", "k_norm_g"):
        small_g[n] = total[r:r + 1, :hd]
        r += 1
    for n in ("conv_w_dw",) + SMALL:
        lead = n == "conv_w_dw"
        w2, m2, v2 = (P[n][0], Mo[n][0], Vo[n][0]) if lead else (P[n], Mo[n], Vo[n])
        d, nm, nv = _adamw_small("adamw_" + n, w2, small_g[n], m2, v2)
        if lead:
            grads[n], deltas[n], new_m[n], new_v[n] = small_g[n][None], d[None], nm[None], nv[None]
        else:
            grads[n], deltas[n], new_m[n], new_v[n] = small_g[n], d, nm, nv

    return (loss, dx0[None], *[grads[n] for n in ORDER], *[deltas[n] for n in ORDER],
            *[new_m[n] for n in ORDER], *[new_v[n] for n in ORDER])
```

```python
import jax
import jax.numpy as jnp
from jax import lax
from jax.experimental import pallas as pl
from jax.experimental.pallas import tpu as pltpu

F32 = jnp.float32
BF16 = jnp.bfloat16
EPS = 1e-6
WINDOW = 128
DILATIONS = (1, 4, 16)
ALIBI_MAX_BIAS = 8.0
LANES = 128
HALO = 32
ADAM_LR, ADAM_B1, ADAM_B2, ADAM_EPS, ADAM_WD, ADAM_STEP = 0.001, 0.9, 0.999, 1e-08, 0.01, 10
VMEM_LIMIT_MB = 62
ROW_TILE = 1024
TN_ACC_ELEMS = 3 * 1024 * 1024
ONCE_PER_ROW_TILE = pl.Buffered(1)
EPILOGUE_ROWS = 256
ACC_COLS = 512
MESH = pl.DeviceIdType.MESH
ANY = pl.BlockSpec(memory_space=pl.ANY)
AXES = ("x", "y", "c")
NEG = -1e30


def _pick(n, cands):
    for c in cands:
        if n % c == 0:
            return c
    return n


def _params(nsem):
    return pltpu.CompilerParams(dimension_semantics=("arbitrary",) * nsem,
                                vmem_limit_bytes=VMEM_LIMIT_MB << 20)


def _nn(a, b):
    return jnp.dot(a, b, preferred_element_type=F32)


def _nt(a, b):
    return lax.dot_general(a, b, (((1,), (1,)), ((), ())), preferred_element_type=F32)


def _tn(a, b):
    return lax.dot_general(a, b, (((0,), (0,)), ((), ())), preferred_element_type=F32)


def _sigmoid(v):
    return jax.nn.sigmoid(v)


def _rms_r(xv):
    return lax.rsqrt(jnp.mean(xv * xv, axis=-1, keepdims=True) + EPS)


def _norm_matmul(name, x, g, ws, swiglu):
    T, D = x.shape
    N = ws[0].shape[1]
    tm = _pick(T, (ROW_TILE, 512, 256, 128))
    tn = _pick(N, (512, 256, 128))
    nw = len(ws)

    def body(*refs):
        x_ref, g_ref = refs[:2]
        w_refs = refs[2:2 + nw]
        outs = refs[2 + nw:-1]
        hs = refs[-1]

        @pl.when(pl.program_id(1) == 0)
        def _():
            for r0 in range(0, tm, EPILOGUE_ROWS):
                rows = slice(r0, r0 + min(EPILOGUE_ROWS, tm))
                xv = x_ref[rows, :]
                hv = (xv * _rms_r(xv) * g_ref[...]).astype(BF16)
                hs[rows, :] = hv
                outs[0][rows, :] = hv

        h = hs[...]
        if swiglu:
            gt = _nn(h, w_refs[0][...])
            u = _nn(h, w_refs[1][...])
            outs[1][...] = gt.astype(BF16)
            outs[2][...] = u.astype(BF16)
            outs[3][...] = (gt * _sigmoid(gt) * u).astype(BF16)
        else:
            outs[1][...] = _nn(h, w_refs[0][...])

    row = pl.BlockSpec((tm, D), lambda i, j: (i, 0))
    col = pl.BlockSpec((D, tn), lambda i, j: (0, j))
    tile = pl.BlockSpec((tm, tn), lambda i, j: (i, j))
    if swiglu:
        out_shape = [jax.ShapeDtypeStruct((T, D), BF16)] + [jax.ShapeDtypeStruct((T, N), BF16)] * 3
        out_specs = [row, tile, tile, tile]
    else:
        out_shape = [jax.ShapeDtypeStruct((T, D), BF16), jax.ShapeDtypeStruct((T, N), F32)]
        out_specs = [row, tile]
    return pl.pallas_call(
        body, name=name, grid=(T // tm, N // tn),
        in_specs=[row, pl.BlockSpec((1, D), lambda i, j: (0, 0))] + [col] * nw,
        out_specs=out_specs, out_shape=out_shape,
        scratch_shapes=[pltpu.VMEM((tm, D), BF16)],
        compiler_params=_params(2),
    )(x, g, *ws)


def _matmul_res(name, a, w, res, scale, tgt=None):
    T, K = a.shape
    N = w.shape[1]
    tm = _pick(T, (ROW_TILE, 512, 256, 128))
    tk = _pick(K, (512, 256, 128))
    nk = K // tk
    loss = tgt is not None

    def body(*refs):
        if loss:
            a_ref, w_ref, res_ref, tgt_ref, acc, dxb_ref, lv_ref = refs
            dx_ref = acc
        else:
            a_ref, w_ref, res_ref, acc = refs
            out_ref = acc
        i, k = pl.program_id(0), pl.program_id(1)

        @pl.when(k == 0)
        def _():
            acc[...] = jnp.zeros_like(acc)

        for c0 in range(0, N, ACC_COLS):
            cols = slice(c0, min(c0 + ACC_COLS, N))
            acc[:, cols] += _nn(a_ref[...], w_ref[:, cols])

        @pl.when(k == nk - 1)
        def _():
            part = jnp.zeros((1, N), F32)
            for r0 in range(0, tm, EPILOGUE_ROWS):
                rows = slice(r0, r0 + min(EPILOGUE_ROWS, tm))
                val = res_ref[rows, :] + scale * acc[rows, :]
                if loss:
                    dv = val - tgt_ref[rows, :]
                    dx = dv * (1.0 / N)
                    dx_ref[rows, :] = dx
                    dxb_ref[rows, :] = dx.astype(BF16)
                    part = part + jnp.sum(dv * dv, axis=0, keepdims=True)
                else:
                    out_ref[rows, :] = val
            if loss:
                @pl.when(i == 0)
                def _():
                    lv_ref[...] = part

                @pl.when(i > 0)
                def _():
                    lv_ref[...] += part

    row = pl.BlockSpec((tm, N), lambda i, k: (i, 0), pipeline_mode=ONCE_PER_ROW_TILE)
    in_specs = [pl.BlockSpec((tm, tk), lambda i, k: (i, k)), pl.BlockSpec((tk, N), lambda i, k: (k, 0)), row]
    args = [a, w, res]
    if loss:
        in_specs.append(row)
        args.append(tgt)
        out_specs = [row, row, pl.BlockSpec((1, N), lambda i, k: (0, 0))]
        out_shape = [jax.ShapeDtypeStruct((T, N), F32), jax.ShapeDtypeStruct((T, N), BF16),
                     jax.ShapeDtypeStruct((1, N), F32)]
    else:
        out_specs = row
        out_shape = jax.ShapeDtypeStruct((T, N), F32)
    return pl.pallas_call(
        body, name=name, grid=(T // tm, nk), in_specs=in_specs, out_specs=out_specs, out_shape=out_shape,
        compiler_params=_params(2),
    )(*args)


def _nt_matmul(name, dyb, w, scale=1.0, gate=None, up=None):
    T, D = dyb.shape
    N = w.shape[0]
    tm = _pick(T, (ROW_TILE, 512, 256, 128))
    tn = _pick(N, (512, 256, 128))
    swiglu = gate is not None

    def body(*refs):
        if swiglu:
            dy_ref, w_ref, g_ref, u_ref, dg_ref, du_ref = refs
        else:
            dy_ref, w_ref, o_ref, ob_ref = refs
        da = _nt(dy_ref[...], w_ref[...]) * scale
        if swiglu:
            gt = g_ref[...].astype(F32)
            u = u_ref[...].astype(F32)
            sg = _sigmoid(gt)
            dg_ref[...] = (da * u * (sg * (1.0 + gt * (1.0 - sg)))).astype(BF16)
            du_ref[...] = (da * (gt * sg)).astype(BF16)
        else:
            o_ref[...] = da
            ob_ref[...] = da.astype(BF16)

    tile = pl.BlockSpec((tm, tn), lambda i, j: (i, j))
    in_specs = [pl.BlockSpec((tm, D), lambda i, j: (i, 0)), pl.BlockSpec((tn, D), lambda i, j: (j, 0))]
    args = [dyb, w]
    if swiglu:
        in_specs += [tile, tile]
        args += [gate, up]
        out_shape = [jax.ShapeDtypeStruct((T, N), BF16)] * 2
    else:
        out_shape = [jax.ShapeDtypeStruct((T, N), F32), jax.ShapeDtypeStruct((T, N), BF16)]
    return pl.pallas_call(
        body, name=name, grid=(T // tm, N // tn), in_specs=in_specs, out_specs=[tile, tile],
        out_shape=out_shape, compiler_params=_params(2),
    )(*args)


def _nt_rms_bwd(name, As, Ws, x, g, dres):
    T, K = As[0].shape
    D = x.shape[1]
    tm = _pick(T, (ROW_TILE, 512, 256, 128))
    tk = _pick(K, (512, 256, 128))
    nk = K // tk
    na = len(As)

    def body(*refs):
        a_refs = refs[:na]
        w_refs = refs[na:2 * na]
        x_ref, g_ref, dres_ref, acc, dxb_ref, dg_ref = refs[2 * na:]
        dx_ref = acc
        i, k = pl.program_id(0), pl.program_id(1)

        @pl.when(k == 0)
        def _():
            acc[...] = jnp.zeros_like(acc)

        for a_ref, w_ref in zip(a_refs, w_refs):
            for c0 in range(0, D, ACC_COLS):
                cols = slice(c0, min(c0 + ACC_COLS, D))
                acc[:, cols] += _nt(a_ref[...], w_ref[cols, :])

        @pl.when(k == nk - 1)
        def _():
            part = jnp.zeros((1, D), F32)
            for r0 in range(0, tm, EPILOGUE_ROWS):
                rows = slice(r0, r0 + min(EPILOGUE_ROWS, tm))
                dh = acc[rows, :]
                xv = x_ref[rows, :]
                r = _rms_r(xv)
                gd = dh * g_ref[...]
                dx = dres_ref[rows, :] + r * gd - xv * (r * r * r) * jnp.mean(gd * xv, axis=-1, keepdims=True)
                dx_ref[rows, :] = dx
                dxb_ref[rows, :] = dx.astype(BF16)
                part = part + jnp.sum(dh * xv * r, axis=0, keepdims=True)

            @pl.when(i == 0)
            def _():
                dg_ref[...] = part

            @pl.when(i > 0)
            def _():
                dg_ref[...] += part

    row = pl.BlockSpec((tm, D), lambda i, k: (i, 0), pipeline_mode=ONCE_PER_ROW_TILE)
    vec = pl.BlockSpec((1, D), lambda i, k: (0, 0))
    return pl.pallas_call(
        body, name=name, grid=(T // tm, nk),
        in_specs=[pl.BlockSpec((tm, tk), lambda i, k: (i, k))] * na
        + [pl.BlockSpec((D, tk), lambda i, k: (0, k))] * na + [row, vec, row],
        out_specs=[row, row, vec],
        out_shape=[jax.ShapeDtypeStruct((T, D), F32), jax.ShapeDtypeStruct((T, D), BF16),
                   jax.ShapeDtypeStruct((1, D), F32)],
        compiler_params=_params(2),
    )(*As, *Ws, x, g, dres)


def _tn_matmul(name, a, b, scale=1.0):
    T, M = a.shape
    N = b.shape[1]
    tn = _pick(N, (2048, 1408, 1280, 1024, 512, 256, 128))
    tm = _pick(M, tuple(c for c in (2048, 1408, 1024, 512, 256, 128) if c * tn <= TN_ACC_ELEMS))
    tk = _pick(T, (1024, 512, 256, 128))
    nk = T // tk

    def body(a_ref, b_ref, o_ref, acc):
        k = pl.program_id(2)

        @pl.when(k == 0)
        def _():
            acc[...] = jnp.zeros_like(acc)

        for r0 in range(0, tm, ACC_COLS):
            rows = slice(r0, min(r0 + ACC_COLS, tm))
            acc[rows, :] += _tn(a_ref[:, rows], b_ref[...])

        @pl.when(k == nk - 1)
        def _():
            o_ref[...] = (acc[...] * scale).astype(BF16)

    return pl.pallas_call(
        body, name=name, grid=(M // tm, N // tn, nk),
        in_specs=[pl.BlockSpec((tk, tm), lambda i, j, k: (k, i)), pl.BlockSpec((tk, tn), lambda i, j, k: (k, j))],
        out_specs=pl.BlockSpec((tm, tn), lambda i, j, k: (i, j)),
        out_shape=jax.ShapeDtypeStruct((M, N), BF16),
        scratch_shapes=[pltpu.VMEM((tm, tn), F32)], compiler_params=_params(3),
    )(a, b)


CONV_ROWS = 128
ROW_CHUNK = 32
LANE_CHUNK = 256


def _conv_fwd(name, z, w32, b, lg, lb, C):
    T = z.shape[0]
    tc = CONV_ROWS
    ntap = 31
    lc = _pick(C, (LANE_CHUNK, LANES))
    rpb = tc // HALO

    def body(zc_ref, zp_ref, w_ref, b_ref, lg_ref, lb_ref, yc_ref, ycv_ref, vbuf, ybuf):
        i = pl.program_id(0)
        zc = zc_ref[...]
        zp = zp_ref[...]
        vbuf[HALO:HALO + tc, :] = zc[:, :C] * _sigmoid(zc[:, C:])
        vbuf[0:HALO, :] = jnp.where(i > 0, zp[:, :C] * _sigmoid(zp[:, C:]), 0.0)
        for r0 in range(0, tc, ROW_CHUNK):
            for c0 in range(0, C, lc):
                acc = jnp.zeros((ROW_CHUNK, lc), F32) + b_ref[:, c0:c0 + lc]
                for k in range(ntap):
                    s = r0 + 2 + k
                    acc = acc + w_ref[k:k + 1, c0:c0 + lc] * vbuf[s:s + ROW_CHUNK, c0:c0 + lc]
                ybuf[r0:r0 + ROW_CHUNK, c0:c0 + lc] = acc
        y = ybuf[...]
        ycv_ref[...] = y
        mu = jnp.mean(y, axis=-1, keepdims=True)
        yc = y - mu
        rstd = lax.rsqrt(jnp.mean(yc * yc, axis=-1, keepdims=True) + EPS)
        ln = yc * rstd * lg_ref[...] + lb_ref[...]
        yc_ref[...] = (ln * _sigmoid(ln)).astype(BF16)

    vec = pl.BlockSpec((1, C), lambda i: (0, 0))
    return pl.pallas_call(
        body, name=name, grid=(T // tc,),
        in_specs=[pl.BlockSpec((tc, 2 * C), lambda i: (i, 0)),
                  pl.BlockSpec((HALO, 2 * C), lambda i: (jnp.maximum(i * rpb - 1, 0), 0)),
                  pl.BlockSpec((HALO, C), lambda i: (0, 0)), vec, vec, vec],
        out_specs=[pl.BlockSpec((tc, C), lambda i: (i, 0))] * 2,
        out_shape=[jax.ShapeDtypeStruct((T, C), BF16), jax.ShapeDtypeStruct((T, C), F32)],
        scratch_shapes=[pltpu.VMEM((tc + HALO, C), F32), pltpu.VMEM((tc, C), F32)],
        compiler_params=_params(1),
    )(z, z, w32, b, lg, lb)


def _conv_bwd(name, z, ycv, dycat, w32, lg, lb, C):
    T = z.shape[0]
    tc = CONV_ROWS
    ntap = 31
    lc = _pick(C, (LANE_CHUNK, LANES))
    rpb = tc // HALO
    nstep = T // tc
    nhb = T // HALO

    def ln_bwd(dyc, y, lgv, lbv):
        mu = jnp.mean(y, axis=-1, keepdims=True)
        yc = y - mu
        rstd = lax.rsqrt(jnp.mean(yc * yc, axis=-1, keepdims=True) + EPS)
        yn = yc * rstd
        ln = yn * lgv + lbv
        sg = _sigmoid(ln)
        dln = dyc * (sg * (1.0 + ln * (1.0 - sg)))
        dyn = dln * lgv
        dy = rstd * (dyn - jnp.mean(dyn, axis=-1, keepdims=True)
                     - yn * jnp.mean(dyn * yn, axis=-1, keepdims=True))
        return dy, dln, yn

    def body(zc_ref, zp_ref, y_ref, yn_ref, d_ref, dn_ref, w_ref, lg_ref, lb_ref,
             dz_ref, dw_ref, db_ref, dlg_ref, dlb_ref, vbuf, dbuf, dvbuf, dwacc):
        i = pl.program_id(0)
        lgv, lbv = lg_ref[...], lb_ref[...]
        zc = zc_ref[...]
        zp = zp_ref[...]
        a = zc[:, :C]
        sgt = _sigmoid(zc[:, C:])
        vbuf[HALO:HALO + tc, :] = a * sgt
        vbuf[0:HALO, :] = jnp.where(i > 0, zp[:, :C] * _sigmoid(zp[:, C:]), 0.0)
        dy, dln, yn = ln_bwd(d_ref[...], y_ref[...], lgv, lbv)
        dbuf[0:tc, :] = dy
        dyn_, _, _ = ln_bwd(dn_ref[...], yn_ref[...], lgv, lbv)
        dbuf[tc:tc + HALO, :] = jnp.where(i < nstep - 1, dyn_, 0.0)

        @pl.when(i == 0)
        def _():
            dwacc[...] = jnp.zeros_like(dwacc)
            db_ref[...] = jnp.zeros_like(db_ref)
            dlg_ref[...] = jnp.zeros_like(dlg_ref)
            dlb_ref[...] = jnp.zeros_like(dlb_ref)

        db_ref[...] += jnp.sum(dy, axis=0, keepdims=True)
        dlg_ref[...] += jnp.sum(dln * yn, axis=0, keepdims=True)
        dlb_ref[...] += jnp.sum(dln, axis=0, keepdims=True)

        for r0 in range(0, tc, ROW_CHUNK):
            for c0 in range(0, C, lc):
                dcur = dbuf[r0:r0 + ROW_CHUNK, c0:c0 + lc]
                acc = jnp.zeros((ROW_CHUNK, lc), F32)
                for k in range(ntap):
                    s = r0 + 30 - k
                    acc = acc + w_ref[k:k + 1, c0:c0 + lc] * dbuf[s:s + ROW_CHUNK, c0:c0 + lc]
                    sv = r0 + 2 + k
                    prod = dcur * vbuf[sv:sv + ROW_CHUNK, c0:c0 + lc]
                    red = prod[0:8]
                    for q in range(8, ROW_CHUNK, 8):
                        red = red + prod[q:q + 8]
                    dwacc[8 * k:8 * k + 8, c0:c0 + lc] += red
                dvbuf[r0:r0 + ROW_CHUNK, c0:c0 + lc] = acc
        dv = dvbuf[...]
        dz_ref[:, :C] = (dv * sgt).astype(BF16)
        dz_ref[:, C:] = (dv * a * sgt * (1.0 - sgt)).astype(BF16)

        @pl.when(i == nstep - 1)
        def _():
            for k in range(ntap):
                dw_ref[k:k + 1, :] = jnp.sum(dwacc[8 * k:8 * k + 8, :], axis=0, keepdims=True)
            dw_ref[ntap:HALO, :] = jnp.zeros((HALO - ntap, C), F32)

    vec = pl.BlockSpec((1, C), lambda i: (0, 0))
    cur = pl.BlockSpec((tc, C), lambda i: (i, 0))
    nxt = pl.BlockSpec((HALO, C), lambda i: (jnp.minimum((i + 1) * rpb, nhb - 1), 0))
    return pl.pallas_call(
        body, name=name, grid=(nstep,),
        in_specs=[pl.BlockSpec((tc, 2 * C), lambda i: (i, 0)),
                  pl.BlockSpec((HALO, 2 * C), lambda i: (jnp.maximum(i * rpb - 1, 0), 0)),
                  cur, nxt, cur, nxt, pl.BlockSpec((HALO, C), lambda i: (0, 0)), vec, vec],
        out_specs=[pl.BlockSpec((tc, 2 * C), lambda i: (i, 0)), pl.BlockSpec((HALO, C), lambda i: (0, 0)),
                   vec, vec, vec],
        out_shape=[jax.ShapeDtypeStruct((T, 2 * C), BF16), jax.ShapeDtypeStruct((HALO, C), F32),
                   jax.ShapeDtypeStruct((1, C), F32), jax.ShapeDtypeStruct((1, C), F32),
                   jax.ShapeDtypeStruct((1, C), F32)],
        scratch_shapes=[pltpu.VMEM((tc + HALO, C), F32), pltpu.VMEM((tc + HALO, C), F32),
                        pltpu.VMEM((tc, C), F32), pltpu.VMEM((8 * HALO, C), F32)],
        compiler_params=_params(1),
    )(z, z, ycv, ycv, dycat, dycat, w32, lg, lb)


def _seg_sum(u, bmat):
    hi = u.astype(BF16)
    lo = (u - hi.astype(F32)).astype(BF16)
    return _nn(hi, bmat) + _nn(lo, bmat)


def _attn_prep(name, z, gq, gk, bmat, A, c0, hd):
    T = z.shape[0]
    tm = _pick(T, (256, 128))

    def body(zq_ref, zk_ref, zv_ref, gq_ref, gk_ref, b_ref, o_ref):
        bm = b_ref[...]
        for idx, (z_ref, g_ref) in enumerate(((zq_ref, gq_ref), (zk_ref, gk_ref))):
            zv = z_ref[...]
            r = lax.rsqrt(_seg_sum(zv * zv, bm) * (1.0 / hd) + EPS)
            o_ref[:, idx * A:(idx + 1) * A] = zv * r * g_ref[...]
        o_ref[:, 2 * A:] = zv_ref[...]

    vec = pl.BlockSpec((1, A), lambda i: (0, 0))
    return pl.pallas_call(
        body, name=name, grid=(T // tm,),
        in_specs=[pl.BlockSpec((tm, A), lambda i: (i, c0)), pl.BlockSpec((tm, A), lambda i: (i, c0 + 1)),
                  pl.BlockSpec((tm, A), lambda i: (i, c0 + 2)), vec, vec,
                  pl.BlockSpec((A, A), lambda i: (0, 0))],
        out_specs=pl.BlockSpec((tm, 3 * A), lambda i: (i, 0)),
        out_shape=jax.ShapeDtypeStruct((T, 3 * A), F32), compiler_params=_params(1),
    )(z, z, z, gq, gk, bmat)


QK_SCALE = 0.125
ATTN_UNROLL = 4


def _fill_bias(bias, sl_ref, hp, d):
    qi = lax.broadcasted_iota(jnp.int32, (WINDOW, 2 * WINDOW), 0)
    kj = lax.broadcasted_iota(jnp.int32, (WINDOW, 2 * WINDOW), 1)
    dist = WINDOW + qi - kj
    inband = (dist >= 0) & (dist <= WINDOW)
    distf = dist.astype(F32)
    for hh in range(2):
        b = jnp.where(inband, -(sl_ref[2 * hp + hh] * d) * distf, NEG)
        bias[2 * hh + 1] = b
        bias[2 * hh] = jnp.where(kj >= WINDOW, b, NEG)


CHUNK = WINDOW * DILATIONS[-1]


def _deinterleave(dst, src, d, rows, dst_pitch, dst_off, src_off):
    for r in range(d):
        if d == 1:
            val = src[src_off:src_off + rows, :]
        else:
            val = src[pl.ds(src_off + r, rows, stride=d), :]
        lo = r * dst_pitch + dst_off
        dst[lo:lo + rows, :] = val.astype(dst.dtype)


def _interleave_add(dst, start, src, d, rows, src_pitch, src_off):
    for r in range(d):
        lo = r * src_pitch + src_off
        idx = pl.ds(start, rows) if d == 1 else pl.ds(start + r, rows, stride=d)
        dst[idx, :] += src[lo:lo + rows, :]


def _attn_fwd(name, qkv, slopes, A):
    T = qkv.shape[0]
    hpn, nch, nblk = A // LANES, T // CHUNK, CHUNK // WINDOW
    nbranch = len(DILATIONS)

    def body(*refs):
        sl_ref, q_ref, k_ref, kp_ref, v_ref, vp_ref, y_ref, lg_ref, qd, kd, vd, od, ld, bias = refs[:14]
        onat, lnat = refs[14:14 + nbranch], refs[14 + nbranch:]
        hp, ch = pl.program_id(0), pl.program_id(1)
        lane = lax.broadcasted_iota(jnp.int32, (1, LANES), 1)
        first = lane < (LANES // 2)
        for bi, d in enumerate(DILATIONS):
            Ld = CHUNK // d
            seg = Ld + WINDOW
            nbr = Ld // WINDOW
            _deinterleave(qd, q_ref, d, Ld, Ld, 0, 0)
            for dst, cur, prev in ((kd, k_ref, kp_ref), (vd, v_ref, vp_ref)):
                _deinterleave(dst, prev, d, WINDOW, seg, 0, CHUNK - WINDOW * d)
                _deinterleave(dst, cur, d, Ld, seg, WINDOW, 0)
            _fill_bias(bias, sl_ref, hp, d)
            ob, lb = (onat[bi], lnat[bi]) if d == 1 else (od, ld)

            def step(it, carry, Ld=Ld, seg=seg, nbr=nbr, ob=ob, lb=lb):
                r, nl = it // nbr, it % nbr
                q0 = pl.multiple_of(r * Ld + nl * WINDOW, WINDOW)
                k0 = pl.multiple_of(r * seg + nl * WINDOW, WINDOW)
                later = jnp.where(ch * nbr + nl > 0, 1, 0)
                qb = qd[pl.ds(q0, WINDOW), :]
                k2 = kd[pl.ds(k0, 2 * WINDOW), :]
                v2 = vd[pl.ds(k0, 2 * WINDOW), :]
                res = []
                for hh in range(2):
                    mh = first if hh == 0 else jnp.logical_not(first)
                    s = _nt(jnp.where(mh, qb, jnp.zeros_like(qb)), k2) + bias[2 * hh + later]
                    mx = jnp.max(s, axis=-1, keepdims=True)
                    p = jnp.exp(s - mx)
                    den = jnp.sum(p, axis=-1, keepdims=True)
                    res.append((_nn(p.astype(BF16), v2) / den, mx + jnp.log(den)))
                ob[pl.ds(q0, WINDOW), :] = jnp.where(first, res[0][0], res[1][0])
                lb[pl.ds(q0, WINDOW), :] = jnp.where(first, res[0][1], res[1][1])
                return carry

            lax.fori_loop(0, nblk, step, 0, unroll=ATTN_UNROLL)
            if d > 1:
                for r in range(d):
                    onat[bi][pl.ds(r, Ld, stride=d), :] = od[r * Ld:(r + 1) * Ld, :]
                    lnat[bi][pl.ds(r, Ld, stride=d), :] = ld[r * Ld:(r + 1) * Ld, :]
        ls = [l[...] for l in lnat]
        mx = ls[0]
        for v in ls[1:]:
            mx = jnp.maximum(mx, v)
        es = [jnp.exp(v - mx) for v in ls]
        den = es[0]
        for e in es[1:]:
            den = den + e
        out = es[0] * onat[0][...]
        for e, o in zip(es[1:], onat[1:]):
            out = out + e * o[...]
        y_ref[...] = (out / den).astype(BF16)
        lg_ref[...] = mx + jnp.log(den)

    blk = lambda m: pl.BlockSpec((CHUNK, LANES), m)
    cur = lambda which: blk(lambda hp, ch: (ch, which * hpn + hp))
    prev = lambda which: blk(lambda hp, ch: (jnp.maximum(ch - 1, 0), which * hpn + hp))
    omap = blk(lambda hp, ch: (ch, hp))
    f32buf = pltpu.VMEM((CHUNK, LANES), F32)
    return pl.pallas_call(
        body, name=name, grid=(hpn, nch),
        in_specs=[pl.BlockSpec(memory_space=pltpu.SMEM), cur(0), cur(1), prev(1), cur(2), prev(2)],
        out_specs=[omap, omap],
        out_shape=[jax.ShapeDtypeStruct((T, A), BF16), jax.ShapeDtypeStruct((T, A), F32)],
        scratch_shapes=[pltpu.VMEM((CHUNK, LANES), BF16), pltpu.VMEM((2 * CHUNK, LANES), BF16),
                        pltpu.VMEM((2 * CHUNK, LANES), BF16), f32buf, f32buf,
                        pltpu.VMEM((4, WINDOW, 2 * WINDOW), F32)] + [f32buf] * (2 * nbranch),
        compiler_params=_params(2),
    )(slopes, qkv, qkv, qkv, qkv, qkv)


def _attn_bwd(name, qkv, dycat, yatt, lg, slopes, A, catoff):
    T = qkv.shape[0]
    hpn, nch, nblk = A // LANES, T // CHUNK, CHUNK // WINDOW
    co = catoff // LANES

    def body(sl_ref, q_ref, k_ref, kp_ref, v_ref, vp_ref, do_ref, o_ref, l_ref, dq_ref, dk_ref, dv_ref,
             qd, kd, vd, dod, ddn, ddd, ldd, dqd, dkd, dvd, bias):
        hp, ch = pl.program_id(0), pl.program_id(1)
        lane = lax.broadcasted_iota(jnp.int32, (1, LANES), 1)
        first = lane < (LANES // 2)
        ddn[...] = do_ref[...] * o_ref[...].astype(F32)
        dq_ref[...] = jnp.zeros_like(dq_ref)

        @pl.when(ch == 0)
        def _():
            dk_ref[...] = jnp.zeros_like(dk_ref)
            dv_ref[...] = jnp.zeros_like(dv_ref)

        base = ch * CHUNK
        for d in DILATIONS:
            Ld = CHUNK // d
            seg = Ld + WINDOW
            nbr = Ld // WINDOW
            _deinterleave(qd, q_ref, d, Ld, Ld, 0, 0)
            _deinterleave(dod, do_ref, d, Ld, Ld, 0, 0)
            _deinterleave(ddd, ddn, d, Ld, Ld, 0, 0)
            _deinterleave(ldd, l_ref, d, Ld, Ld, 0, 0)
            for dst, cur, prev in ((kd, k_ref, kp_ref), (vd, v_ref, vp_ref)):
                _deinterleave(dst, prev, d, WINDOW, seg, 0, CHUNK - WINDOW * d)
                _deinterleave(dst, cur, d, Ld, seg, WINDOW, 0)
            dkd[0:d * seg, :] = jnp.zeros((d * seg, LANES), F32)
            dvd[0:d * seg, :] = jnp.zeros((d * seg, LANES), F32)
            _fill_bias(bias, sl_ref, hp, d)

            def step(it, carry, Ld=Ld, seg=seg, nbr=nbr):
                r, nl = it // nbr, it % nbr
                q0 = pl.multiple_of(r * Ld + nl * WINDOW, WINDOW)
                k0 = pl.multiple_of(r * seg + nl * WINDOW, WINDOW)
                later = jnp.where(ch * nbr + nl > 0, 1, 0)
                qb = qd[pl.ds(q0, WINDOW), :]
                k2 = kd[pl.ds(k0, 2 * WINDOW), :]
                v2 = vd[pl.ds(k0, 2 * WINDOW), :]
                dob = dod[pl.ds(q0, WINDOW), :]
                dd = ddd[pl.ds(q0, WINDOW), :]
                lb = ldd[pl.ds(q0, WINDOW), :]
                dk2 = jnp.zeros((2 * WINDOW, LANES), F32)
                dv2 = jnp.zeros((2 * WINDOW, LANES), F32)
                dqs = []
                for hh in range(2):
                    mh = first if hh == 0 else jnp.logical_not(first)
                    qh = jnp.where(mh, qb, jnp.zeros_like(qb))
                    doh = jnp.where(mh, dob, jnp.zeros_like(dob))
                    lcol = lb[:, hh * (LANES // 2):hh * (LANES // 2) + 1]
                    p = jnp.exp(_nt(qh, k2) + bias[2 * hh + later] - lcol)
                    dcol = jnp.sum(jnp.where(mh, dd, 0.0), axis=-1, keepdims=True)
                    ds = (p * (_nt(doh, v2) - dcol)).astype(BF16)
                    dqs.append(_nn(ds, k2))
                    dk2 = dk2 + _tn(ds, qh)
                    dv2 = dv2 + _tn(p.astype(BF16), doh)
                dqd[pl.ds(q0, WINDOW), :] = jnp.where(first, dqs[0], dqs[1])
                dkd[pl.ds(k0, 2 * WINDOW), :] += dk2
                dvd[pl.ds(k0, 2 * WINDOW), :] += dv2
                return carry

            lax.fori_loop(0, nblk, step, 0, unroll=ATTN_UNROLL)
            _interleave_add(dq_ref, 0, dqd, d, Ld, Ld, 0)
            for acc, out in ((dkd, dk_ref), (dvd, dv_ref)):
                _interleave_add(out, base, acc, d, Ld, seg, WINDOW)

                @pl.when(ch > 0)
                def _(acc=acc, out=out, d=d, seg=seg):
                    _interleave_add(out, base - WINDOW * d, acc, d, WINDOW, seg, 0)

    blk = lambda m: pl.BlockSpec((CHUNK, LANES), m)
    cur = lambda which: blk(lambda hp, ch: (ch, which * hpn + hp))
    prev = lambda which: blk(lambda hp, ch: (jnp.maximum(ch - 1, 0), which * hpn + hp))
    omap = blk(lambda hp, ch: (ch, hp))
    full = pl.BlockSpec((T, LANES), lambda hp, ch: (0, hp))
    f32buf = pltpu.VMEM((CHUNK, LANES), F32)
    bf16buf = pltpu.VMEM((CHUNK, LANES), BF16)
    return pl.pallas_call(
        body, name=name, grid=(hpn, nch),
        in_specs=[pl.BlockSpec(memory_space=pltpu.SMEM), cur(0), cur(1), prev(1), cur(2), prev(2),
                  blk(lambda hp, ch: (ch, co + hp)), omap, omap],
        out_specs=[omap, full, full],
        out_shape=[jax.ShapeDtypeStruct((T, A), F32)] * 3,
        scratch_shapes=[bf16buf, pltpu.VMEM((2 * CHUNK, LANES), BF16), pltpu.VMEM((2 * CHUNK, LANES), BF16),
                        bf16buf, f32buf, f32buf, f32buf, f32buf,
                        pltpu.VMEM((2 * CHUNK, LANES), F32), pltpu.VMEM((2 * CHUNK, LANES), F32),
                        pltpu.VMEM((4, WINDOW, 2 * WINDOW), F32)],
        compiler_params=_params(2),
    )(slopes, qkv, qkv, qkv, qkv, qkv, dycat, yatt, lg)


def _attn_bwd_combine(name, dqs, dks, dvs, z, gq, gk, bmat, fmat, A, c0, hd):
    T = z.shape[0]
    tm = _pick(T, (256, 128))
    nbr = len(dqs)

    def body(*refs):
        dq_refs, dk_refs, dv_refs = refs[:nbr], refs[nbr:2 * nbr], refs[2 * nbr:3 * nbr]
        zq_ref, zk_ref, gq_ref, gk_ref, b_ref, f_ref, dz_ref, dgq_ref, dgk_ref = refs[3 * nbr:]
        i = pl.program_id(0)
        bm = b_ref[...]

        def tot(rs):
            t = rs[0][...]
            for r in rs[1:]:
                t = t + r[...]
            return t

        for idx, (d_refs, z_ref, g_ref, dg_ref, gscale) in enumerate(
                ((dq_refs, zq_ref, gq_ref, dgq_ref, QK_SCALE), (dk_refs, zk_ref, gk_ref, dgk_ref, 1.0))):
            dy = tot(d_refs)
            zv = z_ref[...]
            r = lax.rsqrt(_seg_sum(zv * zv, bm) * (1.0 / hd) + EPS)
            gd = dy * g_ref[...]
            mean = _seg_sum(gd * zv, bm) * (1.0 / hd)
            dz_ref[:, idx * A:(idx + 1) * A] = (r * gd - zv * (r * r * r) * mean).astype(BF16)
            part = jnp.sum(dy * zv * r, axis=0, keepdims=True) * gscale

            @pl.when(i == 0)
            def _():
                dg_ref[...] = part

            @pl.when(i > 0)
            def _():
                dg_ref[...] += part

        dz_ref[:, 2 * A:] = tot(dv_refs).astype(BF16)

        @pl.when(i == T // tm - 1)
        def _():
            fm = f_ref[...]
            for dg_ref in (dgq_ref, dgk_ref):
                v = jnp.broadcast_to(dg_ref[...], (8, A))
                hi = v.astype(BF16)
                mid = (v - hi.astype(F32)).astype(BF16)
                lo = (v - hi.astype(F32) - mid.astype(F32)).astype(BF16)
                dg_ref[...] = (_nn(hi, fm) + _nn(mid, fm) + _nn(lo, fm))[0:1]

    blk = pl.BlockSpec((tm, A), lambda i: (i, 0))
    vec = pl.BlockSpec((1, A), lambda i: (0, 0))
    return pl.pallas_call(
        body, name=name, grid=(T // tm,),
        in_specs=[blk] * (3 * nbr) + [pl.BlockSpec((tm, A), lambda i: (i, c0)),
                                      pl.BlockSpec((tm, A), lambda i: (i, c0 + 1)), vec, vec,
                                      pl.BlockSpec((A, A), lambda i: (0, 0)),
                                      pl.BlockSpec((A, A), lambda i: (0, 0))],
        out_specs=[pl.BlockSpec((tm, 3 * A), lambda i: (i, 0)), vec, vec],
        out_shape=[jax.ShapeDtypeStruct((T, 3 * A), BF16), jax.ShapeDtypeStruct((1, A), F32),
                   jax.ShapeDtypeStruct((1, A), F32)],
        compiler_params=_params(1),
    )(*dqs, *dks, *dvs, z, z, gq, gk, bmat, fmat)


def _local_step(x, tgt, S, comm, hd):
    T, D = x.shape
    C = S["conv_b_dw"].shape[1]
    A = C
    H = A // hd
    Dmix = C + A
    c0 = (2 * C) // A
    slopes = 2.0 ** (-ALIBI_MAX_BIAS * jnp.arange(1, H + 1, dtype=F32) / H)
    seg = jnp.arange(A) // hd
    bmat = (seg[:, None] == seg[None, :]).astype(BF16)
    pos_in_head = jnp.arange(A) % hd
    fmat = (pos_in_head[:, None] == pos_in_head[None, :]).astype(BF16)
    gq = jnp.tile(S["q_norm_g"], (1, H)) * QK_SCALE
    gk = jnp.tile(S["k_norm_g"], (1, H))

    wg1, wu1 = comm.weights(("ffn1_w_gate", "ffn1_w_up"), None)
    h1, gate1, up1, a1 = _norm_matmul("ffn1_up", x, comm.tie(S["ffn1_norm_g"]), [wg1, wu1], True)
    wd1, win, w32 = comm.weights(("ffn1_w_down", "w_in", "conv_w32"), a1)
    x1 = _matmul_res("ffn1_down", a1, wd1, x, 0.5)
    h2, z = _norm_matmul("mix_in", x1, S["mix_norm_g"], [win], False)
    yc, ycv = _conv_fwd("conv_fwd", z, w32, S["conv_b_dw"], S["conv_ln_g"], S["conv_ln_b"], C)
    qkv = _attn_prep("attn_prep", z, gq, gk, bmat, A, c0, hd)
    yatt, lg = _attn_fwd("attn_fwd", qkv, slopes, A)
    ycat = jnp.concatenate([yc, yatt], axis=1)
    wout, wg2, wu2, wd2 = comm.weights(("w_out", "ffn2_w_gate", "ffn2_w_up", "ffn2_w_down"), yatt)
    x2 = _matmul_res("mix_out", ycat, wout, x1, 1.0)
    h3, gate2, up2, a2 = _norm_matmul("ffn2_up", x2, S["ffn2_norm_g"], [wg2, wu2], True)
    dx3, dx3b, lossvec = _matmul_res("ffn2_down_loss", a2, wd2, x2, 0.5, tgt=tgt)

    G = {}
    dgate2, dup2 = _nt_matmul("ffn2_dact", dx3b, wd2, 0.5, gate2, up2)
    comm.reduce_begin("ffn2", {"ffn2_w_down": _tn_matmul("ffn2_dwd", a2, dx3b, 0.5),
                               "ffn2_w_gate": _tn_matmul("ffn2_dwg", h3, dgate2),
                               "ffn2_w_up": _tn_matmul("ffn2_dwu", h3, dup2)})
    dx2, dx2b, G["ffn2_norm_g"] = _nt_rms_bwd("ffn2_dx", [dgate2, dup2], [wg2, wu2],
                                              x2, comm.tie(S["ffn2_norm_g"]), dx3)
    dwout = _tn_matmul("mix_dwout", ycat, dx2b)
    dycat, _ = _nt_matmul("mix_dycat", dx2b, wout)
    dzc, G["conv_w32"], G["conv_b_dw"], G["conv_ln_g"], G["conv_ln_b"] = _conv_bwd(
        "conv_bwd", z, ycv, dycat, w32, S["conv_ln_g"], S["conv_ln_b"], C)
    dq, dk, dv = _attn_bwd("attn_bwd", qkv, dycat, yatt, lg, slopes, A, C)
    dzqkv, G["q_norm_g"], G["k_norm_g"] = _attn_bwd_combine(
        "attn_bwd_combine", [dq], [dk], [dv], z, gq, gk, bmat, fmat, A, c0, hd)
    comm.reduce_end("ffn2", dzqkv)
    dz = jnp.concatenate([dzc, dzqkv], axis=1)
    comm.reduce_begin("mix", {"w_out": dwout, "w_in": _tn_matmul("mix_dwin", h2, dz)})
    dx1, dx1b, G["mix_norm_g"] = _nt_rms_bwd("mix_dx", [dz], [win], x1, comm.tie(S["mix_norm_g"]), dx2)
    dgate1, dup1 = _nt_matmul("ffn1_dact", dx1b, wd1, 0.5, gate1, up1)
    dwd1 = _tn_matmul("ffn1_dwd", a1, dx1b, 0.5)
    dwg1 = _tn_matmul("ffn1_dwg", h1, dgate1)
    dwu1 = _tn_matmul("ffn1_dwu", h1, dup1)
    comm.reduce_end("mix", dwu1)
    comm.reduce_begin("ffn1", {"ffn1_w_down": dwd1, "ffn1_w_gate": dwg1, "ffn1_w_up": dwu1})
    dx0, _, G["ffn1_norm_g"] = _nt_rms_bwd("ffn1_dx", [dgate1, dup1], [wg1, wu1],
                                           x, comm.tie(S["ffn1_norm_g"]), dx1)
    comm.reduce_end("ffn1", dx0)
    return lossvec, dx0, G


BIG = (("ffn1_w_gate", 1), ("ffn1_w_up", 1), ("ffn1_w_down", 0), ("w_in", 1), ("w_out", 0),
       ("ffn2_w_gate", 1), ("ffn2_w_up", 1), ("ffn2_w_down", 0))
AXIS = dict(BIG)
FLIPS = ((1, 0), (0, 1), (1, 1))
HBM = pl.BlockSpec(memory_space=pltpu.HBM)
SEM = pl.BlockSpec(memory_space=pltpu.SEMAPHORE)
EFFECT = pltpu.SideEffectType.DATAFLOW_SIDE_EFFECTING
TOKEN = jax.ShapeDtypeStruct((8, LANES), F32)


def _window(ref, shape, axis, slab=None, half=None):
    idx = [pl.ds(0, shape[0]), pl.ds(0, shape[1])]
    if slab is not None:
        n = shape[axis] // 4
        idx[axis] = pl.ds(pl.multiple_of(slab * n, 8), n)
    if half is not None:
        hs = shape[1 - axis] // 2
        idx[1 - axis] = pl.ds(pl.multiple_of(half * hs, 8), hs)
    return ref.at[idx[0], idx[1]]


def _position():
    return lax.axis_index("x"), lax.axis_index("y"), lax.axis_index("c")


def _half_shape(shape, axis):
    return (shape[0] // 2, shape[1]) if axis == 1 else (shape[0], shape[1] // 2)


def _slab_shape(shape, axis):
    return (shape[0], shape[1] // 4) if axis == 1 else (shape[0] // 4, shape[1])


def _piece_shape(shape, axis):
    return _half_shape(_slab_shape(shape, axis), axis)


def _full_shape(shard, axis):
    return (shard.shape[0], shard.shape[1] * 4) if axis == 1 else (shard.shape[0] * 4, shard.shape[1])


def _hbm(a):
    return pltpu.with_memory_space_constraint(a, pltpu.HBM)


def _remote(src, dst, send_sem, recv_sem, to):
    return pltpu.make_async_remote_copy(src_ref=src, dst_ref=dst, send_sem=send_sem, recv_sem=recv_sem,
                                        device_id=to, device_id_type=MESH)


def _place(name, pos, w, axis):
    R, Cc = w.shape
    tr = _pick(R, (256, 128, 64, 32, 16))
    nrb = R // tr

    def body(pos_ref, w_ref, o_ref):
        o_ref[...] = w_ref[...].astype(BF16)

    omap = (lambda i, p: (i, p[0])) if axis == 1 else (lambda i, p: (p[0] * nrb + i, 0))
    return pl.pallas_call(
        body, name=name,
        grid_spec=pltpu.PrefetchScalarGridSpec(
            num_scalar_prefetch=1, grid=(nrb,), in_specs=[pl.BlockSpec((tr, Cc), lambda i, p: (i, 0))],
            out_specs=pl.BlockSpec((tr, Cc), omap)),
        out_shape=jax.ShapeDtypeStruct(_full_shape(w, axis), BF16), compiler_params=_params(1),
    )(pos, w)


def _gather_now(name, axes, fulls):
    nt = len(fulls)
    shapes = [f.shape for f in fulls]

    def body(*refs):
        outs, token = refs[nt:2 * nt], refs[2 * nt]
        send_sems, recv_sems = refs[2 * nt + 1:]
        x, y, c = _position()
        j0 = 2 * x + y
        sib = (x, y, 1 - c)

        def copy(t, k, slab, half, to):
            win = _window(outs[t], shapes[t], axes[t], slab=slab, half=half)
            return _remote(win, win, send_sems.at[t, k], recv_sems.at[t, k], to)

        sends = []
        for k, (fx, fy) in enumerate(FLIPS):
            for t in range(nt):
                cp = copy(t, k, j0, c, (x ^ fx, y ^ fy, c))
                cp.start()
                sends.append(cp)
        for k, (fx, fy) in enumerate(FLIPS):
            js = 2 * (x ^ fx) + (y ^ fy)
            for t in range(nt):
                copy(t, k, js, c, sib).wait_recv()
                cp = copy(t, 3 + k, js, c, sib)
                cp.start()
                sends.append(cp)
        for k, (fx, fy) in enumerate(FLIPS):
            js = 2 * (x ^ fx) + (y ^ fy)
            for t in range(nt):
                copy(t, 3 + k, js, 1 - c, sib).wait_recv()
        for cp in sends:
            cp.wait_send()
        token[...] = jnp.zeros_like(token)

    res = pl.pallas_call(
        body, name=name, in_specs=[ANY] * nt,
        out_specs=[ANY] * nt + [pl.BlockSpec(memory_space=pltpu.VMEM)],
        out_shape=[jax.ShapeDtypeStruct(s, BF16) for s in shapes] + [TOKEN],
        input_output_aliases={t: t for t in range(nt)},
        scratch_shapes=[pltpu.SemaphoreType.DMA((nt, 6)), pltpu.SemaphoreType.DMA((nt, 6))],
    )(*fulls)
    return list(res[:nt]), res[nt]


def _split_start(name, arrays, ncopies, plan):
    na = len(arrays)

    def body(*refs):
        ins = refs[:na]
        send_sems, recv_sems = refs[na], refs[na + 1]
        token = refs[-1]
        x, y, c = _position()
        for i, (src, dst, to) in enumerate(plan(ins, x, y, c)):
            _remote(src, dst, send_sems.at[i], recv_sems.at[i], to).start()
        token[...] = jnp.zeros_like(token)

    res = pl.pallas_call(
        body, name=name, in_specs=[HBM] * na,
        out_specs=tuple([SEM, SEM] + [HBM] * na + [pl.BlockSpec(memory_space=pltpu.VMEM)]),
        out_shape=tuple([pltpu.SemaphoreType.DMA((ncopies,)), pltpu.SemaphoreType.DMA((ncopies,))]
                        + [pltpu.HBM(a.shape, a.dtype) for a in arrays] + [TOKEN]),
        input_output_aliases={i: 2 + i for i in range(na)},
        compiler_params=pltpu.CompilerParams(has_side_effects=EFFECT),
    )(*[_hbm(a) for a in arrays])
    return (res[0], res[1]), list(res[2:2 + na]), res[-1]


def _split_wait(name, arrays, sems, after, plan):
    na = len(arrays)

    def body(*refs):
        ins = refs[:na]
        send_sems, recv_sems = refs[na], refs[na + 1]
        x, y, c = _position()
        for i, (src, dst, to) in enumerate(plan(ins, x, y, c)):
            cp = _remote(src, dst, send_sems.at[i], recv_sems.at[i], to)
            cp.wait_send()
            cp.wait_recv()

    res = pl.pallas_call(
        body, name=name, in_specs=[HBM] * na + [SEM, SEM, ANY],
        out_specs=tuple([HBM] * na), out_shape=tuple(pltpu.HBM(a.shape, a.dtype) for a in arrays),
        input_output_aliases={i: i for i in range(na)},
        compiler_params=pltpu.CompilerParams(has_side_effects=EFFECT),
    )(*arrays, *sems, after)
    return list(res)


def _gather_plan(axes, shapes, conv_shape):
    nt = len(axes)

    def plan(refs, x, y, c):
        j0 = 2 * x + y
        out = []
        for fx, fy in FLIPS:
            to = (x ^ fx, y ^ fy, c)
            for t in range(nt):
                win = _window(refs[t], shapes[t], axes[t], slab=j0, half=c)
                out.append((win, win, to))
            if conv_shape is not None:
                win = _window(refs[nt], conv_shape, 1, slab=j0)
                out.append((win, win, to))
        return out

    return plan


def _gather_finish(name, axes, fulls):
    nt = len(axes)
    shapes = [f.shape for f in fulls]

    def body(*refs):
        outs = refs[nt:2 * nt]
        send_sems, recv_sems = refs[2 * nt:]
        x, y, c = _position()
        sib = (x, y, 1 - c)
        cps = []
        for k, (fx, fy) in enumerate(FLIPS):
            js = 2 * (x ^ fx) + (y ^ fy)
            for t in range(nt):
                landed = _window(outs[t], shapes[t], axes[t], slab=js, half=c)
                cp = _remote(landed, landed, send_sems.at[t, k], recv_sems.at[t, k], sib)
                cp.start()
                cps.append(cp)
        for k, (fx, fy) in enumerate(FLIPS):
            js = 2 * (x ^ fx) + (y ^ fy)
            for t in range(nt):
                other = _window(outs[t], shapes[t], axes[t], slab=js, half=1 - c)
                _remote(other, other, send_sems.at[t, k], recv_sems.at[t, k], sib).wait_recv()
        for cp in cps:
            cp.wait_send()

    res = pl.pallas_call(
        body, name=name, in_specs=[ANY] * nt, out_specs=[ANY] * nt,
        out_shape=[jax.ShapeDtypeStruct(f.shape, f.dtype) for f in fulls],
        input_output_aliases={t: t for t in range(nt)},
        scratch_shapes=[pltpu.SemaphoreType.DMA((nt, 3)), pltpu.SemaphoreType.DMA((nt, 3))],
    )(*fulls)
    return list(res)


def _pair_exchange(name, srcs, windows, out_shapes, dtype):
    nt = len(srcs)

    def body(*refs):
        ins, outs = refs[:nt], refs[nt:2 * nt]
        send_sems, recv_sems = refs[2 * nt:]
        x, y, c = _position()
        cps = []
        for t in range(nt):
            cp = _remote(windows[t](ins[t], c), outs[t], send_sems.at[t], recv_sems.at[t], (x, y, 1 - c))
            cp.start()
            cps.append(cp)
        for cp in cps:
            cp.wait()

    return pl.pallas_call(
        body, name=name, in_specs=[ANY] * nt, out_specs=[ANY] * nt,
        out_shape=[jax.ShapeDtypeStruct(s, dtype) for s in out_shapes],
        scratch_shapes=[pltpu.SemaphoreType.DMA((nt,)), pltpu.SemaphoreType.DMA((nt,))],
    )(*srcs)


def _scatter_plan(axes, shapes):
    nt = len(axes)

    def plan(refs, x, y, c):
        out = []
        for k, (fx, fy) in enumerate(FLIPS):
            js = 2 * (x ^ fx) + (y ^ fy)
            for t in range(nt):
                src = _window(refs[t], _half_shape(shapes[t], axes[t]), axes[t], slab=js)
                out.append((src, refs[nt + t].at[k], (x ^ fx, y ^ fy, c)))
        return out

    return plan


def _gather_small(packed):
    R, Cc = packed.shape

    def body(p_ref, o_ref, send_sems, recv_sems, loc_sem):
        x, y, c = _position()
        me = 4 * x + 2 * y + c
        mine = pltpu.make_async_copy(p_ref, o_ref.at[me], loc_sem)
        mine.start()
        cps = []
        for k in range(1, 8):
            fx, fy, fc = (k >> 2) & 1, (k >> 1) & 1, k & 1
            cp = pltpu.make_async_remote_copy(
                src_ref=p_ref, dst_ref=o_ref.at[me], send_sem=send_sems.at[k - 1], recv_sem=recv_sems.at[k - 1],
                device_id=(x ^ fx, y ^ fy, c ^ fc), device_id_type=MESH)
            cp.start()
            cps.append(cp)
        for cp in cps:
            cp.wait()
        mine.wait()

    return pl.pallas_call(
        body, name="gather_small_grads", in_specs=[ANY], out_specs=ANY,
        out_shape=jax.ShapeDtypeStruct((8, R, Cc), F32),
        scratch_shapes=[pltpu.SemaphoreType.DMA((7,)), pltpu.SemaphoreType.DMA((7,)), pltpu.SemaphoreType.DMA],
    )(packed)


def _sum_slots(name, slots):
    n, R, Cc = slots.shape

    def body(s_ref, o_ref):
        t = s_ref[0]
        for i in range(1, n):
            t = t + s_ref[i]
        o_ref[...] = t

    return pl.pallas_call(
        body, name=name, grid=(1,), in_specs=[pl.BlockSpec((n, R, Cc), lambda i: (0, 0, 0))],
        out_specs=pl.BlockSpec((R, Cc), lambda i: (0, 0)), out_shape=jax.ShapeDtypeStruct((R, Cc), F32),
        compiler_params=_params(1),
    )(slots)


def _pair_sum(name, pos, g, land, shape, axis):
    hshape = _half_shape(shape, axis)
    R, Cc = hshape
    tr = _pick(R, (256, 128, 64, 32, 16))
    nrb = R // tr

    def body(pos_ref, g_ref, l_ref, o_ref):
        o_ref[...] = (g_ref[...].astype(F32) + l_ref[...].astype(F32)).astype(BF16)

    if axis == 1:
        gmap = lambda i, p: (p[1] * nrb + i, 0)
    else:
        gmap = lambda i, p: (i, p[1])
    blk = pl.BlockSpec((tr, Cc), lambda i, p: (i, 0))
    return pl.pallas_call(
        body, name=name,
        grid_spec=pltpu.PrefetchScalarGridSpec(
            num_scalar_prefetch=1, grid=(nrb,), in_specs=[pl.BlockSpec((tr, Cc), gmap), blk], out_specs=blk),
        out_shape=jax.ShapeDtypeStruct(hshape, BF16), compiler_params=_params(1),
    )(pos, g, land)


def _chip_sum(name, pos, sb, land, shape, axis):
    hshape = _half_shape(shape, axis)
    pshape = _piece_shape(shape, axis)
    R, Cc = pshape
    tr = _pick(R, (256, 128, 64, 32, 16))
    nrb = R // tr

    def body(pos_ref, s_ref, l_ref, o_ref):
        t = s_ref[...].astype(F32)
        for k in range(3):
            t = t + l_ref[k].astype(F32)
        o_ref[...] = t

    if axis == 1:
        smap = lambda i, p: (i, p[0])
    else:
        smap = lambda i, p: (p[0] * nrb + i, 0)
    return pl.pallas_call(
        body, name=name,
        grid_spec=pltpu.PrefetchScalarGridSpec(
            num_scalar_prefetch=1, grid=(nrb,),
            in_specs=[pl.BlockSpec((tr, Cc), smap), pl.BlockSpec((3, tr, Cc), lambda i, p: (0, i, 0))],
            out_specs=pl.BlockSpec((tr, Cc), lambda i, p: (i, 0))),
        out_shape=jax.ShapeDtypeStruct(pshape, F32), compiler_params=_params(1),
    )(pos, sb, land)


def _adam_math(w, g, m, v):
    m = ADAM_B1 * m + (1.0 - ADAM_B1) * g
    v = ADAM_B2 * v + (1.0 - ADAM_B2) * (g * g)
    m_hat = m / (1.0 - ADAM_B1 ** ADAM_STEP)
    v_hat = v / (1.0 - ADAM_B2 ** ADAM_STEP)
    delta = -ADAM_LR * (m_hat / (jnp.sqrt(v_hat) + ADAM_EPS) + ADAM_WD * w)
    return delta, m, v


def _adamw_halves(name, pos, w, m, v, mine, theirs, axis):
    R, Cc = w.shape
    hr, hc = mine.shape
    tr = _pick(hr, (256, 128, 64, 32, 16))
    nrb = hr // tr

    def body(pos_ref, w_ref, m_ref, v_ref, a_ref, b_ref, g_ref, d_ref, nm_ref, nv_ref):
        half = pl.program_id(0)
        g = jnp.where(half == pos_ref[1], a_ref[...], b_ref[...])
        d, nm, nv = _adam_math(w_ref[...], g, m_ref[...], v_ref[...])
        g_ref[...] = g
        d_ref[...] = d
        nm_ref[...] = nm
        nv_ref[...] = nv

    if axis == 1:
        wmap = lambda h, i, p: (h * nrb + i, 0)
    else:
        wmap = lambda h, i, p: (i, h)
    wblk = pl.BlockSpec((tr, hc), wmap)
    hblk = pl.BlockSpec((tr, hc), lambda h, i, p: (i, 0))
    return pl.pallas_call(
        body, name=name,
        grid_spec=pltpu.PrefetchScalarGridSpec(
            num_scalar_prefetch=1, grid=(2, nrb), in_specs=[wblk, wblk, wblk, hblk, hblk], out_specs=[wblk] * 4),
        out_shape=[jax.ShapeDtypeStruct((R, Cc), F32)] * 4, compiler_params=_params(2),
    )(pos, w, m, v, mine, theirs)


def _adamw_small(name, w, g, m, v):
    def body(w_ref, g_ref, m_ref, v_ref, d_ref, nm_ref, nv_ref):
        d, nm, nv = _adam_math(w_ref[...], g_ref[...], m_ref[...], v_ref[...])
        d_ref[...] = d
        nm_ref[...] = nm
        nv_ref[...] = nv

    blk = pl.BlockSpec(w.shape, lambda i: (0, 0))
    return pl.pallas_call(
        body, name=name, grid=(1,), in_specs=[blk] * 4, out_specs=[blk] * 3,
        out_shape=[jax.ShapeDtypeStruct(w.shape, F32)] * 3, compiler_params=_params(1),
    )(w, g, m, v)


SMALL = ("ffn1_norm_g", "mix_norm_g", "conv_b_dw", "conv_ln_g", "conv_ln_b", "q_norm_g", "k_norm_g", "ffn2_norm_g")
ORDER = ("ffn1_norm_g", "ffn1_w_gate", "ffn1_w_up", "ffn1_w_down", "mix_norm_g", "w_in", "conv_w_dw", "conv_b_dw",
         "conv_ln_g", "conv_ln_b", "q_norm_g", "k_norm_g", "w_out", "ffn2_norm_g", "ffn2_w_gate", "ffn2_w_up",
         "ffn2_w_down")
GATHER_FIRST = ("ffn1_w_gate", "ffn1_w_up")
GATHER_SECOND = ("ffn1_w_down", "w_in")
GATHER_THIRD = ("w_out", "ffn2_w_gate", "ffn2_w_up", "ffn2_w_down")


class _Exchange:
    def __init__(self, P, Mo, Vo, conv_shard, pos):
        self.P, self.Mo, self.Vo, self.pos = P, Mo, Vo, pos
        self.tokens = []
        self.pending = {}
        self.reducing = {}
        self.results = {}
        placed = {n: _place("place_" + n, pos, P[n][0], a) for n, a in BIG}
        self.shapes = {n: placed[n].shape for n, _ in BIG}
        first, tok = _gather_now("gather_first", [AXIS[n] for n in GATHER_FIRST], [placed[n] for n in GATHER_FIRST])
        self.ready = dict(zip(GATHER_FIRST, first))
        cq = conv_shard.shape[1]
        conv_full = lax.dynamic_update_slice(jnp.zeros((conv_shard.shape[0], 4 * cq), F32), conv_shard,
                                             (0, pos[0] * cq))
        for gname, names, conv in (("second", GATHER_SECOND, conv_full), ("third", GATHER_THIRD, None)):
            axes = [AXIS[n] for n in names]
            shapes = [self.shapes[n] for n in names]
            arrays = [placed[n] for n in names] + ([conv] if conv is not None else [])
            small = min(range(len(arrays)), key=lambda i: arrays[i].size)
            arrays[small] = arrays[small] + tok[0, 0].astype(arrays[small].dtype)
            plan = _gather_plan(axes, shapes, conv.shape if conv is not None else None)
            sems, thru, tok = _split_start("gather_%s_start" % gname, arrays, 3 * len(arrays), plan)
            self.tokens.append(tok)
            for n in names + (("conv_w32",) if conv is not None else ()):
                self.pending[n] = (gname, names, axes, plan, sems, thru, conv is not None)

    def tie(self, v):
        for tok in self.tokens:
            v = v + tok[0:1, 0:1]
        self.tokens = []
        return v

    def weights(self, names, after):
        if names[0] in self.pending:
            gname, gnames, axes, plan, sems, thru, has_conv = self.pending[names[0]]
            thru = _split_wait("gather_%s_wait" % gname, thru, sems, after, plan)
            nt = len(gnames)
            fulls = _gather_finish("gather_%s_finish" % gname, axes, thru[:nt])
            for n, f in zip(gnames, fulls):
                self.ready[n] = f
                del self.pending[n]
            if has_conv:
                self.ready["conv_w32"] = thru[nt]
                del self.pending["conv_w32"]
        return [self.ready[n] for n in names]

    def reduce_begin(self, gname, grads):
        names = list(grads)
        axes = [AXIS[n] for n in names]
        shapes = [self.shapes[n] for n in names]
        gs = [grads[n] for n in names]
        to_sibling = [(lambda ref, c, s=s, a=a: _window(ref, s, a, half=1 - c)) for s, a in zip(shapes, axes)]
        landed = _pair_exchange("pair_exchange_" + gname, gs, to_sibling,
                                [_half_shape(s, a) for s, a in zip(shapes, axes)], BF16)
        sbs = [_pair_sum("pair_sum_" + n, self.pos, g, l, s, a)
               for n, a, g, l, s in zip(names, axes, gs, landed, shapes)]
        lands = [lax.empty((3,) + _piece_shape(s, a), BF16) for s, a in zip(shapes, axes)]
        plan = _scatter_plan(axes, shapes)
        sems, thru, tok = _split_start("scatter_%s_start" % gname, sbs + lands, 3 * len(names), plan)
        self.tokens.append(tok)
        self.reducing[gname] = (names, axes, shapes, plan, sems, thru)

    def reduce_end(self, gname, after):
        names, axes, shapes, plan, sems, thru = self.reducing.pop(gname)
        nt = len(names)
        thru = _split_wait("scatter_%s_wait" % gname, thru, sems, after, plan)
        mine = [_chip_sum("chip_sum_" + n, self.pos, sb, l, s, a)
                for n, a, sb, l, s in zip(names, axes, thru[:nt], thru[nt:], shapes)]
        theirs = _pair_exchange("half_exchange_" + gname, mine, [(lambda ref, c: ref)] * nt,
                                [m.shape for m in mine], F32)
        for n, a, mi, th in zip(names, axes, mine, theirs):
            g, d, nm, nv = _adamw_halves("adamw_" + n, self.pos, self.P[n][0], self.Mo[n][0], self.Vo[n][0],
                                         mi, th, a)
            self.results[n] = (g[None], d[None], nm[None], nv[None])


def kernel(x, ffn1_norm_g, ffn1_w_gate, ffn1_w_up, ffn1_w_down, mix_norm_g, w_in, conv_w_dw, conv_b_dw, conv_ln_g, conv_ln_b, q_norm_g, k_norm_g, w_out, ffn2_norm_g, ffn2_w_gate, ffn2_w_up, ffn2_w_down, loss_target, m_ffn1_norm_g, m_ffn1_w_gate, m_ffn1_w_up, m_ffn1_w_down, m_mix_norm_g, m_w_in, m_conv_w_dw, m_conv_b_dw, m_conv_ln_g, m_conv_ln_b, m_q_norm_g, m_k_norm_g, m_w_out, m_ffn2_norm_g, m_ffn2_w_gate, m_ffn2_w_up, m_ffn2_w_down, v_ffn1_norm_g, v_ffn1_w_gate, v_ffn1_w_up, v_ffn1_w_down, v_mix_norm_g, v_w_in, v_conv_w_dw, v_conv_b_dw, v_conv_ln_g, v_conv_ln_b, v_q_norm_g, v_k_norm_g, v_w_out, v_ffn2_norm_g, v_ffn2_w_gate, v_ffn2_w_up, v_ffn2_w_down):
    args = dict(locals())
    P = {n: args[n] for n in ORDER}
    Mo = {n: args["m_" + n] for n in ORDER}
    Vo = {n: args["v_" + n] for n in ORDER}
    xs = x[0]
    tgt = loss_target[0]
    T, D = xs.shape
    hd = q_norm_g.shape[-1]
    C = conv_b_dw.shape[-1]
    ntap = conv_w_dw.shape[1]
    cx, cy, cc = _position()
    j0 = 2 * cx + cy
    pos = jnp.stack([j0, cc]).astype(jnp.int32)

    conv_shard = jnp.pad(conv_w_dw[0], ((0, HALO - ntap), (0, 0)))
    comm = _Exchange(P, Mo, Vo, conv_shard, pos)
    lossvec, dx0, G = _local_step(xs, tgt, {n: P[n] for n in SMALL}, comm, hd)
    loss = lax.psum(0.5 / D * jnp.sum(lossvec), AXES)
    grads, deltas, new_m, new_v = {}, {}, {}, {}
    for n, _ in BIG:
        grads[n], deltas[n], new_m[n], new_v[n] = comm.results[n]

    rows = [G["conv_w32"]]
    for n in ("ffn1_norm_g", "mix_norm_g", "ffn2_norm_g"):
        rows.append(G[n].reshape(D // C, C))
    for n in ("conv_b_dw", "conv_ln_g", "conv_ln_b", "q_norm_g", "k_norm_g"):
        rows.append(G[n])
    packed = jnp.concatenate(rows, axis=0)
    packed = jnp.pad(packed, ((0, -packed.shape[0] % 8), (0, 0)))
    total = _sum_slots("sum_small_grads", _gather_small(packed))
    r = HALO
    small_g = {}
    cq = C // 4
    small_g["conv_w_dw"] = lax.dynamic_slice(total[:ntap], (0, j0 * cq), (ntap, cq))
    for n in ("ffn1_norm_g", "mix_norm_g", "ffn2_norm_g"):
        small_g[n] = total[r:r + D // C].reshape(1, D)
        r += D // C
    for n in ("conv_b_dw", "conv_ln_g", "conv_ln_b"):
        small_g[n] = total[r:r + 1]
        r += 1
    for n in ("q_norm_g", "k_norm_g"):
        small_g[n] = total[r:r + 1, :hd]
        r += 1
    for n in ("conv_w_dw",) + SMALL:
        lead = n == "conv_w_dw"
        w2, m2, v2 = (P[n][0], Mo[n][0], Vo[n][0]) if lead else (P[n], Mo[n], Vo[n])
        d, nm, nv = _adamw_small("adamw_" + n, w2, small_g[n], m2, v2)
        if lead:
            grads[n], deltas[n], new_m[n], new_v[n] = small_g[n][None], d[None], nm[None], nv[None]
        else:
            grads[n], deltas[n], new_m[n], new_v[n] = small_g[n], d, nm, nv

    return (loss, dx0[None], *[grads[n] for n in ORDER], *[deltas[n] for n in ORDER],
            *[new_m[n] for n in ORDER], *[new_v[n] for n in ORDER])
```

```python
import jax
import jax.numpy as jnp
from jax import lax
from jax.experimental import pallas as pl
from jax.experimental.pallas import tpu as pltpu

F32 = jnp.float32
BF16 = jnp.bfloat16
EPS = 1e-6
WINDOW = 128
DILATIONS = (1, 4, 16)
ALIBI_MAX_BIAS = 8.0
LANES = 128
HALO = 32
ADAM_LR, ADAM_B1, ADAM_B2, ADAM_EPS, ADAM_WD, ADAM_STEP = 0.001, 0.9, 0.999, 1e-08, 0.01, 10
VMEM_LIMIT_MB = 62
ROW_TILE = 1024
TN_ACC_ELEMS = 3 * 1024 * 1024
ONCE_PER_ROW_TILE = pl.Buffered(1)
EPILOGUE_ROWS = 256
ACC_COLS = 512
MESH = pl.DeviceIdType.MESH
ANY = pl.BlockSpec(memory_space=pl.ANY)
AXES = ("x", "y", "c")
NEG = -1e30


def _pick(n, cands):
    for c in cands:
        if n % c == 0:
            return c
    return n


def _params(nsem):
    return pltpu.CompilerParams(dimension_semantics=("arbitrary",) * nsem,
                                vmem_limit_bytes=VMEM_LIMIT_MB << 20)


def _nn(a, b):
    return jnp.dot(a, b, preferred_element_type=F32)


def _nt(a, b):
    return lax.dot_general(a, b, (((1,), (1,)), ((), ())), preferred_element_type=F32)


def _tn(a, b):
    return lax.dot_general(a, b, (((0,), (0,)), ((), ())), preferred_element_type=F32)


def _sigmoid(v):
    return jax.nn.sigmoid(v)


def _rms_r(xv):
    return lax.rsqrt(jnp.mean(xv * xv, axis=-1, keepdims=True) + EPS)


def _norm_matmul(name, x, g, ws, swiglu):
    T, D = x.shape
    N = ws[0].shape[1]
    tm = _pick(T, (ROW_TILE, 512, 256, 128))
    tn = _pick(N, (512, 256, 128))
    nw = len(ws)

    def body(*refs):
        x_ref, g_ref = refs[:2]
        w_refs = refs[2:2 + nw]
        outs = refs[2 + nw:-1]
        hs = refs[-1]

        @pl.when(pl.program_id(1) == 0)
        def _():
            for r0 in range(0, tm, EPILOGUE_ROWS):
                rows = slice(r0, r0 + min(EPILOGUE_ROWS, tm))
                xv = x_ref[rows, :]
                hv = (xv * _rms_r(xv) * g_ref[...]).astype(BF16)
                hs[rows, :] = hv
                outs[0][rows, :] = hv

        h = hs[...]
        if swiglu:
            gt = _nn(h, w_refs[0][...])
            u = _nn(h, w_refs[1][...])
            sg = _sigmoid(gt)
            silu = gt * sg
            outs[1][...] = (u * (sg * (1.0 + gt * (1.0 - sg)))).astype(BF16)
            outs[2][...] = silu.astype(BF16)
            outs[3][...] = (silu * u).astype(BF16)
        else:
            outs[1][...] = _nn(h, w_refs[0][...])

    row = pl.BlockSpec((tm, D), lambda i, j: (i, 0))
    col = pl.BlockSpec((D, tn), lambda i, j: (0, j))
    tile = pl.BlockSpec((tm, tn), lambda i, j: (i, j))
    if swiglu:
        out_shape = [jax.ShapeDtypeStruct((T, D), BF16)] + [jax.ShapeDtypeStruct((T, N), BF16)] * 3
        out_specs = [row, tile, tile, tile]
    else:
        out_shape = [jax.ShapeDtypeStruct((T, D), BF16), jax.ShapeDtypeStruct((T, N), F32)]
        out_specs = [row, tile]
    return pl.pallas_call(
        body, name=name, grid=(T // tm, N // tn),
        in_specs=[row, pl.BlockSpec((1, D), lambda i, j: (0, 0))] + [col] * nw,
        out_specs=out_specs, out_shape=out_shape,
        scratch_shapes=[pltpu.VMEM((tm, D), BF16)],
        compiler_params=_params(2),
    )(x, g, *ws)


def _matmul_res(name, a, w, res, scale, tgt=None):
    T, K = a.shape
    N = w.shape[1]
    loss = tgt is not None
    tm = _pick(T, (512, 256, 128)) if loss else _pick(T, (ROW_TILE, 512, 256, 128))
    tk = _pick(K, (1408, 1024, 512, 256, 128))
    nk = K // tk

    def body(*refs):
        if loss:
            a_ref, w_ref, res_ref, tgt_ref, acc, dxb_ref, lv_ref = refs
            dx_ref = acc
        else:
            a_ref, w_ref, res_ref, acc = refs
            out_ref = acc
        i, k = pl.program_id(0), pl.program_id(1)

        @pl.when(k == 0)
        def _():
            acc[...] = jnp.zeros_like(acc)

        for c0 in range(0, N, ACC_COLS):
            cols = slice(c0, min(c0 + ACC_COLS, N))
            acc[:, cols] += _nn(a_ref[...], w_ref[:, cols])

        @pl.when(k == nk - 1)
        def _():
            part = jnp.zeros((1, N), F32)
            for r0 in range(0, tm, EPILOGUE_ROWS):
                rows = slice(r0, r0 + min(EPILOGUE_ROWS, tm))
                val = res_ref[rows, :] + scale * acc[rows, :]
                if loss:
                    dv = val - tgt_ref[rows, :]
                    dx = dv * (1.0 / N)
                    dx_ref[rows, :] = dx
                    dxb_ref[rows, :] = dx.astype(BF16)
                    part = part + jnp.sum(dv * dv, axis=0, keepdims=True)
                else:
                    out_ref[rows, :] = val
            if loss:
                @pl.when(i == 0)
                def _():
                    lv_ref[...] = part

                @pl.when(i > 0)
                def _():
                    lv_ref[...] += part

    row = pl.BlockSpec((tm, N), lambda i, k: (i, 0), pipeline_mode=ONCE_PER_ROW_TILE)
    in_specs = [pl.BlockSpec((tm, tk), lambda i, k: (i, k)), pl.BlockSpec((tk, N), lambda i, k: (k, 0)), row]
    args = [a, w, res]
    if loss:
        in_specs.append(row)
        args.append(tgt)
        out_specs = [row, row, pl.BlockSpec((1, N), lambda i, k: (0, 0))]
        out_shape = [jax.ShapeDtypeStruct((T, N), F32), jax.ShapeDtypeStruct((T, N), BF16),
                     jax.ShapeDtypeStruct((1, N), F32)]
    else:
        out_specs = row
        out_shape = jax.ShapeDtypeStruct((T, N), F32)
    return pl.pallas_call(
        body, name=name, grid=(T // tm, nk), in_specs=in_specs, out_specs=out_specs, out_shape=out_shape,
        compiler_params=_params(2),
    )(*args)


def _nt_matmul(name, dyb, w, scale=1.0, gate=None, up=None):
    T, D = dyb.shape
    N = w.shape[0]
    tm = _pick(T, (ROW_TILE, 512, 256, 128))
    tn = _pick(N, (512, 256, 128))
    swiglu = gate is not None

    def body(*refs):
        if swiglu:
            dy_ref, w_ref, g_ref, u_ref, dg_ref, du_ref = refs
        else:
            dy_ref, w_ref, o_ref, ob_ref = refs
        da = _nt(dy_ref[...], w_ref[...]) * scale
        if swiglu:
            dg_ref[...] = (da * g_ref[...].astype(F32)).astype(BF16)
            du_ref[...] = (da * u_ref[...].astype(F32)).astype(BF16)
        else:
            o_ref[...] = da
            ob_ref[...] = da.astype(BF16)

    tile = pl.BlockSpec((tm, tn), lambda i, j: (i, j))
    in_specs = [pl.BlockSpec((tm, D), lambda i, j: (i, 0)), pl.BlockSpec((tn, D), lambda i, j: (j, 0))]
    args = [dyb, w]
    if swiglu:
        in_specs += [tile, tile]
        args += [gate, up]
        out_shape = [jax.ShapeDtypeStruct((T, N), BF16)] * 2
    else:
        out_shape = [jax.ShapeDtypeStruct((T, N), F32), jax.ShapeDtypeStruct((T, N), BF16)]
    return pl.pallas_call(
        body, name=name, grid=(T // tm, N // tn), in_specs=in_specs, out_specs=[tile, tile],
        out_shape=out_shape, compiler_params=_params(2),
    )(*args)


def _nt_rms_bwd(name, As, Ws, x, g, dres):
    T, K = As[0].shape
    D = x.shape[1]
    tm = _pick(T, (ROW_TILE, 512, 256, 128))
    tk = _pick(K, (512, 256, 128))
    nk = K // tk
    na = len(As)

    def body(*refs):
        a_refs = refs[:na]
        w_refs = refs[na:2 * na]
        x_ref, g_ref, dres_ref, acc, dxb_ref, dg_ref = refs[2 * na:]
        dx_ref = acc
        i, k = pl.program_id(0), pl.program_id(1)

        @pl.when(k == 0)
        def _():
            acc[...] = jnp.zeros_like(acc)

        for c0 in range(0, D, ACC_COLS):
            cols = slice(c0, min(c0 + ACC_COLS, D))
            part = _nt(a_refs[0][...], w_refs[0][cols, :])
            for a_ref, w_ref in zip(a_refs[1:], w_refs[1:]):
                part = part + _nt(a_ref[...], w_ref[cols, :])
            acc[:, cols] += part

        @pl.when(k == nk - 1)
        def _():
            part = jnp.zeros((1, D), F32)
            for r0 in range(0, tm, EPILOGUE_ROWS):
                rows = slice(r0, r0 + min(EPILOGUE_ROWS, tm))
                dh = acc[rows, :]
                xv = x_ref[rows, :]
                r = _rms_r(xv)
                gd = dh * g_ref[...]
                dx = dres_ref[rows, :] + r * gd - xv * (r * r * r) * jnp.mean(gd * xv, axis=-1, keepdims=True)
                dx_ref[rows, :] = dx
                dxb_ref[rows, :] = dx.astype(BF16)
                part = part + jnp.sum(dh * xv * r, axis=0, keepdims=True)

            @pl.when(i == 0)
            def _():
                dg_ref[...] = part

            @pl.when(i > 0)
            def _():
                dg_ref[...] += part

    row = pl.BlockSpec((tm, D), lambda i, k: (i, 0), pipeline_mode=ONCE_PER_ROW_TILE)
    vec = pl.BlockSpec((1, D), lambda i, k: (0, 0))
    return pl.pallas_call(
        body, name=name, grid=(T // tm, nk),
        in_specs=[pl.BlockSpec((tm, tk), lambda i, k: (i, k))] * na
        + [pl.BlockSpec((D, tk), lambda i, k: (0, k))] * na + [row, vec, row],
        out_specs=[row, row, vec],
        out_shape=[jax.ShapeDtypeStruct((T, D), F32), jax.ShapeDtypeStruct((T, D), BF16),
                   jax.ShapeDtypeStruct((1, D), F32)],
        compiler_params=_params(2),
    )(*As, *Ws, x, g, dres)


def _tn_matmul(name, a, b, scale=1.0):
    T, M = a.shape
    N = b.shape[1]
    tn = _pick(N, (2048, 1408, 1280, 1024, 512, 256, 128))
    tm = _pick(M, tuple(c for c in (2048, 1408, 1024, 512, 256, 128) if c * tn <= TN_ACC_ELEMS))
    tk = _pick(T, (1024, 512, 256, 128))
    nk = T // tk

    def body(a_ref, b_ref, o_ref, acc):
        k = pl.program_id(2)

        @pl.when(k == 0)
        def _():
            acc[...] = jnp.zeros_like(acc)

        for r0 in range(0, tm, ACC_COLS):
            rows = slice(r0, min(r0 + ACC_COLS, tm))
            acc[rows, :] += _tn(a_ref[:, rows], b_ref[...])

        @pl.when(k == nk - 1)
        def _():
            o_ref[...] = (acc[...] * scale).astype(BF16)

    return pl.pallas_call(
        body, name=name, grid=(M // tm, N // tn, nk),
        in_specs=[pl.BlockSpec((tk, tm), lambda i, j, k: (k, i)), pl.BlockSpec((tk, tn), lambda i, j, k: (k, j))],
        out_specs=pl.BlockSpec((tm, tn), lambda i, j, k: (i, j)),
        out_shape=jax.ShapeDtypeStruct((M, N), BF16),
        scratch_shapes=[pltpu.VMEM((tm, tn), F32)], compiler_params=_params(3),
    )(a, b)


CONV_ROWS = 128
ROW_CHUNK = 32
LANE_CHUNK = 256


def _conv_fwd(name, z, w32, b, lg, lb, C):
    T = z.shape[0]
    tc = CONV_ROWS
    ntap = 31
    lc = _pick(C, (LANE_CHUNK, LANES))
    rpb = tc // HALO

    def body(zc_ref, zp_ref, w_ref, b_ref, lg_ref, lb_ref, yc_ref, ycv_ref, vbuf, ybuf):
        i = pl.program_id(0)
        zc = zc_ref[...]
        zp = zp_ref[...]
        vbuf[HALO:HALO + tc, :] = zc[:, :C] * _sigmoid(zc[:, C:])
        vbuf[0:HALO, :] = jnp.where(i > 0, zp[:, :C] * _sigmoid(zp[:, C:]), 0.0)
        for r0 in range(0, tc, ROW_CHUNK):
            for c0 in range(0, C, lc):
                acc = jnp.zeros((ROW_CHUNK, lc), F32) + b_ref[:, c0:c0 + lc]
                for k in range(ntap):
                    s = r0 + 2 + k
                    acc = acc + w_ref[k:k + 1, c0:c0 + lc] * vbuf[s:s + ROW_CHUNK, c0:c0 + lc]
                ybuf[r0:r0 + ROW_CHUNK, c0:c0 + lc] = acc
        y = ybuf[...]
        ycv_ref[...] = y
        mu = jnp.mean(y, axis=-1, keepdims=True)
        yc = y - mu
        rstd = lax.rsqrt(jnp.mean(yc * yc, axis=-1, keepdims=True) + EPS)
        ln = yc * rstd * lg_ref[...] + lb_ref[...]
        yc_ref[...] = (ln * _sigmoid(ln)).astype(BF16)

    vec = pl.BlockSpec((1, C), lambda i: (0, 0))
    return pl.pallas_call(
        body, name=name, grid=(T // tc,),
        in_specs=[pl.BlockSpec((tc, 2 * C), lambda i: (i, 0)),
                  pl.BlockSpec((HALO, 2 * C), lambda i: (jnp.maximum(i * rpb - 1, 0), 0)),
                  pl.BlockSpec((HALO, C), lambda i: (0, 0)), vec, vec, vec],
        out_specs=[pl.BlockSpec((tc, C), lambda i: (i, 0))] * 2,
        out_shape=[jax.ShapeDtypeStruct((T, C), BF16), jax.ShapeDtypeStruct((T, C), F32)],
        scratch_shapes=[pltpu.VMEM((tc + HALO, C), F32), pltpu.VMEM((tc, C), F32)],
        compiler_params=_params(1),
    )(z, z, w32, b, lg, lb)


def _conv_bwd(name, z, ycv, dycat, w32, lg, lb, C):
    T = z.shape[0]
    tc = CONV_ROWS
    ntap = 31
    lc = _pick(C, (LANE_CHUNK, LANES))
    rpb = tc // HALO
    nstep = T // tc
    nhb = T // HALO

    def ln_bwd(dyc, y, lgv, lbv):
        mu = jnp.mean(y, axis=-1, keepdims=True)
        yc = y - mu
        rstd = lax.rsqrt(jnp.mean(yc * yc, axis=-1, keepdims=True) + EPS)
        yn = yc * rstd
        ln = yn * lgv + lbv
        sg = _sigmoid(ln)
        dln = dyc * (sg * (1.0 + ln * (1.0 - sg)))
        dyn = dln * lgv
        dy = rstd * (dyn - jnp.mean(dyn, axis=-1, keepdims=True)
                     - yn * jnp.mean(dyn * yn, axis=-1, keepdims=True))
        return dy, dln, yn

    def body(zc_ref, zp_ref, y_ref, yn_ref, d_ref, dn_ref, w_ref, lg_ref, lb_ref,
             dz_ref, dw_ref, db_ref, dlg_ref, dlb_ref, vbuf, dbuf, dvbuf, dwacc):
        i = pl.program_id(0)
        lgv, lbv = lg_ref[...], lb_ref[...]
        zc = zc_ref[...]
        zp = zp_ref[...]
        a = zc[:, :C]
        sgt = _sigmoid(zc[:, C:])
        vbuf[HALO:HALO + tc, :] = a * sgt
        vbuf[0:HALO, :] = jnp.where(i > 0, zp[:, :C] * _sigmoid(zp[:, C:]), 0.0)
        dy, dln, yn = ln_bwd(d_ref[...], y_ref[...], lgv, lbv)
        dbuf[0:tc, :] = dy
        dyn_, _, _ = ln_bwd(dn_ref[...], yn_ref[...], lgv, lbv)
        dbuf[tc:tc + HALO, :] = jnp.where(i < nstep - 1, dyn_, 0.0)

        @pl.when(i == 0)
        def _():
            dwacc[...] = jnp.zeros_like(dwacc)
            db_ref[...] = jnp.zeros_like(db_ref)
            dlg_ref[...] = jnp.zeros_like(dlg_ref)
            dlb_ref[...] = jnp.zeros_like(dlb_ref)

        db_ref[...] += jnp.sum(dy, axis=0, keepdims=True)
        dlg_ref[...] += jnp.sum(dln * yn, axis=0, keepdims=True)
        dlb_ref[...] += jnp.sum(dln, axis=0, keepdims=True)

        for r0 in range(0, tc, ROW_CHUNK):
            for c0 in range(0, C, lc):
                dcur = dbuf[r0:r0 + ROW_CHUNK, c0:c0 + lc]
                acc = jnp.zeros((ROW_CHUNK, lc), F32)
                for k in range(ntap):
                    s = r0 + 30 - k
                    acc = acc + w_ref[k:k + 1, c0:c0 + lc] * dbuf[s:s + ROW_CHUNK, c0:c0 + lc]
                    sv = r0 + 2 + k
                    prod = dcur * vbuf[sv:sv + ROW_CHUNK, c0:c0 + lc]
                    red = prod[0:8]
                    for q in range(8, ROW_CHUNK, 8):
                        red = red + prod[q:q + 8]
                    dwacc[8 * k:8 * k + 8, c0:c0 + lc] += red
                dvbuf[r0:r0 + ROW_CHUNK, c0:c0 + lc] = acc
        dv = dvbuf[...]
        dz_ref[:, :C] = (dv * sgt).astype(BF16)
        dz_ref[:, C:] = (dv * a * sgt * (1.0 - sgt)).astype(BF16)

        @pl.when(i == nstep - 1)
        def _():
            for k in range(ntap):
                dw_ref[k:k + 1, :] = jnp.sum(dwacc[8 * k:8 * k + 8, :], axis=0, keepdims=True)
            dw_ref[ntap:HALO, :] = jnp.zeros((HALO - ntap, C), F32)

    vec = pl.BlockSpec((1, C), lambda i: (0, 0))
    cur = pl.BlockSpec((tc, C), lambda i: (i, 0))
    nxt = pl.BlockSpec((HALO, C), lambda i: (jnp.minimum((i + 1) * rpb, nhb - 1), 0))
    return pl.pallas_call(
        body, name=name, grid=(nstep,),
        in_specs=[pl.BlockSpec((tc, 2 * C), lambda i: (i, 0)),
                  pl.BlockSpec((HALO, 2 * C), lambda i: (jnp.maximum(i * rpb - 1, 0), 0)),
                  cur, nxt, cur, nxt, pl.BlockSpec((HALO, C), lambda i: (0, 0)), vec, vec],
        out_specs=[pl.BlockSpec((tc, 2 * C), lambda i: (i, 0)), pl.BlockSpec((HALO, C), lambda i: (0, 0)),
                   vec, vec, vec],
        out_shape=[jax.ShapeDtypeStruct((T, 2 * C), BF16), jax.ShapeDtypeStruct((HALO, C), F32),
                   jax.ShapeDtypeStruct((1, C), F32), jax.ShapeDtypeStruct((1, C), F32),
                   jax.ShapeDtypeStruct((1, C), F32)],
        scratch_shapes=[pltpu.VMEM((tc + HALO, C), F32), pltpu.VMEM((tc + HALO, C), F32),
                        pltpu.VMEM((tc, C), F32), pltpu.VMEM((8 * HALO, C), F32)],
        compiler_params=_params(1),
    )(z, z, ycv, ycv, dycat, dycat, w32, lg, lb)


def _seg_sum(u, bmat):
    hi = u.astype(BF16)
    lo = (u - hi.astype(F32)).astype(BF16)
    return _nn(hi, bmat) + _nn(lo, bmat)


def _attn_prep(name, z, gq, gk, bmat, A, c0, hd):
    T = z.shape[0]
    tm = _pick(T, (256, 128))

    def body(zq_ref, zk_ref, zv_ref, gq_ref, gk_ref, b_ref, o_ref):
        bm = b_ref[...]
        for idx, (z_ref, g_ref) in enumerate(((zq_ref, gq_ref), (zk_ref, gk_ref))):
            zv = z_ref[...]
            r = lax.rsqrt(_seg_sum(zv * zv, bm) * (1.0 / hd) + EPS)
            o_ref[:, idx * A:(idx + 1) * A] = zv * r * g_ref[...]
        o_ref[:, 2 * A:] = zv_ref[...]

    vec = pl.BlockSpec((1, A), lambda i: (0, 0))
    return pl.pallas_call(
        body, name=name, grid=(T // tm,),
        in_specs=[pl.BlockSpec((tm, A), lambda i: (i, c0)), pl.BlockSpec((tm, A), lambda i: (i, c0 + 1)),
                  pl.BlockSpec((tm, A), lambda i: (i, c0 + 2)), vec, vec,
                  pl.BlockSpec((A, A), lambda i: (0, 0))],
        out_specs=pl.BlockSpec((tm, 3 * A), lambda i: (i, 0)),
        out_shape=jax.ShapeDtypeStruct((T, 3 * A), F32), compiler_params=_params(1),
    )(z, z, z, gq, gk, bmat)


QK_SCALE = 0.125
ATTN_UNROLL = 4


def _fill_bias(bias, sl_ref, hp, d):
    qi = lax.broadcasted_iota(jnp.int32, (WINDOW, 2 * WINDOW), 0)
    kj = lax.broadcasted_iota(jnp.int32, (WINDOW, 2 * WINDOW), 1)
    dist = WINDOW + qi - kj
    inband = (dist >= 0) & (dist <= WINDOW)
    distf = dist.astype(F32)
    for hh in range(2):
        b = jnp.where(inband, -(sl_ref[2 * hp + hh] * d) * distf, NEG)
        bias[2 * hh + 1] = b
        bias[2 * hh] = jnp.where(kj >= WINDOW, b, NEG)


CHUNK = WINDOW * DILATIONS[-1]


def _deinterleave(dst, src, d, rows, dst_pitch, dst_off, src_off):
    for r in range(d):
        if d == 1:
            val = src[src_off:src_off + rows, :]
        else:
            val = src[pl.ds(src_off + r, rows, stride=d), :]
        lo = r * dst_pitch + dst_off
        dst[lo:lo + rows, :] = val.astype(dst.dtype)


def _interleave_add(dst, start, src, d, rows, src_pitch, src_off):
    for r in range(d):
        lo = r * src_pitch + src_off
        idx = pl.ds(start, rows) if d == 1 else pl.ds(start + r, rows, stride=d)
        dst[idx, :] += src[lo:lo + rows, :]


def _attn_fwd(name, qkv, slopes, A):
    T = qkv.shape[0]
    hpn, nch, nblk = A // LANES, T // CHUNK, CHUNK // WINDOW
    nbranch = len(DILATIONS)

    def body(*refs):
        sl_ref, q_ref, k_ref, kp_ref, v_ref, vp_ref, y_ref, lg_ref, qd, kd, vd, od, ld, bias = refs[:14]
        onat, lnat = refs[14:14 + nbranch], refs[14 + nbranch:]
        hp, ch = pl.program_id(0), pl.program_id(1)
        lane = lax.broadcasted_iota(jnp.int32, (1, LANES), 1)
        first = lane < (LANES // 2)
        for bi, d in enumerate(DILATIONS):
            Ld = CHUNK // d
            seg = Ld + WINDOW
            nbr = Ld // WINDOW
            _deinterleave(qd, q_ref, d, Ld, Ld, 0, 0)
            for dst, cur, prev in ((kd, k_ref, kp_ref), (vd, v_ref, vp_ref)):
                _deinterleave(dst, prev, d, WINDOW, seg, 0, CHUNK - WINDOW * d)
                _deinterleave(dst, cur, d, Ld, seg, WINDOW, 0)
            _fill_bias(bias, sl_ref, hp, d)
            ob, lb = (onat[bi], lnat[bi]) if d == 1 else (od, ld)

            def step(it, carry, Ld=Ld, seg=seg, nbr=nbr, ob=ob, lb=lb):
                r, nl = it // nbr, it % nbr
                q0 = pl.multiple_of(r * Ld + nl * WINDOW, WINDOW)
                k0 = pl.multiple_of(r * seg + nl * WINDOW, WINDOW)
                later = jnp.where(ch * nbr + nl > 0, 1, 0)
                qb = qd[pl.ds(q0, WINDOW), :]
                k2 = kd[pl.ds(k0, 2 * WINDOW), :]
                v2 = vd[pl.ds(k0, 2 * WINDOW), :]
                res = []
                for hh in range(2):
                    mh = first if hh == 0 else jnp.logical_not(first)
                    s = _nt(jnp.where(mh, qb, jnp.zeros_like(qb)), k2) + bias[2 * hh + later]
                    mx = jnp.max(s, axis=-1, keepdims=True)
                    p = jnp.exp(s - mx)
                    den = jnp.sum(p, axis=-1, keepdims=True)
                    res.append((_nn(p.astype(BF16), v2) / den, mx + jnp.log(den)))
                ob[pl.ds(q0, WINDOW), :] = jnp.where(first, res[0][0], res[1][0])
                lb[pl.ds(q0, WINDOW), :] = jnp.where(first, res[0][1], res[1][1])
                return carry

            lax.fori_loop(0, nblk, step, 0, unroll=ATTN_UNROLL)
            if d > 1:
                for r in range(d):
                    onat[bi][pl.ds(r, Ld, stride=d), :] = od[r * Ld:(r + 1) * Ld, :]
                    lnat[bi][pl.ds(r, Ld, stride=d), :] = ld[r * Ld:(r + 1) * Ld, :]
        ls = [l[...] for l in lnat]
        mx = ls[0]
        for v in ls[1:]:
            mx = jnp.maximum(mx, v)
        es = [jnp.exp(v - mx) for v in ls]
        den = es[0]
        for e in es[1:]:
            den = den + e
        out = es[0] * onat[0][...]
        for e, o in zip(es[1:], onat[1:]):
            out = out + e * o[...]
        y_ref[...] = (out / den).astype(BF16)
        lg_ref[...] = mx + jnp.log(den)

    blk = lambda m: pl.BlockSpec((CHUNK, LANES), m)
    cur = lambda which: blk(lambda hp, ch: (ch, which * hpn + hp))
    prev = lambda which: blk(lambda hp, ch: (jnp.maximum(ch - 1, 0), which * hpn + hp))
    omap = blk(lambda hp, ch: (ch, hp))
    f32buf = pltpu.VMEM((CHUNK, LANES), F32)
    return pl.pallas_call(
        body, name=name, grid=(hpn, nch),
        in_specs=[pl.BlockSpec(memory_space=pltpu.SMEM), cur(0), cur(1), prev(1), cur(2), prev(2)],
        out_specs=[omap, omap],
        out_shape=[jax.ShapeDtypeStruct((T, A), BF16), jax.ShapeDtypeStruct((T, A), F32)],
        scratch_shapes=[pltpu.VMEM((CHUNK, LANES), BF16), pltpu.VMEM((2 * CHUNK, LANES), BF16),
                        pltpu.VMEM((2 * CHUNK, LANES), BF16), f32buf, f32buf,
                        pltpu.VMEM((4, WINDOW, 2 * WINDOW), F32)] + [f32buf] * (2 * nbranch),
        compiler_params=_params(2),
    )(slopes, qkv, qkv, qkv, qkv, qkv)


def _attn_bwd(name, qkv, dycat, yatt, lg, slopes, A, catoff):
    T = qkv.shape[0]
    hpn, nch, nblk = A // LANES, T // CHUNK, CHUNK // WINDOW
    co = catoff // LANES

    def body(sl_ref, q_ref, k_ref, kp_ref, v_ref, vp_ref, do_ref, o_ref, l_ref, dq_ref, dk_ref, dv_ref,
             qd, kd, vd, dod, ddn, ddd, ldd, dqd, dkd, dvd, bias):
        hp, ch = pl.program_id(0), pl.program_id(1)
        lane = lax.broadcasted_iota(jnp.int32, (1, LANES), 1)
        first = lane < (LANES // 2)
        ddn[...] = do_ref[...] * o_ref[...].astype(F32)
        dq_ref[...] = jnp.zeros_like(dq_ref)

        @pl.when(ch == 0)
        def _():
            dk_ref[...] = jnp.zeros_like(dk_ref)
            dv_ref[...] = jnp.zeros_like(dv_ref)

        base = ch * CHUNK
        for d in DILATIONS:
            Ld = CHUNK // d
            seg = Ld + WINDOW
            nbr = Ld // WINDOW
            _deinterleave(qd, q_ref, d, Ld, Ld, 0, 0)
            _deinterleave(dod, do_ref, d, Ld, Ld, 0, 0)
            _deinterleave(ddd, ddn, d, Ld, Ld, 0, 0)
            _deinterleave(ldd, l_ref, d, Ld, Ld, 0, 0)
            for dst, cur, prev in ((kd, k_ref, kp_ref), (vd, v_ref, vp_ref)):
                _deinterleave(dst, prev, d, WINDOW, seg, 0, CHUNK - WINDOW * d)
                _deinterleave(dst, cur, d, Ld, seg, WINDOW, 0)
            dkd[0:d * seg, :] = jnp.zeros((d * seg, LANES), F32)
            dvd[0:d * seg, :] = jnp.zeros((d * seg, LANES), F32)
            _fill_bias(bias, sl_ref, hp, d)

            def step(it, carry, Ld=Ld, seg=seg, nbr=nbr):
                r, nl = it // nbr, it % nbr
                q0 = pl.multiple_of(r * Ld + nl * WINDOW, WINDOW)
                k0 = pl.multiple_of(r * seg + nl * WINDOW, WINDOW)
                later = jnp.where(ch * nbr + nl > 0, 1, 0)
                qb = qd[pl.ds(q0, WINDOW), :]
                k2 = kd[pl.ds(k0, 2 * WINDOW), :]
                v2 = vd[pl.ds(k0, 2 * WINDOW), :]
                dob = dod[pl.ds(q0, WINDOW), :]
                dd = ddd[pl.ds(q0, WINDOW), :]
                lb = ldd[pl.ds(q0, WINDOW), :]
                dk2 = jnp.zeros((2 * WINDOW, LANES), F32)
                dv2 = jnp.zeros((2 * WINDOW, LANES), F32)
                dqs = []
                for hh in range(2):
                    mh = first if hh == 0 else jnp.logical_not(first)
                    qh = jnp.where(mh, qb, jnp.zeros_like(qb))
                    doh = jnp.where(mh, dob, jnp.zeros_like(dob))
                    lcol = lb[:, hh * (LANES // 2):hh * (LANES // 2) + 1]
                    p = jnp.exp(_nt(qh, k2) + bias[2 * hh + later] - lcol)
                    dcol = jnp.sum(jnp.where(mh, dd, 0.0), axis=-1, keepdims=True)
                    ds = (p * (_nt(doh, v2) - dcol)).astype(BF16)
                    dqs.append(_nn(ds, k2))
                    dk2 = dk2 + _tn(ds, qh)
                    dv2 = dv2 + _tn(p.astype(BF16), doh)
                dqd[pl.ds(q0, WINDOW), :] = jnp.where(first, dqs[0], dqs[1])
                dkd[pl.ds(k0, 2 * WINDOW), :] += dk2
                dvd[pl.ds(k0, 2 * WINDOW), :] += dv2
                return carry

            lax.fori_loop(0, nblk, step, 0, unroll=ATTN_UNROLL)
            _interleave_add(dq_ref, 0, dqd, d, Ld, Ld, 0)
            for acc, out in ((dkd, dk_ref), (dvd, dv_ref)):
                _interleave_add(out, base, acc, d, Ld, seg, WINDOW)

                @pl.when(ch > 0)
                def _(acc=acc, out=out, d=d, seg=seg):
                    _interleave_add(out, base - WINDOW * d, acc, d, WINDOW, seg, 0)

    blk = lambda m: pl.BlockSpec((CHUNK, LANES), m)
    cur = lambda which: blk(lambda hp, ch: (ch, which * hpn + hp))
    prev = lambda which: blk(lambda hp, ch: (jnp.maximum(ch - 1, 0), which * hpn + hp))
    omap = blk(lambda hp, ch: (ch, hp))
    full = pl.BlockSpec((T, LANES), lambda hp, ch: (0, hp))
    f32buf = pltpu.VMEM((CHUNK, LANES), F32)
    bf16buf = pltpu.VMEM((CHUNK, LANES), BF16)
    return pl.pallas_call(
        body, name=name, grid=(hpn, nch),
        in_specs=[pl.BlockSpec(memory_space=pltpu.SMEM), cur(0), cur(1), prev(1), cur(2), prev(2),
                  blk(lambda hp, ch: (ch, co + hp)), omap, omap],
        out_specs=[omap, full, full],
        out_shape=[jax.ShapeDtypeStruct((T, A), F32)] * 3,
        scratch_shapes=[bf16buf, pltpu.VMEM((2 * CHUNK, LANES), BF16), pltpu.VMEM((2 * CHUNK, LANES), BF16),
                        bf16buf, f32buf, f32buf, f32buf, f32buf,
                        pltpu.VMEM((2 * CHUNK, LANES), F32), pltpu.VMEM((2 * CHUNK, LANES), F32),
                        pltpu.VMEM((4, WINDOW, 2 * WINDOW), F32)],
        compiler_params=_params(2),
    )(slopes, qkv, qkv, qkv, qkv, qkv, dycat, yatt, lg)


def _attn_bwd_combine(name, dqs, dks, dvs, z, gq, gk, bmat, fmat, A, c0, hd):
    T = z.shape[0]
    tm = _pick(T, (256, 128))
    nbr = len(dqs)

    def body(*refs):
        dq_refs, dk_refs, dv_refs = refs[:nbr], refs[nbr:2 * nbr], refs[2 * nbr:3 * nbr]
        zq_ref, zk_ref, gq_ref, gk_ref, b_ref, f_ref, dz_ref, dgq_ref, dgk_ref = refs[3 * nbr:]
        i = pl.program_id(0)
        bm = b_ref[...]

        def tot(rs):
            t = rs[0][...]
            for r in rs[1:]:
                t = t + r[...]
            return t

        for idx, (d_refs, z_ref, g_ref, dg_ref, gscale) in enumerate(
                ((dq_refs, zq_ref, gq_ref, dgq_ref, QK_SCALE), (dk_refs, zk_ref, gk_ref, dgk_ref, 1.0))):
            dy = tot(d_refs)
            zv = z_ref[...]
            r = lax.rsqrt(_seg_sum(zv * zv, bm) * (1.0 / hd) + EPS)
            gd = dy * g_ref[...]
            mean = _seg_sum(gd * zv, bm) * (1.0 / hd)
            dz_ref[:, idx * A:(idx + 1) * A] = (r * gd - zv * (r * r * r) * mean).astype(BF16)
            part = jnp.sum(dy * zv * r, axis=0, keepdims=True) * gscale

            @pl.when(i == 0)
            def _():
                dg_ref[...] = part

            @pl.when(i > 0)
            def _():
                dg_ref[...] += part

        dz_ref[:, 2 * A:] = tot(dv_refs).astype(BF16)

        @pl.when(i == T // tm - 1)
        def _():
            fm = f_ref[...]
            for dg_ref in (dgq_ref, dgk_ref):
                v = jnp.broadcast_to(dg_ref[...], (8, A))
                hi = v.astype(BF16)
                mid = (v - hi.astype(F32)).astype(BF16)
                lo = (v - hi.astype(F32) - mid.astype(F32)).astype(BF16)
                dg_ref[...] = (_nn(hi, fm) + _nn(mid, fm) + _nn(lo, fm))[0:1]

    blk = pl.BlockSpec((tm, A), lambda i: (i, 0))
    vec = pl.BlockSpec((1, A), lambda i: (0, 0))
    return pl.pallas_call(
        body, name=name, grid=(T // tm,),
        in_specs=[blk] * (3 * nbr) + [pl.BlockSpec((tm, A), lambda i: (i, c0)),
                                      pl.BlockSpec((tm, A), lambda i: (i, c0 + 1)), vec, vec,
                                      pl.BlockSpec((A, A), lambda i: (0, 0)),
                                      pl.BlockSpec((A, A), lambda i: (0, 0))],
        out_specs=[pl.BlockSpec((tm, 3 * A), lambda i: (i, 0)), vec, vec],
        out_shape=[jax.ShapeDtypeStruct((T, 3 * A), BF16), jax.ShapeDtypeStruct((1, A), F32),
                   jax.ShapeDtypeStruct((1, A), F32)],
        compiler_params=_params(1),
    )(*dqs, *dks, *dvs, z, z, gq, gk, bmat, fmat)


def _local_step(x, tgt, S, comm, hd):
    T, D = x.shape
    C = S["conv_b_dw"].shape[1]
    A = C
    H = A // hd
    Dmix = C + A
    c0 = (2 * C) // A
    slopes = 2.0 ** (-ALIBI_MAX_BIAS * jnp.arange(1, H + 1, dtype=F32) / H)
    seg = jnp.arange(A) // hd
    bmat = (seg[:, None] == seg[None, :]).astype(BF16)
    pos_in_head = jnp.arange(A) % hd
    fmat = (pos_in_head[:, None] == pos_in_head[None, :]).astype(BF16)
    gq = jnp.tile(S["q_norm_g"], (1, H)) * QK_SCALE
    gk = jnp.tile(S["k_norm_g"], (1, H))

    wg1, wu1 = comm.weights(("ffn1_w_gate", "ffn1_w_up"), None)
    h1, gate1, up1, a1 = _norm_matmul("ffn1_up", x, comm.tie(S["ffn1_norm_g"]), [wg1, wu1], True)
    wd1, win, w32 = comm.weights(("ffn1_w_down", "w_in", "conv_w32"), a1)
    x1 = _matmul_res("ffn1_down", a1, wd1, x, 0.5)
    h2, z = _norm_matmul("mix_in", x1, S["mix_norm_g"], [win], False)
    yc, ycv = _conv_fwd("conv_fwd", z, w32, S["conv_b_dw"], S["conv_ln_g"], S["conv_ln_b"], C)
    qkv = _attn_prep("attn_prep", z, gq, gk, bmat, A, c0, hd)
    yatt, lg = _attn_fwd("attn_fwd", qkv, slopes, A)
    ycat = jnp.concatenate([yc, yatt], axis=1)
    wout, wg2, wu2, wd2 = comm.weights(("w_out", "ffn2_w_gate", "ffn2_w_up", "ffn2_w_down"), yatt)
    x2 = _matmul_res("mix_out", ycat, wout, x1, 1.0)
    h3, gate2, up2, a2 = _norm_matmul("ffn2_up", x2, S["ffn2_norm_g"], [wg2, wu2], True)
    dx3, dx3b, lossvec = _matmul_res("ffn2_down_loss", a2, wd2, x2, 0.5, tgt=tgt)

    G = {}
    dgate2, dup2 = _nt_matmul("ffn2_dact", dx3b, wd2, 0.5, gate2, up2)
    comm.reduce_begin("ffn2", {"ffn2_w_down": _tn_matmul("ffn2_dwd", a2, dx3b, 0.5),
                               "ffn2_w_gate": _tn_matmul("ffn2_dwg", h3, dgate2),
                               "ffn2_w_up": _tn_matmul("ffn2_dwu", h3, dup2)})
    dx2, dx2b, G["ffn2_norm_g"] = _nt_rms_bwd("ffn2_dx", [dgate2, dup2], [wg2, wu2],
                                              x2, comm.tie(S["ffn2_norm_g"]), dx3)
    dwout = _tn_matmul("mix_dwout", ycat, dx2b)
    dycat, _ = _nt_matmul("mix_dycat", dx2b, wout)
    dzc, G["conv_w32"], G["conv_b_dw"], G["conv_ln_g"], G["conv_ln_b"] = _conv_bwd(
        "conv_bwd", z, ycv, dycat, w32, S["conv_ln_g"], S["conv_ln_b"], C)
    dq, dk, dv = _attn_bwd("attn_bwd", qkv, dycat, yatt, lg, slopes, A, C)
    dzqkv, G["q_norm_g"], G["k_norm_g"] = _attn_bwd_combine(
        "attn_bwd_combine", [dq], [dk], [dv], z, gq, gk, bmat, fmat, A, c0, hd)
    comm.reduce_end("ffn2", dzqkv)
    dz = jnp.concatenate([dzc, dzqkv], axis=1)
    comm.reduce_begin("mix", {"w_out": dwout, "w_in": _tn_matmul("mix_dwin", h2, dz)})
    dx1, dx1b, G["mix_norm_g"] = _nt_rms_bwd("mix_dx", [dz], [win], x1, comm.tie(S["mix_norm_g"]), dx2)
    dgate1, dup1 = _nt_matmul("ffn1_dact", dx1b, wd1, 0.5, gate1, up1)
    dwd1 = _tn_matmul("ffn1_dwd", a1, dx1b, 0.5)
    dwg1 = _tn_matmul("ffn1_dwg", h1, dgate1)
    dwu1 = _tn_matmul("ffn1_dwu", h1, dup1)
    comm.reduce_end("mix", dwu1)
    comm.reduce_begin("ffn1", {"ffn1_w_down": dwd1, "ffn1_w_gate": dwg1, "ffn1_w_up": dwu1})
    dx0, _, G["ffn1_norm_g"] = _nt_rms_bwd("ffn1_dx", [dgate1, dup1], [wg1, wu1],
                                           x, comm.tie(S["ffn1_norm_g"]), dx1)
    comm.reduce_end("ffn1", dx0)
    return lossvec, dx0, G


BIG = (("ffn1_w_gate", 1), ("ffn1_w_up", 1), ("ffn1_w_down", 0), ("w_in", 1), ("w_out", 0),
       ("ffn2_w_gate", 1), ("ffn2_w_up", 1), ("ffn2_w_down", 0))
AXIS = dict(BIG)
FLIPS = ((1, 0), (0, 1), (1, 1))
HBM = pl.BlockSpec(memory_space=pltpu.HBM)
SEM = pl.BlockSpec(memory_space=pltpu.SEMAPHORE)
EFFECT = pltpu.SideEffectType.DATAFLOW_SIDE_EFFECTING
TOKEN = jax.ShapeDtypeStruct((8, LANES), F32)


def _window(ref, shape, axis, slab=None, half=None):
    idx = [pl.ds(0, shape[0]), pl.ds(0, shape[1])]
    if slab is not None:
        n = shape[axis] // 4
        idx[axis] = pl.ds(pl.multiple_of(slab * n, 8), n)
    if half is not None:
        hs = shape[1 - axis] // 2
        idx[1 - axis] = pl.ds(pl.multiple_of(half * hs, 8), hs)
    return ref.at[idx[0], idx[1]]


def _position():
    return lax.axis_index("x"), lax.axis_index("y"), lax.axis_index("c")


def _half_shape(shape, axis):
    return (shape[0] // 2, shape[1]) if axis == 1 else (shape[0], shape[1] // 2)


def _slab_shape(shape, axis):
    return (shape[0], shape[1] // 4) if axis == 1 else (shape[0] // 4, shape[1])


def _piece_shape(shape, axis):
    return _half_shape(_slab_shape(shape, axis), axis)


def _full_shape(shard, axis):
    return (shard.shape[0], shard.shape[1] * 4) if axis == 1 else (shard.shape[0] * 4, shard.shape[1])


def _hbm(a):
    return pltpu.with_memory_space_constraint(a, pltpu.HBM)


def _remote(src, dst, send_sem, recv_sem, to):
    return pltpu.make_async_remote_copy(src_ref=src, dst_ref=dst, send_sem=send_sem, recv_sem=recv_sem,
                                        device_id=to, device_id_type=MESH)


def _place(name, pos, w, axis):
    R, Cc = w.shape
    tr = _pick(R, (256, 128, 64, 32, 16))
    nrb = R // tr

    def body(pos_ref, w_ref, o_ref):
        o_ref[...] = w_ref[...].astype(BF16)

    omap = (lambda i, p: (i, p[0])) if axis == 1 else (lambda i, p: (p[0] * nrb + i, 0))
    return pl.pallas_call(
        body, name=name,
        grid_spec=pltpu.PrefetchScalarGridSpec(
            num_scalar_prefetch=1, grid=(nrb,), in_specs=[pl.BlockSpec((tr, Cc), lambda i, p: (i, 0))],
            out_specs=pl.BlockSpec((tr, Cc), omap)),
        out_shape=jax.ShapeDtypeStruct(_full_shape(w, axis), BF16), compiler_params=_params(1),
    )(pos, w)


def _gather_now(name, axes, fulls):
    nt = len(fulls)
    shapes = [f.shape for f in fulls]

    def body(*refs):
        outs, token = refs[nt:2 * nt], refs[2 * nt]
        send_sems, recv_sems = refs[2 * nt + 1:]
        x, y, c = _position()
        j0 = 2 * x + y
        sib = (x, y, 1 - c)

        def copy(t, k, slab, half, to):
            win = _window(outs[t], shapes[t], axes[t], slab=slab, half=half)
            return _remote(win, win, send_sems.at[t, k], recv_sems.at[t, k], to)

        sends = []
        for k, (fx, fy) in enumerate(FLIPS):
            for t in range(nt):
                cp = copy(t, k, j0, c, (x ^ fx, y ^ fy, c))
                cp.start()
                sends.append(cp)
        for k, (fx, fy) in enumerate(FLIPS):
            js = 2 * (x ^ fx) + (y ^ fy)
            for t in range(nt):
                copy(t, k, js, c, sib).wait_recv()
                cp = copy(t, 3 + k, js, c, sib)
                cp.start()
                sends.append(cp)
        for k, (fx, fy) in enumerate(FLIPS):
            js = 2 * (x ^ fx) + (y ^ fy)
            for t in range(nt):
                copy(t, 3 + k, js, 1 - c, sib).wait_recv()
        for cp in sends:
            cp.wait_send()
        token[...] = jnp.zeros_like(token)

    res = pl.pallas_call(
        body, name=name, in_specs=[ANY] * nt,
        out_specs=[ANY] * nt + [pl.BlockSpec(memory_space=pltpu.VMEM)],
        out_shape=[jax.ShapeDtypeStruct(s, BF16) for s in shapes] + [TOKEN],
        input_output_aliases={t: t for t in range(nt)},
        scratch_shapes=[pltpu.SemaphoreType.DMA((nt, 6)), pltpu.SemaphoreType.DMA((nt, 6))],
    )(*fulls)
    return list(res[:nt]), res[nt]


def _split_start(name, arrays, ncopies, plan):
    na = len(arrays)

    def body(*refs):
        ins = refs[:na]
        send_sems, recv_sems = refs[na], refs[na + 1]
        token = refs[-1]
        x, y, c = _position()
        for i, (src, dst, to) in enumerate(plan(ins, x, y, c)):
            _remote(src, dst, send_sems.at[i], recv_sems.at[i], to).start()
        token[...] = jnp.zeros_like(token)

    res = pl.pallas_call(
        body, name=name, in_specs=[HBM] * na,
        out_specs=tuple([SEM, SEM] + [HBM] * na + [pl.BlockSpec(memory_space=pltpu.VMEM)]),
        out_shape=tuple([pltpu.SemaphoreType.DMA((ncopies,)), pltpu.SemaphoreType.DMA((ncopies,))]
                        + [pltpu.HBM(a.shape, a.dtype) for a in arrays] + [TOKEN]),
        input_output_aliases={i: 2 + i for i in range(na)},
        compiler_params=pltpu.CompilerParams(has_side_effects=EFFECT),
    )(*[_hbm(a) for a in arrays])
    return (res[0], res[1]), list(res[2:2 + na]), res[-1]


def _split_wait(name, arrays, sems, after, plan):
    na = len(arrays)

    def body(*refs):
        ins = refs[:na]
        send_sems, recv_sems = refs[na], refs[na + 1]
        x, y, c = _position()
        for i, (src, dst, to) in enumerate(plan(ins, x, y, c)):
            cp = _remote(src, dst, send_sems.at[i], recv_sems.at[i], to)
            cp.wait_send()
            cp.wait_recv()

    res = pl.pallas_call(
        body, name=name, in_specs=[HBM] * na + [SEM, SEM, ANY],
        out_specs=tuple([HBM] * na), out_shape=tuple(pltpu.HBM(a.shape, a.dtype) for a in arrays),
        input_output_aliases={i: i for i in range(na)},
        compiler_params=pltpu.CompilerParams(has_side_effects=EFFECT),
    )(*arrays, *sems, after)
    return list(res)


def _gather_plan(axes, shapes, conv_shape):
    nt = len(axes)

    def plan(refs, x, y, c):
        j0 = 2 * x + y
        out = []
        for fx, fy in FLIPS:
            to = (x ^ fx, y ^ fy, c)
            for t in range(nt):
                win = _window(refs[t], shapes[t], axes[t], slab=j0, half=c)
                out.append((win, win, to))
            if conv_shape is not None:
                win = _window(refs[nt], conv_shape, 1, slab=j0)
                out.append((win, win, to))
        return out

    return plan


def _gather_finish(name, axes, fulls):
    nt = len(axes)
    shapes = [f.shape for f in fulls]

    def body(*refs):
        outs = refs[nt:2 * nt]
        send_sems, recv_sems = refs[2 * nt:]
        x, y, c = _position()
        sib = (x, y, 1 - c)
        cps = []
        for k, (fx, fy) in enumerate(FLIPS):
            js = 2 * (x ^ fx) + (y ^ fy)
            for t in range(nt):
                landed = _window(outs[t], shapes[t], axes[t], slab=js, half=c)
                cp = _remote(landed, landed, send_sems.at[t, k], recv_sems.at[t, k], sib)
                cp.start()
                cps.append(cp)
        for k, (fx, fy) in enumerate(FLIPS):
            js = 2 * (x ^ fx) + (y ^ fy)
            for t in range(nt):
                other = _window(outs[t], shapes[t], axes[t], slab=js, half=1 - c)
                _remote(other, other, send_sems.at[t, k], recv_sems.at[t, k], sib).wait_recv()
        for cp in cps:
            cp.wait_send()

    res = pl.pallas_call(
        body, name=name, in_specs=[ANY] * nt, out_specs=[ANY] * nt,
        out_shape=[jax.ShapeDtypeStruct(f.shape, f.dtype) for f in fulls],
        input_output_aliases={t: t for t in range(nt)},
        scratch_shapes=[pltpu.SemaphoreType.DMA((nt, 3)), pltpu.SemaphoreType.DMA((nt, 3))],
    )(*fulls)
    return list(res)


def _pair_exchange(name, srcs, windows, out_shapes, dtype):
    nt = len(srcs)

    def body(*refs):
        ins, outs = refs[:nt], refs[nt:2 * nt]
        send_sems, recv_sems = refs[2 * nt:]
        x, y, c = _position()
        cps = []
        for t in range(nt):
            cp = _remote(windows[t](ins[t], c), outs[t], send_sems.at[t], recv_sems.at[t], (x, y, 1 - c))
            cp.start()
            cps.append(cp)
        for cp in cps:
            cp.wait()

    return pl.pallas_call(
        body, name=name, in_specs=[ANY] * nt, out_specs=[ANY] * nt,
        out_shape=[jax.ShapeDtypeStruct(s, dtype) for s in out_shapes],
        scratch_shapes=[pltpu.SemaphoreType.DMA((nt,)), pltpu.SemaphoreType.DMA((nt,))],
    )(*srcs)


def _scatter_plan(axes, shapes):
    nt = len(axes)

    def plan(refs, x, y, c):
        out = []
        for k, (fx, fy) in enumerate(FLIPS):
            js = 2 * (x ^ fx) + (y ^ fy)
            for t in range(nt):
                src = _window(refs[t], _half_shape(shapes[t], axes[t]), axes[t], slab=js)
                out.append((src, refs[nt + t].at[k], (x ^ fx, y ^ fy, c)))
        return out

    return plan


def _gather_small(packed):
    R, Cc = packed.shape

    def body(p_ref, o_ref, send_sems, recv_sems, loc_sem):
        x, y, c = _position()
        me = 4 * x + 2 * y + c
        mine = pltpu.make_async_copy(p_ref, o_ref.at[me], loc_sem)
        mine.start()
        cps = []
        for k in range(1, 8):
            fx, fy, fc = (k >> 2) & 1, (k >> 1) & 1, k & 1
            cp = pltpu.make_async_remote_copy(
                src_ref=p_ref, dst_ref=o_ref.at[me], send_sem=send_sems.at[k - 1], recv_sem=recv_sems.at[k - 1],
                device_id=(x ^ fx, y ^ fy, c ^ fc), device_id_type=MESH)
            cp.start()
            cps.append(cp)
        for cp in cps:
            cp.wait()
        mine.wait()

    return pl.pallas_call(
        body, name="gather_small_grads", in_specs=[ANY], out_specs=ANY,
        out_shape=jax.ShapeDtypeStruct((8, R, Cc), F32),
        scratch_shapes=[pltpu.SemaphoreType.DMA((7,)), pltpu.SemaphoreType.DMA((7,)), pltpu.SemaphoreType.DMA],
    )(packed)


def _sum_slots(name, slots):
    n, R, Cc = slots.shape

    def body(s_ref, o_ref):
        t = s_ref[0]
        for i in range(1, n):
            t = t + s_ref[i]
        o_ref[...] = t

    return pl.pallas_call(
        body, name=name, grid=(1,), in_specs=[pl.BlockSpec((n, R, Cc), lambda i: (0, 0, 0))],
        out_specs=pl.BlockSpec((R, Cc), lambda i: (0, 0)), out_shape=jax.ShapeDtypeStruct((R, Cc), F32),
        compiler_params=_params(1),
    )(slots)


def _pair_sum(name, pos, g, land, shape, axis):
    hshape = _half_shape(shape, axis)
    R, Cc = hshape
    tr = _pick(R, (256, 128, 64, 32, 16))
    nrb = R // tr

    def body(pos_ref, g_ref, l_ref, o_ref):
        o_ref[...] = (g_ref[...].astype(F32) + l_ref[...].astype(F32)).astype(BF16)

    if axis == 1:
        gmap = lambda i, p: (p[1] * nrb + i, 0)
    else:
        gmap = lambda i, p: (i, p[1])
    blk = pl.BlockSpec((tr, Cc), lambda i, p: (i, 0))
    return pl.pallas_call(
        body, name=name,
        grid_spec=pltpu.PrefetchScalarGridSpec(
            num_scalar_prefetch=1, grid=(nrb,), in_specs=[pl.BlockSpec((tr, Cc), gmap), blk], out_specs=blk),
        out_shape=jax.ShapeDtypeStruct(hshape, BF16), compiler_params=_params(1),
    )(pos, g, land)


def _chip_sum(name, pos, sb, land, shape, axis):
    hshape = _half_shape(shape, axis)
    pshape = _piece_shape(shape, axis)
    R, Cc = pshape
    tr = _pick(R, (256, 128, 64, 32, 16))
    nrb = R // tr

    def body(pos_ref, s_ref, l_ref, o_ref):
        t = s_ref[...].astype(F32)
        for k in range(3):
            t = t + l_ref[k].astype(F32)
        o_ref[...] = t

    if axis == 1:
        smap = lambda i, p: (i, p[0])
    else:
        smap = lambda i, p: (p[0] * nrb + i, 0)
    return pl.pallas_call(
        body, name=name,
        grid_spec=pltpu.PrefetchScalarGridSpec(
            num_scalar_prefetch=1, grid=(nrb,),
            in_specs=[pl.BlockSpec((tr, Cc), smap), pl.BlockSpec((3, tr, Cc), lambda i, p: (0, i, 0))],
            out_specs=pl.BlockSpec((tr, Cc), lambda i, p: (i, 0))),
        out_shape=jax.ShapeDtypeStruct(pshape, F32), compiler_params=_params(1),
    )(pos, sb, land)


def _adam_math(w, g, m, v):
    m = ADAM_B1 * m + (1.0 - ADAM_B1) * g
    v = ADAM_B2 * v + (1.0 - ADAM_B2) * (g * g)
    m_hat = m / (1.0 - ADAM_B1 ** ADAM_STEP)
    v_hat = v / (1.0 - ADAM_B2 ** ADAM_STEP)
    delta = -ADAM_LR * (m_hat / (jnp.sqrt(v_hat) + ADAM_EPS) + ADAM_WD * w)
    return delta, m, v


def _adamw_halves(name, pos, w, m, v, mine, theirs, axis):
    R, Cc = w.shape
    hr, hc = mine.shape
    tr = _pick(hr, (256, 128, 64, 32, 16))
    nrb = hr // tr

    def body(pos_ref, w_ref, m_ref, v_ref, a_ref, b_ref, g_ref, d_ref, nm_ref, nv_ref):
        half = pl.program_id(0)
        g = jnp.where(half == pos_ref[1], a_ref[...], b_ref[...])
        d, nm, nv = _adam_math(w_ref[...], g, m_ref[...], v_ref[...])
        g_ref[...] = g
        d_ref[...] = d
        nm_ref[...] = nm
        nv_ref[...] = nv

    if axis == 1:
        wmap = lambda h, i, p: (h * nrb + i, 0)
    else:
        wmap = lambda h, i, p: (i, h)
    wblk = pl.BlockSpec((tr, hc), wmap)
    hblk = pl.BlockSpec((tr, hc), lambda h, i, p: (i, 0))
    return pl.pallas_call(
        body, name=name,
        grid_spec=pltpu.PrefetchScalarGridSpec(
            num_scalar_prefetch=1, grid=(2, nrb), in_specs=[wblk, wblk, wblk, hblk, hblk], out_specs=[wblk] * 4),
        out_shape=[jax.ShapeDtypeStruct((R, Cc), F32)] * 4, compiler_params=_params(2),
    )(pos, w, m, v, mine, theirs)


def _adamw_small(name, w, g, m, v):
    def body(w_ref, g_ref, m_ref, v_ref, d_ref, nm_ref, nv_ref):
        d, nm, nv = _adam_math(w_ref[...], g_ref[...], m_ref[...], v_ref[...])
        d_ref[...] = d
        nm_ref[...] = nm
        nv_ref[...] = nv

    blk = pl.BlockSpec(w.shape, lambda i: (0, 0))
    return pl.pallas_call(
        body, name=name, grid=(1,), in_specs=[blk] * 4, out_specs=[blk] * 3,
        out_shape=[jax.ShapeDtypeStruct(w.shape, F32)] * 3, compiler_params=_params(1),
    )(w, g, m, v)


SMALL = ("ffn1_norm_g", "mix_norm_g", "conv_b_dw", "conv_ln_g", "conv_ln_b", "q_norm_g", "k_norm_g", "ffn2_norm_g")
ORDER = ("ffn1_norm_g", "ffn1_w_gate", "ffn1_w_up", "ffn1_w_down", "mix_norm_g", "w_in", "conv_w_dw", "conv_b_dw",
         "conv_ln_g", "conv_ln_b", "q_norm_g", "k_norm_g", "w_out", "ffn2_norm_g", "ffn2_w_gate", "ffn2_w_up",
         "ffn2_w_down")
GATHER_FIRST = ("ffn1_w_gate", "ffn1_w_up")
GATHER_SECOND = ("ffn1_w_down", "w_in")
GATHER_THIRD = ("w_out", "ffn2_w_gate", "ffn2_w_up", "ffn2_w_down")


class _Exchange:
    def __init__(self, P, Mo, Vo, conv_shard, pos):
        self.P, self.Mo, self.Vo, self.pos = P, Mo, Vo, pos
        self.tokens = []
        self.pending = {}
        self.reducing = {}
        self.results = {}
        placed = {n: _place("place_" + n, pos, P[n][0], a) for n, a in BIG}
        self.shapes = {n: placed[n].shape for n, _ in BIG}
        first, tok = _gather_now("gather_first", [AXIS[n] for n in GATHER_FIRST], [placed[n] for n in GATHER_FIRST])
        self.ready = dict(zip(GATHER_FIRST, first))
        cq = conv_shard.shape[1]
        conv_full = lax.dynamic_update_slice(jnp.zeros((conv_shard.shape[0], 4 * cq), F32), conv_shard,
                                             (0, pos[0] * cq))
        for gname, names, conv in (("second", GATHER_SECOND, conv_full), ("third", GATHER_THIRD, None)):
            axes = [AXIS[n] for n in names]
            shapes = [self.shapes[n] for n in names]
            arrays = [placed[n] for n in names] + ([conv] if conv is not None else [])
            small = min(range(len(arrays)), key=lambda i: arrays[i].size)
            arrays[small] = arrays[small] + tok[0, 0].astype(arrays[small].dtype)
            plan = _gather_plan(axes, shapes, conv.shape if conv is not None else None)
            sems, thru, tok = _split_start("gather_%s_start" % gname, arrays, 3 * len(arrays), plan)
            self.tokens.append(tok)
            for n in names + (("conv_w32",) if conv is not None else ()):
                self.pending[n] = (gname, names, axes, plan, sems, thru, conv is not None)

    def tie(self, v):
        for tok in self.tokens:
            v = v + tok[0:1, 0:1]
        self.tokens = []
        return v

    def weights(self, names, after):
        if names[0] in self.pending:
            gname, gnames, axes, plan, sems, thru, has_conv = self.pending[names[0]]
            thru = _split_wait("gather_%s_wait" % gname, thru, sems, after, plan)
            nt = len(gnames)
            fulls = _gather_finish("gather_%s_finish" % gname, axes, thru[:nt])
            for n, f in zip(gnames, fulls):
                self.ready[n] = f
                del self.pending[n]
            if has_conv:
                self.ready["conv_w32"] = thru[nt]
                del self.pending["conv_w32"]
        return [self.ready[n] for n in names]

    def reduce_begin(self, gname, grads):
        names = list(grads)
        axes = [AXIS[n] for n in names]
        shapes = [self.shapes[n] for n in names]
        gs = [grads[n] for n in names]
        to_sibling = [(lambda ref, c, s=s, a=a: _window(ref, s, a, half=1 - c)) for s, a in zip(shapes, axes)]
        landed = _pair_exchange("pair_exchange_" + gname, gs, to_sibling,
                                [_half_shape(s, a) for s, a in zip(shapes, axes)], BF16)
        sbs = [_pair_sum("pair_sum_" + n, self.pos, g, l, s, a)
               for n, a, g, l, s in zip(names, axes, gs, landed, shapes)]
        lands = [lax.empty((3,) + _piece_shape(s, a), BF16) for s, a in zip(shapes, axes)]
        plan = _scatter_plan(axes, shapes)
        sems, thru, tok = _split_start("scatter_%s_start" % gname, sbs + lands, 3 * len(names), plan)
        self.tokens.append(tok)
        self.reducing[gname] = (names, axes, shapes, plan, sems, thru)

    def reduce_end(self, gname, after):
        names, axes, shapes, plan, sems, thru = self.reducing.pop(gname)
        nt = len(names)
        thru = _split_wait("scatter_%s_wait" % gname, thru, sems, after, plan)
        mine = [_chip_sum("chip_sum_" + n, self.pos, sb, l, s, a)
                for n, a, sb, l, s in zip(names, axes, thru[:nt], thru[nt:], shapes)]
        theirs = _pair_exchange("half_exchange_" + gname, mine, [(lambda ref, c: ref)] * nt,
                                [m.shape for m in mine], F32)
        for n, a, mi, th in zip(names, axes, mine, theirs):
            g, d, nm, nv = _adamw_halves("adamw_" + n, self.pos, self.P[n][0], self.Mo[n][0], self.Vo[n][0],
                                         mi, th, a)
            self.results[n] = (g[None], d[None], nm[None], nv[None])


def kernel(x, ffn1_norm_g, ffn1_w_gate, ffn1_w_up, ffn1_w_down, mix_norm_g, w_in, conv_w_dw, conv_b_dw, conv_ln_g, conv_ln_b, q_norm_g, k_norm_g, w_out, ffn2_norm_g, ffn2_w_gate, ffn2_w_up, ffn2_w_down, loss_target, m_ffn1_norm_g, m_ffn1_w_gate, m_ffn1_w_up, m_ffn1_w_down, m_mix_norm_g, m_w_in, m_conv_w_dw, m_conv_b_dw, m_conv_ln_g, m_conv_ln_b, m_q_norm_g, m_k_norm_g, m_w_out, m_ffn2_norm_g, m_ffn2_w_gate, m_ffn2_w_up, m_ffn2_w_down, v_ffn1_norm_g, v_ffn1_w_gate, v_ffn1_w_up, v_ffn1_w_down, v_mix_norm_g, v_w_in, v_conv_w_dw, v_conv_b_dw, v_conv_ln_g, v_conv_ln_b, v_q_norm_g, v_k_norm_g, v_w_out, v_ffn2_norm_g, v_ffn2_w_gate, v_ffn2_w_up, v_ffn2_w_down):
    args = dict(locals())
    P = {n: args[n] for n in ORDER}
    Mo = {n: args["m_" + n] for n in ORDER}
    Vo = {n: args["v_" + n] for n in ORDER}
    xs = x[0]
    tgt = loss_target[0]
    T, D = xs.shape
    hd = q_norm_g.shape[-1]
    C = conv_b_dw.shape[-1]
    ntap = conv_w_dw.shape[1]
    cx, cy, cc = _position()
    j0 = 2 * cx + cy
    pos = jnp.stack([j0, cc]).astype(jnp.int32)

    conv_shard = jnp.pad(conv_w_dw[0], ((0, HALO - ntap), (0, 0)))
    comm = _Exchange(P, Mo, Vo, conv_shard, pos)
    lossvec, dx0, G = _local_step(xs, tgt, {n: P[n] for n in SMALL}, comm, hd)
    loss = lax.psum(0.5 / D * jnp.sum(lossvec), AXES)
    grads, deltas, new_m, new_v = {}, {}, {}, {}
    for n, _ in BIG:
        grads[n], deltas[n], new_m[n], new_v[n] = comm.results[n]

    rows = [G["conv_w32"]]
    for n in ("ffn1_norm_g", "mix_norm_g", "ffn2_norm_g"):
        rows.append(G[n].reshape(D // C, C))
    for n in ("conv_b_dw", "conv_ln_g", "conv_ln_b", "q_norm_g", "k_norm_g"):
        rows.append(G[n])
    packed = jnp.concatenate(rows, axis=0)
    packed = jnp.pad(packed, ((0, -packed.shape[0] % 8), (0, 0)))
    total = _sum_slots("sum_small_grads", _gather_small(packed))
    r = HALO
    small_g = {}
    cq = C // 4
    small_g["conv_w_dw"] = lax.dynamic_slice(total[:ntap], (0, j0 * cq), (ntap, cq))
    for n in ("ffn1_norm_g", "mix_norm_g", "ffn2_norm_g"):
        small_g[n] = total[r:r + D // C].reshape(1, D)
        r += D // C
    for n in ("conv_b_dw", "conv_ln_g", "conv_ln_b"):
        small_g[n] = total[r:r + 1]
        r += 1
    for n in ("q_norm_g", "k_norm_g"):
        small_g[n] = total[r:r + 1, :hd]
        r += 1
    for n in ("conv_w_dw",) + SMALL:
        lead = n == "conv_w_dw"
        w2, m2, v2 = (P[n][0], Mo[n][0], Vo[n][0]) if lead else (P[n], Mo[n], Vo[n])
        d, nm, nv = _adamw_small("adamw_" + n, w2, small_g[n], m2, v2)
        if lead:
            grads[n], deltas[n], new_m[n], new_v[n] = small_g[n][None], d[None], nm[None], nv[None]
        else:
            grads[n], deltas[n], new_m[n], new_v[n] = small_g[n], d, nm, nv

    return (loss, dx0[None], *[grads[n] for n in ORDER], *[deltas[n] for n in ORDER],
            *[new_m[n] for n in ORDER], *[new_v[n] for n in ORDER])
```

```python
import jax
import jax.numpy as jnp
from jax import lax
from jax.experimental import pallas as pl
from jax.experimental.pallas import tpu as pltpu

F32 = jnp.float32
BF16 = jnp.bfloat16
EPS = 1e-6
WINDOW = 128
DILATIONS = (1, 4, 16)
ALIBI_MAX_BIAS = 8.0
LANES = 128
HALO = 32
ADAM_LR, ADAM_B1, ADAM_B2, ADAM_EPS, ADAM_WD, ADAM_STEP = 0.001, 0.9, 0.999, 1e-08, 0.01, 10
VMEM_LIMIT_MB = 62
ROW_TILE = 1024
TN_ACC_ELEMS = 3 * 1024 * 1024
ONCE_PER_ROW_TILE = pl.Buffered(1)
EPILOGUE_ROWS = 256
ACC_COLS = 512
MESH = pl.DeviceIdType.MESH
ANY = pl.BlockSpec(memory_space=pl.ANY)
AXES = ("x", "y", "c")
NEG = -1e30


def _pick(n, cands):
    for c in cands:
        if n % c == 0:
            return c
    return n


def _params(nsem):
    return pltpu.CompilerParams(dimension_semantics=("arbitrary",) * nsem,
                                vmem_limit_bytes=VMEM_LIMIT_MB << 20)


def _nn(a, b):
    return jnp.dot(a, b, preferred_element_type=F32)


def _nt(a, b):
    return lax.dot_general(a, b, (((1,), (1,)), ((), ())), preferred_element_type=F32)


def _tn(a, b):
    return lax.dot_general(a, b, (((0,), (0,)), ((), ())), preferred_element_type=F32)


def _sigmoid(v):
    return jax.nn.sigmoid(v)


def _rms_r(xv):
    return lax.rsqrt(jnp.mean(xv * xv, axis=-1, keepdims=True) + EPS)


def _norm_matmul(name, x, g, ws, swiglu):
    T, D = x.shape
    N = ws[0].shape[1]
    tm = _pick(T, (ROW_TILE, 512, 256, 128))
    tn = _pick(N, (512, 256, 128))
    nw = len(ws)

    def body(*refs):
        x_ref, g_ref = refs[:2]
        w_refs = refs[2:2 + nw]
        outs = refs[2 + nw:-1]
        hs = refs[-1]

        @pl.when(pl.program_id(1) == 0)
        def _():
            for r0 in range(0, tm, EPILOGUE_ROWS):
                rows = slice(r0, r0 + min(EPILOGUE_ROWS, tm))
                xv = x_ref[rows, :]
                hv = (xv * _rms_r(xv) * g_ref[...]).astype(BF16)
                hs[rows, :] = hv
                outs[0][rows, :] = hv

        h = hs[...]
        if swiglu:
            gt = _nn(h, w_refs[0][...])
            u = _nn(h, w_refs[1][...])
            sg = _sigmoid(gt)
            silu = gt * sg
            outs[1][...] = (u * (sg * (1.0 + gt * (1.0 - sg)))).astype(BF16)
            outs[2][...] = silu.astype(BF16)
            outs[3][...] = (silu * u).astype(BF16)
        else:
            outs[1][...] = _nn(h, w_refs[0][...])

    row = pl.BlockSpec((tm, D), lambda i, j: (i, 0))
    col = pl.BlockSpec((D, tn), lambda i, j: (0, j))
    tile = pl.BlockSpec((tm, tn), lambda i, j: (i, j))
    if swiglu:
        out_shape = [jax.ShapeDtypeStruct((T, D), BF16)] + [jax.ShapeDtypeStruct((T, N), BF16)] * 3
        out_specs = [row, tile, tile, tile]
    else:
        out_shape = [jax.ShapeDtypeStruct((T, D), BF16), jax.ShapeDtypeStruct((T, N), F32)]
        out_specs = [row, tile]
    return pl.pallas_call(
        body, name=name, grid=(T // tm, N // tn),
        in_specs=[row, pl.BlockSpec((1, D), lambda i, j: (0, 0))] + [col] * nw,
        out_specs=out_specs, out_shape=out_shape,
        scratch_shapes=[pltpu.VMEM((tm, D), BF16)],
        compiler_params=_params(2),
    )(x, g, *ws)


def _matmul_res(name, a, w, res, scale, tgt=None):
    T, K = a.shape
    N = w.shape[1]
    loss = tgt is not None
    tm = _pick(T, (512, 256, 128))
    tk = _pick(K, (1408, 1024, 512, 256, 128))
    nk = K // tk

    def body(*refs):
        if loss:
            a_ref, w_ref, res_ref, tgt_ref, dx_ref, dxb_ref, lv_ref, acc = refs
        else:
            a_ref, w_ref, res_ref, out_ref, acc = refs
        i, k = pl.program_id(0), pl.program_id(1)

        @pl.when(k == 0)
        def _():
            acc[...] = jnp.zeros_like(acc)

        acc[...] += _nn(a_ref[...], w_ref[...])

        @pl.when(k == nk - 1)
        def _():
            part = jnp.zeros((1, N), F32)
            for r0 in range(0, tm, EPILOGUE_ROWS):
                rows = slice(r0, r0 + min(EPILOGUE_ROWS, tm))
                val = res_ref[rows, :] + scale * acc[rows, :]
                if loss:
                    dv = val - tgt_ref[rows, :]
                    dx = dv * (1.0 / N)
                    dx_ref[rows, :] = dx
                    dxb_ref[rows, :] = dx.astype(BF16)
                    part = part + jnp.sum(dv * dv, axis=0, keepdims=True)
                else:
                    out_ref[rows, :] = val
            if loss:
                @pl.when(i == 0)
                def _():
                    lv_ref[...] = part

                @pl.when(i > 0)
                def _():
                    lv_ref[...] += part

    row = pl.BlockSpec((tm, N), lambda i, k: (i, 0))
    in_specs = [pl.BlockSpec((tm, tk), lambda i, k: (i, k)), pl.BlockSpec((tk, N), lambda i, k: (k, 0)), row]
    args = [a, w, res]
    if loss:
        in_specs.append(row)
        args.append(tgt)
        out_specs = [row, row, pl.BlockSpec((1, N), lambda i, k: (0, 0))]
        out_shape = [jax.ShapeDtypeStruct((T, N), F32), jax.ShapeDtypeStruct((T, N), BF16),
                     jax.ShapeDtypeStruct((1, N), F32)]
    else:
        out_specs = row
        out_shape = jax.ShapeDtypeStruct((T, N), F32)
    return pl.pallas_call(
        body, name=name, grid=(T // tm, nk), in_specs=in_specs, out_specs=out_specs, out_shape=out_shape,
        scratch_shapes=[pltpu.VMEM((tm, N), F32)], compiler_params=_params(2),
    )(*args)


def _nt_matmul(name, dyb, w, scale=1.0, gate=None, up=None):
    T, D = dyb.shape
    N = w.shape[0]
    tm = _pick(T, (ROW_TILE, 512, 256, 128))
    tn = _pick(N, (512, 256, 128))
    swiglu = gate is not None

    def body(*refs):
        if swiglu:
            dy_ref, w_ref, g_ref, u_ref, dg_ref, du_ref = refs
        else:
            dy_ref, w_ref, o_ref, ob_ref = refs
        da = _nt(dy_ref[...], w_ref[...]) * scale
        if swiglu:
            dg_ref[...] = (da * g_ref[...].astype(F32)).astype(BF16)
            du_ref[...] = (da * u_ref[...].astype(F32)).astype(BF16)
        else:
            o_ref[...] = da
            ob_ref[...] = da.astype(BF16)

    tile = pl.BlockSpec((tm, tn), lambda i, j: (i, j))
    in_specs = [pl.BlockSpec((tm, D), lambda i, j: (i, 0)), pl.BlockSpec((tn, D), lambda i, j: (j, 0))]
    args = [dyb, w]
    if swiglu:
        in_specs += [tile, tile]
        args += [gate, up]
        out_shape = [jax.ShapeDtypeStruct((T, N), BF16)] * 2
    else:
        out_shape = [jax.ShapeDtypeStruct((T, N), F32), jax.ShapeDtypeStruct((T, N), BF16)]
    return pl.pallas_call(
        body, name=name, grid=(T // tm, N // tn), in_specs=in_specs, out_specs=[tile, tile],
        out_shape=out_shape, compiler_params=_params(2),
    )(*args)


def _nt_rms_bwd(name, As, Ws, x, g, dres):
    T, K = As[0].shape
    D = x.shape[1]
    tm = _pick(T, (ROW_TILE, 512, 256, 128))
    tk = _pick(K, (512, 256, 128))
    nk = K // tk
    na = len(As)

    def body(*refs):
        a_refs = refs[:na]
        w_refs = refs[na:2 * na]
        x_ref, g_ref, dres_ref, acc, dxb_ref, dg_ref = refs[2 * na:]
        dx_ref = acc
        i, k = pl.program_id(0), pl.program_id(1)

        @pl.when(k == 0)
        def _():
            acc[...] = jnp.zeros_like(acc)

        for c0 in range(0, D, ACC_COLS):
            cols = slice(c0, min(c0 + ACC_COLS, D))
            part = _nt(a_refs[0][...], w_refs[0][cols, :])
            for a_ref, w_ref in zip(a_refs[1:], w_refs[1:]):
                part = part + _nt(a_ref[...], w_ref[cols, :])
            acc[:, cols] += part

        @pl.when(k == nk - 1)
        def _():
            part = jnp.zeros((1, D), F32)
            for r0 in range(0, tm, EPILOGUE_ROWS):
                rows = slice(r0, r0 + min(EPILOGUE_ROWS, tm))
                dh = acc[rows, :]
                xv = x_ref[rows, :]
                r = _rms_r(xv)
                gd = dh * g_ref[...]
                dx = dres_ref[rows, :] + r * gd - xv * (r * r * r) * jnp.mean(gd * xv, axis=-1, keepdims=True)
                dx_ref[rows, :] = dx
                dxb_ref[rows, :] = dx.astype(BF16)
                part = part + jnp.sum(dh * xv * r, axis=0, keepdims=True)

            @pl.when(i == 0)
            def _():
                dg_ref[...] = part

            @pl.when(i > 0)
            def _():
                dg_ref[...] += part

    row = pl.BlockSpec((tm, D), lambda i, k: (i, 0), pipeline_mode=ONCE_PER_ROW_TILE)
    vec = pl.BlockSpec((1, D), lambda i, k: (0, 0))
    return pl.pallas_call(
        body, name=name, grid=(T // tm, nk),
        in_specs=[pl.BlockSpec((tm, tk), lambda i, k: (i, k))] * na
        + [pl.BlockSpec((D, tk), lambda i, k: (0, k))] * na + [row, vec, row],
        out_specs=[row, row, vec],
        out_shape=[jax.ShapeDtypeStruct((T, D), F32), jax.ShapeDtypeStruct((T, D), BF16),
                   jax.ShapeDtypeStruct((1, D), F32)],
        compiler_params=_params(2),
    )(*As, *Ws, x, g, dres)


def _tn_matmul(name, a, b, scale=1.0):
    T, M = a.shape
    N = b.shape[1]
    tn = _pick(N, (2048, 1408, 1280, 1024, 512, 256, 128))
    tm = _pick(M, tuple(c for c in (2048, 1408, 1024, 512, 256, 128) if c * tn <= TN_ACC_ELEMS))
    tk = _pick(T, (1024, 512, 256, 128))
    nk = T // tk

    def body(a_ref, b_ref, o_ref, acc):
        k = pl.program_id(2)

        @pl.when(k == 0)
        def _():
            acc[...] = jnp.zeros_like(acc)

        for r0 in range(0, tm, ACC_COLS):
            rows = slice(r0, min(r0 + ACC_COLS, tm))
            acc[rows, :] += _tn(a_ref[:, rows], b_ref[...])

        @pl.when(k == nk - 1)
        def _():
            o_ref[...] = (acc[...] * scale).astype(BF16)

    return pl.pallas_call(
        body, name=name, grid=(M // tm, N // tn, nk),
        in_specs=[pl.BlockSpec((tk, tm), lambda i, j, k: (k, i)), pl.BlockSpec((tk, tn), lambda i, j, k: (k, j))],
        out_specs=pl.BlockSpec((tm, tn), lambda i, j, k: (i, j)),
        out_shape=jax.ShapeDtypeStruct((M, N), BF16),
        scratch_shapes=[pltpu.VMEM((tm, tn), F32)], compiler_params=_params(3),
    )(a, b)


CONV_ROWS = 128
ROW_CHUNK = 32
LANE_CHUNK = 256


SUBLANES = 8


def _fill_shifts(sh, buf, rows):
    for p in range(1, SUBLANES):
        sh[p, 0:rows - SUBLANES, :] = buf[p:p + rows - SUBLANES, :]


def _tap(buf, sh, s, n, cols):
    p = s % SUBLANES
    return buf[s:s + n, cols] if p == 0 else sh[p, s - p:s - p + n, cols]


def _conv_fwd(name, z, w32, b, lg, lb, C):
    T = z.shape[0]
    tc = CONV_ROWS
    ntap = 31
    lc = _pick(C, (LANE_CHUNK, LANES))
    rpb = tc // HALO

    def body(zc_ref, zp_ref, w_ref, b_ref, lg_ref, lb_ref, yc_ref, ycv_ref, vbuf, ybuf, vsh):
        i = pl.program_id(0)
        zc = zc_ref[...]
        zp = zp_ref[...]
        vbuf[HALO:HALO + tc, :] = zc[:, :C] * _sigmoid(zc[:, C:])
        vbuf[0:HALO, :] = jnp.where(i > 0, zp[:, :C] * _sigmoid(zp[:, C:]), 0.0)
        _fill_shifts(vsh, vbuf, tc + HALO)
        for r0 in range(0, tc, ROW_CHUNK):
            for c0 in range(0, C, lc):
                cols = slice(c0, c0 + lc)
                acc = jnp.zeros((ROW_CHUNK, lc), F32) + b_ref[:, cols]
                for k in range(ntap):
                    acc = acc + w_ref[k:k + 1, cols] * _tap(vbuf, vsh, r0 + 2 + k, ROW_CHUNK, cols)
                ybuf[r0:r0 + ROW_CHUNK, cols] = acc
        y = ybuf[...]
        ycv_ref[...] = y
        mu = jnp.mean(y, axis=-1, keepdims=True)
        yc = y - mu
        rstd = lax.rsqrt(jnp.mean(yc * yc, axis=-1, keepdims=True) + EPS)
        ln = yc * rstd * lg_ref[...] + lb_ref[...]
        yc_ref[...] = (ln * _sigmoid(ln)).astype(BF16)

    vec = pl.BlockSpec((1, C), lambda i: (0, 0))
    return pl.pallas_call(
        body, name=name, grid=(T // tc,),
        in_specs=[pl.BlockSpec((tc, 2 * C), lambda i: (i, 0)),
                  pl.BlockSpec((HALO, 2 * C), lambda i: (jnp.maximum(i * rpb - 1, 0), 0)),
                  pl.BlockSpec((HALO, C), lambda i: (0, 0)), vec, vec, vec],
        out_specs=[pl.BlockSpec((tc, C), lambda i: (i, 0))] * 2,
        out_shape=[jax.ShapeDtypeStruct((T, C), BF16), jax.ShapeDtypeStruct((T, C), F32)],
        scratch_shapes=[pltpu.VMEM((tc + HALO, C), F32), pltpu.VMEM((tc, C), F32),
                        pltpu.VMEM((SUBLANES, tc + HALO, C), F32)],
        compiler_params=_params(1),
    )(z, z, w32, b, lg, lb)


def _conv_bwd(name, z, ycv, dycat, w32, lg, lb, C):
    T = z.shape[0]
    tc = CONV_ROWS
    ntap = 31
    lc = _pick(C, (LANE_CHUNK, LANES))
    rpb = tc // HALO
    nstep = T // tc
    nhb = T // HALO

    def ln_bwd(dyc, y, lgv, lbv):
        mu = jnp.mean(y, axis=-1, keepdims=True)
        yc = y - mu
        rstd = lax.rsqrt(jnp.mean(yc * yc, axis=-1, keepdims=True) + EPS)
        yn = yc * rstd
        ln = yn * lgv + lbv
        sg = _sigmoid(ln)
        dln = dyc * (sg * (1.0 + ln * (1.0 - sg)))
        dyn = dln * lgv
        dy = rstd * (dyn - jnp.mean(dyn, axis=-1, keepdims=True)
                     - yn * jnp.mean(dyn * yn, axis=-1, keepdims=True))
        return dy, dln, yn

    def body(zc_ref, zp_ref, y_ref, yn_ref, d_ref, dn_ref, w_ref, lg_ref, lb_ref,
             dz_ref, dw_ref, db_ref, dlg_ref, dlb_ref, vbuf, dbuf, dvbuf, dwacc, vsh, dsh):
        i = pl.program_id(0)
        lgv, lbv = lg_ref[...], lb_ref[...]
        zc = zc_ref[...]
        zp = zp_ref[...]
        a = zc[:, :C]
        sgt = _sigmoid(zc[:, C:])
        vbuf[HALO:HALO + tc, :] = a * sgt
        vbuf[0:HALO, :] = jnp.where(i > 0, zp[:, :C] * _sigmoid(zp[:, C:]), 0.0)
        dy, dln, yn = ln_bwd(d_ref[...], y_ref[...], lgv, lbv)
        dbuf[0:tc, :] = dy
        dyn_, _, _ = ln_bwd(dn_ref[...], yn_ref[...], lgv, lbv)
        dbuf[tc:tc + HALO, :] = jnp.where(i < nstep - 1, dyn_, 0.0)

        @pl.when(i == 0)
        def _():
            dwacc[...] = jnp.zeros_like(dwacc)
            db_ref[...] = jnp.zeros_like(db_ref)
            dlg_ref[...] = jnp.zeros_like(dlg_ref)
            dlb_ref[...] = jnp.zeros_like(dlb_ref)

        db_ref[...] += jnp.sum(dy, axis=0, keepdims=True)
        dlg_ref[...] += jnp.sum(dln * yn, axis=0, keepdims=True)
        dlb_ref[...] += jnp.sum(dln, axis=0, keepdims=True)

        _fill_shifts(vsh, vbuf, tc + HALO)
        _fill_shifts(dsh, dbuf, tc + HALO)
        for r0 in range(0, tc, ROW_CHUNK):
            for c0 in range(0, C, lc):
                cols = slice(c0, c0 + lc)
                dcur = dbuf[r0:r0 + ROW_CHUNK, cols]
                acc = jnp.zeros((ROW_CHUNK, lc), F32)
                for k in range(ntap):
                    acc = acc + w_ref[k:k + 1, cols] * _tap(dbuf, dsh, r0 + 30 - k, ROW_CHUNK, cols)
                    prod = dcur * _tap(vbuf, vsh, r0 + 2 + k, ROW_CHUNK, cols)
                    red = prod[0:8]
                    for q in range(8, ROW_CHUNK, 8):
                        red = red + prod[q:q + 8]
                    dwacc[8 * k:8 * k + 8, cols] += red
                dvbuf[r0:r0 + ROW_CHUNK, cols] = acc
        dv = dvbuf[...]
        dz_ref[:, :C] = (dv * sgt).astype(BF16)
        dz_ref[:, C:] = (dv * a * sgt * (1.0 - sgt)).astype(BF16)

        @pl.when(i == nstep - 1)
        def _():
            for k in range(ntap):
                dw_ref[k:k + 1, :] = jnp.sum(dwacc[8 * k:8 * k + 8, :], axis=0, keepdims=True)
            dw_ref[ntap:HALO, :] = jnp.zeros((HALO - ntap, C), F32)

    vec = pl.BlockSpec((1, C), lambda i: (0, 0))
    cur = pl.BlockSpec((tc, C), lambda i: (i, 0))
    nxt = pl.BlockSpec((HALO, C), lambda i: (jnp.minimum((i + 1) * rpb, nhb - 1), 0))
    return pl.pallas_call(
        body, name=name, grid=(nstep,),
        in_specs=[pl.BlockSpec((tc, 2 * C), lambda i: (i, 0)),
                  pl.BlockSpec((HALO, 2 * C), lambda i: (jnp.maximum(i * rpb - 1, 0), 0)),
                  cur, nxt, cur, nxt, pl.BlockSpec((HALO, C), lambda i: (0, 0)), vec, vec],
        out_specs=[pl.BlockSpec((tc, 2 * C), lambda i: (i, 0)), pl.BlockSpec((HALO, C), lambda i: (0, 0)),
                   vec, vec, vec],
        out_shape=[jax.ShapeDtypeStruct((T, 2 * C), BF16), jax.ShapeDtypeStruct((HALO, C), F32),
                   jax.ShapeDtypeStruct((1, C), F32), jax.ShapeDtypeStruct((1, C), F32),
                   jax.ShapeDtypeStruct((1, C), F32)],
        scratch_shapes=[pltpu.VMEM((tc + HALO, C), F32), pltpu.VMEM((tc + HALO, C), F32),
                        pltpu.VMEM((tc, C), F32), pltpu.VMEM((8 * HALO, C), F32),
                        pltpu.VMEM((SUBLANES, tc + HALO, C), F32), pltpu.VMEM((SUBLANES, tc + HALO, C), F32)],
        compiler_params=_params(1),
    )(z, z, ycv, ycv, dycat, dycat, w32, lg, lb)


def _seg_sum(u, bmat):
    hi = u.astype(BF16)
    lo = (u - hi.astype(F32)).astype(BF16)
    return _nn(hi, bmat) + _nn(lo, bmat)


def _attn_prep(name, z, gq, gk, bmat, A, c0, hd):
    T = z.shape[0]
    tm = _pick(T, (256, 128))

    def body(zq_ref, zk_ref, zv_ref, gq_ref, gk_ref, b_ref, o_ref):
        bm = b_ref[...]
        for idx, (z_ref, g_ref) in enumerate(((zq_ref, gq_ref), (zk_ref, gk_ref))):
            zv = z_ref[...]
            r = lax.rsqrt(_seg_sum(zv * zv, bm) * (1.0 / hd) + EPS)
            o_ref[:, idx * A:(idx + 1) * A] = zv * r * g_ref[...]
        o_ref[:, 2 * A:] = zv_ref[...]

    vec = pl.BlockSpec((1, A), lambda i: (0, 0))
    return pl.pallas_call(
        body, name=name, grid=(T // tm,),
        in_specs=[pl.BlockSpec((tm, A), lambda i: (i, c0)), pl.BlockSpec((tm, A), lambda i: (i, c0 + 1)),
                  pl.BlockSpec((tm, A), lambda i: (i, c0 + 2)), vec, vec,
                  pl.BlockSpec((A, A), lambda i: (0, 0))],
        out_specs=pl.BlockSpec((tm, 3 * A), lambda i: (i, 0)),
        out_shape=jax.ShapeDtypeStruct((T, 3 * A), F32), compiler_params=_params(1),
    )(z, z, z, gq, gk, bmat)


QK_SCALE = 0.125
ATTN_UNROLL = 4


def _fill_bias(bias, sl_ref, hp, d):
    qi = lax.broadcasted_iota(jnp.int32, (WINDOW, 2 * WINDOW), 0)
    kj = lax.broadcasted_iota(jnp.int32, (WINDOW, 2 * WINDOW), 1)
    dist = WINDOW + qi - kj
    inband = (dist >= 0) & (dist <= WINDOW)
    distf = dist.astype(F32)
    for hh in range(2):
        b = jnp.where(inband, -(sl_ref[2 * hp + hh] * d) * distf, NEG)
        bias[2 * hh + 1] = b
        bias[2 * hh] = jnp.where(kj >= WINDOW, b, NEG)


CHUNK = WINDOW * DILATIONS[-1]


def _deinterleave(dst, src, d, rows, dst_pitch, dst_off, src_off):
    for r in range(d):
        if d == 1:
            val = src[src_off:src_off + rows, :]
        else:
            val = src[pl.ds(src_off + r, rows, stride=d), :]
        lo = r * dst_pitch + dst_off
        dst[lo:lo + rows, :] = val.astype(dst.dtype)


def _interleave_add(dst, start, src, d, rows, src_pitch, src_off):
    for r in range(d):
        lo = r * src_pitch + src_off
        idx = pl.ds(start, rows) if d == 1 else pl.ds(start + r, rows, stride=d)
        dst[idx, :] += src[lo:lo + rows, :]


def _attn_fwd(name, qkv, slopes, A):
    T = qkv.shape[0]
    hpn, nch, nblk = A // LANES, T // CHUNK, CHUNK // WINDOW
    nbranch = len(DILATIONS)

    def body(*refs):
        sl_ref, q_ref, k_ref, kp_ref, v_ref, vp_ref, y_ref, lg_ref, qd, kd, vd, od, ld, bias = refs[:14]
        onat, lnat = refs[14:14 + nbranch], refs[14 + nbranch:]
        hp, ch = pl.program_id(0), pl.program_id(1)
        lane = lax.broadcasted_iota(jnp.int32, (1, LANES), 1)
        first = lane < (LANES // 2)
        for bi, d in enumerate(DILATIONS):
            Ld = CHUNK // d
            seg = Ld + WINDOW
            nbr = Ld // WINDOW
            _deinterleave(qd, q_ref, d, Ld, Ld, 0, 0)
            for dst, cur, prev in ((kd, k_ref, kp_ref), (vd, v_ref, vp_ref)):
                _deinterleave(dst, prev, d, WINDOW, seg, 0, CHUNK - WINDOW * d)
                _deinterleave(dst, cur, d, Ld, seg, WINDOW, 0)
            _fill_bias(bias, sl_ref, hp, d)
            ob, lb = (onat[bi], lnat[bi]) if d == 1 else (od, ld)

            def step(it, carry, Ld=Ld, seg=seg, nbr=nbr, ob=ob, lb=lb):
                r, nl = it // nbr, it % nbr
                q0 = pl.multiple_of(r * Ld + nl * WINDOW, WINDOW)
                k0 = pl.multiple_of(r * seg + nl * WINDOW, WINDOW)
                later = jnp.where(ch * nbr + nl > 0, 1, 0)
                qb = qd[pl.ds(q0, WINDOW), :]
                k2 = kd[pl.ds(k0, 2 * WINDOW), :]
                v2 = vd[pl.ds(k0, 2 * WINDOW), :]
                res = []
                for hh in range(2):
                    mh = first if hh == 0 else jnp.logical_not(first)
                    s = _nt(jnp.where(mh, qb, jnp.zeros_like(qb)), k2) + bias[2 * hh + later]
                    mx = jnp.max(s, axis=-1, keepdims=True)
                    p = jnp.exp(s - mx)
                    den = jnp.sum(p, axis=-1, keepdims=True)
                    res.append((_nn(p.astype(BF16), v2) / den, mx + jnp.log(den)))
                ob[pl.ds(q0, WINDOW), :] = jnp.where(first, res[0][0], res[1][0])
                lb[pl.ds(q0, WINDOW), :] = jnp.where(first, res[0][1], res[1][1])
                return carry

            lax.fori_loop(0, nblk, step, 0, unroll=ATTN_UNROLL)
            if d > 1:
                for r in range(d):
                    onat[bi][pl.ds(r, Ld, stride=d), :] = od[r * Ld:(r + 1) * Ld, :]
                    lnat[bi][pl.ds(r, Ld, stride=d), :] = ld[r * Ld:(r + 1) * Ld, :]
        ls = [l[...] for l in lnat]
        mx = ls[0]
        for v in ls[1:]:
            mx = jnp.maximum(mx, v)
        es = [jnp.exp(v - mx) for v in ls]
        den = es[0]
        for e in es[1:]:
            den = den + e
        out = es[0] * onat[0][...]
        for e, o in zip(es[1:], onat[1:]):
            out = out + e * o[...]
        y_ref[...] = (out / den).astype(BF16)
        lg_ref[...] = mx + jnp.log(den)

    blk = lambda m: pl.BlockSpec((CHUNK, LANES), m)
    cur = lambda which: blk(lambda hp, ch: (ch, which * hpn + hp))
    prev = lambda which: blk(lambda hp, ch: (jnp.maximum(ch - 1, 0), which * hpn + hp))
    omap = blk(lambda hp, ch: (ch, hp))
    f32buf = pltpu.VMEM((CHUNK, LANES), F32)
    return pl.pallas_call(
        body, name=name, grid=(hpn, nch),
        in_specs=[pl.BlockSpec(memory_space=pltpu.SMEM), cur(0), cur(1), prev(1), cur(2), prev(2)],
        out_specs=[omap, omap],
        out_shape=[jax.ShapeDtypeStruct((T, A), BF16), jax.ShapeDtypeStruct((T, A), F32)],
        scratch_shapes=[pltpu.VMEM((CHUNK, LANES), BF16), pltpu.VMEM((2 * CHUNK, LANES), BF16),
                        pltpu.VMEM((2 * CHUNK, LANES), BF16), f32buf, f32buf,
                        pltpu.VMEM((4, WINDOW, 2 * WINDOW), F32)] + [f32buf] * (2 * nbranch),
        compiler_params=_params(2),
    )(slopes, qkv, qkv, qkv, qkv, qkv)


def _attn_bwd(name, qkv, dycat, yatt, lg, slopes, A, catoff):
    T = qkv.shape[0]
    hpn, nch, nblk = A // LANES, T // CHUNK, CHUNK // WINDOW
    co = catoff // LANES

    def body(sl_ref, q_ref, k_ref, kp_ref, v_ref, vp_ref, do_ref, o_ref, l_ref, dq_ref, dk_ref, dv_ref,
             qd, kd, vd, dod, ddn, ddd, ldd, dqd, dkd, dvd, bias):
        hp, ch = pl.program_id(0), pl.program_id(1)
        lane = lax.broadcasted_iota(jnp.int32, (1, LANES), 1)
        first = lane < (LANES // 2)
        ddn[...] = do_ref[...] * o_ref[...].astype(F32)
        dq_ref[...] = jnp.zeros_like(dq_ref)

        @pl.when(ch == 0)
        def _():
            dk_ref[...] = jnp.zeros_like(dk_ref)
            dv_ref[...] = jnp.zeros_like(dv_ref)

        base = ch * CHUNK
        for d in DILATIONS:
            Ld = CHUNK // d
            seg = Ld + WINDOW
            nbr = Ld // WINDOW
            _deinterleave(qd, q_ref, d, Ld, Ld, 0, 0)
            _deinterleave(dod, do_ref, d, Ld, Ld, 0, 0)
            _deinterleave(ddd, ddn, d, Ld, Ld, 0, 0)
            _deinterleave(ldd, l_ref, d, Ld, Ld, 0, 0)
            for dst, cur, prev in ((kd, k_ref, kp_ref), (vd, v_ref, vp_ref)):
                _deinterleave(dst, prev, d, WINDOW, seg, 0, CHUNK - WINDOW * d)
                _deinterleave(dst, cur, d, Ld, seg, WINDOW, 0)
            dkd[0:d * seg, :] = jnp.zeros((d * seg, LANES), F32)
            dvd[0:d * seg, :] = jnp.zeros((d * seg, LANES), F32)
            _fill_bias(bias, sl_ref, hp, d)

            def step(it, carry, Ld=Ld, seg=seg, nbr=nbr):
                r, nl = it // nbr, it % nbr
                q0 = pl.multiple_of(r * Ld + nl * WINDOW, WINDOW)
                k0 = pl.multiple_of(r * seg + nl * WINDOW, WINDOW)
                later = jnp.where(ch * nbr + nl > 0, 1, 0)
                qb = qd[pl.ds(q0, WINDOW), :]
                k2 = kd[pl.ds(k0, 2 * WINDOW), :]
                v2 = vd[pl.ds(k0, 2 * WINDOW), :]
                dob = dod[pl.ds(q0, WINDOW), :]
                dd = ddd[pl.ds(q0, WINDOW), :]
                lb = ldd[pl.ds(q0, WINDOW), :]
                dk2 = jnp.zeros((2 * WINDOW, LANES), F32)
                dv2 = jnp.zeros((2 * WINDOW, LANES), F32)
                dqs = []
                for hh in range(2):
                    mh = first if hh == 0 else jnp.logical_not(first)
                    qh = jnp.where(mh, qb, jnp.zeros_like(qb))
                    doh = jnp.where(mh, dob, jnp.zeros_like(dob))
                    lcol = lb[:, hh * (LANES // 2):hh * (LANES // 2) + 1]
                    p = jnp.exp(_nt(qh, k2) + bias[2 * hh + later] - lcol)
                    dcol = jnp.sum(jnp.where(mh, dd, 0.0), axis=-1, keepdims=True)
                    ds = (p * (_nt(doh, v2) - dcol)).astype(BF16)
                    dqs.append(_nn(ds, k2))
                    dk2 = dk2 + _tn(ds, qh)
                    dv2 = dv2 + _tn(p.astype(BF16), doh)
                dqd[pl.ds(q0, WINDOW), :] = jnp.where(first, dqs[0], dqs[1])
                dkd[pl.ds(k0, 2 * WINDOW), :] += dk2
                dvd[pl.ds(k0, 2 * WINDOW), :] += dv2
                return carry

            lax.fori_loop(0, nblk, step, 0, unroll=ATTN_UNROLL)
            _interleave_add(dq_ref, 0, dqd, d, Ld, Ld, 0)
            for acc, out in ((dkd, dk_ref), (dvd, dv_ref)):
                _interleave_add(out, base, acc, d, Ld, seg, WINDOW)

                @pl.when(ch > 0)
                def _(acc=acc, out=out, d=d, seg=seg):
                    _interleave_add(out, base - WINDOW * d, acc, d, WINDOW, seg, 0)

    blk = lambda m: pl.BlockSpec((CHUNK, LANES), m)
    cur = lambda which: blk(lambda hp, ch: (ch, which * hpn + hp))
    prev = lambda which: blk(lambda hp, ch: (jnp.maximum(ch - 1, 0), which * hpn + hp))
    omap = blk(lambda hp, ch: (ch, hp))
    full = pl.BlockSpec((T, LANES), lambda hp, ch: (0, hp))
    f32buf = pltpu.VMEM((CHUNK, LANES), F32)
    bf16buf = pltpu.VMEM((CHUNK, LANES), BF16)
    return pl.pallas_call(
        body, name=name, grid=(hpn, nch),
        in_specs=[pl.BlockSpec(memory_space=pltpu.SMEM), cur(0), cur(1), prev(1), cur(2), prev(2),
                  blk(lambda hp, ch: (ch, co + hp)), omap, omap],
        out_specs=[omap, full, full],
        out_shape=[jax.ShapeDtypeStruct((T, A), F32)] * 3,
        scratch_shapes=[bf16buf, pltpu.VMEM((2 * CHUNK, LANES), BF16), pltpu.VMEM((2 * CHUNK, LANES), BF16),
                        bf16buf, f32buf, f32buf, f32buf, f32buf,
                        pltpu.VMEM((2 * CHUNK, LANES), F32), pltpu.VMEM((2 * CHUNK, LANES), F32),
                        pltpu.VMEM((4, WINDOW, 2 * WINDOW), F32)],
        compiler_params=_params(2),
    )(slopes, qkv, qkv, qkv, qkv, qkv, dycat, yatt, lg)


def _attn_bwd_combine(name, dqs, dks, dvs, z, gq, gk, bmat, fmat, A, c0, hd):
    T = z.shape[0]
    tm = _pick(T, (256, 128))
    nbr = len(dqs)

    def body(*refs):
        dq_refs, dk_refs, dv_refs = refs[:nbr], refs[nbr:2 * nbr], refs[2 * nbr:3 * nbr]
        zq_ref, zk_ref, gq_ref, gk_ref, b_ref, f_ref, dz_ref, dgq_ref, dgk_ref = refs[3 * nbr:]
        i = pl.program_id(0)
        bm = b_ref[...]

        def tot(rs):
            t = rs[0][...]
            for r in rs[1:]:
                t = t + r[...]
            return t

        for idx, (d_refs, z_ref, g_ref, dg_ref, gscale) in enumerate(
                ((dq_refs, zq_ref, gq_ref, dgq_ref, QK_SCALE), (dk_refs, zk_ref, gk_ref, dgk_ref, 1.0))):
            dy = tot(d_refs)
            zv = z_ref[...]
            r = lax.rsqrt(_seg_sum(zv * zv, bm) * (1.0 / hd) + EPS)
            gd = dy * g_ref[...]
            mean = _seg_sum(gd * zv, bm) * (1.0 / hd)
            dz_ref[:, idx * A:(idx + 1) * A] = (r * gd - zv * (r * r * r) * mean).astype(BF16)
            part = jnp.sum(dy * zv * r, axis=0, keepdims=True) * gscale

            @pl.when(i == 0)
            def _():
                dg_ref[...] = part

            @pl.when(i > 0)
            def _():
                dg_ref[...] += part

        dz_ref[:, 2 * A:] = tot(dv_refs).astype(BF16)

        @pl.when(i == T // tm - 1)
        def _():
            fm = f_ref[...]
            for dg_ref in (dgq_ref, dgk_ref):
                v = jnp.broadcast_to(dg_ref[...], (8, A))
                hi = v.astype(BF16)
                mid = (v - hi.astype(F32)).astype(BF16)
                lo = (v - hi.astype(F32) - mid.astype(F32)).astype(BF16)
                dg_ref[...] = (_nn(hi, fm) + _nn(mid, fm) + _nn(lo, fm))[0:1]

    blk = pl.BlockSpec((tm, A), lambda i: (i, 0))
    vec = pl.BlockSpec((1, A), lambda i: (0, 0))
    return pl.pallas_call(
        body, name=name, grid=(T // tm,),
        in_specs=[blk] * (3 * nbr) + [pl.BlockSpec((tm, A), lambda i: (i, c0)),
                                      pl.BlockSpec((tm, A), lambda i: (i, c0 + 1)), vec, vec,
                                      pl.BlockSpec((A, A), lambda i: (0, 0)),
                                      pl.BlockSpec((A, A), lambda i: (0, 0))],
        out_specs=[pl.BlockSpec((tm, 3 * A), lambda i: (i, 0)), vec, vec],
        out_shape=[jax.ShapeDtypeStruct((T, 3 * A), BF16), jax.ShapeDtypeStruct((1, A), F32),
                   jax.ShapeDtypeStruct((1, A), F32)],
        compiler_params=_params(1),
    )(*dqs, *dks, *dvs, z, z, gq, gk, bmat, fmat)


def _local_step(x, tgt, S, comm, hd):
    T, D = x.shape
    C = S["conv_b_dw"].shape[1]
    A = C
    H = A // hd
    Dmix = C + A
    c0 = (2 * C) // A
    slopes = 2.0 ** (-ALIBI_MAX_BIAS * jnp.arange(1, H + 1, dtype=F32) / H)
    seg = jnp.arange(A) // hd
    bmat = (seg[:, None] == seg[None, :]).astype(BF16)
    pos_in_head = jnp.arange(A) % hd
    fmat = (pos_in_head[:, None] == pos_in_head[None, :]).astype(BF16)
    gq = jnp.tile(S["q_norm_g"], (1, H)) * QK_SCALE
    gk = jnp.tile(S["k_norm_g"], (1, H))

    wg1, wu1 = comm.weights(("ffn1_w_gate", "ffn1_w_up"), None)
    h1, gate1, up1, a1 = _norm_matmul("ffn1_up", x, comm.tie(S["ffn1_norm_g"]), [wg1, wu1], True)
    wd1, win, w32 = comm.weights(("ffn1_w_down", "w_in", "conv_w32"), a1)
    x1 = _matmul_res("ffn1_down", a1, wd1, x, 0.5)
    h2, z = _norm_matmul("mix_in", x1, S["mix_norm_g"], [win], False)
    yc, ycv = _conv_fwd("conv_fwd", z, w32, S["conv_b_dw"], S["conv_ln_g"], S["conv_ln_b"], C)
    qkv = _attn_prep("attn_prep", z, gq, gk, bmat, A, c0, hd)
    yatt, lg = _attn_fwd("attn_fwd", qkv, slopes, A)
    ycat = jnp.concatenate([yc, yatt], axis=1)
    wout, wg2, wu2, wd2 = comm.weights(("w_out", "ffn2_w_gate", "ffn2_w_up", "ffn2_w_down"), yatt)
    x2 = _matmul_res("mix_out", ycat, wout, x1, 1.0)
    h3, gate2, up2, a2 = _norm_matmul("ffn2_up", x2, S["ffn2_norm_g"], [wg2, wu2], True)
    dx3, dx3b, lossvec = _matmul_res("ffn2_down_loss", a2, wd2, x2, 0.5, tgt=tgt)

    G = {}
    dgate2, dup2 = _nt_matmul("ffn2_dact", dx3b, wd2, 0.5, gate2, up2)
    comm.reduce_begin("ffn2", {"ffn2_w_down": _tn_matmul("ffn2_dwd", a2, dx3b, 0.5),
                               "ffn2_w_gate": _tn_matmul("ffn2_dwg", h3, dgate2),
                               "ffn2_w_up": _tn_matmul("ffn2_dwu", h3, dup2)})
    dx2, dx2b, G["ffn2_norm_g"] = _nt_rms_bwd("ffn2_dx", [dgate2, dup2], [wg2, wu2],
                                              x2, comm.tie(S["ffn2_norm_g"]), dx3)
    dwout = _tn_matmul("mix_dwout", ycat, dx2b)
    dycat, _ = _nt_matmul("mix_dycat", dx2b, wout)
    dzc, G["conv_w32"], G["conv_b_dw"], G["conv_ln_g"], G["conv_ln_b"] = _conv_bwd(
        "conv_bwd", z, ycv, dycat, w32, S["conv_ln_g"], S["conv_ln_b"], C)
    dq, dk, dv = _attn_bwd("attn_bwd", qkv, dycat, yatt, lg, slopes, A, C)
    dzqkv, G["q_norm_g"], G["k_norm_g"] = _attn_bwd_combine(
        "attn_bwd_combine", [dq], [dk], [dv], z, gq, gk, bmat, fmat, A, c0, hd)
    comm.reduce_end("ffn2", dzqkv)
    dz = jnp.concatenate([dzc, dzqkv], axis=1)
    comm.reduce_begin("mix", {"w_out": dwout, "w_in": _tn_matmul("mix_dwin", h2, dz)})
    dx1, dx1b, G["mix_norm_g"] = _nt_rms_bwd("mix_dx", [dz], [win], x1, comm.tie(S["mix_norm_g"]), dx2)
    dgate1, dup1 = _nt_matmul("ffn1_dact", dx1b, wd1, 0.5, gate1, up1)
    dwd1 = _tn_matmul("ffn1_dwd", a1, dx1b, 0.5)
    dwg1 = _tn_matmul("ffn1_dwg", h1, dgate1)
    dwu1 = _tn_matmul("ffn1_dwu", h1, dup1)
    comm.reduce_end("mix", dwu1)
    comm.reduce_begin("ffn1", {"ffn1_w_down": dwd1, "ffn1_w_gate": dwg1, "ffn1_w_up": dwu1})
    dx0, _, G["ffn1_norm_g"] = _nt_rms_bwd("ffn1_dx", [dgate1, dup1], [wg1, wu1],
                                           x, comm.tie(S["ffn1_norm_g"]), dx1)
    comm.reduce_end("ffn1", dx0)
    return lossvec, dx0, G


BIG = (("ffn1_w_gate", 1), ("ffn1_w_up", 1), ("ffn1_w_down", 0), ("w_in", 1), ("w_out", 0),
       ("ffn2_w_gate", 1), ("ffn2_w_up", 1), ("ffn2_w_down", 0))
AXIS = dict(BIG)
FLIPS = ((1, 0), (0, 1), (1, 1))
HBM = pl.BlockSpec(memory_space=pltpu.HBM)
SEM = pl.BlockSpec(memory_space=pltpu.SEMAPHORE)
EFFECT = pltpu.SideEffectType.DATAFLOW_SIDE_EFFECTING
TOKEN = jax.ShapeDtypeStruct((8, LANES), F32)


def _window(ref, shape, axis, slab=None, half=None):
    idx = [pl.ds(0, shape[0]), pl.ds(0, shape[1])]
    if slab is not None:
        n = shape[axis] // 4
        idx[axis] = pl.ds(pl.multiple_of(slab * n, 8), n)
    if half is not None:
        hs = shape[1 - axis] // 2
        idx[1 - axis] = pl.ds(pl.multiple_of(half * hs, 8), hs)
    return ref.at[idx[0], idx[1]]


def _position():
    return lax.axis_index("x"), lax.axis_index("y"), lax.axis_index("c")


def _half_shape(shape, axis):
    return (shape[0] // 2, shape[1]) if axis == 1 else (shape[0], shape[1] // 2)


def _slab_shape(shape, axis):
    return (shape[0], shape[1] // 4) if axis == 1 else (shape[0] // 4, shape[1])


def _piece_shape(shape, axis):
    return _half_shape(_slab_shape(shape, axis), axis)


def _full_shape(shard, axis):
    return (shard.shape[0], shard.shape[1] * 4) if axis == 1 else (shard.shape[0] * 4, shard.shape[1])


def _hbm(a):
    return pltpu.with_memory_space_constraint(a, pltpu.HBM)


def _remote(src, dst, send_sem, recv_sem, to):
    return pltpu.make_async_remote_copy(src_ref=src, dst_ref=dst, send_sem=send_sem, recv_sem=recv_sem,
                                        device_id=to, device_id_type=MESH)


def _place(name, pos, w, axis):
    R, Cc = w.shape
    tr = _pick(R, (256, 128, 64, 32, 16))
    nrb = R // tr

    def body(pos_ref, w_ref, o_ref):
        o_ref[...] = w_ref[...].astype(BF16)

    omap = (lambda i, p: (i, p[0])) if axis == 1 else (lambda i, p: (p[0] * nrb + i, 0))
    return pl.pallas_call(
        body, name=name,
        grid_spec=pltpu.PrefetchScalarGridSpec(
            num_scalar_prefetch=1, grid=(nrb,), in_specs=[pl.BlockSpec((tr, Cc), lambda i, p: (i, 0))],
            out_specs=pl.BlockSpec((tr, Cc), omap)),
        out_shape=jax.ShapeDtypeStruct(_full_shape(w, axis), BF16), compiler_params=_params(1),
    )(pos, w)


def _gather_now(name, axes, fulls):
    nt = len(fulls)
    shapes = [f.shape for f in fulls]

    def body(*refs):
        outs, token = refs[nt:2 * nt], refs[2 * nt]
        send_sems, recv_sems = refs[2 * nt + 1:]
        x, y, c = _position()
        j0 = 2 * x + y
        sib = (x, y, 1 - c)

        def copy(t, k, slab, half, to):
            win = _window(outs[t], shapes[t], axes[t], slab=slab, half=half)
            return _remote(win, win, send_sems.at[t, k], recv_sems.at[t, k], to)

        sends = []
        for k, (fx, fy) in enumerate(FLIPS):
            for t in range(nt):
                cp = copy(t, k, j0, c, (x ^ fx, y ^ fy, c))
                cp.start()
                sends.append(cp)
        for k, (fx, fy) in enumerate(FLIPS):
            js = 2 * (x ^ fx) + (y ^ fy)
            for t in range(nt):
                copy(t, k, js, c, sib).wait_recv()
                cp = copy(t, 3 + k, js, c, sib)
                cp.start()
                sends.append(cp)
        for k, (fx, fy) in enumerate(FLIPS):
            js = 2 * (x ^ fx) + (y ^ fy)
            for t in range(nt):
                copy(t, 3 + k, js, 1 - c, sib).wait_recv()
        for cp in sends:
            cp.wait_send()
        token[...] = jnp.zeros_like(token)

    res = pl.pallas_call(
        body, name=name, in_specs=[ANY] * nt,
        out_specs=[ANY] * nt + [pl.BlockSpec(memory_space=pltpu.VMEM)],
        out_shape=[jax.ShapeDtypeStruct(s, BF16) for s in shapes] + [TOKEN],
        input_output_aliases={t: t for t in range(nt)},
        scratch_shapes=[pltpu.SemaphoreType.DMA((nt, 6)), pltpu.SemaphoreType.DMA((nt, 6))],
    )(*fulls)
    return list(res[:nt]), res[nt]


def _split_start(name, arrays, ncopies, plan):
    na = len(arrays)

    def body(*refs):
        ins = refs[:na]
        send_sems, recv_sems = refs[na], refs[na + 1]
        token = refs[-1]
        x, y, c = _position()
        for i, (src, dst, to) in enumerate(plan(ins, x, y, c)):
            _remote(src, dst, send_sems.at[i], recv_sems.at[i], to).start()
        token[...] = jnp.zeros_like(token)

    res = pl.pallas_call(
        body, name=name, in_specs=[HBM] * na,
        out_specs=tuple([SEM, SEM] + [HBM] * na + [pl.BlockSpec(memory_space=pltpu.VMEM)]),
        out_shape=tuple([pltpu.SemaphoreType.DMA((ncopies,)), pltpu.SemaphoreType.DMA((ncopies,))]
                        + [pltpu.HBM(a.shape, a.dtype) for a in arrays] + [TOKEN]),
        input_output_aliases={i: 2 + i for i in range(na)},
        compiler_params=pltpu.CompilerParams(has_side_effects=EFFECT),
    )(*[_hbm(a) for a in arrays])
    return (res[0], res[1]), list(res[2:2 + na]), res[-1]


def _split_wait(name, arrays, sems, after, plan):
    na = len(arrays)

    def body(*refs):
        ins = refs[:na]
        send_sems, recv_sems = refs[na], refs[na + 1]
        x, y, c = _position()
        for i, (src, dst, to) in enumerate(plan(ins, x, y, c)):
            cp = _remote(src, dst, send_sems.at[i], recv_sems.at[i], to)
            cp.wait_send()
            cp.wait_recv()

    res = pl.pallas_call(
        body, name=name, in_specs=[HBM] * na + [SEM, SEM, ANY],
        out_specs=tuple([HBM] * na), out_shape=tuple(pltpu.HBM(a.shape, a.dtype) for a in arrays),
        input_output_aliases={i: i for i in range(na)},
        compiler_params=pltpu.CompilerParams(has_side_effects=EFFECT),
    )(*arrays, *sems, after)
    return list(res)


def _gather_plan(axes, shapes, conv_shape):
    nt = len(axes)

    def plan(refs, x, y, c):
        j0 = 2 * x + y
        out = []
        for fx, fy in FLIPS:
            to = (x ^ fx, y ^ fy, c)
            for t in range(nt):
                win = _window(refs[t], shapes[t], axes[t], slab=j0, half=c)
                out.append((win, win, to))
            if conv_shape is not None:
                win = _window(refs[nt], conv_shape, 1, slab=j0)
                out.append((win, win, to))
        return out

    return plan


def _gather_finish(name, axes, fulls):
    nt = len(axes)
    shapes = [f.shape for f in fulls]

    def body(*refs):
        outs = refs[nt:2 * nt]
        send_sems, recv_sems = refs[2 * nt:]
        x, y, c = _position()
        sib = (x, y, 1 - c)
        cps = []
        for k, (fx, fy) in enumerate(FLIPS):
            js = 2 * (x ^ fx) + (y ^ fy)
            for t in range(nt):
                landed = _window(outs[t], shapes[t], axes[t], slab=js, half=c)
                cp = _remote(landed, landed, send_sems.at[t, k], recv_sems.at[t, k], sib)
                cp.start()
                cps.append(cp)
        for k, (fx, fy) in enumerate(FLIPS):
            js = 2 * (x ^ fx) + (y ^ fy)
            for t in range(nt):
                other = _window(outs[t], shapes[t], axes[t], slab=js, half=1 - c)
                _remote(other, other, send_sems.at[t, k], recv_sems.at[t, k], sib).wait_recv()
        for cp in cps:
            cp.wait_send()

    res = pl.pallas_call(
        body, name=name, in_specs=[ANY] * nt, out_specs=[ANY] * nt,
        out_shape=[jax.ShapeDtypeStruct(f.shape, f.dtype) for f in fulls],
        input_output_aliases={t: t for t in range(nt)},
        scratch_shapes=[pltpu.SemaphoreType.DMA((nt, 3)), pltpu.SemaphoreType.DMA((nt, 3))],
    )(*fulls)
    return list(res)


def _pair_exchange(name, srcs, windows, out_shapes, dtype):
    nt = len(srcs)

    def body(*refs):
        ins, outs = refs[:nt], refs[nt:2 * nt]
        send_sems, recv_sems = refs[2 * nt:]
        x, y, c = _position()
        cps = []
        for t in range(nt):
            cp = _remote(windows[t](ins[t], c), outs[t], send_sems.at[t], recv_sems.at[t], (x, y, 1 - c))
            cp.start()
            cps.append(cp)
        for cp in cps:
            cp.wait()

    return pl.pallas_call(
        body, name=name, in_specs=[ANY] * nt, out_specs=[ANY] * nt,
        out_shape=[jax.ShapeDtypeStruct(s, dtype) for s in out_shapes],
        scratch_shapes=[pltpu.SemaphoreType.DMA((nt,)), pltpu.SemaphoreType.DMA((nt,))],
    )(*srcs)


def _scatter_plan(axes, shapes):
    nt = len(axes)

    def plan(refs, x, y, c):
        out = []
        for k, (fx, fy) in enumerate(FLIPS):
            js = 2 * (x ^ fx) + (y ^ fy)
            for t in range(nt):
                src = _window(refs[t], _half_shape(shapes[t], axes[t]), axes[t], slab=js)
                out.append((src, refs[nt + t].at[k], (x ^ fx, y ^ fy, c)))
        return out

    return plan


def _gather_small(packed):
    R, Cc = packed.shape

    def body(p_ref, o_ref, send_sems, recv_sems, loc_sem):
        x, y, c = _position()
        me = 4 * x + 2 * y + c
        mine = pltpu.make_async_copy(p_ref, o_ref.at[me], loc_sem)
        mine.start()
        cps = []
        for k in range(1, 8):
            fx, fy, fc = (k >> 2) & 1, (k >> 1) & 1, k & 1
            cp = pltpu.make_async_remote_copy(
                src_ref=p_ref, dst_ref=o_ref.at[me], send_sem=send_sems.at[k - 1], recv_sem=recv_sems.at[k - 1],
                device_id=(x ^ fx, y ^ fy, c ^ fc), device_id_type=MESH)
            cp.start()
            cps.append(cp)
        for cp in cps:
            cp.wait()
        mine.wait()

    return pl.pallas_call(
        body, name="gather_small_grads", in_specs=[ANY], out_specs=ANY,
        out_shape=jax.ShapeDtypeStruct((8, R, Cc), F32),
        scratch_shapes=[pltpu.SemaphoreType.DMA((7,)), pltpu.SemaphoreType.DMA((7,)), pltpu.SemaphoreType.DMA],
    )(packed)


def _sum_slots(name, slots):
    n, R, Cc = slots.shape

    def body(s_ref, o_ref):
        t = s_ref[0]
        for i in range(1, n):
            t = t + s_ref[i]
        o_ref[...] = t

    return pl.pallas_call(
        body, name=name, grid=(1,), in_specs=[pl.BlockSpec((n, R, Cc), lambda i: (0, 0, 0))],
        out_specs=pl.BlockSpec((R, Cc), lambda i: (0, 0)), out_shape=jax.ShapeDtypeStruct((R, Cc), F32),
        compiler_params=_params(1),
    )(slots)


def _pair_sum(name, pos, g, land, shape, axis):
    hshape = _half_shape(shape, axis)
    R, Cc = hshape
    tr = _pick(R, (256, 128, 64, 32, 16))
    nrb = R // tr

    def body(pos_ref, g_ref, l_ref, o_ref):
        o_ref[...] = (g_ref[...].astype(F32) + l_ref[...].astype(F32)).astype(BF16)

    if axis == 1:
        gmap = lambda i, p: (p[1] * nrb + i, 0)
    else:
        gmap = lambda i, p: (i, p[1])
    blk = pl.BlockSpec((tr, Cc), lambda i, p: (i, 0))
    return pl.pallas_call(
        body, name=name,
        grid_spec=pltpu.PrefetchScalarGridSpec(
            num_scalar_prefetch=1, grid=(nrb,), in_specs=[pl.BlockSpec((tr, Cc), gmap), blk], out_specs=blk),
        out_shape=jax.ShapeDtypeStruct(hshape, BF16), compiler_params=_params(1),
    )(pos, g, land)


def _chip_sum(name, pos, sb, land, shape, axis):
    hshape = _half_shape(shape, axis)
    pshape = _piece_shape(shape, axis)
    R, Cc = pshape
    tr = _pick(R, (256, 128, 64, 32, 16))
    nrb = R // tr

    def body(pos_ref, s_ref, l_ref, o_ref):
        t = s_ref[...].astype(F32)
        for k in range(3):
            t = t + l_ref[k].astype(F32)
        o_ref[...] = t

    if axis == 1:
        smap = lambda i, p: (i, p[0])
    else:
        smap = lambda i, p: (p[0] * nrb + i, 0)
    return pl.pallas_call(
        body, name=name,
        grid_spec=pltpu.PrefetchScalarGridSpec(
            num_scalar_prefetch=1, grid=(nrb,),
            in_specs=[pl.BlockSpec((tr, Cc), smap), pl.BlockSpec((3, tr, Cc), lambda i, p: (0, i, 0))],
            out_specs=pl.BlockSpec((tr, Cc), lambda i, p: (i, 0))),
        out_shape=jax.ShapeDtypeStruct(pshape, F32), compiler_params=_params(1),
    )(pos, sb, land)


def _adam_math(w, g, m, v):
    m = ADAM_B1 * m + (1.0 - ADAM_B1) * g
    v = ADAM_B2 * v + (1.0 - ADAM_B2) * (g * g)
    m_hat = m / (1.0 - ADAM_B1 ** ADAM_STEP)
    v_hat = v / (1.0 - ADAM_B2 ** ADAM_STEP)
    delta = -ADAM_LR * (m_hat / (jnp.sqrt(v_hat) + ADAM_EPS) + ADAM_WD * w)
    return delta, m, v


def _adamw_halves(name, pos, w, m, v, mine, theirs, axis):
    R, Cc = w.shape
    hr, hc = mine.shape
    tr = _pick(hr, (256, 128, 64, 32, 16))
    nrb = hr // tr

    def body(pos_ref, w_ref, m_ref, v_ref, a_ref, b_ref, g_ref, d_ref, nm_ref, nv_ref):
        half = pl.program_id(0)
        g = jnp.where(half == pos_ref[1], a_ref[...], b_ref[...])
        d, nm, nv = _adam_math(w_ref[...], g, m_ref[...], v_ref[...])
        g_ref[...] = g
        d_ref[...] = d
        nm_ref[...] = nm
        nv_ref[...] = nv

    if axis == 1:
        wmap = lambda h, i, p: (h * nrb + i, 0)
    else:
        wmap = lambda h, i, p: (i, h)
    wblk = pl.BlockSpec((tr, hc), wmap)
    ablk = pl.BlockSpec((tr, hc), lambda h, i, p: (jnp.where(h == p[1], i, 0), 0))
    bblk = pl.BlockSpec((tr, hc), lambda h, i, p: (jnp.where(h == p[1], 0, i), 0))
    return pl.pallas_call(
        body, name=name,
        grid_spec=pltpu.PrefetchScalarGridSpec(
            num_scalar_prefetch=1, grid=(2, nrb), in_specs=[wblk, wblk, wblk, ablk, bblk], out_specs=[wblk] * 4),
        out_shape=[jax.ShapeDtypeStruct((R, Cc), F32)] * 4, compiler_params=_params(2),
    )(pos, w, m, v, mine, theirs)


def _adamw_small(name, w, g, m, v):
    def body(w_ref, g_ref, m_ref, v_ref, d_ref, nm_ref, nv_ref):
        d, nm, nv = _adam_math(w_ref[...], g_ref[...], m_ref[...], v_ref[...])
        d_ref[...] = d
        nm_ref[...] = nm
        nv_ref[...] = nv

    blk = pl.BlockSpec(w.shape, lambda i: (0, 0))
    return pl.pallas_call(
        body, name=name, grid=(1,), in_specs=[blk] * 4, out_specs=[blk] * 3,
        out_shape=[jax.ShapeDtypeStruct(w.shape, F32)] * 3, compiler_params=_params(1),
    )(w, g, m, v)


SMALL = ("ffn1_norm_g", "mix_norm_g", "conv_b_dw", "conv_ln_g", "conv_ln_b", "q_norm_g", "k_norm_g", "ffn2_norm_g")
ORDER = ("ffn1_norm_g", "ffn1_w_gate", "ffn1_w_up", "ffn1_w_down", "mix_norm_g", "w_in", "conv_w_dw", "conv_b_dw",
         "conv_ln_g", "conv_ln_b", "q_norm_g", "k_norm_g", "w_out", "ffn2_norm_g", "ffn2_w_gate", "ffn2_w_up",
         "ffn2_w_down")
GATHER_FIRST = ("ffn1_w_gate", "ffn1_w_up")
GATHER_SECOND = ("ffn1_w_down", "w_in")
GATHER_THIRD = ("w_out", "ffn2_w_gate", "ffn2_w_up", "ffn2_w_down")


class _Exchange:
    def __init__(self, P, Mo, Vo, conv_shard, pos):
        self.P, self.Mo, self.Vo, self.pos = P, Mo, Vo, pos
        self.tokens = []
        self.pending = {}
        self.reducing = {}
        self.results = {}
        placed = {n: _place("place_" + n, pos, P[n][0], a) for n, a in BIG}
        self.shapes = {n: placed[n].shape for n, _ in BIG}
        first, tok = _gather_now("gather_first", [AXIS[n] for n in GATHER_FIRST], [placed[n] for n in GATHER_FIRST])
        self.ready = dict(zip(GATHER_FIRST, first))
        cq = conv_shard.shape[1]
        conv_full = lax.dynamic_update_slice(jnp.zeros((conv_shard.shape[0], 4 * cq), F32), conv_shard,
                                             (0, pos[0] * cq))
        for gname, names, conv in (("second", GATHER_SECOND, conv_full), ("third", GATHER_THIRD, None)):
            axes = [AXIS[n] for n in names]
            shapes = [self.shapes[n] for n in names]
            arrays = [placed[n] for n in names] + ([conv] if conv is not None else [])
            small = min(range(len(arrays)), key=lambda i: arrays[i].size)
            arrays[small] = arrays[small] + tok[0, 0].astype(arrays[small].dtype)
            plan = _gather_plan(axes, shapes, conv.shape if conv is not None else None)
            sems, thru, tok = _split_start("gather_%s_start" % gname, arrays, 3 * len(arrays), plan)
            self.tokens.append(tok)
            for n in names + (("conv_w32",) if conv is not None else ()):
                self.pending[n] = (gname, names, axes, plan, sems, thru, conv is not None)

    def tie(self, v):
        for tok in self.tokens:
            v = v + tok[0:1, 0:1]
        self.tokens = []
        return v

    def weights(self, names, after):
        if names[0] in self.pending:
            gname, gnames, axes, plan, sems, thru, has_conv = self.pending[names[0]]
            thru = _split_wait("gather_%s_wait" % gname, thru, sems, after, plan)
            nt = len(gnames)
            fulls = _gather_finish("gather_%s_finish" % gname, axes, thru[:nt])
            for n, f in zip(gnames, fulls):
                self.ready[n] = f
                del self.pending[n]
            if has_conv:
                self.ready["conv_w32"] = thru[nt]
                del self.pending["conv_w32"]
        return [self.ready[n] for n in names]

    def reduce_begin(self, gname, grads):
        names = list(grads)
        axes = [AXIS[n] for n in names]
        shapes = [self.shapes[n] for n in names]
        gs = [grads[n] for n in names]
        to_sibling = [(lambda ref, c, s=s, a=a: _window(ref, s, a, half=1 - c)) for s, a in zip(shapes, axes)]
        landed = _pair_exchange("pair_exchange_" + gname, gs, to_sibling,
                                [_half_shape(s, a) for s, a in zip(shapes, axes)], BF16)
        sbs = [_pair_sum("pair_sum_" + n, self.pos, g, l, s, a)
               for n, a, g, l, s in zip(names, axes, gs, landed, shapes)]
        lands = [lax.empty((3,) + _piece_shape(s, a), BF16) for s, a in zip(shapes, axes)]
        plan = _scatter_plan(axes, shapes)
        sems, thru, tok = _split_start("scatter_%s_start" % gname, sbs + lands, 3 * len(names), plan)
        self.tokens.append(tok)
        self.reducing[gname] = (names, axes, shapes, plan, sems, thru)

    def reduce_end(self, gname, after):
        names, axes, shapes, plan, sems, thru = self.reducing.pop(gname)
        nt = len(names)
        thru = _split_wait("scatter_%s_wait" % gname, thru, sems, after, plan)
        mine = [_chip_sum("chip_sum_" + n, self.pos, sb, l, s, a)
                for n, a, sb, l, s in zip(names, axes, thru[:nt], thru[nt:], shapes)]
        theirs = _pair_exchange("half_exchange_" + gname, mine, [(lambda ref, c: ref)] * nt,
                                [m.shape for m in mine], F32)
        for n, a, mi, th in zip(names, axes, mine, theirs):
            g, d, nm, nv = _adamw_halves("adamw_" + n, self.pos, self.P[n][0], self.Mo[n][0], self.Vo[n][0],
                                         mi, th, a)
            self.results[n] = (g[None], d[None], nm[None], nv[None])


def kernel(x, ffn1_norm_g, ffn1_w_gate, ffn1_w_up, ffn1_w_down, mix_norm_g, w_in, conv_w_dw, conv_b_dw, conv_ln_g, conv_ln_b, q_norm_g, k_norm_g, w_out, ffn2_norm_g, ffn2_w_gate, ffn2_w_up, ffn2_w_down, loss_target, m_ffn1_norm_g, m_ffn1_w_gate, m_ffn1_w_up, m_ffn1_w_down, m_mix_norm_g, m_w_in, m_conv_w_dw, m_conv_b_dw, m_conv_ln_g, m_conv_ln_b, m_q_norm_g, m_k_norm_g, m_w_out, m_ffn2_norm_g, m_ffn2_w_gate, m_ffn2_w_up, m_ffn2_w_down, v_ffn1_norm_g, v_ffn1_w_gate, v_ffn1_w_up, v_ffn1_w_down, v_mix_norm_g, v_w_in, v_conv_w_dw, v_conv_b_dw, v_conv_ln_g, v_conv_ln_b, v_q_norm_g, v_k_norm_g, v_w_out, v_ffn2_norm_g, v_ffn2_w_gate, v_ffn2_w_up, v_ffn2_w_down):
    args = dict(locals())
    P = {n: args[n] for n in ORDER}
    Mo = {n: args["m_" + n] for n in ORDER}
    Vo = {n: args["v_" + n] for n in ORDER}
    xs = x[0]
    tgt = loss_target[0]
    T, D = xs.shape
    hd = q_norm_g.shape[-1]
    C = conv_b_dw.shape[-1]
    ntap = conv_w_dw.shape[1]
    cx, cy, cc = _position()
    j0 = 2 * cx + cy
    pos = jnp.stack([j0, cc]).astype(jnp.int32)

    conv_shard = jnp.pad(conv_w_dw[0], ((0, HALO - ntap), (0, 0)))
    comm = _Exchange(P, Mo, Vo, conv_shard, pos)
    lossvec, dx0, G = _local_step(xs, tgt, {n: P[n] for n in SMALL}, comm, hd)
    loss = lax.psum(0.5 / D * jnp.sum(lossvec), AXES)
    grads, deltas, new_m, new_v = {}, {}, {}, {}
    for n, _ in BIG:
        grads[n], deltas[n], new_m[n], new_v[n] = comm.results[n]

    rows = [G["conv_w32"]]
    for n in ("ffn1_norm_g", "mix_norm_g", "ffn2_norm_g"):
        rows.append(G[n].reshape(D // C, C))
    for n in ("conv_b_dw", "conv_ln_g", "conv_ln_b", "q_norm_g", "k_norm_g"):
        rows.append(G[n])
    packed = jnp.concatenate(rows, axis=0)
    packed = jnp.pad(packed, ((0, -packed.shape[0] % 8), (0, 0)))
    total = _sum_slots("sum_small_grads", _gather_small(packed))
    r = HALO
    small_g = {}
    cq = C // 4
    small_g["conv_w_dw"] = lax.dynamic_slice(total[:ntap], (0, j0 * cq), (ntap, cq))
    for n in ("ffn1_norm_g", "mix_norm_g", "ffn2_norm_g"):
        small_g[n] = total[r:r + D // C].reshape(1, D)
        r += D // C
    for n in ("conv_b_dw", "conv_ln_g", "conv_ln_b"):
        small_g[n] = total[r:r + 1]
        r += 1
    for n in ("q_norm_g", "k_norm_g"):
        small_g[n] = total[r:r + 1, :hd]
        r += 1
    for n in ("conv_w_dw",) + SMALL:
        lead = n == "conv_w_dw"
        w2, m2, v2 = (P[n][0], Mo[n][0], Vo[n][0]) if lead else (P[n], Mo[n], Vo[n])
        d, nm, nv = _adamw_small("adamw_" + n, w2, small_g[n], m2, v2)
        if lead:
            grads[n], deltas[n], new_m[n], new_v[n] = small_g[n][None], d[None], nm[None], nv[None]
        else:
            grads[n], deltas[n], new_m[n], new_v[n] = small_g[n], d, nm, nv

    return (loss, dx0[None], *[grads[n] for n in ORDER], *[deltas[n] for n in ORDER],
            *[new_m[n] for n in ORDER], *[new_v[n] for n in ORDER])
```

```python
import jax
import jax.numpy as jnp
from jax import lax
from jax.experimental import pallas as pl
from jax.experimental.pallas import tpu as pltpu

F32 = jnp.float32
BF16 = jnp.bfloat16
EPS = 1e-6
WINDOW = 128
DILATIONS = (1, 4, 16)
ALIBI_MAX_BIAS = 8.0
LANES = 128
HALO = 32
ADAM_LR, ADAM_B1, ADAM_B2, ADAM_EPS, ADAM_WD, ADAM_STEP = 0.001, 0.9, 0.999, 1e-08, 0.01, 10
VMEM_LIMIT_MB = 62
ROW_TILE = 1024
TN_ACC_ELEMS = 3 * 1024 * 1024
EPILOGUE_ROWS = 256
ACC_COLS = 512
MESH = pl.DeviceIdType.MESH
ANY = pl.BlockSpec(memory_space=pl.ANY)
AXES = ("x", "y", "c")
NEG = -1e30


def _pick(n, cands):
    for c in cands:
        if n % c == 0:
            return c
    return n


def _params(nsem):
    return pltpu.CompilerParams(dimension_semantics=("arbitrary",) * nsem,
                                vmem_limit_bytes=VMEM_LIMIT_MB << 20)


def _nn(a, b):
    return jnp.dot(a, b, preferred_element_type=F32)


def _nt(a, b):
    return lax.dot_general(a, b, (((1,), (1,)), ((), ())), preferred_element_type=F32)


def _tn(a, b):
    return lax.dot_general(a, b, (((0,), (0,)), ((), ())), preferred_element_type=F32)


def _sigmoid(v):
    return jax.nn.sigmoid(v)


def _rms_r(xv):
    return lax.rsqrt(jnp.mean(xv * xv, axis=-1, keepdims=True) + EPS)


def _norm_matmul(name, x, g, ws, swiglu):
    T, D = x.shape
    N = ws[0].shape[1]
    tm = _pick(T, (ROW_TILE, 512, 256, 128))
    tn = _pick(N, (512, 256, 128))
    nw = len(ws)

    def body(*refs):
        x_ref, g_ref = refs[:2]
        w_refs = refs[2:2 + nw]
        outs = refs[2 + nw:-1]
        hs = refs[-1]

        @pl.when(pl.program_id(1) == 0)
        def _():
            for r0 in range(0, tm, EPILOGUE_ROWS):
                rows = slice(r0, r0 + min(EPILOGUE_ROWS, tm))
                xv = x_ref[rows, :]
                hv = (xv * _rms_r(xv) * g_ref[...]).astype(BF16)
                hs[rows, :] = hv
                outs[0][rows, :] = hv

        h = hs[...]
        if swiglu:
            gt = _nn(h, w_refs[0][...])
            u = _nn(h, w_refs[1][...])
            sg = _sigmoid(gt)
            silu = gt * sg
            outs[1][...] = (u * (sg * (1.0 + gt * (1.0 - sg)))).astype(BF16)
            outs[2][...] = silu.astype(BF16)
            outs[3][...] = (silu * u).astype(BF16)
        else:
            outs[1][...] = _nn(h, w_refs[0][...])

    row = pl.BlockSpec((tm, D), lambda i, j: (i, 0))
    col = pl.BlockSpec((D, tn), lambda i, j: (0, j))
    tile = pl.BlockSpec((tm, tn), lambda i, j: (i, j))
    if swiglu:
        out_shape = [jax.ShapeDtypeStruct((T, D), BF16)] + [jax.ShapeDtypeStruct((T, N), BF16)] * 3
        out_specs = [row, tile, tile, tile]
    else:
        out_shape = [jax.ShapeDtypeStruct((T, D), BF16), jax.ShapeDtypeStruct((T, N), F32)]
        out_specs = [row, tile]
    return pl.pallas_call(
        body, name=name, grid=(T // tm, N // tn),
        in_specs=[row, pl.BlockSpec((1, D), lambda i, j: (0, 0))] + [col] * nw,
        out_specs=out_specs, out_shape=out_shape,
        scratch_shapes=[pltpu.VMEM((tm, D), BF16)],
        compiler_params=_params(2),
    )(x, g, *ws)


def _matmul_res(name, a, w, res, scale, tgt=None):
    T, K = a.shape
    N = w.shape[1]
    loss = tgt is not None
    tm = _pick(T, (512, 256, 128))
    tk = _pick(K, (1408, 1024, 512, 256, 128))
    nk = K // tk

    def body(*refs):
        if loss:
            a_ref, w_ref, res_ref, tgt_ref, dx_ref, dxb_ref, lv_ref, acc = refs
        else:
            a_ref, w_ref, res_ref, out_ref, acc = refs
        i, k = pl.program_id(0), pl.program_id(1)

        @pl.when(k == 0)
        def _():
            acc[...] = jnp.zeros_like(acc)

        acc[...] += _nn(a_ref[...], w_ref[...])

        @pl.when(k == nk - 1)
        def _():
            part = jnp.zeros((1, N), F32)
            for r0 in range(0, tm, EPILOGUE_ROWS):
                rows = slice(r0, r0 + min(EPILOGUE_ROWS, tm))
                val = res_ref[rows, :] + scale * acc[rows, :]
                if loss:
                    dv = val - tgt_ref[rows, :]
                    dx = dv * (1.0 / N)
                    dx_ref[rows, :] = dx
                    dxb_ref[rows, :] = dx.astype(BF16)
                    part = part + jnp.sum(dv * dv, axis=0, keepdims=True)
                else:
                    out_ref[rows, :] = val
            if loss:
                @pl.when(i == 0)
                def _():
                    lv_ref[...] = part

                @pl.when(i > 0)
                def _():
                    lv_ref[...] += part

    row = pl.BlockSpec((tm, N), lambda i, k: (i, 0))
    in_specs = [pl.BlockSpec((tm, tk), lambda i, k: (i, k)), pl.BlockSpec((tk, N), lambda i, k: (k, 0)), row]
    args = [a, w, res]
    if loss:
        in_specs.append(row)
        args.append(tgt)
        out_specs = [row, row, pl.BlockSpec((1, N), lambda i, k: (0, 0))]
        out_shape = [jax.ShapeDtypeStruct((T, N), F32), jax.ShapeDtypeStruct((T, N), BF16),
                     jax.ShapeDtypeStruct((1, N), F32)]
    else:
        out_specs = row
        out_shape = jax.ShapeDtypeStruct((T, N), F32)
    return pl.pallas_call(
        body, name=name, grid=(T // tm, nk), in_specs=in_specs, out_specs=out_specs, out_shape=out_shape,
        scratch_shapes=[pltpu.VMEM((tm, N), F32)], compiler_params=_params(2),
    )(*args)


def _nt_matmul(name, dyb, w, scale=1.0, gate=None, up=None):
    T, D = dyb.shape
    N = w.shape[0]
    tm = _pick(T, (ROW_TILE, 512, 256, 128))
    tn = _pick(N, (512, 256, 128))
    swiglu = gate is not None

    def body(*refs):
        if swiglu:
            dy_ref, w_ref, g_ref, u_ref, dg_ref, du_ref = refs
        else:
            dy_ref, w_ref, o_ref, ob_ref = refs
        da = _nt(dy_ref[...], w_ref[...]) * scale
        if swiglu:
            dg_ref[...] = (da * g_ref[...].astype(F32)).astype(BF16)
            du_ref[...] = (da * u_ref[...].astype(F32)).astype(BF16)
        else:
            o_ref[...] = da
            ob_ref[...] = da.astype(BF16)

    tile = pl.BlockSpec((tm, tn), lambda i, j: (i, j))
    in_specs = [pl.BlockSpec((tm, D), lambda i, j: (i, 0)), pl.BlockSpec((tn, D), lambda i, j: (j, 0))]
    args = [dyb, w]
    if swiglu:
        in_specs += [tile, tile]
        args += [gate, up]
        out_shape = [jax.ShapeDtypeStruct((T, N), BF16)] * 2
    else:
        out_shape = [jax.ShapeDtypeStruct((T, N), F32), jax.ShapeDtypeStruct((T, N), BF16)]
    return pl.pallas_call(
        body, name=name, grid=(T // tm, N // tn), in_specs=in_specs, out_specs=[tile, tile],
        out_shape=out_shape, compiler_params=_params(2),
    )(*args)


def _nt_rms_bwd(name, As, Ws, x, g, dres):
    T, K = As[0].shape
    D = x.shape[1]
    tm = _pick(T, (ROW_TILE, 512, 256, 128))
    tk = _pick(K, (512, 256, 128))
    nk = K // tk
    na = len(As)

    def body(*refs):
        a_refs = refs[:na]
        w_refs = refs[na:2 * na]
        x_hbm, g_ref, dres_hbm, acc, dxb_ref, dg_ref, x_ref, dres_ref, sems = refs[2 * na:]
        dx_ref = acc
        i, k = pl.program_id(0), pl.program_id(1)

        def row_copies():
            rows = pl.ds(pl.multiple_of(i * tm, tm), tm)
            return (pltpu.make_async_copy(x_hbm.at[rows, :], x_ref, sems.at[0]),
                    pltpu.make_async_copy(dres_hbm.at[rows, :], dres_ref, sems.at[1]))

        @pl.when(k == 0)
        def _():
            acc[...] = jnp.zeros_like(acc)
            for cp in row_copies():
                cp.start()

        for c0 in range(0, D, ACC_COLS):
            cols = slice(c0, min(c0 + ACC_COLS, D))
            part = _nt(a_refs[0][...], w_refs[0][cols, :])
            for a_ref, w_ref in zip(a_refs[1:], w_refs[1:]):
                part = part + _nt(a_ref[...], w_ref[cols, :])
            acc[:, cols] += part

        @pl.when(k == nk - 1)
        def _():
            for cp in row_copies():
                cp.wait()
            part = jnp.zeros((1, D), F32)
            for r0 in range(0, tm, EPILOGUE_ROWS):
                rows = slice(r0, r0 + min(EPILOGUE_ROWS, tm))
                dh = acc[rows, :]
                xv = x_ref[rows, :]
                r = _rms_r(xv)
                gd = dh * g_ref[...]
                dx = dres_ref[rows, :] + r * gd - xv * (r * r * r) * jnp.mean(gd * xv, axis=-1, keepdims=True)
                dx_ref[rows, :] = dx
                dxb_ref[rows, :] = dx.astype(BF16)
                part = part + jnp.sum(dh * xv * r, axis=0, keepdims=True)

            @pl.when(i == 0)
            def _():
                dg_ref[...] = part

            @pl.when(i > 0)
            def _():
                dg_ref[...] += part

    row = pl.BlockSpec((tm, D), lambda i, k: (i, 0), pipeline_mode=pl.Buffered(1))
    vec = pl.BlockSpec((1, D), lambda i, k: (0, 0))
    return pl.pallas_call(
        body, name=name, grid=(T // tm, nk),
        in_specs=[pl.BlockSpec((tm, tk), lambda i, k: (i, k))] * na
        + [pl.BlockSpec((D, tk), lambda i, k: (0, k))] * na + [ANY, vec, ANY],
        out_specs=[row, row, vec],
        out_shape=[jax.ShapeDtypeStruct((T, D), F32), jax.ShapeDtypeStruct((T, D), BF16),
                   jax.ShapeDtypeStruct((1, D), F32)],
        scratch_shapes=[pltpu.VMEM((tm, D), F32), pltpu.VMEM((tm, D), F32), pltpu.SemaphoreType.DMA((2,))],
        compiler_params=_params(2),
    )(*As, *Ws, x, g, dres)


def _tn_matmul(name, a, b, scale=1.0):
    T, M = a.shape
    N = b.shape[1]
    tn = _pick(N, (2048, 1408, 1280, 1024, 512, 256, 128))
    tm = _pick(M, tuple(c for c in (2048, 1408, 1024, 512, 256, 128) if c * tn <= TN_ACC_ELEMS))
    tk = _pick(T, (1024, 512, 256, 128))
    nk = T // tk

    def body(a_ref, b_ref, o_ref, acc):
        k = pl.program_id(2)

        @pl.when(k == 0)
        def _():
            acc[...] = jnp.zeros_like(acc)

        for r0 in range(0, tm, ACC_COLS):
            rows = slice(r0, min(r0 + ACC_COLS, tm))
            acc[rows, :] += _tn(a_ref[:, rows], b_ref[...])

        @pl.when(k == nk - 1)
        def _():
            o_ref[...] = (acc[...] * scale).astype(BF16)

    return pl.pallas_call(
        body, name=name, grid=(M // tm, N // tn, nk),
        in_specs=[pl.BlockSpec((tk, tm), lambda i, j, k: (k, i)), pl.BlockSpec((tk, tn), lambda i, j, k: (k, j))],
        out_specs=pl.BlockSpec((tm, tn), lambda i, j, k: (i, j)),
        out_shape=jax.ShapeDtypeStruct((M, N), BF16),
        scratch_shapes=[pltpu.VMEM((tm, tn), F32)], compiler_params=_params(3),
    )(a, b)


CONV_ROWS = 128
ROW_CHUNK = 32
LANE_CHUNK = 256


SUBLANES = 8


def _fill_shifts(sh, buf, rows):
    for p in range(1, SUBLANES):
        sh[p, 0:rows - SUBLANES, :] = buf[p:p + rows - SUBLANES, :]


def _tap(buf, sh, s, n, cols):
    p = s % SUBLANES
    return buf[s:s + n, cols] if p == 0 else sh[p, s - p:s - p + n, cols]


def _conv_fwd(name, z, w32, b, lg, lb, C):
    T = z.shape[0]
    tc = CONV_ROWS
    ntap = 31
    lc = _pick(C, (LANE_CHUNK, LANES))
    rpb = tc // HALO

    def body(zc_ref, zp_ref, w_ref, b_ref, lg_ref, lb_ref, yc_ref, ycv_ref, vbuf, ybuf, vsh):
        i = pl.program_id(0)
        zc = zc_ref[...]
        zp = zp_ref[...]
        vbuf[HALO:HALO + tc, :] = zc[:, :C] * _sigmoid(zc[:, C:])
        vbuf[0:HALO, :] = jnp.where(i > 0, zp[:, :C] * _sigmoid(zp[:, C:]), 0.0)
        _fill_shifts(vsh, vbuf, tc + HALO)
        for r0 in range(0, tc, ROW_CHUNK):
            for c0 in range(0, C, lc):
                cols = slice(c0, c0 + lc)
                acc = jnp.zeros((ROW_CHUNK, lc), F32) + b_ref[:, cols]
                for k in range(ntap):
                    acc = acc + w_ref[k:k + 1, cols] * _tap(vbuf, vsh, r0 + 2 + k, ROW_CHUNK, cols)
                ybuf[r0:r0 + ROW_CHUNK, cols] = acc
        y = ybuf[...]
        ycv_ref[...] = y
        mu = jnp.mean(y, axis=-1, keepdims=True)
        yc = y - mu
        rstd = lax.rsqrt(jnp.mean(yc * yc, axis=-1, keepdims=True) + EPS)
        ln = yc * rstd * lg_ref[...] + lb_ref[...]
        yc_ref[...] = (ln * _sigmoid(ln)).astype(BF16)

    vec = pl.BlockSpec((1, C), lambda i: (0, 0))
    return pl.pallas_call(
        body, name=name, grid=(T // tc,),
        in_specs=[pl.BlockSpec((tc, 2 * C), lambda i: (i, 0)),
                  pl.BlockSpec((HALO, 2 * C), lambda i: (jnp.maximum(i * rpb - 1, 0), 0)),
                  pl.BlockSpec((HALO, C), lambda i: (0, 0)), vec, vec, vec],
        out_specs=[pl.BlockSpec((tc, C), lambda i: (i, 0))] * 2,
        out_shape=[jax.ShapeDtypeStruct((T, C), BF16), jax.ShapeDtypeStruct((T, C), F32)],
        scratch_shapes=[pltpu.VMEM((tc + HALO, C), F32), pltpu.VMEM((tc, C), F32),
                        pltpu.VMEM((SUBLANES, tc + HALO, C), F32)],
        compiler_params=_params(1),
    )(z, z, w32, b, lg, lb)


def _conv_bwd(name, z, ycv, dycat, w32, lg, lb, C):
    T = z.shape[0]
    tc = CONV_ROWS
    ntap = 31
    lc = _pick(C, (LANE_CHUNK, LANES))
    rpb = tc // HALO
    nstep = T // tc
    nhb = T // HALO

    def ln_bwd(dyc, y, lgv, lbv):
        mu = jnp.mean(y, axis=-1, keepdims=True)
        yc = y - mu
        rstd = lax.rsqrt(jnp.mean(yc * yc, axis=-1, keepdims=True) + EPS)
        yn = yc * rstd
        ln = yn * lgv + lbv
        sg = _sigmoid(ln)
        dln = dyc * (sg * (1.0 + ln * (1.0 - sg)))
        dyn = dln * lgv
        dy = rstd * (dyn - jnp.mean(dyn, axis=-1, keepdims=True)
                     - yn * jnp.mean(dyn * yn, axis=-1, keepdims=True))
        return dy, dln, yn

    def body(zc_ref, zp_ref, y_ref, yn_ref, d_ref, dn_ref, w_ref, lg_ref, lb_ref,
             dz_ref, dw_ref, db_ref, dlg_ref, dlb_ref, vbuf, dbuf, dvbuf, dwacc, vsh, dsh):
        i = pl.program_id(0)
        lgv, lbv = lg_ref[...], lb_ref[...]
        zc = zc_ref[...]
        zp = zp_ref[...]
        a = zc[:, :C]
        sgt = _sigmoid(zc[:, C:])
        vbuf[HALO:HALO + tc, :] = a * sgt
        vbuf[0:HALO, :] = jnp.where(i > 0, zp[:, :C] * _sigmoid(zp[:, C:]), 0.0)
        dy, dln, yn = ln_bwd(d_ref[...], y_ref[...], lgv, lbv)
        dbuf[0:tc, :] = dy
        dyn_, _, _ = ln_bwd(dn_ref[...], yn_ref[...], lgv, lbv)
        dbuf[tc:tc + HALO, :] = jnp.where(i < nstep - 1, dyn_, 0.0)

        @pl.when(i == 0)
        def _():
            dwacc[...] = jnp.zeros_like(dwacc)
            db_ref[...] = jnp.zeros_like(db_ref)
            dlg_ref[...] = jnp.zeros_like(dlg_ref)
            dlb_ref[...] = jnp.zeros_like(dlb_ref)

        db_ref[...] += jnp.sum(dy, axis=0, keepdims=True)
        dlg_ref[...] += jnp.sum(dln * yn, axis=0, keepdims=True)
        dlb_ref[...] += jnp.sum(dln, axis=0, keepdims=True)

        _fill_shifts(vsh, vbuf, tc + HALO)
        _fill_shifts(dsh, dbuf, tc + HALO)
        for r0 in range(0, tc, ROW_CHUNK):
            for c0 in range(0, C, lc):
                cols = slice(c0, c0 + lc)
                dcur = dbuf[r0:r0 + ROW_CHUNK, cols]
                acc = jnp.zeros((ROW_CHUNK, lc), F32)
                for k in range(ntap):
                    acc = acc + w_ref[k:k + 1, cols] * _tap(dbuf, dsh, r0 + 30 - k, ROW_CHUNK, cols)
                    prod = dcur * _tap(vbuf, vsh, r0 + 2 + k, ROW_CHUNK, cols)
                    red = prod[0:8]
                    for q in range(8, ROW_CHUNK, 8):
                        red = red + prod[q:q + 8]
                    dwacc[8 * k:8 * k + 8, cols] += red
                dvbuf[r0:r0 + ROW_CHUNK, cols] = acc
        dv = dvbuf[...]
        dz_ref[:, :C] = (dv * sgt).astype(BF16)
        dz_ref[:, C:] = (dv * a * sgt * (1.0 - sgt)).astype(BF16)

        @pl.when(i == nstep - 1)
        def _():
            for k in range(ntap):
                dw_ref[k:k + 1, :] = jnp.sum(dwacc[8 * k:8 * k + 8, :], axis=0, keepdims=True)
            dw_ref[ntap:HALO, :] = jnp.zeros((HALO - ntap, C), F32)

    vec = pl.BlockSpec((1, C), lambda i: (0, 0))
    cur = pl.BlockSpec((tc, C), lambda i: (i, 0))
    nxt = pl.BlockSpec((HALO, C), lambda i: (jnp.minimum((i + 1) * rpb, nhb - 1), 0))
    return pl.pallas_call(
        body, name=name, grid=(nstep,),
        in_specs=[pl.BlockSpec((tc, 2 * C), lambda i: (i, 0)),
                  pl.BlockSpec((HALO, 2 * C), lambda i: (jnp.maximum(i * rpb - 1, 0), 0)),
                  cur, nxt, cur, nxt, pl.BlockSpec((HALO, C), lambda i: (0, 0)), vec, vec],
        out_specs=[pl.BlockSpec((tc, 2 * C), lambda i: (i, 0)), pl.BlockSpec((HALO, C), lambda i: (0, 0)),
                   vec, vec, vec],
        out_shape=[jax.ShapeDtypeStruct((T, 2 * C), BF16), jax.ShapeDtypeStruct((HALO, C), F32),
                   jax.ShapeDtypeStruct((1, C), F32), jax.ShapeDtypeStruct((1, C), F32),
                   jax.ShapeDtypeStruct((1, C), F32)],
        scratch_shapes=[pltpu.VMEM((tc + HALO, C), F32), pltpu.VMEM((tc + HALO, C), F32),
                        pltpu.VMEM((tc, C), F32), pltpu.VMEM((8 * HALO, C), F32),
                        pltpu.VMEM((SUBLANES, tc + HALO, C), F32), pltpu.VMEM((SUBLANES, tc + HALO, C), F32)],
        compiler_params=_params(1),
    )(z, z, ycv, ycv, dycat, dycat, w32, lg, lb)


def _seg_sum(u, bmat):
    hi = u.astype(BF16)
    lo = (u - hi.astype(F32)).astype(BF16)
    parts = [_nn(hi[:, c:c + LANES], bmat) + _nn(lo[:, c:c + LANES], bmat) for c in range(0, u.shape[1], LANES)]
    return jnp.concatenate(parts, axis=1)


def _attn_prep(name, z, gq, gk, bmat, A, c0, hd):
    T = z.shape[0]
    tm = _pick(T, (256, 128))

    def body(zq_ref, zk_ref, zv_ref, gq_ref, gk_ref, b_ref, o_ref):
        bm = b_ref[...]
        for idx, (z_ref, g_ref) in enumerate(((zq_ref, gq_ref), (zk_ref, gk_ref))):
            zv = z_ref[...]
            r = lax.rsqrt(_seg_sum(zv * zv, bm) * (1.0 / hd) + EPS)
            o_ref[:, idx * A:(idx + 1) * A] = zv * r * g_ref[...]
        o_ref[:, 2 * A:] = zv_ref[...]

    vec = pl.BlockSpec((1, A), lambda i: (0, 0))
    return pl.pallas_call(
        body, name=name, grid=(T // tm,),
        in_specs=[pl.BlockSpec((tm, A), lambda i: (i, c0)), pl.BlockSpec((tm, A), lambda i: (i, c0 + 1)),
                  pl.BlockSpec((tm, A), lambda i: (i, c0 + 2)), vec, vec,
                  pl.BlockSpec((LANES, LANES), lambda i: (0, 0))],
        out_specs=pl.BlockSpec((tm, 3 * A), lambda i: (i, 0)),
        out_shape=jax.ShapeDtypeStruct((T, 3 * A), F32), compiler_params=_params(1),
    )(z, z, z, gq, gk, bmat)


QK_SCALE = 0.125
ATTN_UNROLL = 4


def _fill_bias(bias, sl_ref, hp, d):
    qi = lax.broadcasted_iota(jnp.int32, (WINDOW, 2 * WINDOW), 0)
    kj = lax.broadcasted_iota(jnp.int32, (WINDOW, 2 * WINDOW), 1)
    dist = WINDOW + qi - kj
    inband = (dist >= 0) & (dist <= WINDOW)
    distf = dist.astype(F32)
    for hh in range(2):
        b = jnp.where(inband, -(sl_ref[2 * hp + hh] * d) * distf, NEG)
        bias[2 * hh + 1] = b
        bias[2 * hh] = jnp.where(kj >= WINDOW, b, NEG)


CHUNK = WINDOW * DILATIONS[-1]


def _deinterleave(dst, src, d, rows, dst_pitch, dst_off, src_off):
    for r in range(d):
        if d == 1:
            val = src[src_off:src_off + rows, :]
        else:
            val = src[pl.ds(src_off + r, rows, stride=d), :]
        lo = r * dst_pitch + dst_off
        dst[lo:lo + rows, :] = val.astype(dst.dtype)


def _interleave_add(dst, start, src, d, rows, src_pitch, src_off):
    for r in range(d):
        lo = r * src_pitch + src_off
        idx = pl.ds(start, rows) if d == 1 else pl.ds(start + r, rows, stride=d)
        dst[idx, :] += src[lo:lo + rows, :]


def _attn_fwd(name, qkv, slopes, A):
    T = qkv.shape[0]
    hpn, nch, nblk = A // LANES, T // CHUNK, CHUNK // WINDOW
    nbranch = len(DILATIONS)

    def body(*refs):
        sl_ref, q_ref, k_ref, kp_ref, v_ref, vp_ref, y_ref, lg_ref, qd, kd, vd, od, ld, bias = refs[:14]
        onat, lnat = refs[14:14 + nbranch], refs[14 + nbranch:]
        hp, ch = pl.program_id(0), pl.program_id(1)
        lane = lax.broadcasted_iota(jnp.int32, (1, LANES), 1)
        first = lane < (LANES // 2)
        for bi, d in enumerate(DILATIONS):
            Ld = CHUNK // d
            seg = Ld + WINDOW
            nbr = Ld // WINDOW
            _deinterleave(qd, q_ref, d, Ld, Ld, 0, 0)
            for dst, cur, prev in ((kd, k_ref, kp_ref), (vd, v_ref, vp_ref)):
                _deinterleave(dst, prev, d, WINDOW, seg, 0, CHUNK - WINDOW * d)
                _deinterleave(dst, cur, d, Ld, seg, WINDOW, 0)
            _fill_bias(bias, sl_ref, hp, d)
            ob, lb = (onat[bi], lnat[bi]) if d == 1 else (od, ld)

            def step(it, carry, Ld=Ld, seg=seg, nbr=nbr, ob=ob, lb=lb):
                r, nl = it // nbr, it % nbr
                q0 = pl.multiple_of(r * Ld + nl * WINDOW, WINDOW)
                k0 = pl.multiple_of(r * seg + nl * WINDOW, WINDOW)
                later = jnp.where(ch * nbr + nl > 0, 1, 0)
                qb = qd[pl.ds(q0, WINDOW), :]
                k2 = kd[pl.ds(k0, 2 * WINDOW), :]
                v2 = vd[pl.ds(k0, 2 * WINDOW), :]
                res = []
                for hh in range(2):
                    mh = first if hh == 0 else jnp.logical_not(first)
                    s = _nt(jnp.where(mh, qb, jnp.zeros_like(qb)), k2) + bias[2 * hh + later]
                    mx = jnp.max(s, axis=-1, keepdims=True)
                    p = jnp.exp(s - mx)
                    den = jnp.sum(p, axis=-1, keepdims=True)
                    res.append((_nn(p.astype(BF16), v2) / den, mx + jnp.log(den)))
                ob[pl.ds(q0, WINDOW), :] = jnp.where(first, res[0][0], res[1][0])
                lb[pl.ds(q0, WINDOW), :] = jnp.where(first, res[0][1], res[1][1])
                return carry

            lax.fori_loop(0, nblk, step, 0, unroll=ATTN_UNROLL)
            if d > 1:
                for r in range(d):
                    onat[bi][pl.ds(r, Ld, stride=d), :] = od[r * Ld:(r + 1) * Ld, :]
                    lnat[bi][pl.ds(r, Ld, stride=d), :] = ld[r * Ld:(r + 1) * Ld, :]
        ls = [l[...] for l in lnat]
        mx = ls[0]
        for v in ls[1:]:
            mx = jnp.maximum(mx, v)
        es = [jnp.exp(v - mx) for v in ls]
        den = es[0]
        for e in es[1:]:
            den = den + e
        out = es[0] * onat[0][...]
        for e, o in zip(es[1:], onat[1:]):
            out = out + e * o[...]
        y_ref[...] = (out / den).astype(BF16)
        lg_ref[...] = mx + jnp.log(den)

    blk = lambda m: pl.BlockSpec((CHUNK, LANES), m)
    cur = lambda which: blk(lambda hp, ch: (ch, which * hpn + hp))
    prev = lambda which: blk(lambda hp, ch: (jnp.maximum(ch - 1, 0), which * hpn + hp))
    omap = blk(lambda hp, ch: (ch, hp))
    f32buf = pltpu.VMEM((CHUNK, LANES), F32)
    return pl.pallas_call(
        body, name=name, grid=(hpn, nch),
        in_specs=[pl.BlockSpec(memory_space=pltpu.SMEM), cur(0), cur(1), prev(1), cur(2), prev(2)],
        out_specs=[omap, omap],
        out_shape=[jax.ShapeDtypeStruct((T, A), BF16), jax.ShapeDtypeStruct((T, A), F32)],
        scratch_shapes=[pltpu.VMEM((CHUNK, LANES), BF16), pltpu.VMEM((2 * CHUNK, LANES), BF16),
                        pltpu.VMEM((2 * CHUNK, LANES), BF16), f32buf, f32buf,
                        pltpu.VMEM((4, WINDOW, 2 * WINDOW), F32)] + [f32buf] * (2 * nbranch),
        compiler_params=_params(2),
    )(slopes, qkv, qkv, qkv, qkv, qkv)


def _attn_bwd(name, qkv, dycat, yatt, lg, slopes, A, catoff):
    T = qkv.shape[0]
    hpn, nch, nblk = A // LANES, T // CHUNK, CHUNK // WINDOW
    co = catoff // LANES

    def body(sl_ref, q_ref, k_ref, kp_ref, v_ref, vp_ref, do_ref, o_ref, l_ref, dq_ref, dk_ref, dv_ref,
             qd, kd, vd, dod, ddn, ddd, ldd, dqd, dkd, dvd, bias):
        hp, ch = pl.program_id(0), pl.program_id(1)
        lane = lax.broadcasted_iota(jnp.int32, (1, LANES), 1)
        first = lane < (LANES // 2)
        ddn[...] = do_ref[...] * o_ref[...].astype(F32)
        dq_ref[...] = jnp.zeros_like(dq_ref)

        @pl.when(ch == 0)
        def _():
            dk_ref[...] = jnp.zeros_like(dk_ref)
            dv_ref[...] = jnp.zeros_like(dv_ref)

        base = ch * CHUNK
        for d in DILATIONS:
            Ld = CHUNK // d
            seg = Ld + WINDOW
            nbr = Ld // WINDOW
            _deinterleave(qd, q_ref, d, Ld, Ld, 0, 0)
            _deinterleave(dod, do_ref, d, Ld, Ld, 0, 0)
            _deinterleave(ddd, ddn, d, Ld, Ld, 0, 0)
            _deinterleave(ldd, l_ref, d, Ld, Ld, 0, 0)
            for dst, cur, prev in ((kd, k_ref, kp_ref), (vd, v_ref, vp_ref)):
                _deinterleave(dst, prev, d, WINDOW, seg, 0, CHUNK - WINDOW * d)
                _deinterleave(dst, cur, d, Ld, seg, WINDOW, 0)
            dkd[0:d * seg, :] = jnp.zeros((d * seg, LANES), F32)
            dvd[0:d * seg, :] = jnp.zeros((d * seg, LANES), F32)
            _fill_bias(bias, sl_ref, hp, d)

            def step(it, carry, Ld=Ld, seg=seg, nbr=nbr):
                r, nl = it // nbr, it % nbr
                q0 = pl.multiple_of(r * Ld + nl * WINDOW, WINDOW)
                k0 = pl.multiple_of(r * seg + nl * WINDOW, WINDOW)
                later = jnp.where(ch * nbr + nl > 0, 1, 0)
                qb = qd[pl.ds(q0, WINDOW), :]
                k2 = kd[pl.ds(k0, 2 * WINDOW), :]
                v2 = vd[pl.ds(k0, 2 * WINDOW), :]
                dob = dod[pl.ds(q0, WINDOW), :]
                dd = ddd[pl.ds(q0, WINDOW), :]
                lb = ldd[pl.ds(q0, WINDOW), :]
                dk2 = jnp.zeros((2 * WINDOW, LANES), F32)
                dv2 = jnp.zeros((2 * WINDOW, LANES), F32)
                dqs = []
                for hh in range(2):
                    mh = first if hh == 0 else jnp.logical_not(first)
                    qh = jnp.where(mh, qb, jnp.zeros_like(qb))
                    doh = jnp.where(mh, dob, jnp.zeros_like(dob))
                    lcol = lb[:, hh * (LANES // 2):hh * (LANES // 2) + 1]
                    p = jnp.exp(_nt(qh, k2) + bias[2 * hh + later] - lcol)
                    dcol = jnp.sum(jnp.where(mh, dd, 0.0), axis=-1, keepdims=True)
                    ds = (p * (_nt(doh, v2) - dcol)).astype(BF16)
                    dqs.append(_nn(ds, k2))
                    dk2 = dk2 + _tn(ds, qh)
                    dv2 = dv2 + _tn(p.astype(BF16), doh)
                dqd[pl.ds(q0, WINDOW), :] = jnp.where(first, dqs[0], dqs[1])
                dkd[pl.ds(k0, 2 * WINDOW), :] += dk2
                dvd[pl.ds(k0, 2 * WINDOW), :] += dv2
                return carry

            lax.fori_loop(0, nblk, step, 0, unroll=ATTN_UNROLL)
            _interleave_add(dq_ref, 0, dqd, d, Ld, Ld, 0)
            for acc, out in ((dkd, dk_ref), (dvd, dv_ref)):
                _interleave_add(out, base, acc, d, Ld, seg, WINDOW)

                @pl.when(ch > 0)
                def _(acc=acc, out=out, d=d, seg=seg):
                    _interleave_add(out, base - WINDOW * d, acc, d, WINDOW, seg, 0)

    blk = lambda m: pl.BlockSpec((CHUNK, LANES), m)
    cur = lambda which: blk(lambda hp, ch: (ch, which * hpn + hp))
    prev = lambda which: blk(lambda hp, ch: (jnp.maximum(ch - 1, 0), which * hpn + hp))
    omap = blk(lambda hp, ch: (ch, hp))
    full = pl.BlockSpec((T, LANES), lambda hp, ch: (0, hp))
    f32buf = pltpu.VMEM((CHUNK, LANES), F32)
    bf16buf = pltpu.VMEM((CHUNK, LANES), BF16)
    return pl.pallas_call(
        body, name=name, grid=(hpn, nch),
        in_specs=[pl.BlockSpec(memory_space=pltpu.SMEM), cur(0), cur(1), prev(1), cur(2), prev(2),
                  blk(lambda hp, ch: (ch, co + hp)), omap, omap],
        out_specs=[omap, full, full],
        out_shape=[jax.ShapeDtypeStruct((T, A), F32)] * 3,
        scratch_shapes=[bf16buf, pltpu.VMEM((2 * CHUNK, LANES), BF16), pltpu.VMEM((2 * CHUNK, LANES), BF16),
                        bf16buf, f32buf, f32buf, f32buf, f32buf,
                        pltpu.VMEM((2 * CHUNK, LANES), F32), pltpu.VMEM((2 * CHUNK, LANES), F32),
                        pltpu.VMEM((4, WINDOW, 2 * WINDOW), F32)],
        compiler_params=_params(2),
    )(slopes, qkv, qkv, qkv, qkv, qkv, dycat, yatt, lg)


def _attn_bwd_combine(name, dqs, dks, dvs, z, gq, gk, bmat, fmat, A, c0, hd):
    T = z.shape[0]
    tm = _pick(T, (256, 128))
    nbr = len(dqs)

    def body(*refs):
        dq_refs, dk_refs, dv_refs = refs[:nbr], refs[nbr:2 * nbr], refs[2 * nbr:3 * nbr]
        zq_ref, zk_ref, gq_ref, gk_ref, b_ref, f_ref, dz_ref, dgq_ref, dgk_ref = refs[3 * nbr:]
        i = pl.program_id(0)
        bm = b_ref[...]

        def tot(rs):
            t = rs[0][...]
            for r in rs[1:]:
                t = t + r[...]
            return t

        for idx, (d_refs, z_ref, g_ref, dg_ref, gscale) in enumerate(
                ((dq_refs, zq_ref, gq_ref, dgq_ref, QK_SCALE), (dk_refs, zk_ref, gk_ref, dgk_ref, 1.0))):
            dy = tot(d_refs)
            zv = z_ref[...]
            r = lax.rsqrt(_seg_sum(zv * zv, bm) * (1.0 / hd) + EPS)
            gd = dy * g_ref[...]
            mean = _seg_sum(gd * zv, bm) * (1.0 / hd)
            dz_ref[:, idx * A:(idx + 1) * A] = (r * gd - zv * (r * r * r) * mean).astype(BF16)
            part = jnp.sum(dy * zv * r, axis=0, keepdims=True) * gscale

            @pl.when(i == 0)
            def _():
                dg_ref[...] = part

            @pl.when(i > 0)
            def _():
                dg_ref[...] += part

        dz_ref[:, 2 * A:] = tot(dv_refs).astype(BF16)

        @pl.when(i == T // tm - 1)
        def _():
            fm = f_ref[...]
            for dg_ref in (dgq_ref, dgk_ref):
                v = jnp.broadcast_to(dg_ref[...], (8, A))
                hi = v.astype(BF16)
                mid = (v - hi.astype(F32)).astype(BF16)
                lo = (v - hi.astype(F32) - mid.astype(F32)).astype(BF16)
                dg_ref[...] = (_nn(hi, fm) + _nn(mid, fm) + _nn(lo, fm))[0:1]

    blk = pl.BlockSpec((tm, A), lambda i: (i, 0))
    vec = pl.BlockSpec((1, A), lambda i: (0, 0))
    return pl.pallas_call(
        body, name=name, grid=(T // tm,),
        in_specs=[blk] * (3 * nbr) + [pl.BlockSpec((tm, A), lambda i: (i, c0)),
                                      pl.BlockSpec((tm, A), lambda i: (i, c0 + 1)), vec, vec,
                                      pl.BlockSpec((LANES, LANES), lambda i: (0, 0)),
                                      pl.BlockSpec((A, A), lambda i: (0, 0))],
        out_specs=[pl.BlockSpec((tm, 3 * A), lambda i: (i, 0)), vec, vec],
        out_shape=[jax.ShapeDtypeStruct((T, 3 * A), BF16), jax.ShapeDtypeStruct((1, A), F32),
                   jax.ShapeDtypeStruct((1, A), F32)],
        compiler_params=_params(1),
    )(*dqs, *dks, *dvs, z, z, gq, gk, bmat, fmat)


def _local_step(x, tgt, S, comm, hd):
    T, D = x.shape
    C = S["conv_b_dw"].shape[1]
    A = C
    H = A // hd
    Dmix = C + A
    c0 = (2 * C) // A
    slopes = 2.0 ** (-ALIBI_MAX_BIAS * jnp.arange(1, H + 1, dtype=F32) / H)
    seg = jnp.arange(LANES) // hd
    bmat = (seg[:, None] == seg[None, :]).astype(BF16)
    pos_in_head = jnp.arange(A) % hd
    fmat = (pos_in_head[:, None] == pos_in_head[None, :]).astype(BF16)
    gq = jnp.tile(S["q_norm_g"], (1, H)) * QK_SCALE
    gk = jnp.tile(S["k_norm_g"], (1, H))

    wg1, wu1 = comm.weights(("ffn1_w_gate", "ffn1_w_up"), None)
    h1, gate1, up1, a1 = _norm_matmul("ffn1_up", x, comm.tie(S["ffn1_norm_g"]), [wg1, wu1], True)
    wd1, win, w32 = comm.weights(("ffn1_w_down", "w_in", "conv_w32"), a1)
    x1 = _matmul_res("ffn1_down", a1, wd1, x, 0.5)
    h2, z = _norm_matmul("mix_in", x1, S["mix_norm_g"], [win], False)
    yc, ycv = _conv_fwd("conv_fwd", z, w32, S["conv_b_dw"], S["conv_ln_g"], S["conv_ln_b"], C)
    qkv = _attn_prep("attn_prep", z, gq, gk, bmat, A, c0, hd)
    yatt, lg = _attn_fwd("attn_fwd", qkv, slopes, A)
    ycat = jnp.concatenate([yc, yatt], axis=1)
    wout, wg2, wu2, wd2 = comm.weights(("w_out", "ffn2_w_gate", "ffn2_w_up", "ffn2_w_down"), yatt)
    x2 = _matmul_res("mix_out", ycat, wout, x1, 1.0)
    h3, gate2, up2, a2 = _norm_matmul("ffn2_up", x2, S["ffn2_norm_g"], [wg2, wu2], True)
    dx3, dx3b, lossvec = _matmul_res("ffn2_down_loss", a2, wd2, x2, 0.5, tgt=tgt)

    G = {}
    dgate2, dup2 = _nt_matmul("ffn2_dact", dx3b, wd2, 0.5, gate2, up2)
    comm.reduce_begin("ffn2", {"ffn2_w_down": _tn_matmul("ffn2_dwd", a2, dx3b, 0.5),
                               "ffn2_w_gate": _tn_matmul("ffn2_dwg", h3, dgate2),
                               "ffn2_w_up": _tn_matmul("ffn2_dwu", h3, dup2)})
    dx2, dx2b, G["ffn2_norm_g"] = _nt_rms_bwd("ffn2_dx", [dgate2, dup2], [wg2, wu2],
                                              x2, comm.tie(S["ffn2_norm_g"]), dx3)
    dwout = _tn_matmul("mix_dwout", ycat, dx2b)
    dycat, _ = _nt_matmul("mix_dycat", dx2b, wout)
    dzc, G["conv_w32"], G["conv_b_dw"], G["conv_ln_g"], G["conv_ln_b"] = _conv_bwd(
        "conv_bwd", z, ycv, dycat, w32, S["conv_ln_g"], S["conv_ln_b"], C)
    dq, dk, dv = _attn_bwd("attn_bwd", qkv, dycat, yatt, lg, slopes, A, C)
    dzqkv, G["q_norm_g"], G["k_norm_g"] = _attn_bwd_combine(
        "attn_bwd_combine", [dq], [dk], [dv], z, gq, gk, bmat, fmat, A, c0, hd)
    comm.reduce_end("ffn2", dzqkv)
    dz = jnp.concatenate([dzc, dzqkv], axis=1)
    comm.reduce_begin("mix", {"w_out": dwout, "w_in": _tn_matmul("mix_dwin", h2, dz)})
    dx1, dx1b, G["mix_norm_g"] = _nt_rms_bwd("mix_dx", [dz], [win], x1, comm.tie(S["mix_norm_g"]), dx2)
    dgate1, dup1 = _nt_matmul("ffn1_dact", dx1b, wd1, 0.5, gate1, up1)
    dwd1 = _tn_matmul("ffn1_dwd", a1, dx1b, 0.5)
    dwg1 = _tn_matmul("ffn1_dwg", h1, dgate1)
    dwu1 = _tn_matmul("ffn1_dwu", h1, dup1)
    comm.reduce_end("mix", dwu1)
    comm.reduce_begin("ffn1", {"ffn1_w_down": dwd1, "ffn1_w_gate": dwg1, "ffn1_w_up": dwu1})
    dx0, _, G["ffn1_norm_g"] = _nt_rms_bwd("ffn1_dx", [dgate1, dup1], [wg1, wu1],
                                           x, comm.tie(S["ffn1_norm_g"]), dx1)
    comm.reduce_end("ffn1", dx0)
    return lossvec, dx0, G


BIG = (("ffn1_w_gate", 1), ("ffn1_w_up", 1), ("ffn1_w_down", 0), ("w_in", 1), ("w_out", 0),
       ("ffn2_w_gate", 1), ("ffn2_w_up", 1), ("ffn2_w_down", 0))
AXIS = dict(BIG)
FLIPS = ((1, 0), (0, 1), (1, 1))
HBM = pl.BlockSpec(memory_space=pltpu.HBM)
SEM = pl.BlockSpec(memory_space=pltpu.SEMAPHORE)
EFFECT = pltpu.SideEffectType.DATAFLOW_SIDE_EFFECTING
TOKEN = jax.ShapeDtypeStruct((8, LANES), F32)


def _window(ref, shape, axis, slab=None, half=None):
    idx = [pl.ds(0, shape[0]), pl.ds(0, shape[1])]
    if slab is not None:
        n = shape[axis] // 4
        idx[axis] = pl.ds(pl.multiple_of(slab * n, 8), n)
    if half is not None:
        hs = shape[1 - axis] // 2
        idx[1 - axis] = pl.ds(pl.multiple_of(half * hs, 8), hs)
    return ref.at[idx[0], idx[1]]


def _position():
    return lax.axis_index("x"), lax.axis_index("y"), lax.axis_index("c")


def _half_shape(shape, axis):
    return (shape[0] // 2, shape[1]) if axis == 1 else (shape[0], shape[1] // 2)


def _slab_shape(shape, axis):
    return (shape[0], shape[1] // 4) if axis == 1 else (shape[0] // 4, shape[1])


def _piece_shape(shape, axis):
    return _half_shape(_slab_shape(shape, axis), axis)


def _full_shape(shard, axis):
    return (shard.shape[0], shard.shape[1] * 4) if axis == 1 else (shard.shape[0] * 4, shard.shape[1])


def _hbm(a):
    return pltpu.with_memory_space_constraint(a, pltpu.HBM)


def _remote(src, dst, send_sem, recv_sem, to):
    return pltpu.make_async_remote_copy(src_ref=src, dst_ref=dst, send_sem=send_sem, recv_sem=recv_sem,
                                        device_id=to, device_id_type=MESH)


def _place(name, pos, w, axis):
    R, Cc = w.shape
    tr = _pick(R, (256, 128, 64, 32, 16))
    nrb = R // tr

    def body(pos_ref, w_ref, o_ref):
        o_ref[...] = w_ref[...].astype(BF16)

    omap = (lambda i, p: (i, p[0])) if axis == 1 else (lambda i, p: (p[0] * nrb + i, 0))
    return pl.pallas_call(
        body, name=name,
        grid_spec=pltpu.PrefetchScalarGridSpec(
            num_scalar_prefetch=1, grid=(nrb,), in_specs=[pl.BlockSpec((tr, Cc), lambda i, p: (i, 0))],
            out_specs=pl.BlockSpec((tr, Cc), omap)),
        out_shape=jax.ShapeDtypeStruct(_full_shape(w, axis), BF16), compiler_params=_params(1),
    )(pos, w)


def _gather_now(name, axes, fulls):
    nt = len(fulls)
    shapes = [f.shape for f in fulls]

    def body(*refs):
        outs, token = refs[nt:2 * nt], refs[2 * nt]
        send_sems, recv_sems = refs[2 * nt + 1:]
        x, y, c = _position()
        j0 = 2 * x + y
        sib = (x, y, 1 - c)

        def copy(t, k, slab, half, to):
            win = _window(outs[t], shapes[t], axes[t], slab=slab, half=half)
            return _remote(win, win, send_sems.at[t, k], recv_sems.at[t, k], to)

        sends = []
        for k, (fx, fy) in enumerate(FLIPS):
            for t in range(nt):
                cp = copy(t, k, j0, c, (x ^ fx, y ^ fy, c))
                cp.start()
                sends.append(cp)
        for k, (fx, fy) in enumerate(FLIPS):
            js = 2 * (x ^ fx) + (y ^ fy)
            for t in range(nt):
                copy(t, k, js, c, sib).wait_recv()
                cp = copy(t, 3 + k, js, c, sib)
                cp.start()
                sends.append(cp)
        for k, (fx, fy) in enumerate(FLIPS):
            js = 2 * (x ^ fx) + (y ^ fy)
            for t in range(nt):
                copy(t, 3 + k, js, 1 - c, sib).wait_recv()
        for cp in sends:
            cp.wait_send()
        token[...] = jnp.zeros_like(token)

    res = pl.pallas_call(
        body, name=name, in_specs=[ANY] * nt,
        out_specs=[ANY] * nt + [pl.BlockSpec(memory_space=pltpu.VMEM)],
        out_shape=[jax.ShapeDtypeStruct(s, BF16) for s in shapes] + [TOKEN],
        input_output_aliases={t: t for t in range(nt)},
        scratch_shapes=[pltpu.SemaphoreType.DMA((nt, 6)), pltpu.SemaphoreType.DMA((nt, 6))],
    )(*fulls)
    return list(res[:nt]), res[nt]


def _split_start(name, arrays, ncopies, plan):
    na = len(arrays)

    def body(*refs):
        ins = refs[:na]
        send_sems, recv_sems = refs[na], refs[na + 1]
        token = refs[-1]
        x, y, c = _position()
        for i, (src, dst, to) in enumerate(plan(ins, x, y, c)):
            _remote(src, dst, send_sems.at[i], recv_sems.at[i], to).start()
        token[...] = jnp.zeros_like(token)

    res = pl.pallas_call(
        body, name=name, in_specs=[HBM] * na,
        out_specs=tuple([SEM, SEM] + [HBM] * na + [pl.BlockSpec(memory_space=pltpu.VMEM)]),
        out_shape=tuple([pltpu.SemaphoreType.DMA((ncopies,)), pltpu.SemaphoreType.DMA((ncopies,))]
                        + [pltpu.HBM(a.shape, a.dtype) for a in arrays] + [TOKEN]),
        input_output_aliases={i: 2 + i for i in range(na)},
        compiler_params=pltpu.CompilerParams(has_side_effects=EFFECT),
    )(*[_hbm(a) for a in arrays])
    return (res[0], res[1]), list(res[2:2 + na]), res[-1]


def _split_wait(name, arrays, sems, after, plan):
    na = len(arrays)

    def body(*refs):
        ins = refs[:na]
        send_sems, recv_sems = refs[na], refs[na + 1]
        x, y, c = _position()
        for i, (src, dst, to) in enumerate(plan(ins, x, y, c)):
            cp = _remote(src, dst, send_sems.at[i], recv_sems.at[i], to)
            cp.wait_send()
            cp.wait_recv()

    res = pl.pallas_call(
        body, name=name, in_specs=[HBM] * na + [SEM, SEM, ANY],
        out_specs=tuple([HBM] * na), out_shape=tuple(pltpu.HBM(a.shape, a.dtype) for a in arrays),
        input_output_aliases={i: i for i in range(na)},
        compiler_params=pltpu.CompilerParams(has_side_effects=EFFECT),
    )(*arrays, *sems, after)
    return list(res)


def _gather_plan(axes, shapes, conv_shape):
    nt = len(axes)

    def plan(refs, x, y, c):
        j0 = 2 * x + y
        out = []
        for fx, fy in FLIPS:
            to = (x ^ fx, y ^ fy, c)
            for t in range(nt):
                win = _window(refs[t], shapes[t], axes[t], slab=j0, half=c)
                out.append((win, win, to))
            if conv_shape is not None:
                win = _window(refs[nt], conv_shape, 1, slab=j0)
                out.append((win, win, to))
        return out

    return plan


def _gather_finish(name, axes, fulls):
    nt = len(axes)
    shapes = [f.shape for f in fulls]

    def body(*refs):
        outs = refs[nt:2 * nt]
        send_sems, recv_sems = refs[2 * nt:]
        x, y, c = _position()
        sib = (x, y, 1 - c)
        cps = []
        for k, (fx, fy) in enumerate(FLIPS):
            js = 2 * (x ^ fx) + (y ^ fy)
            for t in range(nt):
                landed = _window(outs[t], shapes[t], axes[t], slab=js, half=c)
                cp = _remote(landed, landed, send_sems.at[t, k], recv_sems.at[t, k], sib)
                cp.start()
                cps.append(cp)
        for k, (fx, fy) in enumerate(FLIPS):
            js = 2 * (x ^ fx) + (y ^ fy)
            for t in range(nt):
                other = _window(outs[t], shapes[t], axes[t], slab=js, half=1 - c)
                _remote(other, other, send_sems.at[t, k], recv_sems.at[t, k], sib).wait_recv()
        for cp in cps:
            cp.wait_send()

    res = pl.pallas_call(
        body, name=name, in_specs=[ANY] * nt, out_specs=[ANY] * nt,
        out_shape=[jax.ShapeDtypeStruct(f.shape, f.dtype) for f in fulls],
        input_output_aliases={t: t for t in range(nt)},
        scratch_shapes=[pltpu.SemaphoreType.DMA((nt, 3)), pltpu.SemaphoreType.DMA((nt, 3))],
    )(*fulls)
    return list(res)


def _pair_exchange(name, srcs, windows, out_shapes, dtype):
    nt = len(srcs)

    def body(*refs):
        ins, outs = refs[:nt], refs[nt:2 * nt]
        send_sems, recv_sems = refs[2 * nt:]
        x, y, c = _position()
        cps = []
        for t in range(nt):
            cp = _remote(windows[t](ins[t], c), outs[t], send_sems.at[t], recv_sems.at[t], (x, y, 1 - c))
            cp.start()
            cps.append(cp)
        for cp in cps:
            cp.wait()

    return pl.pallas_call(
        body, name=name, in_specs=[ANY] * nt, out_specs=[ANY] * nt,
        out_shape=[jax.ShapeDtypeStruct(s, dtype) for s in out_shapes],
        scratch_shapes=[pltpu.SemaphoreType.DMA((nt,)), pltpu.SemaphoreType.DMA((nt,))],
    )(*srcs)


def _scatter_plan(axes, shapes):
    nt = len(axes)

    def plan(refs, x, y, c):
        out = []
        for k, (fx, fy) in enumerate(FLIPS):
            js = 2 * (x ^ fx) + (y ^ fy)
            for t in range(nt):
                src = _window(refs[t], _half_shape(shapes[t], axes[t]), axes[t], slab=js)
                out.append((src, refs[nt + t].at[k], (x ^ fx, y ^ fy, c)))
        return out

    return plan


def _gather_small(packed):
    R, Cc = packed.shape

    def body(p_ref, o_ref, send_sems, recv_sems, loc_sem):
        x, y, c = _position()
        me = 4 * x + 2 * y + c
        mine = pltpu.make_async_copy(p_ref, o_ref.at[me], loc_sem)
        mine.start()
        cps = []
        for k in range(1, 8):
            fx, fy, fc = (k >> 2) & 1, (k >> 1) & 1, k & 1
            cp = pltpu.make_async_remote_copy(
                src_ref=p_ref, dst_ref=o_ref.at[me], send_sem=send_sems.at[k - 1], recv_sem=recv_sems.at[k - 1],
                device_id=(x ^ fx, y ^ fy, c ^ fc), device_id_type=MESH)
            cp.start()
            cps.append(cp)
        for cp in cps:
            cp.wait()
        mine.wait()

    return pl.pallas_call(
        body, name="gather_small_grads", in_specs=[ANY], out_specs=ANY,
        out_shape=jax.ShapeDtypeStruct((8, R, Cc), F32),
        scratch_shapes=[pltpu.SemaphoreType.DMA((7,)), pltpu.SemaphoreType.DMA((7,)), pltpu.SemaphoreType.DMA],
    )(packed)


def _sum_slots(name, slots):
    n, R, Cc = slots.shape

    def body(s_ref, o_ref):
        t = s_ref[0]
        for i in range(1, n):
            t = t + s_ref[i]
        o_ref[...] = t

    return pl.pallas_call(
        body, name=name, grid=(1,), in_specs=[pl.BlockSpec((n, R, Cc), lambda i: (0, 0, 0))],
        out_specs=pl.BlockSpec((R, Cc), lambda i: (0, 0)), out_shape=jax.ShapeDtypeStruct((R, Cc), F32),
        compiler_params=_params(1),
    )(slots)


def _pair_sum(name, pos, g, land, shape, axis):
    hshape = _half_shape(shape, axis)
    R, Cc = hshape
    tr = _pick(R, (256, 128, 64, 32, 16))
    nrb = R // tr

    def body(pos_ref, g_ref, l_ref, o_ref):
        o_ref[...] = (g_ref[...].astype(F32) + l_ref[...].astype(F32)).astype(BF16)

    if axis == 1:
        gmap = lambda i, p: (p[1] * nrb + i, 0)
    else:
        gmap = lambda i, p: (i, p[1])
    blk = pl.BlockSpec((tr, Cc), lambda i, p: (i, 0))
    return pl.pallas_call(
        body, name=name,
        grid_spec=pltpu.PrefetchScalarGridSpec(
            num_scalar_prefetch=1, grid=(nrb,), in_specs=[pl.BlockSpec((tr, Cc), gmap), blk], out_specs=blk),
        out_shape=jax.ShapeDtypeStruct(hshape, BF16), compiler_params=_params(1),
    )(pos, g, land)


def _chip_sum(name, pos, sb, land, shape, axis):
    hshape = _half_shape(shape, axis)
    pshape = _piece_shape(shape, axis)
    R, Cc = pshape
    tr = _pick(R, (256, 128, 64, 32, 16))
    nrb = R // tr

    def body(pos_ref, s_ref, l_ref, o_ref):
        t = s_ref[...].astype(F32)
        for k in range(3):
            t = t + l_ref[k].astype(F32)
        o_ref[...] = t

    if axis == 1:
        smap = lambda i, p: (i, p[0])
    else:
        smap = lambda i, p: (p[0] * nrb + i, 0)
    return pl.pallas_call(
        body, name=name,
        grid_spec=pltpu.PrefetchScalarGridSpec(
            num_scalar_prefetch=1, grid=(nrb,),
            in_specs=[pl.BlockSpec((tr, Cc), smap), pl.BlockSpec((3, tr, Cc), lambda i, p: (0, i, 0))],
            out_specs=pl.BlockSpec((tr, Cc), lambda i, p: (i, 0))),
        out_shape=jax.ShapeDtypeStruct(pshape, F32), compiler_params=_params(1),
    )(pos, sb, land)


def _adam_math(w, g, m, v):
    m = ADAM_B1 * m + (1.0 - ADAM_B1) * g
    v = ADAM_B2 * v + (1.0 - ADAM_B2) * (g * g)
    m_hat = m / (1.0 - ADAM_B1 ** ADAM_STEP)
    v_hat = v / (1.0 - ADAM_B2 ** ADAM_STEP)
    delta = -ADAM_LR * (m_hat / (jnp.sqrt(v_hat) + ADAM_EPS) + ADAM_WD * w)
    return delta, m, v


def _adamw_halves(name, pos, w, m, v, mine, theirs, axis):
    R, Cc = w.shape
    hr, hc = mine.shape
    tr = _pick(hr, (256, 128, 64, 32, 16))
    nrb = hr // tr

    def body(pos_ref, w_ref, m_ref, v_ref, a_ref, b_ref, g_ref, d_ref, nm_ref, nv_ref):
        half = pl.program_id(0)
        g = jnp.where(half == pos_ref[1], a_ref[...], b_ref[...])
        d, nm, nv = _adam_math(w_ref[...], g, m_ref[...], v_ref[...])
        g_ref[...] = g
        d_ref[...] = d
        nm_ref[...] = nm
        nv_ref[...] = nv

    if axis == 1:
        wmap = lambda h, i, p: (h * nrb + i, 0)
    else:
        wmap = lambda h, i, p: (i, h)
    wblk = pl.BlockSpec((tr, hc), wmap)
    ablk = pl.BlockSpec((tr, hc), lambda h, i, p: (jnp.where(h == p[1], i, 0), 0))
    bblk = pl.BlockSpec((tr, hc), lambda h, i, p: (jnp.where(h == p[1], 0, i), 0))
    return pl.pallas_call(
        body, name=name,
        grid_spec=pltpu.PrefetchScalarGridSpec(
            num_scalar_prefetch=1, grid=(2, nrb), in_specs=[wblk, wblk, wblk, ablk, bblk], out_specs=[wblk] * 4),
        out_shape=[jax.ShapeDtypeStruct((R, Cc), F32)] * 4, compiler_params=_params(2),
    )(pos, w, m, v, mine, theirs)


def _adamw_small(name, w, g, m, v):
    def body(w_ref, g_ref, m_ref, v_ref, d_ref, nm_ref, nv_ref):
        d, nm, nv = _adam_math(w_ref[...], g_ref[...], m_ref[...], v_ref[...])
        d_ref[...] = d
        nm_ref[...] = nm
        nv_ref[...] = nv

    blk = pl.BlockSpec(w.shape, lambda i: (0, 0))
    return pl.pallas_call(
        body, name=name, grid=(1,), in_specs=[blk] * 4, out_specs=[blk] * 3,
        out_shape=[jax.ShapeDtypeStruct(w.shape, F32)] * 3, compiler_params=_params(1),
    )(w, g, m, v)


SMALL = ("ffn1_norm_g", "mix_norm_g", "conv_b_dw", "conv_ln_g", "conv_ln_b", "q_norm_g", "k_norm_g", "ffn2_norm_g")
ORDER = ("ffn1_norm_g", "ffn1_w_gate", "ffn1_w_up", "ffn1_w_down", "mix_norm_g", "w_in", "conv_w_dw", "conv_b_dw",
         "conv_ln_g", "conv_ln_b", "q_norm_g", "k_norm_g", "w_out", "ffn2_norm_g", "ffn2_w_gate", "ffn2_w_up",
         "ffn2_w_down")
GATHER_FIRST = ("ffn1_w_gate", "ffn1_w_up")
GATHER_SECOND = ("ffn1_w_down", "w_in")
GATHER_THIRD = ("w_out", "ffn2_w_gate", "ffn2_w_up", "ffn2_w_down")


class _Exchange:
    def __init__(self, P, Mo, Vo, conv_shard, pos):
        self.P, self.Mo, self.Vo, self.pos = P, Mo, Vo, pos
        self.tokens = []
        self.pending = {}
        self.reducing = {}
        self.results = {}
        placed = {n: _place("place_" + n, pos, P[n][0], a) for n, a in BIG}
        self.shapes = {n: placed[n].shape for n, _ in BIG}
        first, tok = _gather_now("gather_first", [AXIS[n] for n in GATHER_FIRST], [placed[n] for n in GATHER_FIRST])
        self.ready = dict(zip(GATHER_FIRST, first))
        cq = conv_shard.shape[1]
        conv_full = lax.dynamic_update_slice(jnp.zeros((conv_shard.shape[0], 4 * cq), F32), conv_shard,
                                             (0, pos[0] * cq))
        for gname, names, conv in (("second", GATHER_SECOND, conv_full), ("third", GATHER_THIRD, None)):
            axes = [AXIS[n] for n in names]
            shapes = [self.shapes[n] for n in names]
            arrays = [placed[n] for n in names] + ([conv] if conv is not None else [])
            small = min(range(len(arrays)), key=lambda i: arrays[i].size)
            arrays[small] = arrays[small] + tok[0, 0].astype(arrays[small].dtype)
            plan = _gather_plan(axes, shapes, conv.shape if conv is not None else None)
            sems, thru, tok = _split_start("gather_%s_start" % gname, arrays, 3 * len(arrays), plan)
            self.tokens.append(tok)
            for n in names + (("conv_w32",) if conv is not None else ()):
                self.pending[n] = (gname, names, axes, plan, sems, thru, conv is not None)

    def tie(self, v):
        for tok in self.tokens:
            v = v + tok[0:1, 0:1]
        self.tokens = []
        return v

    def weights(self, names, after):
        if names[0] in self.pending:
            gname, gnames, axes, plan, sems, thru, has_conv = self.pending[names[0]]
            thru = _split_wait("gather_%s_wait" % gname, thru, sems, after, plan)
            nt = len(gnames)
            fulls = _gather_finish("gather_%s_finish" % gname, axes, thru[:nt])
            for n, f in zip(gnames, fulls):
                self.ready[n] = f
                del self.pending[n]
            if has_conv:
                self.ready["conv_w32"] = thru[nt]
                del self.pending["conv_w32"]
        return [self.ready[n] for n in names]

    def reduce_begin(self, gname, grads):
        names = list(grads)
        axes = [AXIS[n] for n in names]
        shapes = [self.shapes[n] for n in names]
        gs = [grads[n] for n in names]
        to_sibling = [(lambda ref, c, s=s, a=a: _window(ref, s, a, half=1 - c)) for s, a in zip(shapes, axes)]
        landed = _pair_exchange("pair_exchange_" + gname, gs, to_sibling,
                                [_half_shape(s, a) for s, a in zip(shapes, axes)], BF16)
        sbs = [_pair_sum("pair_sum_" + n, self.pos, g, l, s, a)
               for n, a, g, l, s in zip(names, axes, gs, landed, shapes)]
        lands = [lax.empty((3,) + _piece_shape(s, a), BF16) for s, a in zip(shapes, axes)]
        plan = _scatter_plan(axes, shapes)
        sems, thru, tok = _split_start("scatter_%s_start" % gname, sbs + lands, 3 * len(names), plan)
        self.tokens.append(tok)
        self.reducing[gname] = (names, axes, shapes, plan, sems, thru)

    def reduce_end(self, gname, after):
        names, axes, shapes, plan, sems, thru = self.reducing.pop(gname)
        nt = len(names)
        thru = _split_wait("scatter_%s_wait" % gname, thru, sems, after, plan)
        mine = [_chip_sum("chip_sum_" + n, self.pos, sb, l, s, a)
                for n, a, sb, l, s in zip(names, axes, thru[:nt], thru[nt:], shapes)]
        theirs = _pair_exchange("half_exchange_" + gname, mine, [(lambda ref, c: ref)] * nt,
                                [m.shape for m in mine], F32)
        for n, a, mi, th in zip(names, axes, mine, theirs):
            g, d, nm, nv = _adamw_halves("adamw_" + n, self.pos, self.P[n][0], self.Mo[n][0], self.Vo[n][0],
                                         mi, th, a)
            self.results[n] = (g[None], d[None], nm[None], nv[None])


def kernel(x, ffn1_norm_g, ffn1_w_gate, ffn1_w_up, ffn1_w_down, mix_norm_g, w_in, conv_w_dw, conv_b_dw, conv_ln_g, conv_ln_b, q_norm_g, k_norm_g, w_out, ffn2_norm_g, ffn2_w_gate, ffn2_w_up, ffn2_w_down, loss_target, m_ffn1_norm_g, m_ffn1_w_gate, m_ffn1_w_up, m_ffn1_w_down, m_mix_norm_g, m_w_in, m_conv_w_dw, m_conv_b_dw, m_conv_ln_g, m_conv_ln_b, m_q_norm_g, m_k_norm_g, m_w_out, m_ffn2_norm_g, m_ffn2_w_gate, m_ffn2_w_up, m_ffn2_w_down, v_ffn1_norm_g, v_ffn1_w_gate, v_ffn1_w_up, v_ffn1_w_down, v_mix_norm_g, v_w_in, v_conv_w_dw, v_conv_b_dw, v_conv_ln_g, v_conv_ln_b, v_q_norm_g, v_k_norm_g, v_w_out, v_ffn2_norm_g, v_ffn2_w_gate, v_ffn2_w_up, v_ffn2_w_down):
    args = dict(locals())
    P = {n: args[n] for n in ORDER}
    Mo = {n: args["m_" + n] for n in ORDER}
    Vo = {n: args["v_" + n] for n in ORDER}
    xs = x[0]
    tgt = loss_target[0]
    T, D = xs.shape
    hd = q_norm_g.shape[-1]
    C = conv_b_dw.shape[-1]
    ntap = conv_w_dw.shape[1]
    cx, cy, cc = _position()
    j0 = 2 * cx + cy
    pos = jnp.stack([j0, cc]).astype(jnp.int32)

    conv_shard = jnp.pad(conv_w_dw[0], ((0, HALO - ntap), (0, 0)))
    comm = _Exchange(P, Mo, Vo, conv_shard, pos)
    lossvec, dx0, G = _local_step(xs, tgt, {n: P[n] for n in SMALL}, comm, hd)
    loss = lax.psum(0.5 / D * jnp.sum(lossvec), AXES)
    grads, deltas, new_m, new_v = {}, {}, {}, {}
    for n, _ in BIG:
        grads[n], deltas[n], new_m[n], new_v[n] = comm.results[n]

    rows = [G["conv_w32"]]
    for n in ("ffn1_norm_g", "mix_norm_g", "ffn2_norm_g"):
        rows.append(G[n].reshape(D // C, C))
    for n in ("conv_b_dw", "conv_ln_g", "conv_ln_b", "q_norm_g", "k_norm_g"):
        rows.append(G[n])
    packed = jnp.concatenate(rows, axis=0)
    packed = jnp.pad(packed, ((0, -packed.shape[0] % 8), (0, 0)))
    total = _sum_slots("sum_small_grads", _gather_small(packed))
    r = HALO
    small_g = {}
    cq = C // 4
    small_g["conv_w_dw"] = lax.dynamic_slice(total[:ntap], (0, j0 * cq), (ntap, cq))
    for n in ("ffn1_norm_g", "mix_norm_g", "ffn2_norm_g"):
        small_g[n] = total[r:r + D // C].reshape(1, D)
        r += D // C
    for n in ("conv_b_dw", "conv_ln_g", "conv_ln_b"):
        small_g[n] = total[r:r + 1]
        r += 1
    for n in ("q_norm_g", "k_norm_g"):
        small_g[n] = total[r:r + 1, :hd]
        r += 1
    for n in ("conv_w_dw",) + SMALL:
        lead = n == "conv_w_dw"
        w2, m2, v2 = (P[n][0], Mo[n][0], Vo[n][0]) if lead else (P[n], Mo[n], Vo[n])
        d, nm, nv = _adamw_small("adamw_" + n, w2, small_g[n], m2, v2)
        if lead:
            grads[n], deltas[n], new_m[n], new_v[n] = small_g[n][None], d[None], nm[None], nv[None]
        else:
            grads[n], deltas[n], new_m[n], new_v[n] = small_g[n], d, nm, nv

    return (loss, dx0[None], *[grads[n] for n in ORDER], *[deltas[n] for n in ORDER],
            *[new_m[n] for n in ORDER], *[new_v[n] for n in ORDER])
```

```python
import jax
import jax.numpy as jnp
from jax import lax
from jax.experimental import pallas as pl
from jax.experimental.pallas import tpu as pltpu

F32 = jnp.float32
BF16 = jnp.bfloat16
EPS = 1e-6
WINDOW = 128
DILATIONS = (1, 4, 16)
ALIBI_MAX_BIAS = 8.0
LANES = 128
HALO = 32
ADAM_LR, ADAM_B1, ADAM_B2, ADAM_EPS, ADAM_WD, ADAM_STEP = 0.001, 0.9, 0.999, 1e-08, 0.01, 10
VMEM_LIMIT_MB = 62
ROW_TILE = 1024
TN_ACC_ELEMS = 3 * 1024 * 1024
EPILOGUE_ROWS = 256
ACC_COLS = 512
MESH = pl.DeviceIdType.MESH
ANY = pl.BlockSpec(memory_space=pl.ANY)
AXES = ("x", "y", "c")
NEG = -1e30


def _pick(n, cands):
    for c in cands:
        if n % c == 0:
            return c
    return n


def _params(nsem):
    return pltpu.CompilerParams(dimension_semantics=("arbitrary",) * nsem,
                                vmem_limit_bytes=VMEM_LIMIT_MB << 20)


def _nn(a, b):
    return jnp.dot(a, b, preferred_element_type=F32)


def _nt(a, b):
    return lax.dot_general(a, b, (((1,), (1,)), ((), ())), preferred_element_type=F32)


def _tn(a, b):
    return lax.dot_general(a, b, (((0,), (0,)), ((), ())), preferred_element_type=F32)


def _sigmoid(v):
    return jax.nn.sigmoid(v)


def _rms_r(xv):
    return lax.rsqrt(jnp.mean(xv * xv, axis=-1, keepdims=True) + EPS)


def _norm_matmul(name, x, g, ws, swiglu):
    T, D = x.shape
    N = ws[0].shape[1]
    tm = _pick(T, (ROW_TILE, 512, 256, 128))
    tn = _pick(N, (512, 256, 128))
    nw = len(ws)

    def body(*refs):
        x_ref, g_ref = refs[:2]
        w_refs = refs[2:2 + nw]
        outs = refs[2 + nw:-1]
        hs = refs[-1]

        @pl.when(pl.program_id(1) == 0)
        def _():
            for r0 in range(0, tm, EPILOGUE_ROWS):
                rows = slice(r0, r0 + min(EPILOGUE_ROWS, tm))
                xv = x_ref[rows, :]
                hv = (xv * _rms_r(xv) * g_ref[...]).astype(BF16)
                hs[rows, :] = hv
                outs[0][rows, :] = hv

        h = hs[...]
        if swiglu:
            gt = _nn(h, w_refs[0][...])
            u = _nn(h, w_refs[1][...])
            sg = _sigmoid(gt)
            silu = gt * sg
            outs[1][...] = (u * (sg * (1.0 + gt * (1.0 - sg)))).astype(BF16)
            outs[2][...] = silu.astype(BF16)
            outs[3][...] = (silu * u).astype(BF16)
        else:
            outs[1][...] = _nn(h, w_refs[0][...])

    row = pl.BlockSpec((tm, D), lambda i, j: (i, 0))
    col = pl.BlockSpec((D, tn), lambda i, j: (0, j))
    tile = pl.BlockSpec((tm, tn), lambda i, j: (i, j))
    if swiglu:
        out_shape = [jax.ShapeDtypeStruct((T, D), BF16)] + [jax.ShapeDtypeStruct((T, N), BF16)] * 3
        out_specs = [row, tile, tile, tile]
    else:
        out_shape = [jax.ShapeDtypeStruct((T, D), BF16), jax.ShapeDtypeStruct((T, N), F32)]
        out_specs = [row, tile]
    return pl.pallas_call(
        body, name=name, grid=(T // tm, N // tn),
        in_specs=[row, pl.BlockSpec((1, D), lambda i, j: (0, 0))] + [col] * nw,
        out_specs=out_specs, out_shape=out_shape,
        scratch_shapes=[pltpu.VMEM((tm, D), BF16)],
        compiler_params=_params(2),
    )(x, g, *ws)


def _matmul_res(name, a, w, res, scale, tgt=None):
    T, K = a.shape
    N = w.shape[1]
    loss = tgt is not None
    tm = _pick(T, (512, 256, 128))
    tk = _pick(K, (1408, 1024, 512, 256, 128))
    nk = K // tk

    def body(*refs):
        if loss:
            a_ref, w_ref, res_ref, tgt_ref, dx_ref, dxb_ref, lv_ref, acc = refs
        else:
            a_ref, w_ref, res_ref, out_ref, acc = refs
        i, k = pl.program_id(0), pl.program_id(1)

        @pl.when(k == 0)
        def _():
            acc[...] = jnp.zeros_like(acc)

        acc[...] += _nn(a_ref[...], w_ref[...])

        @pl.when(k == nk - 1)
        def _():
            part = jnp.zeros((1, N), F32)
            for r0 in range(0, tm, EPILOGUE_ROWS):
                rows = slice(r0, r0 + min(EPILOGUE_ROWS, tm))
                val = res_ref[rows, :] + scale * acc[rows, :]
                if loss:
                    dv = val - tgt_ref[rows, :]
                    dx = dv * (1.0 / N)
                    dx_ref[rows, :] = dx
                    dxb_ref[rows, :] = dx.astype(BF16)
                    part = part + jnp.sum(dv * dv, axis=0, keepdims=True)
                else:
                    out_ref[rows, :] = val
            if loss:
                @pl.when(i == 0)
                def _():
                    lv_ref[...] = part

                @pl.when(i > 0)
                def _():
                    lv_ref[...] += part

    row = pl.BlockSpec((tm, N), lambda i, k: (i, 0))
    in_specs = [pl.BlockSpec((tm, tk), lambda i, k: (i, k)), pl.BlockSpec((tk, N), lambda i, k: (k, 0)), row]
    args = [a, w, res]
    if loss:
        in_specs.append(row)
        args.append(tgt)
        out_specs = [row, row, pl.BlockSpec((1, N), lambda i, k: (0, 0))]
        out_shape = [jax.ShapeDtypeStruct((T, N), F32), jax.ShapeDtypeStruct((T, N), BF16),
                     jax.ShapeDtypeStruct((1, N), F32)]
    else:
        out_specs = row
        out_shape = jax.ShapeDtypeStruct((T, N), F32)
    return pl.pallas_call(
        body, name=name, grid=(T // tm, nk), in_specs=in_specs, out_specs=out_specs, out_shape=out_shape,
        scratch_shapes=[pltpu.VMEM((tm, N), F32)], compiler_params=_params(2),
    )(*args)


def _nt_matmul(name, dyb, w, scale=1.0, gate=None, up=None):
    T, D = dyb.shape
    N = w.shape[0]
    tm = _pick(T, (ROW_TILE, 512, 256, 128))
    tn = _pick(N, (512, 256, 128))
    swiglu = gate is not None

    def body(*refs):
        if swiglu:
            dy_ref, w_ref, g_ref, u_ref, dg_ref, du_ref = refs
        else:
            dy_ref, w_ref, o_ref, ob_ref = refs
        da = _nt(dy_ref[...], w_ref[...]) * scale
        if swiglu:
            dg_ref[...] = (da * g_ref[...].astype(F32)).astype(BF16)
            du_ref[...] = (da * u_ref[...].astype(F32)).astype(BF16)
        else:
            o_ref[...] = da
            ob_ref[...] = da.astype(BF16)

    tile = pl.BlockSpec((tm, tn), lambda i, j: (i, j))
    in_specs = [pl.BlockSpec((tm, D), lambda i, j: (i, 0)), pl.BlockSpec((tn, D), lambda i, j: (j, 0))]
    args = [dyb, w]
    if swiglu:
        in_specs += [tile, tile]
        args += [gate, up]
        out_shape = [jax.ShapeDtypeStruct((T, N), BF16)] * 2
    else:
        out_shape = [jax.ShapeDtypeStruct((T, N), F32), jax.ShapeDtypeStruct((T, N), BF16)]
    return pl.pallas_call(
        body, name=name, grid=(T // tm, N // tn), in_specs=in_specs, out_specs=[tile, tile],
        out_shape=out_shape, compiler_params=_params(2),
    )(*args)


def _nt_rms_bwd(name, As, Ws, x, g, dres):
    T, K = As[0].shape
    D = x.shape[1]
    na = len(As)
    tm = _pick(T, (ROW_TILE, 512, 256, 128))
    tk = _pick(K, (1024 // na, 512, 256, 128))
    nk = K // tk

    def body(*refs):
        a_refs = refs[:na]
        w_refs = refs[na:2 * na]
        x_hbm, g_ref, dres_hbm, acc, dxb_ref, dg_ref, x_ref, dres_ref, sems = refs[2 * na:]
        dx_ref = acc
        i, k = pl.program_id(0), pl.program_id(1)

        def row_copies():
            rows = pl.ds(pl.multiple_of(i * tm, tm), tm)
            return (pltpu.make_async_copy(x_hbm.at[rows, :], x_ref, sems.at[0]),
                    pltpu.make_async_copy(dres_hbm.at[rows, :], dres_ref, sems.at[1]))

        @pl.when(k == 0)
        def _():
            acc[...] = jnp.zeros_like(acc)
            for cp in row_copies():
                cp.start()

        for c0 in range(0, D, ACC_COLS):
            cols = slice(c0, min(c0 + ACC_COLS, D))
            part = _nt(a_refs[0][...], w_refs[0][cols, :])
            for a_ref, w_ref in zip(a_refs[1:], w_refs[1:]):
                part = part + _nt(a_ref[...], w_ref[cols, :])
            acc[:, cols] += part

        @pl.when(k == nk - 1)
        def _():
            for cp in row_copies():
                cp.wait()
            part = jnp.zeros((1, D), F32)
            for r0 in range(0, tm, EPILOGUE_ROWS):
                rows = slice(r0, r0 + min(EPILOGUE_ROWS, tm))
                dh = acc[rows, :]
                xv = x_ref[rows, :]
                r = _rms_r(xv)
                gd = dh * g_ref[...]
                dx = dres_ref[rows, :] + r * gd - xv * (r * r * r) * jnp.mean(gd * xv, axis=-1, keepdims=True)
                dx_ref[rows, :] = dx
                dxb_ref[rows, :] = dx.astype(BF16)
                part = part + jnp.sum(dh * xv * r, axis=0, keepdims=True)

            @pl.when(i == 0)
            def _():
                dg_ref[...] = part

            @pl.when(i > 0)
            def _():
                dg_ref[...] += part

    row = pl.BlockSpec((tm, D), lambda i, k: (i, 0), pipeline_mode=pl.Buffered(1))
    vec = pl.BlockSpec((1, D), lambda i, k: (0, 0))
    return pl.pallas_call(
        body, name=name, grid=(T // tm, nk),
        in_specs=[pl.BlockSpec((tm, tk), lambda i, k: (i, k))] * na
        + [pl.BlockSpec((D, tk), lambda i, k: (0, k))] * na + [ANY, vec, ANY],
        out_specs=[row, row, vec],
        out_shape=[jax.ShapeDtypeStruct((T, D), F32), jax.ShapeDtypeStruct((T, D), BF16),
                   jax.ShapeDtypeStruct((1, D), F32)],
        scratch_shapes=[pltpu.VMEM((tm, D), F32), pltpu.VMEM((tm, D), F32), pltpu.SemaphoreType.DMA((2,))],
        compiler_params=_params(2),
    )(*As, *Ws, x, g, dres)


def _tn_matmul(name, a, b, scale=1.0):
    T, M = a.shape
    N = b.shape[1]
    tn = _pick(N, (2048, 1408, 1280, 1024, 512, 256, 128))
    tm = _pick(M, tuple(c for c in (2048, 1408, 1024, 512, 256, 128) if c * tn <= TN_ACC_ELEMS))
    tk = _pick(T, (1024, 512, 256, 128))
    nk = T // tk

    def body(a_ref, b_ref, o_ref, acc):
        k = pl.program_id(2)

        @pl.when(k == 0)
        def _():
            acc[...] = jnp.zeros_like(acc)

        for r0 in range(0, tm, ACC_COLS):
            rows = slice(r0, min(r0 + ACC_COLS, tm))
            acc[rows, :] += _tn(a_ref[:, rows], b_ref[...])

        @pl.when(k == nk - 1)
        def _():
            o_ref[...] = (acc[...] * scale).astype(BF16)

    return pl.pallas_call(
        body, name=name, grid=(M // tm, N // tn, nk),
        in_specs=[pl.BlockSpec((tk, tm), lambda i, j, k: (k, i)), pl.BlockSpec((tk, tn), lambda i, j, k: (k, j))],
        out_specs=pl.BlockSpec((tm, tn), lambda i, j, k: (i, j)),
        out_shape=jax.ShapeDtypeStruct((M, N), BF16),
        scratch_shapes=[pltpu.VMEM((tm, tn), F32)], compiler_params=_params(3),
    )(a, b)


CONV_ROWS = 128
ROW_CHUNK = 32
LANE_CHUNK = 256


SUBLANES = 8


def _fill_shifts(sh, buf, rows):
    for p in range(1, SUBLANES):
        sh[p, 0:rows - SUBLANES, :] = buf[p:p + rows - SUBLANES, :]


def _tap(buf, sh, s, n, cols):
    p = s % SUBLANES
    return buf[s:s + n, cols] if p == 0 else sh[p, s - p:s - p + n, cols]


def _conv_fwd(name, z, w32, b, lg, lb, C):
    T = z.shape[0]
    tc = CONV_ROWS
    ntap = 31
    lc = _pick(C, (LANE_CHUNK, LANES))
    rpb = tc // HALO

    def body(zc_ref, zp_ref, w_ref, b_ref, lg_ref, lb_ref, yc_ref, ycv_ref, vbuf, ybuf, vsh):
        i = pl.program_id(0)
        zc = zc_ref[...]
        zp = zp_ref[...]
        vbuf[HALO:HALO + tc, :] = zc[:, :C] * _sigmoid(zc[:, C:])
        vbuf[0:HALO, :] = jnp.where(i > 0, zp[:, :C] * _sigmoid(zp[:, C:]), 0.0)
        _fill_shifts(vsh, vbuf, tc + HALO)
        for r0 in range(0, tc, ROW_CHUNK):
            for c0 in range(0, C, lc):
                cols = slice(c0, c0 + lc)
                acc = jnp.zeros((ROW_CHUNK, lc), F32) + b_ref[:, cols]
                for k in range(ntap):
                    acc = acc + w_ref[k:k + 1, cols] * _tap(vbuf, vsh, r0 + 2 + k, ROW_CHUNK, cols)
                ybuf[r0:r0 + ROW_CHUNK, cols] = acc
        y = ybuf[...]
        ycv_ref[...] = y
        mu = jnp.mean(y, axis=-1, keepdims=True)
        yc = y - mu
        rstd = lax.rsqrt(jnp.mean(yc * yc, axis=-1, keepdims=True) + EPS)
        ln = yc * rstd * lg_ref[...] + lb_ref[...]
        yc_ref[...] = (ln * _sigmoid(ln)).astype(BF16)

    vec = pl.BlockSpec((1, C), lambda i: (0, 0))
    return pl.pallas_call(
        body, name=name, grid=(T // tc,),
        in_specs=[pl.BlockSpec((tc, 2 * C), lambda i: (i, 0)),
                  pl.BlockSpec((HALO, 2 * C), lambda i: (jnp.maximum(i * rpb - 1, 0), 0)),
                  pl.BlockSpec((HALO, C), lambda i: (0, 0)), vec, vec, vec],
        out_specs=[pl.BlockSpec((tc, C), lambda i: (i, 0))] * 2,
        out_shape=[jax.ShapeDtypeStruct((T, 2 * C), BF16), jax.ShapeDtypeStruct((T, C), F32)],
        scratch_shapes=[pltpu.VMEM((tc + HALO, C), F32), pltpu.VMEM((tc, C), F32),
                        pltpu.VMEM((SUBLANES, tc + HALO, C), F32)],
        compiler_params=_params(1),
    )(z, z, w32, b, lg, lb)


def _conv_bwd(name, z, ycv, dycat, w32, lg, lb, C):
    T = z.shape[0]
    tc = CONV_ROWS
    ntap = 31
    lc = _pick(C, (LANE_CHUNK, LANES))
    rpb = tc // HALO
    nstep = T // tc
    nhb = T // HALO

    def ln_bwd(dyc, y, lgv, lbv):
        mu = jnp.mean(y, axis=-1, keepdims=True)
        yc = y - mu
        rstd = lax.rsqrt(jnp.mean(yc * yc, axis=-1, keepdims=True) + EPS)
        yn = yc * rstd
        ln = yn * lgv + lbv
        sg = _sigmoid(ln)
        dln = dyc * (sg * (1.0 + ln * (1.0 - sg)))
        dyn = dln * lgv
        dy = rstd * (dyn - jnp.mean(dyn, axis=-1, keepdims=True)
                     - yn * jnp.mean(dyn * yn, axis=-1, keepdims=True))
        return dy, dln, yn

    def body(zc_ref, zp_ref, y_ref, yn_ref, d_ref, dn_ref, w_ref, lg_ref, lb_ref,
             dz_ref, dw_ref, db_ref, dlg_ref, dlb_ref, vbuf, dbuf, dvbuf, dwacc, vsh, dsh):
        i = pl.program_id(0)
        lgv, lbv = lg_ref[...], lb_ref[...]
        zc = zc_ref[...]
        zp = zp_ref[...]
        a = zc[:, :C]
        sgt = _sigmoid(zc[:, C:])
        vbuf[HALO:HALO + tc, :] = a * sgt
        vbuf[0:HALO, :] = jnp.where(i > 0, zp[:, :C] * _sigmoid(zp[:, C:]), 0.0)
        dy, dln, yn = ln_bwd(d_ref[...], y_ref[...], lgv, lbv)
        dbuf[0:tc, :] = dy
        dyn_, _, _ = ln_bwd(dn_ref[...], yn_ref[...], lgv, lbv)
        dbuf[tc:tc + HALO, :] = jnp.where(i < nstep - 1, dyn_, 0.0)

        @pl.when(i == 0)
        def _():
            dwacc[...] = jnp.zeros_like(dwacc)
            db_ref[...] = jnp.zeros_like(db_ref)
            dlg_ref[...] = jnp.zeros_like(dlg_ref)
            dlb_ref[...] = jnp.zeros_like(dlb_ref)

        db_ref[...] += jnp.sum(dy, axis=0, keepdims=True)
        dlg_ref[...] += jnp.sum(dln * yn, axis=0, keepdims=True)
        dlb_ref[...] += jnp.sum(dln, axis=0, keepdims=True)

        _fill_shifts(vsh, vbuf, tc + HALO)
        _fill_shifts(dsh, dbuf, tc + HALO)
        for r0 in range(0, tc, ROW_CHUNK):
            for c0 in range(0, C, lc):
                cols = slice(c0, c0 + lc)
                dcur = dbuf[r0:r0 + ROW_CHUNK, cols]
                acc = jnp.zeros((ROW_CHUNK, lc), F32)
                for k in range(ntap):
                    acc = acc + w_ref[k:k + 1, cols] * _tap(dbuf, dsh, r0 + 30 - k, ROW_CHUNK, cols)
                    prod = dcur * _tap(vbuf, vsh, r0 + 2 + k, ROW_CHUNK, cols)
                    red = prod[0:8]
                    for q in range(8, ROW_CHUNK, 8):
                        red = red + prod[q:q + 8]
                    dwacc[8 * k:8 * k + 8, cols] += red
                dvbuf[r0:r0 + ROW_CHUNK, cols] = acc
        dv = dvbuf[...]
        dz_ref[:, :C] = (dv * sgt).astype(BF16)
        dz_ref[:, C:] = (dv * a * sgt * (1.0 - sgt)).astype(BF16)

        @pl.when(i == nstep - 1)
        def _():
            for k in range(ntap):
                dw_ref[k:k + 1, :] = jnp.sum(dwacc[8 * k:8 * k + 8, :], axis=0, keepdims=True)
            dw_ref[ntap:HALO, :] = jnp.zeros((HALO - ntap, C), F32)

    vec = pl.BlockSpec((1, C), lambda i: (0, 0))
    cur = pl.BlockSpec((tc, C), lambda i: (i, 0))
    nxt = pl.BlockSpec((HALO, C), lambda i: (jnp.minimum((i + 1) * rpb, nhb - 1), 0))
    return pl.pallas_call(
        body, name=name, grid=(nstep,),
        in_specs=[pl.BlockSpec((tc, 2 * C), lambda i: (i, 0)),
                  pl.BlockSpec((HALO, 2 * C), lambda i: (jnp.maximum(i * rpb - 1, 0), 0)),
                  cur, nxt, cur, nxt, pl.BlockSpec((HALO, C), lambda i: (0, 0)), vec, vec],
        out_specs=[pl.BlockSpec((tc, 2 * C), lambda i: (i, 0)), pl.BlockSpec((HALO, C), lambda i: (0, 0)),
                   vec, vec, vec],
        out_shape=[jax.ShapeDtypeStruct((T, 2 * C), BF16), jax.ShapeDtypeStruct((HALO, C), F32),
                   jax.ShapeDtypeStruct((1, C), F32), jax.ShapeDtypeStruct((1, C), F32),
                   jax.ShapeDtypeStruct((1, C), F32)],
        scratch_shapes=[pltpu.VMEM((tc + HALO, C), F32), pltpu.VMEM((tc + HALO, C), F32),
                        pltpu.VMEM((tc, C), F32), pltpu.VMEM((8 * HALO, C), F32),
                        pltpu.VMEM((SUBLANES, tc + HALO, C), F32), pltpu.VMEM((SUBLANES, tc + HALO, C), F32)],
        compiler_params=_params(1),
    )(z, z, ycv, ycv, dycat, dycat, w32, lg, lb)


def _seg_sum(u, bmat):
    hi = u.astype(BF16)
    lo = (u - hi.astype(F32)).astype(BF16)
    parts = [_nn(hi[:, c:c + LANES], bmat) + _nn(lo[:, c:c + LANES], bmat) for c in range(0, u.shape[1], LANES)]
    return jnp.concatenate(parts, axis=1)


def _attn_prep(name, z, gq, gk, bmat, A, c0, hd):
    T = z.shape[0]
    tm = _pick(T, (256, 128))

    def body(zq_ref, zk_ref, zv_ref, gq_ref, gk_ref, b_ref, o_ref):
        bm = b_ref[...]
        for idx, (z_ref, g_ref) in enumerate(((zq_ref, gq_ref), (zk_ref, gk_ref))):
            zv = z_ref[...]
            r = lax.rsqrt(_seg_sum(zv * zv, bm) * (1.0 / hd) + EPS)
            o_ref[:, idx * A:(idx + 1) * A] = zv * r * g_ref[...]
        o_ref[:, 2 * A:] = zv_ref[...]

    vec = pl.BlockSpec((1, A), lambda i: (0, 0))
    return pl.pallas_call(
        body, name=name, grid=(T // tm,),
        in_specs=[pl.BlockSpec((tm, A), lambda i: (i, c0)), pl.BlockSpec((tm, A), lambda i: (i, c0 + 1)),
                  pl.BlockSpec((tm, A), lambda i: (i, c0 + 2)), vec, vec,
                  pl.BlockSpec((LANES, LANES), lambda i: (0, 0))],
        out_specs=pl.BlockSpec((tm, 3 * A), lambda i: (i, 0)),
        out_shape=jax.ShapeDtypeStruct((T, 3 * A), F32), compiler_params=_params(1),
    )(z, z, z, gq, gk, bmat)


QK_SCALE = 0.125
ATTN_UNROLL = 4


def _fill_bias(bias, sl_ref, hp, d):
    qi = lax.broadcasted_iota(jnp.int32, (WINDOW, 2 * WINDOW), 0)
    kj = lax.broadcasted_iota(jnp.int32, (WINDOW, 2 * WINDOW), 1)
    dist = WINDOW + qi - kj
    inband = (dist >= 0) & (dist <= WINDOW)
    distf = dist.astype(F32)
    for hh in range(2):
        b = jnp.where(inband, -(sl_ref[2 * hp + hh] * d) * distf, NEG)
        bias[2 * hh + 1] = b
        bias[2 * hh] = jnp.where(kj >= WINDOW, b, NEG)


CHUNK = WINDOW * DILATIONS[-1]


def _deinterleave(dst, src, d, rows, dst_pitch, dst_off, src_off):
    for r in range(d):
        if d == 1:
            val = src[src_off:src_off + rows, :]
        else:
            val = src[pl.ds(src_off + r, rows, stride=d), :]
        lo = r * dst_pitch + dst_off
        dst[lo:lo + rows, :] = val.astype(dst.dtype)


def _interleave_add(dst, start, src, d, rows, src_pitch, src_off):
    for r in range(d):
        lo = r * src_pitch + src_off
        idx = pl.ds(start, rows) if d == 1 else pl.ds(start + r, rows, stride=d)
        dst[idx, :] += src[lo:lo + rows, :]


def _attn_fwd(name, qkv, slopes, ycat, A):
    T = qkv.shape[0]
    hpn, nch, nblk = A // LANES, T // CHUNK, CHUNK // WINDOW
    nbranch = len(DILATIONS)
    yoff = (ycat.shape[1] - A) // LANES

    def body(*refs):
        sl_ref, q_ref, k_ref, kp_ref, v_ref, vp_ref, _, y_ref, lg_ref, qd, kd, vd, od, ld, bias = refs[:15]
        onat, lnat = refs[15:15 + nbranch], refs[15 + nbranch:]
        hp, ch = pl.program_id(0), pl.program_id(1)
        lane = lax.broadcasted_iota(jnp.int32, (1, LANES), 1)
        first = lane < (LANES // 2)
        for bi, d in enumerate(DILATIONS):
            Ld = CHUNK // d
            seg = Ld + WINDOW
            nbr = Ld // WINDOW
            _deinterleave(qd, q_ref, d, Ld, Ld, 0, 0)
            for dst, cur, prev in ((kd, k_ref, kp_ref), (vd, v_ref, vp_ref)):
                _deinterleave(dst, prev, d, WINDOW, seg, 0, CHUNK - WINDOW * d)
                _deinterleave(dst, cur, d, Ld, seg, WINDOW, 0)
            _fill_bias(bias, sl_ref, hp, d)
            ob, lb = (onat[bi], lnat[bi]) if d == 1 else (od, ld)

            def step(it, carry, Ld=Ld, seg=seg, nbr=nbr, ob=ob, lb=lb):
                r, nl = it // nbr, it % nbr
                q0 = pl.multiple_of(r * Ld + nl * WINDOW, WINDOW)
                k0 = pl.multiple_of(r * seg + nl * WINDOW, WINDOW)
                later = jnp.where(ch * nbr + nl > 0, 1, 0)
                qb = qd[pl.ds(q0, WINDOW), :]
                k2 = kd[pl.ds(k0, 2 * WINDOW), :]
                v2 = vd[pl.ds(k0, 2 * WINDOW), :]
                res = []
                for hh in range(2):
                    mh = first if hh == 0 else jnp.logical_not(first)
                    s = _nt(jnp.where(mh, qb, jnp.zeros_like(qb)), k2) + bias[2 * hh + later]
                    mx = jnp.max(s, axis=-1, keepdims=True)
                    p = jnp.exp(s - mx)
                    den = jnp.sum(p, axis=-1, keepdims=True)
                    res.append((_nn(p.astype(BF16), v2) / den, mx + jnp.log(den)))
                ob[pl.ds(q0, WINDOW), :] = jnp.where(first, res[0][0], res[1][0])
                lb[pl.ds(q0, WINDOW), :] = jnp.where(first, res[0][1], res[1][1])
                return carry

            lax.fori_loop(0, nblk, step, 0, unroll=ATTN_UNROLL)
            if d > 1:
                for r in range(d):
                    onat[bi][pl.ds(r, Ld, stride=d), :] = od[r * Ld:(r + 1) * Ld, :]
                    lnat[bi][pl.ds(r, Ld, stride=d), :] = ld[r * Ld:(r + 1) * Ld, :]
        ls = [l[...] for l in lnat]
        mx = ls[0]
        for v in ls[1:]:
            mx = jnp.maximum(mx, v)
        es = [jnp.exp(v - mx) for v in ls]
        den = es[0]
        for e in es[1:]:
            den = den + e
        out = es[0] * onat[0][...]
        for e, o in zip(es[1:], onat[1:]):
            out = out + e * o[...]
        y_ref[...] = (out / den).astype(BF16)
        lg_ref[...] = mx + jnp.log(den)

    blk = lambda m: pl.BlockSpec((CHUNK, LANES), m)
    cur = lambda which: blk(lambda hp, ch: (ch, which * hpn + hp))
    prev = lambda which: blk(lambda hp, ch: (jnp.maximum(ch - 1, 0), which * hpn + hp))
    omap = blk(lambda hp, ch: (ch, hp))
    f32buf = pltpu.VMEM((CHUNK, LANES), F32)
    return pl.pallas_call(
        body, name=name, grid=(hpn, nch),
        in_specs=[pl.BlockSpec(memory_space=pltpu.SMEM), cur(0), cur(1), prev(1), cur(2), prev(2), ANY],
        out_specs=[blk(lambda hp, ch: (ch, yoff + hp)), omap],
        out_shape=[jax.ShapeDtypeStruct(ycat.shape, BF16), jax.ShapeDtypeStruct((T, A), F32)],
        input_output_aliases={6: 0},
        scratch_shapes=[pltpu.VMEM((CHUNK, LANES), BF16), pltpu.VMEM((2 * CHUNK, LANES), BF16),
                        pltpu.VMEM((2 * CHUNK, LANES), BF16), f32buf, f32buf,
                        pltpu.VMEM((4, WINDOW, 2 * WINDOW), F32)] + [f32buf] * (2 * nbranch),
        compiler_params=_params(2),
    )(slopes, qkv, qkv, qkv, qkv, qkv, ycat)


def _attn_bwd(name, qkv, dycat, ycat, lg, slopes, bmat, A, catoff):
    T = qkv.shape[0]
    hpn, nch, nblk = A // LANES, T // CHUNK, CHUNK // WINDOW
    co = catoff // LANES

    def body(sl_ref, q_ref, k_ref, kp_ref, v_ref, vp_ref, do_ref, o_ref, l_ref, b_ref, dq_ref, dk_ref, dv_ref,
             qd, kd, vd, dod, ddn, ddd, ldd, dqd, dkd, dvd, bias):
        hp, ch = pl.program_id(0), pl.program_id(1)
        lane = lax.broadcasted_iota(jnp.int32, (1, LANES), 1)
        first = lane < (LANES // 2)
        ddn[...] = _seg_sum(do_ref[...] * o_ref[...].astype(F32), b_ref[...])
        dq_ref[...] = jnp.zeros_like(dq_ref)

        @pl.when(ch == 0)
        def _():
            dk_ref[...] = jnp.zeros_like(dk_ref)
            dv_ref[...] = jnp.zeros_like(dv_ref)

        base = ch * CHUNK
        for d in DILATIONS:
            Ld = CHUNK // d
            seg = Ld + WINDOW
            nbr = Ld // WINDOW
            _deinterleave(qd, q_ref, d, Ld, Ld, 0, 0)
            _deinterleave(dod, do_ref, d, Ld, Ld, 0, 0)
            _deinterleave(ddd, ddn, d, Ld, Ld, 0, 0)
            _deinterleave(ldd, l_ref, d, Ld, Ld, 0, 0)
            for dst, cur, prev in ((kd, k_ref, kp_ref), (vd, v_ref, vp_ref)):
                _deinterleave(dst, prev, d, WINDOW, seg, 0, CHUNK - WINDOW * d)
                _deinterleave(dst, cur, d, Ld, seg, WINDOW, 0)
            dkd[0:d * seg, :] = jnp.zeros((d * seg, LANES), F32)
            dvd[0:d * seg, :] = jnp.zeros((d * seg, LANES), F32)
            _fill_bias(bias, sl_ref, hp, d)

            def step(it, carry, Ld=Ld, seg=seg, nbr=nbr):
                r, nl = it // nbr, it % nbr
                q0 = pl.multiple_of(r * Ld + nl * WINDOW, WINDOW)
                k0 = pl.multiple_of(r * seg + nl * WINDOW, WINDOW)
                later = jnp.where(ch * nbr + nl > 0, 1, 0)
                qb = qd[pl.ds(q0, WINDOW), :]
                k2 = kd[pl.ds(k0, 2 * WINDOW), :]
                v2 = vd[pl.ds(k0, 2 * WINDOW), :]
                dob = dod[pl.ds(q0, WINDOW), :]
                dd = ddd[pl.ds(q0, WINDOW), :]
                lb = ldd[pl.ds(q0, WINDOW), :]
                dk2 = jnp.zeros((2 * WINDOW, LANES), F32)
                dv2 = jnp.zeros((2 * WINDOW, LANES), F32)
                dqs = []
                for hh in range(2):
                    mh = first if hh == 0 else jnp.logical_not(first)
                    qh = jnp.where(mh, qb, jnp.zeros_like(qb))
                    doh = jnp.where(mh, dob, jnp.zeros_like(dob))
                    lcol = lb[:, hh * (LANES // 2):hh * (LANES // 2) + 1]
                    p = jnp.exp(_nt(qh, k2) + bias[2 * hh + later] - lcol)
                    dcol = dd[:, hh * (LANES // 2):hh * (LANES // 2) + 1]
                    ds = (p * (_nt(doh, v2) - dcol)).astype(BF16)
                    dqs.append(_nn(ds, k2))
                    dk2 = dk2 + _tn(ds, qh)
                    dv2 = dv2 + _tn(p.astype(BF16), doh)
                dqd[pl.ds(q0, WINDOW), :] = jnp.where(first, dqs[0], dqs[1])
                dkd[pl.ds(k0, 2 * WINDOW), :] += dk2
                dvd[pl.ds(k0, 2 * WINDOW), :] += dv2
                return carry

            lax.fori_loop(0, nblk, step, 0, unroll=ATTN_UNROLL)
            _interleave_add(dq_ref, 0, dqd, d, Ld, Ld, 0)
            for acc, out in ((dkd, dk_ref), (dvd, dv_ref)):
                _interleave_add(out, base, acc, d, Ld, seg, WINDOW)

                @pl.when(ch > 0)
                def _(acc=acc, out=out, d=d, seg=seg):
                    _interleave_add(out, base - WINDOW * d, acc, d, WINDOW, seg, 0)

    blk = lambda m: pl.BlockSpec((CHUNK, LANES), m)
    cur = lambda which: blk(lambda hp, ch: (ch, which * hpn + hp))
    prev = lambda which: blk(lambda hp, ch: (jnp.maximum(ch - 1, 0), which * hpn + hp))
    omap = blk(lambda hp, ch: (ch, hp))
    full = pl.BlockSpec((T, LANES), lambda hp, ch: (0, hp))
    f32buf = pltpu.VMEM((CHUNK, LANES), F32)
    bf16buf = pltpu.VMEM((CHUNK, LANES), BF16)
    return pl.pallas_call(
        body, name=name, grid=(hpn, nch),
        in_specs=[pl.BlockSpec(memory_space=pltpu.SMEM), cur(0), cur(1), prev(1), cur(2), prev(2),
                  blk(lambda hp, ch: (ch, co + hp)), blk(lambda hp, ch: (ch, co + hp)), omap,
                  pl.BlockSpec((LANES, LANES), lambda hp, ch: (0, 0))],
        out_specs=[omap, full, full],
        out_shape=[jax.ShapeDtypeStruct((T, A), F32)] * 3,
        scratch_shapes=[bf16buf, pltpu.VMEM((2 * CHUNK, LANES), BF16), pltpu.VMEM((2 * CHUNK, LANES), BF16),
                        bf16buf, f32buf, f32buf, f32buf, f32buf,
                        pltpu.VMEM((2 * CHUNK, LANES), F32), pltpu.VMEM((2 * CHUNK, LANES), F32),
                        pltpu.VMEM((4, WINDOW, 2 * WINDOW), F32)],
        compiler_params=_params(2),
    )(slopes, qkv, qkv, qkv, qkv, qkv, dycat, ycat, lg, bmat)


def _attn_bwd_combine(name, dzc, dqs, dks, dvs, z, gq, gk, bmat, fmat, A, c0, hd):
    T = z.shape[0]
    tm = _pick(T, (256, 128))
    nbr = len(dqs)
    W0 = dzc.shape[1]

    def body(*refs):
        dq_refs, dk_refs, dv_refs = refs[:nbr], refs[nbr:2 * nbr], refs[2 * nbr:3 * nbr]
        zq_ref, zk_ref, gq_ref, gk_ref, b_ref, f_ref, dzc_ref, dz_ref, dgq_ref, dgk_ref = refs[3 * nbr:]
        i = pl.program_id(0)
        bm = b_ref[...]
        dz_ref[:, :W0] = dzc_ref[...]

        def tot(rs):
            t = rs[0][...]
            for r in rs[1:]:
                t = t + r[...]
            return t

        for idx, (d_refs, z_ref, g_ref, dg_ref, gscale) in enumerate(
                ((dq_refs, zq_ref, gq_ref, dgq_ref, QK_SCALE), (dk_refs, zk_ref, gk_ref, dgk_ref, 1.0))):
            dy = tot(d_refs)
            zv = z_ref[...]
            r = lax.rsqrt(_seg_sum(zv * zv, bm) * (1.0 / hd) + EPS)
            gd = dy * g_ref[...]
            mean = _seg_sum(gd * zv, bm) * (1.0 / hd)
            dz_ref[:, W0 + idx * A:W0 + (idx + 1) * A] = (r * gd - zv * (r * r * r) * mean).astype(BF16)
            part = jnp.sum(dy * zv * r, axis=0, keepdims=True) * gscale

            @pl.when(i == 0)
            def _():
                dg_ref[...] = part

            @pl.when(i > 0)
            def _():
                dg_ref[...] += part

        dz_ref[:, W0 + 2 * A:] = tot(dv_refs).astype(BF16)

        @pl.when(i == T // tm - 1)
        def _():
            fm = f_ref[...]
            for dg_ref in (dgq_ref, dgk_ref):
                v = jnp.broadcast_to(dg_ref[...], (8, A))
                hi = v.astype(BF16)
                mid = (v - hi.astype(F32)).astype(BF16)
                lo = (v - hi.astype(F32) - mid.astype(F32)).astype(BF16)
                dg_ref[...] = (_nn(hi, fm) + _nn(mid, fm) + _nn(lo, fm))[0:1]

    blk = pl.BlockSpec((tm, A), lambda i: (i, 0))
    vec = pl.BlockSpec((1, A), lambda i: (0, 0))
    return pl.pallas_call(
        body, name=name, grid=(T // tm,),
        in_specs=[blk] * (3 * nbr) + [pl.BlockSpec((tm, A), lambda i: (i, c0)),
                                      pl.BlockSpec((tm, A), lambda i: (i, c0 + 1)), vec, vec,
                                      pl.BlockSpec((LANES, LANES), lambda i: (0, 0)),
                                      pl.BlockSpec((A, A), lambda i: (0, 0)),
                                      pl.BlockSpec((tm, W0), lambda i: (i, 0))],
        out_specs=[pl.BlockSpec((tm, W0 + 3 * A), lambda i: (i, 0)), vec, vec],
        out_shape=[jax.ShapeDtypeStruct((T, W0 + 3 * A), BF16), jax.ShapeDtypeStruct((1, A), F32),
                   jax.ShapeDtypeStruct((1, A), F32)],
        compiler_params=_params(1),
    )(*dqs, *dks, *dvs, z, z, gq, gk, bmat, fmat, dzc)


def _local_step(x, tgt, S, comm, hd):
    T, D = x.shape
    C = S["conv_b_dw"].shape[1]
    A = C
    H = A // hd
    Dmix = C + A
    c0 = (2 * C) // A
    slopes = 2.0 ** (-ALIBI_MAX_BIAS * jnp.arange(1, H + 1, dtype=F32) / H)
    seg = jnp.arange(LANES) // hd
    bmat = (seg[:, None] == seg[None, :]).astype(BF16)
    pos_in_head = jnp.arange(A) % hd
    fmat = (pos_in_head[:, None] == pos_in_head[None, :]).astype(BF16)
    gq = jnp.tile(S["q_norm_g"], (1, H)) * QK_SCALE
    gk = jnp.tile(S["k_norm_g"], (1, H))

    wg1, wu1 = comm.weights(("ffn1_w_gate", "ffn1_w_up"), None)
    h1, gate1, up1, a1 = _norm_matmul("ffn1_up", x, comm.tie(S["ffn1_norm_g"]), [wg1, wu1], True)
    wd1, win, w32 = comm.weights(("ffn1_w_down", "w_in", "conv_w32"), a1)
    x1 = _matmul_res("ffn1_down", a1, wd1, x, 0.5)
    h2, z = _norm_matmul("mix_in", x1, S["mix_norm_g"], [win], False)
    yc, ycv = _conv_fwd("conv_fwd", z, w32, S["conv_b_dw"], S["conv_ln_g"], S["conv_ln_b"], C)
    qkv = _attn_prep("attn_prep", z, gq, gk, bmat, A, c0, hd)
    ycat, lg = _attn_fwd("attn_fwd", qkv, slopes, yc, A)
    wout, wg2, wu2, wd2 = comm.weights(("w_out", "ffn2_w_gate", "ffn2_w_up", "ffn2_w_down"), lg)
    x2 = _matmul_res("mix_out", ycat, wout, x1, 1.0)
    h3, gate2, up2, a2 = _norm_matmul("ffn2_up", x2, S["ffn2_norm_g"], [wg2, wu2], True)
    dx3, dx3b, lossvec = _matmul_res("ffn2_down_loss", a2, wd2, x2, 0.5, tgt=tgt)

    G = {}
    dgate2, dup2 = _nt_matmul("ffn2_dact", dx3b, wd2, 0.5, gate2, up2)
    comm.reduce_begin("ffn2", {"ffn2_w_down": _tn_matmul("ffn2_dwd", a2, dx3b, 0.5),
                               "ffn2_w_gate": _tn_matmul("ffn2_dwg", h3, dgate2),
                               "ffn2_w_up": _tn_matmul("ffn2_dwu", h3, dup2)})
    dx2, dx2b, G["ffn2_norm_g"] = _nt_rms_bwd("ffn2_dx", [dgate2, dup2], [wg2, wu2],
                                              x2, comm.tie(S["ffn2_norm_g"]), dx3)
    dwout = _tn_matmul("mix_dwout", ycat, dx2b)
    dycat, _ = _nt_matmul("mix_dycat", dx2b, wout)
    dzc, G["conv_w32"], G["conv_b_dw"], G["conv_ln_g"], G["conv_ln_b"] = _conv_bwd(
        "conv_bwd", z, ycv, dycat, w32, S["conv_ln_g"], S["conv_ln_b"], C)
    dq, dk, dv = _attn_bwd("attn_bwd", qkv, dycat, ycat, lg, slopes, bmat, A, C)
    dz, G["q_norm_g"], G["k_norm_g"] = _attn_bwd_combine(
        "attn_bwd_combine", dzc, [dq], [dk], [dv], z, gq, gk, bmat, fmat, A, c0, hd)
    comm.reduce_end("ffn2", dz)
    comm.reduce_begin("mix", {"w_out": dwout, "w_in": _tn_matmul("mix_dwin", h2, dz)})
    dx1, dx1b, G["mix_norm_g"] = _nt_rms_bwd("mix_dx", [dz], [win], x1, comm.tie(S["mix_norm_g"]), dx2)
    dgate1, dup1 = _nt_matmul("ffn1_dact", dx1b, wd1, 0.5, gate1, up1)
    dwd1 = _tn_matmul("ffn1_dwd", a1, dx1b, 0.5)
    dwg1 = _tn_matmul("ffn1_dwg", h1, dgate1)
    dwu1 = _tn_matmul("ffn1_dwu", h1, dup1)
    comm.reduce_end("mix", dwu1)
    comm.reduce_begin("ffn1", {"ffn1_w_down": dwd1, "ffn1_w_gate": dwg1, "ffn1_w_up": dwu1})
    dx0, _, G["ffn1_norm_g"] = _nt_rms_bwd("ffn1_dx", [dgate1, dup1], [wg1, wu1],
                                           x, comm.tie(S["ffn1_norm_g"]), dx1)
    comm.reduce_end("ffn1", dx0)
    return lossvec, dx0, G


BIG = (("ffn1_w_gate", 1), ("ffn1_w_up", 1), ("ffn1_w_down", 0), ("w_in", 1), ("w_out", 0),
       ("ffn2_w_gate", 1), ("ffn2_w_up", 1), ("ffn2_w_down", 0))
AXIS = dict(BIG)
FLIPS = ((1, 0), (0, 1), (1, 1))
HBM = pl.BlockSpec(memory_space=pltpu.HBM)
SEM = pl.BlockSpec(memory_space=pltpu.SEMAPHORE)
EFFECT = pltpu.SideEffectType.DATAFLOW_SIDE_EFFECTING
TOKEN = jax.ShapeDtypeStruct((8, LANES), F32)


def _window(ref, shape, axis, slab=None, half=None):
    idx = [pl.ds(0, shape[0]), pl.ds(0, shape[1])]
    if slab is not None:
        n = shape[axis] // 4
        idx[axis] = pl.ds(pl.multiple_of(slab * n, 8), n)
    if half is not None:
        hs = shape[1 - axis] // 2
        idx[1 - axis] = pl.ds(pl.multiple_of(half * hs, 8), hs)
    return ref.at[idx[0], idx[1]]


def _position():
    return lax.axis_index("x"), lax.axis_index("y"), lax.axis_index("c")


def _half_shape(shape, axis):
    return (shape[0] // 2, shape[1]) if axis == 1 else (shape[0], shape[1] // 2)


def _slab_shape(shape, axis):
    return (shape[0], shape[1] // 4) if axis == 1 else (shape[0] // 4, shape[1])


def _piece_shape(shape, axis):
    return _half_shape(_slab_shape(shape, axis), axis)


def _full_shape(shard, axis):
    return (shard.shape[0], shard.shape[1] * 4) if axis == 1 else (shard.shape[0] * 4, shard.shape[1])


def _hbm(a):
    return pltpu.with_memory_space_constraint(a, pltpu.HBM)


def _remote(src, dst, send_sem, recv_sem, to):
    return pltpu.make_async_remote_copy(src_ref=src, dst_ref=dst, send_sem=send_sem, recv_sem=recv_sem,
                                        device_id=to, device_id_type=MESH)


def _place(name, pos, w, axis):
    R, Cc = w.shape
    tr = _pick(R, (256, 128, 64, 32, 16))
    nrb = R // tr

    def body(pos_ref, w_ref, o_ref):
        o_ref[...] = w_ref[...].astype(BF16)

    omap = (lambda i, p: (i, p[0])) if axis == 1 else (lambda i, p: (p[0] * nrb + i, 0))
    return pl.pallas_call(
        body, name=name,
        grid_spec=pltpu.PrefetchScalarGridSpec(
            num_scalar_prefetch=1, grid=(nrb,), in_specs=[pl.BlockSpec((tr, Cc), lambda i, p: (i, 0))],
            out_specs=pl.BlockSpec((tr, Cc), omap)),
        out_shape=jax.ShapeDtypeStruct(_full_shape(w, axis), BF16), compiler_params=_params(1),
    )(pos, w)


def _gather_now(name, axes, fulls):
    nt = len(fulls)
    shapes = [f.shape for f in fulls]

    def body(*refs):
        outs, token = refs[nt:2 * nt], refs[2 * nt]
        send_sems, recv_sems = refs[2 * nt + 1:]
        x, y, c = _position()
        j0 = 2 * x + y
        sib = (x, y, 1 - c)

        def copy(t, k, slab, half, to):
            win = _window(outs[t], shapes[t], axes[t], slab=slab, half=half)
            return _remote(win, win, send_sems.at[t, k], recv_sems.at[t, k], to)

        sends = []
        for k, (fx, fy) in enumerate(FLIPS):
            for t in range(nt):
                cp = copy(t, k, j0, c, (x ^ fx, y ^ fy, c))
                cp.start()
                sends.append(cp)
        for k, (fx, fy) in enumerate(FLIPS):
            js = 2 * (x ^ fx) + (y ^ fy)
            for t in range(nt):
                copy(t, k, js, c, sib).wait_recv()
                cp = copy(t, 3 + k, js, c, sib)
                cp.start()
                sends.append(cp)
        for k, (fx, fy) in enumerate(FLIPS):
            js = 2 * (x ^ fx) + (y ^ fy)
            for t in range(nt):
                copy(t, 3 + k, js, 1 - c, sib).wait_recv()
        for cp in sends:
            cp.wait_send()
        token[...] = jnp.zeros_like(token)

    res = pl.pallas_call(
        body, name=name, in_specs=[ANY] * nt,
        out_specs=[ANY] * nt + [pl.BlockSpec(memory_space=pltpu.VMEM)],
        out_shape=[jax.ShapeDtypeStruct(s, BF16) for s in shapes] + [TOKEN],
        input_output_aliases={t: t for t in range(nt)},
        scratch_shapes=[pltpu.SemaphoreType.DMA((nt, 6)), pltpu.SemaphoreType.DMA((nt, 6))],
    )(*fulls)
    return list(res[:nt]), res[nt]


def _split_start(name, arrays, ncopies, plan):
    na = len(arrays)

    def body(*refs):
        ins = refs[:na]
        send_sems, recv_sems = refs[na], refs[na + 1]
        token = refs[-1]
        x, y, c = _position()
        for i, (src, dst, to) in enumerate(plan(ins, x, y, c)):
            _remote(src, dst, send_sems.at[i], recv_sems.at[i], to).start()
        token[...] = jnp.zeros_like(token)

    res = pl.pallas_call(
        body, name=name, in_specs=[HBM] * na,
        out_specs=tuple([SEM, SEM] + [HBM] * na + [pl.BlockSpec(memory_space=pltpu.VMEM)]),
        out_shape=tuple([pltpu.SemaphoreType.DMA((ncopies,)), pltpu.SemaphoreType.DMA((ncopies,))]
                        + [pltpu.HBM(a.shape, a.dtype) for a in arrays] + [TOKEN]),
        input_output_aliases={i: 2 + i for i in range(na)},
        compiler_params=pltpu.CompilerParams(has_side_effects=EFFECT),
    )(*[_hbm(a) for a in arrays])
    return (res[0], res[1]), list(res[2:2 + na]), res[-1]


def _split_wait(name, arrays, sems, after, plan):
    na = len(arrays)

    def body(*refs):
        ins = refs[:na]
        send_sems, recv_sems = refs[na], refs[na + 1]
        x, y, c = _position()
        for i, (src, dst, to) in enumerate(plan(ins, x, y, c)):
            cp = _remote(src, dst, send_sems.at[i], recv_sems.at[i], to)
            cp.wait_send()
            cp.wait_recv()

    res = pl.pallas_call(
        body, name=name, in_specs=[HBM] * na + [SEM, SEM, ANY],
        out_specs=tuple([HBM] * na), out_shape=tuple(pltpu.HBM(a.shape, a.dtype) for a in arrays),
        input_output_aliases={i: i for i in range(na)},
        compiler_params=pltpu.CompilerParams(has_side_effects=EFFECT),
    )(*arrays, *sems, after)
    return list(res)


def _gather_plan(axes, shapes, conv_shape):
    nt = len(axes)

    def plan(refs, x, y, c):
        j0 = 2 * x + y
        out = []
        for fx, fy in FLIPS:
            to = (x ^ fx, y ^ fy, c)
            for t in range(nt):
                win = _window(refs[t], shapes[t], axes[t], slab=j0, half=c)
                out.append((win, win, to))
            if conv_shape is not None:
                win = _window(refs[nt], conv_shape, 1, slab=j0)
                out.append((win, win, to))
        return out

    return plan


def _gather_finish(name, axes, fulls):
    nt = len(axes)
    shapes = [f.shape for f in fulls]

    def body(*refs):
        outs = refs[nt:2 * nt]
        send_sems, recv_sems = refs[2 * nt:]
        x, y, c = _position()
        sib = (x, y, 1 - c)
        cps = []
        for k, (fx, fy) in enumerate(FLIPS):
            js = 2 * (x ^ fx) + (y ^ fy)
            for t in range(nt):
                landed = _window(outs[t], shapes[t], axes[t], slab=js, half=c)
                cp = _remote(landed, landed, send_sems.at[t, k], recv_sems.at[t, k], sib)
                cp.start()
                cps.append(cp)
        for k, (fx, fy) in enumerate(FLIPS):
            js = 2 * (x ^ fx) + (y ^ fy)
            for t in range(nt):
                other = _window(outs[t], shapes[t], axes[t], slab=js, half=1 - c)
                _remote(other, other, send_sems.at[t, k], recv_sems.at[t, k], sib).wait_recv()
        for cp in cps:
            cp.wait_send()

    res = pl.pallas_call(
        body, name=name, in_specs=[ANY] * nt, out_specs=[ANY] * nt,
        out_shape=[jax.ShapeDtypeStruct(f.shape, f.dtype) for f in fulls],
        input_output_aliases={t: t for t in range(nt)},
        scratch_shapes=[pltpu.SemaphoreType.DMA((nt, 3)), pltpu.SemaphoreType.DMA((nt, 3))],
    )(*fulls)
    return list(res)


def _pair_exchange(name, srcs, windows, out_shapes, dtype):
    nt = len(srcs)

    def body(*refs):
        ins, outs = refs[:nt], refs[nt:2 * nt]
        send_sems, recv_sems = refs[2 * nt:]
        x, y, c = _position()
        cps = []
        for t in range(nt):
            cp = _remote(windows[t](ins[t], c), outs[t], send_sems.at[t], recv_sems.at[t], (x, y, 1 - c))
            cp.start()
            cps.append(cp)
        for cp in cps:
            cp.wait()

    return pl.pallas_call(
        body, name=name, in_specs=[ANY] * nt, out_specs=[ANY] * nt,
        out_shape=[jax.ShapeDtypeStruct(s, dtype) for s in out_shapes],
        scratch_shapes=[pltpu.SemaphoreType.DMA((nt,)), pltpu.SemaphoreType.DMA((nt,))],
    )(*srcs)


def _scatter_plan(axes, shapes):
    nt = len(axes)

    def plan(refs, x, y, c):
        out = []
        for k, (fx, fy) in enumerate(FLIPS):
            js = 2 * (x ^ fx) + (y ^ fy)
            for t in range(nt):
                src = _window(refs[t], _half_shape(shapes[t], axes[t]), axes[t], slab=js)
                out.append((src, refs[nt + t].at[k], (x ^ fx, y ^ fy, c)))
        return out

    return plan


def _gather_small(packed):
    R, Cc = packed.shape

    def body(p_ref, o_ref, send_sems, recv_sems, loc_sem):
        x, y, c = _position()
        me = 4 * x + 2 * y + c
        mine = pltpu.make_async_copy(p_ref, o_ref.at[me], loc_sem)
        mine.start()
        cps = []
        for k in range(1, 8):
            fx, fy, fc = (k >> 2) & 1, (k >> 1) & 1, k & 1
            cp = pltpu.make_async_remote_copy(
                src_ref=p_ref, dst_ref=o_ref.at[me], send_sem=send_sems.at[k - 1], recv_sem=recv_sems.at[k - 1],
                device_id=(x ^ fx, y ^ fy, c ^ fc), device_id_type=MESH)
            cp.start()
            cps.append(cp)
        for cp in cps:
            cp.wait()
        mine.wait()

    return pl.pallas_call(
        body, name="gather_small_grads", in_specs=[ANY], out_specs=ANY,
        out_shape=jax.ShapeDtypeStruct((8, R, Cc), F32),
        scratch_shapes=[pltpu.SemaphoreType.DMA((7,)), pltpu.SemaphoreType.DMA((7,)), pltpu.SemaphoreType.DMA],
    )(packed)


def _sum_slots(name, slots):
    n, R, Cc = slots.shape

    def body(s_ref, o_ref):
        t = s_ref[0]
        for i in range(1, n):
            t = t + s_ref[i]
        o_ref[...] = t

    return pl.pallas_call(
        body, name=name, grid=(1,), in_specs=[pl.BlockSpec((n, R, Cc), lambda i: (0, 0, 0))],
        out_specs=pl.BlockSpec((R, Cc), lambda i: (0, 0)), out_shape=jax.ShapeDtypeStruct((R, Cc), F32),
        compiler_params=_params(1),
    )(slots)


def _pair_sum(name, pos, g, land, shape, axis):
    hshape = _half_shape(shape, axis)
    R, Cc = hshape
    tr = _pick(R, (256, 128, 64, 32, 16))
    nrb = R // tr

    def body(pos_ref, g_ref, l_ref, o_ref):
        o_ref[...] = (g_ref[...].astype(F32) + l_ref[...].astype(F32)).astype(BF16)

    if axis == 1:
        gmap = lambda i, p: (p[1] * nrb + i, 0)
    else:
        gmap = lambda i, p: (i, p[1])
    blk = pl.BlockSpec((tr, Cc), lambda i, p: (i, 0))
    return pl.pallas_call(
        body, name=name,
        grid_spec=pltpu.PrefetchScalarGridSpec(
            num_scalar_prefetch=1, grid=(nrb,), in_specs=[pl.BlockSpec((tr, Cc), gmap), blk], out_specs=blk),
        out_shape=jax.ShapeDtypeStruct(hshape, BF16), compiler_params=_params(1),
    )(pos, g, land)


def _chip_sum(name, pos, sb, land, shape, axis):
    hshape = _half_shape(shape, axis)
    pshape = _piece_shape(shape, axis)
    R, Cc = pshape
    tr = _pick(R, (256, 128, 64, 32, 16))
    nrb = R // tr

    def body(pos_ref, s_ref, l_ref, o_ref):
        t = s_ref[...].astype(F32)
        for k in range(3):
            t = t + l_ref[k].astype(F32)
        o_ref[...] = t

    if axis == 1:
        smap = lambda i, p: (i, p[0])
    else:
        smap = lambda i, p: (p[0] * nrb + i, 0)
    return pl.pallas_call(
        body, name=name,
        grid_spec=pltpu.PrefetchScalarGridSpec(
            num_scalar_prefetch=1, grid=(nrb,),
            in_specs=[pl.BlockSpec((tr, Cc), smap), pl.BlockSpec((3, tr, Cc), lambda i, p: (0, i, 0))],
            out_specs=pl.BlockSpec((tr, Cc), lambda i, p: (i, 0))),
        out_shape=jax.ShapeDtypeStruct(pshape, F32), compiler_params=_params(1),
    )(pos, sb, land)


def _adam_math(w, g, m, v):
    m = ADAM_B1 * m + (1.0 - ADAM_B1) * g
    v = ADAM_B2 * v + (1.0 - ADAM_B2) * (g * g)
    m_hat = m / (1.0 - ADAM_B1 ** ADAM_STEP)
    v_hat = v / (1.0 - ADAM_B2 ** ADAM_STEP)
    delta = -ADAM_LR * (m_hat / (jnp.sqrt(v_hat) + ADAM_EPS) + ADAM_WD * w)
    return delta, m, v


def _adamw_halves(name, pos, w, m, v, mine, theirs, axis):
    R, Cc = w.shape
    hr, hc = mine.shape
    tr = _pick(hr, (256, 128, 64, 32, 16))
    nrb = hr // tr

    def body(pos_ref, w_ref, m_ref, v_ref, a_ref, b_ref, g_ref, d_ref, nm_ref, nv_ref):
        half = pl.program_id(0)
        g = jnp.where(half == pos_ref[1], a_ref[...], b_ref[...])
        d, nm, nv = _adam_math(w_ref[...], g, m_ref[...], v_ref[...])
        g_ref[...] = g
        d_ref[...] = d
        nm_ref[...] = nm
        nv_ref[...] = nv

    if axis == 1:
        wmap = lambda h, i, p: (h * nrb + i, 0)
    else:
        wmap = lambda h, i, p: (i, h)
    wblk = pl.BlockSpec((tr, hc), wmap)
    ablk = pl.BlockSpec((tr, hc), lambda h, i, p: (jnp.where(h == p[1], i, 0), 0))
    bblk = pl.BlockSpec((tr, hc), lambda h, i, p: (jnp.where(h == p[1], 0, i), 0))
    return pl.pallas_call(
        body, name=name,
        grid_spec=pltpu.PrefetchScalarGridSpec(
            num_scalar_prefetch=1, grid=(2, nrb), in_specs=[wblk, wblk, wblk, ablk, bblk], out_specs=[wblk] * 4),
        out_shape=[jax.ShapeDtypeStruct((R, Cc), F32)] * 4, compiler_params=_params(2),
    )(pos, w, m, v, mine, theirs)


def _adamw_small(name, w, g, m, v):
    def body(w_ref, g_ref, m_ref, v_ref, d_ref, nm_ref, nv_ref):
        d, nm, nv = _adam_math(w_ref[...], g_ref[...], m_ref[...], v_ref[...])
        d_ref[...] = d
        nm_ref[...] = nm
        nv_ref[...] = nv

    blk = pl.BlockSpec(w.shape, lambda i: (0, 0))
    return pl.pallas_call(
        body, name=name, grid=(1,), in_specs=[blk] * 4, out_specs=[blk] * 3,
        out_shape=[jax.ShapeDtypeStruct(w.shape, F32)] * 3, compiler_params=_params(1),
    )(w, g, m, v)


SMALL = ("ffn1_norm_g", "mix_norm_g", "conv_b_dw", "conv_ln_g", "conv_ln_b", "q_norm_g", "k_norm_g", "ffn2_norm_g")
ORDER = ("ffn1_norm_g", "ffn1_w_gate", "ffn1_w_up", "ffn1_w_down", "mix_norm_g", "w_in", "conv_w_dw", "conv_b_dw",
         "conv_ln_g", "conv_ln_b", "q_norm_g", "k_norm_g", "w_out", "ffn2_norm_g", "ffn2_w_gate", "ffn2_w_up",
         "ffn2_w_down")
GATHER_FIRST = ("ffn1_w_gate", "ffn1_w_up")
GATHER_SECOND = ("ffn1_w_down", "w_in")
GATHER_THIRD = ("w_out", "ffn2_w_gate", "ffn2_w_up", "ffn2_w_down")


class _Exchange:
    def __init__(self, P, Mo, Vo, conv_shard, pos):
        self.P, self.Mo, self.Vo, self.pos = P, Mo, Vo, pos
        self.tokens = []
        self.pending = {}
        self.reducing = {}
        self.results = {}
        placed = {n: _place("place_" + n, pos, P[n][0], a) for n, a in BIG}
        self.shapes = {n: placed[n].shape for n, _ in BIG}
        first, tok = _gather_now("gather_first", [AXIS[n] for n in GATHER_FIRST], [placed[n] for n in GATHER_FIRST])
        self.ready = dict(zip(GATHER_FIRST, first))
        cq = conv_shard.shape[1]
        conv_full = lax.dynamic_update_slice(jnp.zeros((conv_shard.shape[0], 4 * cq), F32), conv_shard,
                                             (0, pos[0] * cq))
        for gname, names, conv in (("second", GATHER_SECOND, conv_full), ("third", GATHER_THIRD, None)):
            axes = [AXIS[n] for n in names]
            shapes = [self.shapes[n] for n in names]
            arrays = [placed[n] for n in names] + ([conv] if conv is not None else [])
            small = min(range(len(arrays)), key=lambda i: arrays[i].size)
            arrays[small] = arrays[small] + tok[0, 0].astype(arrays[small].dtype)
            plan = _gather_plan(axes, shapes, conv.shape if conv is not None else None)
            sems, thru, tok = _split_start("gather_%s_start" % gname, arrays, 3 * len(arrays), plan)
            self.tokens.append(tok)
            for n in names + (("conv_w32",) if conv is not None else ()):
                self.pending[n] = (gname, names, axes, plan, sems, thru, conv is not None)

    def tie(self, v):
        for tok in self.tokens:
            v = v + tok[0:1, 0:1]
        self.tokens = []
        return v

    def weights(self, names, after):
        if names[0] in self.pending:
            gname, gnames, axes, plan, sems, thru, has_conv = self.pending[names[0]]
            thru = _split_wait("gather_%s_wait" % gname, thru, sems, after, plan)
            nt = len(gnames)
            fulls = _gather_finish("gather_%s_finish" % gname, axes, thru[:nt])
            for n, f in zip(gnames, fulls):
                self.ready[n] = f
                del self.pending[n]
            if has_conv:
                self.ready["conv_w32"] = thru[nt]
                del self.pending["conv_w32"]
        return [self.ready[n] for n in names]

    def reduce_begin(self, gname, grads):
        names = list(grads)
        axes = [AXIS[n] for n in names]
        shapes = [self.shapes[n] for n in names]
        gs = [grads[n] for n in names]
        to_sibling = [(lambda ref, c, s=s, a=a: _window(ref, s, a, half=1 - c)) for s, a in zip(shapes, axes)]
        landed = _pair_exchange("pair_exchange_" + gname, gs, to_sibling,
                                [_half_shape(s, a) for s, a in zip(shapes, axes)], BF16)
        sbs = [_pair_sum("pair_sum_" + n, self.pos, g, l, s, a)
               for n, a, g, l, s in zip(names, axes, gs, landed, shapes)]
        lands = [lax.empty((3,) + _piece_shape(s, a), BF16) for s, a in zip(shapes, axes)]
        plan = _scatter_plan(axes, shapes)
        sems, thru, tok = _split_start("scatter_%s_start" % gname, sbs + lands, 3 * len(names), plan)
        self.tokens.append(tok)
        self.reducing[gname] = (names, axes, shapes, plan, sems, thru)

    def reduce_end(self, gname, after):
        names, axes, shapes, plan, sems, thru = self.reducing.pop(gname)
        nt = len(names)
        thru = _split_wait("scatter_%s_wait" % gname, thru, sems, after, plan)
        mine = [_chip_sum("chip_sum_" + n, self.pos, sb, l, s, a)
                for n, a, sb, l, s in zip(names, axes, thru[:nt], thru[nt:], shapes)]
        theirs = _pair_exchange("half_exchange_" + gname, mine, [(lambda ref, c: ref)] * nt,
                                [m.shape for m in mine], F32)
        for n, a, mi, th in zip(names, axes, mine, theirs):
            g, d, nm, nv = _adamw_halves("adamw_" + n, self.pos, self.P[n][0], self.Mo[n][0], self.Vo[n][0],
                                         mi, th, a)
            self.results[n] = (g[None], d[None], nm[None], nv[None])


def kernel(x, ffn1_norm_g, ffn1_w_gate, ffn1_w_up, ffn1_w_down, mix_norm_g, w_in, conv_w_dw, conv_b_dw, conv_ln_g, conv_ln_b, q_norm_g, k_norm_g, w_out, ffn2_norm_g, ffn2_w_gate, ffn2_w_up, ffn2_w_down, loss_target, m_ffn1_norm_g, m_ffn1_w_gate, m_ffn1_w_up, m_ffn1_w_down, m_mix_norm_g, m_w_in, m_conv_w_dw, m_conv_b_dw, m_conv_ln_g, m_conv_ln_b, m_q_norm_g, m_k_norm_g, m_w_out, m_ffn2_norm_g, m_ffn2_w_gate, m_ffn2_w_up, m_ffn2_w_down, v_ffn1_norm_g, v_ffn1_w_gate, v_ffn1_w_up, v_ffn1_w_down, v_mix_norm_g, v_w_in, v_conv_w_dw, v_conv_b_dw, v_conv_ln_g, v_conv_ln_b, v_q_norm_g, v_k_norm_g, v_w_out, v_ffn2_norm_g, v_ffn2_w_gate, v_ffn2_w_up, v_ffn2_w_down):
    args = dict(locals())
    P = {n: args[n] for n in ORDER}
    Mo = {n: args["m_" + n] for n in ORDER}
    Vo = {n: args["v_" + n] for n in ORDER}
    xs = x[0]
    tgt = loss_target[0]
    T, D = xs.shape
    hd = q_norm_g.shape[-1]
    C = conv_b_dw.shape[-1]
    ntap = conv_w_dw.shape[1]
    cx, cy, cc = _position()
    j0 = 2 * cx + cy
    pos = jnp.stack([j0, cc]).astype(jnp.int32)

    conv_shard = jnp.pad(conv_w_dw[0], ((0, HALO - ntap), (0, 0)))
    comm = _Exchange(P, Mo, Vo, conv_shard, pos)
    lossvec, dx0, G = _local_step(xs, tgt, {n: P[n] for n in SMALL}, comm, hd)
    loss = lax.psum(0.5 / D * jnp.sum(lossvec), AXES)
    grads, deltas, new_m, new_v = {}, {}, {}, {}
    for n, _ in BIG:
        grads[n], deltas[n], new_m[n], new_v[n] = comm.results[n]

    rows = [G["conv_w32"]]
    for n in ("ffn1_norm_g", "mix_norm_g", "ffn2_norm_g"):
        rows.append(G[n].reshape(D // C, C))
    for n in ("conv_b_dw", "conv_ln_g", "conv_ln_b", "q_norm_g", "k_norm_g"):
        rows.append(G[n])
    packed = jnp.concatenate(rows, axis=0)
    packed = jnp.pad(packed, ((0, -packed.shape[0] % 8), (0, 0)))
    total = _sum_slots("sum_small_grads", _gather_small(packed))
    r = HALO
    small_g = {}
    cq = C // 4
    small_g["conv_w_dw"] = lax.dynamic_slice(total[:ntap], (0, j0 * cq), (ntap, cq))
    for n in ("ffn1_norm_g", "mix_norm_g", "ffn2_norm_g"):
        small_g[n] = total[r:r + D // C].reshape(1, D)
        r += D // C
    for n in ("conv_b_dw", "conv_ln_g", "conv_ln_b"):
        small_g[n] = total[r:r + 1]
        r += 1
    for n in ("q_norm_g", "k_norm_g"):
        small_g[n] = total[r:r + 1, :hd]
        r += 1
    for n in ("conv_w_dw",) + SMALL:
        lead = n == "conv_w_dw"
        w2, m2, v2 = (P[n][0], Mo[n][0], Vo[n][0]) if lead else (P[n], Mo[n], Vo[n])
        d, nm, nv = _adamw_small("adamw_" + n, w2, small_g[n], m2, v2)
        if lead:
            grads[n], deltas[n], new_m[n], new_v[n] = small_g[n][None], d[None], nm[None], nv[None]
        else:
            grads[n], deltas[n], new_m[n], new_v[n] = small_g[n], d, nm, nv

    return (loss, dx0[None], *[grads[n] for n in ORDER], *[deltas[n] for n in ORDER],
            *[new_m[n] for n in ORDER], *[new_v[n] for n in ORDER])
```

```python
import jax
import jax.numpy as jnp
from jax import lax
from jax.experimental import pallas as pl
from jax.experimental.pallas import tpu as pltpu

F32 = jnp.float32
BF16 = jnp.bfloat16
EPS = 1e-6
WINDOW = 128
DILATIONS = (1, 4, 16)
ALIBI_MAX_BIAS = 8.0
LANES = 128
HALO = 32
ADAM_LR, ADAM_B1, ADAM_B2, ADAM_EPS, ADAM_WD, ADAM_STEP = 0.001, 0.9, 0.999, 1e-08, 0.01, 10
VMEM_LIMIT_MB = 62
ROW_TILE = 1024
TN_ACC_ELEMS = 3 * 1024 * 1024
EPILOGUE_ROWS = 256
ACC_COLS = 512
MESH = pl.DeviceIdType.MESH
ANY = pl.BlockSpec(memory_space=pl.ANY)
AXES = ("x", "y", "c")
NEG = -1e30


def _pick(n, cands):
    for c in cands:
        if n % c == 0:
            return c
    return n


def _params(nsem):
    return pltpu.CompilerParams(dimension_semantics=("arbitrary",) * nsem,
                                vmem_limit_bytes=VMEM_LIMIT_MB << 20)


def _nn(a, b):
    return jnp.dot(a, b, preferred_element_type=F32)


def _nt(a, b):
    return lax.dot_general(a, b, (((1,), (1,)), ((), ())), preferred_element_type=F32)


def _tn(a, b):
    return lax.dot_general(a, b, (((0,), (0,)), ((), ())), preferred_element_type=F32)


def _sigmoid(v):
    return jax.nn.sigmoid(v)


def _rms_r(xv):
    return lax.rsqrt(jnp.mean(xv * xv, axis=-1, keepdims=True) + EPS)


def _norm_matmul(name, x, g, ws, swiglu):
    T, D = x.shape
    N = ws[0].shape[1]
    tm = _pick(T, (ROW_TILE, 512, 256, 128))
    tn = _pick(N, (512, 256, 128))
    nw = len(ws)

    def body(*refs):
        x_ref, g_ref = refs[:2]
        w_refs = refs[2:2 + nw]
        outs = refs[2 + nw:-1]
        hs = refs[-1]

        @pl.when(pl.program_id(1) == 0)
        def _():
            for r0 in range(0, tm, EPILOGUE_ROWS):
                rows = slice(r0, r0 + min(EPILOGUE_ROWS, tm))
                xv = x_ref[rows, :]
                hv = (xv * _rms_r(xv) * g_ref[...]).astype(BF16)
                hs[rows, :] = hv
                outs[0][rows, :] = hv

        h = hs[...]
        if swiglu:
            gt = _nn(h, w_refs[0][...])
            u = _nn(h, w_refs[1][...])
            sg = _sigmoid(gt)
            silu = gt * sg
            outs[1][...] = (u * (sg * (1.0 + gt * (1.0 - sg)))).astype(BF16)
            outs[2][...] = silu.astype(BF16)
            outs[3][...] = (silu * u).astype(BF16)
        else:
            outs[1][...] = _nn(h, w_refs[0][...])

    row = pl.BlockSpec((tm, D), lambda i, j: (i, 0))
    col = pl.BlockSpec((D, tn), lambda i, j: (0, j))
    tile = pl.BlockSpec((tm, tn), lambda i, j: (i, j))
    if swiglu:
        out_shape = [jax.ShapeDtypeStruct((T, D), BF16)] + [jax.ShapeDtypeStruct((T, N), BF16)] * 3
        out_specs = [row, tile, tile, tile]
    else:
        out_shape = [jax.ShapeDtypeStruct((T, D), BF16), jax.ShapeDtypeStruct((T, N), F32)]
        out_specs = [row, tile]
    return pl.pallas_call(
        body, name=name, grid=(T // tm, N // tn),
        in_specs=[row, pl.BlockSpec((1, D), lambda i, j: (0, 0))] + [col] * nw,
        out_specs=out_specs, out_shape=out_shape,
        scratch_shapes=[pltpu.VMEM((tm, D), BF16)],
        compiler_params=_params(2),
    )(x, g, *ws)


def _matmul_res(name, a, w, res, scale, tgt=None):
    T, K = a.shape
    N = w.shape[1]
    loss = tgt is not None
    tm = _pick(T, (512, 256, 128))
    tk = _pick(K, (1408, 1024, 512, 256, 128))
    nk = K // tk

    def body(*refs):
        if loss:
            a_ref, w_ref, res_ref, tgt_ref, dx_ref, dxb_ref, lv_ref, acc = refs
        else:
            a_ref, w_ref, res_ref, out_ref, acc = refs
        i, k = pl.program_id(0), pl.program_id(1)

        @pl.when(k == 0)
        def _():
            acc[...] = jnp.zeros_like(acc)

        acc[...] += _nn(a_ref[...], w_ref[...])

        @pl.when(k == nk - 1)
        def _():
            part = jnp.zeros((1, N), F32)
            for r0 in range(0, tm, EPILOGUE_ROWS):
                rows = slice(r0, r0 + min(EPILOGUE_ROWS, tm))
                val = res_ref[rows, :] + scale * acc[rows, :]
                if loss:
                    dv = val - tgt_ref[rows, :]
                    dx = dv * (1.0 / N)
                    dx_ref[rows, :] = dx
                    dxb_ref[rows, :] = dx.astype(BF16)
                    part = part + jnp.sum(dv * dv, axis=0, keepdims=True)
                else:
                    out_ref[rows, :] = val
            if loss:
                @pl.when(i == 0)
                def _():
                    lv_ref[...] = part

                @pl.when(i > 0)
                def _():
                    lv_ref[...] += part

    row = pl.BlockSpec((tm, N), lambda i, k: (i, 0))
    in_specs = [pl.BlockSpec((tm, tk), lambda i, k: (i, k)), pl.BlockSpec((tk, N), lambda i, k: (k, 0)), row]
    args = [a, w, res]
    if loss:
        in_specs.append(row)
        args.append(tgt)
        out_specs = [row, row, pl.BlockSpec((1, N), lambda i, k: (0, 0))]
        out_shape = [jax.ShapeDtypeStruct((T, N), F32), jax.ShapeDtypeStruct((T, N), BF16),
                     jax.ShapeDtypeStruct((1, N), F32)]
    else:
        out_specs = row
        out_shape = jax.ShapeDtypeStruct((T, N), F32)
    return pl.pallas_call(
        body, name=name, grid=(T // tm, nk), in_specs=in_specs, out_specs=out_specs, out_shape=out_shape,
        scratch_shapes=[pltpu.VMEM((tm, N), F32)], compiler_params=_params(2),
    )(*args)


def _nt_matmul(name, dyb, w, scale=1.0, gate=None, up=None):
    T, D = dyb.shape
    N = w.shape[0]
    tm = _pick(T, (ROW_TILE, 512, 256, 128))
    tn = _pick(N, (512, 256, 128))
    swiglu = gate is not None

    def body(*refs):
        if swiglu:
            dy_ref, w_ref, g_ref, u_ref, dg_ref, du_ref = refs
        else:
            dy_ref, w_ref, o_ref, ob_ref = refs
        da = _nt(dy_ref[...], w_ref[...]) * scale
        if swiglu:
            dg_ref[...] = (da * g_ref[...].astype(F32)).astype(BF16)
            du_ref[...] = (da * u_ref[...].astype(F32)).astype(BF16)
        else:
            o_ref[...] = da
            ob_ref[...] = da.astype(BF16)

    tile = pl.BlockSpec((tm, tn), lambda i, j: (i, j))
    in_specs = [pl.BlockSpec((tm, D), lambda i, j: (i, 0)), pl.BlockSpec((tn, D), lambda i, j: (j, 0))]
    args = [dyb, w]
    if swiglu:
        in_specs += [tile, tile]
        args += [gate, up]
        out_shape = [jax.ShapeDtypeStruct((T, N), BF16)] * 2
    else:
        out_shape = [jax.ShapeDtypeStruct((T, N), F32), jax.ShapeDtypeStruct((T, N), BF16)]
    return pl.pallas_call(
        body, name=name, grid=(T // tm, N // tn), in_specs=in_specs, out_specs=[tile, tile],
        out_shape=out_shape, compiler_params=_params(2),
    )(*args)


def _nt_rms_bwd(name, As, Ws, x, g, dres):
    T, K = As[0].shape
    D = x.shape[1]
    na = len(As)
    tm = _pick(T, (ROW_TILE, 512, 256, 128))
    tk = _pick(K, (1024 // na, 512, 256, 128))
    nk = K // tk

    def body(*refs):
        a_refs = refs[:na]
        w_refs = refs[na:2 * na]
        x_hbm, g_ref, dres_hbm, acc, dxb_ref, dg_ref, x_ref, dres_ref, sems = refs[2 * na:]
        dx_ref = acc
        i, k = pl.program_id(0), pl.program_id(1)

        def row_copies():
            rows = pl.ds(pl.multiple_of(i * tm, tm), tm)
            return (pltpu.make_async_copy(x_hbm.at[rows, :], x_ref, sems.at[0]),
                    pltpu.make_async_copy(dres_hbm.at[rows, :], dres_ref, sems.at[1]))

        @pl.when(k == 0)
        def _():
            acc[...] = jnp.zeros_like(acc)
            for cp in row_copies():
                cp.start()

        for c0 in range(0, D, ACC_COLS):
            cols = slice(c0, min(c0 + ACC_COLS, D))
            part = _nt(a_refs[0][...], w_refs[0][cols, :])
            for a_ref, w_ref in zip(a_refs[1:], w_refs[1:]):
                part = part + _nt(a_ref[...], w_ref[cols, :])
            acc[:, cols] += part

        @pl.when(k == nk - 1)
        def _():
            for cp in row_copies():
                cp.wait()
            part = jnp.zeros((1, D), F32)
            for r0 in range(0, tm, EPILOGUE_ROWS):
                rows = slice(r0, r0 + min(EPILOGUE_ROWS, tm))
                dh = acc[rows, :]
                xv = x_ref[rows, :]
                r = _rms_r(xv)
                gd = dh * g_ref[...]
                dx = dres_ref[rows, :] + r * gd - xv * (r * r * r) * jnp.mean(gd * xv, axis=-1, keepdims=True)
                dx_ref[rows, :] = dx
                dxb_ref[rows, :] = dx.astype(BF16)
                part = part + jnp.sum(dh * xv * r, axis=0, keepdims=True)

            @pl.when(i == 0)
            def _():
                dg_ref[...] = part

            @pl.when(i > 0)
            def _():
                dg_ref[...] += part

    row = pl.BlockSpec((tm, D), lambda i, k: (i, 0), pipeline_mode=pl.Buffered(1))
    vec = pl.BlockSpec((1, D), lambda i, k: (0, 0))
    return pl.pallas_call(
        body, name=name, grid=(T // tm, nk),
        in_specs=[pl.BlockSpec((tm, tk), lambda i, k: (i, k))] * na
        + [pl.BlockSpec((D, tk), lambda i, k: (0, k))] * na + [ANY, vec, ANY],
        out_specs=[row, row, vec],
        out_shape=[jax.ShapeDtypeStruct((T, D), F32), jax.ShapeDtypeStruct((T, D), BF16),
                   jax.ShapeDtypeStruct((1, D), F32)],
        scratch_shapes=[pltpu.VMEM((tm, D), F32), pltpu.VMEM((tm, D), F32), pltpu.SemaphoreType.DMA((2,))],
        compiler_params=_params(2),
    )(*As, *Ws, x, g, dres)


def _tn_matmul(name, a, b, scale=1.0):
    T, M = a.shape
    N = b.shape[1]
    tn = _pick(N, (2048, 1408, 1280, 1024, 512, 256, 128))
    tm = _pick(M, tuple(c for c in (2048, 1408, 1024, 512, 256, 128) if c * tn <= TN_ACC_ELEMS))
    tk = _pick(T, (1024, 512, 256, 128))
    nk = T // tk

    def body(a_ref, b_ref, o_ref, acc):
        k = pl.program_id(2)

        @pl.when(k == 0)
        def _():
            acc[...] = jnp.zeros_like(acc)

        for r0 in range(0, tm, ACC_COLS):
            rows = slice(r0, min(r0 + ACC_COLS, tm))
            acc[rows, :] += _tn(a_ref[:, rows], b_ref[...])

        @pl.when(k == nk - 1)
        def _():
            o_ref[...] = (acc[...] * scale).astype(BF16)

    return pl.pallas_call(
        body, name=name, grid=(M // tm, N // tn, nk),
        in_specs=[pl.BlockSpec((tk, tm), lambda i, j, k: (k, i)), pl.BlockSpec((tk, tn), lambda i, j, k: (k, j))],
        out_specs=pl.BlockSpec((tm, tn), lambda i, j, k: (i, j)),
        out_shape=jax.ShapeDtypeStruct((M, N), BF16),
        scratch_shapes=[pltpu.VMEM((tm, tn), F32)], compiler_params=_params(3),
    )(a, b)


CONV_ROWS = 128
ROW_CHUNK = 32
LANE_CHUNK = 256


SUBLANES = 8


def _fill_shifts(sh, buf, rows):
    for p in range(1, SUBLANES):
        sh[p, 0:rows - SUBLANES, :] = buf[p:p + rows - SUBLANES, :]


def _tap(buf, sh, s, n, cols):
    p = s % SUBLANES
    return buf[s:s + n, cols] if p == 0 else sh[p, s - p:s - p + n, cols]


def _conv_fwd(name, z, w32, b, lg, lb, C):
    T = z.shape[0]
    tc = CONV_ROWS
    ntap = 31
    lc = _pick(C, (LANE_CHUNK, LANES))
    rpb = tc // HALO

    def body(zc_ref, zp_ref, w_ref, b_ref, lg_ref, lb_ref, yc_ref, ycv_ref, vbuf, ybuf, vsh):
        i = pl.program_id(0)
        zc = zc_ref[...]
        zp = zp_ref[...]
        vbuf[HALO:HALO + tc, :] = zc[:, :C] * _sigmoid(zc[:, C:])
        vbuf[0:HALO, :] = jnp.where(i > 0, zp[:, :C] * _sigmoid(zp[:, C:]), 0.0)
        _fill_shifts(vsh, vbuf, tc + HALO)
        for r0 in range(0, tc, ROW_CHUNK):
            for c0 in range(0, C, lc):
                cols = slice(c0, c0 + lc)
                acc = jnp.zeros((ROW_CHUNK, lc), F32) + b_ref[:, cols]
                for k in range(ntap):
                    acc = acc + w_ref[k:k + 1, cols] * _tap(vbuf, vsh, r0 + 2 + k, ROW_CHUNK, cols)
                ybuf[r0:r0 + ROW_CHUNK, cols] = acc
        y = ybuf[...]
        ycv_ref[...] = y
        mu = jnp.mean(y, axis=-1, keepdims=True)
        yc = y - mu
        rstd = lax.rsqrt(jnp.mean(yc * yc, axis=-1, keepdims=True) + EPS)
        ln = yc * rstd * lg_ref[...] + lb_ref[...]
        yc_ref[...] = (ln * _sigmoid(ln)).astype(BF16)

    vec = pl.BlockSpec((1, C), lambda i: (0, 0))
    return pl.pallas_call(
        body, name=name, grid=(T // tc,),
        in_specs=[pl.BlockSpec((tc, 2 * C), lambda i: (i, 0)),
                  pl.BlockSpec((HALO, 2 * C), lambda i: (jnp.maximum(i * rpb - 1, 0), 0)),
                  pl.BlockSpec((HALO, C), lambda i: (0, 0)), vec, vec, vec],
        out_specs=[pl.BlockSpec((tc, C), lambda i: (i, 0))] * 2,
        out_shape=[jax.ShapeDtypeStruct((T, 2 * C), BF16), jax.ShapeDtypeStruct((T, C), F32)],
        scratch_shapes=[pltpu.VMEM((tc + HALO, C), F32), pltpu.VMEM((tc, C), F32),
                        pltpu.VMEM((SUBLANES, tc + HALO, C), F32)],
        compiler_params=_params(1),
    )(z, z, w32, b, lg, lb)


def _conv_bwd(name, z, ycv, dycat, w32, lg, lb, C):
    T = z.shape[0]
    tc = CONV_ROWS
    ntap = 31
    lc = _pick(C, (LANE_CHUNK, LANES))
    rpb = tc // HALO
    nstep = T // tc
    nhb = T // HALO

    def ln_bwd(dyc, y, lgv, lbv):
        mu = jnp.mean(y, axis=-1, keepdims=True)
        yc = y - mu
        rstd = lax.rsqrt(jnp.mean(yc * yc, axis=-1, keepdims=True) + EPS)
        yn = yc * rstd
        ln = yn * lgv + lbv
        sg = _sigmoid(ln)
        dln = dyc * (sg * (1.0 + ln * (1.0 - sg)))
        dyn = dln * lgv
        dy = rstd * (dyn - jnp.mean(dyn, axis=-1, keepdims=True)
                     - yn * jnp.mean(dyn * yn, axis=-1, keepdims=True))
        return dy, dln, yn

    def body(zc_ref, zp_ref, y_ref, yn_ref, d_ref, dn_ref, w_ref, lg_ref, lb_ref,
             dz_ref, dw_ref, db_ref, dlg_ref, dlb_ref, vbuf, dbuf, dvbuf, dwacc, vsh, dsh):
        i = pl.program_id(0)
        lgv, lbv = lg_ref[...], lb_ref[...]
        zc = zc_ref[...]
        zp = zp_ref[...]
        a = zc[:, :C]
        sgt = _sigmoid(zc[:, C:])
        vbuf[HALO:HALO + tc, :] = a * sgt
        vbuf[0:HALO, :] = jnp.where(i > 0, zp[:, :C] * _sigmoid(zp[:, C:]), 0.0)
        dy, dln, yn = ln_bwd(d_ref[...], y_ref[...], lgv, lbv)
        dbuf[0:tc, :] = dy
        dyn_, _, _ = ln_bwd(dn_ref[...], yn_ref[...], lgv, lbv)
        dbuf[tc:tc + HALO, :] = jnp.where(i < nstep - 1, dyn_, 0.0)

        @pl.when(i == 0)
        def _():
            dwacc[...] = jnp.zeros_like(dwacc)
            db_ref[...] = jnp.zeros_like(db_ref)
            dlg_ref[...] = jnp.zeros_like(dlg_ref)
            dlb_ref[...] = jnp.zeros_like(dlb_ref)

        db_ref[...] += jnp.sum(dy, axis=0, keepdims=True)
        dlg_ref[...] += jnp.sum(dln * yn, axis=0, keepdims=True)
        dlb_ref[...] += jnp.sum(dln, axis=0, keepdims=True)

        _fill_shifts(vsh, vbuf, tc + HALO)
        _fill_shifts(dsh, dbuf, tc + HALO)
        for r0 in range(0, tc, ROW_CHUNK):
            for c0 in range(0, C, lc):
                cols = slice(c0, c0 + lc)
                dcur = dbuf[r0:r0 + ROW_CHUNK, cols]
                acc = jnp.zeros((ROW_CHUNK, lc), F32)
                for k in range(ntap):
                    acc = acc + w_ref[k:k + 1, cols] * _tap(dbuf, dsh, r0 + 30 - k, ROW_CHUNK, cols)
                    prod = dcur * _tap(vbuf, vsh, r0 + 2 + k, ROW_CHUNK, cols)
                    red = prod[0:8]
                    for q in range(8, ROW_CHUNK, 8):
                        red = red + prod[q:q + 8]
                    dwacc[8 * k:8 * k + 8, cols] += red
                dvbuf[r0:r0 + ROW_CHUNK, cols] = acc
        dv = dvbuf[...]
        dz_ref[:, :C] = (dv * sgt).astype(BF16)
        dz_ref[:, C:] = (dv * a * sgt * (1.0 - sgt)).astype(BF16)

        @pl.when(i == nstep - 1)
        def _():
            for k in range(ntap):
                dw_ref[k:k + 1, :] = jnp.sum(dwacc[8 * k:8 * k + 8, :], axis=0, keepdims=True)
            dw_ref[ntap:HALO, :] = jnp.zeros((HALO - ntap, C), F32)

    vec = pl.BlockSpec((1, C), lambda i: (0, 0))
    cur = pl.BlockSpec((tc, C), lambda i: (i, 0))
    nxt = pl.BlockSpec((HALO, C), lambda i: (jnp.minimum((i + 1) * rpb, nhb - 1), 0))
    return pl.pallas_call(
        body, name=name, grid=(nstep,),
        in_specs=[pl.BlockSpec((tc, 2 * C), lambda i: (i, 0)),
                  pl.BlockSpec((HALO, 2 * C), lambda i: (jnp.maximum(i * rpb - 1, 0), 0)),
                  cur, nxt, cur, nxt, pl.BlockSpec((HALO, C), lambda i: (0, 0)), vec, vec],
        out_specs=[pl.BlockSpec((tc, 2 * C), lambda i: (i, 0)), pl.BlockSpec((HALO, C), lambda i: (0, 0)),
                   vec, vec, vec],
        out_shape=[jax.ShapeDtypeStruct((T, 2 * C), BF16), jax.ShapeDtypeStruct((HALO, C), F32),
                   jax.ShapeDtypeStruct((1, C), F32), jax.ShapeDtypeStruct((1, C), F32),
                   jax.ShapeDtypeStruct((1, C), F32)],
        scratch_shapes=[pltpu.VMEM((tc + HALO, C), F32), pltpu.VMEM((tc + HALO, C), F32),
                        pltpu.VMEM((tc, C), F32), pltpu.VMEM((8 * HALO, C), F32),
                        pltpu.VMEM((SUBLANES, tc + HALO, C), F32), pltpu.VMEM((SUBLANES, tc + HALO, C), F32)],
        compiler_params=_params(1),
    )(z, z, ycv, ycv, dycat, dycat, w32, lg, lb)


def _seg_sum(u, bmat):
    hi = u.astype(BF16)
    lo = (u - hi.astype(F32)).astype(BF16)
    parts = [_nn(hi[:, c:c + LANES], bmat) + _nn(lo[:, c:c + LANES], bmat) for c in range(0, u.shape[1], LANES)]
    return jnp.concatenate(parts, axis=1)


def _attn_prep(name, z, gq, gk, bmat, A, c0, hd):
    T = z.shape[0]
    tm = _pick(T, (256, 128))

    def body(zq_ref, zk_ref, zv_ref, gq_ref, gk_ref, b_ref, o_ref):
        bm = b_ref[...]
        for idx, (z_ref, g_ref) in enumerate(((zq_ref, gq_ref), (zk_ref, gk_ref))):
            zv = z_ref[...]
            r = lax.rsqrt(_seg_sum(zv * zv, bm) * (1.0 / hd) + EPS)
            o_ref[:, idx * A:(idx + 1) * A] = zv * r * g_ref[...]
        o_ref[:, 2 * A:] = zv_ref[...]

    vec = pl.BlockSpec((1, A), lambda i: (0, 0))
    return pl.pallas_call(
        body, name=name, grid=(T // tm,),
        in_specs=[pl.BlockSpec((tm, A), lambda i: (i, c0)), pl.BlockSpec((tm, A), lambda i: (i, c0 + 1)),
                  pl.BlockSpec((tm, A), lambda i: (i, c0 + 2)), vec, vec,
                  pl.BlockSpec((LANES, LANES), lambda i: (0, 0))],
        out_specs=pl.BlockSpec((tm, 3 * A), lambda i: (i, 0)),
        out_shape=jax.ShapeDtypeStruct((T, 3 * A), F32), compiler_params=_params(1),
    )(z, z, z, gq, gk, bmat)


QK_SCALE = 0.125
ATTN_UNROLL = 4
ATTN_FWD_UNROLL = 8


def _fill_bias(bias, sl_ref, hp, d):
    qi = lax.broadcasted_iota(jnp.int32, (WINDOW, 2 * WINDOW), 0)
    kj = lax.broadcasted_iota(jnp.int32, (WINDOW, 2 * WINDOW), 1)
    dist = WINDOW + qi - kj
    inband = (dist >= 0) & (dist <= WINDOW)
    distf = dist.astype(F32)
    for hh in range(2):
        b = jnp.where(inband, -(sl_ref[2 * hp + hh] * d) * distf, NEG)
        bias[2 * hh + 1] = b
        bias[2 * hh] = jnp.where(kj >= WINDOW, b, NEG)


CHUNK = WINDOW * DILATIONS[-1]


def _deinterleave(dst, src, d, rows, dst_pitch, dst_off, src_off):
    for r in range(d):
        if d == 1:
            val = src[src_off:src_off + rows, :]
        else:
            val = src[pl.ds(src_off + r, rows, stride=d), :]
        lo = r * dst_pitch + dst_off
        dst[lo:lo + rows, :] = val.astype(dst.dtype)


def _interleave_add(dst, start, src, d, rows, src_pitch, src_off):
    for r in range(d):
        lo = r * src_pitch + src_off
        idx = pl.ds(start, rows) if d == 1 else pl.ds(start + r, rows, stride=d)
        dst[idx, :] += src[lo:lo + rows, :]


def _attn_fwd(name, qkv, slopes, ycat, A):
    T = qkv.shape[0]
    hpn, nch, nblk = A // LANES, T // CHUNK, CHUNK // WINDOW
    nbranch = len(DILATIONS)
    yoff = (ycat.shape[1] - A) // LANES

    def body(*refs):
        sl_ref, q_ref, k_ref, kp_ref, v_ref, vp_ref, _, y_ref, lg_ref, qd, kd, vd, od, ld, bias = refs[:15]
        onat, lnat = refs[15:15 + nbranch], refs[15 + nbranch:]
        hp, ch = pl.program_id(0), pl.program_id(1)
        lane = lax.broadcasted_iota(jnp.int32, (1, LANES), 1)
        first = lane < (LANES // 2)
        for bi, d in enumerate(DILATIONS):
            Ld = CHUNK // d
            seg = Ld + WINDOW
            nbr = Ld // WINDOW
            _deinterleave(qd, q_ref, d, Ld, Ld, 0, 0)
            for dst, cur, prev in ((kd, k_ref, kp_ref), (vd, v_ref, vp_ref)):
                _deinterleave(dst, prev, d, WINDOW, seg, 0, CHUNK - WINDOW * d)
                _deinterleave(dst, cur, d, Ld, seg, WINDOW, 0)
            _fill_bias(bias, sl_ref, hp, d)
            ob, lb = (onat[bi], lnat[bi]) if d == 1 else (od, ld)

            def step(it, carry, Ld=Ld, seg=seg, nbr=nbr, ob=ob, lb=lb):
                r, nl = it // nbr, it % nbr
                q0 = pl.multiple_of(r * Ld + nl * WINDOW, WINDOW)
                k0 = pl.multiple_of(r * seg + nl * WINDOW, WINDOW)
                later = jnp.where(ch * nbr + nl > 0, 1, 0)
                qb = qd[pl.ds(q0, WINDOW), :]
                k2 = kd[pl.ds(k0, 2 * WINDOW), :]
                v2 = vd[pl.ds(k0, 2 * WINDOW), :]
                res = []
                for hh in range(2):
                    mh = first if hh == 0 else jnp.logical_not(first)
                    s = _nt(jnp.where(mh, qb, jnp.zeros_like(qb)), k2) + bias[2 * hh + later]
                    mx = jnp.max(s, axis=-1, keepdims=True)
                    p = jnp.exp(s - mx)
                    den = jnp.sum(p, axis=-1, keepdims=True)
                    res.append((_nn(p.astype(BF16), v2) / den, mx + jnp.log(den)))
                ob[pl.ds(q0, WINDOW), :] = jnp.where(first, res[0][0], res[1][0])
                lb[pl.ds(q0, WINDOW), :] = jnp.where(first, res[0][1], res[1][1])
                return carry

            lax.fori_loop(0, nblk, step, 0, unroll=ATTN_FWD_UNROLL)
            if d > 1:
                for r in range(d):
                    onat[bi][pl.ds(r, Ld, stride=d), :] = od[r * Ld:(r + 1) * Ld, :]
                    lnat[bi][pl.ds(r, Ld, stride=d), :] = ld[r * Ld:(r + 1) * Ld, :]
        ls = [l[...] for l in lnat]
        mx = ls[0]
        for v in ls[1:]:
            mx = jnp.maximum(mx, v)
        es = [jnp.exp(v - mx) for v in ls]
        den = es[0]
        for e in es[1:]:
            den = den + e
        out = es[0] * onat[0][...]
        for e, o in zip(es[1:], onat[1:]):
            out = out + e * o[...]
        y_ref[...] = (out / den).astype(BF16)
        lg_ref[...] = mx + jnp.log(den)

    blk = lambda m: pl.BlockSpec((CHUNK, LANES), m)
    cur = lambda which: blk(lambda hp, ch: (ch, which * hpn + hp))
    prev = lambda which: blk(lambda hp, ch: (jnp.maximum(ch - 1, 0), which * hpn + hp))
    omap = blk(lambda hp, ch: (ch, hp))
    f32buf = pltpu.VMEM((CHUNK, LANES), F32)
    return pl.pallas_call(
        body, name=name, grid=(hpn, nch),
        in_specs=[pl.BlockSpec(memory_space=pltpu.SMEM), cur(0), cur(1), prev(1), cur(2), prev(2), ANY],
        out_specs=[blk(lambda hp, ch: (ch, yoff + hp)), omap],
        out_shape=[jax.ShapeDtypeStruct(ycat.shape, BF16), jax.ShapeDtypeStruct((T, A), F32)],
        input_output_aliases={6: 0},
        scratch_shapes=[pltpu.VMEM((CHUNK, LANES), BF16), pltpu.VMEM((2 * CHUNK, LANES), BF16),
                        pltpu.VMEM((2 * CHUNK, LANES), BF16), f32buf, f32buf,
                        pltpu.VMEM((4, WINDOW, 2 * WINDOW), F32)] + [f32buf] * (2 * nbranch),
        compiler_params=_params(2),
    )(slopes, qkv, qkv, qkv, qkv, qkv, ycat)


def _attn_bwd(name, qkv, dycat, ycat, lg, slopes, bmat, A, catoff):
    T = qkv.shape[0]
    hpn, nch, nblk = A // LANES, T // CHUNK, CHUNK // WINDOW
    co = catoff // LANES

    def body(sl_ref, q_ref, k_ref, kp_ref, v_ref, vp_ref, do_ref, o_ref, l_ref, b_ref, dq_ref, dk_ref, dv_ref,
             qd, kd, vd, dod, ddn, ddd, ldd, dqd, dkd, dvd, bias):
        hp, ch = pl.program_id(0), pl.program_id(1)
        lane = lax.broadcasted_iota(jnp.int32, (1, LANES), 1)
        first = lane < (LANES // 2)
        ddn[...] = _seg_sum(do_ref[...] * o_ref[...].astype(F32), b_ref[...])
        dq_ref[...] = jnp.zeros_like(dq_ref)

        @pl.when(ch == 0)
        def _():
            dk_ref[...] = jnp.zeros_like(dk_ref)
            dv_ref[...] = jnp.zeros_like(dv_ref)

        base = ch * CHUNK
        for d in DILATIONS:
            Ld = CHUNK // d
            seg = Ld + WINDOW
            nbr = Ld // WINDOW
            _deinterleave(qd, q_ref, d, Ld, Ld, 0, 0)
            _deinterleave(dod, do_ref, d, Ld, Ld, 0, 0)
            _deinterleave(ddd, ddn, d, Ld, Ld, 0, 0)
            _deinterleave(ldd, l_ref, d, Ld, Ld, 0, 0)
            for dst, cur, prev in ((kd, k_ref, kp_ref), (vd, v_ref, vp_ref)):
                _deinterleave(dst, prev, d, WINDOW, seg, 0, CHUNK - WINDOW * d)
                _deinterleave(dst, cur, d, Ld, seg, WINDOW, 0)
            dkd[0:d * seg, :] = jnp.zeros((d * seg, LANES), F32)
            dvd[0:d * seg, :] = jnp.zeros((d * seg, LANES), F32)
            _fill_bias(bias, sl_ref, hp, d)

            def step(it, carry, Ld=Ld, seg=seg, nbr=nbr):
                r, nl = it // nbr, it % nbr
                q0 = pl.multiple_of(r * Ld + nl * WINDOW, WINDOW)
                k0 = pl.multiple_of(r * seg + nl * WINDOW, WINDOW)
                later = jnp.where(ch * nbr + nl > 0, 1, 0)
                qb = qd[pl.ds(q0, WINDOW), :]
                k2 = kd[pl.ds(k0, 2 * WINDOW), :]
                v2 = vd[pl.ds(k0, 2 * WINDOW), :]
                dob = dod[pl.ds(q0, WINDOW), :]
                dd = ddd[pl.ds(q0, WINDOW), :]
                lb = ldd[pl.ds(q0, WINDOW), :]
                dk2 = jnp.zeros((2 * WINDOW, LANES), F32)
                dv2 = jnp.zeros((2 * WINDOW, LANES), F32)
                dqs = []
                for hh in range(2):
                    mh = first if hh == 0 else jnp.logical_not(first)
                    qh = jnp.where(mh, qb, jnp.zeros_like(qb))
                    doh = jnp.where(mh, dob, jnp.zeros_like(dob))
                    lcol = lb[:, hh * (LANES // 2):hh * (LANES // 2) + 1]
                    p = jnp.exp(_nt(qh, k2) + bias[2 * hh + later] - lcol)
                    dcol = dd[:, hh * (LANES // 2):hh * (LANES // 2) + 1]
                    ds = (p * (_nt(doh, v2) - dcol)).astype(BF16)
                    dqs.append(_nn(ds, k2))
                    dk2 = dk2 + _tn(ds, qh)
                    dv2 = dv2 + _tn(p.astype(BF16), doh)
                dqd[pl.ds(q0, WINDOW), :] = jnp.where(first, dqs[0], dqs[1])
                dkd[pl.ds(k0, 2 * WINDOW), :] += dk2
                dvd[pl.ds(k0, 2 * WINDOW), :] += dv2
                return carry

            lax.fori_loop(0, nblk, step, 0, unroll=ATTN_UNROLL)
            _interleave_add(dq_ref, 0, dqd, d, Ld, Ld, 0)
            for acc, out in ((dkd, dk_ref), (dvd, dv_ref)):
                _interleave_add(out, base, acc, d, Ld, seg, WINDOW)

                @pl.when(ch > 0)
                def _(acc=acc, out=out, d=d, seg=seg):
                    _interleave_add(out, base - WINDOW * d, acc, d, WINDOW, seg, 0)

    blk = lambda m: pl.BlockSpec((CHUNK, LANES), m)
    cur = lambda which: blk(lambda hp, ch: (ch, which * hpn + hp))
    prev = lambda which: blk(lambda hp, ch: (jnp.maximum(ch - 1, 0), which * hpn + hp))
    omap = blk(lambda hp, ch: (ch, hp))
    full = pl.BlockSpec((T, LANES), lambda hp, ch: (0, hp))
    f32buf = pltpu.VMEM((CHUNK, LANES), F32)
    bf16buf = pltpu.VMEM((CHUNK, LANES), BF16)
    return pl.pallas_call(
        body, name=name, grid=(hpn, nch),
        in_specs=[pl.BlockSpec(memory_space=pltpu.SMEM), cur(0), cur(1), prev(1), cur(2), prev(2),
                  blk(lambda hp, ch: (ch, co + hp)), blk(lambda hp, ch: (ch, co + hp)), omap,
                  pl.BlockSpec((LANES, LANES), lambda hp, ch: (0, 0))],
        out_specs=[omap, full, full],
        out_shape=[jax.ShapeDtypeStruct((T, A), F32)] * 3,
        scratch_shapes=[bf16buf, pltpu.VMEM((2 * CHUNK, LANES), BF16), pltpu.VMEM((2 * CHUNK, LANES), BF16),
                        bf16buf, f32buf, f32buf, f32buf, f32buf,
                        pltpu.VMEM((2 * CHUNK, LANES), F32), pltpu.VMEM((2 * CHUNK, LANES), F32),
                        pltpu.VMEM((4, WINDOW, 2 * WINDOW), F32)],
        compiler_params=_params(2),
    )(slopes, qkv, qkv, qkv, qkv, qkv, dycat, ycat, lg, bmat)


def _attn_bwd_combine(name, dzc, dqs, dks, dvs, z, gq, gk, bmat, fmat, A, c0, hd):
    T = z.shape[0]
    tm = _pick(T, (256, 128))
    nbr = len(dqs)
    W0 = dzc.shape[1]

    def body(*refs):
        dq_refs, dk_refs, dv_refs = refs[:nbr], refs[nbr:2 * nbr], refs[2 * nbr:3 * nbr]
        zq_ref, zk_ref, gq_ref, gk_ref, b_ref, f_ref, dzc_ref, dz_ref, dgq_ref, dgk_ref = refs[3 * nbr:]
        i = pl.program_id(0)
        bm = b_ref[...]
        dz_ref[:, :W0] = dzc_ref[...]

        def tot(rs):
            t = rs[0][...]
            for r in rs[1:]:
                t = t + r[...]
            return t

        for idx, (d_refs, z_ref, g_ref, dg_ref, gscale) in enumerate(
                ((dq_refs, zq_ref, gq_ref, dgq_ref, QK_SCALE), (dk_refs, zk_ref, gk_ref, dgk_ref, 1.0))):
            dy = tot(d_refs)
            zv = z_ref[...]
            r = lax.rsqrt(_seg_sum(zv * zv, bm) * (1.0 / hd) + EPS)
            gd = dy * g_ref[...]
            mean = _seg_sum(gd * zv, bm) * (1.0 / hd)
            dz_ref[:, W0 + idx * A:W0 + (idx + 1) * A] = (r * gd - zv * (r * r * r) * mean).astype(BF16)
            part = jnp.sum(dy * zv * r, axis=0, keepdims=True) * gscale

            @pl.when(i == 0)
            def _():
                dg_ref[...] = part

            @pl.when(i > 0)
            def _():
                dg_ref[...] += part

        dz_ref[:, W0 + 2 * A:] = tot(dv_refs).astype(BF16)

        @pl.when(i == T // tm - 1)
        def _():
            fm = f_ref[...]
            for dg_ref in (dgq_ref, dgk_ref):
                v = jnp.broadcast_to(dg_ref[...], (8, A))
                hi = v.astype(BF16)
                mid = (v - hi.astype(F32)).astype(BF16)
                lo = (v - hi.astype(F32) - mid.astype(F32)).astype(BF16)
                dg_ref[...] = (_nn(hi, fm) + _nn(mid, fm) + _nn(lo, fm))[0:1]

    blk = pl.BlockSpec((tm, A), lambda i: (i, 0))
    vec = pl.BlockSpec((1, A), lambda i: (0, 0))
    return pl.pallas_call(
        body, name=name, grid=(T // tm,),
        in_specs=[blk] * (3 * nbr) + [pl.BlockSpec((tm, A), lambda i: (i, c0)),
                                      pl.BlockSpec((tm, A), lambda i: (i, c0 + 1)), vec, vec,
                                      pl.BlockSpec((LANES, LANES), lambda i: (0, 0)),
                                      pl.BlockSpec((A, A), lambda i: (0, 0)),
                                      pl.BlockSpec((tm, W0), lambda i: (i, 0))],
        out_specs=[pl.BlockSpec((tm, W0 + 3 * A), lambda i: (i, 0)), vec, vec],
        out_shape=[jax.ShapeDtypeStruct((T, W0 + 3 * A), BF16), jax.ShapeDtypeStruct((1, A), F32),
                   jax.ShapeDtypeStruct((1, A), F32)],
        compiler_params=_params(1),
    )(*dqs, *dks, *dvs, z, z, gq, gk, bmat, fmat, dzc)


def _local_step(x, tgt, S, comm, hd):
    T, D = x.shape
    C = S["conv_b_dw"].shape[1]
    A = C
    H = A // hd
    Dmix = C + A
    c0 = (2 * C) // A
    slopes = 2.0 ** (-ALIBI_MAX_BIAS * jnp.arange(1, H + 1, dtype=F32) / H)
    seg = jnp.arange(LANES) // hd
    bmat = (seg[:, None] == seg[None, :]).astype(BF16)
    pos_in_head = jnp.arange(A) % hd
    fmat = (pos_in_head[:, None] == pos_in_head[None, :]).astype(BF16)
    gq = jnp.tile(S["q_norm_g"], (1, H)) * QK_SCALE
    gk = jnp.tile(S["k_norm_g"], (1, H))

    wg1, wu1 = comm.weights(("ffn1_w_gate", "ffn1_w_up"), None)
    h1, gate1, up1, a1 = _norm_matmul("ffn1_up", x, comm.tie(S["ffn1_norm_g"]), [wg1, wu1], True)
    wd1, win, w32 = comm.weights(("ffn1_w_down", "w_in", "conv_w32"), a1)
    x1 = _matmul_res("ffn1_down", a1, wd1, x, 0.5)
    h2, z = _norm_matmul("mix_in", x1, S["mix_norm_g"], [win], False)
    yc, ycv = _conv_fwd("conv_fwd", z, w32, S["conv_b_dw"], S["conv_ln_g"], S["conv_ln_b"], C)
    qkv = _attn_prep("attn_prep", z, gq, gk, bmat, A, c0, hd)
    ycat, lg = _attn_fwd("attn_fwd", qkv, slopes, yc, A)
    wout, wg2, wu2, wd2 = comm.weights(("w_out", "ffn2_w_gate", "ffn2_w_up", "ffn2_w_down"), lg)
    x2 = _matmul_res("mix_out", ycat, wout, x1, 1.0)
    h3, gate2, up2, a2 = _norm_matmul("ffn2_up", x2, S["ffn2_norm_g"], [wg2, wu2], True)
    dx3, dx3b, lossvec = _matmul_res("ffn2_down_loss", a2, wd2, x2, 0.5, tgt=tgt)

    G = {}
    dgate2, dup2 = _nt_matmul("ffn2_dact", dx3b, wd2, 0.5, gate2, up2)
    comm.reduce_begin("ffn2", {"ffn2_w_down": _tn_matmul("ffn2_dwd", a2, dx3b, 0.5),
                               "ffn2_w_gate": _tn_matmul("ffn2_dwg", h3, dgate2),
                               "ffn2_w_up": _tn_matmul("ffn2_dwu", h3, dup2)})
    dx2, dx2b, G["ffn2_norm_g"] = _nt_rms_bwd("ffn2_dx", [dgate2, dup2], [wg2, wu2],
                                              x2, comm.tie(S["ffn2_norm_g"]), dx3)
    dwout = _tn_matmul("mix_dwout", ycat, dx2b)
    dycat, _ = _nt_matmul("mix_dycat", dx2b, wout)
    dzc, G["conv_w32"], G["conv_b_dw"], G["conv_ln_g"], G["conv_ln_b"] = _conv_bwd(
        "conv_bwd", z, ycv, dycat, w32, S["conv_ln_g"], S["conv_ln_b"], C)
    dq, dk, dv = _attn_bwd("attn_bwd", qkv, dycat, ycat, lg, slopes, bmat, A, C)
    dz, G["q_norm_g"], G["k_norm_g"] = _attn_bwd_combine(
        "attn_bwd_combine", dzc, [dq], [dk], [dv], z, gq, gk, bmat, fmat, A, c0, hd)
    comm.reduce_end("ffn2", dz)
    comm.reduce_begin("mix", {"w_out": dwout, "w_in": _tn_matmul("mix_dwin", h2, dz)})
    dx1, dx1b, G["mix_norm_g"] = _nt_rms_bwd("mix_dx", [dz], [win], x1, comm.tie(S["mix_norm_g"]), dx2)
    dgate1, dup1 = _nt_matmul("ffn1_dact", dx1b, wd1, 0.5, gate1, up1)
    dwd1 = _tn_matmul("ffn1_dwd", a1, dx1b, 0.5)
    dwg1 = _tn_matmul("ffn1_dwg", h1, dgate1)
    dwu1 = _tn_matmul("ffn1_dwu", h1, dup1)
    comm.reduce_end("mix", dwu1)
    comm.reduce_begin("ffn1", {"ffn1_w_down": dwd1, "ffn1_w_gate": dwg1, "ffn1_w_up": dwu1})
    dx0, _, G["ffn1_norm_g"] = _nt_rms_bwd("ffn1_dx", [dgate1, dup1], [wg1, wu1],
                                           x, comm.tie(S["ffn1_norm_g"]), dx1)
    comm.reduce_end("ffn1", dx0)
    return lossvec, dx0, G


BIG = (("ffn1_w_gate", 1), ("ffn1_w_up", 1), ("ffn1_w_down", 0), ("w_in", 1), ("w_out", 0),
       ("ffn2_w_gate", 1), ("ffn2_w_up", 1), ("ffn2_w_down", 0))
AXIS = dict(BIG)
FLIPS = ((1, 0), (0, 1), (1, 1))
HBM = pl.BlockSpec(memory_space=pltpu.HBM)
SEM = pl.BlockSpec(memory_space=pltpu.SEMAPHORE)
EFFECT = pltpu.SideEffectType.DATAFLOW_SIDE_EFFECTING
TOKEN = jax.ShapeDtypeStruct((8, LANES), F32)


def _window(ref, shape, axis, slab=None, half=None):
    idx = [pl.ds(0, shape[0]), pl.ds(0, shape[1])]
    if slab is not None:
        n = shape[axis] // 4
        idx[axis] = pl.ds(pl.multiple_of(slab * n, 8), n)
    if half is not None:
        hs = shape[1 - axis] // 2
        idx[1 - axis] = pl.ds(pl.multiple_of(half * hs, 8), hs)
    return ref.at[idx[0], idx[1]]


def _position():
    return lax.axis_index("x"), lax.axis_index("y"), lax.axis_index("c")


def _half_shape(shape, axis):
    return (shape[0] // 2, shape[1]) if axis == 1 else (shape[0], shape[1] // 2)


def _slab_shape(shape, axis):
    return (shape[0], shape[1] // 4) if axis == 1 else (shape[0] // 4, shape[1])


def _piece_shape(shape, axis):
    return _half_shape(_slab_shape(shape, axis), axis)


def _full_shape(shard, axis):
    return (shard.shape[0], shard.shape[1] * 4) if axis == 1 else (shard.shape[0] * 4, shard.shape[1])


def _hbm(a):
    return pltpu.with_memory_space_constraint(a, pltpu.HBM)


def _remote(src, dst, send_sem, recv_sem, to):
    return pltpu.make_async_remote_copy(src_ref=src, dst_ref=dst, send_sem=send_sem, recv_sem=recv_sem,
                                        device_id=to, device_id_type=MESH)


def _place(name, pos, w, axis):
    R, Cc = w.shape
    tr = _pick(R, (256, 128, 64, 32, 16))
    nrb = R // tr

    def body(pos_ref, w_ref, o_ref):
        o_ref[...] = w_ref[...].astype(BF16)

    omap = (lambda i, p: (i, p[0])) if axis == 1 else (lambda i, p: (p[0] * nrb + i, 0))
    return pl.pallas_call(
        body, name=name,
        grid_spec=pltpu.PrefetchScalarGridSpec(
            num_scalar_prefetch=1, grid=(nrb,), in_specs=[pl.BlockSpec((tr, Cc), lambda i, p: (i, 0))],
            out_specs=pl.BlockSpec((tr, Cc), omap)),
        out_shape=jax.ShapeDtypeStruct(_full_shape(w, axis), BF16), compiler_params=_params(1),
    )(pos, w)


def _gather_now(name, axes, fulls):
    nt = len(fulls)
    shapes = [f.shape for f in fulls]

    def body(*refs):
        outs, token = refs[nt:2 * nt], refs[2 * nt]
        send_sems, recv_sems = refs[2 * nt + 1:]
        x, y, c = _position()
        j0 = 2 * x + y
        sib = (x, y, 1 - c)

        def copy(t, k, slab, half, to):
            win = _window(outs[t], shapes[t], axes[t], slab=slab, half=half)
            return _remote(win, win, send_sems.at[t, k], recv_sems.at[t, k], to)

        sends = []
        for k, (fx, fy) in enumerate(FLIPS):
            for t in range(nt):
                cp = copy(t, k, j0, c, (x ^ fx, y ^ fy, c))
                cp.start()
                sends.append(cp)
        for k, (fx, fy) in enumerate(FLIPS):
            js = 2 * (x ^ fx) + (y ^ fy)
            for t in range(nt):
                copy(t, k, js, c, sib).wait_recv()
                cp = copy(t, 3 + k, js, c, sib)
                cp.start()
                sends.append(cp)
        for k, (fx, fy) in enumerate(FLIPS):
            js = 2 * (x ^ fx) + (y ^ fy)
            for t in range(nt):
                copy(t, 3 + k, js, 1 - c, sib).wait_recv()
        for cp in sends:
            cp.wait_send()
        token[...] = jnp.zeros_like(token)

    res = pl.pallas_call(
        body, name=name, in_specs=[ANY] * nt,
        out_specs=[ANY] * nt + [pl.BlockSpec(memory_space=pltpu.VMEM)],
        out_shape=[jax.ShapeDtypeStruct(s, BF16) for s in shapes] + [TOKEN],
        input_output_aliases={t: t for t in range(nt)},
        scratch_shapes=[pltpu.SemaphoreType.DMA((nt, 6)), pltpu.SemaphoreType.DMA((nt, 6))],
    )(*fulls)
    return list(res[:nt]), res[nt]


def _split_start(name, arrays, ncopies, plan):
    na = len(arrays)

    def body(*refs):
        ins = refs[:na]
        send_sems, recv_sems = refs[na], refs[na + 1]
        token = refs[-1]
        x, y, c = _position()
        for i, (src, dst, to) in enumerate(plan(ins, x, y, c)):
            _remote(src, dst, send_sems.at[i], recv_sems.at[i], to).start()
        token[...] = jnp.zeros_like(token)

    res = pl.pallas_call(
        body, name=name, in_specs=[HBM] * na,
        out_specs=tuple([SEM, SEM] + [HBM] * na + [pl.BlockSpec(memory_space=pltpu.VMEM)]),
        out_shape=tuple([pltpu.SemaphoreType.DMA((ncopies,)), pltpu.SemaphoreType.DMA((ncopies,))]
                        + [pltpu.HBM(a.shape, a.dtype) for a in arrays] + [TOKEN]),
        input_output_aliases={i: 2 + i for i in range(na)},
        compiler_params=pltpu.CompilerParams(has_side_effects=EFFECT),
    )(*[_hbm(a) for a in arrays])
    return (res[0], res[1]), list(res[2:2 + na]), res[-1]


def _split_wait(name, arrays, sems, after, plan):
    na = len(arrays)

    def body(*refs):
        ins = refs[:na]
        send_sems, recv_sems = refs[na], refs[na + 1]
        x, y, c = _position()
        for i, (src, dst, to) in enumerate(plan(ins, x, y, c)):
            cp = _remote(src, dst, send_sems.at[i], recv_sems.at[i], to)
            cp.wait_send()
            cp.wait_recv()

    res = pl.pallas_call(
        body, name=name, in_specs=[HBM] * na + [SEM, SEM, ANY],
        out_specs=tuple([HBM] * na), out_shape=tuple(pltpu.HBM(a.shape, a.dtype) for a in arrays),
        input_output_aliases={i: i for i in range(na)},
        compiler_params=pltpu.CompilerParams(has_side_effects=EFFECT),
    )(*arrays, *sems, after)
    return list(res)


def _gather_plan(axes, shapes, conv_shape):
    nt = len(axes)

    def plan(refs, x, y, c):
        j0 = 2 * x + y
        out = []
        for fx, fy in FLIPS:
            to = (x ^ fx, y ^ fy, c)
            for t in range(nt):
                win = _window(refs[t], shapes[t], axes[t], slab=j0, half=c)
                out.append((win, win, to))
            if conv_shape is not None:
                win = _window(refs[nt], conv_shape, 1, slab=j0)
                out.append((win, win, to))
        return out

    return plan


def _gather_finish(name, axes, fulls):
    nt = len(axes)
    shapes = [f.shape for f in fulls]

    def body(*refs):
        outs = refs[nt:2 * nt]
        send_sems, recv_sems = refs[2 * nt:]
        x, y, c = _position()
        sib = (x, y, 1 - c)
        cps = []
        for k, (fx, fy) in enumerate(FLIPS):
            js = 2 * (x ^ fx) + (y ^ fy)
            for t in range(nt):
                landed = _window(outs[t], shapes[t], axes[t], slab=js, half=c)
                cp = _remote(landed, landed, send_sems.at[t, k], recv_sems.at[t, k], sib)
                cp.start()
                cps.append(cp)
        for k, (fx, fy) in enumerate(FLIPS):
            js = 2 * (x ^ fx) + (y ^ fy)
            for t in range(nt):
                other = _window(outs[t], shapes[t], axes[t], slab=js, half=1 - c)
                _remote(other, other, send_sems.at[t, k], recv_sems.at[t, k], sib).wait_recv()
        for cp in cps:
            cp.wait_send()

    res = pl.pallas_call(
        body, name=name, in_specs=[ANY] * nt, out_specs=[ANY] * nt,
        out_shape=[jax.ShapeDtypeStruct(f.shape, f.dtype) for f in fulls],
        input_output_aliases={t: t for t in range(nt)},
        scratch_shapes=[pltpu.SemaphoreType.DMA((nt, 3)), pltpu.SemaphoreType.DMA((nt, 3))],
    )(*fulls)
    return list(res)


def _pair_exchange(name, srcs, windows, out_shapes, dtype):
    nt = len(srcs)

    def body(*refs):
        ins, outs = refs[:nt], refs[nt:2 * nt]
        send_sems, recv_sems = refs[2 * nt:]
        x, y, c = _position()
        cps = []
        for t in range(nt):
            cp = _remote(windows[t](ins[t], c), outs[t], send_sems.at[t], recv_sems.at[t], (x, y, 1 - c))
            cp.start()
            cps.append(cp)
        for cp in cps:
            cp.wait()

    return pl.pallas_call(
        body, name=name, in_specs=[ANY] * nt, out_specs=[ANY] * nt,
        out_shape=[jax.ShapeDtypeStruct(s, dtype) for s in out_shapes],
        scratch_shapes=[pltpu.SemaphoreType.DMA((nt,)), pltpu.SemaphoreType.DMA((nt,))],
    )(*srcs)


def _scatter_plan(axes, shapes):
    nt = len(axes)

    def plan(refs, x, y, c):
        out = []
        for k, (fx, fy) in enumerate(FLIPS):
            js = 2 * (x ^ fx) + (y ^ fy)
            for t in range(nt):
                src = _window(refs[t], _half_shape(shapes[t], axes[t]), axes[t], slab=js)
                out.append((src, refs[nt + t].at[k], (x ^ fx, y ^ fy, c)))
        return out

    return plan


def _gather_small(packed):
    R, Cc = packed.shape

    def body(p_ref, o_ref, send_sems, recv_sems, loc_sem):
        x, y, c = _position()
        me = 4 * x + 2 * y + c
        mine = pltpu.make_async_copy(p_ref, o_ref.at[me], loc_sem)
        mine.start()
        cps = []
        for k in range(1, 8):
            fx, fy, fc = (k >> 2) & 1, (k >> 1) & 1, k & 1
            cp = pltpu.make_async_remote_copy(
                src_ref=p_ref, dst_ref=o_ref.at[me], send_sem=send_sems.at[k - 1], recv_sem=recv_sems.at[k - 1],
                device_id=(x ^ fx, y ^ fy, c ^ fc), device_id_type=MESH)
            cp.start()
            cps.append(cp)
        for cp in cps:
            cp.wait()
        mine.wait()

    return pl.pallas_call(
        body, name="gather_small_grads", in_specs=[ANY], out_specs=ANY,
        out_shape=jax.ShapeDtypeStruct((8, R, Cc), F32),
        scratch_shapes=[pltpu.SemaphoreType.DMA((7,)), pltpu.SemaphoreType.DMA((7,)), pltpu.SemaphoreType.DMA],
    )(packed)


def _sum_slots(name, slots):
    n, R, Cc = slots.shape

    def body(s_ref, o_ref):
        t = s_ref[0]
        for i in range(1, n):
            t = t + s_ref[i]
        o_ref[...] = t

    return pl.pallas_call(
        body, name=name, grid=(1,), in_specs=[pl.BlockSpec((n, R, Cc), lambda i: (0, 0, 0))],
        out_specs=pl.BlockSpec((R, Cc), lambda i: (0, 0)), out_shape=jax.ShapeDtypeStruct((R, Cc), F32),
        compiler_params=_params(1),
    )(slots)


def _pair_sum(name, pos, g, land, shape, axis):
    hshape = _half_shape(shape, axis)
    R, Cc = hshape
    tr = _pick(R, (256, 128, 64, 32, 16))
    nrb = R // tr

    def body(pos_ref, g_ref, l_ref, o_ref):
        o_ref[...] = (g_ref[...].astype(F32) + l_ref[...].astype(F32)).astype(BF16)

    if axis == 1:
        gmap = lambda i, p: (p[1] * nrb + i, 0)
    else:
        gmap = lambda i, p: (i, p[1])
    blk = pl.BlockSpec((tr, Cc), lambda i, p: (i, 0))
    return pl.pallas_call(
        body, name=name,
        grid_spec=pltpu.PrefetchScalarGridSpec(
            num_scalar_prefetch=1, grid=(nrb,), in_specs=[pl.BlockSpec((tr, Cc), gmap), blk], out_specs=blk),
        out_shape=jax.ShapeDtypeStruct(hshape, BF16), compiler_params=_params(1),
    )(pos, g, land)


def _chip_sum(name, pos, sb, land, shape, axis):
    hshape = _half_shape(shape, axis)
    pshape = _piece_shape(shape, axis)
    R, Cc = pshape
    tr = _pick(R, (256, 128, 64, 32, 16))
    nrb = R // tr

    def body(pos_ref, s_ref, l_ref, o_ref):
        t = s_ref[...].astype(F32)
        for k in range(3):
            t = t + l_ref[k].astype(F32)
        o_ref[...] = t

    if axis == 1:
        smap = lambda i, p: (i, p[0])
    else:
        smap = lambda i, p: (p[0] * nrb + i, 0)
    return pl.pallas_call(
        body, name=name,
        grid_spec=pltpu.PrefetchScalarGridSpec(
            num_scalar_prefetch=1, grid=(nrb,),
            in_specs=[pl.BlockSpec((tr, Cc), smap), pl.BlockSpec((3, tr, Cc), lambda i, p: (0, i, 0))],
            out_specs=pl.BlockSpec((tr, Cc), lambda i, p: (i, 0))),
        out_shape=jax.ShapeDtypeStruct(pshape, F32), compiler_params=_params(1),
    )(pos, sb, land)


def _adam_math(w, g, m, v):
    m = ADAM_B1 * m + (1.0 - ADAM_B1) * g
    v = ADAM_B2 * v + (1.0 - ADAM_B2) * (g * g)
    m_hat = m / (1.0 - ADAM_B1 ** ADAM_STEP)
    v_hat = v / (1.0 - ADAM_B2 ** ADAM_STEP)
    delta = -ADAM_LR * (m_hat / (jnp.sqrt(v_hat) + ADAM_EPS) + ADAM_WD * w)
    return delta, m, v


def _adamw_halves(name, pos, w, m, v, mine, theirs, axis):
    R, Cc = w.shape
    hr, hc = mine.shape
    tr = _pick(hr, (256, 128, 64, 32, 16))
    nrb = hr // tr

    def body(pos_ref, w_ref, m_ref, v_ref, a_ref, b_ref, g_ref, d_ref, nm_ref, nv_ref):
        half = pl.program_id(0)
        g = jnp.where(half == pos_ref[1], a_ref[...], b_ref[...])
        d, nm, nv = _adam_math(w_ref[...], g, m_ref[...], v_ref[...])
        g_ref[...] = g
        d_ref[...] = d
        nm_ref[...] = nm
        nv_ref[...] = nv

    if axis == 1:
        wmap = lambda h, i, p: (h * nrb + i, 0)
    else:
        wmap = lambda h, i, p: (i, h)
    wblk = pl.BlockSpec((tr, hc), wmap)
    ablk = pl.BlockSpec((tr, hc), lambda h, i, p: (jnp.where(h == p[1], i, 0), 0))
    bblk = pl.BlockSpec((tr, hc), lambda h, i, p: (jnp.where(h == p[1], 0, i), 0))
    return pl.pallas_call(
        body, name=name,
        grid_spec=pltpu.PrefetchScalarGridSpec(
            num_scalar_prefetch=1, grid=(2, nrb), in_specs=[wblk, wblk, wblk, ablk, bblk], out_specs=[wblk] * 4),
        out_shape=[jax.ShapeDtypeStruct((R, Cc), F32)] * 4, compiler_params=_params(2),
    )(pos, w, m, v, mine, theirs)


def _adamw_small(name, w, g, m, v):
    def body(w_ref, g_ref, m_ref, v_ref, d_ref, nm_ref, nv_ref):
        d, nm, nv = _adam_math(w_ref[...], g_ref[...], m_ref[...], v_ref[...])
        d_ref[...] = d
        nm_ref[...] = nm
        nv_ref[...] = nv

    blk = pl.BlockSpec(w.shape, lambda i: (0, 0))
    return pl.pallas_call(
        body, name=name, grid=(1,), in_specs=[blk] * 4, out_specs=[blk] * 3,
        out_shape=[jax.ShapeDtypeStruct(w.shape, F32)] * 3, compiler_params=_params(1),
    )(w, g, m, v)


SMALL = ("ffn1_norm_g", "mix_norm_g", "conv_b_dw", "conv_ln_g", "conv_ln_b", "q_norm_g", "k_norm_g", "ffn2_norm_g")
ORDER = ("ffn1_norm_g", "ffn1_w_gate", "ffn1_w_up", "ffn1_w_down", "mix_norm_g", "w_in", "conv_w_dw", "conv_b_dw",
         "conv_ln_g", "conv_ln_b", "q_norm_g", "k_norm_g", "w_out", "ffn2_norm_g", "ffn2_w_gate", "ffn2_w_up",
         "ffn2_w_down")
GATHER_FIRST = ("ffn1_w_gate", "ffn1_w_up")
GATHER_SECOND = ("ffn1_w_down", "w_in")
GATHER_THIRD = ("w_out", "ffn2_w_gate", "ffn2_w_up", "ffn2_w_down")


class _Exchange:
    def __init__(self, P, Mo, Vo, conv_shard, pos):
        self.P, self.Mo, self.Vo, self.pos = P, Mo, Vo, pos
        self.tokens = []
        self.pending = {}
        self.reducing = {}
        self.results = {}
        placed = {n: _place("place_" + n, pos, P[n][0], a) for n, a in BIG}
        self.shapes = {n: placed[n].shape for n, _ in BIG}
        first, tok = _gather_now("gather_first", [AXIS[n] for n in GATHER_FIRST], [placed[n] for n in GATHER_FIRST])
        self.ready = dict(zip(GATHER_FIRST, first))
        cq = conv_shard.shape[1]
        conv_full = lax.dynamic_update_slice(jnp.zeros((conv_shard.shape[0], 4 * cq), F32), conv_shard,
                                             (0, pos[0] * cq))
        for gname, names, conv in (("second", GATHER_SECOND, conv_full), ("third", GATHER_THIRD, None)):
            axes = [AXIS[n] for n in names]
            shapes = [self.shapes[n] for n in names]
            arrays = [placed[n] for n in names] + ([conv] if conv is not None else [])
            small = min(range(len(arrays)), key=lambda i: arrays[i].size)
            arrays[small] = arrays[small] + tok[0, 0].astype(arrays[small].dtype)
            plan = _gather_plan(axes, shapes, conv.shape if conv is not None else None)
            sems, thru, tok = _split_start("gather_%s_start" % gname, arrays, 3 * len(arrays), plan)
            self.tokens.append(tok)
            for n in names + (("conv_w32",) if conv is not None else ()):
                self.pending[n] = (gname, names, axes, plan, sems, thru, conv is not None)

    def tie(self, v):
        for tok in self.tokens:
            v = v + tok[0:1, 0:1]
        self.tokens = []
        return v

    def weights(self, names, after):
        if names[0] in self.pending:
            gname, gnames, axes, plan, sems, thru, has_conv = self.pending[names[0]]
            thru = _split_wait("gather_%s_wait" % gname, thru, sems, after, plan)
            nt = len(gnames)
            fulls = _gather_finish("gather_%s_finish" % gname, axes, thru[:nt])
            for n, f in zip(gnames, fulls):
                self.ready[n] = f
                del self.pending[n]
            if has_conv:
                self.ready["conv_w32"] = thru[nt]
                del self.pending["conv_w32"]
        return [self.ready[n] for n in names]

    def reduce_begin(self, gname, grads):
        names = list(grads)
        axes = [AXIS[n] for n in names]
        shapes = [self.shapes[n] for n in names]
        gs = [grads[n] for n in names]
        to_sibling = [(lambda ref, c, s=s, a=a: _window(ref, s, a, half=1 - c)) for s, a in zip(shapes, axes)]
        landed = _pair_exchange("pair_exchange_" + gname, gs, to_sibling,
                                [_half_shape(s, a) for s, a in zip(shapes, axes)], BF16)
        sbs = [_pair_sum("pair_sum_" + n, self.pos, g, l, s, a)
               for n, a, g, l, s in zip(names, axes, gs, landed, shapes)]
        lands = [lax.empty((3,) + _piece_shape(s, a), BF16) for s, a in zip(shapes, axes)]
        plan = _scatter_plan(axes, shapes)
        sems, thru, tok = _split_start("scatter_%s_start" % gname, sbs + lands, 3 * len(names), plan)
        self.tokens.append(tok)
        self.reducing[gname] = (names, axes, shapes, plan, sems, thru)

    def reduce_end(self, gname, after):
        names, axes, shapes, plan, sems, thru = self.reducing.pop(gname)
        nt = len(names)
        thru = _split_wait("scatter_%s_wait" % gname, thru, sems, after, plan)
        mine = [_chip_sum("chip_sum_" + n, self.pos, sb, l, s, a)
                for n, a, sb, l, s in zip(names, axes, thru[:nt], thru[nt:], shapes)]
        theirs = _pair_exchange("half_exchange_" + gname, mine, [(lambda ref, c: ref)] * nt,
                                [m.shape for m in mine], F32)
        for n, a, mi, th in zip(names, axes, mine, theirs):
            g, d, nm, nv = _adamw_halves("adamw_" + n, self.pos, self.P[n][0], self.Mo[n][0], self.Vo[n][0],
                                         mi, th, a)
            self.results[n] = (g[None], d[None], nm[None], nv[None])


def kernel(x, ffn1_norm_g, ffn1_w_gate, ffn1_w_up, ffn1_w_down, mix_norm_g, w_in, conv_w_dw, conv_b_dw, conv_ln_g, conv_ln_b, q_norm_g, k_norm_g, w_out, ffn2_norm_g, ffn2_w_gate, ffn2_w_up, ffn2_w_down, loss_target, m_ffn1_norm_g, m_ffn1_w_gate, m_ffn1_w_up, m_ffn1_w_down, m_mix_norm_g, m_w_in, m_conv_w_dw, m_conv_b_dw, m_conv_ln_g, m_conv_ln_b, m_q_norm_g, m_k_norm_g, m_w_out, m_ffn2_norm_g, m_ffn2_w_gate, m_ffn2_w_up, m_ffn2_w_down, v_ffn1_norm_g, v_ffn1_w_gate, v_ffn1_w_up, v_ffn1_w_down, v_mix_norm_g, v_w_in, v_conv_w_dw, v_conv_b_dw, v_conv_ln_g, v_conv_ln_b, v_q_norm_g, v_k_norm_g, v_w_out, v_ffn2_norm_g, v_ffn2_w_gate, v_ffn2_w_up, v_ffn2_w_down):
    args = dict(locals())
    P = {n: args[n] for n in ORDER}
    Mo = {n: args["m_" + n] for n in ORDER}
    Vo = {n: args["v_" + n] for n in ORDER}
    xs = x[0]
    tgt = loss_target[0]
    T, D = xs.shape
    hd = q_norm_g.shape[-1]
    C = conv_b_dw.shape[-1]
    ntap = conv_w_dw.shape[1]
    cx, cy, cc = _position()
    j0 = 2 * cx + cy
    pos = jnp.stack([j0, cc]).astype(jnp.int32)

    conv_shard = jnp.pad(conv_w_dw[0], ((0, HALO - ntap), (0, 0)))
    comm = _Exchange(P, Mo, Vo, conv_shard, pos)
    lossvec, dx0, G = _local_step(xs, tgt, {n: P[n] for n in SMALL}, comm, hd)
    loss = lax.psum(0.5 / D * jnp.sum(lossvec), AXES)
    grads, deltas, new_m, new_v = {}, {}, {}, {}
    for n, _ in BIG:
        grads[n], deltas[n], new_m[n], new_v[n] = comm.results[n]

    rows = [G["conv_w32"]]
    for n in ("ffn1_norm_g", "mix_norm_g", "ffn2_norm_g"):
        rows.append(G[n].reshape(D // C, C))
    for n in ("conv_b_dw", "conv_ln_g", "conv_ln_b", "q_norm_g", "k_norm_g"):
        rows.append(G[n])
    packed = jnp.concatenate(rows, axis=0)
    packed = jnp.pad(packed, ((0, -packed.shape[0] % 8), (0, 0)))
    total = _sum_slots("sum_small_grads", _gather_small(packed))
    r = HALO
    small_g = {}
    cq = C // 4
    small_g["conv_w_dw"] = lax.dynamic_slice(total[:ntap], (0, j0 * cq), (ntap, cq))
    for n in ("ffn1_norm_g", "mix_norm_g", "ffn2_norm_g"):
        small_g[n] = total[r:r + D // C].reshape(1, D)
        r += D // C
    for n in ("conv_b_dw", "conv_ln_g", "conv_ln_b"):
        small_g[n] = total[r:r + 1]
        r += 1
    for n in ("q_norm_g", "k_norm_g"):
        small_g[n] = total[r:r + 1, :hd]
        r += 1
    for n in ("conv_w_dw",) + SMALL:
        lead = n == "conv_w_dw"
        w2, m2, v2 = (P[n][0], Mo[n][0], Vo[n][0]) if lead else (P[n], Mo[n], Vo[n])
        d, nm, nv = _adamw_small("adamw_" + n, w2, small_g[n], m2, v2)
        if lead:
            grads[n], deltas[n], new_m[n], new_v[n] = small_g[n][None], d[None], nm[None], nv[None]
        else:
            grads[n], deltas[n], new_m[n], new_v[n] = small_g[n], d, nm, nv

    return (loss, dx0[None], *[grads[n] for n in ORDER], *[deltas[n] for n in ORDER],
            *[new_m[n] for n in ORDER], *[new_v[n] for n in ORDER])
```

```python
import jax
import jax.numpy as jnp
from jax import lax
from jax.experimental import pallas as pl
from jax.experimental.pallas import tpu as pltpu

F32 = jnp.float32
BF16 = jnp.bfloat16
EPS = 1e-6
WINDOW = 128
DILATIONS = (1, 4, 16)
ALIBI_MAX_BIAS = 8.0
LANES = 128
HALO = 32
ADAM_LR, ADAM_B1, ADAM_B2, ADAM_EPS, ADAM_WD, ADAM_STEP = 0.001, 0.9, 0.999, 1e-08, 0.01, 10
VMEM_LIMIT_MB = 62
ROW_TILE = 1024
TN_ACC_ELEMS = 3 * 1024 * 1024
EPILOGUE_ROWS = 256
ACC_COLS = 512
MESH = pl.DeviceIdType.MESH
ANY = pl.BlockSpec(memory_space=pl.ANY)
AXES = ("x", "y", "c")
NEG = -1e30


def _pick(n, cands):
    for c in cands:
        if n % c == 0:
            return c
    return n


def _params(nsem):
    return pltpu.CompilerParams(dimension_semantics=("arbitrary",) * nsem,
                                vmem_limit_bytes=VMEM_LIMIT_MB << 20)


def _nn(a, b):
    return jnp.dot(a, b, preferred_element_type=F32)


def _nt(a, b):
    return lax.dot_general(a, b, (((1,), (1,)), ((), ())), preferred_element_type=F32)


def _tn(a, b):
    return lax.dot_general(a, b, (((0,), (0,)), ((), ())), preferred_element_type=F32)


def _sigmoid(v):
    return jax.nn.sigmoid(v)


def _rms_r(xv):
    return lax.rsqrt(jnp.mean(xv * xv, axis=-1, keepdims=True) + EPS)


def _norm_matmul(name, x, g, ws, swiglu):
    T, D = x.shape
    N = ws[0].shape[1]
    tm = _pick(T, (ROW_TILE, 512, 256, 128))
    tn = _pick(N, (512, 256, 128))
    nw = len(ws)

    def body(*refs):
        x_ref, g_ref = refs[:2]
        w_refs = refs[2:2 + nw]
        outs = refs[2 + nw:-1]
        hs = refs[-1]

        @pl.when(pl.program_id(1) == 0)
        def _():
            for r0 in range(0, tm, EPILOGUE_ROWS):
                rows = slice(r0, r0 + min(EPILOGUE_ROWS, tm))
                xv = x_ref[rows, :]
                hv = (xv * _rms_r(xv) * g_ref[...]).astype(BF16)
                hs[rows, :] = hv
                outs[0][rows, :] = hv

        h = hs[...]
        if swiglu:
            gt = _nn(h, w_refs[0][...])
            u = _nn(h, w_refs[1][...])
            sg = _sigmoid(gt)
            silu = gt * sg
            outs[1][...] = (u * (sg * (1.0 + gt * (1.0 - sg)))).astype(BF16)
            outs[2][...] = silu.astype(BF16)
            outs[3][...] = (silu * u).astype(BF16)
        else:
            outs[1][...] = _nn(h, w_refs[0][...])

    row = pl.BlockSpec((tm, D), lambda i, j: (i, 0))
    col = pl.BlockSpec((D, tn), lambda i, j: (0, j))
    tile = pl.BlockSpec((tm, tn), lambda i, j: (i, j))
    if swiglu:
        out_shape = [jax.ShapeDtypeStruct((T, D), BF16)] + [jax.ShapeDtypeStruct((T, N), BF16)] * 3
        out_specs = [row, tile, tile, tile]
    else:
        out_shape = [jax.ShapeDtypeStruct((T, D), BF16), jax.ShapeDtypeStruct((T, N), F32)]
        out_specs = [row, tile]
    return pl.pallas_call(
        body, name=name, grid=(T // tm, N // tn),
        in_specs=[row, pl.BlockSpec((1, D), lambda i, j: (0, 0))] + [col] * nw,
        out_specs=out_specs, out_shape=out_shape,
        scratch_shapes=[pltpu.VMEM((tm, D), BF16)],
        compiler_params=_params(2),
    )(x, g, *ws)


def _matmul_res(name, a, w, res, scale, tgt=None):
    T, K = a.shape
    N = w.shape[1]
    loss = tgt is not None
    tm = _pick(T, (512, 256, 128))
    tk = _pick(K, (1408, 1024, 512, 256, 128))
    nk = K // tk

    def body(*refs):
        if loss:
            a_ref, w_ref, res_ref, tgt_ref, dx_ref, dxb_ref, lv_ref, acc = refs
        else:
            a_ref, w_ref, res_ref, out_ref, acc = refs
        i, k = pl.program_id(0), pl.program_id(1)

        @pl.when(k == 0)
        def _():
            acc[...] = jnp.zeros_like(acc)

        acc[...] += _nn(a_ref[...], w_ref[...])

        @pl.when(k == nk - 1)
        def _():
            part = jnp.zeros((1, N), F32)
            for r0 in range(0, tm, EPILOGUE_ROWS):
                rows = slice(r0, r0 + min(EPILOGUE_ROWS, tm))
                val = res_ref[rows, :] + scale * acc[rows, :]
                if loss:
                    dv = val - tgt_ref[rows, :]
                    dx = dv * (1.0 / N)
                    dx_ref[rows, :] = dx
                    dxb_ref[rows, :] = dx.astype(BF16)
                    part = part + jnp.sum(dv * dv, axis=0, keepdims=True)
                else:
                    out_ref[rows, :] = val
            if loss:
                @pl.when(i == 0)
                def _():
                    lv_ref[...] = part

                @pl.when(i > 0)
                def _():
                    lv_ref[...] += part

    row = pl.BlockSpec((tm, N), lambda i, k: (i, 0))
    in_specs = [pl.BlockSpec((tm, tk), lambda i, k: (i, k)), pl.BlockSpec((tk, N), lambda i, k: (k, 0)), row]
    args = [a, w, res]
    if loss:
        in_specs.append(row)
        args.append(tgt)
        out_specs = [row, row, pl.BlockSpec((1, N), lambda i, k: (0, 0))]
        out_shape = [jax.ShapeDtypeStruct((T, N), F32), jax.ShapeDtypeStruct((T, N), BF16),
                     jax.ShapeDtypeStruct((1, N), F32)]
    else:
        out_specs = row
        out_shape = jax.ShapeDtypeStruct((T, N), F32)
    return pl.pallas_call(
        body, name=name, grid=(T // tm, nk), in_specs=in_specs, out_specs=out_specs, out_shape=out_shape,
        scratch_shapes=[pltpu.VMEM((tm, N), F32)], compiler_params=_params(2),
    )(*args)


def _nt_matmul(name, dyb, w, scale=1.0, gate=None, up=None):
    T, D = dyb.shape
    N = w.shape[0]
    tm = _pick(T, (ROW_TILE, 512, 256, 128))
    tn = _pick(N, (512, 256, 128))
    swiglu = gate is not None

    def body(*refs):
        if swiglu:
            dy_ref, w_ref, g_ref, u_ref, dg_ref, du_ref = refs
        else:
            dy_ref, w_ref, o_ref, ob_ref = refs
        da = _nt(dy_ref[...], w_ref[...]) * scale
        if swiglu:
            dg_ref[...] = (da * g_ref[...].astype(F32)).astype(BF16)
            du_ref[...] = (da * u_ref[...].astype(F32)).astype(BF16)
        else:
            o_ref[...] = da
            ob_ref[...] = da.astype(BF16)

    tile = pl.BlockSpec((tm, tn), lambda i, j: (i, j))
    in_specs = [pl.BlockSpec((tm, D), lambda i, j: (i, 0)), pl.BlockSpec((tn, D), lambda i, j: (j, 0))]
    args = [dyb, w]
    if swiglu:
        in_specs += [tile, tile]
        args += [gate, up]
        out_shape = [jax.ShapeDtypeStruct((T, N), BF16)] * 2
    else:
        out_shape = [jax.ShapeDtypeStruct((T, N), F32), jax.ShapeDtypeStruct((T, N), BF16)]
    return pl.pallas_call(
        body, name=name, grid=(T // tm, N // tn), in_specs=in_specs, out_specs=[tile, tile],
        out_shape=out_shape, compiler_params=_params(2),
    )(*args)


def _nt_rms_bwd(name, As, Ws, x, g, dres):
    T, K = As[0].shape
    D = x.shape[1]
    na = len(As)
    tm = _pick(T, (ROW_TILE, 512, 256, 128))
    tk = _pick(K, (1024 // na, 512, 256, 128))
    nk = K // tk

    def body(*refs):
        a_refs = refs[:na]
        w_refs = refs[na:2 * na]
        x_hbm, g_ref, dres_hbm, acc, dxb_ref, dg_ref, x_ref, dres_ref, sems = refs[2 * na:]
        dx_ref = acc
        i, k = pl.program_id(0), pl.program_id(1)

        def row_copies():
            rows = pl.ds(pl.multiple_of(i * tm, tm), tm)
            return (pltpu.make_async_copy(x_hbm.at[rows, :], x_ref, sems.at[0]),
                    pltpu.make_async_copy(dres_hbm.at[rows, :], dres_ref, sems.at[1]))

        @pl.when(k == 0)
        def _():
            acc[...] = jnp.zeros_like(acc)
            for cp in row_copies():
                cp.start()

        for c0 in range(0, D, ACC_COLS):
            cols = slice(c0, min(c0 + ACC_COLS, D))
            part = _nt(a_refs[0][...], w_refs[0][cols, :])
            for a_ref, w_ref in zip(a_refs[1:], w_refs[1:]):
                part = part + _nt(a_ref[...], w_ref[cols, :])
            acc[:, cols] += part

        @pl.when(k == nk - 1)
        def _():
            for cp in row_copies():
                cp.wait()
            part = jnp.zeros((1, D), F32)
            for r0 in range(0, tm, EPILOGUE_ROWS):
                rows = slice(r0, r0 + min(EPILOGUE_ROWS, tm))
                dh = acc[rows, :]
                xv = x_ref[rows, :]
                r = _rms_r(xv)
                gd = dh * g_ref[...]
                dx = dres_ref[rows, :] + r * gd - xv * (r * r * r) * jnp.mean(gd * xv, axis=-1, keepdims=True)
                dx_ref[rows, :] = dx
                dxb_ref[rows, :] = dx.astype(BF16)
                part = part + jnp.sum(dh * xv * r, axis=0, keepdims=True)

            @pl.when(i == 0)
            def _():
                dg_ref[...] = part

            @pl.when(i > 0)
            def _():
                dg_ref[...] += part

    row = pl.BlockSpec((tm, D), lambda i, k: (i, 0), pipeline_mode=pl.Buffered(1))
    vec = pl.BlockSpec((1, D), lambda i, k: (0, 0))
    return pl.pallas_call(
        body, name=name, grid=(T // tm, nk),
        in_specs=[pl.BlockSpec((tm, tk), lambda i, k: (i, k))] * na
        + [pl.BlockSpec((D, tk), lambda i, k: (0, k))] * na + [ANY, vec, ANY],
        out_specs=[row, row, vec],
        out_shape=[jax.ShapeDtypeStruct((T, D), F32), jax.ShapeDtypeStruct((T, D), BF16),
                   jax.ShapeDtypeStruct((1, D), F32)],
        scratch_shapes=[pltpu.VMEM((tm, D), F32), pltpu.VMEM((tm, D), F32), pltpu.SemaphoreType.DMA((2,))],
        compiler_params=_params(2),
    )(*As, *Ws, x, g, dres)


def _tn_matmul(name, a, b, scale=1.0):
    T, M = a.shape
    N = b.shape[1]
    tn = _pick(N, (2048, 1408, 1280, 1024, 512, 256, 128))
    tm = _pick(M, tuple(c for c in (2048, 1408, 1024, 512, 256, 128) if c * tn <= TN_ACC_ELEMS))
    tk = _pick(T, (1024, 512, 256, 128))
    nk = T // tk

    def body(a_ref, b_ref, o_ref, acc):
        k = pl.program_id(2)

        @pl.when(k == 0)
        def _():
            acc[...] = jnp.zeros_like(acc)

        for r0 in range(0, tm, ACC_COLS):
            rows = slice(r0, min(r0 + ACC_COLS, tm))
            acc[rows, :] += _tn(a_ref[:, rows], b_ref[...])

        @pl.when(k == nk - 1)
        def _():
            o_ref[...] = (acc[...] * scale).astype(BF16)

    return pl.pallas_call(
        body, name=name, grid=(M // tm, N // tn, nk),
        in_specs=[pl.BlockSpec((tk, tm), lambda i, j, k: (k, i)), pl.BlockSpec((tk, tn), lambda i, j, k: (k, j))],
        out_specs=pl.BlockSpec((tm, tn), lambda i, j, k: (i, j)),
        out_shape=jax.ShapeDtypeStruct((M, N), BF16),
        scratch_shapes=[pltpu.VMEM((tm, tn), F32)], compiler_params=_params(3),
    )(a, b)


CONV_ROWS = 128
ROW_CHUNK = 32
LANE_CHUNK = 256


SUBLANES = 8


def _fill_shifts(sh, buf, rows):
    for p in range(1, SUBLANES):
        sh[p, 0:rows - SUBLANES, :] = buf[p:p + rows - SUBLANES, :]


def _tap(buf, sh, s, n, cols):
    p = s % SUBLANES
    return buf[s:s + n, cols] if p == 0 else sh[p, s - p:s - p + n, cols]


def _conv_fwd(name, z, w32, b, lg, lb, C):
    T = z.shape[0]
    tc = CONV_ROWS
    ntap = 31
    lc = _pick(C, (LANE_CHUNK, LANES))
    rpb = tc // HALO

    def body(zc_ref, zp_ref, w_ref, b_ref, lg_ref, lb_ref, yc_ref, ycv_ref, vbuf, ybuf, vsh):
        i = pl.program_id(0)
        zc = zc_ref[...]
        zp = zp_ref[...]
        vbuf[HALO:HALO + tc, :] = zc[:, :C] * _sigmoid(zc[:, C:])
        vbuf[0:HALO, :] = jnp.where(i > 0, zp[:, :C] * _sigmoid(zp[:, C:]), 0.0)
        _fill_shifts(vsh, vbuf, tc + HALO)
        for r0 in range(0, tc, ROW_CHUNK):
            for c0 in range(0, C, lc):
                cols = slice(c0, c0 + lc)
                acc = jnp.zeros((ROW_CHUNK, lc), F32) + b_ref[:, cols]
                for k in range(ntap):
                    acc = acc + w_ref[k:k + 1, cols] * _tap(vbuf, vsh, r0 + 2 + k, ROW_CHUNK, cols)
                ybuf[r0:r0 + ROW_CHUNK, cols] = acc
        y = ybuf[...]
        ycv_ref[...] = y
        mu = jnp.mean(y, axis=-1, keepdims=True)
        yc = y - mu
        rstd = lax.rsqrt(jnp.mean(yc * yc, axis=-1, keepdims=True) + EPS)
        ln = yc * rstd * lg_ref[...] + lb_ref[...]
        yc_ref[...] = (ln * _sigmoid(ln)).astype(BF16)

    vec = pl.BlockSpec((1, C), lambda i: (0, 0))
    return pl.pallas_call(
        body, name=name, grid=(T // tc,),
        in_specs=[pl.BlockSpec((tc, 2 * C), lambda i: (i, 0)),
                  pl.BlockSpec((HALO, 2 * C), lambda i: (jnp.maximum(i * rpb - 1, 0), 0)),
                  pl.BlockSpec((HALO, C), lambda i: (0, 0)), vec, vec, vec],
        out_specs=[pl.BlockSpec((tc, C), lambda i: (i, 0))] * 2,
        out_shape=[jax.ShapeDtypeStruct((T, 2 * C), BF16), jax.ShapeDtypeStruct((T, C), F32)],
        scratch_shapes=[pltpu.VMEM((tc + HALO, C), F32), pltpu.VMEM((tc, C), F32),
                        pltpu.VMEM((SUBLANES, tc + HALO, C), F32)],
        compiler_params=_params(1),
    )(z, z, w32, b, lg, lb)


def _conv_bwd(name, z, ycv, dycat, w32, lg, lb, C):
    T = z.shape[0]
    tc = CONV_ROWS
    ntap = 31
    lc = _pick(C, (LANE_CHUNK, LANES))
    rpb = tc // HALO
    nstep = T // tc
    nhb = T // HALO

    def ln_bwd(dyc, y, lgv, lbv):
        mu = jnp.mean(y, axis=-1, keepdims=True)
        yc = y - mu
        rstd = lax.rsqrt(jnp.mean(yc * yc, axis=-1, keepdims=True) + EPS)
        yn = yc * rstd
        ln = yn * lgv + lbv
        sg = _sigmoid(ln)
        dln = dyc * (sg * (1.0 + ln * (1.0 - sg)))
        dyn = dln * lgv
        dy = rstd * (dyn - jnp.mean(dyn, axis=-1, keepdims=True)
                     - yn * jnp.mean(dyn * yn, axis=-1, keepdims=True))
        return dy, dln, yn

    def body(zc_ref, zp_ref, y_ref, yn_ref, d_ref, dn_ref, w_ref, lg_ref, lb_ref,
             dz_ref, dw_ref, db_ref, dlg_ref, dlb_ref, vbuf, dbuf, dvbuf, dwacc, vsh, dsh):
        i = pl.program_id(0)
        lgv, lbv = lg_ref[...], lb_ref[...]
        zc = zc_ref[...]
        zp = zp_ref[...]
        a = zc[:, :C]
        sgt = _sigmoid(zc[:, C:])
        vbuf[HALO:HALO + tc, :] = a * sgt
        vbuf[0:HALO, :] = jnp.where(i > 0, zp[:, :C] * _sigmoid(zp[:, C:]), 0.0)
        dy, dln, yn = ln_bwd(d_ref[...], y_ref[...], lgv, lbv)
        dbuf[0:tc, :] = dy
        dyn_, _, _ = ln_bwd(dn_ref[...], yn_ref[...], lgv, lbv)
        dbuf[tc:tc + HALO, :] = jnp.where(i < nstep - 1, dyn_, 0.0)

        @pl.when(i == 0)
        def _():
            dwacc[...] = jnp.zeros_like(dwacc)
            db_ref[...] = jnp.zeros_like(db_ref)
            dlg_ref[...] = jnp.zeros_like(dlg_ref)
            dlb_ref[...] = jnp.zeros_like(dlb_ref)

        db_ref[...] += jnp.sum(dy, axis=0, keepdims=True)
        dlg_ref[...] += jnp.sum(dln * yn, axis=0, keepdims=True)
        dlb_ref[...] += jnp.sum(dln, axis=0, keepdims=True)

        _fill_shifts(vsh, vbuf, tc + HALO)
        _fill_shifts(dsh, dbuf, tc + HALO)
        for r0 in range(0, tc, ROW_CHUNK):
            for c0 in range(0, C, lc):
                cols = slice(c0, c0 + lc)
                dcur = dbuf[r0:r0 + ROW_CHUNK, cols]
                acc = jnp.zeros((ROW_CHUNK, lc), F32)
                for k in range(ntap):
                    acc = acc + w_ref[k:k + 1, cols] * _tap(dbuf, dsh, r0 + 30 - k, ROW_CHUNK, cols)
                    prod = dcur * _tap(vbuf, vsh, r0 + 2 + k, ROW_CHUNK, cols)
                    red = prod[0:8]
                    for q in range(8, ROW_CHUNK, 8):
                        red = red + prod[q:q + 8]
                    dwacc[8 * k:8 * k + 8, cols] += red
                dvbuf[r0:r0 + ROW_CHUNK, cols] = acc
        dv = dvbuf[...]
        dz_ref[:, :C] = (dv * sgt).astype(BF16)
        dz_ref[:, C:] = (dv * a * sgt * (1.0 - sgt)).astype(BF16)

        @pl.when(i == nstep - 1)
        def _():
            for k in range(ntap):
                dw_ref[k:k + 1, :] = jnp.sum(dwacc[8 * k:8 * k + 8, :], axis=0, keepdims=True)
            dw_ref[ntap:HALO, :] = jnp.zeros((HALO - ntap, C), F32)

    vec = pl.BlockSpec((1, C), lambda i: (0, 0))
    cur = pl.BlockSpec((tc, C), lambda i: (i, 0))
    nxt = pl.BlockSpec((HALO, C), lambda i: (jnp.minimum((i + 1) * rpb, nhb - 1), 0))
    return pl.pallas_call(
        body, name=name, grid=(nstep,),
        in_specs=[pl.BlockSpec((tc, 2 * C), lambda i: (i, 0)),
                  pl.BlockSpec((HALO, 2 * C), lambda i: (jnp.maximum(i * rpb - 1, 0), 0)),
                  cur, nxt, cur, nxt, pl.BlockSpec((HALO, C), lambda i: (0, 0)), vec, vec],
        out_specs=[pl.BlockSpec((tc, 2 * C), lambda i: (i, 0)), pl.BlockSpec((HALO, C), lambda i: (0, 0)),
                   vec, vec, vec],
        out_shape=[jax.ShapeDtypeStruct((T, 2 * C), BF16), jax.ShapeDtypeStruct((HALO, C), F32),
                   jax.ShapeDtypeStruct((1, C), F32), jax.ShapeDtypeStruct((1, C), F32),
                   jax.ShapeDtypeStruct((1, C), F32)],
        scratch_shapes=[pltpu.VMEM((tc + HALO, C), F32), pltpu.VMEM((tc + HALO, C), F32),
                        pltpu.VMEM((tc, C), F32), pltpu.VMEM((8 * HALO, C), F32),
                        pltpu.VMEM((SUBLANES, tc + HALO, C), F32), pltpu.VMEM((SUBLANES, tc + HALO, C), F32)],
        compiler_params=_params(1),
    )(z, z, ycv, ycv, dycat, dycat, w32, lg, lb)


def _seg_sum(u, bmat):
    hi = u.astype(BF16)
    lo = (u - hi.astype(F32)).astype(BF16)
    parts = [_nn(hi[:, c:c + LANES], bmat) + _nn(lo[:, c:c + LANES], bmat) for c in range(0, u.shape[1], LANES)]
    return jnp.concatenate(parts, axis=1)


def _attn_prep(name, z, gq, gk, bmat, A, c0, hd):
    T = z.shape[0]
    tm = _pick(T, (256, 128))

    def body(zq_ref, zk_ref, zv_ref, gq_ref, gk_ref, b_ref, o_ref):
        bm = b_ref[...]
        for idx, (z_ref, g_ref) in enumerate(((zq_ref, gq_ref), (zk_ref, gk_ref))):
            zv = z_ref[...]
            r = lax.rsqrt(_seg_sum(zv * zv, bm) * (1.0 / hd) + EPS)
            o_ref[:, idx * A:(idx + 1) * A] = zv * r * g_ref[...]
        o_ref[:, 2 * A:] = zv_ref[...]

    vec = pl.BlockSpec((1, A), lambda i: (0, 0))
    return pl.pallas_call(
        body, name=name, grid=(T // tm,),
        in_specs=[pl.BlockSpec((tm, A), lambda i: (i, c0)), pl.BlockSpec((tm, A), lambda i: (i, c0 + 1)),
                  pl.BlockSpec((tm, A), lambda i: (i, c0 + 2)), vec, vec,
                  pl.BlockSpec((LANES, LANES), lambda i: (0, 0))],
        out_specs=pl.BlockSpec((tm, 3 * A), lambda i: (i, 0)),
        out_shape=jax.ShapeDtypeStruct((T, 3 * A), F32), compiler_params=_params(1),
    )(z, z, z, gq, gk, bmat)


QK_SCALE = 0.125
ATTN_UNROLL = 4
ATTN_FWD_UNROLL = 8


def _fill_bias(bias, sl_ref, hp, d):
    qi = lax.broadcasted_iota(jnp.int32, (WINDOW, 2 * WINDOW), 0)
    kj = lax.broadcasted_iota(jnp.int32, (WINDOW, 2 * WINDOW), 1)
    dist = WINDOW + qi - kj
    inband = (dist >= 0) & (dist <= WINDOW)
    distf = dist.astype(F32)
    for hh in range(2):
        b = jnp.where(inband, -(sl_ref[2 * hp + hh] * d) * distf, NEG)
        bias[2 * hh + 1] = b
        bias[2 * hh] = jnp.where(kj >= WINDOW, b, NEG)


CHUNK = WINDOW * DILATIONS[-1]


def _deinterleave(dst, src, d, rows, dst_pitch, dst_off, src_off):
    for r in range(d):
        if d == 1:
            val = src[src_off:src_off + rows, :]
        else:
            val = src[pl.ds(src_off + r, rows, stride=d), :]
        lo = r * dst_pitch + dst_off
        dst[lo:lo + rows, :] = val.astype(dst.dtype)


def _interleave_add(dst, start, src, d, rows, src_pitch, src_off):
    for r in range(d):
        lo = r * src_pitch + src_off
        idx = pl.ds(start, rows) if d == 1 else pl.ds(start + r, rows, stride=d)
        dst[idx, :] += src[lo:lo + rows, :]


def _attn_fwd(name, qkv, slopes, ycat, A):
    T = qkv.shape[0]
    hpn, nch, nblk = A // LANES, T // CHUNK, CHUNK // WINDOW
    nbranch = len(DILATIONS)
    yoff = (ycat.shape[1] - A) // LANES

    def body(*refs):
        sl_ref, q_ref, k_ref, kp_ref, v_ref, vp_ref, _, y_ref, lg_ref, qd, kd, vd, od, ld, bias = refs[:15]
        onat, lnat = refs[15:15 + nbranch], refs[15 + nbranch:]
        hp, ch = pl.program_id(0), pl.program_id(1)
        lane = lax.broadcasted_iota(jnp.int32, (1, LANES), 1)
        first = lane < (LANES // 2)
        for bi, d in enumerate(DILATIONS):
            Ld = CHUNK // d
            seg = Ld + WINDOW
            nbr = Ld // WINDOW
            _deinterleave(qd, q_ref, d, Ld, Ld, 0, 0)
            for dst, cur, prev in ((kd, k_ref, kp_ref), (vd, v_ref, vp_ref)):
                _deinterleave(dst, prev, d, WINDOW, seg, 0, CHUNK - WINDOW * d)
                _deinterleave(dst, cur, d, Ld, seg, WINDOW, 0)
            _fill_bias(bias, sl_ref, hp, d)
            ob, lb = (onat[bi], lnat[bi]) if d == 1 else (od, ld)

            def step(it, carry, Ld=Ld, seg=seg, nbr=nbr, ob=ob, lb=lb):
                r, nl = it // nbr, it % nbr
                q0 = pl.multiple_of(r * Ld + nl * WINDOW, WINDOW)
                k0 = pl.multiple_of(r * seg + nl * WINDOW, WINDOW)
                later = jnp.where(ch * nbr + nl > 0, 1, 0)
                qb = qd[pl.ds(q0, WINDOW), :]
                k2 = kd[pl.ds(k0, 2 * WINDOW), :]
                v2 = vd[pl.ds(k0, 2 * WINDOW), :]
                res = []
                for hh in range(2):
                    mh = first if hh == 0 else jnp.logical_not(first)
                    s = _nt(jnp.where(mh, qb, jnp.zeros_like(qb)), k2) + bias[2 * hh + later]
                    mx = jnp.max(s, axis=-1, keepdims=True)
                    p = jnp.exp(s - mx)
                    den = jnp.sum(p, axis=-1, keepdims=True)
                    res.append((_nn(p.astype(BF16), v2) / den, mx + jnp.log(den)))
                ob[pl.ds(q0, WINDOW), :] = jnp.where(first, res[0][0], res[1][0])
                lb[pl.ds(q0, WINDOW), :] = jnp.where(first, res[0][1], res[1][1])
                return carry

            lax.fori_loop(0, nblk, step, 0, unroll=ATTN_FWD_UNROLL)
            if d > 1:
                for r in range(d):
                    onat[bi][pl.ds(r, Ld, stride=d), :] = od[r * Ld:(r + 1) * Ld, :]
                    lnat[bi][pl.ds(r, Ld, stride=d), :] = ld[r * Ld:(r + 1) * Ld, :]
        ls = [l[...] for l in lnat]
        mx = ls[0]
        for v in ls[1:]:
            mx = jnp.maximum(mx, v)
        es = [jnp.exp(v - mx) for v in ls]
        den = es[0]
        for e in es[1:]:
            den = den + e
        out = es[0] * onat[0][...]
        for e, o in zip(es[1:], onat[1:]):
            out = out + e * o[...]
        y_ref[...] = (out / den).astype(BF16)
        lg_ref[...] = mx + jnp.log(den)

    blk = lambda m: pl.BlockSpec((CHUNK, LANES), m)
    cur = lambda which: blk(lambda hp, ch: (ch, which * hpn + hp))
    prev = lambda which: blk(lambda hp, ch: (jnp.maximum(ch - 1, 0), which * hpn + hp))
    omap = blk(lambda hp, ch: (ch, hp))
    f32buf = pltpu.VMEM((CHUNK, LANES), F32)
    return pl.pallas_call(
        body, name=name, grid=(hpn, nch),
        in_specs=[pl.BlockSpec(memory_space=pltpu.SMEM), cur(0), cur(1), prev(1), cur(2), prev(2), ANY],
        out_specs=[blk(lambda hp, ch: (ch, yoff + hp)), omap],
        out_shape=[jax.ShapeDtypeStruct(ycat.shape, BF16), jax.ShapeDtypeStruct((T, A), F32)],
        input_output_aliases={6: 0},
        scratch_shapes=[pltpu.VMEM((CHUNK, LANES), BF16), pltpu.VMEM((2 * CHUNK, LANES), BF16),
                        pltpu.VMEM((2 * CHUNK, LANES), BF16), f32buf, f32buf,
                        pltpu.VMEM((4, WINDOW, 2 * WINDOW), F32)] + [f32buf] * (2 * nbranch),
        compiler_params=_params(2),
    )(slopes, qkv, qkv, qkv, qkv, qkv, ycat)


def _attn_bwd(name, qkv, dycat, ycat, lg, slopes, bmat, A, catoff):
    T = qkv.shape[0]
    hpn, nch, nblk = A // LANES, T // CHUNK, CHUNK // WINDOW
    co = catoff // LANES

    def body(sl_ref, q_ref, k_ref, kp_ref, v_ref, vp_ref, do_ref, o_ref, l_ref, b_ref, dq_ref, dk_ref, dv_ref,
             qd, kd, vd, dod, ddn, ddd, ldd, dqd, dkd, dvd, bias):
        hp, ch = pl.program_id(0), pl.program_id(1)
        lane = lax.broadcasted_iota(jnp.int32, (1, LANES), 1)
        first = lane < (LANES // 2)
        ddn[...] = _seg_sum(do_ref[...] * o_ref[...].astype(F32), b_ref[...])
        dq_ref[...] = jnp.zeros_like(dq_ref)

        @pl.when(ch == 0)
        def _():
            dk_ref[...] = jnp.zeros_like(dk_ref)
            dv_ref[...] = jnp.zeros_like(dv_ref)

        base = ch * CHUNK
        for d in DILATIONS:
            Ld = CHUNK // d
            seg = Ld + WINDOW
            nbr = Ld // WINDOW
            _deinterleave(qd, q_ref, d, Ld, Ld, 0, 0)
            _deinterleave(dod, do_ref, d, Ld, Ld, 0, 0)
            _deinterleave(ddd, ddn, d, Ld, Ld, 0, 0)
            _deinterleave(ldd, l_ref, d, Ld, Ld, 0, 0)
            for dst, cur, prev in ((kd, k_ref, kp_ref), (vd, v_ref, vp_ref)):
                _deinterleave(dst, prev, d, WINDOW, seg, 0, CHUNK - WINDOW * d)
                _deinterleave(dst, cur, d, Ld, seg, WINDOW, 0)
            dkd[0:d * seg, :] = jnp.zeros((d * seg, LANES), F32)
            dvd[0:d * seg, :] = jnp.zeros((d * seg, LANES), F32)
            _fill_bias(bias, sl_ref, hp, d)

            def step(it, carry, Ld=Ld, seg=seg, nbr=nbr):
                r, nl = it // nbr, it % nbr
                q0 = pl.multiple_of(r * Ld + nl * WINDOW, WINDOW)
                k0 = pl.multiple_of(r * seg + nl * WINDOW, WINDOW)
                later = jnp.where(ch * nbr + nl > 0, 1, 0)
                qb = qd[pl.ds(q0, WINDOW), :]
                k2 = kd[pl.ds(k0, 2 * WINDOW), :]
                v2 = vd[pl.ds(k0, 2 * WINDOW), :]
                dob = dod[pl.ds(q0, WINDOW), :]
                dd = ddd[pl.ds(q0, WINDOW), :]
                lb = ldd[pl.ds(q0, WINDOW), :]
                dk2 = jnp.zeros((2 * WINDOW, LANES), F32)
                dv2 = jnp.zeros((2 * WINDOW, LANES), F32)
                dqs = []
                for hh in range(2):
                    mh = first if hh == 0 else jnp.logical_not(first)
                    qh = jnp.where(mh, qb, jnp.zeros_like(qb))
                    doh = jnp.where(mh, dob, jnp.zeros_like(dob))
                    lcol = lb[:, hh * (LANES // 2):hh * (LANES // 2) + 1]
                    p = jnp.exp(_nt(qh, k2) + bias[2 * hh + later] - lcol)
                    dcol = dd[:, hh * (LANES // 2):hh * (LANES // 2) + 1]
                    ds = (p * (_nt(doh, v2) - dcol)).astype(BF16)
                    dqs.append(_nn(ds, k2))
                    dk2 = dk2 + _tn(ds, qh)
                    dv2 = dv2 + _tn(p.astype(BF16), doh)
                dqd[pl.ds(q0, WINDOW), :] = jnp.where(first, dqs[0], dqs[1])
                dkd[pl.ds(k0, 2 * WINDOW), :] += dk2
                dvd[pl.ds(k0, 2 * WINDOW), :] += dv2
                return carry

            lax.fori_loop(0, nblk, step, 0, unroll=ATTN_UNROLL)
            _interleave_add(dq_ref, 0, dqd, d, Ld, Ld, 0)
            for acc, out in ((dkd, dk_ref), (dvd, dv_ref)):
                _interleave_add(out, base, acc, d, Ld, seg, WINDOW)

                @pl.when(ch > 0)
                def _(acc=acc, out=out, d=d, seg=seg):
                    _interleave_add(out, base - WINDOW * d, acc, d, WINDOW, seg, 0)

    blk = lambda m: pl.BlockSpec((CHUNK, LANES), m)
    cur = lambda which: blk(lambda hp, ch: (ch, which * hpn + hp))
    prev = lambda which: blk(lambda hp, ch: (jnp.maximum(ch - 1, 0), which * hpn + hp))
    omap = blk(lambda hp, ch: (ch, hp))
    full = pl.BlockSpec((T, LANES), lambda hp, ch: (0, hp))
    f32buf = pltpu.VMEM((CHUNK, LANES), F32)
    bf16buf = pltpu.VMEM((CHUNK, LANES), BF16)
    return pl.pallas_call(
        body, name=name, grid=(hpn, nch),
        in_specs=[pl.BlockSpec(memory_space=pltpu.SMEM), cur(0), cur(1), prev(1), cur(2), prev(2),
                  blk(lambda hp, ch: (ch, co + hp)), blk(lambda hp, ch: (ch, co + hp)), omap,
                  pl.BlockSpec((LANES, LANES), lambda hp, ch: (0, 0))],
        out_specs=[omap, full, full],
        out_shape=[jax.ShapeDtypeStruct((T, A), F32)] * 3,
        scratch_shapes=[bf16buf, pltpu.VMEM((2 * CHUNK, LANES), BF16), pltpu.VMEM((2 * CHUNK, LANES), BF16),
                        bf16buf, f32buf, f32buf, f32buf, f32buf,
                        pltpu.VMEM((2 * CHUNK, LANES), F32), pltpu.VMEM((2 * CHUNK, LANES), F32),
                        pltpu.VMEM((4, WINDOW, 2 * WINDOW), F32)],
        compiler_params=_params(2),
    )(slopes, qkv, qkv, qkv, qkv, qkv, dycat, ycat, lg, bmat)


def _attn_bwd_combine(name, dzc, dqs, dks, dvs, z, gq, gk, bmat, fmat, A, c0, hd):
    T = z.shape[0]
    tm = _pick(T, (256, 128))
    nbr = len(dqs)
    W0 = dzc.shape[1]

    def body(*refs):
        dq_refs, dk_refs, dv_refs = refs[:nbr], refs[nbr:2 * nbr], refs[2 * nbr:3 * nbr]
        zq_ref, zk_ref, gq_ref, gk_ref, b_ref, f_ref, dzc_ref, dz_ref, dgq_ref, dgk_ref = refs[3 * nbr:]
        i = pl.program_id(0)
        bm = b_ref[...]
        dz_ref[:, :W0] = dzc_ref[...]

        def tot(rs):
            t = rs[0][...]
            for r in rs[1:]:
                t = t + r[...]
            return t

        for idx, (d_refs, z_ref, g_ref, dg_ref, gscale) in enumerate(
                ((dq_refs, zq_ref, gq_ref, dgq_ref, QK_SCALE), (dk_refs, zk_ref, gk_ref, dgk_ref, 1.0))):
            dy = tot(d_refs)
            zv = z_ref[...]
            r = lax.rsqrt(_seg_sum(zv * zv, bm) * (1.0 / hd) + EPS)
            gd = dy * g_ref[...]
            mean = _seg_sum(gd * zv, bm) * (1.0 / hd)
            dz_ref[:, W0 + idx * A:W0 + (idx + 1) * A] = (r * gd - zv * (r * r * r) * mean).astype(BF16)
            part = jnp.sum(dy * zv * r, axis=0, keepdims=True) * gscale

            @pl.when(i == 0)
            def _():
                dg_ref[...] = part

            @pl.when(i > 0)
            def _():
                dg_ref[...] += part

        dz_ref[:, W0 + 2 * A:] = tot(dv_refs).astype(BF16)

        @pl.when(i == T // tm - 1)
        def _():
            fm = f_ref[...]
            for dg_ref in (dgq_ref, dgk_ref):
                v = jnp.broadcast_to(dg_ref[...], (8, A))
                hi = v.astype(BF16)
                mid = (v - hi.astype(F32)).astype(BF16)
                lo = (v - hi.astype(F32) - mid.astype(F32)).astype(BF16)
                dg_ref[...] = (_nn(hi, fm) + _nn(mid, fm) + _nn(lo, fm))[0:1]

    blk = pl.BlockSpec((tm, A), lambda i: (i, 0))
    vec = pl.BlockSpec((1, A), lambda i: (0, 0))
    return pl.pallas_call(
        body, name=name, grid=(T // tm,),
        in_specs=[blk] * (3 * nbr) + [pl.BlockSpec((tm, A), lambda i: (i, c0)),
                                      pl.BlockSpec((tm, A), lambda i: (i, c0 + 1)), vec, vec,
                                      pl.BlockSpec((LANES, LANES), lambda i: (0, 0)),
                                      pl.BlockSpec((A, A), lambda i: (0, 0)),
                                      pl.BlockSpec((tm, W0), lambda i: (i, 0))],
        out_specs=[pl.BlockSpec((tm, W0 + 3 * A), lambda i: (i, 0)), vec, vec],
        out_shape=[jax.ShapeDtypeStruct((T, W0 + 3 * A), BF16), jax.ShapeDtypeStruct((1, A), F32),
                   jax.ShapeDtypeStruct((1, A), F32)],
        compiler_params=_params(1),
    )(*dqs, *dks, *dvs, z, z, gq, gk, bmat, fmat, dzc)


def _local_step(x, tgt, S, comm, hd):
    T, D = x.shape
    C = S["conv_b_dw"].shape[1]
    A = C
    H = A // hd
    Dmix = C + A
    c0 = (2 * C) // A
    slopes = 2.0 ** (-ALIBI_MAX_BIAS * jnp.arange(1, H + 1, dtype=F32) / H)
    seg = jnp.arange(LANES) // hd
    bmat = (seg[:, None] == seg[None, :]).astype(BF16)
    pos_in_head = jnp.arange(A) % hd
    fmat = (pos_in_head[:, None] == pos_in_head[None, :]).astype(BF16)
    gq = jnp.tile(S["q_norm_g"], (1, H)) * QK_SCALE
    gk = jnp.tile(S["k_norm_g"], (1, H))

    wg1, wu1 = comm.weights(("ffn1_w_gate", "ffn1_w_up"), None)
    h1, gate1, up1, a1 = _norm_matmul("ffn1_up", x, comm.tie(S["ffn1_norm_g"]), [wg1, wu1], True)
    wd1, win, w32 = comm.weights(("ffn1_w_down", "w_in", "conv_w32"), a1)
    x1 = _matmul_res("ffn1_down", a1, wd1, x, 0.5)
    h2, z = _norm_matmul("mix_in", x1, S["mix_norm_g"], [win], False)
    yc, ycv = _conv_fwd("conv_fwd", z, w32, S["conv_b_dw"], S["conv_ln_g"], S["conv_ln_b"], C)
    qkv = _attn_prep("attn_prep", z, gq, gk, bmat, A, c0, hd)
    ycat, lg = _attn_fwd("attn_fwd", qkv, slopes, yc, A)
    wout, wg2, wu2, wd2 = comm.weights(("w_out", "ffn2_w_gate", "ffn2_w_up", "ffn2_w_down"), lg)
    x2 = _matmul_res("mix_out", ycat, wout, x1, 1.0)
    h3, gate2, up2, a2 = _norm_matmul("ffn2_up", x2, S["ffn2_norm_g"], [wg2, wu2], True)
    dx3, dx3b, lossvec = _matmul_res("ffn2_down_loss", a2, wd2, x2, 0.5, tgt=tgt)

    G = {}
    dgate2, dup2 = _nt_matmul("ffn2_dact", dx3b, wd2, 0.5, gate2, up2)
    comm.reduce_begin("ffn2", {"ffn2_w_down": _tn_matmul("ffn2_dwd", a2, dx3b, 0.5),
                               "ffn2_w_gate": _tn_matmul("ffn2_dwg", h3, dgate2),
                               "ffn2_w_up": _tn_matmul("ffn2_dwu", h3, dup2)}, split=True)
    dx2, dx2b, G["ffn2_norm_g"] = _nt_rms_bwd("ffn2_dx", [dgate2, dup2], [wg2, wu2],
                                              x2, comm.tie(S["ffn2_norm_g"]), dx3)
    comm.reduce_scatter("ffn2", dx2b)
    dwout = _tn_matmul("mix_dwout", ycat, dx2b)
    dycat, _ = _nt_matmul("mix_dycat", dx2b, wout)
    dzc, G["conv_w32"], G["conv_b_dw"], G["conv_ln_g"], G["conv_ln_b"] = _conv_bwd(
        "conv_bwd", z, ycv, dycat, w32, S["conv_ln_g"], S["conv_ln_b"], C)
    dq, dk, dv = _attn_bwd("attn_bwd", qkv, dycat, ycat, lg, slopes, bmat, A, C)
    dz, G["q_norm_g"], G["k_norm_g"] = _attn_bwd_combine(
        "attn_bwd_combine", dzc, [dq], [dk], [dv], z, gq, gk, bmat, fmat, A, c0, hd)
    comm.reduce_end("ffn2", dz)
    comm.reduce_begin("mix", {"w_out": dwout, "w_in": _tn_matmul("mix_dwin", h2, dz)}, split=True)
    dx1, dx1b, G["mix_norm_g"] = _nt_rms_bwd("mix_dx", [dz], [win], x1, comm.tie(S["mix_norm_g"]), dx2)
    comm.reduce_scatter("mix", dx1b)
    dgate1, dup1 = _nt_matmul("ffn1_dact", dx1b, wd1, 0.5, gate1, up1)
    dwd1 = _tn_matmul("ffn1_dwd", a1, dx1b, 0.5)
    dwg1 = _tn_matmul("ffn1_dwg", h1, dgate1)
    dwu1 = _tn_matmul("ffn1_dwu", h1, dup1)
    comm.reduce_end("mix", dwu1)
    comm.reduce_begin("ffn1", {"ffn1_w_down": dwd1, "ffn1_w_gate": dwg1, "ffn1_w_up": dwu1})
    dx0, _, G["ffn1_norm_g"] = _nt_rms_bwd("ffn1_dx", [dgate1, dup1], [wg1, wu1],
                                           x, comm.tie(S["ffn1_norm_g"]), dx1)
    comm.reduce_end("ffn1", dx0)
    return lossvec, dx0, G


BIG = (("ffn1_w_gate", 1), ("ffn1_w_up", 1), ("ffn1_w_down", 0), ("w_in", 1), ("w_out", 0),
       ("ffn2_w_gate", 1), ("ffn2_w_up", 1), ("ffn2_w_down", 0))
AXIS = dict(BIG)
FLIPS = ((1, 0), (0, 1), (1, 1))
HBM = pl.BlockSpec(memory_space=pltpu.HBM)
SEM = pl.BlockSpec(memory_space=pltpu.SEMAPHORE)
EFFECT = pltpu.SideEffectType.DATAFLOW_SIDE_EFFECTING
TOKEN = jax.ShapeDtypeStruct((8, LANES), F32)


def _window(ref, shape, axis, slab=None, half=None):
    idx = [pl.ds(0, shape[0]), pl.ds(0, shape[1])]
    if slab is not None:
        n = shape[axis] // 4
        idx[axis] = pl.ds(pl.multiple_of(slab * n, 8), n)
    if half is not None:
        hs = shape[1 - axis] // 2
        idx[1 - axis] = pl.ds(pl.multiple_of(half * hs, 8), hs)
    return ref.at[idx[0], idx[1]]


def _position():
    return lax.axis_index("x"), lax.axis_index("y"), lax.axis_index("c")


def _half_shape(shape, axis):
    return (shape[0] // 2, shape[1]) if axis == 1 else (shape[0], shape[1] // 2)


def _slab_shape(shape, axis):
    return (shape[0], shape[1] // 4) if axis == 1 else (shape[0] // 4, shape[1])


def _piece_shape(shape, axis):
    return _half_shape(_slab_shape(shape, axis), axis)


def _full_shape(shard, axis):
    return (shard.shape[0], shard.shape[1] * 4) if axis == 1 else (shard.shape[0] * 4, shard.shape[1])


def _hbm(a):
    return pltpu.with_memory_space_constraint(a, pltpu.HBM)


def _remote(src, dst, send_sem, recv_sem, to):
    return pltpu.make_async_remote_copy(src_ref=src, dst_ref=dst, send_sem=send_sem, recv_sem=recv_sem,
                                        device_id=to, device_id_type=MESH)


def _place(name, pos, w, axis):
    R, Cc = w.shape
    tr = _pick(R, (256, 128, 64, 32, 16))
    nrb = R // tr

    def body(pos_ref, w_ref, o_ref):
        o_ref[...] = w_ref[...].astype(BF16)

    omap = (lambda i, p: (i, p[0])) if axis == 1 else (lambda i, p: (p[0] * nrb + i, 0))
    return pl.pallas_call(
        body, name=name,
        grid_spec=pltpu.PrefetchScalarGridSpec(
            num_scalar_prefetch=1, grid=(nrb,), in_specs=[pl.BlockSpec((tr, Cc), lambda i, p: (i, 0))],
            out_specs=pl.BlockSpec((tr, Cc), omap)),
        out_shape=jax.ShapeDtypeStruct(_full_shape(w, axis), BF16), compiler_params=_params(1),
    )(pos, w)


def _gather_now(name, axes, fulls):
    nt = len(fulls)
    shapes = [f.shape for f in fulls]

    def body(*refs):
        outs, token = refs[nt:2 * nt], refs[2 * nt]
        send_sems, recv_sems = refs[2 * nt + 1:]
        x, y, c = _position()
        j0 = 2 * x + y
        sib = (x, y, 1 - c)

        def copy(t, k, slab, half, to):
            win = _window(outs[t], shapes[t], axes[t], slab=slab, half=half)
            return _remote(win, win, send_sems.at[t, k], recv_sems.at[t, k], to)

        sends = []
        for k, (fx, fy) in enumerate(FLIPS):
            for t in range(nt):
                cp = copy(t, k, j0, c, (x ^ fx, y ^ fy, c))
                cp.start()
                sends.append(cp)
        for k, (fx, fy) in enumerate(FLIPS):
            js = 2 * (x ^ fx) + (y ^ fy)
            for t in range(nt):
                copy(t, k, js, c, sib).wait_recv()
                cp = copy(t, 3 + k, js, c, sib)
                cp.start()
                sends.append(cp)
        for k, (fx, fy) in enumerate(FLIPS):
            js = 2 * (x ^ fx) + (y ^ fy)
            for t in range(nt):
                copy(t, 3 + k, js, 1 - c, sib).wait_recv()
        for cp in sends:
            cp.wait_send()
        token[...] = jnp.zeros_like(token)

    res = pl.pallas_call(
        body, name=name, in_specs=[ANY] * nt,
        out_specs=[ANY] * nt + [pl.BlockSpec(memory_space=pltpu.VMEM)],
        out_shape=[jax.ShapeDtypeStruct(s, BF16) for s in shapes] + [TOKEN],
        input_output_aliases={t: t for t in range(nt)},
        scratch_shapes=[pltpu.SemaphoreType.DMA((nt, 6)), pltpu.SemaphoreType.DMA((nt, 6))],
    )(*fulls)
    return list(res[:nt]), res[nt]


def _split_start(name, arrays, ncopies, plan):
    na = len(arrays)

    def body(*refs):
        ins = refs[:na]
        send_sems, recv_sems = refs[na], refs[na + 1]
        token = refs[-1]
        x, y, c = _position()
        for i, (src, dst, to) in enumerate(plan(ins, x, y, c)):
            _remote(src, dst, send_sems.at[i], recv_sems.at[i], to).start()
        token[...] = jnp.zeros_like(token)

    res = pl.pallas_call(
        body, name=name, in_specs=[HBM] * na,
        out_specs=tuple([SEM, SEM] + [HBM] * na + [pl.BlockSpec(memory_space=pltpu.VMEM)]),
        out_shape=tuple([pltpu.SemaphoreType.DMA((ncopies,)), pltpu.SemaphoreType.DMA((ncopies,))]
                        + [pltpu.HBM(a.shape, a.dtype) for a in arrays] + [TOKEN]),
        input_output_aliases={i: 2 + i for i in range(na)},
        compiler_params=pltpu.CompilerParams(has_side_effects=EFFECT),
    )(*[_hbm(a) for a in arrays])
    return (res[0], res[1]), list(res[2:2 + na]), res[-1]


def _split_wait(name, arrays, sems, after, plan):
    na = len(arrays)

    def body(*refs):
        ins = refs[:na]
        send_sems, recv_sems = refs[na], refs[na + 1]
        x, y, c = _position()
        for i, (src, dst, to) in enumerate(plan(ins, x, y, c)):
            cp = _remote(src, dst, send_sems.at[i], recv_sems.at[i], to)
            cp.wait_send()
            cp.wait_recv()

    res = pl.pallas_call(
        body, name=name, in_specs=[HBM] * na + [SEM, SEM, ANY],
        out_specs=tuple([HBM] * na), out_shape=tuple(pltpu.HBM(a.shape, a.dtype) for a in arrays),
        input_output_aliases={i: i for i in range(na)},
        compiler_params=pltpu.CompilerParams(has_side_effects=EFFECT),
    )(*arrays, *sems, after)
    return list(res)


def _gather_plan(axes, shapes, conv_shape):
    nt = len(axes)

    def plan(refs, x, y, c):
        j0 = 2 * x + y
        out = []
        for fx, fy in FLIPS:
            to = (x ^ fx, y ^ fy, c)
            for t in range(nt):
                win = _window(refs[t], shapes[t], axes[t], slab=j0, half=c)
                out.append((win, win, to))
            if conv_shape is not None:
                win = _window(refs[nt], conv_shape, 1, slab=j0)
                out.append((win, win, to))
        return out

    return plan


def _gather_finish(name, axes, fulls):
    nt = len(axes)
    shapes = [f.shape for f in fulls]

    def body(*refs):
        outs = refs[nt:2 * nt]
        send_sems, recv_sems = refs[2 * nt:]
        x, y, c = _position()
        sib = (x, y, 1 - c)
        cps = []
        for k, (fx, fy) in enumerate(FLIPS):
            js = 2 * (x ^ fx) + (y ^ fy)
            for t in range(nt):
                landed = _window(outs[t], shapes[t], axes[t], slab=js, half=c)
                cp = _remote(landed, landed, send_sems.at[t, k], recv_sems.at[t, k], sib)
                cp.start()
                cps.append(cp)
        for k, (fx, fy) in enumerate(FLIPS):
            js = 2 * (x ^ fx) + (y ^ fy)
            for t in range(nt):
                other = _window(outs[t], shapes[t], axes[t], slab=js, half=1 - c)
                _remote(other, other, send_sems.at[t, k], recv_sems.at[t, k], sib).wait_recv()
        for cp in cps:
            cp.wait_send()

    res = pl.pallas_call(
        body, name=name, in_specs=[ANY] * nt, out_specs=[ANY] * nt,
        out_shape=[jax.ShapeDtypeStruct(f.shape, f.dtype) for f in fulls],
        input_output_aliases={t: t for t in range(nt)},
        scratch_shapes=[pltpu.SemaphoreType.DMA((nt, 3)), pltpu.SemaphoreType.DMA((nt, 3))],
    )(*fulls)
    return list(res)


def _pair_exchange(name, srcs, windows, out_shapes, dtype):
    nt = len(srcs)

    def body(*refs):
        ins, outs = refs[:nt], refs[nt:2 * nt]
        send_sems, recv_sems = refs[2 * nt:]
        x, y, c = _position()
        cps = []
        for t in range(nt):
            cp = _remote(windows[t](ins[t], c), outs[t], send_sems.at[t], recv_sems.at[t], (x, y, 1 - c))
            cp.start()
            cps.append(cp)
        for cp in cps:
            cp.wait()

    return pl.pallas_call(
        body, name=name, in_specs=[ANY] * nt, out_specs=[ANY] * nt,
        out_shape=[jax.ShapeDtypeStruct(s, dtype) for s in out_shapes],
        scratch_shapes=[pltpu.SemaphoreType.DMA((nt,)), pltpu.SemaphoreType.DMA((nt,))],
    )(*srcs)


def _pair_plan(axes, shapes):
    nt = len(axes)

    def plan(refs, x, y, c):
        return [(_window(refs[t], shapes[t], axes[t], half=1 - c), refs[nt + t], (x, y, 1 - c)) for t in range(nt)]

    return plan


def _scatter_plan(axes, shapes):
    nt = len(axes)

    def plan(refs, x, y, c):
        out = []
        for k, (fx, fy) in enumerate(FLIPS):
            js = 2 * (x ^ fx) + (y ^ fy)
            for t in range(nt):
                src = _window(refs[t], _half_shape(shapes[t], axes[t]), axes[t], slab=js)
                out.append((src, refs[nt + t].at[k], (x ^ fx, y ^ fy, c)))
        return out

    return plan


def _gather_small(packed):
    R, Cc = packed.shape

    def body(p_ref, o_ref, send_sems, recv_sems, loc_sem):
        x, y, c = _position()
        me = 4 * x + 2 * y + c
        mine = pltpu.make_async_copy(p_ref, o_ref.at[me], loc_sem)
        mine.start()
        cps = []
        for k in range(1, 8):
            fx, fy, fc = (k >> 2) & 1, (k >> 1) & 1, k & 1
            cp = pltpu.make_async_remote_copy(
                src_ref=p_ref, dst_ref=o_ref.at[me], send_sem=send_sems.at[k - 1], recv_sem=recv_sems.at[k - 1],
                device_id=(x ^ fx, y ^ fy, c ^ fc), device_id_type=MESH)
            cp.start()
            cps.append(cp)
        for cp in cps:
            cp.wait()
        mine.wait()

    return pl.pallas_call(
        body, name="gather_small_grads", in_specs=[ANY], out_specs=ANY,
        out_shape=jax.ShapeDtypeStruct((8, R, Cc), F32),
        scratch_shapes=[pltpu.SemaphoreType.DMA((7,)), pltpu.SemaphoreType.DMA((7,)), pltpu.SemaphoreType.DMA],
    )(packed)


def _sum_slots(name, slots):
    n, R, Cc = slots.shape

    def body(s_ref, o_ref):
        t = s_ref[0]
        for i in range(1, n):
            t = t + s_ref[i]
        o_ref[...] = t

    return pl.pallas_call(
        body, name=name, grid=(1,), in_specs=[pl.BlockSpec((n, R, Cc), lambda i: (0, 0, 0))],
        out_specs=pl.BlockSpec((R, Cc), lambda i: (0, 0)), out_shape=jax.ShapeDtypeStruct((R, Cc), F32),
        compiler_params=_params(1),
    )(slots)


def _pair_sum(name, pos, g, land, shape, axis):
    hshape = _half_shape(shape, axis)
    R, Cc = hshape
    tr = _pick(R, (256, 128, 64, 32, 16))
    nrb = R // tr

    def body(pos_ref, g_ref, l_ref, o_ref):
        o_ref[...] = (g_ref[...].astype(F32) + l_ref[...].astype(F32)).astype(BF16)

    if axis == 1:
        gmap = lambda i, p: (p[1] * nrb + i, 0)
    else:
        gmap = lambda i, p: (i, p[1])
    blk = pl.BlockSpec((tr, Cc), lambda i, p: (i, 0))
    return pl.pallas_call(
        body, name=name,
        grid_spec=pltpu.PrefetchScalarGridSpec(
            num_scalar_prefetch=1, grid=(nrb,), in_specs=[pl.BlockSpec((tr, Cc), gmap), blk], out_specs=blk),
        out_shape=jax.ShapeDtypeStruct(hshape, BF16), compiler_params=_params(1),
    )(pos, g, land)


def _chip_sum(name, pos, sb, land, shape, axis):
    hshape = _half_shape(shape, axis)
    pshape = _piece_shape(shape, axis)
    R, Cc = pshape
    tr = _pick(R, (256, 128, 64, 32, 16))
    nrb = R // tr

    def body(pos_ref, s_ref, l_ref, o_ref):
        t = s_ref[...].astype(F32)
        for k in range(3):
            t = t + l_ref[k].astype(F32)
        o_ref[...] = t

    if axis == 1:
        smap = lambda i, p: (i, p[0])
    else:
        smap = lambda i, p: (p[0] * nrb + i, 0)
    return pl.pallas_call(
        body, name=name,
        grid_spec=pltpu.PrefetchScalarGridSpec(
            num_scalar_prefetch=1, grid=(nrb,),
            in_specs=[pl.BlockSpec((tr, Cc), smap), pl.BlockSpec((3, tr, Cc), lambda i, p: (0, i, 0))],
            out_specs=pl.BlockSpec((tr, Cc), lambda i, p: (i, 0))),
        out_shape=jax.ShapeDtypeStruct(pshape, F32), compiler_params=_params(1),
    )(pos, sb, land)


def _adam_math(w, g, m, v):
    m = ADAM_B1 * m + (1.0 - ADAM_B1) * g
    v = ADAM_B2 * v + (1.0 - ADAM_B2) * (g * g)
    m_hat = m / (1.0 - ADAM_B1 ** ADAM_STEP)
    v_hat = v / (1.0 - ADAM_B2 ** ADAM_STEP)
    delta = -ADAM_LR * (m_hat / (jnp.sqrt(v_hat) + ADAM_EPS) + ADAM_WD * w)
    return delta, m, v


def _adamw_halves(name, pos, w, m, v, mine, theirs, axis):
    R, Cc = w.shape
    hr, hc = mine.shape
    tr = _pick(hr, (256, 128, 64, 32, 16))
    nrb = hr // tr

    def body(pos_ref, w_ref, m_ref, v_ref, a_ref, b_ref, g_ref, d_ref, nm_ref, nv_ref):
        half = pl.program_id(0)
        g = jnp.where(half == pos_ref[1], a_ref[...], b_ref[...])
        d, nm, nv = _adam_math(w_ref[...], g, m_ref[...], v_ref[...])
        g_ref[...] = g
        d_ref[...] = d
        nm_ref[...] = nm
        nv_ref[...] = nv

    if axis == 1:
        wmap = lambda h, i, p: (h * nrb + i, 0)
    else:
        wmap = lambda h, i, p: (i, h)
    wblk = pl.BlockSpec((tr, hc), wmap)
    ablk = pl.BlockSpec((tr, hc), lambda h, i, p: (jnp.where(h == p[1], i, 0), 0))
    bblk = pl.BlockSpec((tr, hc), lambda h, i, p: (jnp.where(h == p[1], 0, i), 0))
    return pl.pallas_call(
        body, name=name,
        grid_spec=pltpu.PrefetchScalarGridSpec(
            num_scalar_prefetch=1, grid=(2, nrb), in_specs=[wblk, wblk, wblk, ablk, bblk], out_specs=[wblk] * 4),
        out_shape=[jax.ShapeDtypeStruct((R, Cc), F32)] * 4, compiler_params=_params(2),
    )(pos, w, m, v, mine, theirs)


def _adamw_small(name, w, g, m, v):
    def body(w_ref, g_ref, m_ref, v_ref, d_ref, nm_ref, nv_ref):
        d, nm, nv = _adam_math(w_ref[...], g_ref[...], m_ref[...], v_ref[...])
        d_ref[...] = d
        nm_ref[...] = nm
        nv_ref[...] = nv

    blk = pl.BlockSpec(w.shape, lambda i: (0, 0))
    return pl.pallas_call(
        body, name=name, grid=(1,), in_specs=[blk] * 4, out_specs=[blk] * 3,
        out_shape=[jax.ShapeDtypeStruct(w.shape, F32)] * 3, compiler_params=_params(1),
    )(w, g, m, v)


SMALL = ("ffn1_norm_g", "mix_norm_g", "conv_b_dw", "conv_ln_g", "conv_ln_b", "q_norm_g", "k_norm_g", "ffn2_norm_g")
ORDER = ("ffn1_norm_g", "ffn1_w_gate", "ffn1_w_up", "ffn1_w_down", "mix_norm_g", "w_in", "conv_w_dw", "conv_b_dw",
         "conv_ln_g", "conv_ln_b", "q_norm_g", "k_norm_g", "w_out", "ffn2_norm_g", "ffn2_w_gate", "ffn2_w_up",
         "ffn2_w_down")
GATHER_FIRST = ("ffn1_w_gate", "ffn1_w_up")
GATHER_SECOND = ("ffn1_w_down", "w_in")
GATHER_THIRD = ("w_out", "ffn2_w_gate", "ffn2_w_up", "ffn2_w_down")


class _Exchange:
    def __init__(self, P, Mo, Vo, conv_shard, pos):
        self.P, self.Mo, self.Vo, self.pos = P, Mo, Vo, pos
        self.tokens = []
        self.pending = {}
        self.reducing = {}
        self.results = {}
        placed = {n: _place("place_" + n, pos, P[n][0], a) for n, a in BIG}
        self.shapes = {n: placed[n].shape for n, _ in BIG}
        first, tok = _gather_now("gather_first", [AXIS[n] for n in GATHER_FIRST], [placed[n] for n in GATHER_FIRST])
        self.ready = dict(zip(GATHER_FIRST, first))
        cq = conv_shard.shape[1]
        conv_full = lax.dynamic_update_slice(jnp.zeros((conv_shard.shape[0], 4 * cq), F32), conv_shard,
                                             (0, pos[0] * cq))
        for gname, names, conv in (("second", GATHER_SECOND, conv_full), ("third", GATHER_THIRD, None)):
            axes = [AXIS[n] for n in names]
            shapes = [self.shapes[n] for n in names]
            arrays = [placed[n] for n in names] + ([conv] if conv is not None else [])
            small = min(range(len(arrays)), key=lambda i: arrays[i].size)
            arrays[small] = arrays[small] + tok[0, 0].astype(arrays[small].dtype)
            plan = _gather_plan(axes, shapes, conv.shape if conv is not None else None)
            sems, thru, tok = _split_start("gather_%s_start" % gname, arrays, 3 * len(arrays), plan)
            self.tokens.append(tok)
            for n in names + (("conv_w32",) if conv is not None else ()):
                self.pending[n] = (gname, names, axes, plan, sems, thru, conv is not None)

    def tie(self, v):
        for tok in self.tokens:
            v = v + tok[0:1, 0:1]
        self.tokens = []
        return v

    def weights(self, names, after):
        if names[0] in self.pending:
            gname, gnames, axes, plan, sems, thru, has_conv = self.pending[names[0]]
            thru = _split_wait("gather_%s_wait" % gname, thru, sems, after, plan)
            nt = len(gnames)
            fulls = _gather_finish("gather_%s_finish" % gname, axes, thru[:nt])
            for n, f in zip(gnames, fulls):
                self.ready[n] = f
                del self.pending[n]
            if has_conv:
                self.ready["conv_w32"] = thru[nt]
                del self.pending["conv_w32"]
        return [self.ready[n] for n in names]

    def reduce_begin(self, gname, grads, split=False):
        names = list(grads)
        axes = [AXIS[n] for n in names]
        shapes = [self.shapes[n] for n in names]
        gs = [grads[n] for n in names]
        nt = len(names)
        if not split:
            to_sibling = [(lambda ref, c, s=s, a=a: _window(ref, s, a, half=1 - c)) for s, a in zip(shapes, axes)]
            landed = _pair_exchange("pair_exchange_" + gname, gs, to_sibling,
                                    [_half_shape(s, a) for s, a in zip(shapes, axes)], BF16)
            self._scatter(gname, names, axes, shapes, gs, landed)
            return
        plan = _pair_plan(axes, shapes)
        arrays = gs + [lax.empty(_half_shape(s, a), BF16) for s, a in zip(shapes, axes)]
        sems, thru, tok = _split_start("pair_%s_start" % gname, arrays, nt, plan)
        self.tokens.append(tok)
        self.reducing[gname] = (names, axes, shapes, plan, sems, thru)

    def reduce_scatter(self, gname, after):
        names, axes, shapes, plan, sems, thru = self.reducing.pop(gname)
        nt = len(names)
        thru = _split_wait("pair_%s_wait" % gname, thru, sems, after, plan)
        self._scatter(gname, names, axes, shapes, thru[:nt], thru[nt:])

    def _scatter(self, gname, names, axes, shapes, gs, landed):
        sbs = [_pair_sum("pair_sum_" + n, self.pos, g, l, s, a)
               for n, a, g, l, s in zip(names, axes, gs, landed, shapes)]
        lands = [lax.empty((3,) + _piece_shape(s, a), BF16) for s, a in zip(shapes, axes)]
        plan = _scatter_plan(axes, shapes)
        sems, thru, tok = _split_start("scatter_%s_start" % gname, sbs + lands, 3 * len(names), plan)
        self.tokens.append(tok)
        self.reducing[gname] = (names, axes, shapes, plan, sems, thru)

    def reduce_end(self, gname, after):
        names, axes, shapes, plan, sems, thru = self.reducing.pop(gname)
        nt = len(names)
        thru = _split_wait("scatter_%s_wait" % gname, thru, sems, after, plan)
        mine = [_chip_sum("chip_sum_" + n, self.pos, sb, l, s, a)
                for n, a, sb, l, s in zip(names, axes, thru[:nt], thru[nt:], shapes)]
        theirs = _pair_exchange("half_exchange_" + gname, mine, [(lambda ref, c: ref)] * nt,
                                [m.shape for m in mine], F32)
        for n, a, mi, th in zip(names, axes, mine, theirs):
            g, d, nm, nv = _adamw_halves("adamw_" + n, self.pos, self.P[n][0], self.Mo[n][0], self.Vo[n][0],
                                         mi, th, a)
            self.results[n] = (g[None], d[None], nm[None], nv[None])


def kernel(x, ffn1_norm_g, ffn1_w_gate, ffn1_w_up, ffn1_w_down, mix_norm_g, w_in, conv_w_dw, conv_b_dw, conv_ln_g, conv_ln_b, q_norm_g, k_norm_g, w_out, ffn2_norm_g, ffn2_w_gate, ffn2_w_up, ffn2_w_down, loss_target, m_ffn1_norm_g, m_ffn1_w_gate, m_ffn1_w_up, m_ffn1_w_down, m_mix_norm_g, m_w_in, m_conv_w_dw, m_conv_b_dw, m_conv_ln_g, m_conv_ln_b, m_q_norm_g, m_k_norm_g, m_w_out, m_ffn2_norm_g, m_ffn2_w_gate, m_ffn2_w_up, m_ffn2_w_down, v_ffn1_norm_g, v_ffn1_w_gate, v_ffn1_w_up, v_ffn1_w_down, v_mix_norm_g, v_w_in, v_conv_w_dw, v_conv_b_dw, v_conv_ln_g, v_conv_ln_b, v_q_norm_g, v_k_norm_g, v_w_out, v_ffn2_norm_g, v_ffn2_w_gate, v_ffn2_w_up, v_ffn2_w_down):
    args = dict(locals())
    P = {n: args[n] for n in ORDER}
    Mo = {n: args["m_" + n] for n in ORDER}
    Vo = {n: args["v_" + n] for n in ORDER}
    xs = x[0]
    tgt = loss_target[0]
    T, D = xs.shape
    hd = q_norm_g.shape[-1]
    C = conv_b_dw.shape[-1]
    ntap = conv_w_dw.shape[1]
    cx, cy, cc = _position()
    j0 = 2 * cx + cy
    pos = jnp.stack([j0, cc]).astype(jnp.int32)

    conv_shard = jnp.pad(conv_w_dw[0], ((0, HALO - ntap), (0, 0)))
    comm = _Exchange(P, Mo, Vo, conv_shard, pos)
    lossvec, dx0, G = _local_step(xs, tgt, {n: P[n] for n in SMALL}, comm, hd)
    loss = lax.psum(0.5 / D * jnp.sum(lossvec), AXES)
    grads, deltas, new_m, new_v = {}, {}, {}, {}
    for n, _ in BIG:
        grads[n], deltas[n], new_m[n], new_v[n] = comm.results[n]

    rows = [G["conv_w32"]]
    for n in ("ffn1_norm_g", "mix_norm_g", "ffn2_norm_g"):
        rows.append(G[n].reshape(D // C, C))
    for n in ("conv_b_dw", "conv_ln_g", "conv_ln_b", "q_norm_g", "k_norm_g"):
        rows.append(G[n])
    packed = jnp.concatenate(rows, axis=0)
    packed = jnp.pad(packed, ((0, -packed.shape[0] % 8), (0, 0)))
    total = _sum_slots("sum_small_grads", _gather_small(packed))
    r = HALO
    small_g = {}
    cq = C // 4
    small_g["conv_w_dw"] = lax.dynamic_slice(total[:ntap], (0, j0 * cq), (ntap, cq))
    for n in ("ffn1_norm_g", "mix_norm_g", "ffn2_norm_g"):
        small_g[n] = total[r:r + D // C].reshape(1, D)
        r += D // C
    for n in ("conv_b_dw", "conv_ln_g", "conv_ln_b"):
        small_g[n] = total[r:r + 1]
        r += 1
    for n in ("q_norm_g", "k_norm_g"):
        small_g[n] = total[r:r + 1, :hd]
        r += 1
    for n in ("conv_w_dw",) + SMALL:
        lead = n == "conv_w_dw"
        w2, m2, v2 = (P[n][0], Mo[n][0], Vo[n][0]) if lead else (P[n], Mo[n], Vo[n])
        d, nm, nv = _adamw_small("adamw_" + n, w2, small_g[n], m2, v2)
        if lead:
            grads[n], deltas[n], new_m[n], new_v[n] = small_g[n][None], d[None], nm[None], nv[None]
        else:
            grads[n], deltas[n], new_m[n], new_v[n] = small_g[n], d, nm, nv

    return (loss, dx0[None], *[grads[n] for n in ORDER], *[deltas[n] for n in ORDER],
            *[new_m[n] for n in ORDER], *[new_v[n] for n in ORDER])
```

```python
import jax
import jax.numpy as jnp
from jax import lax
from jax.experimental import pallas as pl
from jax.experimental.pallas import tpu as pltpu

F32 = jnp.float32
BF16 = jnp.bfloat16
EPS = 1e-6
WINDOW = 128
DILATIONS = (1, 4, 16)
ALIBI_MAX_BIAS = 8.0
LANES = 128
HALO = 32
ADAM_LR, ADAM_B1, ADAM_B2, ADAM_EPS, ADAM_WD, ADAM_STEP = 0.001, 0.9, 0.999, 1e-08, 0.01, 10
VMEM_LIMIT_MB = 62
ROW_TILE = 1024
TN_ACC_ELEMS = 3 * 1024 * 1024
EPILOGUE_ROWS = 256
ACC_COLS = 512
MESH = pl.DeviceIdType.MESH
ANY = pl.BlockSpec(memory_space=pl.ANY)
AXES = ("x", "y", "c")
NEG = -1e30


def _pick(n, cands):
    for c in cands:
        if n % c == 0:
            return c
    return n


def _params(nsem):
    return pltpu.CompilerParams(dimension_semantics=("arbitrary",) * nsem,
                                vmem_limit_bytes=VMEM_LIMIT_MB << 20)


def _nn(a, b):
    return jnp.dot(a, b, preferred_element_type=F32)


def _nt(a, b):
    return lax.dot_general(a, b, (((1,), (1,)), ((), ())), preferred_element_type=F32)


def _tn(a, b):
    return lax.dot_general(a, b, (((0,), (0,)), ((), ())), preferred_element_type=F32)


def _sigmoid(v):
    return jax.nn.sigmoid(v)


def _rms_r(xv):
    return lax.rsqrt(jnp.mean(xv * xv, axis=-1, keepdims=True) + EPS)


def _norm_matmul(name, x, g, ws, swiglu):
    T, D = x.shape
    N = ws[0].shape[1]
    tm = _pick(T, (ROW_TILE, 512, 256, 128))
    tn = _pick(N, (512, 256, 128))
    nw = len(ws)

    def body(*refs):
        x_ref, g_ref = refs[:2]
        w_refs = refs[2:2 + nw]
        outs = refs[2 + nw:-1]
        hs = refs[-1]

        @pl.when(pl.program_id(1) == 0)
        def _():
            for r0 in range(0, tm, EPILOGUE_ROWS):
                rows = slice(r0, r0 + min(EPILOGUE_ROWS, tm))
                xv = x_ref[rows, :]
                hv = (xv * _rms_r(xv) * g_ref[...]).astype(BF16)
                hs[rows, :] = hv
                outs[0][rows, :] = hv

        h = hs[...]
        if swiglu:
            gt = _nn(h, w_refs[0][...])
            u = _nn(h, w_refs[1][...])
            sg = _sigmoid(gt)
            silu = gt * sg
            outs[1][...] = (u * (sg * (1.0 + gt * (1.0 - sg)))).astype(BF16)
            outs[2][...] = silu.astype(BF16)
            outs[3][...] = (silu * u).astype(BF16)
        else:
            outs[1][...] = _nn(h, w_refs[0][...])

    row = pl.BlockSpec((tm, D), lambda i, j: (i, 0))
    col = pl.BlockSpec((D, tn), lambda i, j: (0, j))
    tile = pl.BlockSpec((tm, tn), lambda i, j: (i, j))
    if swiglu:
        out_shape = [jax.ShapeDtypeStruct((T, D), BF16)] + [jax.ShapeDtypeStruct((T, N), BF16)] * 3
        out_specs = [row, tile, tile, tile]
    else:
        out_shape = [jax.ShapeDtypeStruct((T, D), BF16), jax.ShapeDtypeStruct((T, N), F32)]
        out_specs = [row, tile]
    return pl.pallas_call(
        body, name=name, grid=(T // tm, N // tn),
        in_specs=[row, pl.BlockSpec((1, D), lambda i, j: (0, 0))] + [col] * nw,
        out_specs=out_specs, out_shape=out_shape,
        scratch_shapes=[pltpu.VMEM((tm, D), BF16)],
        compiler_params=_params(2),
    )(x, g, *ws)


def _matmul_res(name, a, w, res, scale, tgt=None):
    T, K = a.shape
    N = w.shape[1]
    loss = tgt is not None
    tm = _pick(T, (512, 256, 128))
    tk = _pick(K, (1408, 1024, 512, 256, 128))
    nk = K // tk

    def body(*refs):
        if loss:
            a_ref, w_ref, res_ref, tgt_ref, dx_ref, dxb_ref, lv_ref, acc = refs
        else:
            a_ref, w_ref, res_ref, out_ref, acc = refs
        i, k = pl.program_id(0), pl.program_id(1)

        @pl.when(k == 0)
        def _():
            acc[...] = jnp.zeros_like(acc)

        acc[...] += _nn(a_ref[...], w_ref[...])

        @pl.when(k == nk - 1)
        def _():
            part = jnp.zeros((1, N), F32)
            for r0 in range(0, tm, EPILOGUE_ROWS):
                rows = slice(r0, r0 + min(EPILOGUE_ROWS, tm))
                val = res_ref[rows, :] + scale * acc[rows, :]
                if loss:
                    dv = val - tgt_ref[rows, :]
                    dx = dv * (1.0 / N)
                    dx_ref[rows, :] = dx
                    dxb_ref[rows, :] = dx.astype(BF16)
                    part = part + jnp.sum(dv * dv, axis=0, keepdims=True)
                else:
                    out_ref[rows, :] = val
            if loss:
                @pl.when(i == 0)
                def _():
                    lv_ref[...] = part

                @pl.when(i > 0)
                def _():
                    lv_ref[...] += part

    row = pl.BlockSpec((tm, N), lambda i, k: (i, 0))
    in_specs = [pl.BlockSpec((tm, tk), lambda i, k: (i, k)), pl.BlockSpec((tk, N), lambda i, k: (k, 0)), row]
    args = [a, w, res]
    if loss:
        in_specs.append(row)
        args.append(tgt)
        out_specs = [row, row, pl.BlockSpec((1, N), lambda i, k: (0, 0))]
        out_shape = [jax.ShapeDtypeStruct((T, N), F32), jax.ShapeDtypeStruct((T, N), BF16),
                     jax.ShapeDtypeStruct((1, N), F32)]
    else:
        out_specs = row
        out_shape = jax.ShapeDtypeStruct((T, N), F32)
    return pl.pallas_call(
        body, name=name, grid=(T // tm, nk), in_specs=in_specs, out_specs=out_specs, out_shape=out_shape,
        scratch_shapes=[pltpu.VMEM((tm, N), F32)], compiler_params=_params(2),
    )(*args)


def _nt_matmul(name, dyb, w, scale=1.0, gate=None, up=None):
    T, D = dyb.shape
    N = w.shape[0]
    tm = _pick(T, (ROW_TILE, 512, 256, 128))
    tn = _pick(N, (512, 256, 128))
    swiglu = gate is not None

    def body(*refs):
        if swiglu:
            dy_ref, w_ref, g_ref, u_ref, dg_ref, du_ref = refs
        else:
            dy_ref, w_ref, o_ref, ob_ref = refs
        da = _nt(dy_ref[...], w_ref[...]) * scale
        if swiglu:
            dg_ref[...] = (da * g_ref[...].astype(F32)).astype(BF16)
            du_ref[...] = (da * u_ref[...].astype(F32)).astype(BF16)
        else:
            o_ref[...] = da
            ob_ref[...] = da.astype(BF16)

    tile = pl.BlockSpec((tm, tn), lambda i, j: (i, j))
    in_specs = [pl.BlockSpec((tm, D), lambda i, j: (i, 0)), pl.BlockSpec((tn, D), lambda i, j: (j, 0))]
    args = [dyb, w]
    if swiglu:
        in_specs += [tile, tile]
        args += [gate, up]
        out_shape = [jax.ShapeDtypeStruct((T, N), BF16)] * 2
    else:
        out_shape = [jax.ShapeDtypeStruct((T, N), F32), jax.ShapeDtypeStruct((T, N), BF16)]
    return pl.pallas_call(
        body, name=name, grid=(T // tm, N // tn), in_specs=in_specs, out_specs=[tile, tile],
        out_shape=out_shape, compiler_params=_params(2),
    )(*args)


def _nt_rms_bwd(name, As, Ws, x, g, dres):
    T, K = As[0].shape
    D = x.shape[1]
    na = len(As)
    tm = _pick(T, (ROW_TILE, 512, 256, 128))
    tk = _pick(K, (1024 // na, 512, 256, 128))
    nk = K // tk

    def body(*refs):
        a_refs = refs[:na]
        w_refs = refs[na:2 * na]
        x_hbm, g_ref, dres_hbm, acc, dxb_ref, dg_ref, x_ref, dres_ref, sems = refs[2 * na:]
        dx_ref = acc
        i, k = pl.program_id(0), pl.program_id(1)

        def row_copies():
            rows = pl.ds(pl.multiple_of(i * tm, tm), tm)
            return (pltpu.make_async_copy(x_hbm.at[rows, :], x_ref, sems.at[0]),
                    pltpu.make_async_copy(dres_hbm.at[rows, :], dres_ref, sems.at[1]))

        @pl.when(k == 0)
        def _():
            acc[...] = jnp.zeros_like(acc)
            for cp in row_copies():
                cp.start()

        for c0 in range(0, D, ACC_COLS):
            cols = slice(c0, min(c0 + ACC_COLS, D))
            part = _nt(a_refs[0][...], w_refs[0][cols, :])
            for a_ref, w_ref in zip(a_refs[1:], w_refs[1:]):
                part = part + _nt(a_ref[...], w_ref[cols, :])
            acc[:, cols] += part

        @pl.when(k == nk - 1)
        def _():
            for cp in row_copies():
                cp.wait()
            part = jnp.zeros((1, D), F32)
            for r0 in range(0, tm, EPILOGUE_ROWS):
                rows = slice(r0, r0 + min(EPILOGUE_ROWS, tm))
                dh = acc[rows, :]
                xv = x_ref[rows, :]
                r = _rms_r(xv)
                gd = dh * g_ref[...]
                dx = dres_ref[rows, :] + r * gd - xv * (r * r * r) * jnp.mean(gd * xv, axis=-1, keepdims=True)
                dx_ref[rows, :] = dx
                dxb_ref[rows, :] = dx.astype(BF16)
                part = part + jnp.sum(dh * xv * r, axis=0, keepdims=True)

            @pl.when(i == 0)
            def _():
                dg_ref[...] = part

            @pl.when(i > 0)
            def _():
                dg_ref[...] += part

    row = pl.BlockSpec((tm, D), lambda i, k: (i, 0), pipeline_mode=pl.Buffered(1))
    vec = pl.BlockSpec((1, D), lambda i, k: (0, 0))
    return pl.pallas_call(
        body, name=name, grid=(T // tm, nk),
        in_specs=[pl.BlockSpec((tm, tk), lambda i, k: (i, k))] * na
        + [pl.BlockSpec((D, tk), lambda i, k: (0, k))] * na + [ANY, vec, ANY],
        out_specs=[row, row, vec],
        out_shape=[jax.ShapeDtypeStruct((T, D), F32), jax.ShapeDtypeStruct((T, D), BF16),
                   jax.ShapeDtypeStruct((1, D), F32)],
        scratch_shapes=[pltpu.VMEM((tm, D), F32), pltpu.VMEM((tm, D), F32), pltpu.SemaphoreType.DMA((2,))],
        compiler_params=_params(2),
    )(*As, *Ws, x, g, dres)


def _tn_matmul(name, a, b, scale=1.0):
    T, M = a.shape
    N = b.shape[1]
    tn = _pick(N, (2048, 1408, 1280, 1024, 512, 256, 128))
    tm = _pick(M, tuple(c for c in (2048, 1408, 1024, 512, 256, 128) if c * tn <= TN_ACC_ELEMS))
    tk = _pick(T, (1024, 512, 256, 128))
    nk = T // tk

    def body(a_ref, b_ref, o_ref, acc):
        k = pl.program_id(2)

        @pl.when(k == 0)
        def _():
            acc[...] = jnp.zeros_like(acc)

        for r0 in range(0, tm, ACC_COLS):
            rows = slice(r0, min(r0 + ACC_COLS, tm))
            acc[rows, :] += _tn(a_ref[:, rows], b_ref[...])

        @pl.when(k == nk - 1)
        def _():
            o_ref[...] = (acc[...] * scale).astype(BF16)

    return pl.pallas_call(
        body, name=name, grid=(M // tm, N // tn, nk),
        in_specs=[pl.BlockSpec((tk, tm), lambda i, j, k: (k, i)), pl.BlockSpec((tk, tn), lambda i, j, k: (k, j))],
        out_specs=pl.BlockSpec((tm, tn), lambda i, j, k: (i, j)),
        out_shape=jax.ShapeDtypeStruct((M, N), BF16),
        scratch_shapes=[pltpu.VMEM((tm, tn), F32)], compiler_params=_params(3),
    )(a, b)


CONV_ROWS = 128
ROW_CHUNK = 32
LANE_CHUNK = 256


SUBLANES = 8


def _fill_shifts(sh, buf, rows):
    for p in range(1, SUBLANES):
        sh[p, 0:rows - SUBLANES, :] = buf[p:p + rows - SUBLANES, :]


def _tap(buf, sh, s, n, cols):
    p = s % SUBLANES
    return buf[s:s + n, cols] if p == 0 else sh[p, s - p:s - p + n, cols]


def _conv_fwd(name, z, w32, b, lg, lb, C):
    T = z.shape[0]
    tc = CONV_ROWS
    ntap = 31
    lc = _pick(C, (LANE_CHUNK, LANES))
    rpb = tc // HALO

    def body(zc_ref, zp_ref, w_ref, b_ref, lg_ref, lb_ref, yc_ref, ycv_ref, vbuf, ybuf, vsh):
        i = pl.program_id(0)
        zc = zc_ref[...]
        zp = zp_ref[...]
        vbuf[HALO:HALO + tc, :] = zc[:, :C] * _sigmoid(zc[:, C:])
        vbuf[0:HALO, :] = jnp.where(i > 0, zp[:, :C] * _sigmoid(zp[:, C:]), 0.0)
        _fill_shifts(vsh, vbuf, tc + HALO)
        for r0 in range(0, tc, ROW_CHUNK):
            for c0 in range(0, C, lc):
                cols = slice(c0, c0 + lc)
                acc = jnp.zeros((ROW_CHUNK, lc), F32) + b_ref[:, cols]
                for k in range(ntap):
                    acc = acc + w_ref[k:k + 1, cols] * _tap(vbuf, vsh, r0 + 2 + k, ROW_CHUNK, cols)
                ybuf[r0:r0 + ROW_CHUNK, cols] = acc
        y = ybuf[...]
        ycv_ref[...] = y
        mu = jnp.mean(y, axis=-1, keepdims=True)
        yc = y - mu
        rstd = lax.rsqrt(jnp.mean(yc * yc, axis=-1, keepdims=True) + EPS)
        ln = yc * rstd * lg_ref[...] + lb_ref[...]
        yc_ref[...] = (ln * _sigmoid(ln)).astype(BF16)

    vec = pl.BlockSpec((1, C), lambda i: (0, 0))
    return pl.pallas_call(
        body, name=name, grid=(T // tc,),
        in_specs=[pl.BlockSpec((tc, 2 * C), lambda i: (i, 0)),
                  pl.BlockSpec((HALO, 2 * C), lambda i: (jnp.maximum(i * rpb - 1, 0), 0)),
                  pl.BlockSpec((HALO, C), lambda i: (0, 0)), vec, vec, vec],
        out_specs=[pl.BlockSpec((tc, C), lambda i: (i, 0))] * 2,
        out_shape=[jax.ShapeDtypeStruct((T, 2 * C), BF16), jax.ShapeDtypeStruct((T, C), F32)],
        scratch_shapes=[pltpu.VMEM((tc + HALO, C), F32), pltpu.VMEM((tc, C), F32),
                        pltpu.VMEM((SUBLANES, tc + HALO, C), F32)],
        compiler_params=_params(1),
    )(z, z, w32, b, lg, lb)


def _conv_bwd(name, z, ycv, dycat, w32, lg, lb, C):
    T = z.shape[0]
    tc = CONV_ROWS
    ntap = 31
    lc = _pick(C, (LANE_CHUNK, LANES))
    rpb = tc // HALO
    nstep = T // tc
    nhb = T // HALO

    def ln_bwd(dyc, y, lgv, lbv):
        mu = jnp.mean(y, axis=-1, keepdims=True)
        yc = y - mu
        rstd = lax.rsqrt(jnp.mean(yc * yc, axis=-1, keepdims=True) + EPS)
        yn = yc * rstd
        ln = yn * lgv + lbv
        sg = _sigmoid(ln)
        dln = dyc * (sg * (1.0 + ln * (1.0 - sg)))
        dyn = dln * lgv
        dy = rstd * (dyn - jnp.mean(dyn, axis=-1, keepdims=True)
                     - yn * jnp.mean(dyn * yn, axis=-1, keepdims=True))
        return dy, dln, yn

    def body(zc_ref, zp_ref, y_ref, yn_ref, d_ref, dn_ref, w_ref, lg_ref, lb_ref,
             dz_ref, dw_ref, db_ref, dlg_ref, dlb_ref, vbuf, dbuf, dvbuf, dwacc, vsh, dsh):
        i = pl.program_id(0)
        lgv, lbv = lg_ref[...], lb_ref[...]
        zc = zc_ref[...]
        zp = zp_ref[...]
        a = zc[:, :C]
        sgt = _sigmoid(zc[:, C:])
        vbuf[HALO:HALO + tc, :] = a * sgt
        vbuf[0:HALO, :] = jnp.where(i > 0, zp[:, :C] * _sigmoid(zp[:, C:]), 0.0)
        dy, dln, yn = ln_bwd(d_ref[...], y_ref[...], lgv, lbv)
        dbuf[0:tc, :] = dy
        dyn_, _, _ = ln_bwd(dn_ref[...], yn_ref[...], lgv, lbv)
        dbuf[tc:tc + HALO, :] = jnp.where(i < nstep - 1, dyn_, 0.0)

        @pl.when(i == 0)
        def _():
            dwacc[...] = jnp.zeros_like(dwacc)
            db_ref[...] = jnp.zeros_like(db_ref)
            dlg_ref[...] = jnp.zeros_like(dlg_ref)
            dlb_ref[...] = jnp.zeros_like(dlb_ref)

        db_ref[...] += jnp.sum(dy, axis=0, keepdims=True)
        dlg_ref[...] += jnp.sum(dln * yn, axis=0, keepdims=True)
        dlb_ref[...] += jnp.sum(dln, axis=0, keepdims=True)

        _fill_shifts(vsh, vbuf, tc + HALO)
        _fill_shifts(dsh, dbuf, tc + HALO)
        for r0 in range(0, tc, ROW_CHUNK):
            for c0 in range(0, C, lc):
                cols = slice(c0, c0 + lc)
                dcur = dbuf[r0:r0 + ROW_CHUNK, cols]
                acc = jnp.zeros((ROW_CHUNK, lc), F32)
                for k in range(ntap):
                    acc = acc + w_ref[k:k + 1, cols] * _tap(dbuf, dsh, r0 + 30 - k, ROW_CHUNK, cols)
                    prod = dcur * _tap(vbuf, vsh, r0 + 2 + k, ROW_CHUNK, cols)
                    red = prod[0:8]
                    for q in range(8, ROW_CHUNK, 8):
                        red = red + prod[q:q + 8]
                    dwacc[8 * k:8 * k + 8, cols] += red
                dvbuf[r0:r0 + ROW_CHUNK, cols] = acc
        dv = dvbuf[...]
        dz_ref[:, :C] = (dv * sgt).astype(BF16)
        dz_ref[:, C:] = (dv * a * sgt * (1.0 - sgt)).astype(BF16)

        @pl.when(i == nstep - 1)
        def _():
            for k in range(ntap):
                dw_ref[k:k + 1, :] = jnp.sum(dwacc[8 * k:8 * k + 8, :], axis=0, keepdims=True)
            dw_ref[ntap:HALO, :] = jnp.zeros((HALO - ntap, C), F32)

    vec = pl.BlockSpec((1, C), lambda i: (0, 0))
    cur = pl.BlockSpec((tc, C), lambda i: (i, 0))
    nxt = pl.BlockSpec((HALO, C), lambda i: (jnp.minimum((i + 1) * rpb, nhb - 1), 0))
    return pl.pallas_call(
        body, name=name, grid=(nstep,),
        in_specs=[pl.BlockSpec((tc, 2 * C), lambda i: (i, 0)),
                  pl.BlockSpec((HALO, 2 * C), lambda i: (jnp.maximum(i * rpb - 1, 0), 0)),
                  cur, nxt, cur, nxt, pl.BlockSpec((HALO, C), lambda i: (0, 0)), vec, vec],
        out_specs=[pl.BlockSpec((tc, 2 * C), lambda i: (i, 0)), pl.BlockSpec((HALO, C), lambda i: (0, 0)),
                   vec, vec, vec],
        out_shape=[jax.ShapeDtypeStruct((T, 2 * C), BF16), jax.ShapeDtypeStruct((HALO, C), F32),
                   jax.ShapeDtypeStruct((1, C), F32), jax.ShapeDtypeStruct((1, C), F32),
                   jax.ShapeDtypeStruct((1, C), F32)],
        scratch_shapes=[pltpu.VMEM((tc + HALO, C), F32), pltpu.VMEM((tc + HALO, C), F32),
                        pltpu.VMEM((tc, C), F32), pltpu.VMEM((8 * HALO, C), F32),
                        pltpu.VMEM((SUBLANES, tc + HALO, C), F32), pltpu.VMEM((SUBLANES, tc + HALO, C), F32)],
        compiler_params=_params(1),
    )(z, z, ycv, ycv, dycat, dycat, w32, lg, lb)


def _seg_sum(u, bmat):
    hi = u.astype(BF16)
    lo = (u - hi.astype(F32)).astype(BF16)
    parts = [_nn(hi[:, c:c + LANES], bmat) + _nn(lo[:, c:c + LANES], bmat) for c in range(0, u.shape[1], LANES)]
    return jnp.concatenate(parts, axis=1)


def _attn_prep(name, z, gq, gk, bmat, A, c0, hd):
    T = z.shape[0]
    tm = _pick(T, (256, 128))

    def body(zq_ref, zk_ref, zv_ref, gq_ref, gk_ref, b_ref, o_ref):
        bm = b_ref[...]
        for idx, (z_ref, g_ref) in enumerate(((zq_ref, gq_ref), (zk_ref, gk_ref))):
            zv = z_ref[...]
            r = lax.rsqrt(_seg_sum(zv * zv, bm) * (1.0 / hd) + EPS)
            o_ref[:, idx * A:(idx + 1) * A] = zv * r * g_ref[...]
        o_ref[:, 2 * A:] = zv_ref[...]

    vec = pl.BlockSpec((1, A), lambda i: (0, 0))
    return pl.pallas_call(
        body, name=name, grid=(T // tm,),
        in_specs=[pl.BlockSpec((tm, A), lambda i: (i, c0)), pl.BlockSpec((tm, A), lambda i: (i, c0 + 1)),
                  pl.BlockSpec((tm, A), lambda i: (i, c0 + 2)), vec, vec,
                  pl.BlockSpec((LANES, LANES), lambda i: (0, 0))],
        out_specs=pl.BlockSpec((tm, 3 * A), lambda i: (i, 0)),
        out_shape=jax.ShapeDtypeStruct((T, 3 * A), F32), compiler_params=_params(1),
    )(z, z, z, gq, gk, bmat)


QK_SCALE = 0.125
ATTN_UNROLL = 8
ATTN_FWD_UNROLL = 8


def _fill_bias(bias, sl_ref, hp, d):
    qi = lax.broadcasted_iota(jnp.int32, (WINDOW, 2 * WINDOW), 0)
    kj = lax.broadcasted_iota(jnp.int32, (WINDOW, 2 * WINDOW), 1)
    dist = WINDOW + qi - kj
    inband = (dist >= 0) & (dist <= WINDOW)
    distf = dist.astype(F32)
    for hh in range(2):
        b = jnp.where(inband, -(sl_ref[2 * hp + hh] * d) * distf, NEG)
        bias[2 * hh + 1] = b
        bias[2 * hh] = jnp.where(kj >= WINDOW, b, NEG)


CHUNK = WINDOW * DILATIONS[-1]
assert DILATIONS[0] == 1


def _deinterleave(dst, src, d, rows, dst_pitch, dst_off, src_off):
    for r in range(d):
        if d == 1:
            val = src[src_off:src_off + rows, :]
        else:
            val = src[pl.ds(src_off + r, rows, stride=d), :]
        lo = r * dst_pitch + dst_off
        dst[lo:lo + rows, :] = val.astype(dst.dtype)


def _interleave_add(dst, start, src, d, rows, src_pitch, src_off):
    for r in range(d):
        lo = r * src_pitch + src_off
        idx = pl.ds(start, rows) if d == 1 else pl.ds(start + r, rows, stride=d)
        dst[idx, :] += src[lo:lo + rows, :]


def _attn_fwd(name, qkv, slopes, ycat, A):
    T = qkv.shape[0]
    hpn, nch, nblk = A // LANES, T // CHUNK, CHUNK // WINDOW
    nbranch = len(DILATIONS)
    yoff = (ycat.shape[1] - A) // LANES

    def body(*refs):
        sl_ref, q_ref, k_ref, kp_ref, v_ref, vp_ref, _, y_ref, lg_ref, qd, kd, vd, od, ld, bias = refs[:15]
        onat, lnat = refs[15:15 + nbranch], refs[15 + nbranch:]
        hp, ch = pl.program_id(0), pl.program_id(1)
        lane = lax.broadcasted_iota(jnp.int32, (1, LANES), 1)
        first = lane < (LANES // 2)
        for bi, d in enumerate(DILATIONS):
            Ld = CHUNK // d
            seg = Ld + WINDOW
            nbr = Ld // WINDOW
            _deinterleave(qd, q_ref, d, Ld, Ld, 0, 0)
            for dst, cur, prev in ((kd, k_ref, kp_ref), (vd, v_ref, vp_ref)):
                _deinterleave(dst, prev, d, WINDOW, seg, 0, CHUNK - WINDOW * d)
                _deinterleave(dst, cur, d, Ld, seg, WINDOW, 0)
            _fill_bias(bias, sl_ref, hp, d)
            ob, lb = (onat[bi], lnat[bi]) if d == 1 else (od, ld)

            def step(it, carry, Ld=Ld, seg=seg, nbr=nbr, ob=ob, lb=lb):
                r, nl = it // nbr, it % nbr
                q0 = pl.multiple_of(r * Ld + nl * WINDOW, WINDOW)
                k0 = pl.multiple_of(r * seg + nl * WINDOW, WINDOW)
                later = jnp.where(ch * nbr + nl > 0, 1, 0)
                qb = qd[pl.ds(q0, WINDOW), :]
                k2 = kd[pl.ds(k0, 2 * WINDOW), :]
                v2 = vd[pl.ds(k0, 2 * WINDOW), :]
                res = []
                for hh in range(2):
                    mh = first if hh == 0 else jnp.logical_not(first)
                    s = _nt(jnp.where(mh, qb, jnp.zeros_like(qb)), k2) + bias[2 * hh + later]
                    mx = jnp.max(s, axis=-1, keepdims=True)
                    p = jnp.exp(s - mx)
                    den = jnp.sum(p, axis=-1, keepdims=True)
                    res.append((_nn(p.astype(BF16), v2) / den, mx + jnp.log(den)))
                ob[pl.ds(q0, WINDOW), :] = jnp.where(first, res[0][0], res[1][0])
                lb[pl.ds(q0, WINDOW), :] = jnp.where(first, res[0][1], res[1][1])
                return carry

            lax.fori_loop(0, nblk, step, 0, unroll=ATTN_FWD_UNROLL)
            if d > 1:
                for r in range(d):
                    onat[bi][pl.ds(r, Ld, stride=d), :] = od[r * Ld:(r + 1) * Ld, :]
                    lnat[bi][pl.ds(r, Ld, stride=d), :] = ld[r * Ld:(r + 1) * Ld, :]
        ls = [l[...] for l in lnat]
        mx = ls[0]
        for v in ls[1:]:
            mx = jnp.maximum(mx, v)
        es = [jnp.exp(v - mx) for v in ls]
        den = es[0]
        for e in es[1:]:
            den = den + e
        out = es[0] * onat[0][...]
        for e, o in zip(es[1:], onat[1:]):
            out = out + e * o[...]
        y_ref[...] = (out / den).astype(BF16)
        lg_ref[...] = mx + jnp.log(den)

    blk = lambda m: pl.BlockSpec((CHUNK, LANES), m)
    cur = lambda which: blk(lambda hp, ch: (ch, which * hpn + hp))
    prev = lambda which: blk(lambda hp, ch: (jnp.maximum(ch - 1, 0), which * hpn + hp))
    omap = blk(lambda hp, ch: (ch, hp))
    f32buf = pltpu.VMEM((CHUNK, LANES), F32)
    return pl.pallas_call(
        body, name=name, grid=(hpn, nch),
        in_specs=[pl.BlockSpec(memory_space=pltpu.SMEM), cur(0), cur(1), prev(1), cur(2), prev(2), ANY],
        out_specs=[blk(lambda hp, ch: (ch, yoff + hp)), omap],
        out_shape=[jax.ShapeDtypeStruct(ycat.shape, BF16), jax.ShapeDtypeStruct((T, A), F32)],
        input_output_aliases={6: 0},
        scratch_shapes=[pltpu.VMEM((CHUNK, LANES), BF16), pltpu.VMEM((2 * CHUNK, LANES), BF16),
                        pltpu.VMEM((2 * CHUNK, LANES), BF16), f32buf, f32buf,
                        pltpu.VMEM((4, WINDOW, 2 * WINDOW), F32)] + [f32buf] * (2 * nbranch),
        compiler_params=_params(2),
    )(slopes, qkv, qkv, qkv, qkv, qkv, ycat)


def _attn_bwd(name, qkv, dycat, ycat, lg, slopes, bmat, A, catoff):
    T = qkv.shape[0]
    hpn, nch, nblk = A // LANES, T // CHUNK, CHUNK // WINDOW
    co = catoff // LANES

    def body(sl_ref, q_ref, k_ref, kp_ref, v_ref, vp_ref, do_ref, o_ref, l_ref, b_ref, dq_ref, dk_ref, dv_ref,
             qd, kd, vd, dod, ddn, ddd, ldd, dqd, dkd, dvd, bias):
        hp, ch = pl.program_id(0), pl.program_id(1)
        lane = lax.broadcasted_iota(jnp.int32, (1, LANES), 1)
        first = lane < (LANES // 2)
        ddn[...] = _seg_sum(do_ref[...] * o_ref[...].astype(F32), b_ref[...])

        @pl.when(ch == 0)
        def _():
            dk_ref[...] = jnp.zeros_like(dk_ref)
            dv_ref[...] = jnp.zeros_like(dv_ref)

        base = ch * CHUNK
        for d in DILATIONS:
            Ld = CHUNK // d
            seg = Ld + WINDOW
            nbr = Ld // WINDOW
            _deinterleave(qd, q_ref, d, Ld, Ld, 0, 0)
            _deinterleave(dod, do_ref, d, Ld, Ld, 0, 0)
            if d > 1:
                _deinterleave(ddd, ddn, d, Ld, Ld, 0, 0)
                _deinterleave(ldd, l_ref, d, Ld, Ld, 0, 0)
            dsrc, lsrc, dqdst = (ddn, l_ref, dq_ref) if d == 1 else (ddd, ldd, dqd)
            for dst, cur, prev in ((kd, k_ref, kp_ref), (vd, v_ref, vp_ref)):
                _deinterleave(dst, prev, d, WINDOW, seg, 0, CHUNK - WINDOW * d)
                _deinterleave(dst, cur, d, Ld, seg, WINDOW, 0)
            dkd[0:d * seg, :] = jnp.zeros((d * seg, LANES), F32)
            dvd[0:d * seg, :] = jnp.zeros((d * seg, LANES), F32)
            _fill_bias(bias, sl_ref, hp, d)

            def step(it, carry, Ld=Ld, seg=seg, nbr=nbr, dsrc=dsrc, lsrc=lsrc, dqdst=dqdst):
                r, nl = it // nbr, it % nbr
                q0 = pl.multiple_of(r * Ld + nl * WINDOW, WINDOW)
                k0 = pl.multiple_of(r * seg + nl * WINDOW, WINDOW)
                later = jnp.where(ch * nbr + nl > 0, 1, 0)
                qb = qd[pl.ds(q0, WINDOW), :]
                k2 = kd[pl.ds(k0, 2 * WINDOW), :]
                v2 = vd[pl.ds(k0, 2 * WINDOW), :]
                dob = dod[pl.ds(q0, WINDOW), :]
                dd = dsrc[pl.ds(q0, WINDOW), :]
                lb = lsrc[pl.ds(q0, WINDOW), :]
                dk2 = jnp.zeros((2 * WINDOW, LANES), F32)
                dv2 = jnp.zeros((2 * WINDOW, LANES), F32)
                dqs = []
                for hh in range(2):
                    mh = first if hh == 0 else jnp.logical_not(first)
                    qh = jnp.where(mh, qb, jnp.zeros_like(qb))
                    doh = jnp.where(mh, dob, jnp.zeros_like(dob))
                    lcol = lb[:, hh * (LANES // 2):hh * (LANES // 2) + 1]
                    p = jnp.exp(_nt(qh, k2) + bias[2 * hh + later] - lcol)
                    dcol = dd[:, hh * (LANES // 2):hh * (LANES // 2) + 1]
                    ds = (p * (_nt(doh, v2) - dcol)).astype(BF16)
                    dqs.append(_nn(ds, k2))
                    dk2 = dk2 + _tn(ds, qh)
                    dv2 = dv2 + _tn(p.astype(BF16), doh)
                dqdst[pl.ds(q0, WINDOW), :] = jnp.where(first, dqs[0], dqs[1])
                dkd[pl.ds(k0, 2 * WINDOW), :] += dk2
                dvd[pl.ds(k0, 2 * WINDOW), :] += dv2
                return carry

            lax.fori_loop(0, nblk, step, 0, unroll=ATTN_UNROLL)
            if d > 1:
                _interleave_add(dq_ref, 0, dqd, d, Ld, Ld, 0)
            for acc, out in ((dkd, dk_ref), (dvd, dv_ref)):
                _interleave_add(out, base, acc, d, Ld, seg, WINDOW)

                @pl.when(ch > 0)
                def _(acc=acc, out=out, d=d, seg=seg):
                    _interleave_add(out, base - WINDOW * d, acc, d, WINDOW, seg, 0)

    blk = lambda m: pl.BlockSpec((CHUNK, LANES), m)
    cur = lambda which: blk(lambda hp, ch: (ch, which * hpn + hp))
    prev = lambda which: blk(lambda hp, ch: (jnp.maximum(ch - 1, 0), which * hpn + hp))
    omap = blk(lambda hp, ch: (ch, hp))
    full = pl.BlockSpec((T, LANES), lambda hp, ch: (0, hp))
    f32buf = pltpu.VMEM((CHUNK, LANES), F32)
    bf16buf = pltpu.VMEM((CHUNK, LANES), BF16)
    return pl.pallas_call(
        body, name=name, grid=(hpn, nch),
        in_specs=[pl.BlockSpec(memory_space=pltpu.SMEM), cur(0), cur(1), prev(1), cur(2), prev(2),
                  blk(lambda hp, ch: (ch, co + hp)), blk(lambda hp, ch: (ch, co + hp)), omap,
                  pl.BlockSpec((LANES, LANES), lambda hp, ch: (0, 0))],
        out_specs=[omap, full, full],
        out_shape=[jax.ShapeDtypeStruct((T, A), F32)] * 3,
        scratch_shapes=[bf16buf, pltpu.VMEM((2 * CHUNK, LANES), BF16), pltpu.VMEM((2 * CHUNK, LANES), BF16),
                        bf16buf, f32buf, f32buf, f32buf, f32buf,
                        pltpu.VMEM((2 * CHUNK, LANES), F32), pltpu.VMEM((2 * CHUNK, LANES), F32),
                        pltpu.VMEM((4, WINDOW, 2 * WINDOW), F32)],
        compiler_params=_params(2),
    )(slopes, qkv, qkv, qkv, qkv, qkv, dycat, ycat, lg, bmat)


def _attn_bwd_combine(name, dzc, dqs, dks, dvs, z, gq, gk, bmat, fmat, A, c0, hd):
    T = z.shape[0]
    tm = _pick(T, (256, 128))
    nbr = len(dqs)
    W0 = dzc.shape[1]

    def body(*refs):
        dq_refs, dk_refs, dv_refs = refs[:nbr], refs[nbr:2 * nbr], refs[2 * nbr:3 * nbr]
        zq_ref, zk_ref, gq_ref, gk_ref, b_ref, f_ref, dzc_ref, dz_ref, dgq_ref, dgk_ref = refs[3 * nbr:]
        i = pl.program_id(0)
        bm = b_ref[...]
        dz_ref[:, :W0] = dzc_ref[...]

        def tot(rs):
            t = rs[0][...]
            for r in rs[1:]:
                t = t + r[...]
            return t

        for idx, (d_refs, z_ref, g_ref, dg_ref, gscale) in enumerate(
                ((dq_refs, zq_ref, gq_ref, dgq_ref, QK_SCALE), (dk_refs, zk_ref, gk_ref, dgk_ref, 1.0))):
            dy = tot(d_refs)
            zv = z_ref[...]
            r = lax.rsqrt(_seg_sum(zv * zv, bm) * (1.0 / hd) + EPS)
            gd = dy * g_ref[...]
            mean = _seg_sum(gd * zv, bm) * (1.0 / hd)
            dz_ref[:, W0 + idx * A:W0 + (idx + 1) * A] = (r * gd - zv * (r * r * r) * mean).astype(BF16)
            part = jnp.sum(dy * zv * r, axis=0, keepdims=True) * gscale

            @pl.when(i == 0)
            def _():
                dg_ref[...] = part

            @pl.when(i > 0)
            def _():
                dg_ref[...] += part

        dz_ref[:, W0 + 2 * A:] = tot(dv_refs).astype(BF16)

        @pl.when(i == T // tm - 1)
        def _():
            fm = f_ref[...]
            for dg_ref in (dgq_ref, dgk_ref):
                v = jnp.broadcast_to(dg_ref[...], (8, A))
                hi = v.astype(BF16)
                mid = (v - hi.astype(F32)).astype(BF16)
                lo = (v - hi.astype(F32) - mid.astype(F32)).astype(BF16)
                dg_ref[...] = (_nn(hi, fm) + _nn(mid, fm) + _nn(lo, fm))[0:1]

    blk = pl.BlockSpec((tm, A), lambda i: (i, 0))
    vec = pl.BlockSpec((1, A), lambda i: (0, 0))
    return pl.pallas_call(
        body, name=name, grid=(T // tm,),
        in_specs=[blk] * (3 * nbr) + [pl.BlockSpec((tm, A), lambda i: (i, c0)),
                                      pl.BlockSpec((tm, A), lambda i: (i, c0 + 1)), vec, vec,
                                      pl.BlockSpec((LANES, LANES), lambda i: (0, 0)),
                                      pl.BlockSpec((A, A), lambda i: (0, 0)),
                                      pl.BlockSpec((tm, W0), lambda i: (i, 0))],
        out_specs=[pl.BlockSpec((tm, W0 + 3 * A), lambda i: (i, 0)), vec, vec],
        out_shape=[jax.ShapeDtypeStruct((T, W0 + 3 * A), BF16), jax.ShapeDtypeStruct((1, A), F32),
                   jax.ShapeDtypeStruct((1, A), F32)],
        compiler_params=_params(1),
    )(*dqs, *dks, *dvs, z, z, gq, gk, bmat, fmat, dzc)


def _local_step(x, tgt, S, comm, hd):
    T, D = x.shape
    C = S["conv_b_dw"].shape[1]
    A = C
    H = A // hd
    Dmix = C + A
    c0 = (2 * C) // A
    slopes = 2.0 ** (-ALIBI_MAX_BIAS * jnp.arange(1, H + 1, dtype=F32) / H)
    seg = jnp.arange(LANES) // hd
    bmat = (seg[:, None] == seg[None, :]).astype(BF16)
    pos_in_head = jnp.arange(A) % hd
    fmat = (pos_in_head[:, None] == pos_in_head[None, :]).astype(BF16)
    gq = jnp.tile(S["q_norm_g"], (1, H)) * QK_SCALE
    gk = jnp.tile(S["k_norm_g"], (1, H))

    wg1, wu1 = comm.weights(("ffn1_w_gate", "ffn1_w_up"), None)
    h1, gate1, up1, a1 = _norm_matmul("ffn1_up", x, comm.tie(S["ffn1_norm_g"]), [wg1, wu1], True)
    wd1, win, w32 = comm.weights(("ffn1_w_down", "w_in", "conv_w32"), a1)
    x1 = _matmul_res("ffn1_down", a1, wd1, x, 0.5)
    h2, z = _norm_matmul("mix_in", x1, S["mix_norm_g"], [win], False)
    yc, ycv = _conv_fwd("conv_fwd", z, w32, S["conv_b_dw"], S["conv_ln_g"], S["conv_ln_b"], C)
    qkv = _attn_prep("attn_prep", z, gq, gk, bmat, A, c0, hd)
    ycat, lg = _attn_fwd("attn_fwd", qkv, slopes, yc, A)
    wout, wg2, wu2, wd2 = comm.weights(("w_out", "ffn2_w_gate", "ffn2_w_up", "ffn2_w_down"), lg)
    x2 = _matmul_res("mix_out", ycat, wout, x1, 1.0)
    h3, gate2, up2, a2 = _norm_matmul("ffn2_up", x2, S["ffn2_norm_g"], [wg2, wu2], True)
    dx3, dx3b, lossvec = _matmul_res("ffn2_down_loss", a2, wd2, x2, 0.5, tgt=tgt)

    G = {}
    dgate2, dup2 = _nt_matmul("ffn2_dact", dx3b, wd2, 0.5, gate2, up2)
    comm.reduce_begin("ffn2", {"ffn2_w_down": _tn_matmul("ffn2_dwd", a2, dx3b, 0.5),
                               "ffn2_w_gate": _tn_matmul("ffn2_dwg", h3, dgate2),
                               "ffn2_w_up": _tn_matmul("ffn2_dwu", h3, dup2)}, split=True)
    dx2, dx2b, G["ffn2_norm_g"] = _nt_rms_bwd("ffn2_dx", [dgate2, dup2], [wg2, wu2],
                                              x2, comm.tie(S["ffn2_norm_g"]), dx3)
    comm.reduce_scatter("ffn2", dx2b)
    dwout = _tn_matmul("mix_dwout", ycat, dx2b)
    dycat, _ = _nt_matmul("mix_dycat", dx2b, wout)
    dzc, G["conv_w32"], G["conv_b_dw"], G["conv_ln_g"], G["conv_ln_b"] = _conv_bwd(
        "conv_bwd", z, ycv, dycat, w32, S["conv_ln_g"], S["conv_ln_b"], C)
    dq, dk, dv = _attn_bwd("attn_bwd", qkv, dycat, ycat, lg, slopes, bmat, A, C)
    dz, G["q_norm_g"], G["k_norm_g"] = _attn_bwd_combine(
        "attn_bwd_combine", dzc, [dq], [dk], [dv], z, gq, gk, bmat, fmat, A, c0, hd)
    comm.reduce_end("ffn2", dz)
    comm.reduce_begin("mix", {"w_out": dwout, "w_in": _tn_matmul("mix_dwin", h2, dz)}, split=True)
    dx1, dx1b, G["mix_norm_g"] = _nt_rms_bwd("mix_dx", [dz], [win], x1, comm.tie(S["mix_norm_g"]), dx2)
    comm.reduce_scatter("mix", dx1b)
    dgate1, dup1 = _nt_matmul("ffn1_dact", dx1b, wd1, 0.5, gate1, up1)
    dwd1 = _tn_matmul("ffn1_dwd", a1, dx1b, 0.5)
    dwg1 = _tn_matmul("ffn1_dwg", h1, dgate1)
    dwu1 = _tn_matmul("ffn1_dwu", h1, dup1)
    comm.reduce_end("mix", dwu1)
    comm.reduce_begin("ffn1", {"ffn1_w_down": dwd1, "ffn1_w_gate": dwg1, "ffn1_w_up": dwu1})
    dx0, _, G["ffn1_norm_g"] = _nt_rms_bwd("ffn1_dx", [dgate1, dup1], [wg1, wu1],
                                           x, comm.tie(S["ffn1_norm_g"]), dx1)
    comm.reduce_end("ffn1", dx0)
    return lossvec, dx0, G


BIG = (("ffn1_w_gate", 1), ("ffn1_w_up", 1), ("ffn1_w_down", 0), ("w_in", 1), ("w_out", 0),
       ("ffn2_w_gate", 1), ("ffn2_w_up", 1), ("ffn2_w_down", 0))
AXIS = dict(BIG)
FLIPS = ((1, 0), (0, 1), (1, 1))
HBM = pl.BlockSpec(memory_space=pltpu.HBM)
SEM = pl.BlockSpec(memory_space=pltpu.SEMAPHORE)
EFFECT = pltpu.SideEffectType.DATAFLOW_SIDE_EFFECTING
TOKEN = jax.ShapeDtypeStruct((8, LANES), F32)


def _window(ref, shape, axis, slab=None, half=None):
    idx = [pl.ds(0, shape[0]), pl.ds(0, shape[1])]
    if slab is not None:
        n = shape[axis] // 4
        idx[axis] = pl.ds(pl.multiple_of(slab * n, 8), n)
    if half is not None:
        hs = shape[1 - axis] // 2
        idx[1 - axis] = pl.ds(pl.multiple_of(half * hs, 8), hs)
    return ref.at[idx[0], idx[1]]


def _position():
    return lax.axis_index("x"), lax.axis_index("y"), lax.axis_index("c")


def _half_shape(shape, axis):
    return (shape[0] // 2, shape[1]) if axis == 1 else (shape[0], shape[1] // 2)


def _slab_shape(shape, axis):
    return (shape[0], shape[1] // 4) if axis == 1 else (shape[0] // 4, shape[1])


def _piece_shape(shape, axis):
    return _half_shape(_slab_shape(shape, axis), axis)


def _full_shape(shard, axis):
    return (shard.shape[0], shard.shape[1] * 4) if axis == 1 else (shard.shape[0] * 4, shard.shape[1])


def _hbm(a):
    return pltpu.with_memory_space_constraint(a, pltpu.HBM)


def _remote(src, dst, send_sem, recv_sem, to):
    return pltpu.make_async_remote_copy(src_ref=src, dst_ref=dst, send_sem=send_sem, recv_sem=recv_sem,
                                        device_id=to, device_id_type=MESH)


def _place(name, pos, w, axis):
    R, Cc = w.shape
    tr = _pick(R, (256, 128, 64, 32, 16))
    nrb = R // tr

    def body(pos_ref, w_ref, o_ref):
        o_ref[...] = w_ref[...].astype(BF16)

    omap = (lambda i, p: (i, p[0])) if axis == 1 else (lambda i, p: (p[0] * nrb + i, 0))
    return pl.pallas_call(
        body, name=name,
        grid_spec=pltpu.PrefetchScalarGridSpec(
            num_scalar_prefetch=1, grid=(nrb,), in_specs=[pl.BlockSpec((tr, Cc), lambda i, p: (i, 0))],
            out_specs=pl.BlockSpec((tr, Cc), omap)),
        out_shape=jax.ShapeDtypeStruct(_full_shape(w, axis), BF16), compiler_params=_params(1),
    )(pos, w)


def _gather_now(name, axes, fulls):
    nt = len(fulls)
    shapes = [f.shape for f in fulls]

    def body(*refs):
        outs, token = refs[nt:2 * nt], refs[2 * nt]
        send_sems, recv_sems = refs[2 * nt + 1:]
        x, y, c = _position()
        j0 = 2 * x + y
        sib = (x, y, 1 - c)

        def copy(t, k, slab, half, to):
            win = _window(outs[t], shapes[t], axes[t], slab=slab, half=half)
            return _remote(win, win, send_sems.at[t, k], recv_sems.at[t, k], to)

        sends = []
        for k, (fx, fy) in enumerate(FLIPS):
            for t in range(nt):
                cp = copy(t, k, j0, c, (x ^ fx, y ^ fy, c))
                cp.start()
                sends.append(cp)
        for k, (fx, fy) in enumerate(FLIPS):
            js = 2 * (x ^ fx) + (y ^ fy)
            for t in range(nt):
                copy(t, k, js, c, sib).wait_recv()
                cp = copy(t, 3 + k, js, c, sib)
                cp.start()
                sends.append(cp)
        for k, (fx, fy) in enumerate(FLIPS):
            js = 2 * (x ^ fx) + (y ^ fy)
            for t in range(nt):
                copy(t, 3 + k, js, 1 - c, sib).wait_recv()
        for cp in sends:
            cp.wait_send()
        token[...] = jnp.zeros_like(token)

    res = pl.pallas_call(
        body, name=name, in_specs=[ANY] * nt,
        out_specs=[ANY] * nt + [pl.BlockSpec(memory_space=pltpu.VMEM)],
        out_shape=[jax.ShapeDtypeStruct(s, BF16) for s in shapes] + [TOKEN],
        input_output_aliases={t: t for t in range(nt)},
        scratch_shapes=[pltpu.SemaphoreType.DMA((nt, 6)), pltpu.SemaphoreType.DMA((nt, 6))],
    )(*fulls)
    return list(res[:nt]), res[nt]


def _split_start(name, arrays, ncopies, plan):
    na = len(arrays)

    def body(*refs):
        ins = refs[:na]
        send_sems, recv_sems = refs[na], refs[na + 1]
        token = refs[-1]
        x, y, c = _position()
        for i, (src, dst, to) in enumerate(plan(ins, x, y, c)):
            _remote(src, dst, send_sems.at[i], recv_sems.at[i], to).start()
        token[...] = jnp.zeros_like(token)

    res = pl.pallas_call(
        body, name=name, in_specs=[HBM] * na,
        out_specs=tuple([SEM, SEM] + [HBM] * na + [pl.BlockSpec(memory_space=pltpu.VMEM)]),
        out_shape=tuple([pltpu.SemaphoreType.DMA((ncopies,)), pltpu.SemaphoreType.DMA((ncopies,))]
                        + [pltpu.HBM(a.shape, a.dtype) for a in arrays] + [TOKEN]),
        input_output_aliases={i: 2 + i for i in range(na)},
        compiler_params=pltpu.CompilerParams(has_side_effects=EFFECT),
    )(*[_hbm(a) for a in arrays])
    return (res[0], res[1]), list(res[2:2 + na]), res[-1]


def _split_wait(name, arrays, sems, after, plan):
    na = len(arrays)

    def body(*refs):
        ins = refs[:na]
        send_sems, recv_sems = refs[na], refs[na + 1]
        x, y, c = _position()
        for i, (src, dst, to) in enumerate(plan(ins, x, y, c)):
            cp = _remote(src, dst, send_sems.at[i], recv_sems.at[i], to)
            cp.wait_send()
            cp.wait_recv()

    res = pl.pallas_call(
        body, name=name, in_specs=[HBM] * na + [SEM, SEM, ANY],
        out_specs=tuple([HBM] * na), out_shape=tuple(pltpu.HBM(a.shape, a.dtype) for a in arrays),
        input_output_aliases={i: i for i in range(na)},
        compiler_params=pltpu.CompilerParams(has_side_effects=EFFECT),
    )(*arrays, *sems, after)
    return list(res)


def _gather_plan(axes, shapes, conv_shape):
    nt = len(axes)

    def plan(refs, x, y, c):
        j0 = 2 * x + y
        out = []
        for fx, fy in FLIPS:
            to = (x ^ fx, y ^ fy, c)
            for t in range(nt):
                win = _window(refs[t], shapes[t], axes[t], slab=j0, half=c)
                out.append((win, win, to))
            if conv_shape is not None:
                win = _window(refs[nt], conv_shape, 1, slab=j0)
                out.append((win, win, to))
        return out

    return plan


def _gather_finish(name, axes, fulls):
    nt = len(axes)
    shapes = [f.shape for f in fulls]

    def body(*refs):
        outs = refs[nt:2 * nt]
        send_sems, recv_sems = refs[2 * nt:]
        x, y, c = _position()
        sib = (x, y, 1 - c)
        cps = []
        for k, (fx, fy) in enumerate(FLIPS):
            js = 2 * (x ^ fx) + (y ^ fy)
            for t in range(nt):
                landed = _window(outs[t], shapes[t], axes[t], slab=js, half=c)
                cp = _remote(landed, landed, send_sems.at[t, k], recv_sems.at[t, k], sib)
                cp.start()
                cps.append(cp)
        for k, (fx, fy) in enumerate(FLIPS):
            js = 2 * (x ^ fx) + (y ^ fy)
            for t in range(nt):
                other = _window(outs[t], shapes[t], axes[t], slab=js, half=1 - c)
                _remote(other, other, send_sems.at[t, k], recv_sems.at[t, k], sib).wait_recv()
        for cp in cps:
            cp.wait_send()

    res = pl.pallas_call(
        body, name=name, in_specs=[ANY] * nt, out_specs=[ANY] * nt,
        out_shape=[jax.ShapeDtypeStruct(f.shape, f.dtype) for f in fulls],
        input_output_aliases={t: t for t in range(nt)},
        scratch_shapes=[pltpu.SemaphoreType.DMA((nt, 3)), pltpu.SemaphoreType.DMA((nt, 3))],
    )(*fulls)
    return list(res)


def _pair_exchange(name, srcs, windows, out_shapes, dtype):
    nt = len(srcs)

    def body(*refs):
        ins, outs = refs[:nt], refs[nt:2 * nt]
        send_sems, recv_sems = refs[2 * nt:]
        x, y, c = _position()
        cps = []
        for t in range(nt):
            cp = _remote(windows[t](ins[t], c), outs[t], send_sems.at[t], recv_sems.at[t], (x, y, 1 - c))
            cp.start()
            cps.append(cp)
        for cp in cps:
            cp.wait()

    return pl.pallas_call(
        body, name=name, in_specs=[ANY] * nt, out_specs=[ANY] * nt,
        out_shape=[jax.ShapeDtypeStruct(s, dtype) for s in out_shapes],
        scratch_shapes=[pltpu.SemaphoreType.DMA((nt,)), pltpu.SemaphoreType.DMA((nt,))],
    )(*srcs)


def _pair_plan(axes, shapes):
    nt = len(axes)

    def plan(refs, x, y, c):
        return [(_window(refs[t], shapes[t], axes[t], half=1 - c), refs[nt + t], (x, y, 1 - c)) for t in range(nt)]

    return plan


def _scatter_plan(axes, shapes):
    nt = len(axes)

    def plan(refs, x, y, c):
        out = []
        for k, (fx, fy) in enumerate(FLIPS):
            js = 2 * (x ^ fx) + (y ^ fy)
            for t in range(nt):
                src = _window(refs[t], _half_shape(shapes[t], axes[t]), axes[t], slab=js)
                out.append((src, refs[nt + t].at[k], (x ^ fx, y ^ fy, c)))
        return out

    return plan


def _gather_small(packed):
    R, Cc = packed.shape

    def body(p_ref, o_ref, send_sems, recv_sems, loc_sem):
        x, y, c = _position()
        me = 4 * x + 2 * y + c
        mine = pltpu.make_async_copy(p_ref, o_ref.at[me], loc_sem)
        mine.start()
        cps = []
        for k in range(1, 8):
            fx, fy, fc = (k >> 2) & 1, (k >> 1) & 1, k & 1
            cp = pltpu.make_async_remote_copy(
                src_ref=p_ref, dst_ref=o_ref.at[me], send_sem=send_sems.at[k - 1], recv_sem=recv_sems.at[k - 1],
                device_id=(x ^ fx, y ^ fy, c ^ fc), device_id_type=MESH)
            cp.start()
            cps.append(cp)
        for cp in cps:
            cp.wait()
        mine.wait()

    return pl.pallas_call(
        body, name="gather_small_grads", in_specs=[ANY], out_specs=ANY,
        out_shape=jax.ShapeDtypeStruct((8, R, Cc), F32),
        scratch_shapes=[pltpu.SemaphoreType.DMA((7,)), pltpu.SemaphoreType.DMA((7,)), pltpu.SemaphoreType.DMA],
    )(packed)


def _sum_slots(name, slots):
    n, R, Cc = slots.shape

    def body(s_ref, o_ref):
        t = s_ref[0]
        for i in range(1, n):
            t = t + s_ref[i]
        o_ref[...] = t

    return pl.pallas_call(
        body, name=name, grid=(1,), in_specs=[pl.BlockSpec((n, R, Cc), lambda i: (0, 0, 0))],
        out_specs=pl.BlockSpec((R, Cc), lambda i: (0, 0)), out_shape=jax.ShapeDtypeStruct((R, Cc), F32),
        compiler_params=_params(1),
    )(slots)


def _pair_sum(name, pos, g, land, shape, axis):
    hshape = _half_shape(shape, axis)
    R, Cc = hshape
    tr = _pick(R, (256, 128, 64, 32, 16))
    nrb = R // tr

    def body(pos_ref, g_ref, l_ref, o_ref):
        o_ref[...] = (g_ref[...].astype(F32) + l_ref[...].astype(F32)).astype(BF16)

    if axis == 1:
        gmap = lambda i, p: (p[1] * nrb + i, 0)
    else:
        gmap = lambda i, p: (i, p[1])
    blk = pl.BlockSpec((tr, Cc), lambda i, p: (i, 0))
    return pl.pallas_call(
        body, name=name,
        grid_spec=pltpu.PrefetchScalarGridSpec(
            num_scalar_prefetch=1, grid=(nrb,), in_specs=[pl.BlockSpec((tr, Cc), gmap), blk], out_specs=blk),
        out_shape=jax.ShapeDtypeStruct(hshape, BF16), compiler_params=_params(1),
    )(pos, g, land)


def _chip_sum(name, pos, sb, land, shape, axis):
    hshape = _half_shape(shape, axis)
    pshape = _piece_shape(shape, axis)
    R, Cc = pshape
    tr = _pick(R, (256, 128, 64, 32, 16))
    nrb = R // tr

    def body(pos_ref, s_ref, l_ref, o_ref):
        t = s_ref[...].astype(F32)
        for k in range(3):
            t = t + l_ref[k].astype(F32)
        o_ref[...] = t

    if axis == 1:
        smap = lambda i, p: (i, p[0])
    else:
        smap = lambda i, p: (p[0] * nrb + i, 0)
    return pl.pallas_call(
        body, name=name,
        grid_spec=pltpu.PrefetchScalarGridSpec(
            num_scalar_prefetch=1, grid=(nrb,),
            in_specs=[pl.BlockSpec((tr, Cc), smap), pl.BlockSpec((3, tr, Cc), lambda i, p: (0, i, 0))],
            out_specs=pl.BlockSpec((tr, Cc), lambda i, p: (i, 0))),
        out_shape=jax.ShapeDtypeStruct(pshape, F32), compiler_params=_params(1),
    )(pos, sb, land)


def _adam_math(w, g, m, v):
    m = ADAM_B1 * m + (1.0 - ADAM_B1) * g
    v = ADAM_B2 * v + (1.0 - ADAM_B2) * (g * g)
    m_hat = m / (1.0 - ADAM_B1 ** ADAM_STEP)
    v_hat = v / (1.0 - ADAM_B2 ** ADAM_STEP)
    delta = -ADAM_LR * (m_hat / (jnp.sqrt(v_hat) + ADAM_EPS) + ADAM_WD * w)
    return delta, m, v


def _adamw_halves(name, pos, w, m, v, mine, theirs, axis):
    R, Cc = w.shape
    hr, hc = mine.shape
    tr = _pick(hr, (256, 128, 64, 32, 16))
    nrb = hr // tr

    def body(pos_ref, w_ref, m_ref, v_ref, a_ref, b_ref, g_ref, d_ref, nm_ref, nv_ref):
        half = pl.program_id(0)
        g = jnp.where(half == pos_ref[1], a_ref[...], b_ref[...])
        d, nm, nv = _adam_math(w_ref[...], g, m_ref[...], v_ref[...])
        g_ref[...] = g
        d_ref[...] = d
        nm_ref[...] = nm
        nv_ref[...] = nv

    if axis == 1:
        wmap = lambda h, i, p: (h * nrb + i, 0)
    else:
        wmap = lambda h, i, p: (i, h)
    wblk = pl.BlockSpec((tr, hc), wmap)
    ablk = pl.BlockSpec((tr, hc), lambda h, i, p: (jnp.where(h == p[1], i, 0), 0))
    bblk = pl.BlockSpec((tr, hc), lambda h, i, p: (jnp.where(h == p[1], 0, i), 0))
    return pl.pallas_call(
        body, name=name,
        grid_spec=pltpu.PrefetchScalarGridSpec(
            num_scalar_prefetch=1, grid=(2, nrb), in_specs=[wblk, wblk, wblk, ablk, bblk], out_specs=[wblk] * 4),
        out_shape=[jax.ShapeDtypeStruct((R, Cc), F32)] * 4, compiler_params=_params(2),
    )(pos, w, m, v, mine, theirs)


def _adamw_small(name, w, g, m, v):
    def body(w_ref, g_ref, m_ref, v_ref, d_ref, nm_ref, nv_ref):
        d, nm, nv = _adam_math(w_ref[...], g_ref[...], m_ref[...], v_ref[...])
        d_ref[...] = d
        nm_ref[...] = nm
        nv_ref[...] = nv

    blk = pl.BlockSpec(w.shape, lambda i: (0, 0))
    return pl.pallas_call(
        body, name=name, grid=(1,), in_specs=[blk] * 4, out_specs=[blk] * 3,
        out_shape=[jax.ShapeDtypeStruct(w.shape, F32)] * 3, compiler_params=_params(1),
    )(w, g, m, v)


SMALL = ("ffn1_norm_g", "mix_norm_g", "conv_b_dw", "conv_ln_g", "conv_ln_b", "q_norm_g", "k_norm_g", "ffn2_norm_g")
ORDER = ("ffn1_norm_g", "ffn1_w_gate", "ffn1_w_up", "ffn1_w_down", "mix_norm_g", "w_in", "conv_w_dw", "conv_b_dw",
         "conv_ln_g", "conv_ln_b", "q_norm_g", "k_norm_g", "w_out", "ffn2_norm_g", "ffn2_w_gate", "ffn2_w_up",
         "ffn2_w_down")
GATHER_FIRST = ("ffn1_w_gate", "ffn1_w_up")
GATHER_SECOND = ("ffn1_w_down", "w_in")
GATHER_THIRD = ("w_out", "ffn2_w_gate", "ffn2_w_up", "ffn2_w_down")


class _Exchange:
    def __init__(self, P, Mo, Vo, conv_shard, pos):
        self.P, self.Mo, self.Vo, self.pos = P, Mo, Vo, pos
        self.tokens = []
        self.pending = {}
        self.reducing = {}
        self.results = {}
        placed = {n: _place("place_" + n, pos, P[n][0], a) for n, a in BIG}
        self.shapes = {n: placed[n].shape for n, _ in BIG}
        first, tok = _gather_now("gather_first", [AXIS[n] for n in GATHER_FIRST], [placed[n] for n in GATHER_FIRST])
        self.ready = dict(zip(GATHER_FIRST, first))
        cq = conv_shard.shape[1]
        conv_full = lax.dynamic_update_slice(jnp.zeros((conv_shard.shape[0], 4 * cq), F32), conv_shard,
                                             (0, pos[0] * cq))
        for gname, names, conv in (("second", GATHER_SECOND, conv_full), ("third", GATHER_THIRD, None)):
            axes = [AXIS[n] for n in names]
            shapes = [self.shapes[n] for n in names]
            arrays = [placed[n] for n in names] + ([conv] if conv is not None else [])
            small = min(range(len(arrays)), key=lambda i: arrays[i].size)
            arrays[small] = arrays[small] + tok[0, 0].astype(arrays[small].dtype)
            plan = _gather_plan(axes, shapes, conv.shape if conv is not None else None)
            sems, thru, tok = _split_start("gather_%s_start" % gname, arrays, 3 * len(arrays), plan)
            self.tokens.append(tok)
            for n in names + (("conv_w32",) if conv is not None else ()):
                self.pending[n] = (gname, names, axes, plan, sems, thru, conv is not None)

    def tie(self, v):
        for tok in self.tokens:
            v = v + tok[0:1, 0:1]
        self.tokens = []
        return v

    def weights(self, names, after):
        if names[0] in self.pending:
            gname, gnames, axes, plan, sems, thru, has_conv = self.pending[names[0]]
            thru = _split_wait("gather_%s_wait" % gname, thru, sems, after, plan)
            nt = len(gnames)
            fulls = _gather_finish("gather_%s_finish" % gname, axes, thru[:nt])
            for n, f in zip(gnames, fulls):
                self.ready[n] = f
                del self.pending[n]
            if has_conv:
                self.ready["conv_w32"] = thru[nt]
                del self.pending["conv_w32"]
        return [self.ready[n] for n in names]

    def reduce_begin(self, gname, grads, split=False):
        names = list(grads)
        axes = [AXIS[n] for n in names]
        shapes = [self.shapes[n] for n in names]
        gs = [grads[n] for n in names]
        nt = len(names)
        if not split:
            to_sibling = [(lambda ref, c, s=s, a=a: _window(ref, s, a, half=1 - c)) for s, a in zip(shapes, axes)]
            landed = _pair_exchange("pair_exchange_" + gname, gs, to_sibling,
                                    [_half_shape(s, a) for s, a in zip(shapes, axes)], BF16)
            self._scatter(gname, names, axes, shapes, gs, landed)
            return
        plan = _pair_plan(axes, shapes)
        arrays = gs + [lax.empty(_half_shape(s, a), BF16) for s, a in zip(shapes, axes)]
        sems, thru, tok = _split_start("pair_%s_start" % gname, arrays, nt, plan)
        self.tokens.append(tok)
        self.reducing[gname] = (names, axes, shapes, plan, sems, thru)

    def reduce_scatter(self, gname, after):
        names, axes, shapes, plan, sems, thru = self.reducing.pop(gname)
        nt = len(names)
        thru = _split_wait("pair_%s_wait" % gname, thru, sems, after, plan)
        self._scatter(gname, names, axes, shapes, thru[:nt], thru[nt:])

    def _scatter(self, gname, names, axes, shapes, gs, landed):
        sbs = [_pair_sum("pair_sum_" + n, self.pos, g, l, s, a)
               for n, a, g, l, s in zip(names, axes, gs, landed, shapes)]
        lands = [lax.empty((3,) + _piece_shape(s, a), BF16) for s, a in zip(shapes, axes)]
        plan = _scatter_plan(axes, shapes)
        sems, thru, tok = _split_start("scatter_%s_start" % gname, sbs + lands, 3 * len(names), plan)
        self.tokens.append(tok)
        self.reducing[gname] = (names, axes, shapes, plan, sems, thru)

    def reduce_end(self, gname, after):
        names, axes, shapes, plan, sems, thru = self.reducing.pop(gname)
        nt = len(names)
        thru = _split_wait("scatter_%s_wait" % gname, thru, sems, after, plan)
        mine = [_chip_sum("chip_sum_" + n, self.pos, sb, l, s, a)
                for n, a, sb, l, s in zip(names, axes, thru[:nt], thru[nt:], shapes)]
        theirs = _pair_exchange("half_exchange_" + gname, mine, [(lambda ref, c: ref)] * nt,
                                [m.shape for m in mine], F32)
        for n, a, mi, th in zip(names, axes, mine, theirs):
            g, d, nm, nv = _adamw_halves("adamw_" + n, self.pos, self.P[n][0], self.Mo[n][0], self.Vo[n][0],
                                         mi, th, a)
            self.results[n] = (g[None], d[None], nm[None], nv[None])


def kernel(x, ffn1_norm_g, ffn1_w_gate, ffn1_w_up, ffn1_w_down, mix_norm_g, w_in, conv_w_dw, conv_b_dw, conv_ln_g, conv_ln_b, q_norm_g, k_norm_g, w_out, ffn2_norm_g, ffn2_w_gate, ffn2_w_up, ffn2_w_down, loss_target, m_ffn1_norm_g, m_ffn1_w_gate, m_ffn1_w_up, m_ffn1_w_down, m_mix_norm_g, m_w_in, m_conv_w_dw, m_conv_b_dw, m_conv_ln_g, m_conv_ln_b, m_q_norm_g, m_k_norm_g, m_w_out, m_ffn2_norm_g, m_ffn2_w_gate, m_ffn2_w_up, m_ffn2_w_down, v_ffn1_norm_g, v_ffn1_w_gate, v_ffn1_w_up, v_ffn1_w_down, v_mix_norm_g, v_w_in, v_conv_w_dw, v_conv_b_dw, v_conv_ln_g, v_conv_ln_b, v_q_norm_g, v_k_norm_g, v_w_out, v_ffn2_norm_g, v_ffn2_w_gate, v_ffn2_w_up, v_ffn2_w_down):
    args = dict(locals())
    P = {n: args[n] for n in ORDER}
    Mo = {n: args["m_" + n] for n in ORDER}
    Vo = {n: args["v_" + n] for n in ORDER}
    xs = x[0]
    tgt = loss_target[0]
    T, D = xs.shape
    hd = q_norm_g.shape[-1]
    C = conv_b_dw.shape[-1]
    ntap = conv_w_dw.shape[1]
    cx, cy, cc = _position()
    j0 = 2 * cx + cy
    pos = jnp.stack([j0, cc]).astype(jnp.int32)

    conv_shard = jnp.pad(conv_w_dw[0], ((0, HALO - ntap), (0, 0)))
    comm = _Exchange(P, Mo, Vo, conv_shard, pos)
    lossvec, dx0, G = _local_step(xs, tgt, {n: P[n] for n in SMALL}, comm, hd)
    loss = lax.psum(0.5 / D * jnp.sum(lossvec), AXES)
    grads, deltas, new_m, new_v = {}, {}, {}, {}
    for n, _ in BIG:
        grads[n], deltas[n], new_m[n], new_v[n] = comm.results[n]

    rows = [G["conv_w32"]]
    for n in ("ffn1_norm_g", "mix_norm_g", "ffn2_norm_g"):
        rows.append(G[n].reshape(D // C, C))
    for n in ("conv_b_dw", "conv_ln_g", "conv_ln_b", "q_norm_g", "k_norm_g"):
        rows.append(G[n])
    packed = jnp.concatenate(rows, axis=0)
    packed = jnp.pad(packed, ((0, -packed.shape[0] % 8), (0, 0)))
    total = _sum_slots("sum_small_grads", _gather_small(packed))
    r = HALO
    small_g = {}
    cq = C // 4
    small_g["conv_w_dw"] = lax.dynamic_slice(total[:ntap], (0, j0 * cq), (ntap, cq))
    for n in ("ffn1_norm_g", "mix_norm_g", "ffn2_norm_g"):
        small_g[n] = total[r:r + D // C].reshape(1, D)
        r += D // C
    for n in ("conv_b_dw", "conv_ln_g", "conv_ln_b"):
        small_g[n] = total[r:r + 1]
        r += 1
    for n in ("q_norm_g", "k_norm_g"):
        small_g[n] = total[r:r + 1, :hd]
        r += 1
    for n in ("conv_w_dw",) + SMALL:
        lead = n == "conv_w_dw"
        w2, m2, v2 = (P[n][0], Mo[n][0], Vo[n][0]) if lead else (P[n], Mo[n], Vo[n])
        d, nm, nv = _adamw_small("adamw_" + n, w2, small_g[n], m2, v2)
        if lead:
            grads[n], deltas[n], new_m[n], new_v[n] = small_g[n][None], d[None], nm[None], nv[None]
        else:
            grads[n], deltas[n], new_m[n], new_v[n] = small_g[n], d, nm, nv

    return (loss, dx0[None], *[grads[n] for n in ORDER], *[deltas[n] for n in ORDER],
            *[new_m[n] for n in ORDER], *[new_v[n] for n in ORDER])
```

```python
import jax
import jax.numpy as jnp
from jax import lax
from jax.experimental import pallas as pl
from jax.experimental.pallas import tpu as pltpu

F32 = jnp.float32
BF16 = jnp.bfloat16
EPS = 1e-6
WINDOW = 128
DILATIONS = (1, 4, 16)
ALIBI_MAX_BIAS = 8.0
LANES = 128
HALO = 32
ADAM_LR, ADAM_B1, ADAM_B2, ADAM_EPS, ADAM_WD, ADAM_STEP = 0.001, 0.9, 0.999, 1e-08, 0.01, 10
VMEM_LIMIT_MB = 62
ROW_TILE = 1024
TN_ACC_ELEMS = 3 * 1024 * 1024
EPILOGUE_ROWS = 256
ACC_COLS = 512
MESH = pl.DeviceIdType.MESH
ANY = pl.BlockSpec(memory_space=pl.ANY)
AXES = ("x", "y", "c")
NEG = -1e30


def _pick(n, cands):
    for c in cands:
        if n % c == 0:
            return c
    return n


def _params(nsem):
    return pltpu.CompilerParams(dimension_semantics=("arbitrary",) * nsem,
                                vmem_limit_bytes=VMEM_LIMIT_MB << 20)


def _nn(a, b):
    return jnp.dot(a, b, preferred_element_type=F32)


def _nt(a, b):
    return lax.dot_general(a, b, (((1,), (1,)), ((), ())), preferred_element_type=F32)


def _tn(a, b):
    return lax.dot_general(a, b, (((0,), (0,)), ((), ())), preferred_element_type=F32)


def _sigmoid(v):
    return jax.nn.sigmoid(v)


def _rms_r(xv):
    return lax.rsqrt(jnp.mean(xv * xv, axis=-1, keepdims=True) + EPS)


def _norm_matmul(name, x, g, ws, swiglu):
    T, D = x.shape
    N = ws[0].shape[1]
    tm = _pick(T, (ROW_TILE, 512, 256, 128))
    tn = _pick(N, (512, 256, 128))
    nw = len(ws)

    def body(*refs):
        x_ref, g_ref = refs[:2]
        w_refs = refs[2:2 + nw]
        outs = refs[2 + nw:-1]
        hs = refs[-1]

        @pl.when(pl.program_id(1) == 0)
        def _():
            for r0 in range(0, tm, EPILOGUE_ROWS):
                rows = slice(r0, r0 + min(EPILOGUE_ROWS, tm))
                xv = x_ref[rows, :]
                hv = (xv * _rms_r(xv) * g_ref[...]).astype(BF16)
                hs[rows, :] = hv
                outs[0][rows, :] = hv

        h = hs[...]
        if swiglu:
            gt = _nn(h, w_refs[0][...])
            u = _nn(h, w_refs[1][...])
            sg = _sigmoid(gt)
            silu = gt * sg
            outs[1][...] = (u * (sg * (1.0 + gt * (1.0 - sg)))).astype(BF16)
            outs[2][...] = silu.astype(BF16)
            outs[3][...] = (silu * u).astype(BF16)
        else:
            outs[1][...] = _nn(h, w_refs[0][...])

    row = pl.BlockSpec((tm, D), lambda i, j: (i, 0))
    col = pl.BlockSpec((D, tn), lambda i, j: (0, j))
    tile = pl.BlockSpec((tm, tn), lambda i, j: (i, j))
    if swiglu:
        out_shape = [jax.ShapeDtypeStruct((T, D), BF16)] + [jax.ShapeDtypeStruct((T, N), BF16)] * 3
        out_specs = [row, tile, tile, tile]
    else:
        out_shape = [jax.ShapeDtypeStruct((T, D), BF16), jax.ShapeDtypeStruct((T, N), F32)]
        out_specs = [row, tile]
    return pl.pallas_call(
        body, name=name, grid=(T // tm, N // tn),
        in_specs=[row, pl.BlockSpec((1, D), lambda i, j: (0, 0))] + [col] * nw,
        out_specs=out_specs, out_shape=out_shape,
        scratch_shapes=[pltpu.VMEM((tm, D), BF16)],
        compiler_params=_params(2),
    )(x, g, *ws)


def _matmul_res(name, a, w, res, scale, tgt=None):
    T, K = a.shape
    N = w.shape[1]
    loss = tgt is not None
    tm = _pick(T, (512, 256, 128))
    tk = _pick(K, (1408, 1024, 512, 256, 128))
    nk = K // tk

    def body(*refs):
        if loss:
            a_ref, w_ref, res_ref, tgt_ref, dx_ref, dxb_ref, lv_ref, acc = refs
        else:
            a_ref, w_ref, res_ref, out_ref, acc = refs
        i, k = pl.program_id(0), pl.program_id(1)

        @pl.when(k == 0)
        def _():
            acc[...] = jnp.zeros_like(acc)

        acc[...] += _nn(a_ref[...], w_ref[...])

        @pl.when(k == nk - 1)
        def _():
            part = jnp.zeros((1, N), F32)
            for r0 in range(0, tm, EPILOGUE_ROWS):
                rows = slice(r0, r0 + min(EPILOGUE_ROWS, tm))
                val = res_ref[rows, :] + scale * acc[rows, :]
                if loss:
                    dv = val - tgt_ref[rows, :]
                    dx = dv * (1.0 / N)
                    dx_ref[rows, :] = dx
                    dxb_ref[rows, :] = dx.astype(BF16)
                    part = part + jnp.sum(dv * dv, axis=0, keepdims=True)
                else:
                    out_ref[rows, :] = val
            if loss:
                @pl.when(i == 0)
                def _():
                    lv_ref[...] = part

                @pl.when(i > 0)
                def _():
                    lv_ref[...] += part

    row = pl.BlockSpec((tm, N), lambda i, k: (i, 0))
    in_specs = [pl.BlockSpec((tm, tk), lambda i, k: (i, k)), pl.BlockSpec((tk, N), lambda i, k: (k, 0)), row]
    args = [a, w, res]
    if loss:
        in_specs.append(row)
        args.append(tgt)
        out_specs = [row, row, pl.BlockSpec((1, N), lambda i, k: (0, 0))]
        out_shape = [jax.ShapeDtypeStruct((T, N), F32), jax.ShapeDtypeStruct((T, N), BF16),
                     jax.ShapeDtypeStruct((1, N), F32)]
    else:
        out_specs = row
        out_shape = jax.ShapeDtypeStruct((T, N), F32)
    return pl.pallas_call(
        body, name=name, grid=(T // tm, nk), in_specs=in_specs, out_specs=out_specs, out_shape=out_shape,
        scratch_shapes=[pltpu.VMEM((tm, N), F32)], compiler_params=_params(2),
    )(*args)


def _nt_matmul(name, dyb, w, scale=1.0, gate=None, up=None):
    T, D = dyb.shape
    N = w.shape[0]
    tm = _pick(T, (ROW_TILE, 512, 256, 128))
    tn = _pick(N, (512, 256, 128))
    swiglu = gate is not None

    def body(*refs):
        if swiglu:
            dy_ref, w_ref, g_ref, u_ref, dg_ref, du_ref = refs
        else:
            dy_ref, w_ref, o_ref, ob_ref = refs
        da = _nt(dy_ref[...], w_ref[...]) * scale
        if swiglu:
            dg_ref[...] = (da * g_ref[...].astype(F32)).astype(BF16)
            du_ref[...] = (da * u_ref[...].astype(F32)).astype(BF16)
        else:
            o_ref[...] = da
            ob_ref[...] = da.astype(BF16)

    tile = pl.BlockSpec((tm, tn), lambda i, j: (i, j))
    in_specs = [pl.BlockSpec((tm, D), lambda i, j: (i, 0)), pl.BlockSpec((tn, D), lambda i, j: (j, 0))]
    args = [dyb, w]
    if swiglu:
        in_specs += [tile, tile]
        args += [gate, up]
        out_shape = [jax.ShapeDtypeStruct((T, N), BF16)] * 2
    else:
        out_shape = [jax.ShapeDtypeStruct((T, N), F32), jax.ShapeDtypeStruct((T, N), BF16)]
    return pl.pallas_call(
        body, name=name, grid=(T // tm, N // tn), in_specs=in_specs, out_specs=[tile, tile],
        out_shape=out_shape, compiler_params=_params(2),
    )(*args)


def _nt_rms_bwd(name, As, Ws, x, g, dres):
    T, K = As[0].shape
    D = x.shape[1]
    na = len(As)
    tm = _pick(T, (ROW_TILE, 512, 256, 128))
    tk = _pick(K, (1024 // na, 512, 256, 128))
    nk = K // tk

    def body(*refs):
        a_refs = refs[:na]
        w_refs = refs[na:2 * na]
        x_hbm, g_ref, dres_hbm, acc, dxb_ref, dg_ref, x_ref, dres_ref, sems = refs[2 * na:]
        dx_ref = acc
        i, k = pl.program_id(0), pl.program_id(1)

        def row_copies():
            rows = pl.ds(pl.multiple_of(i * tm, tm), tm)
            return (pltpu.make_async_copy(x_hbm.at[rows, :], x_ref, sems.at[0]),
                    pltpu.make_async_copy(dres_hbm.at[rows, :], dres_ref, sems.at[1]))

        @pl.when(k == 0)
        def _():
            acc[...] = jnp.zeros_like(acc)
            for cp in row_copies():
                cp.start()

        for c0 in range(0, D, ACC_COLS):
            cols = slice(c0, min(c0 + ACC_COLS, D))
            part = _nt(a_refs[0][...], w_refs[0][cols, :])
            for a_ref, w_ref in zip(a_refs[1:], w_refs[1:]):
                part = part + _nt(a_ref[...], w_ref[cols, :])
            acc[:, cols] += part

        @pl.when(k == nk - 1)
        def _():
            for cp in row_copies():
                cp.wait()
            part = jnp.zeros((1, D), F32)
            for r0 in range(0, tm, EPILOGUE_ROWS):
                rows = slice(r0, r0 + min(EPILOGUE_ROWS, tm))
                dh = acc[rows, :]
                xv = x_ref[rows, :]
                r = _rms_r(xv)
                gd = dh * g_ref[...]
                dx = dres_ref[rows, :] + r * gd - xv * (r * r * r) * jnp.mean(gd * xv, axis=-1, keepdims=True)
                dx_ref[rows, :] = dx
                dxb_ref[rows, :] = dx.astype(BF16)
                part = part + jnp.sum(dh * xv * r, axis=0, keepdims=True)

            @pl.when(i == 0)
            def _():
                dg_ref[...] = part

            @pl.when(i > 0)
            def _():
                dg_ref[...] += part

    row = pl.BlockSpec((tm, D), lambda i, k: (i, 0), pipeline_mode=pl.Buffered(1))
    vec = pl.BlockSpec((1, D), lambda i, k: (0, 0))
    return pl.pallas_call(
        body, name=name, grid=(T // tm, nk),
        in_specs=[pl.BlockSpec((tm, tk), lambda i, k: (i, k))] * na
        + [pl.BlockSpec((D, tk), lambda i, k: (0, k))] * na + [ANY, vec, ANY],
        out_specs=[row, row, vec],
        out_shape=[jax.ShapeDtypeStruct((T, D), F32), jax.ShapeDtypeStruct((T, D), BF16),
                   jax.ShapeDtypeStruct((1, D), F32)],
        scratch_shapes=[pltpu.VMEM((tm, D), F32), pltpu.VMEM((tm, D), F32), pltpu.SemaphoreType.DMA((2,))],
        compiler_params=_params(2),
    )(*As, *Ws, x, g, dres)


def _tn_matmul(name, a, b, scale=1.0):
    T, M = a.shape
    N = b.shape[1]
    tn = _pick(N, (2048, 1408, 1280, 1024, 512, 256, 128))
    tm = _pick(M, tuple(c for c in (2048, 1408, 1024, 512, 256, 128) if c * tn <= TN_ACC_ELEMS))
    tk = _pick(T, (1024, 512, 256, 128))
    nk = T // tk

    def body(a_ref, b_ref, o_ref, acc):
        k = pl.program_id(2)

        @pl.when(k == 0)
        def _():
            acc[...] = jnp.zeros_like(acc)

        for r0 in range(0, tm, ACC_COLS):
            rows = slice(r0, min(r0 + ACC_COLS, tm))
            acc[rows, :] += _tn(a_ref[:, rows], b_ref[...])

        @pl.when(k == nk - 1)
        def _():
            o_ref[...] = (acc[...] * scale).astype(BF16)

    return pl.pallas_call(
        body, name=name, grid=(M // tm, N // tn, nk),
        in_specs=[pl.BlockSpec((tk, tm), lambda i, j, k: (k, i)), pl.BlockSpec((tk, tn), lambda i, j, k: (k, j))],
        out_specs=pl.BlockSpec((tm, tn), lambda i, j, k: (i, j)),
        out_shape=jax.ShapeDtypeStruct((M, N), BF16),
        scratch_shapes=[pltpu.VMEM((tm, tn), F32)], compiler_params=_params(3),
    )(a, b)


CONV_ROWS = 128
ROW_CHUNK = 32
LANE_CHUNK = 256


SUBLANES = 8


def _fill_shifts(sh, buf, rows):
    for p in range(1, SUBLANES):
        sh[p, 0:rows - SUBLANES, :] = buf[p:p + rows - SUBLANES, :]


def _tap(buf, sh, s, n, cols):
    p = s % SUBLANES
    return buf[s:s + n, cols] if p == 0 else sh[p, s - p:s - p + n, cols]


def _conv_fwd(name, z, w32, b, lg, lb, C):
    T = z.shape[0]
    tc = CONV_ROWS
    ntap = 31
    lc = _pick(C, (LANE_CHUNK, LANES))
    rpb = tc // HALO

    def body(zc_ref, zp_ref, w_ref, b_ref, lg_ref, lb_ref, yc_ref, ycv_ref, vbuf, ybuf, vsh):
        i = pl.program_id(0)
        zc = zc_ref[...]
        zp = zp_ref[...]
        vbuf[HALO:HALO + tc, :] = zc[:, :C] * _sigmoid(zc[:, C:])
        vbuf[0:HALO, :] = jnp.where(i > 0, zp[:, :C] * _sigmoid(zp[:, C:]), 0.0)
        _fill_shifts(vsh, vbuf, tc + HALO)
        for r0 in range(0, tc, ROW_CHUNK):
            for c0 in range(0, C, lc):
                cols = slice(c0, c0 + lc)
                acc = jnp.zeros((ROW_CHUNK, lc), F32) + b_ref[:, cols]
                for k in range(ntap):
                    acc = acc + w_ref[k:k + 1, cols] * _tap(vbuf, vsh, r0 + 2 + k, ROW_CHUNK, cols)
                ybuf[r0:r0 + ROW_CHUNK, cols] = acc
        y = ybuf[...]
        ycv_ref[...] = y
        mu = jnp.mean(y, axis=-1, keepdims=True)
        yc = y - mu
        rstd = lax.rsqrt(jnp.mean(yc * yc, axis=-1, keepdims=True) + EPS)
        ln = yc * rstd * lg_ref[...] + lb_ref[...]
        yc_ref[...] = (ln * _sigmoid(ln)).astype(BF16)

    vec = pl.BlockSpec((1, C), lambda i: (0, 0))
    return pl.pallas_call(
        body, name=name, grid=(T // tc,),
        in_specs=[pl.BlockSpec((tc, 2 * C), lambda i: (i, 0)),
                  pl.BlockSpec((HALO, 2 * C), lambda i: (jnp.maximum(i * rpb - 1, 0), 0)),
                  pl.BlockSpec((HALO, C), lambda i: (0, 0)), vec, vec, vec],
        out_specs=[pl.BlockSpec((tc, C), lambda i: (i, 0))] * 2,
        out_shape=[jax.ShapeDtypeStruct((T, 2 * C), BF16), jax.ShapeDtypeStruct((T, C), F32)],
        scratch_shapes=[pltpu.VMEM((tc + HALO, C), F32), pltpu.VMEM((tc, C), F32),
                        pltpu.VMEM((SUBLANES, tc + HALO, C), F32)],
        compiler_params=_params(1),
    )(z, z, w32, b, lg, lb)


def _conv_bwd(name, z, ycv, dycat, w32, lg, lb, C):
    T = z.shape[0]
    tc = CONV_ROWS
    ntap = 31
    lc = _pick(C, (LANE_CHUNK, LANES))
    rpb = tc // HALO
    nstep = T // tc
    nhb = T // HALO

    def ln_bwd(dyc, y, lgv, lbv):
        mu = jnp.mean(y, axis=-1, keepdims=True)
        yc = y - mu
        rstd = lax.rsqrt(jnp.mean(yc * yc, axis=-1, keepdims=True) + EPS)
        yn = yc * rstd
        ln = yn * lgv + lbv
        sg = _sigmoid(ln)
        dln = dyc * (sg * (1.0 + ln * (1.0 - sg)))
        dyn = dln * lgv
        dy = rstd * (dyn - jnp.mean(dyn, axis=-1, keepdims=True)
                     - yn * jnp.mean(dyn * yn, axis=-1, keepdims=True))
        return dy, dln, yn

    def body(zc_ref, zp_ref, y_ref, yn_ref, d_ref, dn_ref, w_ref, lg_ref, lb_ref,
             dz_ref, dw_ref, db_ref, dlg_ref, dlb_ref, vbuf, dbuf, dvbuf, dwacc, vsh, dsh):
        i = pl.program_id(0)
        lgv, lbv = lg_ref[...], lb_ref[...]
        zc = zc_ref[...]
        zp = zp_ref[...]
        a = zc[:, :C]
        sgt = _sigmoid(zc[:, C:])
        vbuf[HALO:HALO + tc, :] = a * sgt
        vbuf[0:HALO, :] = jnp.where(i > 0, zp[:, :C] * _sigmoid(zp[:, C:]), 0.0)
        dy, dln, yn = ln_bwd(d_ref[...], y_ref[...], lgv, lbv)
        dbuf[0:tc, :] = dy
        dyn_, _, _ = ln_bwd(dn_ref[...], yn_ref[...], lgv, lbv)
        dbuf[tc:tc + HALO, :] = jnp.where(i < nstep - 1, dyn_, 0.0)

        @pl.when(i == 0)
        def _():
            dwacc[...] = jnp.zeros_like(dwacc)
            db_ref[...] = jnp.zeros_like(db_ref)
            dlg_ref[...] = jnp.zeros_like(dlg_ref)
            dlb_ref[...] = jnp.zeros_like(dlb_ref)

        db_ref[...] += jnp.sum(dy, axis=0, keepdims=True)
        dlg_ref[...] += jnp.sum(dln * yn, axis=0, keepdims=True)
        dlb_ref[...] += jnp.sum(dln, axis=0, keepdims=True)

        _fill_shifts(vsh, vbuf, tc + HALO)
        _fill_shifts(dsh, dbuf, tc + HALO)
        for r0 in range(0, tc, ROW_CHUNK):
            for c0 in range(0, C, lc):
                cols = slice(c0, c0 + lc)
                dcur = dbuf[r0:r0 + ROW_CHUNK, cols]
                acc = jnp.zeros((ROW_CHUNK, lc), F32)
                for k in range(ntap):
                    acc = acc + w_ref[k:k + 1, cols] * _tap(dbuf, dsh, r0 + 30 - k, ROW_CHUNK, cols)
                    prod = dcur * _tap(vbuf, vsh, r0 + 2 + k, ROW_CHUNK, cols)
                    red = prod[0:8]
                    for q in range(8, ROW_CHUNK, 8):
                        red = red + prod[q:q + 8]
                    dwacc[8 * k:8 * k + 8, cols] += red
                dvbuf[r0:r0 + ROW_CHUNK, cols] = acc
        dv = dvbuf[...]
        dz_ref[:, :C] = (dv * sgt).astype(BF16)
        dz_ref[:, C:] = (dv * a * sgt * (1.0 - sgt)).astype(BF16)

        @pl.when(i == nstep - 1)
        def _():
            for k in range(ntap):
                dw_ref[k:k + 1, :] = jnp.sum(dwacc[8 * k:8 * k + 8, :], axis=0, keepdims=True)
            dw_ref[ntap:HALO, :] = jnp.zeros((HALO - ntap, C), F32)

    vec = pl.BlockSpec((1, C), lambda i: (0, 0))
    cur = pl.BlockSpec((tc, C), lambda i: (i, 0))
    nxt = pl.BlockSpec((HALO, C), lambda i: (jnp.minimum((i + 1) * rpb, nhb - 1), 0))
    return pl.pallas_call(
        body, name=name, grid=(nstep,),
        in_specs=[pl.BlockSpec((tc, 2 * C), lambda i: (i, 0)),
                  pl.BlockSpec((HALO, 2 * C), lambda i: (jnp.maximum(i * rpb - 1, 0), 0)),
                  cur, nxt, cur, nxt, pl.BlockSpec((HALO, C), lambda i: (0, 0)), vec, vec],
        out_specs=[pl.BlockSpec((tc, 2 * C), lambda i: (i, 0)), pl.BlockSpec((HALO, C), lambda i: (0, 0)),
                   vec, vec, vec],
        out_shape=[jax.ShapeDtypeStruct((T, 2 * C), BF16), jax.ShapeDtypeStruct((HALO, C), F32),
                   jax.ShapeDtypeStruct((1, C), F32), jax.ShapeDtypeStruct((1, C), F32),
                   jax.ShapeDtypeStruct((1, C), F32)],
        scratch_shapes=[pltpu.VMEM((tc + HALO, C), F32), pltpu.VMEM((tc + HALO, C), F32),
                        pltpu.VMEM((tc, C), F32), pltpu.VMEM((8 * HALO, C), F32),
                        pltpu.VMEM((SUBLANES, tc + HALO, C), F32), pltpu.VMEM((SUBLANES, tc + HALO, C), F32)],
        compiler_params=_params(1),
    )(z, z, ycv, ycv, dycat, dycat, w32, lg, lb)


def _seg_sum(u, bmat):
    hi = u.astype(BF16)
    lo = (u - hi.astype(F32)).astype(BF16)
    parts = [_nn(hi[:, c:c + LANES], bmat) + _nn(lo[:, c:c + LANES], bmat) for c in range(0, u.shape[1], LANES)]
    return jnp.concatenate(parts, axis=1)


def _attn_prep(name, z, gq, gk, bmat, A, c0, hd):
    T = z.shape[0]
    tm = _pick(T, (256, 128))

    def body(zq_ref, zk_ref, zv_ref, gq_ref, gk_ref, b_ref, o_ref):
        bm = b_ref[...]
        for idx, (z_ref, g_ref) in enumerate(((zq_ref, gq_ref), (zk_ref, gk_ref))):
            zv = z_ref[...]
            r = lax.rsqrt(_seg_sum(zv * zv, bm) * (1.0 / hd) + EPS)
            o_ref[:, idx * A:(idx + 1) * A] = zv * r * g_ref[...]
        o_ref[:, 2 * A:] = zv_ref[...]

    vec = pl.BlockSpec((1, A), lambda i: (0, 0))
    return pl.pallas_call(
        body, name=name, grid=(T // tm,),
        in_specs=[pl.BlockSpec((tm, A), lambda i: (i, c0)), pl.BlockSpec((tm, A), lambda i: (i, c0 + 1)),
                  pl.BlockSpec((tm, A), lambda i: (i, c0 + 2)), vec, vec,
                  pl.BlockSpec((LANES, LANES), lambda i: (0, 0))],
        out_specs=pl.BlockSpec((tm, 3 * A), lambda i: (i, 0)),
        out_shape=jax.ShapeDtypeStruct((T, 3 * A), F32), compiler_params=_params(1),
    )(z, z, z, gq, gk, bmat)


QK_SCALE = 0.125
ATTN_UNROLL = 16
ATTN_FWD_UNROLL = 16


def _fill_bias(bias, sl_ref, hp, d):
    qi = lax.broadcasted_iota(jnp.int32, (WINDOW, 2 * WINDOW), 0)
    kj = lax.broadcasted_iota(jnp.int32, (WINDOW, 2 * WINDOW), 1)
    dist = WINDOW + qi - kj
    inband = (dist >= 0) & (dist <= WINDOW)
    distf = dist.astype(F32)
    for hh in range(2):
        b = jnp.where(inband, -(sl_ref[2 * hp + hh] * d) * distf, NEG)
        bias[2 * hh + 1] = b
        bias[2 * hh] = jnp.where(kj >= WINDOW, b, NEG)


CHUNK = WINDOW * DILATIONS[-1]
assert DILATIONS[0] == 1


def _deinterleave(dst, src, d, rows, dst_pitch, dst_off, src_off):
    for r in range(d):
        if d == 1:
            val = src[src_off:src_off + rows, :]
        else:
            val = src[pl.ds(src_off + r, rows, stride=d), :]
        lo = r * dst_pitch + dst_off
        dst[lo:lo + rows, :] = val.astype(dst.dtype)


def _interleave_add(dst, start, src, d, rows, src_pitch, src_off):
    for r in range(d):
        lo = r * src_pitch + src_off
        idx = pl.ds(start, rows) if d == 1 else pl.ds(start + r, rows, stride=d)
        dst[idx, :] += src[lo:lo + rows, :]


def _attn_fwd(name, qkv, slopes, ycat, A):
    T = qkv.shape[0]
    hpn, nch, nblk = A // LANES, T // CHUNK, CHUNK // WINDOW
    nbranch = len(DILATIONS)
    yoff = (ycat.shape[1] - A) // LANES

    def body(*refs):
        sl_ref, q_ref, k_ref, kp_ref, v_ref, vp_ref, _, y_ref, lg_ref, qd, kd, vd, od, ld, bias = refs[:15]
        onat, lnat = refs[15:15 + nbranch], refs[15 + nbranch:]
        hp, ch = pl.program_id(0), pl.program_id(1)
        lane = lax.broadcasted_iota(jnp.int32, (1, LANES), 1)
        first = lane < (LANES // 2)
        for bi, d in enumerate(DILATIONS):
            Ld = CHUNK // d
            seg = Ld + WINDOW
            nbr = Ld // WINDOW
            _deinterleave(qd, q_ref, d, Ld, Ld, 0, 0)
            for dst, cur, prev in ((kd, k_ref, kp_ref), (vd, v_ref, vp_ref)):
                _deinterleave(dst, prev, d, WINDOW, seg, 0, CHUNK - WINDOW * d)
                _deinterleave(dst, cur, d, Ld, seg, WINDOW, 0)
            _fill_bias(bias, sl_ref, hp, d)
            ob, lb = (onat[bi], lnat[bi]) if d == 1 else (od, ld)

            def step(it, carry, Ld=Ld, seg=seg, nbr=nbr, ob=ob, lb=lb):
                r, nl = it // nbr, it % nbr
                q0 = pl.multiple_of(r * Ld + nl * WINDOW, WINDOW)
                k0 = pl.multiple_of(r * seg + nl * WINDOW, WINDOW)
                later = jnp.where(ch * nbr + nl > 0, 1, 0)
                qb = qd[pl.ds(q0, WINDOW), :]
                k2 = kd[pl.ds(k0, 2 * WINDOW), :]
                v2 = vd[pl.ds(k0, 2 * WINDOW), :]
                res = []
                for hh in range(2):
                    mh = first if hh == 0 else jnp.logical_not(first)
                    s = _nt(jnp.where(mh, qb, jnp.zeros_like(qb)), k2) + bias[2 * hh + later]
                    mx = jnp.max(s, axis=-1, keepdims=True)
                    p = jnp.exp(s - mx)
                    den = jnp.sum(p, axis=-1, keepdims=True)
                    res.append((_nn(p.astype(BF16), v2) / den, mx + jnp.log(den)))
                ob[pl.ds(q0, WINDOW), :] = jnp.where(first, res[0][0], res[1][0])
                lb[pl.ds(q0, WINDOW), :] = jnp.where(first, res[0][1], res[1][1])
                return carry

            lax.fori_loop(0, nblk, step, 0, unroll=ATTN_FWD_UNROLL)
            if d > 1:
                for r in range(d):
                    onat[bi][pl.ds(r, Ld, stride=d), :] = od[r * Ld:(r + 1) * Ld, :]
                    lnat[bi][pl.ds(r, Ld, stride=d), :] = ld[r * Ld:(r + 1) * Ld, :]
        ls = [l[...] for l in lnat]
        mx = ls[0]
        for v in ls[1:]:
            mx = jnp.maximum(mx, v)
        es = [jnp.exp(v - mx) for v in ls]
        den = es[0]
        for e in es[1:]:
            den = den + e
        out = es[0] * onat[0][...]
        for e, o in zip(es[1:], onat[1:]):
            out = out + e * o[...]
        y_ref[...] = (out / den).astype(BF16)
        lg_ref[...] = mx + jnp.log(den)

    blk = lambda m: pl.BlockSpec((CHUNK, LANES), m)
    cur = lambda which: blk(lambda hp, ch: (ch, which * hpn + hp))
    prev = lambda which: blk(lambda hp, ch: (jnp.maximum(ch - 1, 0), which * hpn + hp))
    omap = blk(lambda hp, ch: (ch, hp))
    f32buf = pltpu.VMEM((CHUNK, LANES), F32)
    return pl.pallas_call(
        body, name=name, grid=(hpn, nch),
        in_specs=[pl.BlockSpec(memory_space=pltpu.SMEM), cur(0), cur(1), prev(1), cur(2), prev(2), ANY],
        out_specs=[blk(lambda hp, ch: (ch, yoff + hp)), omap],
        out_shape=[jax.ShapeDtypeStruct(ycat.shape, BF16), jax.ShapeDtypeStruct((T, A), F32)],
        input_output_aliases={6: 0},
        scratch_shapes=[pltpu.VMEM((CHUNK, LANES), BF16), pltpu.VMEM((2 * CHUNK, LANES), BF16),
                        pltpu.VMEM((2 * CHUNK, LANES), BF16), f32buf, f32buf,
                        pltpu.VMEM((4, WINDOW, 2 * WINDOW), F32)] + [f32buf] * (2 * nbranch),
        compiler_params=_params(2),
    )(slopes, qkv, qkv, qkv, qkv, qkv, ycat)


def _attn_bwd(name, qkv, dycat, ycat, lg, slopes, bmat, A, catoff):
    T = qkv.shape[0]
    hpn, nch, nblk = A // LANES, T // CHUNK, CHUNK // WINDOW
    co = catoff // LANES

    def body(sl_ref, q_ref, k_ref, kp_ref, v_ref, vp_ref, do_ref, o_ref, l_ref, b_ref, dq_ref, dk_ref, dv_ref,
             qd, kd, vd, dod, ddn, ddd, ldd, dqd, dkd, dvd, bias):
        hp, ch = pl.program_id(0), pl.program_id(1)
        lane = lax.broadcasted_iota(jnp.int32, (1, LANES), 1)
        first = lane < (LANES // 2)
        ddn[...] = _seg_sum(do_ref[...] * o_ref[...].astype(F32), b_ref[...])

        @pl.when(ch == 0)
        def _():
            dk_ref[...] = jnp.zeros_like(dk_ref)
            dv_ref[...] = jnp.zeros_like(dv_ref)

        base = ch * CHUNK
        for d in DILATIONS:
            Ld = CHUNK // d
            seg = Ld + WINDOW
            nbr = Ld // WINDOW
            _deinterleave(qd, q_ref, d, Ld, Ld, 0, 0)
            _deinterleave(dod, do_ref, d, Ld, Ld, 0, 0)
            if d > 1:
                _deinterleave(ddd, ddn, d, Ld, Ld, 0, 0)
                _deinterleave(ldd, l_ref, d, Ld, Ld, 0, 0)
            dsrc, lsrc, dqdst = (ddn, l_ref, dq_ref) if d == 1 else (ddd, ldd, dqd)
            for dst, cur, prev in ((kd, k_ref, kp_ref), (vd, v_ref, vp_ref)):
                _deinterleave(dst, prev, d, WINDOW, seg, 0, CHUNK - WINDOW * d)
                _deinterleave(dst, cur, d, Ld, seg, WINDOW, 0)
            dkd[0:d * seg, :] = jnp.zeros((d * seg, LANES), F32)
            dvd[0:d * seg, :] = jnp.zeros((d * seg, LANES), F32)
            _fill_bias(bias, sl_ref, hp, d)

            def step(it, carry, Ld=Ld, seg=seg, nbr=nbr, dsrc=dsrc, lsrc=lsrc, dqdst=dqdst):
                r, nl = it // nbr, it % nbr
                q0 = pl.multiple_of(r * Ld + nl * WINDOW, WINDOW)
                k0 = pl.multiple_of(r * seg + nl * WINDOW, WINDOW)
                later = jnp.where(ch * nbr + nl > 0, 1, 0)
                qb = qd[pl.ds(q0, WINDOW), :]
                k2 = kd[pl.ds(k0, 2 * WINDOW), :]
                v2 = vd[pl.ds(k0, 2 * WINDOW), :]
                dob = dod[pl.ds(q0, WINDOW), :]
                dd = dsrc[pl.ds(q0, WINDOW), :]
                lb = lsrc[pl.ds(q0, WINDOW), :]
                dk2 = jnp.zeros((2 * WINDOW, LANES), F32)
                dv2 = jnp.zeros((2 * WINDOW, LANES), F32)
                dqs = []
                for hh in range(2):
                    mh = first if hh == 0 else jnp.logical_not(first)
                    qh = jnp.where(mh, qb, jnp.zeros_like(qb))
                    doh = jnp.where(mh, dob, jnp.zeros_like(dob))
                    lcol = lb[:, hh * (LANES // 2):hh * (LANES // 2) + 1]
                    p = jnp.exp(_nt(qh, k2) + bias[2 * hh + later] - lcol)
                    dcol = dd[:, hh * (LANES // 2):hh * (LANES // 2) + 1]
                    ds = (p * (_nt(doh, v2) - dcol)).astype(BF16)
                    dqs.append(_nn(ds, k2))
                    dk2 = dk2 + _tn(ds, qh)
                    dv2 = dv2 + _tn(p.astype(BF16), doh)
                dqdst[pl.ds(q0, WINDOW), :] = jnp.where(first, dqs[0], dqs[1])
                dkd[pl.ds(k0, 2 * WINDOW), :] += dk2
                dvd[pl.ds(k0, 2 * WINDOW), :] += dv2
                return carry

            lax.fori_loop(0, nblk, step, 0, unroll=ATTN_UNROLL)
            if d > 1:
                _interleave_add(dq_ref, 0, dqd, d, Ld, Ld, 0)
            for acc, out in ((dkd, dk_ref), (dvd, dv_ref)):
                _interleave_add(out, base, acc, d, Ld, seg, WINDOW)

                @pl.when(ch > 0)
                def _(acc=acc, out=out, d=d, seg=seg):
                    _interleave_add(out, base - WINDOW * d, acc, d, WINDOW, seg, 0)

    blk = lambda m: pl.BlockSpec((CHUNK, LANES), m)
    cur = lambda which: blk(lambda hp, ch: (ch, which * hpn + hp))
    prev = lambda which: blk(lambda hp, ch: (jnp.maximum(ch - 1, 0), which * hpn + hp))
    omap = blk(lambda hp, ch: (ch, hp))
    full = pl.BlockSpec((T, LANES), lambda hp, ch: (0, hp))
    f32buf = pltpu.VMEM((CHUNK, LANES), F32)
    bf16buf = pltpu.VMEM((CHUNK, LANES), BF16)
    return pl.pallas_call(
        body, name=name, grid=(hpn, nch),
        in_specs=[pl.BlockSpec(memory_space=pltpu.SMEM), cur(0), cur(1), prev(1), cur(2), prev(2),
                  blk(lambda hp, ch: (ch, co + hp)), blk(lambda hp, ch: (ch, co + hp)), omap,
                  pl.BlockSpec((LANES, LANES), lambda hp, ch: (0, 0))],
        out_specs=[omap, full, full],
        out_shape=[jax.ShapeDtypeStruct((T, A), F32)] * 3,
        scratch_shapes=[bf16buf, pltpu.VMEM((2 * CHUNK, LANES), BF16), pltpu.VMEM((2 * CHUNK, LANES), BF16),
                        bf16buf, f32buf, f32buf, f32buf, f32buf,
                        pltpu.VMEM((2 * CHUNK, LANES), F32), pltpu.VMEM((2 * CHUNK, LANES), F32),
                        pltpu.VMEM((4, WINDOW, 2 * WINDOW), F32)],
        compiler_params=_params(2),
    )(slopes, qkv, qkv, qkv, qkv, qkv, dycat, ycat, lg, bmat)


def _attn_bwd_combine(name, dzc, dqs, dks, dvs, z, gq, gk, bmat, fmat, A, c0, hd):
    T = z.shape[0]
    tm = _pick(T, (256, 128))
    nbr = len(dqs)
    W0 = dzc.shape[1]

    def body(*refs):
        dq_refs, dk_refs, dv_refs = refs[:nbr], refs[nbr:2 * nbr], refs[2 * nbr:3 * nbr]
        zq_ref, zk_ref, gq_ref, gk_ref, b_ref, f_ref, dzc_ref, dz_ref, dgq_ref, dgk_ref = refs[3 * nbr:]
        i = pl.program_id(0)
        bm = b_ref[...]
        dz_ref[:, :W0] = dzc_ref[...]

        def tot(rs):
            t = rs[0][...]
            for r in rs[1:]:
                t = t + r[...]
            return t

        for idx, (d_refs, z_ref, g_ref, dg_ref, gscale) in enumerate(
                ((dq_refs, zq_ref, gq_ref, dgq_ref, QK_SCALE), (dk_refs, zk_ref, gk_ref, dgk_ref, 1.0))):
            dy = tot(d_refs)
            zv = z_ref[...]
            r = lax.rsqrt(_seg_sum(zv * zv, bm) * (1.0 / hd) + EPS)
            gd = dy * g_ref[...]
            mean = _seg_sum(gd * zv, bm) * (1.0 / hd)
            dz_ref[:, W0 + idx * A:W0 + (idx + 1) * A] = (r * gd - zv * (r * r * r) * mean).astype(BF16)
            part = jnp.sum(dy * zv * r, axis=0, keepdims=True) * gscale

            @pl.when(i == 0)
            def _():
                dg_ref[...] = part

            @pl.when(i > 0)
            def _():
                dg_ref[...] += part

        dz_ref[:, W0 + 2 * A:] = tot(dv_refs).astype(BF16)

        @pl.when(i == T // tm - 1)
        def _():
            fm = f_ref[...]
            for dg_ref in (dgq_ref, dgk_ref):
                v = jnp.broadcast_to(dg_ref[...], (8, A))
                hi = v.astype(BF16)
                mid = (v - hi.astype(F32)).astype(BF16)
                lo = (v - hi.astype(F32) - mid.astype(F32)).astype(BF16)
                dg_ref[...] = (_nn(hi, fm) + _nn(mid, fm) + _nn(lo, fm))[0:1]

    blk = pl.BlockSpec((tm, A), lambda i: (i, 0))
    vec = pl.BlockSpec((1, A), lambda i: (0, 0))
    return pl.pallas_call(
        body, name=name, grid=(T // tm,),
        in_specs=[blk] * (3 * nbr) + [pl.BlockSpec((tm, A), lambda i: (i, c0)),
                                      pl.BlockSpec((tm, A), lambda i: (i, c0 + 1)), vec, vec,
                                      pl.BlockSpec((LANES, LANES), lambda i: (0, 0)),
                                      pl.BlockSpec((A, A), lambda i: (0, 0)),
                                      pl.BlockSpec((tm, W0), lambda i: (i, 0))],
        out_specs=[pl.BlockSpec((tm, W0 + 3 * A), lambda i: (i, 0)), vec, vec],
        out_shape=[jax.ShapeDtypeStruct((T, W0 + 3 * A), BF16), jax.ShapeDtypeStruct((1, A), F32),
                   jax.ShapeDtypeStruct((1, A), F32)],
        compiler_params=_params(1),
    )(*dqs, *dks, *dvs, z, z, gq, gk, bmat, fmat, dzc)


def _local_step(x, tgt, S, comm, hd):
    T, D = x.shape
    C = S["conv_b_dw"].shape[1]
    A = C
    H = A // hd
    Dmix = C + A
    c0 = (2 * C) // A
    slopes = 2.0 ** (-ALIBI_MAX_BIAS * jnp.arange(1, H + 1, dtype=F32) / H)
    seg = jnp.arange(LANES) // hd
    bmat = (seg[:, None] == seg[None, :]).astype(BF16)
    pos_in_head = jnp.arange(A) % hd
    fmat = (pos_in_head[:, None] == pos_in_head[None, :]).astype(BF16)
    gq = jnp.tile(S["q_norm_g"], (1, H)) * QK_SCALE
    gk = jnp.tile(S["k_norm_g"], (1, H))

    wg1, wu1 = comm.weights(("ffn1_w_gate", "ffn1_w_up"), None)
    h1, gate1, up1, a1 = _norm_matmul("ffn1_up", x, comm.tie(S["ffn1_norm_g"]), [wg1, wu1], True)
    wd1, win, w32 = comm.weights(("ffn1_w_down", "w_in", "conv_w32"), a1)
    x1 = _matmul_res("ffn1_down", a1, wd1, x, 0.5)
    h2, z = _norm_matmul("mix_in", x1, S["mix_norm_g"], [win], False)
    yc, ycv = _conv_fwd("conv_fwd", z, w32, S["conv_b_dw"], S["conv_ln_g"], S["conv_ln_b"], C)
    qkv = _attn_prep("attn_prep", z, gq, gk, bmat, A, c0, hd)
    ycat, lg = _attn_fwd("attn_fwd", qkv, slopes, yc, A)
    wout, wg2, wu2, wd2 = comm.weights(("w_out", "ffn2_w_gate", "ffn2_w_up", "ffn2_w_down"), lg)
    x2 = _matmul_res("mix_out", ycat, wout, x1, 1.0)
    h3, gate2, up2, a2 = _norm_matmul("ffn2_up", x2, S["ffn2_norm_g"], [wg2, wu2], True)
    dx3, dx3b, lossvec = _matmul_res("ffn2_down_loss", a2, wd2, x2, 0.5, tgt=tgt)

    G = {}
    dgate2, dup2 = _nt_matmul("ffn2_dact", dx3b, wd2, 0.5, gate2, up2)
    comm.reduce_begin("ffn2", {"ffn2_w_down": _tn_matmul("ffn2_dwd", a2, dx3b, 0.5),
                               "ffn2_w_gate": _tn_matmul("ffn2_dwg", h3, dgate2),
                               "ffn2_w_up": _tn_matmul("ffn2_dwu", h3, dup2)}, split=True)
    dx2, dx2b, G["ffn2_norm_g"] = _nt_rms_bwd("ffn2_dx", [dgate2, dup2], [wg2, wu2],
                                              x2, comm.tie(S["ffn2_norm_g"]), dx3)
    comm.reduce_scatter("ffn2", dx2b)
    dwout = _tn_matmul("mix_dwout", ycat, dx2b)
    dycat, _ = _nt_matmul("mix_dycat", dx2b, wout)
    dzc, G["conv_w32"], G["conv_b_dw"], G["conv_ln_g"], G["conv_ln_b"] = _conv_bwd(
        "conv_bwd", z, ycv, dycat, w32, S["conv_ln_g"], S["conv_ln_b"], C)
    dq, dk, dv = _attn_bwd("attn_bwd", qkv, dycat, ycat, lg, slopes, bmat, A, C)
    dz, G["q_norm_g"], G["k_norm_g"] = _attn_bwd_combine(
        "attn_bwd_combine", dzc, [dq], [dk], [dv], z, gq, gk, bmat, fmat, A, c0, hd)
    comm.reduce_end("ffn2", dz)
    comm.reduce_begin("mix", {"w_out": dwout, "w_in": _tn_matmul("mix_dwin", h2, dz)}, split=True)
    dx1, dx1b, G["mix_norm_g"] = _nt_rms_bwd("mix_dx", [dz], [win], x1, comm.tie(S["mix_norm_g"]), dx2)
    comm.reduce_scatter("mix", dx1b)
    dgate1, dup1 = _nt_matmul("ffn1_dact", dx1b, wd1, 0.5, gate1, up1)
    dwd1 = _tn_matmul("ffn1_dwd", a1, dx1b, 0.5)
    dwg1 = _tn_matmul("ffn1_dwg", h1, dgate1)
    dwu1 = _tn_matmul("ffn1_dwu", h1, dup1)
    comm.reduce_end("mix", dwu1)
    comm.reduce_begin("ffn1", {"ffn1_w_down": dwd1, "ffn1_w_gate": dwg1, "ffn1_w_up": dwu1})
    dx0, _, G["ffn1_norm_g"] = _nt_rms_bwd("ffn1_dx", [dgate1, dup1], [wg1, wu1],
                                           x, comm.tie(S["ffn1_norm_g"]), dx1)
    comm.reduce_end("ffn1", dx0)
    return lossvec, dx0, G


BIG = (("ffn1_w_gate", 1), ("ffn1_w_up", 1), ("ffn1_w_down", 0), ("w_in", 1), ("w_out", 0),
       ("ffn2_w_gate", 1), ("ffn2_w_up", 1), ("ffn2_w_down", 0))
AXIS = dict(BIG)
FLIPS = ((1, 0), (0, 1), (1, 1))
HBM = pl.BlockSpec(memory_space=pltpu.HBM)
SEM = pl.BlockSpec(memory_space=pltpu.SEMAPHORE)
EFFECT = pltpu.SideEffectType.DATAFLOW_SIDE_EFFECTING
TOKEN = jax.ShapeDtypeStruct((8, LANES), F32)


def _window(ref, shape, axis, slab=None, half=None):
    idx = [pl.ds(0, shape[0]), pl.ds(0, shape[1])]
    if slab is not None:
        n = shape[axis] // 4
        idx[axis] = pl.ds(pl.multiple_of(slab * n, 8), n)
    if half is not None:
        hs = shape[1 - axis] // 2
        idx[1 - axis] = pl.ds(pl.multiple_of(half * hs, 8), hs)
    return ref.at[idx[0], idx[1]]


def _position():
    return lax.axis_index("x"), lax.axis_index("y"), lax.axis_index("c")


def _half_shape(shape, axis):
    return (shape[0] // 2, shape[1]) if axis == 1 else (shape[0], shape[1] // 2)


def _slab_shape(shape, axis):
    return (shape[0], shape[1] // 4) if axis == 1 else (shape[0] // 4, shape[1])


def _piece_shape(shape, axis):
    return _half_shape(_slab_shape(shape, axis), axis)


def _full_shape(shard, axis):
    return (shard.shape[0], shard.shape[1] * 4) if axis == 1 else (shard.shape[0] * 4, shard.shape[1])


def _hbm(a):
    return pltpu.with_memory_space_constraint(a, pltpu.HBM)


def _remote(src, dst, send_sem, recv_sem, to):
    return pltpu.make_async_remote_copy(src_ref=src, dst_ref=dst, send_sem=send_sem, recv_sem=recv_sem,
                                        device_id=to, device_id_type=MESH)


def _place(name, pos, w, axis):
    R, Cc = w.shape
    tr = _pick(R, (256, 128, 64, 32, 16))
    nrb = R // tr

    def body(pos_ref, w_ref, o_ref):
        o_ref[...] = w_ref[...].astype(BF16)

    omap = (lambda i, p: (i, p[0])) if axis == 1 else (lambda i, p: (p[0] * nrb + i, 0))
    return pl.pallas_call(
        body, name=name,
        grid_spec=pltpu.PrefetchScalarGridSpec(
            num_scalar_prefetch=1, grid=(nrb,), in_specs=[pl.BlockSpec((tr, Cc), lambda i, p: (i, 0))],
            out_specs=pl.BlockSpec((tr, Cc), omap)),
        out_shape=jax.ShapeDtypeStruct(_full_shape(w, axis), BF16), compiler_params=_params(1),
    )(pos, w)


def _gather_now(name, axes, fulls):
    nt = len(fulls)
    shapes = [f.shape for f in fulls]

    def body(*refs):
        outs, token = refs[nt:2 * nt], refs[2 * nt]
        send_sems, recv_sems = refs[2 * nt + 1:]
        x, y, c = _position()
        j0 = 2 * x + y
        sib = (x, y, 1 - c)

        def copy(t, k, slab, half, to):
            win = _window(outs[t], shapes[t], axes[t], slab=slab, half=half)
            return _remote(win, win, send_sems.at[t, k], recv_sems.at[t, k], to)

        sends = []
        for k, (fx, fy) in enumerate(FLIPS):
            for t in range(nt):
                cp = copy(t, k, j0, c, (x ^ fx, y ^ fy, c))
                cp.start()
                sends.append(cp)
        for k, (fx, fy) in enumerate(FLIPS):
            js = 2 * (x ^ fx) + (y ^ fy)
            for t in range(nt):
                copy(t, k, js, c, sib).wait_recv()
                cp = copy(t, 3 + k, js, c, sib)
                cp.start()
                sends.append(cp)
        for k, (fx, fy) in enumerate(FLIPS):
            js = 2 * (x ^ fx) + (y ^ fy)
            for t in range(nt):
                copy(t, 3 + k, js, 1 - c, sib).wait_recv()
        for cp in sends:
            cp.wait_send()
        token[...] = jnp.zeros_like(token)

    res = pl.pallas_call(
        body, name=name, in_specs=[ANY] * nt,
        out_specs=[ANY] * nt + [pl.BlockSpec(memory_space=pltpu.VMEM)],
        out_shape=[jax.ShapeDtypeStruct(s, BF16) for s in shapes] + [TOKEN],
        input_output_aliases={t: t for t in range(nt)},
        scratch_shapes=[pltpu.SemaphoreType.DMA((nt, 6)), pltpu.SemaphoreType.DMA((nt, 6))],
    )(*fulls)
    return list(res[:nt]), res[nt]


def _split_start(name, arrays, ncopies, plan):
    na = len(arrays)

    def body(*refs):
        ins = refs[:na]
        send_sems, recv_sems = refs[na], refs[na + 1]
        token = refs[-1]
        x, y, c = _position()
        for i, (src, dst, to) in enumerate(plan(ins, x, y, c)):
            _remote(src, dst, send_sems.at[i], recv_sems.at[i], to).start()
        token[...] = jnp.zeros_like(token)

    res = pl.pallas_call(
        body, name=name, in_specs=[HBM] * na,
        out_specs=tuple([SEM, SEM] + [HBM] * na + [pl.BlockSpec(memory_space=pltpu.VMEM)]),
        out_shape=tuple([pltpu.SemaphoreType.DMA((ncopies,)), pltpu.SemaphoreType.DMA((ncopies,))]
                        + [pltpu.HBM(a.shape, a.dtype) for a in arrays] + [TOKEN]),
        input_output_aliases={i: 2 + i for i in range(na)},
        compiler_params=pltpu.CompilerParams(has_side_effects=EFFECT),
    )(*[_hbm(a) for a in arrays])
    return (res[0], res[1]), list(res[2:2 + na]), res[-1]


def _split_wait(name, arrays, sems, after, plan):
    na = len(arrays)

    def body(*refs):
        ins = refs[:na]
        send_sems, recv_sems = refs[na], refs[na + 1]
        x, y, c = _position()
        for i, (src, dst, to) in enumerate(plan(ins, x, y, c)):
            cp = _remote(src, dst, send_sems.at[i], recv_sems.at[i], to)
            cp.wait_send()
            cp.wait_recv()

    res = pl.pallas_call(
        body, name=name, in_specs=[HBM] * na + [SEM, SEM, ANY],
        out_specs=tuple([HBM] * na), out_shape=tuple(pltpu.HBM(a.shape, a.dtype) for a in arrays),
        input_output_aliases={i: i for i in range(na)},
        compiler_params=pltpu.CompilerParams(has_side_effects=EFFECT),
    )(*arrays, *sems, after)
    return list(res)


def _gather_plan(axes, shapes, conv_shape):
    nt = len(axes)

    def plan(refs, x, y, c):
        j0 = 2 * x + y
        out = []
        for fx, fy in FLIPS:
            to = (x ^ fx, y ^ fy, c)
            for t in range(nt):
                win = _window(refs[t], shapes[t], axes[t], slab=j0, half=c)
                out.append((win, win, to))
            if conv_shape is not None:
                win = _window(refs[nt], conv_shape, 1, slab=j0)
                out.append((win, win, to))
        return out

    return plan


def _gather_finish(name, axes, fulls):
    nt = len(axes)
    shapes = [f.shape for f in fulls]

    def body(*refs):
        outs = refs[nt:2 * nt]
        send_sems, recv_sems = refs[2 * nt:]
        x, y, c = _position()
        sib = (x, y, 1 - c)
        cps = []
        for k, (fx, fy) in enumerate(FLIPS):
            js = 2 * (x ^ fx) + (y ^ fy)
            for t in range(nt):
                landed = _window(outs[t], shapes[t], axes[t], slab=js, half=c)
                cp = _remote(landed, landed, send_sems.at[t, k], recv_sems.at[t, k], sib)
                cp.start()
                cps.append(cp)
        for k, (fx, fy) in enumerate(FLIPS):
            js = 2 * (x ^ fx) + (y ^ fy)
            for t in range(nt):
                other = _window(outs[t], shapes[t], axes[t], slab=js, half=1 - c)
                _remote(other, other, send_sems.at[t, k], recv_sems.at[t, k], sib).wait_recv()
        for cp in cps:
            cp.wait_send()

    res = pl.pallas_call(
        body, name=name, in_specs=[ANY] * nt, out_specs=[ANY] * nt,
        out_shape=[jax.ShapeDtypeStruct(f.shape, f.dtype) for f in fulls],
        input_output_aliases={t: t for t in range(nt)},
        scratch_shapes=[pltpu.SemaphoreType.DMA((nt, 3)), pltpu.SemaphoreType.DMA((nt, 3))],
    )(*fulls)
    return list(res)


def _pair_exchange(name, srcs, windows, out_shapes, dtype):
    nt = len(srcs)

    def body(*refs):
        ins, outs = refs[:nt], refs[nt:2 * nt]
        send_sems, recv_sems = refs[2 * nt:]
        x, y, c = _position()
        cps = []
        for t in range(nt):
            cp = _remote(windows[t](ins[t], c), outs[t], send_sems.at[t], recv_sems.at[t], (x, y, 1 - c))
            cp.start()
            cps.append(cp)
        for cp in cps:
            cp.wait()

    return pl.pallas_call(
        body, name=name, in_specs=[ANY] * nt, out_specs=[ANY] * nt,
        out_shape=[jax.ShapeDtypeStruct(s, dtype) for s in out_shapes],
        scratch_shapes=[pltpu.SemaphoreType.DMA((nt,)), pltpu.SemaphoreType.DMA((nt,))],
    )(*srcs)


def _pair_plan(axes, shapes):
    nt = len(axes)

    def plan(refs, x, y, c):
        return [(_window(refs[t], shapes[t], axes[t], half=1 - c), refs[nt + t], (x, y, 1 - c)) for t in range(nt)]

    return plan


def _scatter_plan(axes, shapes):
    nt = len(axes)

    def plan(refs, x, y, c):
        out = []
        for k, (fx, fy) in enumerate(FLIPS):
            js = 2 * (x ^ fx) + (y ^ fy)
            for t in range(nt):
                src = _window(refs[t], _half_shape(shapes[t], axes[t]), axes[t], slab=js)
                out.append((src, refs[nt + t].at[k], (x ^ fx, y ^ fy, c)))
        return out

    return plan


def _gather_small(packed):
    R, Cc = packed.shape

    def body(p_ref, o_ref, send_sems, recv_sems, loc_sem):
        x, y, c = _position()
        me = 4 * x + 2 * y + c
        mine = pltpu.make_async_copy(p_ref, o_ref.at[me], loc_sem)
        mine.start()
        cps = []
        for k in range(1, 8):
            fx, fy, fc = (k >> 2) & 1, (k >> 1) & 1, k & 1
            cp = pltpu.make_async_remote_copy(
                src_ref=p_ref, dst_ref=o_ref.at[me], send_sem=send_sems.at[k - 1], recv_sem=recv_sems.at[k - 1],
                device_id=(x ^ fx, y ^ fy, c ^ fc), device_id_type=MESH)
            cp.start()
            cps.append(cp)
        for cp in cps:
            cp.wait()
        mine.wait()

    return pl.pallas_call(
        body, name="gather_small_grads", in_specs=[ANY], out_specs=ANY,
        out_shape=jax.ShapeDtypeStruct((8, R, Cc), F32),
        scratch_shapes=[pltpu.SemaphoreType.DMA((7,)), pltpu.SemaphoreType.DMA((7,)), pltpu.SemaphoreType.DMA],
    )(packed)


def _sum_slots(name, slots):
    n, R, Cc = slots.shape

    def body(s_ref, o_ref):
        t = s_ref[0]
        for i in range(1, n):
            t = t + s_ref[i]
        o_ref[...] = t

    return pl.pallas_call(
        body, name=name, grid=(1,), in_specs=[pl.BlockSpec((n, R, Cc), lambda i: (0, 0, 0))],
        out_specs=pl.BlockSpec((R, Cc), lambda i: (0, 0)), out_shape=jax.ShapeDtypeStruct((R, Cc), F32),
        compiler_params=_params(1),
    )(slots)


def _pair_sum(name, pos, g, land, shape, axis):
    hshape = _half_shape(shape, axis)
    R, Cc = hshape
    tr = _pick(R, (256, 128, 64, 32, 16))
    nrb = R // tr

    def body(pos_ref, g_ref, l_ref, o_ref):
        o_ref[...] = (g_ref[...].astype(F32) + l_ref[...].astype(F32)).astype(BF16)

    if axis == 1:
        gmap = lambda i, p: (p[1] * nrb + i, 0)
    else:
        gmap = lambda i, p: (i, p[1])
    blk = pl.BlockSpec((tr, Cc), lambda i, p: (i, 0))
    return pl.pallas_call(
        body, name=name,
        grid_spec=pltpu.PrefetchScalarGridSpec(
            num_scalar_prefetch=1, grid=(nrb,), in_specs=[pl.BlockSpec((tr, Cc), gmap), blk], out_specs=blk),
        out_shape=jax.ShapeDtypeStruct(hshape, BF16), compiler_params=_params(1),
    )(pos, g, land)


def _chip_sum(name, pos, sb, land, shape, axis):
    hshape = _half_shape(shape, axis)
    pshape = _piece_shape(shape, axis)
    R, Cc = pshape
    tr = _pick(R, (256, 128, 64, 32, 16))
    nrb = R // tr

    def body(pos_ref, s_ref, l_ref, o_ref):
        t = s_ref[...].astype(F32)
        for k in range(3):
            t = t + l_ref[k].astype(F32)
        o_ref[...] = t

    if axis == 1:
        smap = lambda i, p: (i, p[0])
    else:
        smap = lambda i, p: (p[0] * nrb + i, 0)
    return pl.pallas_call(
        body, name=name,
        grid_spec=pltpu.PrefetchScalarGridSpec(
            num_scalar_prefetch=1, grid=(nrb,),
            in_specs=[pl.BlockSpec((tr, Cc), smap), pl.BlockSpec((3, tr, Cc), lambda i, p: (0, i, 0))],
            out_specs=pl.BlockSpec((tr, Cc), lambda i, p: (i, 0))),
        out_shape=jax.ShapeDtypeStruct(pshape, F32), compiler_params=_params(1),
    )(pos, sb, land)


def _adam_math(w, g, m, v):
    m = ADAM_B1 * m + (1.0 - ADAM_B1) * g
    v = ADAM_B2 * v + (1.0 - ADAM_B2) * (g * g)
    m_hat = m / (1.0 - ADAM_B1 ** ADAM_STEP)
    v_hat = v / (1.0 - ADAM_B2 ** ADAM_STEP)
    delta = -ADAM_LR * (m_hat / (jnp.sqrt(v_hat) + ADAM_EPS) + ADAM_WD * w)
    return delta, m, v


def _adamw_halves(name, pos, w, m, v, mine, theirs, axis):
    R, Cc = w.shape
    hr, hc = mine.shape
    tr = _pick(hr, (256, 128, 64, 32, 16))
    nrb = hr // tr

    def body(pos_ref, w_ref, m_ref, v_ref, a_ref, b_ref, g_ref, d_ref, nm_ref, nv_ref):
        half = pl.program_id(0)
        g = jnp.where(half == pos_ref[1], a_ref[...], b_ref[...])
        d, nm, nv = _adam_math(w_ref[...], g, m_ref[...], v_ref[...])
        g_ref[...] = g
        d_ref[...] = d
        nm_ref[...] = nm
        nv_ref[...] = nv

    if axis == 1:
        wmap = lambda h, i, p: (h * nrb + i, 0)
    else:
        wmap = lambda h, i, p: (i, h)
    wblk = pl.BlockSpec((tr, hc), wmap)
    ablk = pl.BlockSpec((tr, hc), lambda h, i, p: (jnp.where(h == p[1], i, 0), 0))
    bblk = pl.BlockSpec((tr, hc), lambda h, i, p: (jnp.where(h == p[1], 0, i), 0))
    return pl.pallas_call(
        body, name=name,
        grid_spec=pltpu.PrefetchScalarGridSpec(
            num_scalar_prefetch=1, grid=(2, nrb), in_specs=[wblk, wblk, wblk, ablk, bblk], out_specs=[wblk] * 4),
        out_shape=[jax.ShapeDtypeStruct((R, Cc), F32)] * 4, compiler_params=_params(2),
    )(pos, w, m, v, mine, theirs)


def _adamw_small(name, w, g, m, v):
    def body(w_ref, g_ref, m_ref, v_ref, d_ref, nm_ref, nv_ref):
        d, nm, nv = _adam_math(w_ref[...], g_ref[...], m_ref[...], v_ref[...])
        d_ref[...] = d
        nm_ref[...] = nm
        nv_ref[...] = nv

    blk = pl.BlockSpec(w.shape, lambda i: (0, 0))
    return pl.pallas_call(
        body, name=name, grid=(1,), in_specs=[blk] * 4, out_specs=[blk] * 3,
        out_shape=[jax.ShapeDtypeStruct(w.shape, F32)] * 3, compiler_params=_params(1),
    )(w, g, m, v)


SMALL = ("ffn1_norm_g", "mix_norm_g", "conv_b_dw", "conv_ln_g", "conv_ln_b", "q_norm_g", "k_norm_g", "ffn2_norm_g")
ORDER = ("ffn1_norm_g", "ffn1_w_gate", "ffn1_w_up", "ffn1_w_down", "mix_norm_g", "w_in", "conv_w_dw", "conv_b_dw",
         "conv_ln_g", "conv_ln_b", "q_norm_g", "k_norm_g", "w_out", "ffn2_norm_g", "ffn2_w_gate", "ffn2_w_up",
         "ffn2_w_down")
GATHER_FIRST = ("ffn1_w_gate", "ffn1_w_up")
GATHER_SECOND = ("ffn1_w_down", "w_in")
GATHER_THIRD = ("w_out", "ffn2_w_gate", "ffn2_w_up", "ffn2_w_down")


class _Exchange:
    def __init__(self, P, Mo, Vo, conv_shard, pos):
        self.P, self.Mo, self.Vo, self.pos = P, Mo, Vo, pos
        self.tokens = []
        self.pending = {}
        self.reducing = {}
        self.results = {}
        placed = {n: _place("place_" + n, pos, P[n][0], a) for n, a in BIG}
        self.shapes = {n: placed[n].shape for n, _ in BIG}
        first, tok = _gather_now("gather_first", [AXIS[n] for n in GATHER_FIRST], [placed[n] for n in GATHER_FIRST])
        self.ready = dict(zip(GATHER_FIRST, first))
        cq = conv_shard.shape[1]
        conv_full = lax.dynamic_update_slice(jnp.zeros((conv_shard.shape[0], 4 * cq), F32), conv_shard,
                                             (0, pos[0] * cq))
        for gname, names, conv in (("second", GATHER_SECOND, conv_full), ("third", GATHER_THIRD, None)):
            axes = [AXIS[n] for n in names]
            shapes = [self.shapes[n] for n in names]
            arrays = [placed[n] for n in names] + ([conv] if conv is not None else [])
            small = min(range(len(arrays)), key=lambda i: arrays[i].size)
            arrays[small] = arrays[small] + tok[0, 0].astype(arrays[small].dtype)
            plan = _gather_plan(axes, shapes, conv.shape if conv is not None else None)
            sems, thru, tok = _split_start("gather_%s_start" % gname, arrays, 3 * len(arrays), plan)
            self.tokens.append(tok)
            for n in names + (("conv_w32",) if conv is not None else ()):
                self.pending[n] = (gname, names, axes, plan, sems, thru, conv is not None)

    def tie(self, v):
        for tok in self.tokens:
            v = v + tok[0:1, 0:1]
        self.tokens = []
        return v

    def weights(self, names, after):
        if names[0] in self.pending:
            gname, gnames, axes, plan, sems, thru, has_conv = self.pending[names[0]]
            thru = _split_wait("gather_%s_wait" % gname, thru, sems, after, plan)
            nt = len(gnames)
            fulls = _gather_finish("gather_%s_finish" % gname, axes, thru[:nt])
            for n, f in zip(gnames, fulls):
                self.ready[n] = f
                del self.pending[n]
            if has_conv:
                self.ready["conv_w32"] = thru[nt]
                del self.pending["conv_w32"]
        return [self.ready[n] for n in names]

    def reduce_begin(self, gname, grads, split=False):
        names = list(grads)
        axes = [AXIS[n] for n in names]
        shapes = [self.shapes[n] for n in names]
        gs = [grads[n] for n in names]
        nt = len(names)
        if not split:
            to_sibling = [(lambda ref, c, s=s, a=a: _window(ref, s, a, half=1 - c)) for s, a in zip(shapes, axes)]
            landed = _pair_exchange("pair_exchange_" + gname, gs, to_sibling,
                                    [_half_shape(s, a) for s, a in zip(shapes, axes)], BF16)
            self._scatter(gname, names, axes, shapes, gs, landed)
            return
        plan = _pair_plan(axes, shapes)
        arrays = gs + [lax.empty(_half_shape(s, a), BF16) for s, a in zip(shapes, axes)]
        sems, thru, tok = _split_start("pair_%s_start" % gname, arrays, nt, plan)
        self.tokens.append(tok)
        self.reducing[gname] = (names, axes, shapes, plan, sems, thru)

    def reduce_scatter(self, gname, after):
        names, axes, shapes, plan, sems, thru = self.reducing.pop(gname)
        nt = len(names)
        thru = _split_wait("pair_%s_wait" % gname, thru, sems, after, plan)
        self._scatter(gname, names, axes, shapes, thru[:nt], thru[nt:])

    def _scatter(self, gname, names, axes, shapes, gs, landed):
        sbs = [_pair_sum("pair_sum_" + n, self.pos, g, l, s, a)
               for n, a, g, l, s in zip(names, axes, gs, landed, shapes)]
        lands = [lax.empty((3,) + _piece_shape(s, a), BF16) for s, a in zip(shapes, axes)]
        plan = _scatter_plan(axes, shapes)
        sems, thru, tok = _split_start("scatter_%s_start" % gname, sbs + lands, 3 * len(names), plan)
        self.tokens.append(tok)
        self.reducing[gname] = (names, axes, shapes, plan, sems, thru)

    def reduce_end(self, gname, after):
        names, axes, shapes, plan, sems, thru = self.reducing.pop(gname)
        nt = len(names)
        thru = _split_wait("scatter_%s_wait" % gname, thru, sems, after, plan)
        mine = [_chip_sum("chip_sum_" + n, self.pos, sb, l, s, a)
                for n, a, sb, l, s in zip(names, axes, thru[:nt], thru[nt:], shapes)]
        theirs = _pair_exchange("half_exchange_" + gname, mine, [(lambda ref, c: ref)] * nt,
                                [m.shape for m in mine], F32)
        for n, a, mi, th in zip(names, axes, mine, theirs):
            g, d, nm, nv = _adamw_halves("adamw_" + n, self.pos, self.P[n][0], self.Mo[n][0], self.Vo[n][0],
                                         mi, th, a)
            self.results[n] = (g[None], d[None], nm[None], nv[None])


def kernel(x, ffn1_norm_g, ffn1_w_gate, ffn1_w_up, ffn1_w_down, mix_norm_g, w_in, conv_w_dw, conv_b_dw, conv_ln_g, conv_ln_b, q_norm_g, k_norm_g, w_out, ffn2_norm_g, ffn2_w_gate, ffn2_w_up, ffn2_w_down, loss_target, m_ffn1_norm_g, m_ffn1_w_gate, m_ffn1_w_up, m_ffn1_w_down, m_mix_norm_g, m_w_in, m_conv_w_dw, m_conv_b_dw, m_conv_ln_g, m_conv_ln_b, m_q_norm_g, m_k_norm_g, m_w_out, m_ffn2_norm_g, m_ffn2_w_gate, m_ffn2_w_up, m_ffn2_w_down, v_ffn1_norm_g, v_ffn1_w_gate, v_ffn1_w_up, v_ffn1_w_down, v_mix_norm_g, v_w_in, v_conv_w_dw, v_conv_b_dw, v_conv_ln_g, v_conv_ln_b, v_q_norm_g, v_k_norm_g, v_w_out, v_ffn2_norm_g, v_ffn2_w_gate, v_ffn2_w_up, v_ffn2_w_down):
    args = dict(locals())
    P = {n: args[n] for n in ORDER}
    Mo = {n: args["m_" + n] for n in ORDER}
    Vo = {n: args["v_" + n] for n in ORDER}
    xs = x[0]
    tgt = loss_target[0]
    T, D = xs.shape
    hd = q_norm_g.shape[-1]
    C = conv_b_dw.shape[-1]
    ntap = conv_w_dw.shape[1]
    cx, cy, cc = _position()
    j0 = 2 * cx + cy
    pos = jnp.stack([j0, cc]).astype(jnp.int32)

    conv_shard = jnp.pad(conv_w_dw[0], ((0, HALO - ntap), (0, 0)))
    comm = _Exchange(P, Mo, Vo, conv_shard, pos)
    lossvec, dx0, G = _local_step(xs, tgt, {n: P[n] for n in SMALL}, comm, hd)
    loss = lax.psum(0.5 / D * jnp.sum(lossvec), AXES)
    grads, deltas, new_m, new_v = {}, {}, {}, {}
    for n, _ in BIG:
        grads[n], deltas[n], new_m[n], new_v[n] = comm.results[n]

    rows = [G["conv_w32"]]
    for n in ("ffn1_norm_g", "mix_norm_g", "ffn2_norm_g"):
        rows.append(G[n].reshape(D // C, C))
    for n in ("conv_b_dw", "conv_ln_g", "conv_ln_b", "q_norm_g", "k_norm_g"):
        rows.append(G[n])
    packed = jnp.concatenate(rows, axis=0)
    packed = jnp.pad(packed, ((0, -packed.shape[0] % 8), (0, 0)))
    total = _sum_slots("sum_small_grads", _gather_small(packed))
    r = HALO
    small_g = {}
    cq = C // 4
    small_g["conv_w_dw"] = lax.dynamic_slice(total[:ntap], (0, j0 * cq), (ntap, cq))
    for n in ("ffn1_norm_g", "mix_norm_g", "ffn2_norm_g"):
        small_g[n] = total[r:r + D // C].reshape(1, D)
        r += D // C
    for n in ("conv_b_dw", "conv_ln_g", "conv_ln_b"):
        small_g[n] = total[r:r + 1]
        r += 1
    for n in ("q_norm_g", "k_norm_g"):
        small_g[n] = total[r:r + 1, :hd]
        r += 1
    for n in ("conv_w_dw",) + SMALL:
        lead = n == "conv_w_dw"
        w2, m2, v2 = (P[n][0], Mo[n][0], Vo[n][0]) if lead else (P[n], Mo[n], Vo[n])
        d, nm, nv = _adamw_small("adamw_" + n, w2, small_g[n], m2, v2)
        if lead:
            grads[n], deltas[n], new_m[n], new_v[n] = small_g[n][None], d[None], nm[None], nv[None]
        else:
            grads[n], deltas[n], new_m[n], new_v[n] = small_g[n], d, nm, nv

    return (loss, dx0[None], *[grads[n] for n in ORDER], *[deltas[n] for n in ORDER],
            *[new_m[n] for n in ORDER], *[new_v[n] for n in ORDER])
```

```python
import jax
import jax.numpy as jnp
from jax import lax
from jax.experimental import pallas as pl
from jax.experimental.pallas import tpu as pltpu

F32 = jnp.float32
BF16 = jnp.bfloat16
EPS = 1e-6
WINDOW = 128
DILATIONS = (1, 4, 16)
ALIBI_MAX_BIAS = 8.0
LANES = 128
HALO = 32
ADAM_LR, ADAM_B1, ADAM_B2, ADAM_EPS, ADAM_WD, ADAM_STEP = 0.001, 0.9, 0.999, 1e-08, 0.01, 10
VMEM_LIMIT_MB = 62
ROW_TILE = 1024
TN_ACC_ELEMS = 3 * 1024 * 1024
EPILOGUE_ROWS = 16
ACC_COLS = 512
MESH = pl.DeviceIdType.MESH
ANY = pl.BlockSpec(memory_space=pl.ANY)
AXES = ("x", "y", "c")
NEG = -1e30


def _pick(n, cands):
    for c in cands:
        if n % c == 0:
            return c
    return n


def _params(nsem):
    return pltpu.CompilerParams(dimension_semantics=("arbitrary",) * nsem,
                                vmem_limit_bytes=VMEM_LIMIT_MB << 20)


def _nn(a, b):
    return jnp.dot(a, b, preferred_element_type=F32)


def _nt(a, b):
    return lax.dot_general(a, b, (((1,), (1,)), ((), ())), preferred_element_type=F32)


def _tn(a, b):
    return lax.dot_general(a, b, (((0,), (0,)), ((), ())), preferred_element_type=F32)


def _sigmoid(v):
    return jax.nn.sigmoid(v)


def _rms_r(xv):
    return lax.rsqrt(jnp.mean(xv * xv, axis=-1, keepdims=True) + EPS)


def _norm_matmul(name, x, g, ws, swiglu):
    T, D = x.shape
    N = ws[0].shape[1]
    tm = _pick(T, (ROW_TILE, 512, 256, 128))
    tn = _pick(N, (512, 256, 128))
    nw = len(ws)

    def body(*refs):
        x_ref, g_ref = refs[:2]
        w_refs = refs[2:2 + nw]
        outs = refs[2 + nw:-1]
        hs = refs[-1]

        @pl.when(pl.program_id(1) == 0)
        def _():
            for r0 in range(0, tm, EPILOGUE_ROWS):
                rows = slice(r0, r0 + min(EPILOGUE_ROWS, tm))
                xv = x_ref[rows, :]
                hv = (xv * _rms_r(xv) * g_ref[...]).astype(BF16)
                hs[rows, :] = hv
                outs[0][rows, :] = hv

        h = hs[...]
        if swiglu:
            gt = _nn(h, w_refs[0][...])
            u = _nn(h, w_refs[1][...])
            sg = _sigmoid(gt)
            silu = gt * sg
            outs[1][...] = (u * (sg * (1.0 + gt * (1.0 - sg)))).astype(BF16)
            outs[2][...] = silu.astype(BF16)
            outs[3][...] = (silu * u).astype(BF16)
        else:
            outs[1][...] = _nn(h, w_refs[0][...])

    row = pl.BlockSpec((tm, D), lambda i, j: (i, 0))
    col = pl.BlockSpec((D, tn), lambda i, j: (0, j))
    tile = pl.BlockSpec((tm, tn), lambda i, j: (i, j))
    if swiglu:
        out_shape = [jax.ShapeDtypeStruct((T, D), BF16)] + [jax.ShapeDtypeStruct((T, N), BF16)] * 3
        out_specs = [row, tile, tile, tile]
    else:
        out_shape = [jax.ShapeDtypeStruct((T, D), BF16), jax.ShapeDtypeStruct((T, N), F32)]
        out_specs = [row, tile]
    return pl.pallas_call(
        body, name=name, grid=(T // tm, N // tn),
        in_specs=[row, pl.BlockSpec((1, D), lambda i, j: (0, 0))] + [col] * nw,
        out_specs=out_specs, out_shape=out_shape,
        scratch_shapes=[pltpu.VMEM((tm, D), BF16)],
        compiler_params=_params(2),
    )(x, g, *ws)


def _matmul_res(name, a, w, res, scale, tgt=None):
    T, K = a.shape
    N = w.shape[1]
    loss = tgt is not None
    tm = _pick(T, (512, 256, 128))
    tk = _pick(K, (1408, 1024, 512, 256, 128))
    nk = K // tk

    def body(*refs):
        if loss:
            a_ref, w_ref, res_ref, tgt_ref, dx_ref, dxb_ref, lv_ref, acc = refs
        else:
            a_ref, w_ref, res_ref, out_ref, acc = refs
        i, k = pl.program_id(0), pl.program_id(1)

        @pl.when(k == 0)
        def _():
            acc[...] = jnp.zeros_like(acc)

        acc[...] += _nn(a_ref[...], w_ref[...])

        @pl.when(k == nk - 1)
        def _():
            part = jnp.zeros((1, N), F32)
            for r0 in range(0, tm, EPILOGUE_ROWS):
                rows = slice(r0, r0 + min(EPILOGUE_ROWS, tm))
                val = res_ref[rows, :] + scale * acc[rows, :]
                if loss:
                    dv = val - tgt_ref[rows, :]
                    dx = dv * (1.0 / N)
                    dx_ref[rows, :] = dx
                    dxb_ref[rows, :] = dx.astype(BF16)
                    part = part + jnp.sum(dv * dv, axis=0, keepdims=True)
                else:
                    out_ref[rows, :] = val
            if loss:
                @pl.when(i == 0)
                def _():
                    lv_ref[...] = part

                @pl.when(i > 0)
                def _():
                    lv_ref[...] += part

    row = pl.BlockSpec((tm, N), lambda i, k: (i, 0))
    in_specs = [pl.BlockSpec((tm, tk), lambda i, k: (i, k)), pl.BlockSpec((tk, N), lambda i, k: (k, 0)), row]
    args = [a, w, res]
    if loss:
        in_specs.append(row)
        args.append(tgt)
        out_specs = [row, row, pl.BlockSpec((1, N), lambda i, k: (0, 0))]
        out_shape = [jax.ShapeDtypeStruct((T, N), F32), jax.ShapeDtypeStruct((T, N), BF16),
                     jax.ShapeDtypeStruct((1, N), F32)]
    else:
        out_specs = row
        out_shape = jax.ShapeDtypeStruct((T, N), F32)
    return pl.pallas_call(
        body, name=name, grid=(T // tm, nk), in_specs=in_specs, out_specs=out_specs, out_shape=out_shape,
        scratch_shapes=[pltpu.VMEM((tm, N), F32)], compiler_params=_params(2),
    )(*args)


def _nt_matmul(name, dyb, w, scale=1.0, gate=None, up=None):
    T, D = dyb.shape
    N = w.shape[0]
    tm = _pick(T, (ROW_TILE, 512, 256, 128))
    tn = _pick(N, (512, 256, 128))
    swiglu = gate is not None

    def body(*refs):
        if swiglu:
            dy_ref, w_ref, g_ref, u_ref, dg_ref, du_ref = refs
        else:
            dy_ref, w_ref, o_ref, ob_ref = refs
        da = _nt(dy_ref[...], w_ref[...]) * scale
        if swiglu:
            dg_ref[...] = (da * g_ref[...].astype(F32)).astype(BF16)
            du_ref[...] = (da * u_ref[...].astype(F32)).astype(BF16)
        else:
            o_ref[...] = da
            ob_ref[...] = da.astype(BF16)

    tile = pl.BlockSpec((tm, tn), lambda i, j: (i, j))
    in_specs = [pl.BlockSpec((tm, D), lambda i, j: (i, 0)), pl.BlockSpec((tn, D), lambda i, j: (j, 0))]
    args = [dyb, w]
    if swiglu:
        in_specs += [tile, tile]
        args += [gate, up]
        out_shape = [jax.ShapeDtypeStruct((T, N), BF16)] * 2
    else:
        out_shape = [jax.ShapeDtypeStruct((T, N), F32), jax.ShapeDtypeStruct((T, N), BF16)]
    return pl.pallas_call(
        body, name=name, grid=(T // tm, N // tn), in_specs=in_specs, out_specs=[tile, tile],
        out_shape=out_shape, compiler_params=_params(2),
    )(*args)


def _nt_rms_bwd(name, As, Ws, x, g, dres):
    T, K = As[0].shape
    D = x.shape[1]
    na = len(As)
    tm = _pick(T, (ROW_TILE, 512, 256, 128))
    tk = _pick(K, (1024 // na, 512, 256, 128))
    nk = K // tk

    def body(*refs):
        a_refs = refs[:na]
        w_refs = refs[na:2 * na]
        x_hbm, g_ref, dres_hbm, acc, dxb_ref, dg_ref, x_ref, dres_ref, sems = refs[2 * na:]
        dx_ref = acc
        i, k = pl.program_id(0), pl.program_id(1)

        def row_copies():
            rows = pl.ds(pl.multiple_of(i * tm, tm), tm)
            return (pltpu.make_async_copy(x_hbm.at[rows, :], x_ref, sems.at[0]),
                    pltpu.make_async_copy(dres_hbm.at[rows, :], dres_ref, sems.at[1]))

        @pl.when(k == 0)
        def _():
            acc[...] = jnp.zeros_like(acc)
            for cp in row_copies():
                cp.start()

        for c0 in range(0, D, ACC_COLS):
            cols = slice(c0, min(c0 + ACC_COLS, D))
            part = _nt(a_refs[0][...], w_refs[0][cols, :])
            for a_ref, w_ref in zip(a_refs[1:], w_refs[1:]):
                part = part + _nt(a_ref[...], w_ref[cols, :])
            acc[:, cols] += part

        @pl.when(k == nk - 1)
        def _():
            for cp in row_copies():
                cp.wait()
            part = jnp.zeros((1, D), F32)
            for r0 in range(0, tm, EPILOGUE_ROWS):
                rows = slice(r0, r0 + min(EPILOGUE_ROWS, tm))
                dh = acc[rows, :]
                xv = x_ref[rows, :]
                r = _rms_r(xv)
                gd = dh * g_ref[...]
                dx = dres_ref[rows, :] + r * gd - xv * (r * r * r) * jnp.mean(gd * xv, axis=-1, keepdims=True)
                dx_ref[rows, :] = dx
                dxb_ref[rows, :] = dx.astype(BF16)
                part = part + jnp.sum(dh * xv * r, axis=0, keepdims=True)

            @pl.when(i == 0)
            def _():
                dg_ref[...] = part

            @pl.when(i > 0)
            def _():
                dg_ref[...] += part

    row = pl.BlockSpec((tm, D), lambda i, k: (i, 0), pipeline_mode=pl.Buffered(1))
    vec = pl.BlockSpec((1, D), lambda i, k: (0, 0))
    return pl.pallas_call(
        body, name=name, grid=(T // tm, nk),
        in_specs=[pl.BlockSpec((tm, tk), lambda i, k: (i, k))] * na
        + [pl.BlockSpec((D, tk), lambda i, k: (0, k))] * na + [ANY, vec, ANY],
        out_specs=[row, row, vec],
        out_shape=[jax.ShapeDtypeStruct((T, D), F32), jax.ShapeDtypeStruct((T, D), BF16),
                   jax.ShapeDtypeStruct((1, D), F32)],
        scratch_shapes=[pltpu.VMEM((tm, D), F32), pltpu.VMEM((tm, D), F32), pltpu.SemaphoreType.DMA((2,))],
        compiler_params=_params(2),
    )(*As, *Ws, x, g, dres)


def _tn_matmul(name, a, b, scale=1.0):
    T, M = a.shape
    N = b.shape[1]
    tn = _pick(N, (2048, 1408, 1280, 1024, 512, 256, 128))
    tm = _pick(M, tuple(c for c in (2048, 1408, 1024, 512, 256, 128) if c * tn <= TN_ACC_ELEMS))
    tk = _pick(T, (1024, 512, 256, 128))
    nk = T // tk

    def body(a_ref, b_ref, o_ref, acc):
        k = pl.program_id(2)

        @pl.when(k == 0)
        def _():
            acc[...] = jnp.zeros_like(acc)

        for r0 in range(0, tm, ACC_COLS):
            rows = slice(r0, min(r0 + ACC_COLS, tm))
            acc[rows, :] += _tn(a_ref[:, rows], b_ref[...])

        @pl.when(k == nk - 1)
        def _():
            o_ref[...] = (acc[...] * scale).astype(BF16)

    return pl.pallas_call(
        body, name=name, grid=(M // tm, N // tn, nk),
        in_specs=[pl.BlockSpec((tk, tm), lambda i, j, k: (k, i)), pl.BlockSpec((tk, tn), lambda i, j, k: (k, j))],
        out_specs=pl.BlockSpec((tm, tn), lambda i, j, k: (i, j)),
        out_shape=jax.ShapeDtypeStruct((M, N), BF16),
        scratch_shapes=[pltpu.VMEM((tm, tn), F32)], compiler_params=_params(3),
    )(a, b)


CONV_ROWS = 128
ROW_CHUNK = 32
LANE_CHUNK = 256


SUBLANES = 8


def _fill_shifts(sh, buf, rows):
    for p in range(1, SUBLANES):
        sh[p, 0:rows - SUBLANES, :] = buf[p:p + rows - SUBLANES, :]


def _tap(buf, sh, s, n, cols):
    p = s % SUBLANES
    return buf[s:s + n, cols] if p == 0 else sh[p, s - p:s - p + n, cols]


def _conv_fwd(name, z, w32, b, lg, lb, C):
    T = z.shape[0]
    tc = CONV_ROWS
    ntap = 31
    lc = _pick(C, (LANE_CHUNK, LANES))
    rpb = tc // HALO

    def body(zc_ref, zp_ref, w_ref, b_ref, lg_ref, lb_ref, yc_ref, ycv_ref, vbuf, ybuf, vsh):
        i = pl.program_id(0)
        zc = zc_ref[...]
        zp = zp_ref[...]
        vbuf[HALO:HALO + tc, :] = zc[:, :C] * _sigmoid(zc[:, C:])
        vbuf[0:HALO, :] = jnp.where(i > 0, zp[:, :C] * _sigmoid(zp[:, C:]), 0.0)
        _fill_shifts(vsh, vbuf, tc + HALO)
        for r0 in range(0, tc, ROW_CHUNK):
            for c0 in range(0, C, lc):
                cols = slice(c0, c0 + lc)
                acc = jnp.zeros((ROW_CHUNK, lc), F32) + b_ref[:, cols]
                for k in range(ntap):
                    acc = acc + w_ref[k:k + 1, cols] * _tap(vbuf, vsh, r0 + 2 + k, ROW_CHUNK, cols)
                ybuf[r0:r0 + ROW_CHUNK, cols] = acc
        y = ybuf[...]
        ycv_ref[...] = y
        mu = jnp.mean(y, axis=-1, keepdims=True)
        yc = y - mu
        rstd = lax.rsqrt(jnp.mean(yc * yc, axis=-1, keepdims=True) + EPS)
        ln = yc * rstd * lg_ref[...] + lb_ref[...]
        yc_ref[...] = (ln * _sigmoid(ln)).astype(BF16)

    vec = pl.BlockSpec((1, C), lambda i: (0, 0))
    return pl.pallas_call(
        body, name=name, grid=(T // tc,),
        in_specs=[pl.BlockSpec((tc, 2 * C), lambda i: (i, 0)),
                  pl.BlockSpec((HALO, 2 * C), lambda i: (jnp.maximum(i * rpb - 1, 0), 0)),
                  pl.BlockSpec((HALO, C), lambda i: (0, 0)), vec, vec, vec],
        out_specs=[pl.BlockSpec((tc, C), lambda i: (i, 0))] * 2,
        out_shape=[jax.ShapeDtypeStruct((T, 2 * C), BF16), jax.ShapeDtypeStruct((T, C), F32)],
        scratch_shapes=[pltpu.VMEM((tc + HALO, C), F32), pltpu.VMEM((tc, C), F32),
                        pltpu.VMEM((SUBLANES, tc + HALO, C), F32)],
        compiler_params=_params(1),
    )(z, z, w32, b, lg, lb)


def _conv_bwd(name, z, ycv, dycat, w32, lg, lb, C):
    T = z.shape[0]
    tc = CONV_ROWS
    ntap = 31
    lc = _pick(C, (LANE_CHUNK, LANES))
    rpb = tc // HALO
    nstep = T // tc
    nhb = T // HALO

    def ln_bwd(dyc, y, lgv, lbv):
        mu = jnp.mean(y, axis=-1, keepdims=True)
        yc = y - mu
        rstd = lax.rsqrt(jnp.mean(yc * yc, axis=-1, keepdims=True) + EPS)
        yn = yc * rstd
        ln = yn * lgv + lbv
        sg = _sigmoid(ln)
        dln = dyc * (sg * (1.0 + ln * (1.0 - sg)))
        dyn = dln * lgv
        dy = rstd * (dyn - jnp.mean(dyn, axis=-1, keepdims=True)
                     - yn * jnp.mean(dyn * yn, axis=-1, keepdims=True))
        return dy, dln, yn

    def body(zc_ref, zp_ref, y_ref, yn_ref, d_ref, dn_ref, w_ref, lg_ref, lb_ref,
             dz_ref, dw_ref, db_ref, dlg_ref, dlb_ref, vbuf, dbuf, dvbuf, dwacc, vsh, dsh):
        i = pl.program_id(0)
        lgv, lbv = lg_ref[...], lb_ref[...]
        zc = zc_ref[...]
        zp = zp_ref[...]
        a = zc[:, :C]
        sgt = _sigmoid(zc[:, C:])
        vbuf[HALO:HALO + tc, :] = a * sgt
        vbuf[0:HALO, :] = jnp.where(i > 0, zp[:, :C] * _sigmoid(zp[:, C:]), 0.0)
        dy, dln, yn = ln_bwd(d_ref[...], y_ref[...], lgv, lbv)
        dbuf[0:tc, :] = dy
        dyn_, _, _ = ln_bwd(dn_ref[...], yn_ref[...], lgv, lbv)
        dbuf[tc:tc + HALO, :] = jnp.where(i < nstep - 1, dyn_, 0.0)

        @pl.when(i == 0)
        def _():
            dwacc[...] = jnp.zeros_like(dwacc)
            db_ref[...] = jnp.zeros_like(db_ref)
            dlg_ref[...] = jnp.zeros_like(dlg_ref)
            dlb_ref[...] = jnp.zeros_like(dlb_ref)

        db_ref[...] += jnp.sum(dy, axis=0, keepdims=True)
        dlg_ref[...] += jnp.sum(dln * yn, axis=0, keepdims=True)
        dlb_ref[...] += jnp.sum(dln, axis=0, keepdims=True)

        _fill_shifts(vsh, vbuf, tc + HALO)
        _fill_shifts(dsh, dbuf, tc + HALO)
        for r0 in range(0, tc, ROW_CHUNK):
            for c0 in range(0, C, lc):
                cols = slice(c0, c0 + lc)
                dcur = dbuf[r0:r0 + ROW_CHUNK, cols]
                acc = jnp.zeros((ROW_CHUNK, lc), F32)
                for k in range(ntap):
                    acc = acc + w_ref[k:k + 1, cols] * _tap(dbuf, dsh, r0 + 30 - k, ROW_CHUNK, cols)
                    prod = dcur * _tap(vbuf, vsh, r0 + 2 + k, ROW_CHUNK, cols)
                    red = prod[0:8]
                    for q in range(8, ROW_CHUNK, 8):
                        red = red + prod[q:q + 8]
                    dwacc[8 * k:8 * k + 8, cols] += red
                dvbuf[r0:r0 + ROW_CHUNK, cols] = acc
        dv = dvbuf[...]
        dz_ref[:, :C] = (dv * sgt).astype(BF16)
        dz_ref[:, C:] = (dv * a * sgt * (1.0 - sgt)).astype(BF16)

        @pl.when(i == nstep - 1)
        def _():
            for k in range(ntap):
                dw_ref[k:k + 1, :] = jnp.sum(dwacc[8 * k:8 * k + 8, :], axis=0, keepdims=True)
            dw_ref[ntap:HALO, :] = jnp.zeros((HALO - ntap, C), F32)

    vec = pl.BlockSpec((1, C), lambda i: (0, 0))
    cur = pl.BlockSpec((tc, C), lambda i: (i, 0))
    nxt = pl.BlockSpec((HALO, C), lambda i: (jnp.minimum((i + 1) * rpb, nhb - 1), 0))
    return pl.pallas_call(
        body, name=name, grid=(nstep,),
        in_specs=[pl.BlockSpec((tc, 2 * C), lambda i: (i, 0)),
                  pl.BlockSpec((HALO, 2 * C), lambda i: (jnp.maximum(i * rpb - 1, 0), 0)),
                  cur, nxt, cur, nxt, pl.BlockSpec((HALO, C), lambda i: (0, 0)), vec, vec],
        out_specs=[pl.BlockSpec((tc, 2 * C), lambda i: (i, 0)), pl.BlockSpec((HALO, C), lambda i: (0, 0)),
                   vec, vec, vec],
        out_shape=[jax.ShapeDtypeStruct((T, 2 * C), BF16), jax.ShapeDtypeStruct((HALO, C), F32),
                   jax.ShapeDtypeStruct((1, C), F32), jax.ShapeDtypeStruct((1, C), F32),
                   jax.ShapeDtypeStruct((1, C), F32)],
        scratch_shapes=[pltpu.VMEM((tc + HALO, C), F32), pltpu.VMEM((tc + HALO, C), F32),
                        pltpu.VMEM((tc, C), F32), pltpu.VMEM((8 * HALO, C), F32),
                        pltpu.VMEM((SUBLANES, tc + HALO, C), F32), pltpu.VMEM((SUBLANES, tc + HALO, C), F32)],
        compiler_params=_params(1),
    )(z, z, ycv, ycv, dycat, dycat, w32, lg, lb)


def _seg_sum(u, bmat):
    hi = u.astype(BF16)
    lo = (u - hi.astype(F32)).astype(BF16)
    parts = [_nn(hi[:, c:c + LANES], bmat) + _nn(lo[:, c:c + LANES], bmat) for c in range(0, u.shape[1], LANES)]
    return jnp.concatenate(parts, axis=1)


def _attn_prep(name, z, gq, gk, bmat, A, c0, hd):
    T = z.shape[0]
    tm = _pick(T, (256, 128))

    def body(zq_ref, zk_ref, zv_ref, gq_ref, gk_ref, b_ref, o_ref):
        bm = b_ref[...]
        for idx, (z_ref, g_ref) in enumerate(((zq_ref, gq_ref), (zk_ref, gk_ref))):
            zv = z_ref[...]
            r = lax.rsqrt(_seg_sum(zv * zv, bm) * (1.0 / hd) + EPS)
            o_ref[:, idx * A:(idx + 1) * A] = zv * r * g_ref[...]
        o_ref[:, 2 * A:] = zv_ref[...]

    vec = pl.BlockSpec((1, A), lambda i: (0, 0))
    return pl.pallas_call(
        body, name=name, grid=(T // tm,),
        in_specs=[pl.BlockSpec((tm, A), lambda i: (i, c0)), pl.BlockSpec((tm, A), lambda i: (i, c0 + 1)),
                  pl.BlockSpec((tm, A), lambda i: (i, c0 + 2)), vec, vec,
                  pl.BlockSpec((LANES, LANES), lambda i: (0, 0))],
        out_specs=pl.BlockSpec((tm, 3 * A), lambda i: (i, 0)),
        out_shape=jax.ShapeDtypeStruct((T, 3 * A), F32), compiler_params=_params(1),
    )(z, z, z, gq, gk, bmat)


QK_SCALE = 0.125
ATTN_UNROLL = 16
ATTN_FWD_UNROLL = 16


def _fill_bias(bias, sl_ref, hp, d):
    qi = lax.broadcasted_iota(jnp.int32, (WINDOW, 2 * WINDOW), 0)
    kj = lax.broadcasted_iota(jnp.int32, (WINDOW, 2 * WINDOW), 1)
    dist = WINDOW + qi - kj
    inband = (dist >= 0) & (dist <= WINDOW)
    distf = dist.astype(F32)
    for hh in range(2):
        b = jnp.where(inband, -(sl_ref[2 * hp + hh] * d) * distf, NEG)
        bias[2 * hh + 1] = b
        bias[2 * hh] = jnp.where(kj >= WINDOW, b, NEG)


CHUNK = WINDOW * DILATIONS[-1]
assert DILATIONS[0] == 1


def _deinterleave(dst, src, d, rows, dst_pitch, dst_off, src_off):
    for r in range(d):
        if d == 1:
            val = src[src_off:src_off + rows, :]
        else:
            val = src[pl.ds(src_off + r, rows, stride=d), :]
        lo = r * dst_pitch + dst_off
        dst[lo:lo + rows, :] = val.astype(dst.dtype)


def _interleave_add(dst, start, src, d, rows, src_pitch, src_off):
    for r in range(d):
        lo = r * src_pitch + src_off
        idx = pl.ds(start, rows) if d == 1 else pl.ds(start + r, rows, stride=d)
        dst[idx, :] += src[lo:lo + rows, :]


def _attn_fwd(name, qkv, slopes, ycat, A):
    T = qkv.shape[0]
    hpn, nch, nblk = A // LANES, T // CHUNK, CHUNK // WINDOW
    nbranch = len(DILATIONS)
    yoff = (ycat.shape[1] - A) // LANES

    def body(*refs):
        sl_ref, q_ref, k_ref, kp_ref, v_ref, vp_ref, _, y_ref, lg_ref, qd, kd, vd, od, ld, bias = refs[:15]
        onat, lnat = refs[15:15 + nbranch], refs[15 + nbranch:]
        hp, ch = pl.program_id(0), pl.program_id(1)
        lane = lax.broadcasted_iota(jnp.int32, (1, LANES), 1)
        first = lane < (LANES // 2)
        for bi, d in enumerate(DILATIONS):
            Ld = CHUNK // d
            seg = Ld + WINDOW
            nbr = Ld // WINDOW
            _deinterleave(qd, q_ref, d, Ld, Ld, 0, 0)
            for dst, cur, prev in ((kd, k_ref, kp_ref), (vd, v_ref, vp_ref)):
                _deinterleave(dst, prev, d, WINDOW, seg, 0, CHUNK - WINDOW * d)
                _deinterleave(dst, cur, d, Ld, seg, WINDOW, 0)
            _fill_bias(bias, sl_ref, hp, d)
            ob, lb = (onat[bi], lnat[bi]) if d == 1 else (od, ld)

            def step(it, carry, Ld=Ld, seg=seg, nbr=nbr, ob=ob, lb=lb):
                r, nl = it // nbr, it % nbr
                q0 = pl.multiple_of(r * Ld + nl * WINDOW, WINDOW)
                k0 = pl.multiple_of(r * seg + nl * WINDOW, WINDOW)
                later = jnp.where(ch * nbr + nl > 0, 1, 0)
                qb = qd[pl.ds(q0, WINDOW), :]
                k2 = kd[pl.ds(k0, 2 * WINDOW), :]
                v2 = vd[pl.ds(k0, 2 * WINDOW), :]
                res = []
                for hh in range(2):
                    mh = first if hh == 0 else jnp.logical_not(first)
                    s = _nt(jnp.where(mh, qb, jnp.zeros_like(qb)), k2) + bias[2 * hh + later]
                    mx = jnp.max(s, axis=-1, keepdims=True)
                    p = jnp.exp(s - mx)
                    den = jnp.sum(p, axis=-1, keepdims=True)
                    res.append((_nn(p.astype(BF16), v2) / den, mx + jnp.log(den)))
                ob[pl.ds(q0, WINDOW), :] = jnp.where(first, res[0][0], res[1][0])
                lb[pl.ds(q0, WINDOW), :] = jnp.where(first, res[0][1], res[1][1])
                return carry

            lax.fori_loop(0, nblk, step, 0, unroll=ATTN_FWD_UNROLL)
            if d > 1:
                for r in range(d):
                    onat[bi][pl.ds(r, Ld, stride=d), :] = od[r * Ld:(r + 1) * Ld, :]
                    lnat[bi][pl.ds(r, Ld, stride=d), :] = ld[r * Ld:(r + 1) * Ld, :]
        ls = [l[...] for l in lnat]
        mx = ls[0]
        for v in ls[1:]:
            mx = jnp.maximum(mx, v)
        es = [jnp.exp(v - mx) for v in ls]
        den = es[0]
        for e in es[1:]:
            den = den + e
        out = es[0] * onat[0][...]
        for e, o in zip(es[1:], onat[1:]):
            out = out + e * o[...]
        y_ref[...] = (out / den).astype(BF16)
        lg_ref[...] = mx + jnp.log(den)

    blk = lambda m: pl.BlockSpec((CHUNK, LANES), m)
    cur = lambda which: blk(lambda hp, ch: (ch, which * hpn + hp))
    prev = lambda which: blk(lambda hp, ch: (jnp.maximum(ch - 1, 0), which * hpn + hp))
    omap = blk(lambda hp, ch: (ch, hp))
    f32buf = pltpu.VMEM((CHUNK, LANES), F32)
    return pl.pallas_call(
        body, name=name, grid=(hpn, nch),
        in_specs=[pl.BlockSpec(memory_space=pltpu.SMEM), cur(0), cur(1), prev(1), cur(2), prev(2), ANY],
        out_specs=[blk(lambda hp, ch: (ch, yoff + hp)), omap],
        out_shape=[jax.ShapeDtypeStruct(ycat.shape, BF16), jax.ShapeDtypeStruct((T, A), F32)],
        input_output_aliases={6: 0},
        scratch_shapes=[pltpu.VMEM((CHUNK, LANES), BF16), pltpu.VMEM((2 * CHUNK, LANES), BF16),
                        pltpu.VMEM((2 * CHUNK, LANES), BF16), f32buf, f32buf,
                        pltpu.VMEM((4, WINDOW, 2 * WINDOW), F32)] + [f32buf] * (2 * nbranch),
        compiler_params=_params(2),
    )(slopes, qkv, qkv, qkv, qkv, qkv, ycat)


def _attn_bwd(name, qkv, dycat, ycat, lg, slopes, bmat, A, catoff):
    T = qkv.shape[0]
    hpn, nch, nblk = A // LANES, T // CHUNK, CHUNK // WINDOW
    co = catoff // LANES

    def body(sl_ref, q_ref, k_ref, kp_ref, v_ref, vp_ref, do_ref, o_ref, l_ref, b_ref, dq_ref, dk_ref, dv_ref,
             qd, kd, vd, dod, ddn, ddd, ldd, dqd, dkd, dvd, bias):
        hp, ch = pl.program_id(0), pl.program_id(1)
        lane = lax.broadcasted_iota(jnp.int32, (1, LANES), 1)
        first = lane < (LANES // 2)
        ddn[...] = _seg_sum(do_ref[...] * o_ref[...].astype(F32), b_ref[...])

        @pl.when(ch == 0)
        def _():
            dk_ref[...] = jnp.zeros_like(dk_ref)
            dv_ref[...] = jnp.zeros_like(dv_ref)

        base = ch * CHUNK
        for d in DILATIONS:
            Ld = CHUNK // d
            seg = Ld + WINDOW
            nbr = Ld // WINDOW
            _deinterleave(qd, q_ref, d, Ld, Ld, 0, 0)
            _deinterleave(dod, do_ref, d, Ld, Ld, 0, 0)
            if d > 1:
                _deinterleave(ddd, ddn, d, Ld, Ld, 0, 0)
                _deinterleave(ldd, l_ref, d, Ld, Ld, 0, 0)
            dsrc, lsrc, dqdst = (ddn, l_ref, dq_ref) if d == 1 else (ddd, ldd, dqd)
            for dst, cur, prev in ((kd, k_ref, kp_ref), (vd, v_ref, vp_ref)):
                _deinterleave(dst, prev, d, WINDOW, seg, 0, CHUNK - WINDOW * d)
                _deinterleave(dst, cur, d, Ld, seg, WINDOW, 0)
            dkd[0:d * seg, :] = jnp.zeros((d * seg, LANES), F32)
            dvd[0:d * seg, :] = jnp.zeros((d * seg, LANES), F32)
            _fill_bias(bias, sl_ref, hp, d)

            def step(it, carry, Ld=Ld, seg=seg, nbr=nbr, dsrc=dsrc, lsrc=lsrc, dqdst=dqdst):
                r, nl = it // nbr, it % nbr
                q0 = pl.multiple_of(r * Ld + nl * WINDOW, WINDOW)
                k0 = pl.multiple_of(r * seg + nl * WINDOW, WINDOW)
                later = jnp.where(ch * nbr + nl > 0, 1, 0)
                qb = qd[pl.ds(q0, WINDOW), :]
                k2 = kd[pl.ds(k0, 2 * WINDOW), :]
                v2 = vd[pl.ds(k0, 2 * WINDOW), :]
                dob = dod[pl.ds(q0, WINDOW), :]
                dd = dsrc[pl.ds(q0, WINDOW), :]
                lb = lsrc[pl.ds(q0, WINDOW), :]
                dk2 = jnp.zeros((2 * WINDOW, LANES), F32)
                dv2 = jnp.zeros((2 * WINDOW, LANES), F32)
                dqs = []
                for hh in range(2):
                    mh = first if hh == 0 else jnp.logical_not(first)
                    qh = jnp.where(mh, qb, jnp.zeros_like(qb))
                    doh = jnp.where(mh, dob, jnp.zeros_like(dob))
                    lcol = lb[:, hh * (LANES // 2):hh * (LANES // 2) + 1]
                    p = jnp.exp(_nt(qh, k2) + bias[2 * hh + later] - lcol)
                    dcol = dd[:, hh * (LANES // 2):hh * (LANES // 2) + 1]
                    ds = (p * (_nt(doh, v2) - dcol)).astype(BF16)
                    dqs.append(_nn(ds, k2))
                    dk2 = dk2 + _tn(ds, qh)
                    dv2 = dv2 + _tn(p.astype(BF16), doh)
                dqdst[pl.ds(q0, WINDOW), :] = jnp.where(first, dqs[0], dqs[1])
                dkd[pl.ds(k0, 2 * WINDOW), :] += dk2
                dvd[pl.ds(k0, 2 * WINDOW), :] += dv2
                return carry

            lax.fori_loop(0, nblk, step, 0, unroll=ATTN_UNROLL)
            if d > 1:
                _interleave_add(dq_ref, 0, dqd, d, Ld, Ld, 0)
            for acc, out in ((dkd, dk_ref), (dvd, dv_ref)):
                _interleave_add(out, base, acc, d, Ld, seg, WINDOW)

                @pl.when(ch > 0)
                def _(acc=acc, out=out, d=d, seg=seg):
                    _interleave_add(out, base - WINDOW * d, acc, d, WINDOW, seg, 0)

    blk = lambda m: pl.BlockSpec((CHUNK, LANES), m)
    cur = lambda which: blk(lambda hp, ch: (ch, which * hpn + hp))
    prev = lambda which: blk(lambda hp, ch: (jnp.maximum(ch - 1, 0), which * hpn + hp))
    omap = blk(lambda hp, ch: (ch, hp))
    full = pl.BlockSpec((T, LANES), lambda hp, ch: (0, hp))
    f32buf = pltpu.VMEM((CHUNK, LANES), F32)
    bf16buf = pltpu.VMEM((CHUNK, LANES), BF16)
    return pl.pallas_call(
        body, name=name, grid=(hpn, nch),
        in_specs=[pl.BlockSpec(memory_space=pltpu.SMEM), cur(0), cur(1), prev(1), cur(2), prev(2),
                  blk(lambda hp, ch: (ch, co + hp)), blk(lambda hp, ch: (ch, co + hp)), omap,
                  pl.BlockSpec((LANES, LANES), lambda hp, ch: (0, 0))],
        out_specs=[omap, full, full],
        out_shape=[jax.ShapeDtypeStruct((T, A), F32)] * 3,
        scratch_shapes=[bf16buf, pltpu.VMEM((2 * CHUNK, LANES), BF16), pltpu.VMEM((2 * CHUNK, LANES), BF16),
                        bf16buf, f32buf, f32buf, f32buf, f32buf,
                        pltpu.VMEM((2 * CHUNK, LANES), F32), pltpu.VMEM((2 * CHUNK, LANES), F32),
                        pltpu.VMEM((4, WINDOW, 2 * WINDOW), F32)],
        compiler_params=_params(2),
    )(slopes, qkv, qkv, qkv, qkv, qkv, dycat, ycat, lg, bmat)


def _attn_bwd_combine(name, dzc, dqs, dks, dvs, z, gq, gk, bmat, fmat, A, c0, hd):
    T = z.shape[0]
    tm = _pick(T, (256, 128))
    nbr = len(dqs)
    W0 = dzc.shape[1]

    def body(*refs):
        dq_refs, dk_refs, dv_refs = refs[:nbr], refs[nbr:2 * nbr], refs[2 * nbr:3 * nbr]
        zq_ref, zk_ref, gq_ref, gk_ref, b_ref, f_ref, dzc_ref, dz_ref, dgq_ref, dgk_ref = refs[3 * nbr:]
        i = pl.program_id(0)
        bm = b_ref[...]
        dz_ref[:, :W0] = dzc_ref[...]

        def tot(rs):
            t = rs[0][...]
            for r in rs[1:]:
                t = t + r[...]
            return t

        for idx, (d_refs, z_ref, g_ref, dg_ref, gscale) in enumerate(
                ((dq_refs, zq_ref, gq_ref, dgq_ref, QK_SCALE), (dk_refs, zk_ref, gk_ref, dgk_ref, 1.0))):
            dy = tot(d_refs)
            zv = z_ref[...]
            r = lax.rsqrt(_seg_sum(zv * zv, bm) * (1.0 / hd) + EPS)
            gd = dy * g_ref[...]
            mean = _seg_sum(gd * zv, bm) * (1.0 / hd)
            dz_ref[:, W0 + idx * A:W0 + (idx + 1) * A] = (r * gd - zv * (r * r * r) * mean).astype(BF16)
            part = jnp.sum(dy * zv * r, axis=0, keepdims=True) * gscale

            @pl.when(i == 0)
            def _():
                dg_ref[...] = part

            @pl.when(i > 0)
            def _():
                dg_ref[...] += part

        dz_ref[:, W0 + 2 * A:] = tot(dv_refs).astype(BF16)

        @pl.when(i == T // tm - 1)
        def _():
            fm = f_ref[...]
            for dg_ref in (dgq_ref, dgk_ref):
                v = jnp.broadcast_to(dg_ref[...], (8, A))
                hi = v.astype(BF16)
                mid = (v - hi.astype(F32)).astype(BF16)
                lo = (v - hi.astype(F32) - mid.astype(F32)).astype(BF16)
                dg_ref[...] = (_nn(hi, fm) + _nn(mid, fm) + _nn(lo, fm))[0:1]

    blk = pl.BlockSpec((tm, A), lambda i: (i, 0))
    vec = pl.BlockSpec((1, A), lambda i: (0, 0))
    return pl.pallas_call(
        body, name=name, grid=(T // tm,),
        in_specs=[blk] * (3 * nbr) + [pl.BlockSpec((tm, A), lambda i: (i, c0)),
                                      pl.BlockSpec((tm, A), lambda i: (i, c0 + 1)), vec, vec,
                                      pl.BlockSpec((LANES, LANES), lambda i: (0, 0)),
                                      pl.BlockSpec((A, A), lambda i: (0, 0)),
                                      pl.BlockSpec((tm, W0), lambda i: (i, 0))],
        out_specs=[pl.BlockSpec((tm, W0 + 3 * A), lambda i: (i, 0)), vec, vec],
        out_shape=[jax.ShapeDtypeStruct((T, W0 + 3 * A), BF16), jax.ShapeDtypeStruct((1, A), F32),
                   jax.ShapeDtypeStruct((1, A), F32)],
        compiler_params=_params(1),
    )(*dqs, *dks, *dvs, z, z, gq, gk, bmat, fmat, dzc)


def _local_step(x, tgt, S, comm, hd):
    T, D = x.shape
    C = S["conv_b_dw"].shape[1]
    A = C
    H = A // hd
    Dmix = C + A
    c0 = (2 * C) // A
    slopes = 2.0 ** (-ALIBI_MAX_BIAS * jnp.arange(1, H + 1, dtype=F32) / H)
    seg = jnp.arange(LANES) // hd
    bmat = (seg[:, None] == seg[None, :]).astype(BF16)
    pos_in_head = jnp.arange(A) % hd
    fmat = (pos_in_head[:, None] == pos_in_head[None, :]).astype(BF16)
    gq = jnp.tile(S["q_norm_g"], (1, H)) * QK_SCALE
    gk = jnp.tile(S["k_norm_g"], (1, H))

    wg1, wu1 = comm.weights(("ffn1_w_gate", "ffn1_w_up"), None)
    h1, gate1, up1, a1 = _norm_matmul("ffn1_up", x, comm.tie(S["ffn1_norm_g"]), [wg1, wu1], True)
    wd1, win, w32 = comm.weights(("ffn1_w_down", "w_in", "conv_w32"), a1)
    x1 = _matmul_res("ffn1_down", a1, wd1, x, 0.5)
    h2, z = _norm_matmul("mix_in", x1, S["mix_norm_g"], [win], False)
    yc, ycv = _conv_fwd("conv_fwd", z, w32, S["conv_b_dw"], S["conv_ln_g"], S["conv_ln_b"], C)
    qkv = _attn_prep("attn_prep", z, gq, gk, bmat, A, c0, hd)
    ycat, lg = _attn_fwd("attn_fwd", qkv, slopes, yc, A)
    wout, wg2, wu2, wd2 = comm.weights(("w_out", "ffn2_w_gate", "ffn2_w_up", "ffn2_w_down"), lg)
    x2 = _matmul_res("mix_out", ycat, wout, x1, 1.0)
    h3, gate2, up2, a2 = _norm_matmul("ffn2_up", x2, S["ffn2_norm_g"], [wg2, wu2], True)
    dx3, dx3b, lossvec = _matmul_res("ffn2_down_loss", a2, wd2, x2, 0.5, tgt=tgt)

    G = {}
    dgate2, dup2 = _nt_matmul("ffn2_dact", dx3b, wd2, 0.5, gate2, up2)
    comm.reduce_begin("ffn2", {"ffn2_w_down": _tn_matmul("ffn2_dwd", a2, dx3b, 0.5),
                               "ffn2_w_gate": _tn_matmul("ffn2_dwg", h3, dgate2),
                               "ffn2_w_up": _tn_matmul("ffn2_dwu", h3, dup2)}, split=True)
    dx2, dx2b, G["ffn2_norm_g"] = _nt_rms_bwd("ffn2_dx", [dgate2, dup2], [wg2, wu2],
                                              x2, comm.tie(S["ffn2_norm_g"]), dx3)
    comm.reduce_scatter("ffn2", dx2b)
    dwout = _tn_matmul("mix_dwout", ycat, dx2b)
    dycat, _ = _nt_matmul("mix_dycat", dx2b, wout)
    dzc, G["conv_w32"], G["conv_b_dw"], G["conv_ln_g"], G["conv_ln_b"] = _conv_bwd(
        "conv_bwd", z, ycv, dycat, w32, S["conv_ln_g"], S["conv_ln_b"], C)
    dq, dk, dv = _attn_bwd("attn_bwd", qkv, dycat, ycat, lg, slopes, bmat, A, C)
    dz, G["q_norm_g"], G["k_norm_g"] = _attn_bwd_combine(
        "attn_bwd_combine", dzc, [dq], [dk], [dv], z, gq, gk, bmat, fmat, A, c0, hd)
    comm.reduce_end("ffn2", dz)
    comm.reduce_begin("mix", {"w_out": dwout, "w_in": _tn_matmul("mix_dwin", h2, dz)}, split=True)
    dx1, dx1b, G["mix_norm_g"] = _nt_rms_bwd("mix_dx", [dz], [win], x1, comm.tie(S["mix_norm_g"]), dx2)
    comm.reduce_scatter("mix", dx1b)
    dgate1, dup1 = _nt_matmul("ffn1_dact", dx1b, wd1, 0.5, gate1, up1)
    dwd1 = _tn_matmul("ffn1_dwd", a1, dx1b, 0.5)
    dwg1 = _tn_matmul("ffn1_dwg", h1, dgate1)
    dwu1 = _tn_matmul("ffn1_dwu", h1, dup1)
    comm.reduce_end("mix", dwu1)
    comm.reduce_begin("ffn1", {"ffn1_w_down": dwd1, "ffn1_w_gate": dwg1, "ffn1_w_up": dwu1})
    dx0, _, G["ffn1_norm_g"] = _nt_rms_bwd("ffn1_dx", [dgate1, dup1], [wg1, wu1],
                                           x, comm.tie(S["ffn1_norm_g"]), dx1)
    comm.reduce_end("ffn1", dx0)
    return lossvec, dx0, G


BIG = (("ffn1_w_gate", 1), ("ffn1_w_up", 1), ("ffn1_w_down", 0), ("w_in", 1), ("w_out", 0),
       ("ffn2_w_gate", 1), ("ffn2_w_up", 1), ("ffn2_w_down", 0))
AXIS = dict(BIG)
FLIPS = ((1, 0), (0, 1), (1, 1))
HBM = pl.BlockSpec(memory_space=pltpu.HBM)
SEM = pl.BlockSpec(memory_space=pltpu.SEMAPHORE)
EFFECT = pltpu.SideEffectType.DATAFLOW_SIDE_EFFECTING
TOKEN = jax.ShapeDtypeStruct((8, LANES), F32)


def _window(ref, shape, axis, slab=None, half=None):
    idx = [pl.ds(0, shape[0]), pl.ds(0, shape[1])]
    if slab is not None:
        n = shape[axis] // 4
        idx[axis] = pl.ds(pl.multiple_of(slab * n, 8), n)
    if half is not None:
        hs = shape[1 - axis] // 2
        idx[1 - axis] = pl.ds(pl.multiple_of(half * hs, 8), hs)
    return ref.at[idx[0], idx[1]]


def _position():
    return lax.axis_index("x"), lax.axis_index("y"), lax.axis_index("c")


def _half_shape(shape, axis):
    return (shape[0] // 2, shape[1]) if axis == 1 else (shape[0], shape[1] // 2)


def _slab_shape(shape, axis):
    return (shape[0], shape[1] // 4) if axis == 1 else (shape[0] // 4, shape[1])


def _piece_shape(shape, axis):
    return _half_shape(_slab_shape(shape, axis), axis)


def _full_shape(shard, axis):
    return (shard.shape[0], shard.shape[1] * 4) if axis == 1 else (shard.shape[0] * 4, shard.shape[1])


def _hbm(a):
    return pltpu.with_memory_space_constraint(a, pltpu.HBM)


def _remote(src, dst, send_sem, recv_sem, to):
    return pltpu.make_async_remote_copy(src_ref=src, dst_ref=dst, send_sem=send_sem, recv_sem=recv_sem,
                                        device_id=to, device_id_type=MESH)


def _place(name, pos, w, axis):
    R, Cc = w.shape
    tr = _pick(R, (256, 128, 64, 32, 16))
    nrb = R // tr

    def body(pos_ref, w_ref, o_ref):
        o_ref[...] = w_ref[...].astype(BF16)

    omap = (lambda i, p: (i, p[0])) if axis == 1 else (lambda i, p: (p[0] * nrb + i, 0))
    return pl.pallas_call(
        body, name=name,
        grid_spec=pltpu.PrefetchScalarGridSpec(
            num_scalar_prefetch=1, grid=(nrb,), in_specs=[pl.BlockSpec((tr, Cc), lambda i, p: (i, 0))],
            out_specs=pl.BlockSpec((tr, Cc), omap)),
        out_shape=jax.ShapeDtypeStruct(_full_shape(w, axis), BF16), compiler_params=_params(1),
    )(pos, w)


def _gather_now(name, axes, fulls):
    nt = len(fulls)
    shapes = [f.shape for f in fulls]

    def body(*refs):
        outs, token = refs[nt:2 * nt], refs[2 * nt]
        send_sems, recv_sems = refs[2 * nt + 1:]
        x, y, c = _position()
        j0 = 2 * x + y
        sib = (x, y, 1 - c)

        def copy(t, k, slab, half, to):
            win = _window(outs[t], shapes[t], axes[t], slab=slab, half=half)
            return _remote(win, win, send_sems.at[t, k], recv_sems.at[t, k], to)

        sends = []
        for k, (fx, fy) in enumerate(FLIPS):
            for t in range(nt):
                cp = copy(t, k, j0, c, (x ^ fx, y ^ fy, c))
                cp.start()
                sends.append(cp)
        for k, (fx, fy) in enumerate(FLIPS):
            js = 2 * (x ^ fx) + (y ^ fy)
            for t in range(nt):
                copy(t, k, js, c, sib).wait_recv()
                cp = copy(t, 3 + k, js, c, sib)
                cp.start()
                sends.append(cp)
        for k, (fx, fy) in enumerate(FLIPS):
            js = 2 * (x ^ fx) + (y ^ fy)
            for t in range(nt):
                copy(t, 3 + k, js, 1 - c, sib).wait_recv()
        for cp in sends:
            cp.wait_send()
        token[...] = jnp.zeros_like(token)

    res = pl.pallas_call(
        body, name=name, in_specs=[ANY] * nt,
        out_specs=[ANY] * nt + [pl.BlockSpec(memory_space=pltpu.VMEM)],
        out_shape=[jax.ShapeDtypeStruct(s, BF16) for s in shapes] + [TOKEN],
        input_output_aliases={t: t for t in range(nt)},
        scratch_shapes=[pltpu.SemaphoreType.DMA((nt, 6)), pltpu.SemaphoreType.DMA((nt, 6))],
    )(*fulls)
    return list(res[:nt]), res[nt]


def _split_start(name, arrays, ncopies, plan):
    na = len(arrays)

    def body(*refs):
        ins = refs[:na]
        send_sems, recv_sems = refs[na], refs[na + 1]
        token = refs[-1]
        x, y, c = _position()
        for i, (src, dst, to) in enumerate(plan(ins, x, y, c)):
            _remote(src, dst, send_sems.at[i], recv_sems.at[i], to).start()
        token[...] = jnp.zeros_like(token)

    res = pl.pallas_call(
        body, name=name, in_specs=[HBM] * na,
        out_specs=tuple([SEM, SEM] + [HBM] * na + [pl.BlockSpec(memory_space=pltpu.VMEM)]),
        out_shape=tuple([pltpu.SemaphoreType.DMA((ncopies,)), pltpu.SemaphoreType.DMA((ncopies,))]
                        + [pltpu.HBM(a.shape, a.dtype) for a in arrays] + [TOKEN]),
        input_output_aliases={i: 2 + i for i in range(na)},
        compiler_params=pltpu.CompilerParams(has_side_effects=EFFECT),
    )(*[_hbm(a) for a in arrays])
    return (res[0], res[1]), list(res[2:2 + na]), res[-1]


def _split_wait(name, arrays, sems, after, plan):
    na = len(arrays)

    def body(*refs):
        ins = refs[:na]
        send_sems, recv_sems = refs[na], refs[na + 1]
        x, y, c = _position()
        for i, (src, dst, to) in enumerate(plan(ins, x, y, c)):
            cp = _remote(src, dst, send_sems.at[i], recv_sems.at[i], to)
            cp.wait_send()
            cp.wait_recv()

    res = pl.pallas_call(
        body, name=name, in_specs=[HBM] * na + [SEM, SEM, ANY],
        out_specs=tuple([HBM] * na), out_shape=tuple(pltpu.HBM(a.shape, a.dtype) for a in arrays),
        input_output_aliases={i: i for i in range(na)},
        compiler_params=pltpu.CompilerParams(has_side_effects=EFFECT),
    )(*arrays, *sems, after)
    return list(res)


def _gather_plan(axes, shapes, conv_shape):
    nt = len(axes)

    def plan(refs, x, y, c):
        j0 = 2 * x + y
        out = []
        for fx, fy in FLIPS:
            to = (x ^ fx, y ^ fy, c)
            for t in range(nt):
                win = _window(refs[t], shapes[t], axes[t], slab=j0, half=c)
                out.append((win, win, to))
            if conv_shape is not None:
                win = _window(refs[nt], conv_shape, 1, slab=j0)
                out.append((win, win, to))
        return out

    return plan


def _gather_finish(name, axes, fulls):
    nt = len(axes)
    shapes = [f.shape for f in fulls]

    def body(*refs):
        outs = refs[nt:2 * nt]
        send_sems, recv_sems = refs[2 * nt:]
        x, y, c = _position()
        sib = (x, y, 1 - c)
        cps = []
        for k, (fx, fy) in enumerate(FLIPS):
            js = 2 * (x ^ fx) + (y ^ fy)
            for t in range(nt):
                landed = _window(outs[t], shapes[t], axes[t], slab=js, half=c)
                cp = _remote(landed, landed, send_sems.at[t, k], recv_sems.at[t, k], sib)
                cp.start()
                cps.append(cp)
        for k, (fx, fy) in enumerate(FLIPS):
            js = 2 * (x ^ fx) + (y ^ fy)
            for t in range(nt):
                other = _window(outs[t], shapes[t], axes[t], slab=js, half=1 - c)
                _remote(other, other, send_sems.at[t, k], recv_sems.at[t, k], sib).wait_recv()
        for cp in cps:
            cp.wait_send()

    res = pl.pallas_call(
        body, name=name, in_specs=[ANY] * nt, out_specs=[ANY] * nt,
        out_shape=[jax.ShapeDtypeStruct(f.shape, f.dtype) for f in fulls],
        input_output_aliases={t: t for t in range(nt)},
        scratch_shapes=[pltpu.SemaphoreType.DMA((nt, 3)), pltpu.SemaphoreType.DMA((nt, 3))],
    )(*fulls)
    return list(res)


def _pair_exchange(name, srcs, windows, out_shapes, dtype):
    nt = len(srcs)

    def body(*refs):
        ins, outs = refs[:nt], refs[nt:2 * nt]
        send_sems, recv_sems = refs[2 * nt:]
        x, y, c = _position()
        cps = []
        for t in range(nt):
            cp = _remote(windows[t](ins[t], c), outs[t], send_sems.at[t], recv_sems.at[t], (x, y, 1 - c))
            cp.start()
            cps.append(cp)
        for cp in cps:
            cp.wait()

    return pl.pallas_call(
        body, name=name, in_specs=[ANY] * nt, out_specs=[ANY] * nt,
        out_shape=[jax.ShapeDtypeStruct(s, dtype) for s in out_shapes],
        scratch_shapes=[pltpu.SemaphoreType.DMA((nt,)), pltpu.SemaphoreType.DMA((nt,))],
    )(*srcs)


def _pair_plan(axes, shapes):
    nt = len(axes)

    def plan(refs, x, y, c):
        return [(_window(refs[t], shapes[t], axes[t], half=1 - c), refs[nt + t], (x, y, 1 - c)) for t in range(nt)]

    return plan


def _scatter_plan(axes, shapes):
    nt = len(axes)

    def plan(refs, x, y, c):
        out = []
        for k, (fx, fy) in enumerate(FLIPS):
            js = 2 * (x ^ fx) + (y ^ fy)
            for t in range(nt):
                src = _window(refs[t], _half_shape(shapes[t], axes[t]), axes[t], slab=js)
                out.append((src, refs[nt + t].at[k], (x ^ fx, y ^ fy, c)))
        return out

    return plan


def _gather_small(packed):
    R, Cc = packed.shape

    def body(p_ref, o_ref, send_sems, recv_sems, loc_sem):
        x, y, c = _position()
        me = 4 * x + 2 * y + c
        mine = pltpu.make_async_copy(p_ref, o_ref.at[me], loc_sem)
        mine.start()
        cps = []
        for k in range(1, 8):
            fx, fy, fc = (k >> 2) & 1, (k >> 1) & 1, k & 1
            cp = pltpu.make_async_remote_copy(
                src_ref=p_ref, dst_ref=o_ref.at[me], send_sem=send_sems.at[k - 1], recv_sem=recv_sems.at[k - 1],
                device_id=(x ^ fx, y ^ fy, c ^ fc), device_id_type=MESH)
            cp.start()
            cps.append(cp)
        for cp in cps:
            cp.wait()
        mine.wait()

    return pl.pallas_call(
        body, name="gather_small_grads", in_specs=[ANY], out_specs=ANY,
        out_shape=jax.ShapeDtypeStruct((8, R, Cc), F32),
        scratch_shapes=[pltpu.SemaphoreType.DMA((7,)), pltpu.SemaphoreType.DMA((7,)), pltpu.SemaphoreType.DMA],
    )(packed)


def _sum_slots(name, slots):
    n, R, Cc = slots.shape

    def body(s_ref, o_ref):
        t = s_ref[0]
        for i in range(1, n):
            t = t + s_ref[i]
        o_ref[...] = t

    return pl.pallas_call(
        body, name=name, grid=(1,), in_specs=[pl.BlockSpec((n, R, Cc), lambda i: (0, 0, 0))],
        out_specs=pl.BlockSpec((R, Cc), lambda i: (0, 0)), out_shape=jax.ShapeDtypeStruct((R, Cc), F32),
        compiler_params=_params(1),
    )(slots)


def _pair_sum(name, pos, g, land, shape, axis):
    hshape = _half_shape(shape, axis)
    R, Cc = hshape
    tr = _pick(R, (256, 128, 64, 32, 16))
    nrb = R // tr

    def body(pos_ref, g_ref, l_ref, o_ref):
        o_ref[...] = (g_ref[...].astype(F32) + l_ref[...].astype(F32)).astype(BF16)

    if axis == 1:
        gmap = lambda i, p: (p[1] * nrb + i, 0)
    else:
        gmap = lambda i, p: (i, p[1])
    blk = pl.BlockSpec((tr, Cc), lambda i, p: (i, 0))
    return pl.pallas_call(
        body, name=name,
        grid_spec=pltpu.PrefetchScalarGridSpec(
            num_scalar_prefetch=1, grid=(nrb,), in_specs=[pl.BlockSpec((tr, Cc), gmap), blk], out_specs=blk),
        out_shape=jax.ShapeDtypeStruct(hshape, BF16), compiler_params=_params(1),
    )(pos, g, land)


def _chip_sum(name, pos, sb, land, shape, axis):
    hshape = _half_shape(shape, axis)
    pshape = _piece_shape(shape, axis)
    R, Cc = pshape
    tr = _pick(R, (256, 128, 64, 32, 16))
    nrb = R // tr

    def body(pos_ref, s_ref, l_ref, o_ref):
        t = s_ref[...].astype(F32)
        for k in range(3):
            t = t + l_ref[k].astype(F32)
        o_ref[...] = t

    if axis == 1:
        smap = lambda i, p: (i, p[0])
    else:
        smap = lambda i, p: (p[0] * nrb + i, 0)
    return pl.pallas_call(
        body, name=name,
        grid_spec=pltpu.PrefetchScalarGridSpec(
            num_scalar_prefetch=1, grid=(nrb,),
            in_specs=[pl.BlockSpec((tr, Cc), smap), pl.BlockSpec((3, tr, Cc), lambda i, p: (0, i, 0))],
            out_specs=pl.BlockSpec((tr, Cc), lambda i, p: (i, 0))),
        out_shape=jax.ShapeDtypeStruct(pshape, F32), compiler_params=_params(1),
    )(pos, sb, land)


def _adam_math(w, g, m, v):
    m = ADAM_B1 * m + (1.0 - ADAM_B1) * g
    v = ADAM_B2 * v + (1.0 - ADAM_B2) * (g * g)
    m_hat = m / (1.0 - ADAM_B1 ** ADAM_STEP)
    v_hat = v / (1.0 - ADAM_B2 ** ADAM_STEP)
    delta = -ADAM_LR * (m_hat / (jnp.sqrt(v_hat) + ADAM_EPS) + ADAM_WD * w)
    return delta, m, v


def _adamw_halves(name, pos, w, m, v, mine, theirs, axis):
    R, Cc = w.shape
    hr, hc = mine.shape
    tr = _pick(hr, (256, 128, 64, 32, 16))
    nrb = hr // tr

    def body(pos_ref, w_ref, m_ref, v_ref, a_ref, b_ref, g_ref, d_ref, nm_ref, nv_ref):
        half = pl.program_id(0)
        g = jnp.where(half == pos_ref[1], a_ref[...], b_ref[...])
        d, nm, nv = _adam_math(w_ref[...], g, m_ref[...], v_ref[...])
        g_ref[...] = g
        d_ref[...] = d
        nm_ref[...] = nm
        nv_ref[...] = nv

    if axis == 1:
        wmap = lambda h, i, p: (h * nrb + i, 0)
    else:
        wmap = lambda h, i, p: (i, h)
    wblk = pl.BlockSpec((tr, hc), wmap)
    ablk = pl.BlockSpec((tr, hc), lambda h, i, p: (jnp.where(h == p[1], i, 0), 0))
    bblk = pl.BlockSpec((tr, hc), lambda h, i, p: (jnp.where(h == p[1], 0, i), 0))
    return pl.pallas_call(
        body, name=name,
        grid_spec=pltpu.PrefetchScalarGridSpec(
            num_scalar_prefetch=1, grid=(2, nrb), in_specs=[wblk, wblk, wblk, ablk, bblk], out_specs=[wblk] * 4),
        out_shape=[jax.ShapeDtypeStruct((R, Cc), F32)] * 4, compiler_params=_params(2),
    )(pos, w, m, v, mine, theirs)


def _adamw_small(name, w, g, m, v):
    def body(w_ref, g_ref, m_ref, v_ref, d_ref, nm_ref, nv_ref):
        d, nm, nv = _adam_math(w_ref[...], g_ref[...], m_ref[...], v_ref[...])
        d_ref[...] = d
        nm_ref[...] = nm
        nv_ref[...] = nv

    blk = pl.BlockSpec(w.shape, lambda i: (0, 0))
    return pl.pallas_call(
        body, name=name, grid=(1,), in_specs=[blk] * 4, out_specs=[blk] * 3,
        out_shape=[jax.ShapeDtypeStruct(w.shape, F32)] * 3, compiler_params=_params(1),
    )(w, g, m, v)


SMALL = ("ffn1_norm_g", "mix_norm_g", "conv_b_dw", "conv_ln_g", "conv_ln_b", "q_norm_g", "k_norm_g", "ffn2_norm_g")
ORDER = ("ffn1_norm_g", "ffn1_w_gate", "ffn1_w_up", "ffn1_w_down", "mix_norm_g", "w_in", "conv_w_dw", "conv_b_dw",
         "conv_ln_g", "conv_ln_b", "q_norm_g", "k_norm_g", "w_out", "ffn2_norm_g", "ffn2_w_gate", "ffn2_w_up",
         "ffn2_w_down")
GATHER_FIRST = ("ffn1_w_gate", "ffn1_w_up")
GATHER_SECOND = ("ffn1_w_down", "w_in")
GATHER_THIRD = ("w_out", "ffn2_w_gate", "ffn2_w_up", "ffn2_w_down")


class _Exchange:
    def __init__(self, P, Mo, Vo, conv_shard, pos):
        self.P, self.Mo, self.Vo, self.pos = P, Mo, Vo, pos
        self.tokens = []
        self.pending = {}
        self.reducing = {}
        self.results = {}
        placed = {n: _place("place_" + n, pos, P[n][0], a) for n, a in BIG}
        self.shapes = {n: placed[n].shape for n, _ in BIG}
        first, tok = _gather_now("gather_first", [AXIS[n] for n in GATHER_FIRST], [placed[n] for n in GATHER_FIRST])
        self.ready = dict(zip(GATHER_FIRST, first))
        cq = conv_shard.shape[1]
        conv_full = lax.dynamic_update_slice(jnp.zeros((conv_shard.shape[0], 4 * cq), F32), conv_shard,
                                             (0, pos[0] * cq))
        for gname, names, conv in (("second", GATHER_SECOND, conv_full), ("third", GATHER_THIRD, None)):
            axes = [AXIS[n] for n in names]
            shapes = [self.shapes[n] for n in names]
            arrays = [placed[n] for n in names] + ([conv] if conv is not None else [])
            small = min(range(len(arrays)), key=lambda i: arrays[i].size)
            arrays[small] = arrays[small] + tok[0, 0].astype(arrays[small].dtype)
            plan = _gather_plan(axes, shapes, conv.shape if conv is not None else None)
            sems, thru, tok = _split_start("gather_%s_start" % gname, arrays, 3 * len(arrays), plan)
            self.tokens.append(tok)
            for n in names + (("conv_w32",) if conv is not None else ()):
                self.pending[n] = (gname, names, axes, plan, sems, thru, conv is not None)

    def tie(self, v):
        for tok in self.tokens:
            v = v + tok[0:1, 0:1]
        self.tokens = []
        return v

    def weights(self, names, after):
        if names[0] in self.pending:
            gname, gnames, axes, plan, sems, thru, has_conv = self.pending[names[0]]
            thru = _split_wait("gather_%s_wait" % gname, thru, sems, after, plan)
            nt = len(gnames)
            fulls = _gather_finish("gather_%s_finish" % gname, axes, thru[:nt])
            for n, f in zip(gnames, fulls):
                self.ready[n] = f
                del self.pending[n]
            if has_conv:
                self.ready["conv_w32"] = thru[nt]
                del self.pending["conv_w32"]
        return [self.ready[n] for n in names]

    def reduce_begin(self, gname, grads, split=False):
        names = list(grads)
        axes = [AXIS[n] for n in names]
        shapes = [self.shapes[n] for n in names]
        gs = [grads[n] for n in names]
        nt = len(names)
        if not split:
            to_sibling = [(lambda ref, c, s=s, a=a: _window(ref, s, a, half=1 - c)) for s, a in zip(shapes, axes)]
            landed = _pair_exchange("pair_exchange_" + gname, gs, to_sibling,
                                    [_half_shape(s, a) for s, a in zip(shapes, axes)], BF16)
            self._scatter(gname, names, axes, shapes, gs, landed)
            return
        plan = _pair_plan(axes, shapes)
        arrays = gs + [lax.empty(_half_shape(s, a), BF16) for s, a in zip(shapes, axes)]
        sems, thru, tok = _split_start("pair_%s_start" % gname, arrays, nt, plan)
        self.tokens.append(tok)
        self.reducing[gname] = (names, axes, shapes, plan, sems, thru)

    def reduce_scatter(self, gname, after):
        names, axes, shapes, plan, sems, thru = self.reducing.pop(gname)
        nt = len(names)
        thru = _split_wait("pair_%s_wait" % gname, thru, sems, after, plan)
        self._scatter(gname, names, axes, shapes, thru[:nt], thru[nt:])

    def _scatter(self, gname, names, axes, shapes, gs, landed):
        sbs = [_pair_sum("pair_sum_" + n, self.pos, g, l, s, a)
               for n, a, g, l, s in zip(names, axes, gs, landed, shapes)]
        lands = [lax.empty((3,) + _piece_shape(s, a), BF16) for s, a in zip(shapes, axes)]
        plan = _scatter_plan(axes, shapes)
        sems, thru, tok = _split_start("scatter_%s_start" % gname, sbs + lands, 3 * len(names), plan)
        self.tokens.append(tok)
        self.reducing[gname] = (names, axes, shapes, plan, sems, thru)

    def reduce_end(self, gname, after):
        names, axes, shapes, plan, sems, thru = self.reducing.pop(gname)
        nt = len(names)
        thru = _split_wait("scatter_%s_wait" % gname, thru, sems, after, plan)
        mine = [_chip_sum("chip_sum_" + n, self.pos, sb, l, s, a)
                for n, a, sb, l, s in zip(names, axes, thru[:nt], thru[nt:], shapes)]
        theirs = _pair_exchange("half_exchange_" + gname, mine, [(lambda ref, c: ref)] * nt,
                                [m.shape for m in mine], F32)
        for n, a, mi, th in zip(names, axes, mine, theirs):
            g, d, nm, nv = _adamw_halves("adamw_" + n, self.pos, self.P[n][0], self.Mo[n][0], self.Vo[n][0],
                                         mi, th, a)
            self.results[n] = (g[None], d[None], nm[None], nv[None])


def kernel(x, ffn1_norm_g, ffn1_w_gate, ffn1_w_up, ffn1_w_down, mix_norm_g, w_in, conv_w_dw, conv_b_dw, conv_ln_g, conv_ln_b, q_norm_g, k_norm_g, w_out, ffn2_norm_g, ffn2_w_gate, ffn2_w_up, ffn2_w_down, loss_target, m_ffn1_norm_g, m_ffn1_w_gate, m_ffn1_w_up, m_ffn1_w_down, m_mix_norm_g, m_w_in, m_conv_w_dw, m_conv_b_dw, m_conv_ln_g, m_conv_ln_b, m_q_norm_g, m_k_norm_g, m_w_out, m_ffn2_norm_g, m_ffn2_w_gate, m_ffn2_w_up, m_ffn2_w_down, v_ffn1_norm_g, v_ffn1_w_gate, v_ffn1_w_up, v_ffn1_w_down, v_mix_norm_g, v_w_in, v_conv_w_dw, v_conv_b_dw, v_conv_ln_g, v_conv_ln_b, v_q_norm_g, v_k_norm_g, v_w_out, v_ffn2_norm_g, v_ffn2_w_gate, v_ffn2_w_up, v_ffn2_w_down):
    args = dict(locals())
    P = {n: args[n] for n in ORDER}
    Mo = {n: args["m_" + n] for n in ORDER}
    Vo = {n: args["v_" + n] for n in ORDER}
    xs = x[0]
    tgt = loss_target[0]
    T, D = xs.shape
    hd = q_norm_g.shape[-1]
    C = conv_b_dw.shape[-1]
    ntap = conv_w_dw.shape[1]
    cx, cy, cc = _position()
    j0 = 2 * cx + cy
    pos = jnp.stack([j0, cc]).astype(jnp.int32)

    conv_shard = jnp.pad(conv_w_dw[0], ((0, HALO - ntap), (0, 0)))
    comm = _Exchange(P, Mo, Vo, conv_shard, pos)
    lossvec, dx0, G = _local_step(xs, tgt, {n: P[n] for n in SMALL}, comm, hd)
    loss = lax.psum(0.5 / D * jnp.sum(lossvec), AXES)
    grads, deltas, new_m, new_v = {}, {}, {}, {}
    for n, _ in BIG:
        grads[n], deltas[n], new_m[n], new_v[n] = comm.results[n]

    rows = [G["conv_w32"]]
    for n in ("ffn1_norm_g", "mix_norm_g", "ffn2_norm_g"):
        rows.append(G[n].reshape(D // C, C))
    for n in ("conv_b_dw", "conv_ln_g", "conv_ln_b", "q_norm_g", "k_norm_g"):
        rows.append(G[n])
    packed = jnp.concatenate(rows, axis=0)
    packed = jnp.pad(packed, ((0, -packed.shape[0] % 8), (0, 0)))
    total = _sum_slots("sum_small_grads", _gather_small(packed))
    r = HALO
    small_g = {}
    cq = C // 4
    small_g["conv_w_dw"] = lax.dynamic_slice(total[:ntap], (0, j0 * cq), (ntap, cq))
    for n in ("ffn1_norm_g", "mix_norm_g", "ffn2_norm_g"):
        small_g[n] = total[r:r + D // C].reshape(1, D)
        r += D // C
    for n in ("conv_b_dw", "conv_ln_g", "conv_ln_b"):
        small_g[n] = total[r:r + 1]
        r += 1
    for n in ("q_norm_g", "k_norm_g"):
        small_g[n] = total[r:r + 1, :hd]
        r += 1
    for n in ("conv_w_dw",) + SMALL:
        lead = n == "conv_w_dw"
        w2, m2, v2 = (P[n][0], Mo[n][0], Vo[n][0]) if lead else (P[n], Mo[n], Vo[n])
        d, nm, nv = _adamw_small("adamw_" + n, w2, small_g[n], m2, v2)
        if lead:
            grads[n], deltas[n], new_m[n], new_v[n] = small_g[n][None], d[None], nm[None], nv[None]
        else:
            grads[n], deltas[n], new_m[n], new_v[n] = small_g[n], d, nm, nv

    return (loss, dx0[None], *[grads[n] for n in ORDER], *[deltas[n] for n in ORDER],
            *[new_m[n] for n in ORDER], *[new_v[n] for n in ORDER])
```
